```python
import math
import jax, jax.numpy as jnp
from jax import lax
import numpy as np

D_MODEL = 1024
BATCH = 8
SEQ = 4096
DEPTH = 4

MLA_HEADS = 8
QK_NOPE_DIM = 64
QK_ROPE_DIM = 32
QK_HEAD_DIM = QK_NOPE_DIM + QK_ROPE_DIM
V_HEAD_DIM = 64
Q_LORA_RANK = 256
KV_LORA_RANK = 256
CONV_CHANNELS = D_MODEL // 2
SHORT_CONV_WIDTH = 3
MIX_IN_DIM = Q_LORA_RANK + KV_LORA_RANK + QK_ROPE_DIM + 3 * CONV_CHANNELS
MIX_OUT_DIM = MLA_HEADS * V_HEAD_DIM + CONV_CHANNELS
ROPE_THETA = 10000.0
Q_BLOCK = 128
SSM_WIDTH = D_MODEL
SSM_GROUP = 16
SSM_GROUPS = SSM_WIDTH // SSM_GROUP
SSM_STATE = 64
DT_MIN = 1e-3
DT_MAX = 1e-1
FFN_HIDDEN = 2816
FFN_CONV_WIDTH = 3
N_EVEN = (DEPTH + 1) // 2
N_ODD = DEPTH // 2
EPS = 1e-6

kernel_name = "hybrid_mla_shortconv_s5_convffn"


def rms_norm(x, g):
    x32 = x.astype(jnp.float32)
    y = x32 * lax.rsqrt(jnp.mean(x32 * x32, axis=-1, keepdims=True) + EPS)
    return (y * g.astype(jnp.float32)).astype(x.dtype)


def causal_depthwise_conv(x, w):
    k_width, channels = w.shape
    return lax.conv_general_dilated(
        x, w[:, None, :].astype(x.dtype), window_strides=(1,), padding=[(k_width - 1, 0)],
        dimension_numbers=("NWC", "WIO", "NWC"), feature_group_count=channels)


def rope_tables(seq):
    inv_freq = 1.0 / (ROPE_THETA ** (jnp.arange(0, QK_ROPE_DIM, 2, dtype=jnp.float32) / QK_ROPE_DIM))
    ang = jnp.arange(seq, dtype=jnp.float32)[:, None] * inv_freq[None, :]
    return jnp.cos(ang)[:, None, :], jnp.sin(ang)[:, None, :]


def apply_rope(x, cos, sin):
    x1, x2 = jnp.split(x.astype(jnp.float32), 2, axis=-1)
    return jnp.concatenate([x1 * cos - x2 * sin, x2 * cos + x1 * sin], axis=-1).astype(x.dtype)


def causal_block_attention(q, k, v):
    seq = q.shape[1]
    scale = q.shape[-1] ** -0.5
    outs = []
    for i in range(seq // Q_BLOCK):
        lo, hi = i * Q_BLOCK, (i + 1) * Q_BLOCK
        s = jnp.einsum("bqhd,bkhd->bhqk", q[:, lo:hi], k[:, :hi]).astype(jnp.float32) * scale
        causal = jnp.arange(hi)[None, :] <= jnp.arange(lo, hi)[:, None]
        s = jnp.where(causal, s, -jnp.inf)
        p = jax.nn.softmax(s, axis=-1).astype(v.dtype)
        outs.append(jnp.einsum("bhqk,bkhd->bqhd", p, v[:, :hi]))
    return jnp.concatenate(outs, axis=1)


def mla_shortconv_mixer(h, w_in, cq_norm, ckv_norm, w_uq, w_ukv, q_gain, k_gain, sconv_w, w_out, cos, sin):
    bsz, seq, _ = h.shape
    proj = h @ w_in
    splits = np.cumsum([Q_LORA_RANK, KV_LORA_RANK, QK_ROPE_DIM, CONV_CHANNELS, CONV_CHANNELS]).tolist()
    c_q, c_kv, k_rope, gate_b, gate_c, conv_in = jnp.split(proj, splits, axis=-1)
    q = (rms_norm(c_q, cq_norm) @ w_uq).reshape(bsz, seq, MLA_HEADS, QK_HEAD_DIM)
    kv = (rms_norm(c_kv, ckv_norm) @ w_ukv).reshape(bsz, seq, MLA_HEADS, QK_NOPE_DIM + V_HEAD_DIM)
    k_nope, v = kv[..., :QK_NOPE_DIM], kv[..., QK_NOPE_DIM:]
    k = jnp.concatenate(
        [k_nope, jnp.broadcast_to(k_rope[:, :, None, :], (bsz, seq, MLA_HEADS, QK_ROPE_DIM))], axis=-1)
    q = rms_norm(q, q_gain)
    k = rms_norm(k, k_gain)
    q = jnp.concatenate([q[..., :QK_NOPE_DIM], apply_rope(q[..., QK_NOPE_DIM:], cos, sin)], axis=-1)
    k = jnp.concatenate([k[..., :QK_NOPE_DIM], apply_rope(k[..., QK_NOPE_DIM:], cos, sin)], axis=-1)
    attn = causal_block_attention(q, k, v).reshape(bsz, seq, MLA_HEADS * V_HEAD_DIM)
    conv = gate_b * causal_depthwise_conv(gate_c * conv_in, sconv_w)
    return jnp.concatenate([attn, conv], axis=-1) @ w_out


def _ssm_combine(earlier, later):
    ar1, ai1, br1, bi1 = earlier
    ar2, ai2, br2, bi2 = later
    return (ar2 * ar1 - ai2 * ai1,
            ar2 * ai1 + ai2 * ar1,
            ar2 * br1 - ai2 * bi1 + br2,
            ar2 * bi1 + ai2 * br1 + bi2)


def s5_mixer(h, w_in, lambda_re, lambda_im, log_step, b_re, b_im, c_re, c_im, d_skip, w_glu):
    bsz, seq, _ = h.shape
    f32 = jnp.float32
    u = (h @ w_in).astype(f32)
    ug = u.reshape(bsz, seq, SSM_GROUPS, SSM_GROUP)
    lr, li = lambda_re.astype(f32), lambda_im.astype(f32)
    dt = jnp.exp(log_step.astype(f32))[:, None]
    mag = jnp.exp(lr * dt)
    ar, ai = mag * jnp.cos(li * dt), mag * jnp.sin(li * dt)
    nr, ni = ar - 1.0, ai
    den = lr * lr + li * li
    zr, zi = (nr * lr + ni * li) / den, (ni * lr - nr * li) / den
    br_, bi_ = b_re.astype(f32), b_im.astype(f32)
    bbar_r = zr[..., None] * br_ - zi[..., None] * bi_
    bbar_i = zr[..., None] * bi_ + zi[..., None] * br_
    bu_r = jnp.einsum("gpc,bsgc->bsgp", bbar_r, ug)
    bu_i = jnp.einsum("gpc,bsgc->bsgp", bbar_i, ug)
    a_r = jnp.broadcast_to(ar, (1, seq, SSM_GROUPS, SSM_STATE))
    a_i = jnp.broadcast_to(ai, (1, seq, SSM_GROUPS, SSM_STATE))
    _, _, st_r, st_i = lax.associative_scan(_ssm_combine, (a_r, a_i, bu_r, bu_i), axis=1)
    y = (jnp.einsum("gcp,bsgp->bsgc", c_re.astype(f32), st_r)
         - jnp.einsum("gcp,bsgp->bsgc", c_im.astype(f32), st_i)).reshape(bsz, seq, SSM_WIDTH)
    y = y + d_skip.astype(f32) * u
    g = jax.nn.gelu(y).astype(h.dtype)
    a, b = jnp.split(g @ w_glu, 2, axis=-1)
    return a * jax.nn.sigmoid(b)


def conv_ffn(h, w_up, conv_w, w_down):
    up = causal_depthwise_conv(h @ w_up, conv_w)
    gate, val = jnp.split(up, 2, axis=-1)
    return (jax.nn.silu(gate) * val) @ w_down


def _fwd_setup_inputs(seed: int = 0) -> dict:
    key = jax.random.key(seed)
    ks = jax.random.split(key, 32)
    f32 = jnp.float32

    def nrm(k, shape, scale):
        return jax.random.normal(k, shape, f32) * scale

    def gain(k, shape):
        return 1.0 + 0.02 * jax.random.normal(k, shape, f32)

    lam_im_base = jnp.pi * jnp.arange(SSM_STATE, dtype=f32)
    return {
        "x": nrm(ks[0], (BATCH, SEQ, D_MODEL), 1.0),
        "attn_norm": gain(ks[1], (N_EVEN, D_MODEL)),
        "mix_w_in": nrm(ks[2], (N_EVEN, D_MODEL, MIX_IN_DIM), D_MODEL ** -0.5),
        "cq_norm": gain(ks[3], (N_EVEN, Q_LORA_RANK)),
        "ckv_norm": gain(ks[4], (N_EVEN, KV_LORA_RANK)),
        "w_uq": nrm(ks[5], (N_EVEN, Q_LORA_RANK, MLA_HEADS * QK_HEAD_DIM), Q_LORA_RANK ** -0.5),
        "w_ukv": nrm(ks[6], (N_EVEN, KV_LORA_RANK, MLA_HEADS * (QK_NOPE_DIM + V_HEAD_DIM)), KV_LORA_RANK ** -0.5),
        "q_gain": gain(ks[7], (N_EVEN, QK_HEAD_DIM)),
        "k_gain": gain(ks[8], (N_EVEN, QK_HEAD_DIM)),
        "sconv_w": nrm(ks[9], (N_EVEN, SHORT_CONV_WIDTH, CONV_CHANNELS), SHORT_CONV_WIDTH ** -0.5),
        "mix_w_out": nrm(ks[10], (N_EVEN, MIX_OUT_DIM, D_MODEL), MIX_OUT_DIM ** -0.5),
        "ssm_norm": gain(ks[11], (N_ODD, D_MODEL)),
        "ssm_w_in": nrm(ks[12], (N_ODD, D_MODEL, SSM_WIDTH), D_MODEL ** -0.5),
        "lambda_re": -0.5 + 0.01 * jax.random.normal(ks[13], (N_ODD, SSM_GROUPS, SSM_STATE), f32),
        "lambda_im": lam_im_base + 0.01 * jax.random.normal(ks[14], (N_ODD, SSM_GROUPS, SSM_STATE), f32),
        "log_step": jax.random.uniform(ks[15], (N_ODD, SSM_GROUPS), f32,
                                       minval=math.log(DT_MIN), maxval=math.log(DT_MAX)),
        "b_re": nrm(ks[16], (N_ODD, SSM_GROUPS, SSM_STATE, SSM_GROUP), (2 * SSM_GROUP) ** -0.5),
        "b_im": nrm(ks[17], (N_ODD, SSM_GROUPS, SSM_STATE, SSM_GROUP), (2 * SSM_GROUP) ** -0.5),
        "c_re": nrm(ks[18], (N_ODD, SSM_GROUPS, SSM_GROUP, SSM_STATE), (2 * SSM_STATE) ** -0.5),
        "c_im": nrm(ks[19], (N_ODD, SSM_GROUPS, SSM_GROUP, SSM_STATE), (2 * SSM_STATE) ** -0.5),
        "d_skip": nrm(ks[20], (N_ODD, SSM_WIDTH), 1.0),
        "w_glu": nrm(ks[21], (N_ODD, SSM_WIDTH, 2 * D_MODEL), SSM_WIDTH ** -0.5),
        "ffn_norm": gain(ks[22], (DEPTH, D_MODEL)),
        "ffn_w_up": nrm(ks[23], (DEPTH, D_MODEL, 2 * FFN_HIDDEN), D_MODEL ** -0.5),
        "ffn_conv_w": nrm(ks[24], (DEPTH, FFN_CONV_WIDTH, 2 * FFN_HIDDEN), FFN_CONV_WIDTH ** -0.5),
        "ffn_w_down": nrm(ks[25], (DEPTH, FFN_HIDDEN, D_MODEL), FFN_HIDDEN ** -0.5),
    }


def _fwd_reference(x, attn_norm, mix_w_in, cq_norm, ckv_norm, w_uq, w_ukv, q_gain, k_gain, sconv_w, mix_w_out,
              ssm_norm, ssm_w_in, lambda_re, lambda_im, log_step, b_re, b_im, c_re, c_im, d_skip, w_glu,
              ffn_norm, ffn_w_up, ffn_conv_w, ffn_w_down):
    cos, sin = rope_tables(x.shape[1])
    for layer in range(DEPTH):
        i = layer // 2
        if layer % 2 == 0:
            x = x + mla_shortconv_mixer(rms_norm(x, attn_norm[i]), mix_w_in[i], cq_norm[i], ckv_norm[i],
                                        w_uq[i], w_ukv[i], q_gain[i], k_gain[i], sconv_w[i], mix_w_out[i],
                                        cos, sin)
        else:
            x = x + s5_mixer(rms_norm(x, ssm_norm[i]), ssm_w_in[i], lambda_re[i], lambda_im[i], log_step[i],
                             b_re[i], b_im[i], c_re[i], c_im[i], d_skip[i], w_glu[i]).astype(x.dtype)
        x = x + conv_ffn(rms_norm(x, ffn_norm[layer]), ffn_w_up[layer], ffn_conv_w[layer], ffn_w_down[layer])
    return x


import jax as _jax
import jax.numpy as _jnp

TWIN_FORMAT = 'train_step'
FWD_PARAMS = ['x', 'attn_norm', 'mix_w_in', 'cq_norm', 'ckv_norm', 'w_uq', 'w_ukv', 'q_gain', 'k_gain', 'sconv_w', 'mix_w_out', 'ssm_norm', 'ssm_w_in', 'lambda_re', 'lambda_im', 'log_step', 'b_re', 'b_im', 'c_re', 'c_im', 'd_skip', 'w_glu', 'ffn_norm', 'ffn_w_up', 'ffn_conv_w', 'ffn_w_down']
TWIN_WEIGHTS = ['attn_norm', 'mix_w_in', 'cq_norm', 'ckv_norm', 'w_uq', 'w_ukv', 'q_gain', 'k_gain', 'sconv_w', 'mix_w_out', 'ssm_norm', 'ssm_w_in', 'lambda_re', 'lambda_im', 'log_step', 'b_re', 'b_im', 'c_re', 'c_im', 'd_skip', 'w_glu', 'ffn_norm', 'ffn_w_up', 'ffn_conv_w', 'ffn_w_down']
TWIN_DIFF_INPUT = 'x'
TWIN_INPUTS = ['x', 'attn_norm', 'mix_w_in', 'cq_norm', 'ckv_norm', 'w_uq', 'w_ukv', 'q_gain', 'k_gain', 'sconv_w', 'mix_w_out', 'ssm_norm', 'ssm_w_in', 'lambda_re', 'lambda_im', 'log_step', 'b_re', 'b_im', 'c_re', 'c_im', 'd_skip', 'w_glu', 'ffn_norm', 'ffn_w_up', 'ffn_conv_w', 'ffn_w_down', 'loss_target', 'm_attn_norm', 'm_mix_w_in', 'm_cq_norm', 'm_ckv_norm', 'm_w_uq', 'm_w_ukv', 'm_q_gain', 'm_k_gain', 'm_sconv_w', 'm_mix_w_out', 'm_ssm_norm', 'm_ssm_w_in', 'm_lambda_re', 'm_lambda_im', 'm_log_step', 'm_b_re', 'm_b_im', 'm_c_re', 'm_c_im', 'm_d_skip', 'm_w_glu', 'm_ffn_norm', 'm_ffn_w_up', 'm_ffn_conv_w', 'm_ffn_w_down', 'v_attn_norm', 'v_mix_w_in', 'v_cq_norm', 'v_ckv_norm', 'v_w_uq', 'v_w_ukv', 'v_q_gain', 'v_k_gain', 'v_sconv_w', 'v_mix_w_out', 'v_ssm_norm', 'v_ssm_w_in', 'v_lambda_re', 'v_lambda_im', 'v_log_step', 'v_b_re', 'v_b_im', 'v_c_re', 'v_c_im', 'v_d_skip', 'v_w_glu', 'v_ffn_norm', 'v_ffn_w_up', 'v_ffn_conv_w', 'v_ffn_w_down']
TWIN_OUTPUTS = ['loss', 'grad_x', 'grad_attn_norm', 'grad_mix_w_in', 'grad_cq_norm', 'grad_ckv_norm', 'grad_w_uq', 'grad_w_ukv', 'grad_q_gain', 'grad_k_gain', 'grad_sconv_w', 'grad_mix_w_out', 'grad_ssm_norm', 'grad_ssm_w_in', 'grad_lambda_re', 'grad_lambda_im', 'grad_log_step', 'grad_b_re', 'grad_b_im', 'grad_c_re', 'grad_c_im', 'grad_d_skip', 'grad_w_glu', 'grad_ffn_norm', 'grad_ffn_w_up', 'grad_ffn_conv_w', 'grad_ffn_w_down', 'delta_attn_norm', 'delta_mix_w_in', 'delta_cq_norm', 'delta_ckv_norm', 'delta_w_uq', 'delta_w_ukv', 'delta_q_gain', 'delta_k_gain', 'delta_sconv_w', 'delta_mix_w_out', 'delta_ssm_norm', 'delta_ssm_w_in', 'delta_lambda_re', 'delta_lambda_im', 'delta_log_step', 'delta_b_re', 'delta_b_im', 'delta_c_re', 'delta_c_im', 'delta_d_skip', 'delta_w_glu', 'delta_ffn_norm', 'delta_ffn_w_up', 'delta_ffn_conv_w', 'delta_ffn_w_down', 'new_m_attn_norm', 'new_m_mix_w_in', 'new_m_cq_norm', 'new_m_ckv_norm', 'new_m_w_uq', 'new_m_w_ukv', 'new_m_q_gain', 'new_m_k_gain', 'new_m_sconv_w', 'new_m_mix_w_out', 'new_m_ssm_norm', 'new_m_ssm_w_in', 'new_m_lambda_re', 'new_m_lambda_im', 'new_m_log_step', 'new_m_b_re', 'new_m_b_im', 'new_m_c_re', 'new_m_c_im', 'new_m_d_skip', 'new_m_w_glu', 'new_m_ffn_norm', 'new_m_ffn_w_up', 'new_m_ffn_conv_w', 'new_m_ffn_w_down', 'new_v_attn_norm', 'new_v_mix_w_in', 'new_v_cq_norm', 'new_v_ckv_norm', 'new_v_w_uq', 'new_v_w_ukv', 'new_v_q_gain', 'new_v_k_gain', 'new_v_sconv_w', 'new_v_mix_w_out', 'new_v_ssm_norm', 'new_v_ssm_w_in', 'new_v_lambda_re', 'new_v_lambda_im', 'new_v_log_step', 'new_v_b_re', 'new_v_b_im', 'new_v_c_re', 'new_v_c_im', 'new_v_d_skip', 'new_v_w_glu', 'new_v_ffn_norm', 'new_v_ffn_w_up', 'new_v_ffn_conv_w', 'new_v_ffn_w_down']
TWIN_LEAF_KINDS = {'loss': 'loss', 'grad_x': 'grad_x', 'grad_attn_norm': 'grad_w', 'grad_mix_w_in': 'grad_w', 'grad_cq_norm': 'grad_w', 'grad_ckv_norm': 'grad_w', 'grad_w_uq': 'grad_w', 'grad_w_ukv': 'grad_w', 'grad_q_gain': 'grad_w', 'grad_k_gain': 'grad_w', 'grad_sconv_w': 'grad_w', 'grad_mix_w_out': 'grad_w', 'grad_ssm_norm': 'grad_w', 'grad_ssm_w_in': 'grad_w', 'grad_lambda_re': 'grad_w', 'grad_lambda_im': 'grad_w', 'grad_log_step': 'grad_w', 'grad_b_re': 'grad_w', 'grad_b_im': 'grad_w', 'grad_c_re': 'grad_w', 'grad_c_im': 'grad_w', 'grad_d_skip': 'grad_w', 'grad_w_glu': 'grad_w', 'grad_ffn_norm': 'grad_w', 'grad_ffn_w_up': 'grad_w', 'grad_ffn_conv_w': 'grad_w', 'grad_ffn_w_down': 'grad_w', 'delta_attn_norm': 'delta_w', 'delta_mix_w_in': 'delta_w', 'delta_cq_norm': 'delta_w', 'delta_ckv_norm': 'delta_w', 'delta_w_uq': 'delta_w', 'delta_w_ukv': 'delta_w', 'delta_q_gain': 'delta_w', 'delta_k_gain': 'delta_w', 'delta_sconv_w': 'delta_w', 'delta_mix_w_out': 'delta_w', 'delta_ssm_norm': 'delta_w', 'delta_ssm_w_in': 'delta_w', 'delta_lambda_re': 'delta_w', 'delta_lambda_im': 'delta_w', 'delta_log_step': 'delta_w', 'delta_b_re': 'delta_w', 'delta_b_im': 'delta_w', 'delta_c_re': 'delta_w', 'delta_c_im': 'delta_w', 'delta_d_skip': 'delta_w', 'delta_w_glu': 'delta_w', 'delta_ffn_norm': 'delta_w', 'delta_ffn_w_up': 'delta_w', 'delta_ffn_conv_w': 'delta_w', 'delta_ffn_w_down': 'delta_w', 'new_m_attn_norm': 'new_m', 'new_m_mix_w_in': 'new_m', 'new_m_cq_norm': 'new_m', 'new_m_ckv_norm': 'new_m', 'new_m_w_uq': 'new_m', 'new_m_w_ukv': 'new_m', 'new_m_q_gain': 'new_m', 'new_m_k_gain': 'new_m', 'new_m_sconv_w': 'new_m', 'new_m_mix_w_out': 'new_m', 'new_m_ssm_norm': 'new_m', 'new_m_ssm_w_in': 'new_m', 'new_m_lambda_re': 'new_m', 'new_m_lambda_im': 'new_m', 'new_m_log_step': 'new_m', 'new_m_b_re': 'new_m', 'new_m_b_im': 'new_m', 'new_m_c_re': 'new_m', 'new_m_c_im': 'new_m', 'new_m_d_skip': 'new_m', 'new_m_w_glu': 'new_m', 'new_m_ffn_norm': 'new_m', 'new_m_ffn_w_up': 'new_m', 'new_m_ffn_conv_w': 'new_m', 'new_m_ffn_w_down': 'new_m', 'new_v_attn_norm': 'new_v', 'new_v_mix_w_in': 'new_v', 'new_v_cq_norm': 'new_v', 'new_v_ckv_norm': 'new_v', 'new_v_w_uq': 'new_v', 'new_v_w_ukv': 'new_v', 'new_v_q_gain': 'new_v', 'new_v_k_gain': 'new_v', 'new_v_sconv_w': 'new_v', 'new_v_mix_w_out': 'new_v', 'new_v_ssm_norm': 'new_v', 'new_v_ssm_w_in': 'new_v', 'new_v_lambda_re': 'new_v', 'new_v_lambda_im': 'new_v', 'new_v_log_step': 'new_v', 'new_v_b_re': 'new_v', 'new_v_b_im': 'new_v', 'new_v_c_re': 'new_v', 'new_v_c_im': 'new_v', 'new_v_d_skip': 'new_v', 'new_v_w_glu': 'new_v', 'new_v_ffn_norm': 'new_v', 'new_v_ffn_w_up': 'new_v', 'new_v_ffn_conv_w': 'new_v', 'new_v_ffn_w_down': 'new_v'}


def _forward(args):
    return _fwd_reference(*[args[k] for k in FWD_PARAMS])


def _output_shape():
    out = _jax.eval_shape(lambda: _forward(_fwd_setup_inputs(0)))
    return out.shape, out.dtype

N_MICROBATCH = 1
ADAM_LR = 0.001
ADAM_B1 = 0.9
ADAM_B2 = 0.999
ADAM_EPS = 1e-08
ADAM_WD = 0.01
ADAM_STEP = 10
PER_EXAMPLE_BATCH_AXIS = {'x': 0, 'loss_target': 0}
SHARED_INPUTS = []
_WEIGHT_DTYPES = {'attn_norm': _jnp.float32, 'mix_w_in': _jnp.float32, 'cq_norm': _jnp.float32, 'ckv_norm': _jnp.float32, 'w_uq': _jnp.float32, 'w_ukv': _jnp.float32, 'q_gain': _jnp.float32, 'k_gain': _jnp.float32, 'sconv_w': _jnp.float32, 'mix_w_out': _jnp.float32, 'ssm_norm': _jnp.float32, 'ssm_w_in': _jnp.float32, 'lambda_re': _jnp.float32, 'lambda_im': _jnp.float32, 'log_step': _jnp.float32, 'b_re': _jnp.float32, 'b_im': _jnp.float32, 'c_re': _jnp.float32, 'c_im': _jnp.float32, 'd_skip': _jnp.float32, 'w_glu': _jnp.float32, 'ffn_norm': _jnp.float32, 'ffn_w_up': _jnp.float32, 'ffn_conv_w': _jnp.float32, 'ffn_w_down': _jnp.float32}
MOMENT_SCALE = {'attn_norm': 4.876932e+01, 'mix_w_in': 1.259055e+00, 'cq_norm': 2.579862e-01, 'ckv_norm': 9.296200e-01, 'w_uq': 1.510051e-01, 'w_ukv': 3.709374e-01, 'q_gain': 8.807045e-01, 'k_gain': 8.804712e-01, 'sconv_w': 1.805318e+01, 'mix_w_out': 1.054397e+00, 'ssm_norm': 4.387883e+00, 'ssm_w_in': 5.447982e-01, 'lambda_re': 3.067966e-02, 'lambda_im': 3.269508e-02, 'log_step': 8.771110e+00, 'b_re': 2.057673e-02, 'b_im': 1.863127e-02, 'c_re': 3.605854e-02, 'c_im': 3.758976e-02, 'd_skip': 5.155706e+00, 'w_glu': 1.454226e+00, 'ffn_norm': 2.577386e+01, 'ffn_w_up': 4.482581e-01, 'ffn_conv_w': 3.396477e+00, 'ffn_w_down': 5.709523e-01}


def _to_microbatches(a, axis):
    t = _jnp.moveaxis(a, axis, 0)
    t = t.reshape((N_MICROBATCH, t.shape[0] // N_MICROBATCH) + t.shape[1:])
    return _jnp.moveaxis(t, 1, axis + 1)


def setup_inputs(seed: int = 0) -> dict:
    inp = _fwd_setup_inputs(seed)
    key = _jax.random.fold_in(_jax.random.key(seed), 7919)
    shape, _ = _output_shape()
    out = dict(inp)
    out["loss_target"] = _jax.random.normal(_jax.random.fold_in(key, 0), shape, _jnp.float32)
    for i, name in enumerate(TWIN_WEIGHTS):
        w = inp[name].astype(_jnp.float32)
        if MOMENT_SCALE is None:
            s = _jnp.sqrt(_jnp.mean(_jnp.square(w)) + 1e-30)
        else:
            s = MOMENT_SCALE[name]
        km, kv = _jax.random.split(_jax.random.fold_in(key, i + 1))
        out[name] = w
        out["m_" + name] = s * _jax.random.normal(km, w.shape, _jnp.float32)
        out["v_" + name] = (s * s) * _jax.random.uniform(kv, w.shape, _jnp.float32, 0.5, 1.5)
    if N_MICROBATCH > 1:
        for name, axis in PER_EXAMPLE_BATCH_AXIS.items():
            out[name] = _to_microbatches(out[name], axis)
    return {'x': out['x'], 'attn_norm': out['attn_norm'], 'mix_w_in': out['mix_w_in'], 'cq_norm': out['cq_norm'], 'ckv_norm': out['ckv_norm'], 'w_uq': out['w_uq'], 'w_ukv': out['w_ukv'], 'q_gain': out['q_gain'], 'k_gain': out['k_gain'], 'sconv_w': out['sconv_w'], 'mix_w_out': out['mix_w_out'], 'ssm_norm': out['ssm_norm'], 'ssm_w_in': out['ssm_w_in'], 'lambda_re': out['lambda_re'], 'lambda_im': out['lambda_im'], 'log_step': out['log_step'], 'b_re': out['b_re'], 'b_im': out['b_im'], 'c_re': out['c_re'], 'c_im': out['c_im'], 'd_skip': out['d_skip'], 'w_glu': out['w_glu'], 'ffn_norm': out['ffn_norm'], 'ffn_w_up': out['ffn_w_up'], 'ffn_conv_w': out['ffn_conv_w'], 'ffn_w_down': out['ffn_w_down'], 'loss_target': out['loss_target'], 'm_attn_norm': out['m_attn_norm'], 'm_mix_w_in': out['m_mix_w_in'], 'm_cq_norm': out['m_cq_norm'], 'm_ckv_norm': out['m_ckv_norm'], 'm_w_uq': out['m_w_uq'], 'm_w_ukv': out['m_w_ukv'], 'm_q_gain': out['m_q_gain'], 'm_k_gain': out['m_k_gain'], 'm_sconv_w': out['m_sconv_w'], 'm_mix_w_out': out['m_mix_w_out'], 'm_ssm_norm': out['m_ssm_norm'], 'm_ssm_w_in': out['m_ssm_w_in'], 'm_lambda_re': out['m_lambda_re'], 'm_lambda_im': out['m_lambda_im'], 'm_log_step': out['m_log_step'], 'm_b_re': out['m_b_re'], 'm_b_im': out['m_b_im'], 'm_c_re': out['m_c_re'], 'm_c_im': out['m_c_im'], 'm_d_skip': out['m_d_skip'], 'm_w_glu': out['m_w_glu'], 'm_ffn_norm': out['m_ffn_norm'], 'm_ffn_w_up': out['m_ffn_w_up'], 'm_ffn_conv_w': out['m_ffn_conv_w'], 'm_ffn_w_down': out['m_ffn_w_down'], 'v_attn_norm': out['v_attn_norm'], 'v_mix_w_in': out['v_mix_w_in'], 'v_cq_norm': out['v_cq_norm'], 'v_ckv_norm': out['v_ckv_norm'], 'v_w_uq': out['v_w_uq'], 'v_w_ukv': out['v_w_ukv'], 'v_q_gain': out['v_q_gain'], 'v_k_gain': out['v_k_gain'], 'v_sconv_w': out['v_sconv_w'], 'v_mix_w_out': out['v_mix_w_out'], 'v_ssm_norm': out['v_ssm_norm'], 'v_ssm_w_in': out['v_ssm_w_in'], 'v_lambda_re': out['v_lambda_re'], 'v_lambda_im': out['v_lambda_im'], 'v_log_step': out['v_log_step'], 'v_b_re': out['v_b_re'], 'v_b_im': out['v_b_im'], 'v_c_re': out['v_c_re'], 'v_c_im': out['v_c_im'], 'v_d_skip': out['v_d_skip'], 'v_w_glu': out['v_w_glu'], 'v_ffn_norm': out['v_ffn_norm'], 'v_ffn_w_up': out['v_ffn_w_up'], 'v_ffn_conv_w': out['v_ffn_conv_w'], 'v_ffn_w_down': out['v_ffn_w_down']}


def _loss(weights, diff, rest, loss_target):
    with _jax.named_scope("forward"):
        args = {**rest, TWIN_DIFF_INPUT: diff, **{k: w.astype(_WEIGHT_DTYPES[k]) for k, w in weights.items()}}
        y = _forward(args)
    with _jax.named_scope("loss_head"):
        err = _jnp.square(y.astype(_jnp.float32) - loss_target)
        return 0.5 * _jnp.sum(_jnp.mean(err, axis=-1)) if err.ndim else 0.5 * err


def _adamw(w, g, m, v):
    m = ADAM_B1 * m + (1.0 - ADAM_B1) * g
    v = ADAM_B2 * v + (1.0 - ADAM_B2) * _jnp.square(g)
    m_hat = m / (1.0 - ADAM_B1 ** ADAM_STEP)
    v_hat = v / (1.0 - ADAM_B2 ** ADAM_STEP)
    delta = -ADAM_LR * (m_hat / (_jnp.sqrt(v_hat) + ADAM_EPS) + ADAM_WD * w)
    return delta, m, v


def reference(x, attn_norm, mix_w_in, cq_norm, ckv_norm, w_uq, w_ukv, q_gain, k_gain, sconv_w, mix_w_out, ssm_norm, ssm_w_in, lambda_re, lambda_im, log_step, b_re, b_im, c_re, c_im, d_skip, w_glu, ffn_norm, ffn_w_up, ffn_conv_w, ffn_w_down, loss_target, m_attn_norm, m_mix_w_in, m_cq_norm, m_ckv_norm, m_w_uq, m_w_ukv, m_q_gain, m_k_gain, m_sconv_w, m_mix_w_out, m_ssm_norm, m_ssm_w_in, m_lambda_re, m_lambda_im, m_log_step, m_b_re, m_b_im, m_c_re, m_c_im, m_d_skip, m_w_glu, m_ffn_norm, m_ffn_w_up, m_ffn_conv_w, m_ffn_w_down, v_attn_norm, v_mix_w_in, v_cq_norm, v_ckv_norm, v_w_uq, v_w_ukv, v_q_gain, v_k_gain, v_sconv_w, v_mix_w_out, v_ssm_norm, v_ssm_w_in, v_lambda_re, v_lambda_im, v_log_step, v_b_re, v_b_im, v_c_re, v_c_im, v_d_skip, v_w_glu, v_ffn_norm, v_ffn_w_up, v_ffn_conv_w, v_ffn_w_down):
    given = dict(x=x, attn_norm=attn_norm, mix_w_in=mix_w_in, cq_norm=cq_norm, ckv_norm=ckv_norm, w_uq=w_uq, w_ukv=w_ukv, q_gain=q_gain, k_gain=k_gain, sconv_w=sconv_w, mix_w_out=mix_w_out, ssm_norm=ssm_norm, ssm_w_in=ssm_w_in, lambda_re=lambda_re, lambda_im=lambda_im, log_step=log_step, b_re=b_re, b_im=b_im, c_re=c_re, c_im=c_im, d_skip=d_skip, w_glu=w_glu, ffn_norm=ffn_norm, ffn_w_up=ffn_w_up, ffn_conv_w=ffn_conv_w, ffn_w_down=ffn_w_down, loss_target=loss_target, m_attn_norm=m_attn_norm, m_mix_w_in=m_mix_w_in, m_cq_norm=m_cq_norm, m_ckv_norm=m_ckv_norm, m_w_uq=m_w_uq, m_w_ukv=m_w_ukv, m_q_gain=m_q_gain, m_k_gain=m_k_gain, m_sconv_w=m_sconv_w, m_mix_w_out=m_mix_w_out, m_ssm_norm=m_ssm_norm, m_ssm_w_in=m_ssm_w_in, m_lambda_re=m_lambda_re, m_lambda_im=m_lambda_im, m_log_step=m_log_step, m_b_re=m_b_re, m_b_im=m_b_im, m_c_re=m_c_re, m_c_im=m_c_im, m_d_skip=m_d_skip, m_w_glu=m_w_glu, m_ffn_norm=m_ffn_norm, m_ffn_w_up=m_ffn_w_up, m_ffn_conv_w=m_ffn_conv_w, m_ffn_w_down=m_ffn_w_down, v_attn_norm=v_attn_norm, v_mix_w_in=v_mix_w_in, v_cq_norm=v_cq_norm, v_ckv_norm=v_ckv_norm, v_w_uq=v_w_uq, v_w_ukv=v_w_ukv, v_q_gain=v_q_gain, v_k_gain=v_k_gain, v_sconv_w=v_sconv_w, v_mix_w_out=v_mix_w_out, v_ssm_norm=v_ssm_norm, v_ssm_w_in=v_ssm_w_in, v_lambda_re=v_lambda_re, v_lambda_im=v_lambda_im, v_log_step=v_log_step, v_b_re=v_b_re, v_b_im=v_b_im, v_c_re=v_c_re, v_c_im=v_c_im, v_d_skip=v_d_skip, v_w_glu=v_w_glu, v_ffn_norm=v_ffn_norm, v_ffn_w_up=v_ffn_w_up, v_ffn_conv_w=v_ffn_conv_w, v_ffn_w_down=v_ffn_w_down)
    weights = {n: given[n] for n in TWIN_WEIGHTS}
    shared = {n: given[n] for n in SHARED_INPUTS}
    per_example = {n: given[n] for n in ['x']}
    grad_fn = _jax.value_and_grad(_loss, argnums=(0, 1))

    def one_microbatch(ex, loss_target):
        ex = dict(ex)
        diff = ex.pop(TWIN_DIFF_INPUT)
        return grad_fn(weights, diff, {**shared, **ex}, loss_target)

    if N_MICROBATCH == 1:
        loss, (grad_w, grad_x) = one_microbatch(per_example, given["loss_target"])
    else:
        def body(carry, xs):
            loss_sum, grad_sum = carry
            l_k, (gw_k, gx_k) = one_microbatch(xs[0], xs[1])
            with _jax.named_scope("update"):
                return (loss_sum + l_k, _jax.tree.map(_jnp.add, grad_sum, gw_k)), gx_k

        init = (_jnp.zeros((), _jnp.float32), _jax.tree.map(_jnp.zeros_like, weights))
        (loss, grad_w), grad_x = _jax.lax.scan(body, init, (per_example, given["loss_target"]))
    with _jax.named_scope("update"):
        delta_w, new_m, new_v = {}, {}, {}
        for n in TWIN_WEIGHTS:
            delta_w[n], new_m[n], new_v[n] = _adamw(weights[n], grad_w[n], given["m_" + n], given["v_" + n])
    return (loss, grad_x, *[grad_w[n] for n in TWIN_WEIGHTS], *[delta_w[n] for n in TWIN_WEIGHTS],
            *[new_m[n] for n in TWIN_WEIGHTS], *[new_v[n] for n in TWIN_WEIGHTS])
```

```python
import functools
import math

import numpy as np
import jax
import jax.numpy as jnp
from jax import lax
from jax.experimental import pallas as pl
from jax.experimental.pallas import tpu as pltpu

F32, BF16 = jnp.float32, jnp.bfloat16

D_MODEL = 1024
DEPTH = 4
HEADS = 8
NOPE, ROPE, QK, VD = 64, 32, 96, 64
HP = 128
QR, KVR = 256, 256
CONVC = 512
FFN_H = 2816
G, P, GC = 64, 64, 16
NST = G * P
SLAB = 8
LANE = 128
EPS = 1e-6
ROPE_THETA = 10000.0
ADAM_LR, ADAM_B1, ADAM_B2, ADAM_EPS, ADAM_WD, ADAM_STEP = 0.001, 0.9, 0.999, 1e-08, 0.01, 10
VMEM_LIMIT = 48 * 1024 * 1024
MESH = pl.DeviceIdType.MESH
ANY = pl.BlockSpec(memory_space=pl.ANY)


def _cparams(*sem):
    return pltpu.CompilerParams(dimension_semantics=sem, vmem_limit_bytes=VMEM_LIMIT)


def _pick(dim, prefs):
    for p in prefs:
        if dim % p == 0:
            return p
    return dim


def _rows(s):
    return _pick(s, (512, 256, 128, 64, 32, 16, 8))


def mm(a, b, *, ta=False, tb=False, add=None, out_dtype=F32, name):
    if ta:
        kdim, m = a.shape
    else:
        m, kdim = a.shape
    n = b.shape[0] if tb else b.shape[1]
    tm = _pick(m, (512, 256, 128))
    tn = _pick(n, (512, 256)) if n % 256 == 0 else (n if n <= 2304 else _pick(n, (128,)))
    tk = _pick(kdim, (512, 256, 128))
    nk = kdim // tk
    dn = (((0 if ta else 1,), (1 if tb else 0,)), ((), ()))

    def body(*refs):
        if add is None:
            a_ref, b_ref, o_ref, acc = refs
        else:
            a_ref, b_ref, add_ref, o_ref, acc = refs
        k = pl.program_id(2)

        @pl.when(k == 0)
        def _():
            acc[...] = jnp.zeros_like(acc)

        acc[...] += lax.dot_general(a_ref[...].astype(BF16), b_ref[...].astype(BF16), dn,
                                    preferred_element_type=F32)

        @pl.when(k == nk - 1)
        def _():
            r = acc[...]
            if add is not None:
                r = r + add_ref[...].astype(F32)
            o_ref[...] = r.astype(out_dtype)

    a_spec = pl.BlockSpec((tk, tm), lambda i, j, k: (k, i)) if ta else pl.BlockSpec((tm, tk), lambda i, j, k: (i, k))
    b_spec = pl.BlockSpec((tn, tk), lambda i, j, k: (j, k)) if tb else pl.BlockSpec((tk, tn), lambda i, j, k: (k, j))
    o_spec = pl.BlockSpec((tm, tn), lambda i, j, k: (i, j))
    ins, specs = [a, b], [a_spec, b_spec]
    if add is not None:
        ins.append(add)
        specs.append(o_spec)
    return pl.pallas_call(
        body, name=name, grid=(m // tm, n // tn, nk), in_specs=specs, out_specs=o_spec,
        out_shape=jax.ShapeDtypeStruct((m, n), out_dtype),
        scratch_shapes=[pltpu.VMEM((tm, tn), F32)],
        compiler_params=_cparams("parallel", "parallel", "arbitrary"))(*ins)


def rms_fwd(x, g, *, col=0, out_dtype=BF16, name):
    s = x.shape[0]
    d = g.shape[1]
    tm = _rows(s)

    def body(x_ref, g_ref, o_ref):
        xv = x_ref[...]
        r = lax.rsqrt(jnp.mean(xv * xv, axis=-1, keepdims=True) + EPS)
        o_ref[...] = (xv * r * g_ref[...]).astype(out_dtype)

    return pl.pallas_call(
        body, name=name, grid=(s // tm,),
        in_specs=[pl.BlockSpec((tm, d), lambda i: (i, col)), pl.BlockSpec((1, d), lambda i: (0, 0))],
        out_specs=pl.BlockSpec((tm, d), lambda i: (i, 0)),
        out_shape=jax.ShapeDtypeStruct((s, d), out_dtype),
        compiler_params=_cparams("parallel"))(x, g)


def rms_bwd(x, g, dy, *, col=0, add=None, out_dtype=F32, name):
    s = x.shape[0]
    d = g.shape[1]
    tm = _rows(s)

    def body(*refs):
        if add is None:
            x_ref, g_ref, dy_ref, dx_ref, dg_ref = refs
        else:
            x_ref, g_ref, dy_ref, add_ref, dx_ref, dg_ref = refs

        @pl.when(pl.program_id(0) == 0)
        def _():
            dg_ref[...] = jnp.zeros_like(dg_ref)

        xv = x_ref[...]
        dyv = dy_ref[...].astype(F32)
        r = lax.rsqrt(jnp.mean(xv * xv, axis=-1, keepdims=True) + EPS)
        xh = xv * r
        dg_ref[...] += jnp.sum(dyv * xh, axis=0, keepdims=True)
        dxh = dyv * g_ref[...]
        dx = r * (dxh - xh * jnp.mean(dxh * xh, axis=-1, keepdims=True))
        if add is not None:
            dx = dx + add_ref[...]
        dx_ref[...] = dx.astype(out_dtype)

    row = pl.BlockSpec((tm, d), lambda i: (i, 0))
    ins = [x, g, dy]
    specs = [pl.BlockSpec((tm, d), lambda i: (i, col)), pl.BlockSpec((1, d), lambda i: (0, 0)), row]
    if add is not None:
        ins.append(add)
        specs.append(row)
    return pl.pallas_call(
        body, name=name, grid=(s // tm,), in_specs=specs,
        out_specs=[row, pl.BlockSpec((1, d), lambda i: (0, 0))],
        out_shape=[jax.ShapeDtypeStruct((s, d), out_dtype), jax.ShapeDtypeStruct((1, d), F32)],
        compiler_params=_cparams("arbitrary"))(*ins)


def _rope_tables(s):
    inv = 1.0 / (ROPE_THETA ** (jnp.arange(0, ROPE, 2, dtype=F32) / ROPE))
    ang = jnp.arange(s, dtype=F32)[:, None] * inv[None, :]
    cos, sin = jnp.cos(ang), jnp.sin(ang)
    z = lambda w: jnp.zeros((s, w), F32)
    c = jnp.concatenate([jnp.ones((s, NOPE), F32), cos, cos, z(HP - QK)], axis=1)
    s1 = jnp.concatenate([z(NOPE), -sin, z(HP - NOPE - ROPE // 2)], axis=1)
    s2 = jnp.concatenate([z(NOPE + ROPE // 2), sin, z(HP - QK)], axis=1)
    return c, s1, s2


def qkprep_fwd(qraw, kv, proj, qg, kg, tabs, *, name):
    s = qraw.shape[0]
    tm = _rows(s)
    kr_col = (proj.shape[1] - HP) // HP

    def body(q_ref, k_ref, kr_ref, qg_ref, kg_ref, c_ref, s1_ref, s2_ref, qo_ref, ko_ref):
        c, s1, s2 = c_ref[...], s1_ref[...], s2_ref[...]

        def f(xv, gain):
            r = lax.rsqrt(jnp.sum(xv * xv, axis=-1, keepdims=True) * (1.0 / QK) + EPS)
            xn = xv * r * gain
            return xn * c + pltpu.roll(xn, HP - ROPE // 2, 1) * s1 + pltpu.roll(xn, ROPE // 2, 1) * s2

        qo_ref[...] = f(q_ref[...], qg_ref[...]).astype(BF16)
        ko_ref[...] = f(k_ref[...] + kr_ref[...], kg_ref[...]).astype(BF16)

    head = pl.BlockSpec((tm, HP), lambda i, h: (i, h))
    tab = pl.BlockSpec((tm, HP), lambda i, h: (i, 0))
    gain = pl.BlockSpec((1, HP), lambda i, h: (0, 0))
    return pl.pallas_call(
        body, name=name, grid=(s // tm, HEADS),
        in_specs=[head, head, pl.BlockSpec((tm, HP), lambda i, h: (i, kr_col)), gain, gain, tab, tab, tab],
        out_specs=[head, head],
        out_shape=[jax.ShapeDtypeStruct((s, HEADS * HP), BF16)] * 2,
        compiler_params=_cparams("parallel", "parallel"))(qraw, kv, proj, qg, kg, *tabs)


def qkprep_bwd(qraw, kv, proj, qg, kg, tabs, dq, dk, *, name):
    s = qraw.shape[0]
    tm = _rows(s)
    kr_col = (proj.shape[1] - HP) // HP

    def body(q_ref, k_ref, kr_ref, qg_ref, kg_ref, c_ref, s1_ref, s2_ref, dq_ref, dk_ref,
             dqr_ref, dkr_ref, dkrope_ref, dqg_ref, dkg_ref):
        i, h = pl.program_id(0), pl.program_id(1)
        c, s1, s2 = c_ref[...], s1_ref[...], s2_ref[...]

        @pl.when((i == 0) & (h == 0))
        def _():
            dqg_ref[...] = jnp.zeros_like(dqg_ref)
            dkg_ref[...] = jnp.zeros_like(dkg_ref)

        @pl.when(h == 0)
        def _():
            dkrope_ref[...] = jnp.zeros_like(dkrope_ref)

        def f(xv, gain, dout):
            r = lax.rsqrt(jnp.sum(xv * xv, axis=-1, keepdims=True) * (1.0 / QK) + EPS)
            xh = xv * r
            dxn = dout * c + pltpu.roll(dout * s1, ROPE // 2, 1) + pltpu.roll(dout * s2, HP - ROPE // 2, 1)
            dgain = jnp.sum(dxn * xh, axis=0, keepdims=True)
            dxh = dxn * gain
            dx = r * (dxh - xh * (jnp.sum(dxh * xh, axis=-1, keepdims=True) * (1.0 / QK)))
            return dx, dgain

        dxq, dgq = f(q_ref[...], qg_ref[...], dq_ref[...])
        dxk, dgk = f(k_ref[...] + kr_ref[...], kg_ref[...], dk_ref[...])
        dqr_ref[...] = dxq.astype(BF16)
        dkr_ref[...] = dxk.astype(BF16)
        dqg_ref[...] += dgq
        dkg_ref[...] += dgk
        lane = lax.broadcasted_iota(jnp.int32, dxk.shape, 1)
        dkrope_ref[...] += jnp.where((lane >= NOPE) & (lane < QK), dxk, 0.0)

    head = pl.BlockSpec((tm, HP), lambda i, h: (i, h))
    tab = pl.BlockSpec((tm, HP), lambda i, h: (i, 0))
    gain = pl.BlockSpec((1, HP), lambda i, h: (0, 0))
    return pl.pallas_call(
        body, name=name, grid=(s // tm, HEADS),
        in_specs=[head, head, pl.BlockSpec((tm, HP), lambda i, h: (i, kr_col)), gain, gain, tab, tab, tab, head, head],
        out_specs=[head, head, tab, gain, gain],
        out_shape=[jax.ShapeDtypeStruct((s, HEADS * HP), BF16)] * 2
        + [jax.ShapeDtypeStruct((s, HP), F32), jax.ShapeDtypeStruct((1, HP), F32), jax.ShapeDtypeStruct((1, HP), F32)],
        compiler_params=_cparams("arbitrary", "arbitrary"))(qraw, kv, proj, qg, kg, *tabs, dq, dk)


ATT_SCALE = QK ** -0.5
NEG = -1e30


def _att_tile(s):
    return _pick(s, (512, 256, 128))


def _causal(sv, diag):
    r = lax.broadcasted_iota(jnp.int32, sv.shape, 0)
    c = lax.broadcasted_iota(jnp.int32, sv.shape, 1)
    return jnp.where(diag & (c > r), NEG, sv)


NT = (((1,), (1,)), ((), ()))
TN = (((0,), (0,)), ((), ()))


def attn_fwd(q, k, kv, *, name):
    s = q.shape[0]
    t = _att_tile(s)
    nb = s // t

    def body(q_ref, k_ref, v_ref, o_ref, lse_ref, m_s, l_s, acc):
        qb, kb = pl.program_id(1), pl.program_id(2)

        @pl.when(kb == 0)
        def _():
            m_s[...] = jnp.full_like(m_s, NEG)
            l_s[...] = jnp.zeros_like(l_s)
            acc[...] = jnp.zeros_like(acc)

        @pl.when(kb <= qb)
        def _():
            sv = lax.dot_general(q_ref[...], k_ref[...], NT, preferred_element_type=F32) * ATT_SCALE
            sv = _causal(sv, kb == qb)
            m_new = jnp.maximum(m_s[...], jnp.max(sv, axis=-1, keepdims=True))
            alpha = jnp.exp(m_s[...] - m_new)
            p = jnp.exp(sv - m_new)
            l_s[...] = alpha * l_s[...] + jnp.sum(p, axis=-1, keepdims=True)
            acc[...] = alpha * acc[...] + jnp.dot(p.astype(BF16), v_ref[...].astype(BF16), preferred_element_type=F32)
            m_s[...] = m_new

        @pl.when(kb == qb)
        def _():
            o_ref[...] = acc[...] / l_s[...]
            lse_ref[...] = jnp.broadcast_to(m_s[...] + jnp.log(l_s[...]), lse_ref.shape)

    qs = pl.BlockSpec((t, HP), lambda h, i, j: (i, h))
    ks = pl.BlockSpec((t, HP), lambda h, i, j: (jnp.minimum(i, j), h))
    vs = pl.BlockSpec((t, HP), lambda h, i, j: (jnp.minimum(i, j), HEADS + h))
    return pl.pallas_call(
        body, name=name, grid=(HEADS, nb, nb), in_specs=[qs, ks, vs], out_specs=[qs, qs],
        out_shape=[jax.ShapeDtypeStruct((s, HEADS * HP), F32)] * 2,
        scratch_shapes=[pltpu.VMEM((t, 1), F32), pltpu.VMEM((t, 1), F32), pltpu.VMEM((t, HP), F32)],
        compiler_params=_cparams("parallel", "parallel", "arbitrary"))(q, k, kv)


def attn_bwd_dq(q, k, kv, o, lse, do, *, name):
    s = q.shape[0]
    t = _att_tile(s)
    nb = s // t

    def body(q_ref, k_ref, v_ref, o_ref, lse_ref, do_ref, dq_ref, acc):
        qb, kb = pl.program_id(1), pl.program_id(2)

        @pl.when(kb == 0)
        def _():
            acc[...] = jnp.zeros_like(acc)

        @pl.when(kb <= qb)
        def _():
            dov = do_ref[...]
            delta = jnp.sum(dov * o_ref[...], axis=-1, keepdims=True)
            sv = lax.dot_general(q_ref[...], k_ref[...], NT, preferred_element_type=F32) * ATT_SCALE
            sv = _causal(sv, kb == qb)
            p = jnp.exp(sv - lse_ref[...][:, 0:1])
            dp = lax.dot_general(dov.astype(BF16), v_ref[...].astype(BF16), NT, preferred_element_type=F32)
            ds = (p * (dp - delta)).astype(BF16)
            acc[...] += jnp.dot(ds, k_ref[...], preferred_element_type=F32)

        @pl.when(kb == qb)
        def _():
            dq_ref[...] = acc[...] * ATT_SCALE

    qs = pl.BlockSpec((t, HP), lambda h, i, j: (i, h))
    ks = pl.BlockSpec((t, HP), lambda h, i, j: (jnp.minimum(i, j), h))
    vs = pl.BlockSpec((t, HP), lambda h, i, j: (jnp.minimum(i, j), HEADS + h))
    return pl.pallas_call(
        body, name=name, grid=(HEADS, nb, nb), in_specs=[qs, ks, vs, qs, qs, qs], out_specs=qs,
        out_shape=jax.ShapeDtypeStruct((s, HEADS * HP), F32),
        scratch_shapes=[pltpu.VMEM((t, HP), F32)],
        compiler_params=_cparams("parallel", "parallel", "arbitrary"))(q, k, kv, o, lse, do)


def attn_bwd_dkv(q, k, kv, o, lse, do, *, name):
    s = q.shape[0]
    t = _att_tile(s)
    nb = s // t

    def body(q_ref, k_ref, v_ref, o_ref, lse_ref, do_ref, dk_ref, dv_ref, dk_acc, dv_acc):
        kb, qb = pl.program_id(1), pl.program_id(2)

        @pl.when(qb == 0)
        def _():
            dk_acc[...] = jnp.zeros_like(dk_acc)
            dv_acc[...] = jnp.zeros_like(dv_acc)

        @pl.when(qb >= kb)
        def _():
            dov = do_ref[...]
            delta = jnp.sum(dov * o_ref[...], axis=-1, keepdims=True)
            sv = lax.dot_general(q_ref[...], k_ref[...], NT, preferred_element_type=F32) * ATT_SCALE
            sv = _causal(sv, kb == qb)
            p = jnp.exp(sv - lse_ref[...][:, 0:1])
            dob = dov.astype(BF16)
            dp = lax.dot_general(dob, v_ref[...].astype(BF16), NT, preferred_element_type=F32)
            ds = (p * (dp - delta)).astype(BF16)
            dv_acc[...] += lax.dot_general(p.astype(BF16), dob, TN, preferred_element_type=F32)
            dk_acc[...] += lax.dot_general(ds, q_ref[...], TN, preferred_element_type=F32)

        @pl.when(qb == nb - 1)
        def _():
            dk_ref[...] = dk_acc[...] * ATT_SCALE
            dv_ref[...] = dv_acc[...].astype(BF16)

    qs = pl.BlockSpec((t, HP), lambda h, j, i: (jnp.maximum(i, j), h))
    ks = pl.BlockSpec((t, HP), lambda h, j, i: (j, h))
    vs = pl.BlockSpec((t, HP), lambda h, j, i: (j, HEADS + h))
    return pl.pallas_call(
        body, name=name, grid=(HEADS, nb, nb), in_specs=[qs, ks, vs, qs, qs, qs], out_specs=[ks, ks],
        out_shape=[jax.ShapeDtypeStruct((s, HEADS * HP), F32), jax.ShapeDtypeStruct((s, HEADS * HP), BF16)],
        scratch_shapes=[pltpu.VMEM((t, HP), F32), pltpu.VMEM((t, HP), F32)],
        compiler_params=_cparams("parallel", "parallel", "arbitrary"))(q, k, kv, o, lse, do)


HALO = 8


def _conv3(zw, w):
    return w[2:3] * zw + w[1:2] * pltpu.roll(zw, 1, 0) + w[0:1] * pltpu.roll(zw, 2, 0)


def _conv3_t(dc, w):
    n = dc.shape[0]
    return w[2:3] * dc + w[1:2] * pltpu.roll(dc, n - 1, 0) + w[0:1] * pltpu.roll(dc, n - 2, 0)


def _conv3_dw(dc, zw, r):
    z0 = zw[HALO:HALO + r]
    z1 = pltpu.roll(zw, 1, 0)[HALO:HALO + r]
    z2 = pltpu.roll(zw, 2, 0)[HALO:HALO + r]
    return [jnp.sum(dc * z, axis=0, keepdims=True) for z in (z2, z1, z0)]


def _halo_specs(r, colfn):
    rb = r // HALO
    cur = pl.BlockSpec((r, LANE), lambda j, i: (i, colfn(j)))
    prev = pl.BlockSpec((HALO, LANE), lambda j, i: (jnp.maximum(i * rb - 1, 0), colfn(j)))

    def nxt(nrow_blocks):
        return pl.BlockSpec((HALO, LANE), lambda j, i: (jnp.minimum((i + 1) * rb, nrow_blocks * rb - 1), colfn(j)))

    return cur, prev, nxt


def ffnact_fwd(up, w, *, name):
    s, c2 = up.shape
    hh = c2 // 2
    nj = hh // LANE
    r = _rows(s)
    nt = s // r

    def body(g_ref, gp_ref, v_ref, vp_ref, wg_ref, wv_ref, o_ref):
        pm = (pl.program_id(1) > 0).astype(F32)
        cg = _conv3(jnp.concatenate([gp_ref[...] * pm, g_ref[...]], axis=0), wg_ref[...])[HALO:]
        cv = _conv3(jnp.concatenate([vp_ref[...] * pm, v_ref[...]], axis=0), wv_ref[...])[HALO:]
        o_ref[...] = (cg * jax.nn.sigmoid(cg) * cv).astype(BF16)

    gcur, gprev, _ = _halo_specs(r, lambda j: j)
    vcur, vprev, _ = _halo_specs(r, lambda j: nj + j)
    wg = pl.BlockSpec((3, LANE), lambda j, i: (0, j))
    wv = pl.BlockSpec((3, LANE), lambda j, i: (0, nj + j))
    return pl.pallas_call(
        body, name=name, grid=(nj, nt), in_specs=[gcur, gprev, vcur, vprev, wg, wv],
        out_specs=pl.BlockSpec((r, LANE), lambda j, i: (i, j)),
        out_shape=jax.ShapeDtypeStruct((s, hh), BF16),
        compiler_params=_cparams("parallel", "parallel"))(up, up, up, up, w, w)


def ffnact_bwd(up, w, dact, *, name):
    s, c2 = up.shape
    hh = c2 // 2
    nj = hh // LANE
    r = _rows(s)
    nt = s // r

    def body(g_ref, gp_ref, gn_ref, v_ref, vp_ref, vn_ref, wg_ref, wv_ref, da_ref, dan_ref,
             dg_ref, dv_ref, dwg_ref, dwv_ref):
        i = pl.program_id(1)
        pm = (i > 0).astype(F32)
        nm = (i < nt - 1).astype(F32)

        @pl.when(i == 0)
        def _():
            dwg_ref[...] = jnp.zeros_like(dwg_ref)
            dwv_ref[...] = jnp.zeros_like(dwv_ref)

        wg, wv = wg_ref[...], wv_ref[...]
        zg = jnp.concatenate([gp_ref[...] * pm, g_ref[...], gn_ref[...]], axis=0)
        zv = jnp.concatenate([vp_ref[...] * pm, v_ref[...], vn_ref[...]], axis=0)
        cg = _conv3(zg, wg)[HALO:]
        cv = _conv3(zv, wv)[HALO:]
        da = jnp.concatenate([da_ref[...], dan_ref[...] * nm], axis=0)
        sg = jax.nn.sigmoid(cg)
        dcg = da * cv * (sg * (1.0 + cg * (1.0 - sg)))
        dcv = da * (cg * sg)
        dg_ref[...] = _conv3_t(dcg, wg)[:r].astype(BF16)
        dv_ref[...] = _conv3_t(dcv, wv)[:r].astype(BF16)
        for kk, (a, b) in enumerate(zip(_conv3_dw(dcg[:r], zg, r), _conv3_dw(dcv[:r], zv, r))):
            dwg_ref[kk:kk + 1, :] += a
            dwv_ref[kk:kk + 1, :] += b

    gcur, gprev, gnext = _halo_specs(r, lambda j: j)
    vcur, vprev, vnext = _halo_specs(r, lambda j: nj + j)
    acur, _, anext = _halo_specs(r, lambda j: j)
    wg = pl.BlockSpec((3, LANE), lambda j, i: (0, j))
    wv = pl.BlockSpec((3, LANE), lambda j, i: (0, nj + j))
    dupg, dupv, dwg, dwv = pl.pallas_call(
        body, name=name, grid=(nj, nt),
        in_specs=[gcur, gprev, gnext(nt), vcur, vprev, vnext(nt), wg, wv, acur, anext(nt)],
        out_specs=[acur, acur, wg, wg],
        out_shape=[jax.ShapeDtypeStruct((s, hh), BF16), jax.ShapeDtypeStruct((s, hh), BF16),
                   jax.ShapeDtypeStruct((3, hh), F32), jax.ShapeDtypeStruct((3, hh), F32)],
        compiler_params=_cparams("parallel", "arbitrary"))(up, up, up, up, up, up, w, w, dact, dact)
    return jnp.concatenate([dupg, dupv], axis=1), jnp.concatenate([dwg, dwv], axis=1)


def sconv_fwd(proj, w, *, name):
    s = proj.shape[0]
    nj = CONVC // LANE
    r = _rows(s)
    nt = s // r

    def body(b_ref, c_ref, cp_ref, x_ref, xp_ref, w_ref, o_ref):
        pm = (pl.program_id(1) > 0).astype(F32)
        zw = jnp.concatenate([cp_ref[...] * xp_ref[...] * pm, c_ref[...] * x_ref[...]], axis=0)
        o_ref[...] = (b_ref[...] * _conv3(zw, w_ref[...])[HALO:]).astype(BF16)

    bcur, _, _ = _halo_specs(r, lambda j: 4 + j)
    ccur, cprev, _ = _halo_specs(r, lambda j: 8 + j)
    xcur, xprev, _ = _halo_specs(r, lambda j: 12 + j)
    ws = pl.BlockSpec((3, LANE), lambda j, i: (0, j))
    return pl.pallas_call(
        body, name=name, grid=(nj, nt), in_specs=[bcur, ccur, cprev, xcur, xprev, ws],
        out_specs=pl.BlockSpec((r, LANE), lambda j, i: (i, j)),
        out_shape=jax.ShapeDtypeStruct((s, CONVC), BF16),
        compiler_params=_cparams("parallel", "parallel"))(proj, proj, proj, proj, proj, w)


def sconv_bwd(proj, w, dy, *, name):
    s = proj.shape[0]
    nj = CONVC // LANE
    r = _rows(s)
    nt = s // r

    def body(b_ref, bn_ref, c_ref, cp_ref, x_ref, xp_ref, w_ref, dy_ref, dyn_ref, db_ref, dc_ref, dx_ref, dw_ref):
        i = pl.program_id(1)
        pm = (i > 0).astype(F32)
        nm = (i < nt - 1).astype(F32)

        @pl.when(i == 0)
        def _():
            dw_ref[...] = jnp.zeros_like(dw_ref)

        wv = w_ref[...]
        zw = jnp.concatenate([cp_ref[...] * xp_ref[...] * pm, c_ref[...] * x_ref[...]], axis=0)
        conv = _conv3(zw, wv)[HALO:]
        dyv = dy_ref[...]
        db_ref[...] = (dyv * conv).astype(BF16)
        dconv = jnp.concatenate([dyv * b_ref[...], dyn_ref[...] * bn_ref[...] * nm], axis=0)
        dz = _conv3_t(dconv, wv)[:r]
        dc_ref[...] = (dz * x_ref[...]).astype(BF16)
        dx_ref[...] = (dz * c_ref[...]).astype(BF16)
        for kk, a in enumerate(_conv3_dw(dconv[:r], zw, r)):
            dw_ref[kk:kk + 1, :] += a

    bcur, _, bnext = _halo_specs(r, lambda j: 4 + j)
    ccur, cprev, _ = _halo_specs(r, lambda j: 8 + j)
    xcur, xprev, _ = _halo_specs(r, lambda j: 12 + j)
    ycur, _, ynext = _halo_specs(r, lambda j: j)
    ws = pl.BlockSpec((3, LANE), lambda j, i: (0, j))
    out = pl.BlockSpec((r, LANE), lambda j, i: (i, j))
    db, dc, dx, dw = pl.pallas_call(
        body, name=name, grid=(nj, nt),
        in_specs=[bcur, bnext(nt), ccur, cprev, xcur, xprev, ws, ycur, ynext(nt)],
        out_specs=[out, out, out, ws],
        out_shape=[jax.ShapeDtypeStruct((s, CONVC), BF16)] * 3 + [jax.ShapeDtypeStruct((3, CONVC), F32)],
        compiler_params=_cparams("parallel", "arbitrary"))(proj, proj, proj, proj, proj, proj, w, dy, dy)
    return jnp.concatenate([db, dc, dx], axis=1), dw


SW = 512
NJ = NST // SW


def _scan_tables(ar, ai):
    def cmul(x, y):
        return x[0] * y[0] - x[1] * y[1], x[0] * y[1] + x[1] * y[0]

    def build(a, reverse):
        pw = [a]
        for _ in range(SLAB - 1):
            pw.append(cmul(pw[-1], a))
        row = jnp.arange(SLAB)[:, None]
        tabs = []
        for kk in (1, 2, 4):
            mask = ((row < SLAB - kk) if reverse else (row >= kk)).astype(F32)
            tabs += [mask * pw[kk - 1][0][None, :], mask * pw[kk - 1][1][None, :]]
        order = list(range(SLAB - 1, -1, -1)) if reverse else list(range(SLAB))
        tabs += [jnp.stack([pw[o][0] for o in order]), jnp.stack([pw[o][1] for o in order])]
        return jnp.stack(tabs)

    return build((ar, ai), False), build((ar, -ai), True)


def _slab_scan(xr, xi, tabs, cr, ci, reverse):
    for n, kk in enumerate((1, 2, 4)):
        sh = SLAB - kk if reverse else kk
        tr, ti = tabs[2 * n], tabs[2 * n + 1]
        sr, si = pltpu.roll(xr, sh, 0), pltpu.roll(xi, sh, 0)
        xr, xi = xr + tr * sr - ti * si, xi + tr * si + ti * sr
    tr, ti = tabs[6], tabs[7]
    return xr + tr * cr - ti * ci, xi + tr * ci + ti * cr


def s5_fwd(u, bbd_r, bbd_i, cbd_r, cbd_i, tab, *, name):
    s = u.shape[0]
    tbk = _rows(s)
    nt = s // tbk
    nsl = tbk // SLAB

    def body(u_ref, br_ref, bi_ref, cr_ref, ci_ref, tab_ref, y_ref, sr_ref, si_ref, bur, bui, carry):
        @pl.when(pl.program_id(1) == 0)
        def _():
            carry[...] = jnp.zeros_like(carry)

        ub = u_ref[...].astype(BF16)
        bur[...] = jnp.dot(ub, br_ref[0], preferred_element_type=F32)
        bui[...] = jnp.dot(ub, bi_ref[0], preferred_element_type=F32)
        tabs = [tab_ref[n] for n in range(8)]

        def slab(n, c):
            r0 = pl.multiple_of(n * SLAB, SLAB)
            sr, si = _slab_scan(bur[pl.ds(r0, SLAB), :], bui[pl.ds(r0, SLAB), :], tabs, c[0], c[1], False)
            sr_ref[pl.ds(r0, SLAB), :] = sr
            si_ref[pl.ds(r0, SLAB), :] = si
            return (jnp.broadcast_to(sr[SLAB - 1:SLAB], sr.shape), jnp.broadcast_to(si[SLAB - 1:SLAB], si.shape))

        cr, ci = lax.fori_loop(0, nsl, slab, (carry[0], carry[1]))
        carry[0] = cr
        carry[1] = ci
        y_ref[...] = (jnp.dot(sr_ref[...].astype(BF16), cr_ref[0], preferred_element_type=F32)
                      - jnp.dot(si_ref[...].astype(BF16), ci_ref[0], preferred_element_type=F32))

    us = pl.BlockSpec((tbk, LANE), lambda j, t: (t, j))
    bs = pl.BlockSpec((1, LANE, SW), lambda j, t: (j, 0, 0))
    cs = pl.BlockSpec((1, SW, LANE), lambda j, t: (j, 0, 0))
    ts = pl.BlockSpec((8, SLAB, SW), lambda j, t: (0, 0, j))
    ss = pl.BlockSpec((tbk, SW), lambda j, t: (t, j))
    return pl.pallas_call(
        body, name=name, grid=(NJ, nt), in_specs=[us, bs, bs, cs, cs, ts], out_specs=[us, ss, ss],
        out_shape=[jax.ShapeDtypeStruct((s, D_MODEL), F32), jax.ShapeDtypeStruct((s, NST), F32),
                   jax.ShapeDtypeStruct((s, NST), F32)],
        scratch_shapes=[pltpu.VMEM((tbk, SW), F32), pltpu.VMEM((tbk, SW), F32), pltpu.VMEM((2, SLAB, SW), F32)],
        compiler_params=_cparams("parallel", "arbitrary"))(u, bbd_r, bbd_i, cbd_r, cbd_i, tab)


def s5_bwd(u, dy, dskip, st_r, st_i, bbd_r, bbd_i, cbd_r, cbd_i, tabrev, *, name):
    s = u.shape[0]
    tbk = _rows(s)
    nt = s // tbk
    nsl = tbk // SLAB
    rbk = tbk // SLAB

    def body(u_ref, dy_ref, d_ref, sr_ref, si_ref, pr_ref, pi_ref, br_ref, bi_ref, cr_ref, ci_ref, tab_ref,
             du_ref, dbr_ref, dbi_ref, dcr_ref, dci_ref, da_ref, lam_r, lam_i, carry):
        t = pl.program_id(1)

        @pl.when(t == 0)
        def _():
            carry[...] = jnp.zeros_like(carry)
            dbr_ref[...] = jnp.zeros_like(dbr_ref)
            dbi_ref[...] = jnp.zeros_like(dbi_ref)
            dcr_ref[...] = jnp.zeros_like(dcr_ref)
            dci_ref[...] = jnp.zeros_like(dci_ref)
            da_ref[...] = jnp.zeros_like(da_ref)

        dyv = dy_ref[...]
        dyh = dyv.astype(BF16)
        lam_r[...] = lax.dot_general(dyh, cr_ref[0], NT, preferred_element_type=F32)
        lam_i[...] = -lax.dot_general(dyh, ci_ref[0], NT, preferred_element_type=F32)
        tabs = [tab_ref[n] for n in range(8)]

        def slab(n, c):
            r0 = pl.multiple_of((nsl - 1 - n) * SLAB, SLAB)
            lr, li = _slab_scan(lam_r[pl.ds(r0, SLAB), :], lam_i[pl.ds(r0, SLAB), :], tabs, c[0], c[1], True)
            lam_r[pl.ds(r0, SLAB), :] = lr
            lam_i[pl.ds(r0, SLAB), :] = li
            return (jnp.broadcast_to(lr[0:1], lr.shape), jnp.broadcast_to(li[0:1], li.shape))

        cr, ci = lax.fori_loop(0, nsl, slab, (carry[0], carry[1]))
        carry[0] = cr
        carry[1] = ci
        lr, li = lam_r[...], lam_i[...]
        lrh, lih = lr.astype(BF16), li.astype(BF16)
        du = (dyv * d_ref[...] + lax.dot_general(lrh, br_ref[0], NT, preferred_element_type=F32)
              + lax.dot_general(lih, bi_ref[0], NT, preferred_element_type=F32))
        du_ref[...] = du.astype(BF16)
        ub = u_ref[...].astype(BF16)
        dbr_ref[0] += lax.dot_general(ub, lrh, TN, preferred_element_type=F32)
        dbi_ref[0] += lax.dot_general(ub, lih, TN, preferred_element_type=F32)
        srv, siv = sr_ref[...], si_ref[...]
        dcr_ref[0] += lax.dot_general(srv.astype(BF16), dyh, TN, preferred_element_type=F32)
        dci_ref[0] -= lax.dot_general(siv.astype(BF16), dyh, TN, preferred_element_type=F32)
        first = lax.broadcasted_iota(jnp.int32, srv.shape, 0) == 0
        pm = (t < nt - 1).astype(F32)
        spr = jnp.where(first, pr_ref[SLAB - 1:SLAB, :] * pm, pltpu.roll(srv, 1, 0))
        spi = jnp.where(first, pi_ref[SLAB - 1:SLAB, :] * pm, pltpu.roll(siv, 1, 0))
        da_ref[0:1, :] += jnp.sum(lr * spr + li * spi, axis=0, keepdims=True)
        da_ref[1:2, :] += jnp.sum(li * spr - lr * spi, axis=0, keepdims=True)

    rv = lambda t: nt - 1 - t
    us = pl.BlockSpec((tbk, LANE), lambda j, t: (rv(t), j))
    ds = pl.BlockSpec((1, LANE), lambda j, t: (0, j))
    ss = pl.BlockSpec((tbk, SW), lambda j, t: (rv(t), j))
    ps = pl.BlockSpec((SLAB, SW), lambda j, t: (jnp.maximum(rv(t) * rbk - 1, 0), j))
    bs = pl.BlockSpec((1, LANE, SW), lambda j, t: (j, 0, 0))
    cs = pl.BlockSpec((1, SW, LANE), lambda j, t: (j, 0, 0))
    ts = pl.BlockSpec((8, SLAB, SW), lambda j, t: (0, 0, j))
    das = pl.BlockSpec((2, SW), lambda j, t: (0, j))
    return pl.pallas_call(
        body, name=name, grid=(NJ, nt),
        in_specs=[us, us, ds, ss, ss, ps, ps, bs, bs, cs, cs, ts],
        out_specs=[us, bs, bs, cs, cs, das],
        out_shape=[jax.ShapeDtypeStruct((s, D_MODEL), BF16),
                   jax.ShapeDtypeStruct((NJ, LANE, SW), F32), jax.ShapeDtypeStruct((NJ, LANE, SW), F32),
                   jax.ShapeDtypeStruct((NJ, SW, LANE), F32), jax.ShapeDtypeStruct((NJ, SW, LANE), F32),
                   jax.ShapeDtypeStruct((2, NST), F32)],
        scratch_shapes=[pltpu.VMEM((tbk, SW), F32), pltpu.VMEM((tbk, SW), F32), pltpu.VMEM((2, SLAB, SW), F32)],
        compiler_params=_cparams("parallel", "arbitrary"))(
            u, dy, dskip, st_r, st_i, st_r, st_i, bbd_r, bbd_i, cbd_r, cbd_i, tabrev)


GELU_C = math.sqrt(2.0 / math.pi)
GELU_A = 0.044715


def s5post_fwd(y, u, dskip, *, name):
    s = y.shape[0]
    tm = _rows(s)

    def body(y_ref, u_ref, d_ref, o_ref):
        z = y_ref[...] + d_ref[...] * u_ref[...]
        o_ref[...] = (0.5 * z * (1.0 + jnp.tanh(GELU_C * (z + GELU_A * z * z * z)))).astype(BF16)

    row = pl.BlockSpec((tm, D_MODEL), lambda i: (i, 0))
    vec = pl.BlockSpec((1, D_MODEL), lambda i: (0, 0))
    return pl.pallas_call(body, name=name, grid=(s // tm,), in_specs=[row, row, vec], out_specs=row,
                          out_shape=jax.ShapeDtypeStruct((s, D_MODEL), BF16),
                          compiler_params=_cparams("parallel"))(y, u, dskip)


def s5post_bwd(y, u, dskip, dg, *, name):
    s = y.shape[0]
    tm = _rows(s)

    def body(y_ref, u_ref, d_ref, dg_ref, dz_ref, dd_ref):
        @pl.when(pl.program_id(0) == 0)
        def _():
            dd_ref[...] = jnp.zeros_like(dd_ref)

        uv = u_ref[...]
        z = y_ref[...] + d_ref[...] * uv
        th = jnp.tanh(GELU_C * (z + GELU_A * z * z * z))
        dgelu = 0.5 * (1.0 + th) + 0.5 * z * (1.0 - th * th) * (GELU_C * (1.0 + 3.0 * GELU_A * z * z))
        dz = dg_ref[...] * dgelu
        dz_ref[...] = dz
        dd_ref[...] += jnp.sum(dz * uv, axis=0, keepdims=True)

    row = pl.BlockSpec((tm, D_MODEL), lambda i: (i, 0))
    vec = pl.BlockSpec((1, D_MODEL), lambda i: (0, 0))
    return pl.pallas_call(body, name=name, grid=(s // tm,), in_specs=[row, row, vec, row], out_specs=[row, vec],
                          out_shape=[jax.ShapeDtypeStruct((s, D_MODEL), F32), jax.ShapeDtypeStruct((1, D_MODEL), F32)],
                          compiler_params=_cparams("arbitrary"))(y, u, dskip, dg)


def glu_fwd(glu, x, *, name):
    s = x.shape[0]
    tm = _rows(s)

    def body(a_ref, b_ref, x_ref, o_ref):
        o_ref[...] = x_ref[...] + a_ref[...] * jax.nn.sigmoid(b_ref[...])

    row = pl.BlockSpec((tm, D_MODEL), lambda i: (i, 0))
    return pl.pallas_call(body, name=name, grid=(s // tm,),
                          in_specs=[row, pl.BlockSpec((tm, D_MODEL), lambda i: (i, 1)), row], out_specs=row,
                          out_shape=jax.ShapeDtypeStruct((s, D_MODEL), F32),
                          compiler_params=_cparams("parallel"))(glu, glu, x)


def glu_bwd(glu, dx, *, name):
    s = dx.shape[0]
    tm = _rows(s)

    def body(a_ref, b_ref, dx_ref, o_ref):
        sg = jax.nn.sigmoid(b_ref[...])
        dxv = dx_ref[...]
        o_ref[:, :D_MODEL] = (dxv * sg).astype(BF16)
        o_ref[:, D_MODEL:] = (dxv * a_ref[...] * sg * (1.0 - sg)).astype(BF16)

    row = pl.BlockSpec((tm, D_MODEL), lambda i: (i, 0))
    return pl.pallas_call(body, name=name, grid=(s // tm,),
                          in_specs=[row, pl.BlockSpec((tm, D_MODEL), lambda i: (i, 1)), row],
                          out_specs=pl.BlockSpec((tm, 2 * D_MODEL), lambda i: (i, 0)),
                          out_shape=jax.ShapeDtypeStruct((s, 2 * D_MODEL), BF16),
                          compiler_params=_cparams("parallel"))(glu, glu, dx)


def loss_head(y, target, *, name):
    s = y.shape[0]
    tm = _rows(s)

    def body(y_ref, t_ref, dy_ref, l_ref):
        @pl.when(pl.program_id(0) == 0)
        def _():
            l_ref[...] = jnp.zeros_like(l_ref)

        e = y_ref[...] - t_ref[...]
        dy_ref[...] = e * (1.0 / D_MODEL)
        e2 = jnp.sum((e * e).reshape(tm // 8, 8, D_MODEL), axis=0)
        acc = e2[:, 0:LANE]
        for kk in range(1, D_MODEL // LANE):
            acc = acc + e2[:, kk * LANE:(kk + 1) * LANE]
        l_ref[...] += acc

    row = pl.BlockSpec((tm, D_MODEL), lambda i: (i, 0))
    return pl.pallas_call(body, name=name, grid=(s // tm,), in_specs=[row, row],
                          out_specs=[row, pl.BlockSpec((8, LANE), lambda i: (0, 0))],
                          out_shape=[jax.ShapeDtypeStruct((s, D_MODEL), F32), jax.ShapeDtypeStruct((8, LANE), F32)],
                          compiler_params=_cparams("arbitrary"))(y, target)


PACKW = 1024
NCHIP = 4


def _mesh_pos():
    return lax.axis_index("x"), lax.axis_index("y"), lax.axis_index("c")


def _chip_exchange(buf, scatter, name):
    shape = buf.shape[1:] if scatter else buf.shape

    def body(in_ref, out_ref, send_sems, recv_sems, local_sem):
        x, y, c = _mesh_pos()
        me = 2 * x + y
        peers = [(1 - x, y), (x, 1 - y), (1 - x, 1 - y)]

        def copy(j, px, py):
            src = in_ref.at[2 * px + py] if scatter else in_ref
            return pltpu.make_async_remote_copy(src_ref=src, dst_ref=out_ref.at[me], send_sem=send_sems.at[j],
                                                recv_sem=recv_sems.at[j], device_id=(px, py, c), device_id_type=MESH)

        mine = pltpu.make_async_copy(in_ref.at[me] if scatter else in_ref, out_ref.at[me], local_sem)
        mine.start()
        sends = [copy(j, px, py) for j, (px, py) in enumerate(peers)]
        for cp in sends:
            cp.start()
        for j, (px, py) in enumerate(peers):
            pltpu.make_async_remote_copy(src_ref=in_ref.at[me] if scatter else in_ref, dst_ref=out_ref.at[2 * px + py],
                                         send_sem=send_sems.at[j], recv_sem=recv_sems.at[j],
                                         device_id=(px, py, c), device_id_type=MESH).wait_recv()
        for cp in sends:
            cp.wait_send()
        mine.wait()

    return pl.pallas_call(
        body, name=name, in_specs=[ANY], out_specs=ANY,
        out_shape=jax.ShapeDtypeStruct((NCHIP,) + tuple(shape), buf.dtype),
        scratch_shapes=[pltpu.SemaphoreType.DMA((3,)), pltpu.SemaphoreType.DMA((3,)), pltpu.SemaphoreType.DMA],
    )(buf)


def sibling_swap(bufs, name):
    n = len(bufs)

    def body(*refs):
        ins, outs, send_sems, recv_sems = refs[:n], refs[n:2 * n], refs[2 * n], refs[2 * n + 1]
        x, y, c = _mesh_pos()
        cps = [pltpu.make_async_remote_copy(src_ref=ins[k], dst_ref=outs[k], send_sem=send_sems.at[k],
                                            recv_sem=recv_sems.at[k], device_id=(x, y, 1 - c), device_id_type=MESH)
               for k in range(n)]
        for cp in cps:
            cp.start()
        for cp in cps:
            cp.wait()

    return pl.pallas_call(
        body, name=name, in_specs=[ANY] * n, out_specs=[ANY] * n,
        out_shape=[jax.ShapeDtypeStruct(b.shape, b.dtype) for b in bufs],
        scratch_shapes=[pltpu.SemaphoreType.DMA((n,)), pltpu.SemaphoreType.DMA((n,))],
    )(*bufs)


def sum_slots(buf, *, name):
    _, rows, w = buf.shape
    tm = _pick(rows, (512, 256, 128, 64, 32, 16))

    def body(b_ref, o_ref):
        acc = b_ref[0].astype(F32)
        for kk in range(1, NCHIP):
            acc = acc + b_ref[kk].astype(F32)
        o_ref[...] = acc

    return pl.pallas_call(body, name=name, grid=(rows // tm,),
                          in_specs=[pl.BlockSpec((NCHIP, tm, w), lambda i: (0, i, 0))],
                          out_specs=pl.BlockSpec((tm, w), lambda i: (i, 0)),
                          out_shape=jax.ShapeDtypeStruct((rows, w), F32),
                          compiler_params=_cparams("parallel"))(buf)


def adamw(p_mine, p_other, w, m, v, *, name):
    rows, wd = w.shape
    tm = _pick(rows, (256, 128, 64, 32, 16, 8))
    c1 = 1.0 - ADAM_B1 ** ADAM_STEP
    c2 = 1.0 - ADAM_B2 ** ADAM_STEP

    def body(a_ref, b_ref, w_ref, m_ref, v_ref, g_ref, d_ref, nm_ref, nv_ref):
        g = a_ref[...] + b_ref[...]
        nm = ADAM_B1 * m_ref[...] + (1.0 - ADAM_B1) * g
        nv = ADAM_B2 * v_ref[...] + (1.0 - ADAM_B2) * (g * g)
        g_ref[...] = g
        nm_ref[...] = nm
        nv_ref[...] = nv
        d_ref[...] = -ADAM_LR * ((nm / c1) / (jnp.sqrt(nv / c2) + ADAM_EPS) + ADAM_WD * w_ref[...])

    row = pl.BlockSpec((tm, wd), lambda i: (i, 0))
    return pl.pallas_call(body, name=name, grid=(rows // tm,), in_specs=[row] * 5, out_specs=[row] * 4,
                          out_shape=[jax.ShapeDtypeStruct((rows, wd), F32)] * 4,
                          compiler_params=_cparams("parallel"))(p_mine, p_other, w, m, v)


def _pack(arrs, dtype, row_mult=512):
    flat = jnp.concatenate([a.reshape(-1).astype(dtype) for a in arrs])
    quantum = row_mult * PACKW
    total = -(-flat.shape[0] // quantum) * quantum
    return jnp.pad(flat, (0, total - flat.shape[0])).reshape(total // PACKW, PACKW)


def _unpack(buf, shapes):
    lead = buf.shape[:-2]
    flat = buf.reshape(lead + (-1,))
    out, off = [], 0
    for shp in shapes:
        n = int(np.prod(shp))
        out.append(flat[..., off:off + n].reshape(lead + tuple(shp)))
        off += n
    return out


BIG = [("mix_w_in", 2), ("w_uq", 2), ("w_ukv", 2), ("mix_w_out", 1), ("ssm_w_in", 1), ("w_glu", 2),
       ("ffn_w_up", 2), ("ffn_w_down", 1)]
SMALL = [("sconv_w", 2), ("ssm_norm", 1), ("d_skip", 1), ("ffn_conv_w", 2)]
REPL = ["attn_norm", "cq_norm", "ckv_norm", "q_gain", "k_gain", "lambda_re", "lambda_im", "log_step",
        "b_re", "b_im", "c_re", "c_im", "ffn_norm"]
ORDER = ["attn_norm", "mix_w_in", "cq_norm", "ckv_norm", "w_uq", "w_ukv", "q_gain", "k_gain", "sconv_w", "mix_w_out",
         "ssm_norm", "ssm_w_in", "lambda_re", "lambda_im", "log_step", "b_re", "b_im", "c_re", "c_im", "d_skip",
         "w_glu", "ffn_norm", "ffn_w_up", "ffn_conv_w", "ffn_w_down"]


def _join(g, axis):
    return jnp.concatenate([g[k] for k in range(NCHIP)], axis=axis)


def _split(full, axis):
    return jnp.stack(jnp.split(full, NCHIP, axis=axis))


def _discretize(lr, li, ls, b_re, b_im):
    dt = jnp.exp(ls)[:, None]
    mag = jnp.exp(lr * dt)
    ar, ai = mag * jnp.cos(li * dt), mag * jnp.sin(li * dt)
    nr, ni = ar - 1.0, ai
    den = lr * lr + li * li
    zr, zi = (nr * lr + ni * li) / den, (ni * lr - nr * li) / den
    bbar_r = zr[..., None] * b_re - zi[..., None] * b_im
    bbar_i = zr[..., None] * b_im + zi[..., None] * b_re
    return ar, ai, bbar_r, bbar_i


def _b_blockdiag(bbar):
    gl = G // NJ
    bb = bbar.reshape(NJ, gl, P, GC).transpose(0, 1, 3, 2)
    return jnp.einsum("jgcp,gh->jgchp", bb, jnp.eye(gl, dtype=bbar.dtype)).reshape(NJ, gl * GC, gl * P)


def _b_blockdiag_t(dbd):
    gl = G // NJ
    d = jnp.einsum("jgchp,gh->jgcp", dbd.reshape(NJ, gl, GC, gl, P), jnp.eye(gl, dtype=dbd.dtype))
    return d.transpose(0, 1, 3, 2).reshape(G, P, GC)


def _c_blockdiag(cmat):
    gl = G // NJ
    cc = cmat.reshape(NJ, gl, GC, P).transpose(0, 1, 3, 2)
    return jnp.einsum("jgpc,gh->jgphc", cc, jnp.eye(gl, dtype=cmat.dtype)).reshape(NJ, gl * P, gl * GC)


def _c_blockdiag_t(dbd):
    gl = G // NJ
    d = jnp.einsum("jgphc,gh->jgpc", dbd.reshape(NJ, gl, P, gl, GC), jnp.eye(gl, dtype=dbd.dtype))
    return d.transpose(0, 1, 3, 2).reshape(G, GC, P)


def _pad_heads_cols(w, width):
    r = w.shape[0]
    return jnp.pad(w.reshape(r, HEADS, width), ((0, 0), (0, 0), (0, HP - width))).reshape(r, HEADS * HP)


def _unpad_heads_cols(w, width):
    r = w.shape[0]
    return w.reshape(r, HEADS, HP)[:, :, :width].reshape(r, HEADS * width)


W_IN_SPLIT = (QR + KVR, QR + KVR + ROPE)


def _w_in_layout(w):
    a, b = W_IN_SPLIT
    kr = jnp.pad(w[:, a:b], ((0, 0), (NOPE, HP - QK)))
    return jnp.concatenate([w[:, :a], w[:, b:], kr], axis=1)


def _w_in_layout_t(dw):
    a, b = W_IN_SPLIT
    n = dw.shape[1] - HP
    return jnp.concatenate([dw[:, :a], dw[:, n + NOPE:n + QK], dw[:, a:n]], axis=1)


def _ffn_fwd(x, l, wt, name):
    h = rms_fwd(x, wt["ffn_norm"][l][None], name=f"{name}_norm")
    up = mm(h, wt["ffn_w_up"][l], name=f"{name}_up")
    act = ffnact_fwd(up, wt["ffn_conv_w"][l], name=f"{name}_act")
    out = mm(act, wt["ffn_w_down"][l], add=x, name=f"{name}_down")
    return out, (x, h, up, act)


def _ffn_bwd(dout, saved, l, wt, name):
    x, h, up, act = saved
    g = {}
    dact = mm(dout, wt["ffn_w_down"][l], tb=True, name=f"{name}_ddown")
    g["ffn_w_down"] = mm(act, dout, ta=True, name=f"{name}_dwdown")
    dup, g["ffn_conv_w"] = ffnact_bwd(up, wt["ffn_conv_w"][l], dact, name=f"{name}_dact")
    g["ffn_w_up"] = mm(h, dup, ta=True, name=f"{name}_dwup")
    dh = mm(dup, wt["ffn_w_up"][l], tb=True, name=f"{name}_dup")
    dx, dg = rms_bwd(x, wt["ffn_norm"][l][None], dh, add=dout, name=f"{name}_dnorm")
    g["ffn_norm"] = dg[0]
    return dx, g


def _even_fwd(x, i, wt, tabs, name):
    h = rms_fwd(x, wt["attn_norm"][i][None], name=f"{name}_norm")
    proj = mm(h, wt["w_in2"][i], name=f"{name}_in")
    cqn = rms_fwd(proj, wt["cq_norm"][i][None], col=0, name=f"{name}_cqnorm")
    ckvn = rms_fwd(proj, wt["ckv_norm"][i][None], col=1, name=f"{name}_ckvnorm")
    qraw = mm(cqn, wt["w_uq_p"][i], name=f"{name}_uq")
    kv = mm(ckvn, wt["w_ukv_p"][i], name=f"{name}_ukv")
    q, k = qkprep_fwd(qraw, kv, proj, wt["q_gain_p"][i], wt["k_gain_p"][i], tabs, name=f"{name}_qkprep")
    o, lse = attn_fwd(q, k, kv, name=f"{name}_attn")
    conv = sconv_fwd(proj, wt["sconv_w"][i], name=f"{name}_sconv")
    t = mm(o, wt["w_out_a"][i], add=x, name=f"{name}_outa")
    out = mm(conv, wt["w_out_c"][i], add=t, name=f"{name}_outc")
    return out, (x, h, proj, cqn, ckvn, qraw, kv, q, k, o, lse, conv)


def _even_bwd(dout, saved, i, wt, tabs, name):
    x, h, proj, cqn, ckvn, qraw, kv, q, k, o, lse, conv = saved
    g = {}
    do = mm(dout, wt["w_out_a"][i], tb=True, name=f"{name}_douta")
    dconv = mm(dout, wt["w_out_c"][i], tb=True, name=f"{name}_doutc")
    g["w_out_a"] = mm(o, dout, ta=True, name=f"{name}_dwouta")
    g["w_out_c"] = mm(conv, dout, ta=True, name=f"{name}_dwoutc")
    dgates, g["sconv_w"] = sconv_bwd(proj, wt["sconv_w"][i], dconv, name=f"{name}_dsconv")
    dq = attn_bwd_dq(q, k, kv, o, lse, do, name=f"{name}_dattn_q")
    dk, dv = attn_bwd_dkv(q, k, kv, o, lse, do, name=f"{name}_dattn_kv")
    dqraw, dkraw, dkrope, dqg, dkg = qkprep_bwd(qraw, kv, proj, wt["q_gain_p"][i], wt["k_gain_p"][i], tabs, dq, dk,
                                                name=f"{name}_dqkprep")
    g["q_gain"], g["k_gain"] = dqg[0, :QK], dkg[0, :QK]
    dcqn = mm(dqraw, wt["w_uq_p"][i], tb=True, name=f"{name}_duq")
    g["w_uq_p"] = mm(cqn, dqraw, ta=True, name=f"{name}_dwuq")
    dkv = jnp.concatenate([dkraw, dv], axis=1)
    dckvn = mm(dkv, wt["w_ukv_p"][i], tb=True, name=f"{name}_dukv")
    g["w_ukv_p"] = mm(ckvn, dkv, ta=True, name=f"{name}_dwukv")
    dcq, dgq = rms_bwd(proj, wt["cq_norm"][i][None], dcqn, col=0, out_dtype=BF16, name=f"{name}_dcqnorm")
    dckv, dgkv = rms_bwd(proj, wt["ckv_norm"][i][None], dckvn, col=1, out_dtype=BF16, name=f"{name}_dckvnorm")
    g["cq_norm"], g["ckv_norm"] = dgq[0], dgkv[0]
    dproj = jnp.concatenate([dcq, dckv, dgates, dkrope.astype(BF16)], axis=1)
    g["w_in2"] = mm(h, dproj, ta=True, name=f"{name}_dwin")
    dh = mm(dproj, wt["w_in2"][i], tb=True, name=f"{name}_din")
    dx, dg = rms_bwd(x, wt["attn_norm"][i][None], dh, add=dout, name=f"{name}_dnorm")
    g["attn_norm"] = dg[0]
    return dx, g


def _odd_fwd(x, i, wt, name):
    h = rms_fwd(x, wt["ssm_norm"][i][None], name=f"{name}_norm")
    u = mm(h, wt["ssm_w_in"][i], name=f"{name}_in")
    y, st_r, st_i = s5_fwd(u, wt["bbd_r"][i], wt["bbd_i"][i], wt["cbd_r"][i], wt["cbd_i"][i], wt["tab_f"][i],
                           name=f"{name}_scan")
    gl = s5post_fwd(y, u, wt["d_skip"][i][None], name=f"{name}_gelu")
    glu = mm(gl, wt["w_glu"][i], name=f"{name}_glu")
    out = glu_fwd(glu, x, name=f"{name}_gate")
    return out, (x, h, u, y, st_r, st_i, gl, glu)


def _odd_bwd(dout, saved, i, wt, name):
    x, h, u, y, st_r, st_i, gl, glu = saved
    g = {}
    dglu = glu_bwd(glu, dout, name=f"{name}_dgate")
    g["w_glu"] = mm(gl, dglu, ta=True, name=f"{name}_dwglu")
    dgl = mm(dglu, wt["w_glu"][i], tb=True, name=f"{name}_dglu")
    dz, dd = s5post_bwd(y, u, wt["d_skip"][i][None], dgl, name=f"{name}_dgelu")
    g["d_skip"] = dd[0]
    du, g["bbd_r"], g["bbd_i"], g["cbd_r"], g["cbd_i"], g["a"] = s5_bwd(
        u, dz, wt["d_skip"][i][None], st_r, st_i, wt["bbd_r"][i], wt["bbd_i"][i], wt["cbd_r"][i], wt["cbd_i"][i],
        wt["tab_r"][i], name=f"{name}_dscan")
    g["ssm_w_in"] = mm(h, du, ta=True, name=f"{name}_dwin")
    dh = mm(du, wt["ssm_w_in"][i], tb=True, name=f"{name}_din")
    dx, dg = rms_bwd(x, wt["ssm_norm"][i][None], dh, add=dout, name=f"{name}_dnorm")
    g["ssm_norm"] = dg[0]
    return dx, g


def _local_step(x, target, full):
    s = x.shape[0]
    n_even = (DEPTH + 1) // 2
    n_odd = DEPTH // 2
    tabs = _rope_tables(s)
    wt = dict(full)
    wt["w_in2"] = jnp.stack([_w_in_layout(full["mix_w_in"][i]) for i in range(n_even)])
    wt["w_uq_p"] = jnp.stack([_pad_heads_cols(full["w_uq"][i], QK) for i in range(n_even)])
    ukv = full["w_ukv"].reshape(n_even, KVR, HEADS, NOPE + VD)
    wt["w_ukv_p"] = jnp.stack([jnp.concatenate(
        [_pad_heads_cols(ukv[i, :, :, :NOPE].reshape(KVR, HEADS * NOPE), NOPE),
         _pad_heads_cols(ukv[i, :, :, NOPE:].reshape(KVR, HEADS * VD), VD)], axis=1) for i in range(n_even)])
    wt["w_out_a"] = jnp.stack([_pad_heads_cols(full["mix_w_out"][i, :HEADS * VD].T, VD).T for i in range(n_even)])
    wt["w_out_c"] = full["mix_w_out"][:, HEADS * VD:]
    wt["q_gain_p"] = jnp.pad(full["q_gain"], ((0, 0), (0, HP - QK)))[:, None, :]
    wt["k_gain_p"] = jnp.pad(full["k_gain"], ((0, 0), (0, HP - QK)))[:, None, :]

    disc_vjp = []
    for key in ("bbd_r", "bbd_i", "cbd_r", "cbd_i", "tab_f", "tab_r"):
        wt[key] = []
    for i in range(n_odd):
        (ar, ai, bbr, bbi), vjp = jax.vjp(_discretize, full["lambda_re"][i], full["lambda_im"][i], full["log_step"][i],
                                          full["b_re"][i], full["b_im"][i])
        disc_vjp.append(vjp)
        tf, tr = _scan_tables(ar.reshape(-1), ai.reshape(-1))
        wt["tab_f"].append(tf)
        wt["tab_r"].append(tr)
        wt["bbd_r"].append(_b_blockdiag(bbr).astype(BF16))
        wt["bbd_i"].append(_b_blockdiag(bbi).astype(BF16))
        wt["cbd_r"].append(_c_blockdiag(full["c_re"][i]).astype(BF16))
        wt["cbd_i"].append(_c_blockdiag(full["c_im"][i]).astype(BF16))

    saved = []
    for layer in range(DEPTH):
        i = layer // 2
        if layer % 2 == 0:
            x, sm = _even_fwd(x, i, wt, tabs, f"l{layer}_mla")
        else:
            x, sm = _odd_fwd(x, i, wt, f"l{layer}_s5")
        x, sf = _ffn_fwd(x, layer, wt, f"l{layer}_ffn")
        saved.append((sm, sf))
    dx, lslab = loss_head(x, target, name="loss_head")

    grads = {n: [None] * (DEPTH if n.startswith("ffn") else n_even) for n in ORDER}
    for layer in reversed(range(DEPTH)):
        i = layer // 2
        sm, sf = saved[layer]
        dx, g = _ffn_bwd(dx, sf, layer, wt, f"l{layer}_ffn")
        for n in ("ffn_norm", "ffn_w_up", "ffn_conv_w", "ffn_w_down"):
            grads[n][layer] = g[n]
        if layer % 2 == 0:
            dx, g = _even_bwd(dx, sm, i, wt, tabs, f"l{layer}_mla")
            grads["mix_w_in"][i] = _w_in_layout_t(g["w_in2"])
            grads["w_uq"][i] = _unpad_heads_cols(g["w_uq_p"], QK)
            dk_, dv_ = g["w_ukv_p"][:, :HEADS * HP], g["w_ukv_p"][:, HEADS * HP:]
            grads["w_ukv"][i] = jnp.concatenate(
                [dk_.reshape(KVR, HEADS, HP)[:, :, :NOPE], dv_.reshape(KVR, HEADS, HP)[:, :, :VD]],
                axis=2).reshape(KVR, HEADS * (NOPE + VD))
            grads["mix_w_out"][i] = jnp.concatenate(
                [_unpad_heads_cols(g["w_out_a"].T, VD).T, g["w_out_c"]], axis=0)
            for n in ("attn_norm", "cq_norm", "ckv_norm", "q_gain", "k_gain", "sconv_w"):
                grads[n][i] = g[n]
        else:
            dx, g = _odd_bwd(dx, sm, i, wt, f"l{layer}_s5")
            dlr, dli, dls, dbr, dbi = disc_vjp[i]((g["a"][0].reshape(G, P), g["a"][1].reshape(G, P),
                                                    _b_blockdiag_t(g["bbd_r"]), _b_blockdiag_t(g["bbd_i"])))
            grads["lambda_re"][i], grads["lambda_im"][i], grads["log_step"][i] = dlr, dli, dls
            grads["b_re"][i], grads["b_im"][i] = dbr, dbi
            grads["c_re"][i], grads["c_im"][i] = _c_blockdiag_t(g["cbd_r"]), _c_blockdiag_t(g["cbd_i"])
            for n in ("ssm_norm", "ssm_w_in", "d_skip", "w_glu"):
                grads[n][i] = g[n]
    grads = {n: jnp.stack(v) for n, v in grads.items()}
    return jnp.sum(lslab), dx, grads


def kernel(x, attn_norm, mix_w_in, cq_norm, ckv_norm, w_uq, w_ukv, q_gain, k_gain, sconv_w, mix_w_out, ssm_norm, ssm_w_in, lambda_re, lambda_im, log_step, b_re, b_im, c_re, c_im, d_skip, w_glu, ffn_norm, ffn_w_up, ffn_conv_w, ffn_w_down, loss_target, m_attn_norm, m_mix_w_in, m_cq_norm, m_ckv_norm, m_w_uq, m_w_ukv, m_q_gain, m_k_gain, m_sconv_w, m_mix_w_out, m_ssm_norm, m_ssm_w_in, m_lambda_re, m_lambda_im, m_log_step, m_b_re, m_b_im, m_c_re, m_c_im, m_d_skip, m_w_glu, m_ffn_norm, m_ffn_w_up, m_ffn_conv_w, m_ffn_w_down, v_attn_norm, v_mix_w_in, v_cq_norm, v_ckv_norm, v_w_uq, v_w_ukv, v_q_gain, v_k_gain, v_sconv_w, v_mix_w_out, v_ssm_norm, v_ssm_w_in, v_lambda_re, v_lambda_im, v_log_step, v_b_re, v_b_im, v_c_re, v_c_im, v_d_skip, v_w_glu, v_ffn_norm, v_ffn_w_up, v_ffn_conv_w, v_ffn_w_down):
    args = dict(locals())
    w = {n: args[n] for n in ORDER}
    m = {n: args["m_" + n] for n in ORDER}
    v = {n: args["v_" + n] for n in ORDER}
    me = 2 * lax.axis_index("x") + lax.axis_index("y")

    big_shapes = [w[n].shape for n, _ in BIG]
    small_shapes = [w[n].shape for n, _ in SMALL]
    gb = _chip_exchange(_pack([w[n] for n, _ in BIG], BF16), False, "gather_w_bf16")
    gs = _chip_exchange(_pack([w[n] for n, _ in SMALL], F32, 64), False, "gather_w_f32")
    full = {n: w[n] for n in REPL}
    for (n, ax), g in zip(BIG, _unpack(gb, big_shapes)):
        full[n] = _join(g, ax)
    for (n, ax), g in zip(SMALL, _unpack(gs, small_shapes)):
        full[n] = _join(g, ax)

    sq, dx, grads = _local_step(x[0], loss_target[0], full)
    loss = lax.psum(0.5 * sq / D_MODEL, ("x", "y", "c"))

    gbig = jnp.stack([_pack([_split(grads[n], ax)[kk] for n, ax in BIG], BF16) for kk in range(NCHIP)])
    p_big = sum_slots(_chip_exchange(gbig, True, "scatter_g_bf16"), name="sum_g_big")
    rep_names = REPL + [n for n, _ in SMALL]
    gsm = _chip_exchange(_pack([grads[n] for n in rep_names], F32, 64), False, "gather_g_f32")
    p_small_full = sum_slots(gsm, name="sum_g_small")
    q_big, q_small_full = sibling_swap([p_big, p_small_full], "swap_g")

    def local_small(buf):
        parts = _unpack(buf, [grads[n].shape for n in rep_names])
        out = parts[:len(REPL)]
        for (n, ax), part in zip(SMALL, parts[len(REPL):]):
            out.append(lax.dynamic_index_in_dim(_split(part, ax), me, 0, keepdims=False))
        return _pack(out, F32, 64)

    big_names = [n for n, _ in BIG]
    res_big = adamw(p_big, q_big, *[_pack([d[n] for n in big_names], F32) for d in (w, m, v)], name="adamw_big")
    res_small = adamw(local_small(p_small_full), local_small(q_small_full),
                      *[_pack([d[n] for n in rep_names], F32, 64) for d in (w, m, v)], name="adamw_small")
    outs = [{}, {}, {}, {}]
    for kind in range(4):
        for n, a in zip(big_names, _unpack(res_big[kind], big_shapes)):
            outs[kind][n] = a
        for n, a in zip(rep_names, _unpack(res_small[kind], [w[n].shape for n in rep_names])):
            outs[kind][n] = a
    return (loss, dx[None], *[outs[0][n] for n in ORDER], *[outs[1][n] for n in ORDER],
            *[outs[2][n] for n in ORDER], *[outs[3][n] for n in ORDER])
```

```python
import functools
import math

import numpy as np
import jax
import jax.numpy as jnp
from jax import lax
from jax.experimental import pallas as pl
from jax.experimental.pallas import tpu as pltpu

F32, BF16 = jnp.float32, jnp.bfloat16

D_MODEL = 1024
DEPTH = 4
HEADS = 8
NOPE, ROPE, QK, VD = 64, 32, 96, 64
HP = 128
QR, KVR = 256, 256
CONVC = 512
FFN_H = 2816
G, P, GC = 64, 64, 16
NST = G * P
SLAB = 8
LANE = 128
EPS = 1e-6
ROPE_THETA = 10000.0
ADAM_LR, ADAM_B1, ADAM_B2, ADAM_EPS, ADAM_WD, ADAM_STEP = 0.001, 0.9, 0.999, 1e-08, 0.01, 10
VMEM_LIMIT = 48 * 1024 * 1024
MESH = pl.DeviceIdType.MESH
ANY = pl.BlockSpec(memory_space=pl.ANY)


def _cparams(*sem):
    return pltpu.CompilerParams(dimension_semantics=sem, vmem_limit_bytes=VMEM_LIMIT)


def _pick(dim, prefs):
    for p in prefs:
        if dim % p == 0:
            return p
    return dim


def _rows(s):
    return _pick(s, (512, 256, 128, 64, 32, 16, 8))


MM_VMEM_BUDGET = 36 * 1024 * 1024
MM_MAX_TILE_ELEMS = 640 * 1024
HBM_BYTES_PER_US = 3.0e6
STEP_OVERHEAD_US = 0.35


def _lane_tiles(n):
    c = {t for t in range(LANE, min(n, 1536) + 1, LANE) if n % t == 0}
    if n <= 2304 or not c:
        c.add(n)
    return sorted(c, reverse=True)


def _mm_tiles(m, n, k, sa, sb, so):
    best = None
    for tm in [t for t in (1024, 512, 256) if m % t == 0] or [m]:
        for tn in _lane_tiles(n):
            if tm * tn > MM_MAX_TILE_ELEMS:
                continue
            if 2 * (tm * k * sa + k * tn * sb + tm * tn * so) + 4 * tm * tn > MM_VMEM_BUDGET:
                continue
            steps = (m // tm) * (n // tn)
            for inner_n in (True, False):
                moved = (m * k * sa + (m // tm) * k * n * sb) if inner_n else (k * n * sb + (n // tn) * m * k * sa)
                cost = (moved + m * n * so) / HBM_BYTES_PER_US + steps * STEP_OVERHEAD_US
                if best is None or cost < best[0]:
                    best = (cost, tm, tn, inner_n)
    return best[1:]


def mm(a, b, *, ta=False, tb=False, add=None, out_dtype=F32, name):
    if ta:
        kdim, m = a.shape
    else:
        m, kdim = a.shape
    n = b.shape[0] if tb else b.shape[1]
    so = jnp.dtype(out_dtype).itemsize + (0 if add is None else add.dtype.itemsize)
    tm, tn, inner_n = _mm_tiles(m, n, kdim, a.dtype.itemsize, b.dtype.itemsize, so)
    dn = (((0 if ta else 1,), (1 if tb else 0,)), ((), ()))

    def body(*refs):
        if add is None:
            a_ref, b_ref, o_ref = refs
        else:
            a_ref, b_ref, add_ref, o_ref = refs
        r = lax.dot_general(a_ref[...].astype(BF16), b_ref[...].astype(BF16), dn, preferred_element_type=F32)
        if add is not None:
            r = r + add_ref[...].astype(F32)
        o_ref[...] = r.astype(out_dtype)

    ij = (lambda g0, g1: (g0, g1)) if inner_n else (lambda g0, g1: (g1, g0))
    a_spec = (pl.BlockSpec((kdim, tm), lambda g0, g1: (0, ij(g0, g1)[0])) if ta
              else pl.BlockSpec((tm, kdim), lambda g0, g1: (ij(g0, g1)[0], 0)))
    b_spec = (pl.BlockSpec((tn, kdim), lambda g0, g1: (ij(g0, g1)[1], 0)) if tb
              else pl.BlockSpec((kdim, tn), lambda g0, g1: (0, ij(g0, g1)[1])))
    o_spec = pl.BlockSpec((tm, tn), lambda g0, g1: ij(g0, g1))
    ins, specs = [a, b], [a_spec, b_spec]
    if add is not None:
        ins.append(add)
        specs.append(o_spec)
    grid = (m // tm, n // tn) if inner_n else (n // tn, m // tm)
    return pl.pallas_call(
        body, name=name, grid=grid, in_specs=specs, out_specs=o_spec,
        out_shape=jax.ShapeDtypeStruct((m, n), out_dtype),
        compiler_params=_cparams("parallel", "parallel"))(*ins)


def rms_fwd(x, g, *, col=0, out_dtype=BF16, name):
    s = x.shape[0]
    d = g.shape[1]
    tm = _rows(s)

    def body(x_ref, g_ref, o_ref):
        xv = x_ref[...]
        r = lax.rsqrt(jnp.mean(xv * xv, axis=-1, keepdims=True) + EPS)
        o_ref[...] = (xv * r * g_ref[...]).astype(out_dtype)

    return pl.pallas_call(
        body, name=name, grid=(s // tm,),
        in_specs=[pl.BlockSpec((tm, d), lambda i: (i, col)), pl.BlockSpec((1, d), lambda i: (0, 0))],
        out_specs=pl.BlockSpec((tm, d), lambda i: (i, 0)),
        out_shape=jax.ShapeDtypeStruct((s, d), out_dtype),
        compiler_params=_cparams("parallel"))(x, g)


def rms_bwd(x, g, dy, *, col=0, add=None, out_dtype=F32, twin=False, name):
    s = x.shape[0]
    d = g.shape[1]
    tm = _rows(s)

    def body(*refs):
        refs = list(refs)
        dg_ref = refs.pop()
        dxh_ref = refs.pop() if twin else None
        dx_ref = refs.pop()
        add_ref = refs.pop() if add is not None else None
        x_ref, g_ref, dy_ref = refs

        @pl.when(pl.program_id(0) == 0)
        def _():
            dg_ref[...] = jnp.zeros_like(dg_ref)

        xv = x_ref[...]
        dyv = dy_ref[...].astype(F32)
        r = lax.rsqrt(jnp.mean(xv * xv, axis=-1, keepdims=True) + EPS)
        xh = xv * r
        dg_ref[...] += jnp.sum(dyv * xh, axis=0, keepdims=True)
        dxh = dyv * g_ref[...]
        dx = r * (dxh - xh * jnp.mean(dxh * xh, axis=-1, keepdims=True))
        if add is not None:
            dx = dx + add_ref[...]
        dx_ref[...] = dx.astype(out_dtype)
        if twin:
            dxh_ref[...] = dx.astype(BF16)

    row = pl.BlockSpec((tm, d), lambda i: (i, 0))
    vec = pl.BlockSpec((1, d), lambda i: (0, 0))
    ins = [x, g, dy]
    specs = [pl.BlockSpec((tm, d), lambda i: (i, col)), vec, row]
    if add is not None:
        ins.append(add)
        specs.append(row)
    dxs = [jax.ShapeDtypeStruct((s, d), out_dtype)] + ([jax.ShapeDtypeStruct((s, d), BF16)] if twin else [])
    return pl.pallas_call(
        body, name=name, grid=(s // tm,), in_specs=specs,
        out_specs=[row] * len(dxs) + [vec],
        out_shape=dxs + [jax.ShapeDtypeStruct((1, d), F32)],
        compiler_params=_cparams("arbitrary"))(*ins)


def _rope_tables(s):
    inv = 1.0 / (ROPE_THETA ** (jnp.arange(0, ROPE, 2, dtype=F32) / ROPE))
    ang = jnp.arange(s, dtype=F32)[:, None] * inv[None, :]
    cos, sin = jnp.cos(ang), jnp.sin(ang)
    z = lambda w: jnp.zeros((s, w), F32)
    c = jnp.concatenate([jnp.ones((s, NOPE), F32), cos, cos, z(HP - QK)], axis=1)
    s1 = jnp.concatenate([z(NOPE), -sin, z(HP - NOPE - ROPE // 2)], axis=1)
    s2 = jnp.concatenate([z(NOPE + ROPE // 2), sin, z(HP - QK)], axis=1)
    return c, s1, s2


def qkprep_fwd(qraw, kv, proj, qg, kg, tabs, *, name):
    s = qraw.shape[0]
    tm = _rows(s)
    kr_col = (proj.shape[1] - HP) // HP

    def body(q_ref, k_ref, kr_ref, qg_ref, kg_ref, c_ref, s1_ref, s2_ref, qo_ref, ko_ref):
        c, s1, s2 = c_ref[...], s1_ref[...], s2_ref[...]

        def f(xv, gain):
            r = lax.rsqrt(jnp.sum(xv * xv, axis=-1, keepdims=True) * (1.0 / QK) + EPS)
            xn = xv * r * gain
            return xn * c + pltpu.roll(xn, HP - ROPE // 2, 1) * s1 + pltpu.roll(xn, ROPE // 2, 1) * s2

        qo_ref[...] = f(q_ref[...], qg_ref[...]).astype(BF16)
        ko_ref[...] = f(k_ref[...] + kr_ref[...], kg_ref[...]).astype(BF16)

    head = pl.BlockSpec((tm, HP), lambda i, h: (i, h))
    tab = pl.BlockSpec((tm, HP), lambda i, h: (i, 0))
    gain = pl.BlockSpec((1, HP), lambda i, h: (0, 0))
    return pl.pallas_call(
        body, name=name, grid=(s // tm, HEADS),
        in_specs=[head, head, pl.BlockSpec((tm, HP), lambda i, h: (i, kr_col)), gain, gain, tab, tab, tab],
        out_specs=[head, head],
        out_shape=[jax.ShapeDtypeStruct((s, HEADS * HP), BF16)] * 2,
        compiler_params=_cparams("parallel", "parallel"))(qraw, kv, proj, qg, kg, *tabs)


def qkprep_bwd(qraw, kv, proj, qg, kg, tabs, dq, dk, *, name):
    s = qraw.shape[0]
    tm = _rows(s)
    kr_col = (proj.shape[1] - HP) // HP

    def body(q_ref, k_ref, kr_ref, qg_ref, kg_ref, c_ref, s1_ref, s2_ref, dq_ref, dk_ref,
             dqr_ref, dkr_ref, dkrope_ref, dqg_ref, dkg_ref):
        i, h = pl.program_id(0), pl.program_id(1)
        c, s1, s2 = c_ref[...], s1_ref[...], s2_ref[...]

        @pl.when((i == 0) & (h == 0))
        def _():
            dqg_ref[...] = jnp.zeros_like(dqg_ref)
            dkg_ref[...] = jnp.zeros_like(dkg_ref)

        @pl.when(h == 0)
        def _():
            dkrope_ref[...] = jnp.zeros_like(dkrope_ref)

        def f(xv, gain, dout):
            r = lax.rsqrt(jnp.sum(xv * xv, axis=-1, keepdims=True) * (1.0 / QK) + EPS)
            xh = xv * r
            dxn = dout * c + pltpu.roll(dout * s1, ROPE // 2, 1) + pltpu.roll(dout * s2, HP - ROPE // 2, 1)
            dgain = jnp.sum(dxn * xh, axis=0, keepdims=True)
            dxh = dxn * gain
            dx = r * (dxh - xh * (jnp.sum(dxh * xh, axis=-1, keepdims=True) * (1.0 / QK)))
            return dx, dgain

        dxq, dgq = f(q_ref[...], qg_ref[...], dq_ref[...])
        dxk, dgk = f(k_ref[...] + kr_ref[...], kg_ref[...], dk_ref[...])
        dqr_ref[...] = dxq.astype(BF16)
        dkr_ref[...] = dxk.astype(BF16)
        dqg_ref[...] += dgq
        dkg_ref[...] += dgk
        lane = lax.broadcasted_iota(jnp.int32, dxk.shape, 1)
        dkrope_ref[...] += jnp.where((lane >= NOPE) & (lane < QK), dxk, 0.0)

    head = pl.BlockSpec((tm, HP), lambda i, h: (i, h))
    tab = pl.BlockSpec((tm, HP), lambda i, h: (i, 0))
    gain = pl.BlockSpec((1, HP), lambda i, h: (0, 0))
    return pl.pallas_call(
        body, name=name, grid=(s // tm, HEADS),
        in_specs=[head, head, pl.BlockSpec((tm, HP), lambda i, h: (i, kr_col)), gain, gain, tab, tab, tab, head, head],
        out_specs=[head, head, tab, gain, gain],
        out_shape=[jax.ShapeDtypeStruct((s, HEADS * HP), BF16)] * 2
        + [jax.ShapeDtypeStruct((s, HP), F32), jax.ShapeDtypeStruct((1, HP), F32), jax.ShapeDtypeStruct((1, HP), F32)],
        compiler_params=_cparams("arbitrary", "arbitrary"))(qraw, kv, proj, qg, kg, *tabs, dq, dk)


ATT_SCALE = QK ** -0.5
NEG = -1e30


def _att_tile(s):
    return _pick(s, (512, 256, 128))


def _causal(sv, diag):
    r = lax.broadcasted_iota(jnp.int32, sv.shape, 0)
    c = lax.broadcasted_iota(jnp.int32, sv.shape, 1)
    return jnp.where(diag & (c > r), NEG, sv)


NT = (((1,), (1,)), ((), ()))
TN = (((0,), (0,)), ((), ()))


def attn_fwd(q, k, kv, *, name):
    s = q.shape[0]
    t = _att_tile(s)
    nb = s // t

    def body(q_ref, k_ref, v_ref, o_ref, oh_ref, lse_ref, m_s, l_s, acc):
        qb, kb = pl.program_id(1), pl.program_id(2)

        @pl.when(kb == 0)
        def _():
            m_s[...] = jnp.full_like(m_s, NEG)
            l_s[...] = jnp.zeros_like(l_s)
            acc[...] = jnp.zeros_like(acc)

        @pl.when(kb <= qb)
        def _():
            sv = lax.dot_general(q_ref[...], k_ref[...], NT, preferred_element_type=F32) * ATT_SCALE
            sv = _causal(sv, kb == qb)
            m_new = jnp.maximum(m_s[...], jnp.max(sv, axis=-1, keepdims=True))
            alpha = jnp.exp(m_s[...] - m_new)
            p = jnp.exp(sv - m_new)
            l_s[...] = alpha * l_s[...] + jnp.sum(p, axis=-1, keepdims=True)
            acc[...] = alpha * acc[...] + jnp.dot(p.astype(BF16), v_ref[...].astype(BF16), preferred_element_type=F32)
            m_s[...] = m_new

        @pl.when(kb == qb)
        def _():
            ov = acc[...] / l_s[...]
            o_ref[...] = ov
            oh_ref[...] = ov.astype(BF16)
            lse_ref[...] = jnp.broadcast_to(m_s[...] + jnp.log(l_s[...]), lse_ref.shape)

    qs = pl.BlockSpec((t, HP), lambda h, i, j: (i, h))
    ks = pl.BlockSpec((t, HP), lambda h, i, j: (jnp.minimum(i, j), h))
    vs = pl.BlockSpec((t, HP), lambda h, i, j: (jnp.minimum(i, j), HEADS + h))
    return pl.pallas_call(
        body, name=name, grid=(HEADS, nb, nb), in_specs=[qs, ks, vs], out_specs=[qs, qs, qs],
        out_shape=[jax.ShapeDtypeStruct((s, HEADS * HP), F32), jax.ShapeDtypeStruct((s, HEADS * HP), BF16),
                   jax.ShapeDtypeStruct((s, HEADS * HP), F32)],
        scratch_shapes=[pltpu.VMEM((t, 1), F32), pltpu.VMEM((t, 1), F32), pltpu.VMEM((t, HP), F32)],
        compiler_params=_cparams("parallel", "parallel", "arbitrary"))(q, k, kv)


def attn_bwd_dq(q, k, kv, o, lse, do, *, name):
    s = q.shape[0]
    t = _att_tile(s)
    nb = s // t

    def body(q_ref, k_ref, v_ref, o_ref, lse_ref, do_ref, dq_ref, acc):
        qb, kb = pl.program_id(1), pl.program_id(2)

        @pl.when(kb == 0)
        def _():
            acc[...] = jnp.zeros_like(acc)

        @pl.when(kb <= qb)
        def _():
            dov = do_ref[...]
            delta = jnp.sum(dov * o_ref[...], axis=-1, keepdims=True)
            sv = lax.dot_general(q_ref[...], k_ref[...], NT, preferred_element_type=F32) * ATT_SCALE
            sv = _causal(sv, kb == qb)
            p = jnp.exp(sv - lse_ref[...][:, 0:1])
            dp = lax.dot_general(dov.astype(BF16), v_ref[...].astype(BF16), NT, preferred_element_type=F32)
            ds = (p * (dp - delta)).astype(BF16)
            acc[...] += jnp.dot(ds, k_ref[...], preferred_element_type=F32)

        @pl.when(kb == qb)
        def _():
            dq_ref[...] = acc[...] * ATT_SCALE

    qs = pl.BlockSpec((t, HP), lambda h, i, j: (i, h))
    ks = pl.BlockSpec((t, HP), lambda h, i, j: (jnp.minimum(i, j), h))
    vs = pl.BlockSpec((t, HP), lambda h, i, j: (jnp.minimum(i, j), HEADS + h))
    return pl.pallas_call(
        body, name=name, grid=(HEADS, nb, nb), in_specs=[qs, ks, vs, qs, qs, qs], out_specs=qs,
        out_shape=jax.ShapeDtypeStruct((s, HEADS * HP), F32),
        scratch_shapes=[pltpu.VMEM((t, HP), F32)],
        compiler_params=_cparams("parallel", "parallel", "arbitrary"))(q, k, kv, o, lse, do)


def attn_bwd_dkv(q, k, kv, o, lse, do, *, name):
    s = q.shape[0]
    t = _att_tile(s)
    nb = s // t

    def body(q_ref, k_ref, v_ref, o_ref, lse_ref, do_ref, dk_ref, dv_ref, dk_acc, dv_acc):
        kb, qb = pl.program_id(1), pl.program_id(2)

        @pl.when(qb == 0)
        def _():
            dk_acc[...] = jnp.zeros_like(dk_acc)
            dv_acc[...] = jnp.zeros_like(dv_acc)

        @pl.when(qb >= kb)
        def _():
            dov = do_ref[...]
            delta = jnp.sum(dov * o_ref[...], axis=-1, keepdims=True)
            sv = lax.dot_general(q_ref[...], k_ref[...], NT, preferred_element_type=F32) * ATT_SCALE
            sv = _causal(sv, kb == qb)
            p = jnp.exp(sv - lse_ref[...][:, 0:1])
            dob = dov.astype(BF16)
            dp = lax.dot_general(dob, v_ref[...].astype(BF16), NT, preferred_element_type=F32)
            ds = (p * (dp - delta)).astype(BF16)
            dv_acc[...] += lax.dot_general(p.astype(BF16), dob, TN, preferred_element_type=F32)
            dk_acc[...] += lax.dot_general(ds, q_ref[...], TN, preferred_element_type=F32)

        @pl.when(qb == nb - 1)
        def _():
            dk_ref[...] = dk_acc[...] * ATT_SCALE
            dv_ref[...] = dv_acc[...].astype(BF16)

    qs = pl.BlockSpec((t, HP), lambda h, j, i: (jnp.maximum(i, j), h))
    ks = pl.BlockSpec((t, HP), lambda h, j, i: (j, h))
    vs = pl.BlockSpec((t, HP), lambda h, j, i: (j, HEADS + h))
    return pl.pallas_call(
        body, name=name, grid=(HEADS, nb, nb), in_specs=[qs, ks, vs, qs, qs, qs], out_specs=[ks, ks],
        out_shape=[jax.ShapeDtypeStruct((s, HEADS * HP), F32), jax.ShapeDtypeStruct((s, HEADS * HP), BF16)],
        scratch_shapes=[pltpu.VMEM((t, HP), F32), pltpu.VMEM((t, HP), F32)],
        compiler_params=_cparams("parallel", "parallel", "arbitrary"))(q, k, kv, o, lse, do)


HALO = 8
CW = 256


def _conv3(zw, w):
    return w[2:3] * zw + w[1:2] * pltpu.roll(zw, 1, 0) + w[0:1] * pltpu.roll(zw, 2, 0)


def _conv3_t(dc, w):
    n = dc.shape[0]
    return w[2:3] * dc + w[1:2] * pltpu.roll(dc, n - 1, 0) + w[0:1] * pltpu.roll(dc, n - 2, 0)


def _conv3_dw(dc, zw, r):
    z0 = zw[HALO:HALO + r]
    z1 = pltpu.roll(zw, 1, 0)[HALO:HALO + r]
    z2 = pltpu.roll(zw, 2, 0)[HALO:HALO + r]
    return [jnp.sum(dc * z, axis=0, keepdims=True) for z in (z2, z1, z0)]


def _halo_specs(r, colfn):
    rb = r // HALO
    cur = pl.BlockSpec((r, CW), lambda j, i: (i, colfn(j)))
    prev = pl.BlockSpec((HALO, CW), lambda j, i: (jnp.maximum(i * rb - 1, 0), colfn(j)))

    def nxt(nrow_blocks):
        return pl.BlockSpec((HALO, CW), lambda j, i: (jnp.minimum((i + 1) * rb, nrow_blocks * rb - 1), colfn(j)))

    return cur, prev, nxt


def ffnact_fwd(up, w, *, name):
    s, c2 = up.shape
    hh = c2 // 2
    nj = hh // CW
    r = _rows(s)
    nt = s // r

    def body(g_ref, gp_ref, v_ref, vp_ref, wg_ref, wv_ref, o_ref):
        pm = (pl.program_id(1) > 0).astype(F32)
        cg = _conv3(jnp.concatenate([gp_ref[...] * pm, g_ref[...]], axis=0), wg_ref[...])[HALO:]
        cv = _conv3(jnp.concatenate([vp_ref[...] * pm, v_ref[...]], axis=0), wv_ref[...])[HALO:]
        o_ref[...] = (cg * jax.nn.sigmoid(cg) * cv).astype(BF16)

    gcur, gprev, _ = _halo_specs(r, lambda j: j)
    vcur, vprev, _ = _halo_specs(r, lambda j: nj + j)
    wg = pl.BlockSpec((3, CW), lambda j, i: (0, j))
    wv = pl.BlockSpec((3, CW), lambda j, i: (0, nj + j))
    return pl.pallas_call(
        body, name=name, grid=(nj, nt), in_specs=[gcur, gprev, vcur, vprev, wg, wv],
        out_specs=pl.BlockSpec((r, CW), lambda j, i: (i, j)),
        out_shape=jax.ShapeDtypeStruct((s, hh), BF16),
        compiler_params=_cparams("parallel", "parallel"))(up, up, up, up, w, w)


def ffnact_bwd(up, w, dact, *, name):
    s, c2 = up.shape
    hh = c2 // 2
    nj = hh // CW
    r = _rows(s)
    nt = s // r

    def body(g_ref, gp_ref, gn_ref, v_ref, vp_ref, vn_ref, wg_ref, wv_ref, da_ref, dan_ref,
             dg_ref, dv_ref, dwg_ref, dwv_ref):
        i = pl.program_id(1)
        pm = (i > 0).astype(F32)
        nm = (i < nt - 1).astype(F32)

        @pl.when(i == 0)
        def _():
            dwg_ref[...] = jnp.zeros_like(dwg_ref)
            dwv_ref[...] = jnp.zeros_like(dwv_ref)

        wg, wv = wg_ref[...], wv_ref[...]
        zg = jnp.concatenate([gp_ref[...] * pm, g_ref[...], gn_ref[...]], axis=0)
        zv = jnp.concatenate([vp_ref[...] * pm, v_ref[...], vn_ref[...]], axis=0)
        cg = _conv3(zg, wg)[HALO:]
        cv = _conv3(zv, wv)[HALO:]
        da = jnp.concatenate([da_ref[...], dan_ref[...] * nm], axis=0)
        sg = jax.nn.sigmoid(cg)
        dcg = da * cv * (sg * (1.0 + cg * (1.0 - sg)))
        dcv = da * (cg * sg)
        dg_ref[...] = _conv3_t(dcg, wg)[:r].astype(BF16)
        dv_ref[...] = _conv3_t(dcv, wv)[:r].astype(BF16)
        for kk, (a, b) in enumerate(zip(_conv3_dw(dcg[:r], zg, r), _conv3_dw(dcv[:r], zv, r))):
            dwg_ref[kk:kk + 1, :] += a
            dwv_ref[kk:kk + 1, :] += b

    gcur, gprev, gnext = _halo_specs(r, lambda j: j)
    vcur, vprev, vnext = _halo_specs(r, lambda j: nj + j)
    acur, _, anext = _halo_specs(r, lambda j: j)
    wg = pl.BlockSpec((3, CW), lambda j, i: (0, j))
    wv = pl.BlockSpec((3, CW), lambda j, i: (0, nj + j))
    dupg, dupv, dwg, dwv = pl.pallas_call(
        body, name=name, grid=(nj, nt),
        in_specs=[gcur, gprev, gnext(nt), vcur, vprev, vnext(nt), wg, wv, acur, anext(nt)],
        out_specs=[acur, acur, wg, wg],
        out_shape=[jax.ShapeDtypeStruct((s, hh), BF16), jax.ShapeDtypeStruct((s, hh), BF16),
                   jax.ShapeDtypeStruct((3, hh), F32), jax.ShapeDtypeStruct((3, hh), F32)],
        compiler_params=_cparams("parallel", "arbitrary"))(up, up, up, up, up, up, w, w, dact, dact)
    return jnp.concatenate([dupg, dupv], axis=1), jnp.concatenate([dwg, dwv], axis=1)


def sconv_fwd(proj, w, *, name):
    s = proj.shape[0]
    nj = CONVC // CW
    r = _rows(s)
    nt = s // r

    def body(b_ref, c_ref, cp_ref, x_ref, xp_ref, w_ref, o_ref):
        pm = (pl.program_id(1) > 0).astype(F32)
        zw = jnp.concatenate([cp_ref[...] * xp_ref[...] * pm, c_ref[...] * x_ref[...]], axis=0)
        o_ref[...] = (b_ref[...] * _conv3(zw, w_ref[...])[HALO:]).astype(BF16)

    bcur, _, _ = _halo_specs(r, lambda j: (QR + KVR) // CW + j)
    ccur, cprev, _ = _halo_specs(r, lambda j: (QR + KVR + CONVC) // CW + j)
    xcur, xprev, _ = _halo_specs(r, lambda j: (QR + KVR + 2 * CONVC) // CW + j)
    ws = pl.BlockSpec((3, CW), lambda j, i: (0, j))
    return pl.pallas_call(
        body, name=name, grid=(nj, nt), in_specs=[bcur, ccur, cprev, xcur, xprev, ws],
        out_specs=pl.BlockSpec((r, CW), lambda j, i: (i, j)),
        out_shape=jax.ShapeDtypeStruct((s, CONVC), BF16),
        compiler_params=_cparams("parallel", "parallel"))(proj, proj, proj, proj, proj, w)


def sconv_bwd(proj, w, dy, *, name):
    s = proj.shape[0]
    nj = CONVC // CW
    r = _rows(s)
    nt = s // r

    def body(b_ref, bn_ref, c_ref, cp_ref, x_ref, xp_ref, w_ref, dy_ref, dyn_ref, db_ref, dc_ref, dx_ref, dw_ref):
        i = pl.program_id(1)
        pm = (i > 0).astype(F32)
        nm = (i < nt - 1).astype(F32)

        @pl.when(i == 0)
        def _():
            dw_ref[...] = jnp.zeros_like(dw_ref)

        wv = w_ref[...]
        zw = jnp.concatenate([cp_ref[...] * xp_ref[...] * pm, c_ref[...] * x_ref[...]], axis=0)
        conv = _conv3(zw, wv)[HALO:]
        dyv = dy_ref[...]
        db_ref[...] = (dyv * conv).astype(BF16)
        dconv = jnp.concatenate([dyv * b_ref[...], dyn_ref[...] * bn_ref[...] * nm], axis=0)
        dz = _conv3_t(dconv, wv)[:r]
        dc_ref[...] = (dz * x_ref[...]).astype(BF16)
        dx_ref[...] = (dz * c_ref[...]).astype(BF16)
        for kk, a in enumerate(_conv3_dw(dconv[:r], zw, r)):
            dw_ref[kk:kk + 1, :] += a

    bcur, _, bnext = _halo_specs(r, lambda j: (QR + KVR) // CW + j)
    ccur, cprev, _ = _halo_specs(r, lambda j: (QR + KVR + CONVC) // CW + j)
    xcur, xprev, _ = _halo_specs(r, lambda j: (QR + KVR + 2 * CONVC) // CW + j)
    ycur, _, ynext = _halo_specs(r, lambda j: j)
    ws = pl.BlockSpec((3, CW), lambda j, i: (0, j))
    out = pl.BlockSpec((r, CW), lambda j, i: (i, j))
    db, dc, dx, dw = pl.pallas_call(
        body, name=name, grid=(nj, nt),
        in_specs=[bcur, bnext(nt), ccur, cprev, xcur, xprev, ws, ycur, ynext(nt)],
        out_specs=[out, out, out, ws],
        out_shape=[jax.ShapeDtypeStruct((s, CONVC), BF16)] * 3 + [jax.ShapeDtypeStruct((3, CONVC), F32)],
        compiler_params=_cparams("parallel", "arbitrary"))(proj, proj, proj, proj, proj, proj, w, dy, dy)
    return jnp.concatenate([db, dc, dx], axis=1), dw


SW = 512
NJ = NST // SW


def _scan_tables(ar, ai):
    def cmul(x, y):
        return x[0] * y[0] - x[1] * y[1], x[0] * y[1] + x[1] * y[0]

    def build(a, reverse):
        pw = [a]
        for _ in range(SLAB - 1):
            pw.append(cmul(pw[-1], a))
        row = jnp.arange(SLAB)[:, None]
        tabs = []
        for kk in (1, 2, 4):
            mask = ((row < SLAB - kk) if reverse else (row >= kk)).astype(F32)
            tabs += [mask * pw[kk - 1][0][None, :], mask * pw[kk - 1][1][None, :]]
        order = list(range(SLAB - 1, -1, -1)) if reverse else list(range(SLAB))
        tabs += [jnp.stack([pw[o][0] for o in order]), jnp.stack([pw[o][1] for o in order])]
        return jnp.stack(tabs)

    return build((ar, ai), False), build((ar, -ai), True)


def _slab_scan(xr, xi, tabs, cr, ci, reverse):
    for n, kk in enumerate((1, 2, 4)):
        sh = SLAB - kk if reverse else kk
        tr, ti = tabs[2 * n], tabs[2 * n + 1]
        sr, si = pltpu.roll(xr, sh, 0), pltpu.roll(xi, sh, 0)
        xr, xi = xr + tr * sr - ti * si, xi + tr * si + ti * sr
    tr, ti = tabs[6], tabs[7]
    return xr + tr * cr - ti * ci, xi + tr * ci + ti * cr


def s5_fwd(u, bbd_r, bbd_i, cbd_r, cbd_i, tab, *, name):
    s = u.shape[0]
    tbk = _rows(s)
    nt = s // tbk
    nsl = tbk // SLAB

    def body(u_ref, br_ref, bi_ref, cr_ref, ci_ref, tab_ref, y_ref, sr_ref, si_ref, bur, bui, carry):
        @pl.when(pl.program_id(1) == 0)
        def _():
            carry[...] = jnp.zeros_like(carry)

        ub = u_ref[...].astype(BF16)
        bur[...] = jnp.dot(ub, br_ref[0], preferred_element_type=F32)
        bui[...] = jnp.dot(ub, bi_ref[0], preferred_element_type=F32)
        tabs = [tab_ref[n] for n in range(8)]

        def slab(n, c):
            r0 = pl.multiple_of(n * SLAB, SLAB)
            sr, si = _slab_scan(bur[pl.ds(r0, SLAB), :], bui[pl.ds(r0, SLAB), :], tabs, c[0], c[1], False)
            sr_ref[pl.ds(r0, SLAB), :] = sr
            si_ref[pl.ds(r0, SLAB), :] = si
            return (jnp.broadcast_to(sr[SLAB - 1:SLAB], sr.shape), jnp.broadcast_to(si[SLAB - 1:SLAB], si.shape))

        cr, ci = lax.fori_loop(0, nsl, slab, (carry[0], carry[1]))
        carry[0] = cr
        carry[1] = ci
        y_ref[...] = (jnp.dot(sr_ref[...].astype(BF16), cr_ref[0], preferred_element_type=F32)
                      - jnp.dot(si_ref[...].astype(BF16), ci_ref[0], preferred_element_type=F32))

    us = pl.BlockSpec((tbk, LANE), lambda j, t: (t, j))
    bs = pl.BlockSpec((1, LANE, SW), lambda j, t: (j, 0, 0))
    cs = pl.BlockSpec((1, SW, LANE), lambda j, t: (j, 0, 0))
    ts = pl.BlockSpec((8, SLAB, SW), lambda j, t: (0, 0, j))
    ss = pl.BlockSpec((tbk, SW), lambda j, t: (t, j))
    return pl.pallas_call(
        body, name=name, grid=(NJ, nt), in_specs=[us, bs, bs, cs, cs, ts], out_specs=[us, ss, ss],
        out_shape=[jax.ShapeDtypeStruct((s, D_MODEL), F32), jax.ShapeDtypeStruct((s, NST), F32),
                   jax.ShapeDtypeStruct((s, NST), F32)],
        scratch_shapes=[pltpu.VMEM((tbk, SW), F32), pltpu.VMEM((tbk, SW), F32), pltpu.VMEM((2, SLAB, SW), F32)],
        compiler_params=_cparams("parallel", "arbitrary"))(u, bbd_r, bbd_i, cbd_r, cbd_i, tab)


def s5_bwd(u, dy, dskip, st_r, st_i, bbd_r, bbd_i, cbd_r, cbd_i, tabrev, *, name):
    s = u.shape[0]
    tbk = _rows(s)
    nt = s // tbk
    nsl = tbk // SLAB
    rbk = tbk // SLAB

    def body(u_ref, dy_ref, d_ref, sr_ref, si_ref, pr_ref, pi_ref, br_ref, bi_ref, cr_ref, ci_ref, tab_ref,
             du_ref, dbr_ref, dbi_ref, dcr_ref, dci_ref, da_ref, lam_r, lam_i, carry):
        t = pl.program_id(1)

        @pl.when(t == 0)
        def _():
            carry[...] = jnp.zeros_like(carry)
            dbr_ref[...] = jnp.zeros_like(dbr_ref)
            dbi_ref[...] = jnp.zeros_like(dbi_ref)
            dcr_ref[...] = jnp.zeros_like(dcr_ref)
            dci_ref[...] = jnp.zeros_like(dci_ref)
            da_ref[...] = jnp.zeros_like(da_ref)

        dyv = dy_ref[...]
        dyh = dyv.astype(BF16)
        lam_r[...] = lax.dot_general(dyh, cr_ref[0], NT, preferred_element_type=F32)
        lam_i[...] = -lax.dot_general(dyh, ci_ref[0], NT, preferred_element_type=F32)
        tabs = [tab_ref[n] for n in range(8)]

        def slab(n, c):
            r0 = pl.multiple_of((nsl - 1 - n) * SLAB, SLAB)
            lr, li = _slab_scan(lam_r[pl.ds(r0, SLAB), :], lam_i[pl.ds(r0, SLAB), :], tabs, c[0], c[1], True)
            lam_r[pl.ds(r0, SLAB), :] = lr
            lam_i[pl.ds(r0, SLAB), :] = li
            return (jnp.broadcast_to(lr[0:1], lr.shape), jnp.broadcast_to(li[0:1], li.shape))

        cr, ci = lax.fori_loop(0, nsl, slab, (carry[0], carry[1]))
        carry[0] = cr
        carry[1] = ci
        lr, li = lam_r[...], lam_i[...]
        lrh, lih = lr.astype(BF16), li.astype(BF16)
        du = (dyv * d_ref[...] + lax.dot_general(lrh, br_ref[0], NT, preferred_element_type=F32)
              + lax.dot_general(lih, bi_ref[0], NT, preferred_element_type=F32))
        du_ref[...] = du.astype(BF16)
        ub = u_ref[...].astype(BF16)
        dbr_ref[0] += lax.dot_general(ub, lrh, TN, preferred_element_type=F32)
        dbi_ref[0] += lax.dot_general(ub, lih, TN, preferred_element_type=F32)
        srv, siv = sr_ref[...], si_ref[...]
        dcr_ref[0] += lax.dot_general(srv.astype(BF16), dyh, TN, preferred_element_type=F32)
        dci_ref[0] -= lax.dot_general(siv.astype(BF16), dyh, TN, preferred_element_type=F32)
        first = lax.broadcasted_iota(jnp.int32, srv.shape, 0) == 0
        pm = (t < nt - 1).astype(F32)
        spr = jnp.where(first, pr_ref[SLAB - 1:SLAB, :] * pm, pltpu.roll(srv, 1, 0))
        spi = jnp.where(first, pi_ref[SLAB - 1:SLAB, :] * pm, pltpu.roll(siv, 1, 0))
        da_ref[0:1, :] += jnp.sum(lr * spr + li * spi, axis=0, keepdims=True)
        da_ref[1:2, :] += jnp.sum(li * spr - lr * spi, axis=0, keepdims=True)

    rv = lambda t: nt - 1 - t
    us = pl.BlockSpec((tbk, LANE), lambda j, t: (rv(t), j))
    ds = pl.BlockSpec((1, LANE), lambda j, t: (0, j))
    ss = pl.BlockSpec((tbk, SW), lambda j, t: (rv(t), j))
    ps = pl.BlockSpec((SLAB, SW), lambda j, t: (jnp.maximum(rv(t) * rbk - 1, 0), j))
    bs = pl.BlockSpec((1, LANE, SW), lambda j, t: (j, 0, 0))
    cs = pl.BlockSpec((1, SW, LANE), lambda j, t: (j, 0, 0))
    ts = pl.BlockSpec((8, SLAB, SW), lambda j, t: (0, 0, j))
    das = pl.BlockSpec((2, SW), lambda j, t: (0, j))
    return pl.pallas_call(
        body, name=name, grid=(NJ, nt),
        in_specs=[us, us, ds, ss, ss, ps, ps, bs, bs, cs, cs, ts],
        out_specs=[us, bs, bs, cs, cs, das],
        out_shape=[jax.ShapeDtypeStruct((s, D_MODEL), BF16),
                   jax.ShapeDtypeStruct((NJ, LANE, SW), F32), jax.ShapeDtypeStruct((NJ, LANE, SW), F32),
                   jax.ShapeDtypeStruct((NJ, SW, LANE), F32), jax.ShapeDtypeStruct((NJ, SW, LANE), F32),
                   jax.ShapeDtypeStruct((2, NST), F32)],
        scratch_shapes=[pltpu.VMEM((tbk, SW), F32), pltpu.VMEM((tbk, SW), F32), pltpu.VMEM((2, SLAB, SW), F32)],
        compiler_params=_cparams("parallel", "arbitrary"))(
            u, dy, dskip, st_r, st_i, st_r, st_i, bbd_r, bbd_i, cbd_r, cbd_i, tabrev)


GELU_C = math.sqrt(2.0 / math.pi)
GELU_A = 0.044715


def s5post_fwd(y, u, dskip, *, name):
    s = y.shape[0]
    tm = _rows(s)

    def body(y_ref, u_ref, d_ref, o_ref):
        z = y_ref[...] + d_ref[...] * u_ref[...]
        o_ref[...] = (0.5 * z * (1.0 + jnp.tanh(GELU_C * (z + GELU_A * z * z * z)))).astype(BF16)

    row = pl.BlockSpec((tm, D_MODEL), lambda i: (i, 0))
    vec = pl.BlockSpec((1, D_MODEL), lambda i: (0, 0))
    return pl.pallas_call(body, name=name, grid=(s // tm,), in_specs=[row, row, vec], out_specs=row,
                          out_shape=jax.ShapeDtypeStruct((s, D_MODEL), BF16),
                          compiler_params=_cparams("parallel"))(y, u, dskip)


def s5post_bwd(y, u, dskip, dg, *, name):
    s = y.shape[0]
    tm = _rows(s)

    def body(y_ref, u_ref, d_ref, dg_ref, dz_ref, dd_ref):
        @pl.when(pl.program_id(0) == 0)
        def _():
            dd_ref[...] = jnp.zeros_like(dd_ref)

        uv = u_ref[...]
        z = y_ref[...] + d_ref[...] * uv
        th = jnp.tanh(GELU_C * (z + GELU_A * z * z * z))
        dgelu = 0.5 * (1.0 + th) + 0.5 * z * (1.0 - th * th) * (GELU_C * (1.0 + 3.0 * GELU_A * z * z))
        dz = dg_ref[...] * dgelu
        dz_ref[...] = dz
        dd_ref[...] += jnp.sum(dz * uv, axis=0, keepdims=True)

    row = pl.BlockSpec((tm, D_MODEL), lambda i: (i, 0))
    vec = pl.BlockSpec((1, D_MODEL), lambda i: (0, 0))
    return pl.pallas_call(body, name=name, grid=(s // tm,), in_specs=[row, row, vec, row], out_specs=[row, vec],
                          out_shape=[jax.ShapeDtypeStruct((s, D_MODEL), F32), jax.ShapeDtypeStruct((1, D_MODEL), F32)],
                          compiler_params=_cparams("arbitrary"))(y, u, dskip, dg)


def glu_fwd(glu, x, *, name):
    s = x.shape[0]
    tm = _rows(s)

    def body(a_ref, b_ref, x_ref, o_ref):
        o_ref[...] = x_ref[...] + a_ref[...] * jax.nn.sigmoid(b_ref[...])

    row = pl.BlockSpec((tm, D_MODEL), lambda i: (i, 0))
    return pl.pallas_call(body, name=name, grid=(s // tm,),
                          in_specs=[row, pl.BlockSpec((tm, D_MODEL), lambda i: (i, 1)), row], out_specs=row,
                          out_shape=jax.ShapeDtypeStruct((s, D_MODEL), F32),
                          compiler_params=_cparams("parallel"))(glu, glu, x)


def glu_bwd(glu, dx, *, name):
    s = dx.shape[0]
    tm = _rows(s)

    def body(a_ref, b_ref, dx_ref, o_ref):
        sg = jax.nn.sigmoid(b_ref[...])
        dxv = dx_ref[...]
        o_ref[:, :D_MODEL] = (dxv * sg).astype(BF16)
        o_ref[:, D_MODEL:] = (dxv * a_ref[...] * sg * (1.0 - sg)).astype(BF16)

    row = pl.BlockSpec((tm, D_MODEL), lambda i: (i, 0))
    return pl.pallas_call(body, name=name, grid=(s // tm,),
                          in_specs=[row, pl.BlockSpec((tm, D_MODEL), lambda i: (i, 1)), row],
                          out_specs=pl.BlockSpec((tm, 2 * D_MODEL), lambda i: (i, 0)),
                          out_shape=jax.ShapeDtypeStruct((s, 2 * D_MODEL), BF16),
                          compiler_params=_cparams("parallel"))(glu, glu, dx)


def loss_head(y, target, *, name):
    s = y.shape[0]
    tm = _rows(s)

    def body(y_ref, t_ref, dy_ref, dyh_ref, l_ref):
        @pl.when(pl.program_id(0) == 0)
        def _():
            l_ref[...] = jnp.zeros_like(l_ref)

        e = y_ref[...] - t_ref[...]
        dy_ref[...] = e * (1.0 / D_MODEL)
        dyh_ref[...] = (e * (1.0 / D_MODEL)).astype(BF16)
        e2 = jnp.sum((e * e).reshape(tm // 8, 8, D_MODEL), axis=0)
        acc = e2[:, 0:LANE]
        for kk in range(1, D_MODEL // LANE):
            acc = acc + e2[:, kk * LANE:(kk + 1) * LANE]
        l_ref[...] += acc

    row = pl.BlockSpec((tm, D_MODEL), lambda i: (i, 0))
    return pl.pallas_call(body, name=name, grid=(s // tm,), in_specs=[row, row],
                          out_specs=[row, row, pl.BlockSpec((8, LANE), lambda i: (0, 0))],
                          out_shape=[jax.ShapeDtypeStruct((s, D_MODEL), F32), jax.ShapeDtypeStruct((s, D_MODEL), BF16),
                                     jax.ShapeDtypeStruct((8, LANE), F32)],
                          compiler_params=_cparams("arbitrary"))(y, target)


PACKW = 1024
NCHIP = 4


def _mesh_pos():
    return lax.axis_index("x"), lax.axis_index("y"), lax.axis_index("c")


def _chip_exchange(bufs, scatter, name):
    n = len(bufs)
    shapes = [b.shape[1:] if scatter else b.shape for b in bufs]

    def body(*refs):
        ins, outs = refs[:n], refs[n:2 * n]
        send_sems, recv_sems, local_sems = refs[2 * n:]
        x, y, c = _mesh_pos()
        me = 2 * x + y
        peers = [(1 - x, y), (x, 1 - y), (1 - x, 1 - y)]

        def copy(a, j, px, py, dst_slot):
            src = ins[a].at[2 * px + py] if scatter else ins[a]
            return pltpu.make_async_remote_copy(src_ref=src, dst_ref=outs[a].at[dst_slot],
                                                send_sem=send_sems.at[3 * a + j], recv_sem=recv_sems.at[3 * a + j],
                                                device_id=(px, py, c), device_id_type=MESH)

        mine = [pltpu.make_async_copy(ins[a].at[me] if scatter else ins[a], outs[a].at[me], local_sems.at[a])
                for a in range(n)]
        sends = [copy(a, j, px, py, me) for a in range(n) for j, (px, py) in enumerate(peers)]
        for cp in mine + sends:
            cp.start()
        for a in range(n):
            for j, (px, py) in enumerate(peers):
                copy(a, j, px, py, 2 * px + py).wait_recv()
        for cp in sends:
            cp.wait_send()
        for cp in mine:
            cp.wait()

    return pl.pallas_call(
        body, name=name, in_specs=[ANY] * n, out_specs=[ANY] * n,
        out_shape=[jax.ShapeDtypeStruct((NCHIP,) + tuple(shp), b.dtype) for shp, b in zip(shapes, bufs)],
        scratch_shapes=[pltpu.SemaphoreType.DMA((3 * n,)), pltpu.SemaphoreType.DMA((3 * n,)),
                        pltpu.SemaphoreType.DMA((n,))],
    )(*bufs)


def sibling_swap(bufs, name):
    n = len(bufs)

    def body(*refs):
        ins, outs, send_sems, recv_sems = refs[:n], refs[n:2 * n], refs[2 * n], refs[2 * n + 1]
        x, y, c = _mesh_pos()
        cps = [pltpu.make_async_remote_copy(src_ref=ins[k], dst_ref=outs[k], send_sem=send_sems.at[k],
                                            recv_sem=recv_sems.at[k], device_id=(x, y, 1 - c), device_id_type=MESH)
               for k in range(n)]
        for cp in cps:
            cp.start()
        for cp in cps:
            cp.wait()

    return pl.pallas_call(
        body, name=name, in_specs=[ANY] * n, out_specs=[ANY] * n,
        out_shape=[jax.ShapeDtypeStruct(b.shape, b.dtype) for b in bufs],
        scratch_shapes=[pltpu.SemaphoreType.DMA((n,)), pltpu.SemaphoreType.DMA((n,))],
    )(*bufs)


EW_VMEM_BUDGET = 20 * 1024 * 1024


def _ew_rows(rows, w, bytes_per_elem):
    wpad = -(-w // LANE) * LANE
    for t in (1024, 512, 256, 128, 64, 32, 16, 8):
        if rows % t == 0 and 2 * t * wpad * bytes_per_elem <= EW_VMEM_BUDGET:
            return t
    return rows


def sum_slots(buf, *, name):
    _, rows, w = buf.shape
    tm = _ew_rows(rows, w, NCHIP * buf.dtype.itemsize + 4)

    def body(b_ref, o_ref):
        acc = b_ref[0].astype(F32)
        for kk in range(1, NCHIP):
            acc = acc + b_ref[kk].astype(F32)
        o_ref[...] = acc

    return pl.pallas_call(body, name=name, grid=(rows // tm,),
                          in_specs=[pl.BlockSpec((NCHIP, tm, w), lambda i: (0, i, 0))],
                          out_specs=pl.BlockSpec((tm, w), lambda i: (i, 0)),
                          out_shape=jax.ShapeDtypeStruct((rows, w), F32),
                          compiler_params=_cparams("parallel"))(buf)


def adamw(p_mine, p_other, w, m, v, *, name):
    rows, wd = w.shape
    tm = _ew_rows(rows, wd, 9 * 4)
    c1 = 1.0 - ADAM_B1 ** ADAM_STEP
    c2 = 1.0 - ADAM_B2 ** ADAM_STEP

    def body(a_ref, b_ref, w_ref, m_ref, v_ref, g_ref, d_ref, nm_ref, nv_ref):
        g = a_ref[...] + b_ref[...]
        nm = ADAM_B1 * m_ref[...] + (1.0 - ADAM_B1) * g
        nv = ADAM_B2 * v_ref[...] + (1.0 - ADAM_B2) * (g * g)
        g_ref[...] = g
        nm_ref[...] = nm
        nv_ref[...] = nv
        d_ref[...] = -ADAM_LR * ((nm / c1) / (jnp.sqrt(nv / c2) + ADAM_EPS) + ADAM_WD * w_ref[...])

    row = pl.BlockSpec((tm, wd), lambda i: (i, 0))
    return pl.pallas_call(body, name=name, grid=(rows // tm,), in_specs=[row] * 5, out_specs=[row] * 4,
                          out_shape=[jax.ShapeDtypeStruct((rows, wd), F32)] * 4,
                          compiler_params=_cparams("parallel"))(p_mine, p_other, w, m, v)


def _rows2d(a, lead=0):
    return a.reshape(a.shape[:lead] + (-1, a.shape[-1]))


BIG = [("mix_w_in", 2), ("w_uq", 2), ("w_ukv", 2), ("mix_w_out", 1), ("ssm_w_in", 1), ("w_glu", 2),
       ("ffn_w_up", 2), ("ffn_w_down", 1)]
SMALL = [("sconv_w", 2), ("ssm_norm", 1), ("d_skip", 1), ("ffn_conv_w", 2)]
REPL = ["attn_norm", "cq_norm", "ckv_norm", "q_gain", "k_gain", "lambda_re", "lambda_im", "log_step",
        "b_re", "b_im", "c_re", "c_im", "ffn_norm"]
ORDER = ["attn_norm", "mix_w_in", "cq_norm", "ckv_norm", "w_uq", "w_ukv", "q_gain", "k_gain", "sconv_w", "mix_w_out",
         "ssm_norm", "ssm_w_in", "lambda_re", "lambda_im", "log_step", "b_re", "b_im", "c_re", "c_im", "d_skip",
         "w_glu", "ffn_norm", "ffn_w_up", "ffn_conv_w", "ffn_w_down"]


def _join(g, axis):
    return jnp.concatenate([g[k] for k in range(NCHIP)], axis=axis)


def _split(full, axis):
    return jnp.stack(jnp.split(full, NCHIP, axis=axis))


def _discretize(lr, li, ls, b_re, b_im):
    dt = jnp.exp(ls)[:, None]
    mag = jnp.exp(lr * dt)
    ar, ai = mag * jnp.cos(li * dt), mag * jnp.sin(li * dt)
    nr, ni = ar - 1.0, ai
    den = lr * lr + li * li
    zr, zi = (nr * lr + ni * li) / den, (ni * lr - nr * li) / den
    bbar_r = zr[..., None] * b_re - zi[..., None] * b_im
    bbar_i = zr[..., None] * b_im + zi[..., None] * b_re
    return ar, ai, bbar_r, bbar_i


def _b_blockdiag(bbar):
    gl = G // NJ
    bb = bbar.reshape(NJ, gl, P, GC).transpose(0, 1, 3, 2)
    return jnp.einsum("jgcp,gh->jgchp", bb, jnp.eye(gl, dtype=bbar.dtype)).reshape(NJ, gl * GC, gl * P)


def _b_blockdiag_t(dbd):
    gl = G // NJ
    d = jnp.einsum("jgchp,gh->jgcp", dbd.reshape(NJ, gl, GC, gl, P), jnp.eye(gl, dtype=dbd.dtype))
    return d.transpose(0, 1, 3, 2).reshape(G, P, GC)


def _c_blockdiag(cmat):
    gl = G // NJ
    cc = cmat.reshape(NJ, gl, GC, P).transpose(0, 1, 3, 2)
    return jnp.einsum("jgpc,gh->jgphc", cc, jnp.eye(gl, dtype=cmat.dtype)).reshape(NJ, gl * P, gl * GC)


def _c_blockdiag_t(dbd):
    gl = G // NJ
    d = jnp.einsum("jgphc,gh->jgpc", dbd.reshape(NJ, gl, P, gl, GC), jnp.eye(gl, dtype=dbd.dtype))
    return d.transpose(0, 1, 3, 2).reshape(G, GC, P)


def _pad_heads_cols(w, width):
    r = w.shape[0]
    return jnp.pad(w.reshape(r, HEADS, width), ((0, 0), (0, 0), (0, HP - width))).reshape(r, HEADS * HP)


def _unpad_heads_cols(w, width):
    r = w.shape[0]
    return w.reshape(r, HEADS, HP)[:, :, :width].reshape(r, HEADS * width)


W_IN_SPLIT = (QR + KVR, QR + KVR + ROPE)


def _w_in_layout(w):
    a, b = W_IN_SPLIT
    kr = jnp.pad(w[:, a:b], ((0, 0), (NOPE, HP - QK)))
    return jnp.concatenate([w[:, :a], w[:, b:], kr], axis=1)


def _w_in_layout_t(dw):
    a, b = W_IN_SPLIT
    n = dw.shape[1] - HP
    return jnp.concatenate([dw[:, :a], dw[:, n + NOPE:n + QK], dw[:, a:n]], axis=1)


def _ffn_fwd(x, l, wt, name):
    h = rms_fwd(x, wt["ffn_norm"][l][None], name=f"{name}_norm")
    up = mm(h, wt["ffn_w_up"][l], name=f"{name}_up")
    act = ffnact_fwd(up, wt["ffn_conv_w"][l], name=f"{name}_act")
    out = mm(act, wt["ffn_w_down"][l], add=x, name=f"{name}_down")
    return out, (x, h, up, act)


def _ffn_bwd(dout, douth, saved, l, wt, name):
    x, h, up, act = saved
    g = {}
    dact = mm(douth, wt["ffn_w_down"][l], tb=True, name=f"{name}_ddown")
    g["ffn_w_down"] = mm(act, douth, ta=True, out_dtype=BF16, name=f"{name}_dwdown")
    dup, g["ffn_conv_w"] = ffnact_bwd(up, wt["ffn_conv_w"][l], dact, name=f"{name}_dact")
    g["ffn_w_up"] = mm(h, dup, ta=True, out_dtype=BF16, name=f"{name}_dwup")
    dh = mm(dup, wt["ffn_w_up"][l], tb=True, name=f"{name}_dup")
    dx, dxh, dg = rms_bwd(x, wt["ffn_norm"][l][None], dh, add=dout, twin=True, name=f"{name}_dnorm")
    g["ffn_norm"] = dg[0]
    return dx, dxh, g


def _even_fwd(x, i, wt, tabs, name):
    h = rms_fwd(x, wt["attn_norm"][i][None], name=f"{name}_norm")
    proj = mm(h, wt["w_in2"][i], name=f"{name}_in")
    cqn = rms_fwd(proj, wt["cq_norm"][i][None], col=0, name=f"{name}_cqnorm")
    ckvn = rms_fwd(proj, wt["ckv_norm"][i][None], col=1, name=f"{name}_ckvnorm")
    qraw = mm(cqn, wt["w_uq_p"][i], name=f"{name}_uq")
    kv = mm(ckvn, wt["w_ukv_p"][i], name=f"{name}_ukv")
    q, k = qkprep_fwd(qraw, kv, proj, wt["q_gain_p"][i], wt["k_gain_p"][i], tabs, name=f"{name}_qkprep")
    o, oh, lse = attn_fwd(q, k, kv, name=f"{name}_attn")
    conv = sconv_fwd(proj, wt["sconv_w"][i], name=f"{name}_sconv")
    t = mm(oh, wt["w_out_a"][i], add=x, name=f"{name}_outa")
    out = mm(conv, wt["w_out_c"][i], add=t, name=f"{name}_outc")
    return out, (x, h, proj, cqn, ckvn, qraw, kv, q, k, o, oh, lse, conv)


def _even_bwd(dout, douth, saved, i, wt, tabs, name):
    x, h, proj, cqn, ckvn, qraw, kv, q, k, o, oh, lse, conv = saved
    g = {}
    do = mm(douth, wt["w_out_a"][i], tb=True, name=f"{name}_douta")
    dconv = mm(douth, wt["w_out_c"][i], tb=True, name=f"{name}_doutc")
    g["w_out_a"] = mm(oh, douth, ta=True, out_dtype=BF16, name=f"{name}_dwouta")
    g["w_out_c"] = mm(conv, douth, ta=True, out_dtype=BF16, name=f"{name}_dwoutc")
    dgates, g["sconv_w"] = sconv_bwd(proj, wt["sconv_w"][i], dconv, name=f"{name}_dsconv")
    dq = attn_bwd_dq(q, k, kv, o, lse, do, name=f"{name}_dattn_q")
    dk, dv = attn_bwd_dkv(q, k, kv, o, lse, do, name=f"{name}_dattn_kv")
    dqraw, dkraw, dkrope, dqg, dkg = qkprep_bwd(qraw, kv, proj, wt["q_gain_p"][i], wt["k_gain_p"][i], tabs, dq, dk,
                                                name=f"{name}_dqkprep")
    g["q_gain"], g["k_gain"] = dqg[0, :QK], dkg[0, :QK]
    dcqn = mm(dqraw, wt["w_uq_p"][i], tb=True, name=f"{name}_duq")
    g["w_uq_p"] = mm(cqn, dqraw, ta=True, out_dtype=BF16, name=f"{name}_dwuq")
    dkv = jnp.concatenate([dkraw, dv], axis=1)
    dckvn = mm(dkv, wt["w_ukv_p"][i], tb=True, name=f"{name}_dukv")
    g["w_ukv_p"] = mm(ckvn, dkv, ta=True, out_dtype=BF16, name=f"{name}_dwukv")
    dcq, dgq = rms_bwd(proj, wt["cq_norm"][i][None], dcqn, col=0, out_dtype=BF16, name=f"{name}_dcqnorm")
    dckv, dgkv = rms_bwd(proj, wt["ckv_norm"][i][None], dckvn, col=1, out_dtype=BF16, name=f"{name}_dckvnorm")
    g["cq_norm"], g["ckv_norm"] = dgq[0], dgkv[0]
    dproj = jnp.concatenate([dcq, dckv, dgates, dkrope.astype(BF16)], axis=1)
    g["w_in2"] = mm(h, dproj, ta=True, out_dtype=BF16, name=f"{name}_dwin")
    dh = mm(dproj, wt["w_in2"][i], tb=True, name=f"{name}_din")
    dx, dxh, dg = rms_bwd(x, wt["attn_norm"][i][None], dh, add=dout, twin=True, name=f"{name}_dnorm")
    g["attn_norm"] = dg[0]
    return dx, dxh, g


def _odd_fwd(x, i, wt, name):
    h = rms_fwd(x, wt["ssm_norm"][i][None], name=f"{name}_norm")
    u = mm(h, wt["ssm_w_in"][i], name=f"{name}_in")
    y, st_r, st_i = s5_fwd(u, wt["bbd_r"][i], wt["bbd_i"][i], wt["cbd_r"][i], wt["cbd_i"][i], wt["tab_f"][i],
                           name=f"{name}_scan")
    gl = s5post_fwd(y, u, wt["d_skip"][i][None], name=f"{name}_gelu")
    glu = mm(gl, wt["w_glu"][i], name=f"{name}_glu")
    out = glu_fwd(glu, x, name=f"{name}_gate")
    return out, (x, h, u, y, st_r, st_i, gl, glu)


def _odd_bwd(dout, douth, saved, i, wt, name):
    x, h, u, y, st_r, st_i, gl, glu = saved
    g = {}
    dglu = glu_bwd(glu, dout, name=f"{name}_dgate")
    g["w_glu"] = mm(gl, dglu, ta=True, out_dtype=BF16, name=f"{name}_dwglu")
    dgl = mm(dglu, wt["w_glu"][i], tb=True, name=f"{name}_dglu")
    dz, dd = s5post_bwd(y, u, wt["d_skip"][i][None], dgl, name=f"{name}_dgelu")
    g["d_skip"] = dd[0]
    du, g["bbd_r"], g["bbd_i"], g["cbd_r"], g["cbd_i"], g["a"] = s5_bwd(
        u, dz, wt["d_skip"][i][None], st_r, st_i, wt["bbd_r"][i], wt["bbd_i"][i], wt["cbd_r"][i], wt["cbd_i"][i],
        wt["tab_r"][i], name=f"{name}_dscan")
    g["ssm_w_in"] = mm(h, du, ta=True, out_dtype=BF16, name=f"{name}_dwin")
    dh = mm(du, wt["ssm_w_in"][i], tb=True, name=f"{name}_din")
    dx, dxh, dg = rms_bwd(x, wt["ssm_norm"][i][None], dh, add=dout, twin=True, name=f"{name}_dnorm")
    g["ssm_norm"] = dg[0]
    return dx, dxh, g


def _local_step(x, target, full):
    s = x.shape[0]
    n_even = (DEPTH + 1) // 2
    n_odd = DEPTH // 2
    tabs = _rope_tables(s)
    wt = dict(full)
    wt["w_in2"] = jnp.stack([_w_in_layout(full["mix_w_in"][i]) for i in range(n_even)])
    wt["w_uq_p"] = jnp.stack([_pad_heads_cols(full["w_uq"][i], QK) for i in range(n_even)])
    ukv = full["w_ukv"].reshape(n_even, KVR, HEADS, NOPE + VD)
    wt["w_ukv_p"] = jnp.stack([jnp.concatenate(
        [_pad_heads_cols(ukv[i, :, :, :NOPE].reshape(KVR, HEADS * NOPE), NOPE),
         _pad_heads_cols(ukv[i, :, :, NOPE:].reshape(KVR, HEADS * VD), VD)], axis=1) for i in range(n_even)])
    wt["w_out_a"] = jnp.stack([_pad_heads_cols(full["mix_w_out"][i, :HEADS * VD].T, VD).T for i in range(n_even)])
    wt["w_out_c"] = full["mix_w_out"][:, HEADS * VD:]
    wt["q_gain_p"] = jnp.pad(full["q_gain"], ((0, 0), (0, HP - QK)))[:, None, :]
    wt["k_gain_p"] = jnp.pad(full["k_gain"], ((0, 0), (0, HP - QK)))[:, None, :]

    disc_vjp = []
    for key in ("bbd_r", "bbd_i", "cbd_r", "cbd_i", "tab_f", "tab_r"):
        wt[key] = []
    for i in range(n_odd):
        (ar, ai, bbr, bbi), vjp = jax.vjp(_discretize, full["lambda_re"][i], full["lambda_im"][i], full["log_step"][i],
                                          full["b_re"][i], full["b_im"][i])
        disc_vjp.append(vjp)
        tf, tr = _scan_tables(ar.reshape(-1), ai.reshape(-1))
        wt["tab_f"].append(tf)
        wt["tab_r"].append(tr)
        wt["bbd_r"].append(_b_blockdiag(bbr).astype(BF16))
        wt["bbd_i"].append(_b_blockdiag(bbi).astype(BF16))
        wt["cbd_r"].append(_c_blockdiag(full["c_re"][i]).astype(BF16))
        wt["cbd_i"].append(_c_blockdiag(full["c_im"][i]).astype(BF16))

    saved = []
    for layer in range(DEPTH):
        i = layer // 2
        if layer % 2 == 0:
            x, sm = _even_fwd(x, i, wt, tabs, f"l{layer}_mla")
        else:
            x, sm = _odd_fwd(x, i, wt, f"l{layer}_s5")
        x, sf = _ffn_fwd(x, layer, wt, f"l{layer}_ffn")
        saved.append((sm, sf))
    dx, dxh, lslab = loss_head(x, target, name="loss_head")

    grads = {n: [None] * (DEPTH if n.startswith("ffn") else n_even) for n in ORDER}
    for layer in reversed(range(DEPTH)):
        i = layer // 2
        sm, sf = saved[layer]
        dx, dxh, g = _ffn_bwd(dx, dxh, sf, layer, wt, f"l{layer}_ffn")
        for n in ("ffn_norm", "ffn_w_up", "ffn_conv_w", "ffn_w_down"):
            grads[n][layer] = g[n]
        if layer % 2 == 0:
            dx, dxh, g = _even_bwd(dx, dxh, sm, i, wt, tabs, f"l{layer}_mla")
            grads["mix_w_in"][i] = _w_in_layout_t(g["w_in2"])
            grads["w_uq"][i] = _unpad_heads_cols(g["w_uq_p"], QK)
            dk_, dv_ = g["w_ukv_p"][:, :HEADS * HP], g["w_ukv_p"][:, HEADS * HP:]
            grads["w_ukv"][i] = jnp.concatenate(
                [dk_.reshape(KVR, HEADS, HP)[:, :, :NOPE], dv_.reshape(KVR, HEADS, HP)[:, :, :VD]],
                axis=2).reshape(KVR, HEADS * (NOPE + VD))
            grads["mix_w_out"][i] = jnp.concatenate(
                [_unpad_heads_cols(g["w_out_a"].T, VD).T, g["w_out_c"]], axis=0)
            for n in ("attn_norm", "cq_norm", "ckv_norm", "q_gain", "k_gain", "sconv_w"):
                grads[n][i] = g[n]
        else:
            dx, dxh, g = _odd_bwd(dx, dxh, sm, i, wt, f"l{layer}_s5")
            dlr, dli, dls, dbr, dbi = disc_vjp[i]((g["a"][0].reshape(G, P), g["a"][1].reshape(G, P),
                                                    _b_blockdiag_t(g["bbd_r"]), _b_blockdiag_t(g["bbd_i"])))
            grads["lambda_re"][i], grads["lambda_im"][i], grads["log_step"][i] = dlr, dli, dls
            grads["b_re"][i], grads["b_im"][i] = dbr, dbi
            grads["c_re"][i], grads["c_im"][i] = _c_blockdiag_t(g["cbd_r"]), _c_blockdiag_t(g["cbd_i"])
            for n in ("ssm_norm", "ssm_w_in", "d_skip", "w_glu"):
                grads[n][i] = g[n]
    grads = {n: jnp.stack(v) for n, v in grads.items()}
    return jnp.sum(lslab), dx, grads


def kernel(x, attn_norm, mix_w_in, cq_norm, ckv_norm, w_uq, w_ukv, q_gain, k_gain, sconv_w, mix_w_out, ssm_norm, ssm_w_in, lambda_re, lambda_im, log_step, b_re, b_im, c_re, c_im, d_skip, w_glu, ffn_norm, ffn_w_up, ffn_conv_w, ffn_w_down, loss_target, m_attn_norm, m_mix_w_in, m_cq_norm, m_ckv_norm, m_w_uq, m_w_ukv, m_q_gain, m_k_gain, m_sconv_w, m_mix_w_out, m_ssm_norm, m_ssm_w_in, m_lambda_re, m_lambda_im, m_log_step, m_b_re, m_b_im, m_c_re, m_c_im, m_d_skip, m_w_glu, m_ffn_norm, m_ffn_w_up, m_ffn_conv_w, m_ffn_w_down, v_attn_norm, v_mix_w_in, v_cq_norm, v_ckv_norm, v_w_uq, v_w_ukv, v_q_gain, v_k_gain, v_sconv_w, v_mix_w_out, v_ssm_norm, v_ssm_w_in, v_lambda_re, v_lambda_im, v_log_step, v_b_re, v_b_im, v_c_re, v_c_im, v_d_skip, v_w_glu, v_ffn_norm, v_ffn_w_up, v_ffn_conv_w, v_ffn_w_down):
    args = dict(locals())
    w = {n: args[n] for n in ORDER}
    m = {n: args["m_" + n] for n in ORDER}
    v = {n: args["v_" + n] for n in ORDER}
    me = 2 * lax.axis_index("x") + lax.axis_index("y")

    gb = _chip_exchange([w[n].astype(BF16) for n, _ in BIG], False, "gather_w_bf16")
    gs = _chip_exchange([w[n] for n, _ in SMALL], False, "gather_w_f32")
    full = {n: w[n] for n in REPL}
    for (n, ax), g in zip(BIG + SMALL, gb + gs):
        full[n] = _join(g, ax)

    sq, dx, grads = _local_step(x[0], loss_target[0], full)
    loss = lax.psum(0.5 * sq / D_MODEL, ("x", "y", "c"))

    rep_names = REPL + [n for n, _ in SMALL]
    names = [n for n, _ in BIG] + rep_names
    slots = (_chip_exchange([_split(grads[n], ax) for n, ax in BIG], True, "scatter_g_bf16")
             + _chip_exchange([grads[n] for n in rep_names], False, "gather_g_f32"))
    mine = [sum_slots(_rows2d(sl, 1), name=f"sum_{n}") for n, sl in zip(names, slots)]
    other = sibling_swap(mine, "swap_g")

    def local(n, p):
        ax = dict(SMALL).get(n)
        if ax is None:
            return p
        part = lax.dynamic_index_in_dim(_split(p.reshape(grads[n].shape), ax), me, 0, keepdims=False)
        return _rows2d(part)

    outs = {}
    for n, p, q in zip(names, mine, other):
        res = adamw(local(n, p), local(n, q), _rows2d(w[n]), _rows2d(m[n]), _rows2d(v[n]), name=f"adamw_{n}")
        outs[n] = [r.reshape(w[n].shape) for r in res]
    return (loss, dx[None], *[outs[n][0] for n in ORDER], *[outs[n][1] for n in ORDER],
            *[outs[n][2] for n in ORDER], *[outs[n][3] for n in ORDER])
```

```python
import functools
import math

import numpy as np
import jax
import jax.numpy as jnp
from jax import lax
from jax.experimental import pallas as pl
from jax.experimental.pallas import tpu as pltpu

F32, BF16 = jnp.float32, jnp.bfloat16

D_MODEL = 1024
DEPTH = 4
HEADS = 8
NOPE, ROPE, QK, VD = 64, 32, 96, 64
HP = 128
QR, KVR = 256, 256
CONVC = 512
FFN_H = 2816
G, P, GC = 64, 64, 16
NST = G * P
SLAB = 8
LANE = 128
EPS = 1e-6
ROPE_THETA = 10000.0
ADAM_LR, ADAM_B1, ADAM_B2, ADAM_EPS, ADAM_WD, ADAM_STEP = 0.001, 0.9, 0.999, 1e-08, 0.01, 10
VMEM_LIMIT = 48 * 1024 * 1024
MESH = pl.DeviceIdType.MESH
ANY = pl.BlockSpec(memory_space=pl.ANY)


def _cparams(*sem):
    return pltpu.CompilerParams(dimension_semantics=sem, vmem_limit_bytes=VMEM_LIMIT)


def _pick(dim, prefs):
    for p in prefs:
        if dim % p == 0:
            return p
    return dim


def _rows(s):
    return _pick(s, (512, 256, 128, 64, 32, 16, 8))


MM_VMEM_BUDGET = 36 * 1024 * 1024
MM_MAX_TILE_ELEMS = 640 * 1024
HBM_BYTES_PER_US = 3.0e6
STEP_OVERHEAD_US = 0.35


def _lane_tiles(n):
    c = {t for t in range(LANE, min(n, 1536) + 1, LANE) if n % t == 0}
    if n <= 2304 or not c:
        c.add(n)
    return sorted(c, reverse=True)


def _mm_tiles(m, n, k, sa, sb, so):
    best = None
    for tm in [t for t in (1024, 512, 256) if m % t == 0] or [m]:
        for tn in _lane_tiles(n):
            if tm * tn > MM_MAX_TILE_ELEMS:
                continue
            if 2 * (tm * k * sa + k * tn * sb + tm * tn * so) + 4 * tm * tn > MM_VMEM_BUDGET:
                continue
            steps = (m // tm) * (n // tn)
            for inner_n in (True, False):
                moved = (m * k * sa + (m // tm) * k * n * sb) if inner_n else (k * n * sb + (n // tn) * m * k * sa)
                cost = (moved + m * n * so) / HBM_BYTES_PER_US + steps * STEP_OVERHEAD_US
                if best is None or cost < best[0]:
                    best = (cost, tm, tn, inner_n)
    return best[1:]


def mm(a, b, *, ta=False, tb=False, add=None, out_dtype=F32, name):
    if ta:
        kdim, m = a.shape
    else:
        m, kdim = a.shape
    n = b.shape[0] if tb else b.shape[1]
    so = jnp.dtype(out_dtype).itemsize + (0 if add is None else add.dtype.itemsize)
    tm, tn, inner_n = _mm_tiles(m, n, kdim, a.dtype.itemsize, b.dtype.itemsize, so)
    dn = (((0 if ta else 1,), (1 if tb else 0,)), ((), ()))

    def body(*refs):
        if add is None:
            a_ref, b_ref, o_ref = refs
        else:
            a_ref, b_ref, add_ref, o_ref = refs
        r = lax.dot_general(a_ref[...].astype(BF16), b_ref[...].astype(BF16), dn, preferred_element_type=F32)
        if add is not None:
            r = r + add_ref[...].astype(F32)
        o_ref[...] = r.astype(out_dtype)

    ij = (lambda g0, g1: (g0, g1)) if inner_n else (lambda g0, g1: (g1, g0))
    a_spec = (pl.BlockSpec((kdim, tm), lambda g0, g1: (0, ij(g0, g1)[0])) if ta
              else pl.BlockSpec((tm, kdim), lambda g0, g1: (ij(g0, g1)[0], 0)))
    b_spec = (pl.BlockSpec((tn, kdim), lambda g0, g1: (ij(g0, g1)[1], 0)) if tb
              else pl.BlockSpec((kdim, tn), lambda g0, g1: (0, ij(g0, g1)[1])))
    o_spec = pl.BlockSpec((tm, tn), lambda g0, g1: ij(g0, g1))
    ins, specs = [a, b], [a_spec, b_spec]
    if add is not None:
        ins.append(add)
        specs.append(o_spec)
    grid = (m // tm, n // tn) if inner_n else (n // tn, m // tm)
    return pl.pallas_call(
        body, name=name, grid=grid, in_specs=specs, out_specs=o_spec,
        out_shape=jax.ShapeDtypeStruct((m, n), out_dtype),
        compiler_params=_cparams("parallel", "parallel"))(*ins)


def rms_fwd(x, g, *, col=0, out_dtype=BF16, name):
    s = x.shape[0]
    d = g.shape[1]
    tm = _rows(s)

    def body(x_ref, g_ref, o_ref):
        xv = x_ref[...]
        r = lax.rsqrt(jnp.mean(xv * xv, axis=-1, keepdims=True) + EPS)
        o_ref[...] = (xv * r * g_ref[...]).astype(out_dtype)

    return pl.pallas_call(
        body, name=name, grid=(s // tm,),
        in_specs=[pl.BlockSpec((tm, d), lambda i: (i, col)), pl.BlockSpec((1, d), lambda i: (0, 0))],
        out_specs=pl.BlockSpec((tm, d), lambda i: (i, 0)),
        out_shape=jax.ShapeDtypeStruct((s, d), out_dtype),
        compiler_params=_cparams("parallel"))(x, g)


def rms_bwd(x, g, dy, *, col=0, add=None, out_dtype=F32, twin=False, name):
    s = x.shape[0]
    d = g.shape[1]
    tm = _rows(s)

    def body(*refs):
        refs = list(refs)
        dg_ref = refs.pop()
        dxh_ref = refs.pop() if twin else None
        dx_ref = refs.pop()
        add_ref = refs.pop() if add is not None else None
        x_ref, g_ref, dy_ref = refs

        @pl.when(pl.program_id(0) == 0)
        def _():
            dg_ref[...] = jnp.zeros_like(dg_ref)

        xv = x_ref[...]
        dyv = dy_ref[...].astype(F32)
        r = lax.rsqrt(jnp.mean(xv * xv, axis=-1, keepdims=True) + EPS)
        xh = xv * r
        dg_ref[...] += jnp.sum(dyv * xh, axis=0, keepdims=True)
        dxh = dyv * g_ref[...]
        dx = r * (dxh - xh * jnp.mean(dxh * xh, axis=-1, keepdims=True))
        if add is not None:
            dx = dx + add_ref[...]
        dx_ref[...] = dx.astype(out_dtype)
        if twin:
            dxh_ref[...] = dx.astype(BF16)

    row = pl.BlockSpec((tm, d), lambda i: (i, 0))
    vec = pl.BlockSpec((1, d), lambda i: (0, 0))
    ins = [x, g, dy]
    specs = [pl.BlockSpec((tm, d), lambda i: (i, col)), vec, row]
    if add is not None:
        ins.append(add)
        specs.append(row)
    dxs = [jax.ShapeDtypeStruct((s, d), out_dtype)] + ([jax.ShapeDtypeStruct((s, d), BF16)] if twin else [])
    return pl.pallas_call(
        body, name=name, grid=(s // tm,), in_specs=specs,
        out_specs=[row] * len(dxs) + [vec],
        out_shape=dxs + [jax.ShapeDtypeStruct((1, d), F32)],
        compiler_params=_cparams("arbitrary"))(*ins)


def _rope_tables(s):
    inv = 1.0 / (ROPE_THETA ** (jnp.arange(0, ROPE, 2, dtype=F32) / ROPE))
    ang = jnp.arange(s, dtype=F32)[:, None] * inv[None, :]
    cos, sin = jnp.cos(ang), jnp.sin(ang)
    z = lambda w: jnp.zeros((s, w), F32)
    c = jnp.concatenate([jnp.ones((s, NOPE), F32), cos, cos, z(HP - QK)], axis=1)
    s1 = jnp.concatenate([z(NOPE), -sin, z(HP - NOPE - ROPE // 2)], axis=1)
    s2 = jnp.concatenate([z(NOPE + ROPE // 2), sin, z(HP - QK)], axis=1)
    return c, s1, s2


def qkprep_fwd(qraw, kv, proj, qg, kg, tabs, *, name):
    s = qraw.shape[0]
    tm = _rows(s)
    kr_col = (proj.shape[1] - HP) // HP

    def body(q_ref, k_ref, v_ref, kr_ref, qg_ref, kg_ref, c_ref, s1_ref, s2_ref, qo_ref, ko_ref, vo_ref):
        c, s1, s2 = c_ref[...], s1_ref[...], s2_ref[...]

        def f(xv, gain):
            r = lax.rsqrt(jnp.sum(xv * xv, axis=-1, keepdims=True) * (1.0 / QK) + EPS)
            xn = xv * r * gain
            return xn * c + pltpu.roll(xn, HP - ROPE // 2, 1) * s1 + pltpu.roll(xn, ROPE // 2, 1) * s2

        qo_ref[...] = f(q_ref[...], qg_ref[...]).astype(BF16)
        ko_ref[...] = f(k_ref[...] + kr_ref[...], kg_ref[...]).astype(BF16)
        vv = v_ref[...]
        lane = lax.broadcasted_iota(jnp.int32, vv.shape, 1)
        vo_ref[...] = jnp.where(lane == VD, 1.0, vv).astype(BF16)

    head = pl.BlockSpec((tm, HP), lambda i, h: (i, h))
    tab = pl.BlockSpec((tm, HP), lambda i, h: (i, 0))
    gain = pl.BlockSpec((1, HP), lambda i, h: (0, 0))
    return pl.pallas_call(
        body, name=name, grid=(s // tm, HEADS),
        in_specs=[head, head, pl.BlockSpec((tm, HP), lambda i, h: (i, HEADS + h)),
                  pl.BlockSpec((tm, HP), lambda i, h: (i, kr_col)), gain, gain, tab, tab, tab],
        out_specs=[head, head, head],
        out_shape=[jax.ShapeDtypeStruct((s, HEADS * HP), BF16)] * 3,
        compiler_params=_cparams("parallel", "parallel"))(qraw, kv, kv, proj, qg, kg, *tabs)


def qkprep_bwd(qraw, kv, proj, qg, kg, tabs, dq, dk, *, name):
    s = qraw.shape[0]
    tm = _rows(s)
    kr_col = (proj.shape[1] - HP) // HP

    def body(q_ref, k_ref, kr_ref, qg_ref, kg_ref, c_ref, s1_ref, s2_ref, dq_ref, dk_ref,
             dqr_ref, dkr_ref, dkrope_ref, dqg_ref, dkg_ref):
        i, h = pl.program_id(0), pl.program_id(1)
        c, s1, s2 = c_ref[...], s1_ref[...], s2_ref[...]

        @pl.when((i == 0) & (h == 0))
        def _():
            dqg_ref[...] = jnp.zeros_like(dqg_ref)
            dkg_ref[...] = jnp.zeros_like(dkg_ref)

        @pl.when(h == 0)
        def _():
            dkrope_ref[...] = jnp.zeros_like(dkrope_ref)

        def f(xv, gain, dout):
            r = lax.rsqrt(jnp.sum(xv * xv, axis=-1, keepdims=True) * (1.0 / QK) + EPS)
            xh = xv * r
            dxn = dout * c + pltpu.roll(dout * s1, ROPE // 2, 1) + pltpu.roll(dout * s2, HP - ROPE // 2, 1)
            dgain = jnp.sum(dxn * xh, axis=0, keepdims=True)
            dxh = dxn * gain
            dx = r * (dxh - xh * (jnp.sum(dxh * xh, axis=-1, keepdims=True) * (1.0 / QK)))
            return dx, dgain

        dxq, dgq = f(q_ref[...], qg_ref[...], dq_ref[...])
        dxk, dgk = f(k_ref[...] + kr_ref[...], kg_ref[...], dk_ref[...])
        dqr_ref[...] = dxq.astype(BF16)
        dkr_ref[...] = dxk.astype(BF16)
        dqg_ref[...] += dgq
        dkg_ref[...] += dgk
        lane = lax.broadcasted_iota(jnp.int32, dxk.shape, 1)
        dkrope_ref[...] += jnp.where((lane >= NOPE) & (lane < QK), dxk, 0.0)

    head = pl.BlockSpec((tm, HP), lambda i, h: (i, h))
    tab = pl.BlockSpec((tm, HP), lambda i, h: (i, 0))
    gain = pl.BlockSpec((1, HP), lambda i, h: (0, 0))
    return pl.pallas_call(
        body, name=name, grid=(s // tm, HEADS),
        in_specs=[head, head, pl.BlockSpec((tm, HP), lambda i, h: (i, kr_col)), gain, gain, tab, tab, tab, head, head],
        out_specs=[head, head, tab, gain, gain],
        out_shape=[jax.ShapeDtypeStruct((s, HEADS * HP), BF16)] * 2
        + [jax.ShapeDtypeStruct((s, HP), F32), jax.ShapeDtypeStruct((1, HP), F32), jax.ShapeDtypeStruct((1, HP), F32)],
        compiler_params=_cparams("arbitrary", "arbitrary"))(qraw, kv, proj, qg, kg, *tabs, dq, dk)


ATT_SCALE = QK ** -0.5
NEG = -1e30


def _att_tile(s):
    return _pick(s, (512, 256, 128))


def _causal(sv, diag):
    r = lax.broadcasted_iota(jnp.int32, sv.shape, 0)
    c = lax.broadcasted_iota(jnp.int32, sv.shape, 1)
    return jnp.where(diag & (c > r), NEG, sv)


NT = (((1,), (1,)), ((), ()))
TN = (((0,), (0,)), ((), ()))


def _row_of(col):
    return jnp.broadcast_to(col, (col.shape[0], LANE)).T[0:SLAB, :]


def _att_specs(s):
    t = _att_tile(s)
    nb = s // t
    tile = pl.BlockSpec((t, HP), lambda h, i: (i, h))
    whole = pl.BlockSpec((s, HP), lambda h, i: (0, h))
    row = pl.BlockSpec((1, 1, SLAB, t), lambda h, i: (h, i, 0, 0))
    rows = pl.BlockSpec((1, nb, SLAB, t), lambda h, i: (h, 0, 0, 0))
    return t, nb, tile, whole, row, rows


def attn_fwd(q, k, v, *, name):
    s = q.shape[0]
    t, nb, tile, whole, row, _ = _att_specs(s)

    def body(q_ref, k_ref, v_ref, o_ref, oh_ref, lse_ref, lset_ref, s_scr, mb_scr, acc):
        qb = pl.program_id(1)
        qv = q_ref[...]

        def scores(j):
            r0 = pl.multiple_of(j * t, t)
            return lax.dot_general(qv, k_ref[pl.ds(r0, t), :], NT, preferred_element_type=F32) * ATT_SCALE

        def fold(sv):
            m = sv[:, 0:LANE]
            for kk in range(1, t // LANE):
                m = jnp.maximum(m, sv[:, kk * LANE:(kk + 1) * LANE])
            return m

        def first(j, m):
            sv = scores(j)
            s_scr[j] = sv
            return jnp.maximum(m, fold(sv))

        m = lax.fori_loop(0, qb, first, jnp.full((t, LANE), NEG, F32))
        sd = _causal(scores(qb), True)
        s_scr[qb] = sd
        mcol = jnp.max(jnp.maximum(m, fold(sd)), axis=-1, keepdims=True)
        mb_scr[...] = jnp.broadcast_to(mcol, (t, t))
        acc[...] = jnp.zeros_like(acc)

        def second(j, carry):
            r0 = pl.multiple_of(j * t, t)
            p = jnp.exp(s_scr[j] - mb_scr[...]).astype(BF16)
            acc[...] += jnp.dot(p, v_ref[pl.ds(r0, t), :], preferred_element_type=F32)
            return carry

        lax.fori_loop(0, qb + 1, second, 0)
        av = acc[...]
        lsum = av[:, VD:VD + 1]
        lane = lax.broadcasted_iota(jnp.int32, av.shape, 1)
        ov = jnp.where(lane == VD, 0.0, av / lsum)
        o_ref[...] = ov
        oh_ref[...] = ov.astype(BF16)
        lse = mcol + jnp.log(lsum)
        lse_ref[...] = jnp.broadcast_to(lse, lse_ref.shape)
        lset_ref[0, 0] = _row_of(lse)

    return pl.pallas_call(
        body, name=name, grid=(HEADS, nb), in_specs=[tile, whole, whole], out_specs=[tile, tile, tile, row],
        out_shape=[jax.ShapeDtypeStruct((s, HEADS * HP), F32), jax.ShapeDtypeStruct((s, HEADS * HP), BF16),
                   jax.ShapeDtypeStruct((s, HEADS * HP), F32), jax.ShapeDtypeStruct((HEADS, nb, SLAB, t), F32)],
        scratch_shapes=[pltpu.VMEM((nb, t, t), F32), pltpu.VMEM((t, t), F32), pltpu.VMEM((t, HP), F32)],
        compiler_params=_cparams("parallel", "parallel"))(q, k, v)


def attn_bwd_dq(q, k, v, o, lse, do, *, name):
    s = q.shape[0]
    t, nb, tile, whole, row, _ = _att_specs(s)

    def body(q_ref, k_ref, v_ref, o_ref, lse_ref, do_ref, dq_ref, doh_ref, dt_ref, lb_scr, db_scr, acc):
        qb = pl.program_id(1)
        qv = q_ref[...]
        dov = do_ref[...]
        dob = dov.astype(BF16)
        doh_ref[...] = dob
        delta = jnp.sum(dov * o_ref[...], axis=-1, keepdims=True)
        dt_ref[0, 0] = _row_of(delta)
        lb_scr[...] = jnp.broadcast_to(lse_ref[...][:, 0:1], (t, t))
        db_scr[...] = jnp.broadcast_to(delta, (t, t))
        acc[...] = jnp.zeros_like(acc)

        def step(j, diag):
            r0 = pl.multiple_of(j * t, t)
            kj = k_ref[pl.ds(r0, t), :]
            sv = lax.dot_general(qv, kj, NT, preferred_element_type=F32) * ATT_SCALE
            if diag:
                sv = _causal(sv, True)
            p = jnp.exp(sv - lb_scr[...])
            dp = lax.dot_general(dob, v_ref[pl.ds(r0, t), :], NT, preferred_element_type=F32)
            ds = (p * (dp - db_scr[...])).astype(BF16)
            acc[...] += jnp.dot(ds, kj, preferred_element_type=F32)

        def off_diag(j, carry):
            step(j, False)
            return carry

        lax.fori_loop(0, qb, off_diag, 0)
        step(qb, True)
        dq_ref[...] = acc[...] * ATT_SCALE

    return pl.pallas_call(
        body, name=name, grid=(HEADS, nb), in_specs=[tile, whole, whole, tile, tile, tile],
        out_specs=[tile, tile, row],
        out_shape=[jax.ShapeDtypeStruct((s, HEADS * HP), F32), jax.ShapeDtypeStruct((s, HEADS * HP), BF16),
                   jax.ShapeDtypeStruct((HEADS, nb, SLAB, t), F32)],
        scratch_shapes=[pltpu.VMEM((t, t), F32), pltpu.VMEM((t, t), F32), pltpu.VMEM((t, HP), F32)],
        compiler_params=_cparams("parallel", "parallel"))(q, k, v, o, lse, do)


def attn_bwd_dkv(q, k, v, doh, lset, deltat, *, name):
    s = q.shape[0]
    t, nb, tile, whole, _, rows = _att_specs(s)

    def body(q_ref, k_ref, v_ref, do_ref, lt_ref, dt_ref, dk_ref, dv_ref, dk_acc, dv_acc):
        kb = pl.program_id(1)
        kt, vt = k_ref[...], v_ref[...]
        dk_acc[...] = jnp.zeros_like(dk_acc)
        dv_acc[...] = jnp.zeros_like(dv_acc)

        def step(i, diag):
            r0 = pl.multiple_of(i * t, t)
            qi, doi = q_ref[pl.ds(r0, t), :], do_ref[pl.ds(r0, t), :]
            st = lax.dot_general(kt, qi, NT, preferred_element_type=F32) * ATT_SCALE
            if diag:
                kr = lax.broadcasted_iota(jnp.int32, st.shape, 0)
                qc = lax.broadcasted_iota(jnp.int32, st.shape, 1)
                st = jnp.where(kr > qc, NEG, st)
            pt = jnp.exp(st - lt_ref[0, i][0:1, :])
            dpt = lax.dot_general(vt, doi, NT, preferred_element_type=F32)
            dst = (pt * (dpt - dt_ref[0, i][0:1, :])).astype(BF16)
            dv_acc[...] += jnp.dot(pt.astype(BF16), doi, preferred_element_type=F32)
            dk_acc[...] += jnp.dot(dst, qi, preferred_element_type=F32)

        def off_diag(i, carry):
            step(i, False)
            return carry

        step(kb, True)
        lax.fori_loop(kb + 1, nb, off_diag, 0)
        dk_ref[...] = dk_acc[...] * ATT_SCALE
        dvv = dv_acc[...]
        lane = lax.broadcasted_iota(jnp.int32, dvv.shape, 1)
        dv_ref[...] = jnp.where(lane == VD, 0.0, dvv).astype(BF16)

    return pl.pallas_call(
        body, name=name, grid=(HEADS, nb), in_specs=[whole, tile, tile, whole, rows, rows], out_specs=[tile, tile],
        out_shape=[jax.ShapeDtypeStruct((s, HEADS * HP), F32), jax.ShapeDtypeStruct((s, HEADS * HP), BF16)],
        scratch_shapes=[pltpu.VMEM((t, HP), F32), pltpu.VMEM((t, HP), F32)],
        compiler_params=_cparams("parallel", "parallel"))(q, k, v, doh, lset, deltat)


HALO = 8
CW = 256


def _conv3(zw, w):
    return w[2:3] * zw + w[1:2] * pltpu.roll(zw, 1, 0) + w[0:1] * pltpu.roll(zw, 2, 0)


def _conv3_t(dc, w):
    n = dc.shape[0]
    return w[2:3] * dc + w[1:2] * pltpu.roll(dc, n - 1, 0) + w[0:1] * pltpu.roll(dc, n - 2, 0)


def _conv3_dw(dc, zw, r):
    z0 = zw[HALO:HALO + r]
    z1 = pltpu.roll(zw, 1, 0)[HALO:HALO + r]
    z2 = pltpu.roll(zw, 2, 0)[HALO:HALO + r]
    return [jnp.sum(dc * z, axis=0, keepdims=True) for z in (z2, z1, z0)]


def _halo_specs(r, colfn):
    rb = r // HALO
    cur = pl.BlockSpec((r, CW), lambda j, i: (i, colfn(j)))
    prev = pl.BlockSpec((HALO, CW), lambda j, i: (jnp.maximum(i * rb - 1, 0), colfn(j)))

    def nxt(nrow_blocks):
        return pl.BlockSpec((HALO, CW), lambda j, i: (jnp.minimum((i + 1) * rb, nrow_blocks * rb - 1), colfn(j)))

    return cur, prev, nxt


def ffnact_fwd(up, w, *, name):
    s, c2 = up.shape
    hh = c2 // 2
    nj = hh // CW
    r = _rows(s)
    nt = s // r

    def body(g_ref, gp_ref, v_ref, vp_ref, wg_ref, wv_ref, o_ref):
        pm = (pl.program_id(1) > 0).astype(F32)
        cg = _conv3(jnp.concatenate([gp_ref[...] * pm, g_ref[...]], axis=0), wg_ref[...])[HALO:]
        cv = _conv3(jnp.concatenate([vp_ref[...] * pm, v_ref[...]], axis=0), wv_ref[...])[HALO:]
        o_ref[...] = (cg * jax.nn.sigmoid(cg) * cv).astype(BF16)

    gcur, gprev, _ = _halo_specs(r, lambda j: j)
    vcur, vprev, _ = _halo_specs(r, lambda j: nj + j)
    wg = pl.BlockSpec((3, CW), lambda j, i: (0, j))
    wv = pl.BlockSpec((3, CW), lambda j, i: (0, nj + j))
    return pl.pallas_call(
        body, name=name, grid=(nj, nt), in_specs=[gcur, gprev, vcur, vprev, wg, wv],
        out_specs=pl.BlockSpec((r, CW), lambda j, i: (i, j)),
        out_shape=jax.ShapeDtypeStruct((s, hh), BF16),
        compiler_params=_cparams("parallel", "parallel"))(up, up, up, up, w, w)


def ffnact_bwd(up, w, dact, *, name):
    s, c2 = up.shape
    hh = c2 // 2
    nj = hh // CW
    r = _rows(s)
    nt = s // r

    def body(g_ref, gp_ref, gn_ref, v_ref, vp_ref, vn_ref, wg_ref, wv_ref, da_ref, dan_ref,
             dg_ref, dv_ref, dwg_ref, dwv_ref):
        i = pl.program_id(1)
        pm = (i > 0).astype(F32)
        nm = (i < nt - 1).astype(F32)

        @pl.when(i == 0)
        def _():
            dwg_ref[...] = jnp.zeros_like(dwg_ref)
            dwv_ref[...] = jnp.zeros_like(dwv_ref)

        wg, wv = wg_ref[...], wv_ref[...]
        zg = jnp.concatenate([gp_ref[...] * pm, g_ref[...], gn_ref[...]], axis=0)
        zv = jnp.concatenate([vp_ref[...] * pm, v_ref[...], vn_ref[...]], axis=0)
        cg = _conv3(zg, wg)[HALO:]
        cv = _conv3(zv, wv)[HALO:]
        da = jnp.concatenate([da_ref[...], dan_ref[...] * nm], axis=0)
        sg = jax.nn.sigmoid(cg)
        dcg = da * cv * (sg * (1.0 + cg * (1.0 - sg)))
        dcv = da * (cg * sg)
        dg_ref[...] = _conv3_t(dcg, wg)[:r].astype(BF16)
        dv_ref[...] = _conv3_t(dcv, wv)[:r].astype(BF16)
        for kk, (a, b) in enumerate(zip(_conv3_dw(dcg[:r], zg, r), _conv3_dw(dcv[:r], zv, r))):
            dwg_ref[kk:kk + 1, :] += a
            dwv_ref[kk:kk + 1, :] += b

    gcur, gprev, gnext = _halo_specs(r, lambda j: j)
    vcur, vprev, vnext = _halo_specs(r, lambda j: nj + j)
    acur, _, anext = _halo_specs(r, lambda j: j)
    wg = pl.BlockSpec((3, CW), lambda j, i: (0, j))
    wv = pl.BlockSpec((3, CW), lambda j, i: (0, nj + j))
    dupg, dupv, dwg, dwv = pl.pallas_call(
        body, name=name, grid=(nj, nt),
        in_specs=[gcur, gprev, gnext(nt), vcur, vprev, vnext(nt), wg, wv, acur, anext(nt)],
        out_specs=[acur, acur, wg, wg],
        out_shape=[jax.ShapeDtypeStruct((s, hh), BF16), jax.ShapeDtypeStruct((s, hh), BF16),
                   jax.ShapeDtypeStruct((3, hh), F32), jax.ShapeDtypeStruct((3, hh), F32)],
        compiler_params=_cparams("parallel", "arbitrary"))(up, up, up, up, up, up, w, w, dact, dact)
    return jnp.concatenate([dupg, dupv], axis=1), jnp.concatenate([dwg, dwv], axis=1)


def sconv_fwd(proj, w, *, name):
    s = proj.shape[0]
    nj = CONVC // CW
    r = _rows(s)
    nt = s // r

    def body(b_ref, c_ref, cp_ref, x_ref, xp_ref, w_ref, o_ref):
        pm = (pl.program_id(1) > 0).astype(F32)
        zw = jnp.concatenate([cp_ref[...] * xp_ref[...] * pm, c_ref[...] * x_ref[...]], axis=0)
        o_ref[...] = (b_ref[...] * _conv3(zw, w_ref[...])[HALO:]).astype(BF16)

    bcur, _, _ = _halo_specs(r, lambda j: (QR + KVR) // CW + j)
    ccur, cprev, _ = _halo_specs(r, lambda j: (QR + KVR + CONVC) // CW + j)
    xcur, xprev, _ = _halo_specs(r, lambda j: (QR + KVR + 2 * CONVC) // CW + j)
    ws = pl.BlockSpec((3, CW), lambda j, i: (0, j))
    return pl.pallas_call(
        body, name=name, grid=(nj, nt), in_specs=[bcur, ccur, cprev, xcur, xprev, ws],
        out_specs=pl.BlockSpec((r, CW), lambda j, i: (i, j)),
        out_shape=jax.ShapeDtypeStruct((s, CONVC), BF16),
        compiler_params=_cparams("parallel", "parallel"))(proj, proj, proj, proj, proj, w)


def sconv_bwd(proj, w, dy, *, name):
    s = proj.shape[0]
    nj = CONVC // CW
    r = _rows(s)
    nt = s // r

    def body(b_ref, bn_ref, c_ref, cp_ref, x_ref, xp_ref, w_ref, dy_ref, dyn_ref, db_ref, dc_ref, dx_ref, dw_ref):
        i = pl.program_id(1)
        pm = (i > 0).astype(F32)
        nm = (i < nt - 1).astype(F32)

        @pl.when(i == 0)
        def _():
            dw_ref[...] = jnp.zeros_like(dw_ref)

        wv = w_ref[...]
        zw = jnp.concatenate([cp_ref[...] * xp_ref[...] * pm, c_ref[...] * x_ref[...]], axis=0)
        conv = _conv3(zw, wv)[HALO:]
        dyv = dy_ref[...]
        db_ref[...] = (dyv * conv).astype(BF16)
        dconv = jnp.concatenate([dyv * b_ref[...], dyn_ref[...] * bn_ref[...] * nm], axis=0)
        dz = _conv3_t(dconv, wv)[:r]
        dc_ref[...] = (dz * x_ref[...]).astype(BF16)
        dx_ref[...] = (dz * c_ref[...]).astype(BF16)
        for kk, a in enumerate(_conv3_dw(dconv[:r], zw, r)):
            dw_ref[kk:kk + 1, :] += a

    bcur, _, bnext = _halo_specs(r, lambda j: (QR + KVR) // CW + j)
    ccur, cprev, _ = _halo_specs(r, lambda j: (QR + KVR + CONVC) // CW + j)
    xcur, xprev, _ = _halo_specs(r, lambda j: (QR + KVR + 2 * CONVC) // CW + j)
    ycur, _, ynext = _halo_specs(r, lambda j: j)
    ws = pl.BlockSpec((3, CW), lambda j, i: (0, j))
    out = pl.BlockSpec((r, CW), lambda j, i: (i, j))
    db, dc, dx, dw = pl.pallas_call(
        body, name=name, grid=(nj, nt),
        in_specs=[bcur, bnext(nt), ccur, cprev, xcur, xprev, ws, ycur, ynext(nt)],
        out_specs=[out, out, out, ws],
        out_shape=[jax.ShapeDtypeStruct((s, CONVC), BF16)] * 3 + [jax.ShapeDtypeStruct((3, CONVC), F32)],
        compiler_params=_cparams("parallel", "arbitrary"))(proj, proj, proj, proj, proj, proj, w, dy, dy)
    return jnp.concatenate([db, dc, dx], axis=1), dw


SW = 512
NJ = NST // SW


def _scan_tables(ar, ai):
    def cmul(x, y):
        return x[0] * y[0] - x[1] * y[1], x[0] * y[1] + x[1] * y[0]

    def build(a, reverse):
        pw = [a]
        for _ in range(SLAB - 1):
            pw.append(cmul(pw[-1], a))
        row = jnp.arange(SLAB)[:, None]
        tabs = []
        for kk in (1, 2, 4):
            mask = ((row < SLAB - kk) if reverse else (row >= kk)).astype(F32)
            tabs += [mask * pw[kk - 1][0][None, :], mask * pw[kk - 1][1][None, :]]
        order = list(range(SLAB - 1, -1, -1)) if reverse else list(range(SLAB))
        tabs += [jnp.stack([pw[o][0] for o in order]), jnp.stack([pw[o][1] for o in order])]
        return jnp.stack(tabs)

    return build((ar, ai), False), build((ar, -ai), True)


def _slab_scan(xr, xi, tabs, cr, ci, reverse):
    for n, kk in enumerate((1, 2, 4)):
        sh = SLAB - kk if reverse else kk
        tr, ti = tabs[2 * n], tabs[2 * n + 1]
        sr, si = pltpu.roll(xr, sh, 0), pltpu.roll(xi, sh, 0)
        xr, xi = xr + tr * sr - ti * si, xi + tr * si + ti * sr
    tr, ti = tabs[6], tabs[7]
    return xr + tr * cr - ti * ci, xi + tr * ci + ti * cr


def s5_fwd(u, bbd_r, bbd_i, cbd_r, cbd_i, tab, *, name):
    s = u.shape[0]
    tbk = _rows(s)
    nt = s // tbk
    nsl = tbk // SLAB

    def body(u_ref, br_ref, bi_ref, cr_ref, ci_ref, tab_ref, y_ref, sr_ref, si_ref, bur, bui, carry):
        @pl.when(pl.program_id(1) == 0)
        def _():
            carry[...] = jnp.zeros_like(carry)

        ub = u_ref[...].astype(BF16)
        bur[...] = jnp.dot(ub, br_ref[0], preferred_element_type=F32)
        bui[...] = jnp.dot(ub, bi_ref[0], preferred_element_type=F32)
        tabs = [tab_ref[n] for n in range(8)]

        def slab(n, c):
            r0 = pl.multiple_of(n * SLAB, SLAB)
            sr, si = _slab_scan(bur[pl.ds(r0, SLAB), :], bui[pl.ds(r0, SLAB), :], tabs, c[0], c[1], False)
            sr_ref[pl.ds(r0, SLAB), :] = sr
            si_ref[pl.ds(r0, SLAB), :] = si
            return (jnp.broadcast_to(sr[SLAB - 1:SLAB], sr.shape), jnp.broadcast_to(si[SLAB - 1:SLAB], si.shape))

        cr, ci = lax.fori_loop(0, nsl, slab, (carry[0], carry[1]))
        carry[0] = cr
        carry[1] = ci
        y_ref[...] = (jnp.dot(sr_ref[...].astype(BF16), cr_ref[0], preferred_element_type=F32)
                      - jnp.dot(si_ref[...].astype(BF16), ci_ref[0], preferred_element_type=F32))

    us = pl.BlockSpec((tbk, LANE), lambda j, t: (t, j))
    bs = pl.BlockSpec((1, LANE, SW), lambda j, t: (j, 0, 0))
    cs = pl.BlockSpec((1, SW, LANE), lambda j, t: (j, 0, 0))
    ts = pl.BlockSpec((8, SLAB, SW), lambda j, t: (0, 0, j))
    ss = pl.BlockSpec((tbk, SW), lambda j, t: (t, j))
    return pl.pallas_call(
        body, name=name, grid=(NJ, nt), in_specs=[us, bs, bs, cs, cs, ts], out_specs=[us, ss, ss],
        out_shape=[jax.ShapeDtypeStruct((s, D_MODEL), F32), jax.ShapeDtypeStruct((s, NST), F32),
                   jax.ShapeDtypeStruct((s, NST), F32)],
        scratch_shapes=[pltpu.VMEM((tbk, SW), F32), pltpu.VMEM((tbk, SW), F32), pltpu.VMEM((2, SLAB, SW), F32)],
        compiler_params=_cparams("parallel", "arbitrary"))(u, bbd_r, bbd_i, cbd_r, cbd_i, tab)


def s5_bwd(u, dy, dskip, st_r, st_i, bbd_r, bbd_i, cbd_r, cbd_i, tabrev, *, name):
    s = u.shape[0]
    tbk = _rows(s)
    nt = s // tbk
    nsl = tbk // SLAB
    rbk = tbk // SLAB

    def body(u_ref, dy_ref, d_ref, sr_ref, si_ref, pr_ref, pi_ref, br_ref, bi_ref, cr_ref, ci_ref, tab_ref,
             du_ref, dbr_ref, dbi_ref, dcr_ref, dci_ref, da_ref, lam_r, lam_i, carry):
        t = pl.program_id(1)

        @pl.when(t == 0)
        def _():
            carry[...] = jnp.zeros_like(carry)
            dbr_ref[...] = jnp.zeros_like(dbr_ref)
            dbi_ref[...] = jnp.zeros_like(dbi_ref)
            dcr_ref[...] = jnp.zeros_like(dcr_ref)
            dci_ref[...] = jnp.zeros_like(dci_ref)
            da_ref[...] = jnp.zeros_like(da_ref)

        dyv = dy_ref[...]
        dyh = dyv.astype(BF16)
        lam_r[...] = lax.dot_general(dyh, cr_ref[0], NT, preferred_element_type=F32)
        lam_i[...] = -lax.dot_general(dyh, ci_ref[0], NT, preferred_element_type=F32)
        tabs = [tab_ref[n] for n in range(8)]

        def slab(n, c):
            r0 = pl.multiple_of((nsl - 1 - n) * SLAB, SLAB)
            lr, li = _slab_scan(lam_r[pl.ds(r0, SLAB), :], lam_i[pl.ds(r0, SLAB), :], tabs, c[0], c[1], True)
            lam_r[pl.ds(r0, SLAB), :] = lr
            lam_i[pl.ds(r0, SLAB), :] = li
            return (jnp.broadcast_to(lr[0:1], lr.shape), jnp.broadcast_to(li[0:1], li.shape))

        cr, ci = lax.fori_loop(0, nsl, slab, (carry[0], carry[1]))
        carry[0] = cr
        carry[1] = ci
        lr, li = lam_r[...], lam_i[...]
        lrh, lih = lr.astype(BF16), li.astype(BF16)
        du = (dyv * d_ref[...] + lax.dot_general(lrh, br_ref[0], NT, preferred_element_type=F32)
              + lax.dot_general(lih, bi_ref[0], NT, preferred_element_type=F32))
        du_ref[...] = du.astype(BF16)
        ub = u_ref[...].astype(BF16)
        dbr_ref[0] += lax.dot_general(ub, lrh, TN, preferred_element_type=F32)
        dbi_ref[0] += lax.dot_general(ub, lih, TN, preferred_element_type=F32)
        srv, siv = sr_ref[...], si_ref[...]
        dcr_ref[0] += lax.dot_general(srv.astype(BF16), dyh, TN, preferred_element_type=F32)
        dci_ref[0] -= lax.dot_general(siv.astype(BF16), dyh, TN, preferred_element_type=F32)
        first = lax.broadcasted_iota(jnp.int32, srv.shape, 0) == 0
        pm = (t < nt - 1).astype(F32)
        spr = jnp.where(first, pr_ref[SLAB - 1:SLAB, :] * pm, pltpu.roll(srv, 1, 0))
        spi = jnp.where(first, pi_ref[SLAB - 1:SLAB, :] * pm, pltpu.roll(siv, 1, 0))
        da_ref[0:1, :] += jnp.sum(lr * spr + li * spi, axis=0, keepdims=True)
        da_ref[1:2, :] += jnp.sum(li * spr - lr * spi, axis=0, keepdims=True)

    rv = lambda t: nt - 1 - t
    us = pl.BlockSpec((tbk, LANE), lambda j, t: (rv(t), j))
    ds = pl.BlockSpec((1, LANE), lambda j, t: (0, j))
    ss = pl.BlockSpec((tbk, SW), lambda j, t: (rv(t), j))
    ps = pl.BlockSpec((SLAB, SW), lambda j, t: (jnp.maximum(rv(t) * rbk - 1, 0), j))
    bs = pl.BlockSpec((1, LANE, SW), lambda j, t: (j, 0, 0))
    cs = pl.BlockSpec((1, SW, LANE), lambda j, t: (j, 0, 0))
    ts = pl.BlockSpec((8, SLAB, SW), lambda j, t: (0, 0, j))
    das = pl.BlockSpec((2, SW), lambda j, t: (0, j))
    return pl.pallas_call(
        body, name=name, grid=(NJ, nt),
        in_specs=[us, us, ds, ss, ss, ps, ps, bs, bs, cs, cs, ts],
        out_specs=[us, bs, bs, cs, cs, das],
        out_shape=[jax.ShapeDtypeStruct((s, D_MODEL), BF16),
                   jax.ShapeDtypeStruct((NJ, LANE, SW), F32), jax.ShapeDtypeStruct((NJ, LANE, SW), F32),
                   jax.ShapeDtypeStruct((NJ, SW, LANE), F32), jax.ShapeDtypeStruct((NJ, SW, LANE), F32),
                   jax.ShapeDtypeStruct((2, NST), F32)],
        scratch_shapes=[pltpu.VMEM((tbk, SW), F32), pltpu.VMEM((tbk, SW), F32), pltpu.VMEM((2, SLAB, SW), F32)],
        compiler_params=_cparams("parallel", "arbitrary"))(
            u, dy, dskip, st_r, st_i, st_r, st_i, bbd_r, bbd_i, cbd_r, cbd_i, tabrev)


GELU_C = math.sqrt(2.0 / math.pi)
GELU_A = 0.044715


def s5post_fwd(y, u, dskip, *, name):
    s = y.shape[0]
    tm = _rows(s)

    def body(y_ref, u_ref, d_ref, o_ref):
        z = y_ref[...] + d_ref[...] * u_ref[...]
        o_ref[...] = (0.5 * z * (1.0 + jnp.tanh(GELU_C * (z + GELU_A * z * z * z)))).astype(BF16)

    row = pl.BlockSpec((tm, D_MODEL), lambda i: (i, 0))
    vec = pl.BlockSpec((1, D_MODEL), lambda i: (0, 0))
    return pl.pallas_call(body, name=name, grid=(s // tm,), in_specs=[row, row, vec], out_specs=row,
                          out_shape=jax.ShapeDtypeStruct((s, D_MODEL), BF16),
                          compiler_params=_cparams("parallel"))(y, u, dskip)


def s5post_bwd(y, u, dskip, dg, *, name):
    s = y.shape[0]
    tm = _rows(s)

    def body(y_ref, u_ref, d_ref, dg_ref, dz_ref, dd_ref):
        @pl.when(pl.program_id(0) == 0)
        def _():
            dd_ref[...] = jnp.zeros_like(dd_ref)

        uv = u_ref[...]
        z = y_ref[...] + d_ref[...] * uv
        th = jnp.tanh(GELU_C * (z + GELU_A * z * z * z))
        dgelu = 0.5 * (1.0 + th) + 0.5 * z * (1.0 - th * th) * (GELU_C * (1.0 + 3.0 * GELU_A * z * z))
        dz = dg_ref[...] * dgelu
        dz_ref[...] = dz
        dd_ref[...] += jnp.sum(dz * uv, axis=0, keepdims=True)

    row = pl.BlockSpec((tm, D_MODEL), lambda i: (i, 0))
    vec = pl.BlockSpec((1, D_MODEL), lambda i: (0, 0))
    return pl.pallas_call(body, name=name, grid=(s // tm,), in_specs=[row, row, vec, row], out_specs=[row, vec],
                          out_shape=[jax.ShapeDtypeStruct((s, D_MODEL), F32), jax.ShapeDtypeStruct((1, D_MODEL), F32)],
                          compiler_params=_cparams("arbitrary"))(y, u, dskip, dg)


def glu_fwd(glu, x, *, name):
    s = x.shape[0]
    tm = _rows(s)

    def body(a_ref, b_ref, x_ref, o_ref):
        o_ref[...] = x_ref[...] + a_ref[...] * jax.nn.sigmoid(b_ref[...])

    row = pl.BlockSpec((tm, D_MODEL), lambda i: (i, 0))
    return pl.pallas_call(body, name=name, grid=(s // tm,),
                          in_specs=[row, pl.BlockSpec((tm, D_MODEL), lambda i: (i, 1)), row], out_specs=row,
                          out_shape=jax.ShapeDtypeStruct((s, D_MODEL), F32),
                          compiler_params=_cparams("parallel"))(glu, glu, x)


def glu_bwd(glu, dx, *, name):
    s = dx.shape[0]
    tm = _rows(s)

    def body(a_ref, b_ref, dx_ref, o_ref):
        sg = jax.nn.sigmoid(b_ref[...])
        dxv = dx_ref[...]
        o_ref[:, :D_MODEL] = (dxv * sg).astype(BF16)
        o_ref[:, D_MODEL:] = (dxv * a_ref[...] * sg * (1.0 - sg)).astype(BF16)

    row = pl.BlockSpec((tm, D_MODEL), lambda i: (i, 0))
    return pl.pallas_call(body, name=name, grid=(s // tm,),
                          in_specs=[row, pl.BlockSpec((tm, D_MODEL), lambda i: (i, 1)), row],
                          out_specs=pl.BlockSpec((tm, 2 * D_MODEL), lambda i: (i, 0)),
                          out_shape=jax.ShapeDtypeStruct((s, 2 * D_MODEL), BF16),
                          compiler_params=_cparams("parallel"))(glu, glu, dx)


def loss_head(y, target, *, name):
    s = y.shape[0]
    tm = _rows(s)

    def body(y_ref, t_ref, dy_ref, dyh_ref, l_ref):
        @pl.when(pl.program_id(0) == 0)
        def _():
            l_ref[...] = jnp.zeros_like(l_ref)

        e = y_ref[...] - t_ref[...]
        dy_ref[...] = e * (1.0 / D_MODEL)
        dyh_ref[...] = (e * (1.0 / D_MODEL)).astype(BF16)
        e2 = jnp.sum((e * e).reshape(tm // 8, 8, D_MODEL), axis=0)
        acc = e2[:, 0:LANE]
        for kk in range(1, D_MODEL // LANE):
            acc = acc + e2[:, kk * LANE:(kk + 1) * LANE]
        l_ref[...] += acc

    row = pl.BlockSpec((tm, D_MODEL), lambda i: (i, 0))
    return pl.pallas_call(body, name=name, grid=(s // tm,), in_specs=[row, row],
                          out_specs=[row, row, pl.BlockSpec((8, LANE), lambda i: (0, 0))],
                          out_shape=[jax.ShapeDtypeStruct((s, D_MODEL), F32), jax.ShapeDtypeStruct((s, D_MODEL), BF16),
                                     jax.ShapeDtypeStruct((8, LANE), F32)],
                          compiler_params=_cparams("arbitrary"))(y, target)


PACKW = 1024
NCHIP = 4


def _mesh_pos():
    return lax.axis_index("x"), lax.axis_index("y"), lax.axis_index("c")


def _chip_exchange(bufs, scatter, name):
    n = len(bufs)
    shapes = [b.shape[1:] if scatter else b.shape for b in bufs]

    def body(*refs):
        ins, outs = refs[:n], refs[n:2 * n]
        send_sems, recv_sems, local_sems = refs[2 * n:]
        x, y, c = _mesh_pos()
        me = 2 * x + y
        peers = [(1 - x, y), (x, 1 - y), (1 - x, 1 - y)]

        def copy(a, j, px, py, dst_slot):
            src = ins[a].at[2 * px + py] if scatter else ins[a]
            return pltpu.make_async_remote_copy(src_ref=src, dst_ref=outs[a].at[dst_slot],
                                                send_sem=send_sems.at[3 * a + j], recv_sem=recv_sems.at[3 * a + j],
                                                device_id=(px, py, c), device_id_type=MESH)

        mine = [pltpu.make_async_copy(ins[a].at[me] if scatter else ins[a], outs[a].at[me], local_sems.at[a])
                for a in range(n)]
        sends = [copy(a, j, px, py, me) for a in range(n) for j, (px, py) in enumerate(peers)]
        for cp in mine + sends:
            cp.start()
        for a in range(n):
            for j, (px, py) in enumerate(peers):
                copy(a, j, px, py, 2 * px + py).wait_recv()
        for cp in sends:
            cp.wait_send()
        for cp in mine:
            cp.wait()

    return pl.pallas_call(
        body, name=name, in_specs=[ANY] * n, out_specs=[ANY] * n,
        out_shape=[jax.ShapeDtypeStruct((NCHIP,) + tuple(shp), b.dtype) for shp, b in zip(shapes, bufs)],
        scratch_shapes=[pltpu.SemaphoreType.DMA((3 * n,)), pltpu.SemaphoreType.DMA((3 * n,)),
                        pltpu.SemaphoreType.DMA((n,))],
    )(*bufs)


def sibling_swap(bufs, name):
    n = len(bufs)

    def body(*refs):
        ins, outs, send_sems, recv_sems = refs[:n], refs[n:2 * n], refs[2 * n], refs[2 * n + 1]
        x, y, c = _mesh_pos()
        cps = [pltpu.make_async_remote_copy(src_ref=ins[k], dst_ref=outs[k], send_sem=send_sems.at[k],
                                            recv_sem=recv_sems.at[k], device_id=(x, y, 1 - c), device_id_type=MESH)
               for k in range(n)]
        for cp in cps:
            cp.start()
        for cp in cps:
            cp.wait()

    return pl.pallas_call(
        body, name=name, in_specs=[ANY] * n, out_specs=[ANY] * n,
        out_shape=[jax.ShapeDtypeStruct(b.shape, b.dtype) for b in bufs],
        scratch_shapes=[pltpu.SemaphoreType.DMA((n,)), pltpu.SemaphoreType.DMA((n,))],
    )(*bufs)


EW_VMEM_BUDGET = 20 * 1024 * 1024


def _ew_rows(rows, w, bytes_per_elem):
    wpad = -(-w // LANE) * LANE
    for t in (1024, 512, 256, 128, 64, 32, 16, 8):
        if rows % t == 0 and 2 * t * wpad * bytes_per_elem <= EW_VMEM_BUDGET:
            return t
    return rows


def sum_slots(buf, *, name):
    _, rows, w = buf.shape
    tm = _ew_rows(rows, w, NCHIP * buf.dtype.itemsize + 4)

    def body(b_ref, o_ref):
        acc = b_ref[0].astype(F32)
        for kk in range(1, NCHIP):
            acc = acc + b_ref[kk].astype(F32)
        o_ref[...] = acc

    return pl.pallas_call(body, name=name, grid=(rows // tm,),
                          in_specs=[pl.BlockSpec((NCHIP, tm, w), lambda i: (0, i, 0))],
                          out_specs=pl.BlockSpec((tm, w), lambda i: (i, 0)),
                          out_shape=jax.ShapeDtypeStruct((rows, w), F32),
                          compiler_params=_cparams("parallel"))(buf)


def adamw(p_mine, p_other, w, m, v, *, name):
    rows, wd = w.shape
    tm = _ew_rows(rows, wd, 9 * 4)
    c1 = 1.0 - ADAM_B1 ** ADAM_STEP
    c2 = 1.0 - ADAM_B2 ** ADAM_STEP

    def body(a_ref, b_ref, w_ref, m_ref, v_ref, g_ref, d_ref, nm_ref, nv_ref):
        g = a_ref[...] + b_ref[...]
        nm = ADAM_B1 * m_ref[...] + (1.0 - ADAM_B1) * g
        nv = ADAM_B2 * v_ref[...] + (1.0 - ADAM_B2) * (g * g)
        g_ref[...] = g
        nm_ref[...] = nm
        nv_ref[...] = nv
        d_ref[...] = -ADAM_LR * ((nm / c1) / (jnp.sqrt(nv / c2) + ADAM_EPS) + ADAM_WD * w_ref[...])

    row = pl.BlockSpec((tm, wd), lambda i: (i, 0))
    return pl.pallas_call(body, name=name, grid=(rows // tm,), in_specs=[row] * 5, out_specs=[row] * 4,
                          out_shape=[jax.ShapeDtypeStruct((rows, wd), F32)] * 4,
                          compiler_params=_cparams("parallel"))(p_mine, p_other, w, m, v)


def _rows2d(a, lead=0):
    tail = a.shape[lead:]
    n = int(np.prod(tail))
    if tail[-1] < LANE // 2 and n % (8 * LANE) == 0:
        return a.reshape(a.shape[:lead] + (n // (8 * LANE), 8 * LANE))
    return a.reshape(a.shape[:lead] + (-1, tail[-1]))


BIG = [("mix_w_in", 2), ("w_uq", 2), ("w_ukv", 2), ("mix_w_out", 1), ("ssm_w_in", 1), ("w_glu", 2),
       ("ffn_w_up", 2), ("ffn_w_down", 1)]
SMALL = [("sconv_w", 2), ("ssm_norm", 1), ("d_skip", 1), ("ffn_conv_w", 2)]
REPL = ["attn_norm", "cq_norm", "ckv_norm", "q_gain", "k_gain", "lambda_re", "lambda_im", "log_step",
        "b_re", "b_im", "c_re", "c_im", "ffn_norm"]
ORDER = ["attn_norm", "mix_w_in", "cq_norm", "ckv_norm", "w_uq", "w_ukv", "q_gain", "k_gain", "sconv_w", "mix_w_out",
         "ssm_norm", "ssm_w_in", "lambda_re", "lambda_im", "log_step", "b_re", "b_im", "c_re", "c_im", "d_skip",
         "w_glu", "ffn_norm", "ffn_w_up", "ffn_conv_w", "ffn_w_down"]


def _join(g, axis):
    return jnp.concatenate([g[k] for k in range(NCHIP)], axis=axis)


def _split(full, axis):
    return jnp.stack(jnp.split(full, NCHIP, axis=axis))


def _discretize(lr, li, ls, b_re, b_im):
    dt = jnp.exp(ls)[:, None]
    mag = jnp.exp(lr * dt)
    ar, ai = mag * jnp.cos(li * dt), mag * jnp.sin(li * dt)
    nr, ni = ar - 1.0, ai
    den = lr * lr + li * li
    zr, zi = (nr * lr + ni * li) / den, (ni * lr - nr * li) / den
    bbar_r = zr[..., None] * b_re - zi[..., None] * b_im
    bbar_i = zr[..., None] * b_im + zi[..., None] * b_re
    return ar, ai, bbar_r, bbar_i


def _b_blockdiag(bbar):
    gl = G // NJ
    bb = bbar.reshape(NJ, gl, P, GC).transpose(0, 1, 3, 2)
    return jnp.einsum("jgcp,gh->jgchp", bb, jnp.eye(gl, dtype=bbar.dtype)).reshape(NJ, gl * GC, gl * P)


def _b_blockdiag_t(dbd):
    gl = G // NJ
    d = jnp.einsum("jgchp,gh->jgcp", dbd.reshape(NJ, gl, GC, gl, P), jnp.eye(gl, dtype=dbd.dtype))
    return d.transpose(0, 1, 3, 2).reshape(G, P, GC)


def _c_blockdiag(cmat):
    gl = G // NJ
    cc = cmat.reshape(NJ, gl, GC, P).transpose(0, 1, 3, 2)
    return jnp.einsum("jgpc,gh->jgphc", cc, jnp.eye(gl, dtype=cmat.dtype)).reshape(NJ, gl * P, gl * GC)


def _c_blockdiag_t(dbd):
    gl = G // NJ
    d = jnp.einsum("jgphc,gh->jgpc", dbd.reshape(NJ, gl, P, gl, GC), jnp.eye(gl, dtype=dbd.dtype))
    return d.transpose(0, 1, 3, 2).reshape(G, GC, P)


def _pad_heads_cols(w, width):
    r = w.shape[0]
    return jnp.pad(w.reshape(r, HEADS, width), ((0, 0), (0, 0), (0, HP - width))).reshape(r, HEADS * HP)


def _unpad_heads_cols(w, width):
    r = w.shape[0]
    return w.reshape(r, HEADS, HP)[:, :, :width].reshape(r, HEADS * width)


W_IN_SPLIT = (QR + KVR, QR + KVR + ROPE)


def _w_in_layout(w):
    a, b = W_IN_SPLIT
    kr = jnp.pad(w[:, a:b], ((0, 0), (NOPE, HP - QK)))
    return jnp.concatenate([w[:, :a], w[:, b:], kr], axis=1)


def _w_in_layout_t(dw):
    a, b = W_IN_SPLIT
    n = dw.shape[1] - HP
    return jnp.concatenate([dw[:, :a], dw[:, n + NOPE:n + QK], dw[:, a:n]], axis=1)


def _ffn_fwd(x, l, wt, name):
    h = rms_fwd(x, wt["ffn_norm"][l][None], name=f"{name}_norm")
    up = mm(h, wt["ffn_w_up"][l], name=f"{name}_up")
    act = ffnact_fwd(up, wt["ffn_conv_w"][l], name=f"{name}_act")
    out = mm(act, wt["ffn_w_down"][l], add=x, name=f"{name}_down")
    return out, (x, h, up, act)


def _ffn_bwd(dout, douth, saved, l, wt, name):
    x, h, up, act = saved
    g = {}
    dact = mm(douth, wt["ffn_w_down"][l], tb=True, name=f"{name}_ddown")
    g["ffn_w_down"] = mm(act, douth, ta=True, out_dtype=BF16, name=f"{name}_dwdown")
    dup, g["ffn_conv_w"] = ffnact_bwd(up, wt["ffn_conv_w"][l], dact, name=f"{name}_dact")
    g["ffn_w_up"] = mm(h, dup, ta=True, out_dtype=BF16, name=f"{name}_dwup")
    dh = mm(dup, wt["ffn_w_up"][l], tb=True, name=f"{name}_dup")
    dx, dxh, dg = rms_bwd(x, wt["ffn_norm"][l][None], dh, add=dout, twin=True, name=f"{name}_dnorm")
    g["ffn_norm"] = dg[0]
    return dx, dxh, g


def _even_fwd(x, i, wt, tabs, name):
    h = rms_fwd(x, wt["attn_norm"][i][None], name=f"{name}_norm")
    proj = mm(h, wt["w_in2"][i], name=f"{name}_in")
    cqn = rms_fwd(proj, wt["cq_norm"][i][None], col=0, name=f"{name}_cqnorm")
    ckvn = rms_fwd(proj, wt["ckv_norm"][i][None], col=1, name=f"{name}_ckvnorm")
    qraw = mm(cqn, wt["w_uq_p"][i], name=f"{name}_uq")
    kv = mm(ckvn, wt["w_ukv_p"][i], name=f"{name}_ukv")
    q, k, v = qkprep_fwd(qraw, kv, proj, wt["q_gain_p"][i], wt["k_gain_p"][i], tabs, name=f"{name}_qkprep")
    o, oh, lse, lset = attn_fwd(q, k, v, name=f"{name}_attn")
    conv = sconv_fwd(proj, wt["sconv_w"][i], name=f"{name}_sconv")
    t = mm(oh, wt["w_out_a"][i], add=x, name=f"{name}_outa")
    out = mm(conv, wt["w_out_c"][i], add=t, name=f"{name}_outc")
    return out, (x, h, proj, cqn, ckvn, qraw, kv, q, k, v, o, oh, lse, lset, conv)


def _even_bwd(dout, douth, saved, i, wt, tabs, name):
    x, h, proj, cqn, ckvn, qraw, kv, q, k, v, o, oh, lse, lset, conv = saved
    g = {}
    do = mm(douth, wt["w_out_a"][i], tb=True, name=f"{name}_douta")
    dconv = mm(douth, wt["w_out_c"][i], tb=True, name=f"{name}_doutc")
    g["w_out_a"] = mm(oh, douth, ta=True, out_dtype=BF16, name=f"{name}_dwouta")
    g["w_out_c"] = mm(conv, douth, ta=True, out_dtype=BF16, name=f"{name}_dwoutc")
    dgates, g["sconv_w"] = sconv_bwd(proj, wt["sconv_w"][i], dconv, name=f"{name}_dsconv")
    dq, doh, deltat = attn_bwd_dq(q, k, v, o, lse, do, name=f"{name}_dattn_q")
    dk, dv = attn_bwd_dkv(q, k, v, doh, lset, deltat, name=f"{name}_dattn_kv")
    dqraw, dkraw, dkrope, dqg, dkg = qkprep_bwd(qraw, kv, proj, wt["q_gain_p"][i], wt["k_gain_p"][i], tabs, dq, dk,
                                                name=f"{name}_dqkprep")
    g["q_gain"], g["k_gain"] = dqg[0, :QK], dkg[0, :QK]
    dcqn = mm(dqraw, wt["w_uq_p"][i], tb=True, name=f"{name}_duq")
    g["w_uq_p"] = mm(cqn, dqraw, ta=True, out_dtype=BF16, name=f"{name}_dwuq")
    dkv = jnp.concatenate([dkraw, dv], axis=1)
    dckvn = mm(dkv, wt["w_ukv_p"][i], tb=True, name=f"{name}_dukv")
    g["w_ukv_p"] = mm(ckvn, dkv, ta=True, out_dtype=BF16, name=f"{name}_dwukv")
    dcq, dgq = rms_bwd(proj, wt["cq_norm"][i][None], dcqn, col=0, out_dtype=BF16, name=f"{name}_dcqnorm")
    dckv, dgkv = rms_bwd(proj, wt["ckv_norm"][i][None], dckvn, col=1, out_dtype=BF16, name=f"{name}_dckvnorm")
    g["cq_norm"], g["ckv_norm"] = dgq[0], dgkv[0]
    dproj = jnp.concatenate([dcq, dckv, dgates, dkrope.astype(BF16)], axis=1)
    g["w_in2"] = mm(h, dproj, ta=True, out_dtype=BF16, name=f"{name}_dwin")
    dh = mm(dproj, wt["w_in2"][i], tb=True, name=f"{name}_din")
    dx, dxh, dg = rms_bwd(x, wt["attn_norm"][i][None], dh, add=dout, twin=True, name=f"{name}_dnorm")
    g["attn_norm"] = dg[0]
    return dx, dxh, g


def _odd_fwd(x, i, wt, name):
    h = rms_fwd(x, wt["ssm_norm"][i][None], name=f"{name}_norm")
    u = mm(h, wt["ssm_w_in"][i], name=f"{name}_in")
    y, st_r, st_i = s5_fwd(u, wt["bbd_r"][i], wt["bbd_i"][i], wt["cbd_r"][i], wt["cbd_i"][i], wt["tab_f"][i],
                           name=f"{name}_scan")
    gl = s5post_fwd(y, u, wt["d_skip"][i][None], name=f"{name}_gelu")
    glu = mm(gl, wt["w_glu"][i], name=f"{name}_glu")
    out = glu_fwd(glu, x, name=f"{name}_gate")
    return out, (x, h, u, y, st_r, st_i, gl, glu)


def _odd_bwd(dout, douth, saved, i, wt, name):
    x, h, u, y, st_r, st_i, gl, glu = saved
    g = {}
    dglu = glu_bwd(glu, dout, name=f"{name}_dgate")
    g["w_glu"] = mm(gl, dglu, ta=True, out_dtype=BF16, name=f"{name}_dwglu")
    dgl = mm(dglu, wt["w_glu"][i], tb=True, name=f"{name}_dglu")
    dz, dd = s5post_bwd(y, u, wt["d_skip"][i][None], dgl, name=f"{name}_dgelu")
    g["d_skip"] = dd[0]
    du, g["bbd_r"], g["bbd_i"], g["cbd_r"], g["cbd_i"], g["a"] = s5_bwd(
        u, dz, wt["d_skip"][i][None], st_r, st_i, wt["bbd_r"][i], wt["bbd_i"][i], wt["cbd_r"][i], wt["cbd_i"][i],
        wt["tab_r"][i], name=f"{name}_dscan")
    g["ssm_w_in"] = mm(h, du, ta=True, out_dtype=BF16, name=f"{name}_dwin")
    dh = mm(du, wt["ssm_w_in"][i], tb=True, name=f"{name}_din")
    dx, dxh, dg = rms_bwd(x, wt["ssm_norm"][i][None], dh, add=dout, twin=True, name=f"{name}_dnorm")
    g["ssm_norm"] = dg[0]
    return dx, dxh, g


def _local_step(x, target, full):
    s = x.shape[0]
    n_even = (DEPTH + 1) // 2
    n_odd = DEPTH // 2
    tabs = _rope_tables(s)
    wt = dict(full)
    wt["w_in2"] = jnp.stack([_w_in_layout(full["mix_w_in"][i]) for i in range(n_even)])
    wt["w_uq_p"] = jnp.stack([_pad_heads_cols(full["w_uq"][i], QK) for i in range(n_even)])
    ukv = full["w_ukv"].reshape(n_even, KVR, HEADS, NOPE + VD)
    wt["w_ukv_p"] = jnp.stack([jnp.concatenate(
        [_pad_heads_cols(ukv[i, :, :, :NOPE].reshape(KVR, HEADS * NOPE), NOPE),
         _pad_heads_cols(ukv[i, :, :, NOPE:].reshape(KVR, HEADS * VD), VD)], axis=1) for i in range(n_even)])
    wt["w_out_a"] = jnp.stack([_pad_heads_cols(full["mix_w_out"][i, :HEADS * VD].T, VD).T for i in range(n_even)])
    wt["w_out_c"] = full["mix_w_out"][:, HEADS * VD:]
    wt["q_gain_p"] = jnp.pad(full["q_gain"], ((0, 0), (0, HP - QK)))[:, None, :]
    wt["k_gain_p"] = jnp.pad(full["k_gain"], ((0, 0), (0, HP - QK)))[:, None, :]

    disc_vjp = []
    for key in ("bbd_r", "bbd_i", "cbd_r", "cbd_i", "tab_f", "tab_r"):
        wt[key] = []
    for i in range(n_odd):
        (ar, ai, bbr, bbi), vjp = jax.vjp(_discretize, full["lambda_re"][i], full["lambda_im"][i], full["log_step"][i],
                                          full["b_re"][i], full["b_im"][i])
        disc_vjp.append(vjp)
        tf, tr = _scan_tables(ar.reshape(-1), ai.reshape(-1))
        wt["tab_f"].append(tf)
        wt["tab_r"].append(tr)
        wt["bbd_r"].append(_b_blockdiag(bbr).astype(BF16))
        wt["bbd_i"].append(_b_blockdiag(bbi).astype(BF16))
        wt["cbd_r"].append(_c_blockdiag(full["c_re"][i]).astype(BF16))
        wt["cbd_i"].append(_c_blockdiag(full["c_im"][i]).astype(BF16))

    saved = []
    for layer in range(DEPTH):
        i = layer // 2
        if layer % 2 == 0:
            x, sm = _even_fwd(x, i, wt, tabs, f"l{layer}_mla")
        else:
            x, sm = _odd_fwd(x, i, wt, f"l{layer}_s5")
        x, sf = _ffn_fwd(x, layer, wt, f"l{layer}_ffn")
        saved.append((sm, sf))
    dx, dxh, lslab = loss_head(x, target, name="loss_head")

    grads = {n: [None] * (DEPTH if n.startswith("ffn") else n_even) for n in ORDER}
    for layer in reversed(range(DEPTH)):
        i = layer // 2
        sm, sf = saved[layer]
        dx, dxh, g = _ffn_bwd(dx, dxh, sf, layer, wt, f"l{layer}_ffn")
        for n in ("ffn_norm", "ffn_w_up", "ffn_conv_w", "ffn_w_down"):
            grads[n][layer] = g[n]
        if layer % 2 == 0:
            dx, dxh, g = _even_bwd(dx, dxh, sm, i, wt, tabs, f"l{layer}_mla")
            grads["mix_w_in"][i] = _w_in_layout_t(g["w_in2"])
            grads["w_uq"][i] = _unpad_heads_cols(g["w_uq_p"], QK)
            dk_, dv_ = g["w_ukv_p"][:, :HEADS * HP], g["w_ukv_p"][:, HEADS * HP:]
            grads["w_ukv"][i] = jnp.concatenate(
                [dk_.reshape(KVR, HEADS, HP)[:, :, :NOPE], dv_.reshape(KVR, HEADS, HP)[:, :, :VD]],
                axis=2).reshape(KVR, HEADS * (NOPE + VD))
            grads["mix_w_out"][i] = jnp.concatenate(
                [_unpad_heads_cols(g["w_out_a"].T, VD).T, g["w_out_c"]], axis=0)
            for n in ("attn_norm", "cq_norm", "ckv_norm", "q_gain", "k_gain", "sconv_w"):
                grads[n][i] = g[n]
        else:
            dx, dxh, g = _odd_bwd(dx, dxh, sm, i, wt, f"l{layer}_s5")
            dlr, dli, dls, dbr, dbi = disc_vjp[i]((g["a"][0].reshape(G, P), g["a"][1].reshape(G, P),
                                                    _b_blockdiag_t(g["bbd_r"]), _b_blockdiag_t(g["bbd_i"])))
            grads["lambda_re"][i], grads["lambda_im"][i], grads["log_step"][i] = dlr, dli, dls
            grads["b_re"][i], grads["b_im"][i] = dbr, dbi
            grads["c_re"][i], grads["c_im"][i] = _c_blockdiag_t(g["cbd_r"]), _c_blockdiag_t(g["cbd_i"])
            for n in ("ssm_norm", "ssm_w_in", "d_skip", "w_glu"):
                grads[n][i] = g[n]
    grads = {n: jnp.stack(v) for n, v in grads.items()}
    return jnp.sum(lslab), dx, grads


def kernel(x, attn_norm, mix_w_in, cq_norm, ckv_norm, w_uq, w_ukv, q_gain, k_gain, sconv_w, mix_w_out, ssm_norm, ssm_w_in, lambda_re, lambda_im, log_step, b_re, b_im, c_re, c_im, d_skip, w_glu, ffn_norm, ffn_w_up, ffn_conv_w, ffn_w_down, loss_target, m_attn_norm, m_mix_w_in, m_cq_norm, m_ckv_norm, m_w_uq, m_w_ukv, m_q_gain, m_k_gain, m_sconv_w, m_mix_w_out, m_ssm_norm, m_ssm_w_in, m_lambda_re, m_lambda_im, m_log_step, m_b_re, m_b_im, m_c_re, m_c_im, m_d_skip, m_w_glu, m_ffn_norm, m_ffn_w_up, m_ffn_conv_w, m_ffn_w_down, v_attn_norm, v_mix_w_in, v_cq_norm, v_ckv_norm, v_w_uq, v_w_ukv, v_q_gain, v_k_gain, v_sconv_w, v_mix_w_out, v_ssm_norm, v_ssm_w_in, v_lambda_re, v_lambda_im, v_log_step, v_b_re, v_b_im, v_c_re, v_c_im, v_d_skip, v_w_glu, v_ffn_norm, v_ffn_w_up, v_ffn_conv_w, v_ffn_w_down):
    args = dict(locals())
    w = {n: args[n] for n in ORDER}
    m = {n: args["m_" + n] for n in ORDER}
    v = {n: args["v_" + n] for n in ORDER}
    me = 2 * lax.axis_index("x") + lax.axis_index("y")

    gb = _chip_exchange([w[n].astype(BF16) for n, _ in BIG], False, "gather_w_bf16")
    gs = _chip_exchange([w[n] for n, _ in SMALL], False, "gather_w_f32")
    full = {n: w[n] for n in REPL}
    for (n, ax), g in zip(BIG + SMALL, gb + gs):
        full[n] = _join(g, ax)

    sq, dx, grads = _local_step(x[0], loss_target[0], full)
    loss = lax.psum(0.5 * sq / D_MODEL, ("x", "y", "c"))

    rep_names = REPL + [n for n, _ in SMALL]
    names = [n for n, _ in BIG] + rep_names
    slots = (_chip_exchange([_split(grads[n], ax) for n, ax in BIG], True, "scatter_g_bf16")
             + _chip_exchange([_rows2d(grads[n]) for n in rep_names], False, "gather_g_f32"))
    mine = [sum_slots(_rows2d(sl, 1), name=f"sum_{n}") for n, sl in zip(names, slots)]
    other = sibling_swap(mine, "swap_g")

    def local(n, p):
        ax = dict(SMALL).get(n)
        if ax is None:
            return p
        part = lax.dynamic_index_in_dim(_split(p.reshape(grads[n].shape), ax), me, 0, keepdims=False)
        return _rows2d(part)

    outs = {}
    for n, p, q in zip(names, mine, other):
        res = adamw(local(n, p), local(n, q), _rows2d(w[n]), _rows2d(m[n]), _rows2d(v[n]), name=f"adamw_{n}")
        outs[n] = [r.reshape(w[n].shape) for r in res]
    return (loss, dx[None], *[outs[n][0] for n in ORDER], *[outs[n][1] for n in ORDER],
            *[outs[n][2] for n in ORDER], *[outs[n][3] for n in ORDER])
```

```python
import functools
import math

import numpy as np
import jax
import jax.numpy as jnp
from jax import lax
from jax.experimental import pallas as pl
from jax.experimental.pallas import tpu as pltpu

F32, BF16 = jnp.float32, jnp.bfloat16

D_MODEL = 1024
DEPTH = 4
HEADS = 8
NOPE, ROPE, QK, VD = 64, 32, 96, 64
HP = 128
QR, KVR = 256, 256
CONVC = 512
FFN_H = 2816
G, P, GC = 64, 64, 16
NST = G * P
SLAB = 8
LANE = 128
EPS = 1e-6
ROPE_THETA = 10000.0
ADAM_LR, ADAM_B1, ADAM_B2, ADAM_EPS, ADAM_WD, ADAM_STEP = 0.001, 0.9, 0.999, 1e-08, 0.01, 10
VMEM_LIMIT = 48 * 1024 * 1024
MESH = pl.DeviceIdType.MESH
ANY = pl.BlockSpec(memory_space=pl.ANY)


def _cparams(*sem):
    return pltpu.CompilerParams(dimension_semantics=sem, vmem_limit_bytes=VMEM_LIMIT)


def _pick(dim, prefs):
    for p in prefs:
        if dim % p == 0:
            return p
    return dim


def _rows(s):
    return _pick(s, (512, 256, 128, 64, 32, 16, 8))


MM_VMEM_BUDGET = 36 * 1024 * 1024
MM_MAX_TILE_ELEMS = 640 * 1024
HBM_BYTES_PER_US = 3.0e6
STEP_OVERHEAD_US = 0.35


def _lane_tiles(n):
    c = {t for t in range(LANE, min(n, 1536) + 1, LANE) if n % t == 0}
    if n <= 2304 or not c:
        c.add(n)
    return sorted(c, reverse=True)


def _mm_tiles(m, n, k, sa, sb, so):
    best = None
    for tm in [t for t in (1024, 512, 256) if m % t == 0] or [m]:
        for tn in _lane_tiles(n):
            if tm * tn > MM_MAX_TILE_ELEMS:
                continue
            if 2 * (tm * k * sa + k * tn * sb + tm * tn * so) + 4 * tm * tn > MM_VMEM_BUDGET:
                continue
            steps = (m // tm) * (n // tn)
            for inner_n in (True, False):
                moved = (m * k * sa + (m // tm) * k * n * sb) if inner_n else (k * n * sb + (n // tn) * m * k * sa)
                cost = (moved + m * n * so) / HBM_BYTES_PER_US + steps * STEP_OVERHEAD_US
                if best is None or cost < best[0]:
                    best = (cost, tm, tn, inner_n)
    return best[1:]


def mm(a, b, *, ta=False, tb=False, add=None, out_dtype=F32, name):
    if ta:
        kdim, m = a.shape
    else:
        m, kdim = a.shape
    n = b.shape[0] if tb else b.shape[1]
    so = jnp.dtype(out_dtype).itemsize + (0 if add is None else add.dtype.itemsize)
    tm, tn, inner_n = _mm_tiles(m, n, kdim, a.dtype.itemsize, b.dtype.itemsize, so)
    dn = (((0 if ta else 1,), (1 if tb else 0,)), ((), ()))

    def body(*refs):
        if add is None:
            a_ref, b_ref, o_ref = refs
        else:
            a_ref, b_ref, add_ref, o_ref = refs
        r = lax.dot_general(a_ref[...].astype(BF16), b_ref[...].astype(BF16), dn, preferred_element_type=F32)
        if add is not None:
            r = r + add_ref[...].astype(F32)
        o_ref[...] = r.astype(out_dtype)

    ij = (lambda g0, g1: (g0, g1)) if inner_n else (lambda g0, g1: (g1, g0))
    a_spec = (pl.BlockSpec((kdim, tm), lambda g0, g1: (0, ij(g0, g1)[0])) if ta
              else pl.BlockSpec((tm, kdim), lambda g0, g1: (ij(g0, g1)[0], 0)))
    b_spec = (pl.BlockSpec((tn, kdim), lambda g0, g1: (ij(g0, g1)[1], 0)) if tb
              else pl.BlockSpec((kdim, tn), lambda g0, g1: (0, ij(g0, g1)[1])))
    o_spec = pl.BlockSpec((tm, tn), lambda g0, g1: ij(g0, g1))
    ins, specs = [a, b], [a_spec, b_spec]
    if add is not None:
        ins.append(add)
        specs.append(o_spec)
    grid = (m // tm, n // tn) if inner_n else (n // tn, m // tm)
    return pl.pallas_call(
        body, name=name, grid=grid, in_specs=specs, out_specs=o_spec,
        out_shape=jax.ShapeDtypeStruct((m, n), out_dtype),
        compiler_params=_cparams("parallel", "parallel"))(*ins)


def rms_fwd(x, g, *, col=0, out_dtype=BF16, name):
    s = x.shape[0]
    d = g.shape[1]
    tm = _rows(s)

    def body(x_ref, g_ref, o_ref):
        xv = x_ref[...]
        r = lax.rsqrt(jnp.mean(xv * xv, axis=-1, keepdims=True) + EPS)
        o_ref[...] = (xv * r * g_ref[...]).astype(out_dtype)

    return pl.pallas_call(
        body, name=name, grid=(s // tm,),
        in_specs=[pl.BlockSpec((tm, d), lambda i: (i, col)), pl.BlockSpec((1, d), lambda i: (0, 0))],
        out_specs=pl.BlockSpec((tm, d), lambda i: (i, 0)),
        out_shape=jax.ShapeDtypeStruct((s, d), out_dtype),
        compiler_params=_cparams("parallel"))(x, g)


def rms_bwd(x, g, dy, *, col=0, add=None, out_dtype=F32, twin=False, name):
    s = x.shape[0]
    d = g.shape[1]
    tm = _rows(s)

    def body(*refs):
        refs = list(refs)
        dg_ref = refs.pop()
        dxh_ref = refs.pop() if twin else None
        dx_ref = refs.pop()
        add_ref = refs.pop() if add is not None else None
        x_ref, g_ref, dy_ref = refs

        @pl.when(pl.program_id(0) == 0)
        def _():
            dg_ref[...] = jnp.zeros_like(dg_ref)

        xv = x_ref[...]
        dyv = dy_ref[...].astype(F32)
        r = lax.rsqrt(jnp.mean(xv * xv, axis=-1, keepdims=True) + EPS)
        xh = xv * r
        dg_ref[...] += jnp.sum(dyv * xh, axis=0, keepdims=True)
        dxh = dyv * g_ref[...]
        dx = r * (dxh - xh * jnp.mean(dxh * xh, axis=-1, keepdims=True))
        if add is not None:
            dx = dx + add_ref[...]
        dx_ref[...] = dx.astype(out_dtype)
        if twin:
            dxh_ref[...] = dx.astype(BF16)

    row = pl.BlockSpec((tm, d), lambda i: (i, 0))
    vec = pl.BlockSpec((1, d), lambda i: (0, 0))
    ins = [x, g, dy]
    specs = [pl.BlockSpec((tm, d), lambda i: (i, col)), vec, row]
    if add is not None:
        ins.append(add)
        specs.append(row)
    dxs = [jax.ShapeDtypeStruct((s, d), out_dtype)] + ([jax.ShapeDtypeStruct((s, d), BF16)] if twin else [])
    return pl.pallas_call(
        body, name=name, grid=(s // tm,), in_specs=specs,
        out_specs=[row] * len(dxs) + [vec],
        out_shape=dxs + [jax.ShapeDtypeStruct((1, d), F32)],
        compiler_params=_cparams("arbitrary"))(*ins)


def _rope_tables(s):
    inv = 1.0 / (ROPE_THETA ** (jnp.arange(0, ROPE, 2, dtype=F32) / ROPE))
    ang = jnp.arange(s, dtype=F32)[:, None] * inv[None, :]
    cos, sin = jnp.cos(ang), jnp.sin(ang)
    z = lambda w: jnp.zeros((s, w), F32)
    c = jnp.concatenate([jnp.ones((s, NOPE), F32), cos, cos, z(HP - QK)], axis=1)
    s1 = jnp.concatenate([z(NOPE), -sin, z(HP - NOPE - ROPE // 2)], axis=1)
    s2 = jnp.concatenate([z(NOPE + ROPE // 2), sin, z(HP - QK)], axis=1)
    return c, s1, s2


def qkprep_fwd(qraw, kv, proj, qg, kg, tabs, *, name):
    s = qraw.shape[0]
    tm = _rows(s)
    kr_col = (proj.shape[1] - HP) // HP

    def body(q_ref, k_ref, v_ref, kr_ref, qg_ref, kg_ref, c_ref, s1_ref, s2_ref, qo_ref, ko_ref, vo_ref):
        c, s1, s2 = c_ref[...], s1_ref[...], s2_ref[...]

        def f(xv, gain):
            r = lax.rsqrt(jnp.sum(xv * xv, axis=-1, keepdims=True) * (1.0 / QK) + EPS)
            xn = xv * r * gain
            return xn * c + pltpu.roll(xn, HP - ROPE // 2, 1) * s1 + pltpu.roll(xn, ROPE // 2, 1) * s2

        qo_ref[...] = f(q_ref[...], qg_ref[...]).astype(BF16)
        ko_ref[...] = f(k_ref[...] + kr_ref[...], kg_ref[...]).astype(BF16)
        vv = v_ref[...]
        lane = lax.broadcasted_iota(jnp.int32, vv.shape, 1)
        vo_ref[...] = jnp.where(lane == VD, 1.0, vv).astype(BF16)

    head = pl.BlockSpec((tm, HP), lambda i, h: (i, h))
    tab = pl.BlockSpec((tm, HP), lambda i, h: (i, 0))
    gain = pl.BlockSpec((1, HP), lambda i, h: (0, 0))
    return pl.pallas_call(
        body, name=name, grid=(s // tm, HEADS),
        in_specs=[head, head, pl.BlockSpec((tm, HP), lambda i, h: (i, HEADS + h)),
                  pl.BlockSpec((tm, HP), lambda i, h: (i, kr_col)), gain, gain, tab, tab, tab],
        out_specs=[head, head, head],
        out_shape=[jax.ShapeDtypeStruct((s, HEADS * HP), BF16)] * 3,
        compiler_params=_cparams("parallel", "parallel"))(qraw, kv, kv, proj, qg, kg, *tabs)


def qkprep_bwd(qraw, kv, proj, qg, kg, tabs, dq, dk, *, name):
    s = qraw.shape[0]
    tm = _rows(s)
    kr_col = (proj.shape[1] - HP) // HP

    def body(q_ref, k_ref, kr_ref, qg_ref, kg_ref, c_ref, s1_ref, s2_ref, dq_ref, dk_ref,
             dqr_ref, dkr_ref, dkrope_ref, dqg_ref, dkg_ref):
        i, h = pl.program_id(0), pl.program_id(1)
        c, s1, s2 = c_ref[...], s1_ref[...], s2_ref[...]

        @pl.when((i == 0) & (h == 0))
        def _():
            dqg_ref[...] = jnp.zeros_like(dqg_ref)
            dkg_ref[...] = jnp.zeros_like(dkg_ref)

        @pl.when(h == 0)
        def _():
            dkrope_ref[...] = jnp.zeros_like(dkrope_ref)

        def f(xv, gain, dout):
            r = lax.rsqrt(jnp.sum(xv * xv, axis=-1, keepdims=True) * (1.0 / QK) + EPS)
            xh = xv * r
            dxn = dout * c + pltpu.roll(dout * s1, ROPE // 2, 1) + pltpu.roll(dout * s2, HP - ROPE // 2, 1)
            dgain = jnp.sum(dxn * xh, axis=0, keepdims=True)
            dxh = dxn * gain
            dx = r * (dxh - xh * (jnp.sum(dxh * xh, axis=-1, keepdims=True) * (1.0 / QK)))
            return dx, dgain

        dxq, dgq = f(q_ref[...], qg_ref[...], dq_ref[...])
        dxk, dgk = f(k_ref[...] + kr_ref[...], kg_ref[...], dk_ref[...])
        dqr_ref[...] = dxq.astype(BF16)
        dkr_ref[...] = dxk.astype(BF16)
        dqg_ref[...] += dgq
        dkg_ref[...] += dgk
        lane = lax.broadcasted_iota(jnp.int32, dxk.shape, 1)
        dkrope_ref[...] += jnp.where((lane >= NOPE) & (lane < QK), dxk, 0.0)

    head = pl.BlockSpec((tm, HP), lambda i, h: (i, h))
    tab = pl.BlockSpec((tm, HP), lambda i, h: (i, 0))
    gain = pl.BlockSpec((1, HP), lambda i, h: (0, 0))
    return pl.pallas_call(
        body, name=name, grid=(s // tm, HEADS),
        in_specs=[head, head, pl.BlockSpec((tm, HP), lambda i, h: (i, kr_col)), gain, gain, tab, tab, tab, head, head],
        out_specs=[head, head, tab, gain, gain],
        out_shape=[jax.ShapeDtypeStruct((s, HEADS * HP), BF16)] * 2
        + [jax.ShapeDtypeStruct((s, HP), F32), jax.ShapeDtypeStruct((1, HP), F32), jax.ShapeDtypeStruct((1, HP), F32)],
        compiler_params=_cparams("arbitrary", "arbitrary"))(qraw, kv, proj, qg, kg, *tabs, dq, dk)


ATT_SCALE = QK ** -0.5
NEG = -1e30


def _att_tile(s):
    return _pick(s, (512, 256, 128))


def _causal(sv, diag):
    r = lax.broadcasted_iota(jnp.int32, sv.shape, 0)
    c = lax.broadcasted_iota(jnp.int32, sv.shape, 1)
    return jnp.where(diag & (c > r), NEG, sv)


NT = (((1,), (1,)), ((), ()))
TN = (((0,), (0,)), ((), ()))


def _row_of(col):
    return jnp.broadcast_to(col, (col.shape[0], LANE)).T[0:SLAB, :]


def _att_specs(s):
    t = _att_tile(s)
    nb = s // t
    tile = pl.BlockSpec((t, HP), lambda h, i: (i, h))
    whole = pl.BlockSpec((s, HP), lambda h, i: (0, h))
    row = pl.BlockSpec((1, 1, SLAB, t), lambda h, i: (h, i, 0, 0))
    rows = pl.BlockSpec((1, nb, SLAB, t), lambda h, i: (h, 0, 0, 0))
    return t, nb, tile, whole, row, rows


def attn_fwd(q, k, v, *, name):
    s = q.shape[0]
    t, nb, tile, whole, row, _ = _att_specs(s)

    def body(q_ref, k_ref, v_ref, o_ref, oh_ref, lse_ref, lset_ref, s_scr, mb_scr, acc):
        qb = pl.program_id(1)
        qv = q_ref[...]

        def scores(j):
            r0 = pl.multiple_of(j * t, t)
            return lax.dot_general(qv, k_ref[pl.ds(r0, t), :], NT, preferred_element_type=F32) * ATT_SCALE

        def fold(sv):
            m = sv[:, 0:LANE]
            for kk in range(1, t // LANE):
                m = jnp.maximum(m, sv[:, kk * LANE:(kk + 1) * LANE])
            return m

        def first(j, m):
            sv = scores(j)
            s_scr[j] = sv
            return jnp.maximum(m, fold(sv))

        m = lax.fori_loop(0, qb, first, jnp.full((t, LANE), NEG, F32))
        sd = _causal(scores(qb), True)
        s_scr[qb] = sd
        mcol = jnp.max(jnp.maximum(m, fold(sd)), axis=-1, keepdims=True)
        mb_scr[...] = jnp.broadcast_to(mcol, (t, t))
        acc[...] = jnp.zeros_like(acc)

        def second(j, carry):
            r0 = pl.multiple_of(j * t, t)
            p = jnp.exp(s_scr[j] - mb_scr[...]).astype(BF16)
            acc[...] += jnp.dot(p, v_ref[pl.ds(r0, t), :], preferred_element_type=F32)
            return carry

        lax.fori_loop(0, qb + 1, second, 0)
        av = acc[...]
        lsum = av[:, VD:VD + 1]
        lane = lax.broadcasted_iota(jnp.int32, av.shape, 1)
        ov = jnp.where(lane == VD, 0.0, av / lsum)
        o_ref[...] = ov
        oh_ref[...] = ov.astype(BF16)
        lse = mcol + jnp.log(lsum)
        lse_ref[...] = jnp.broadcast_to(lse, lse_ref.shape)
        lset_ref[0, 0] = _row_of(lse)

    return pl.pallas_call(
        body, name=name, grid=(HEADS, nb), in_specs=[tile, whole, whole], out_specs=[tile, tile, tile, row],
        out_shape=[jax.ShapeDtypeStruct((s, HEADS * HP), F32), jax.ShapeDtypeStruct((s, HEADS * HP), BF16),
                   jax.ShapeDtypeStruct((s, HEADS * HP), F32), jax.ShapeDtypeStruct((HEADS, nb, SLAB, t), F32)],
        scratch_shapes=[pltpu.VMEM((nb, t, t), F32), pltpu.VMEM((t, t), F32), pltpu.VMEM((t, HP), F32)],
        compiler_params=_cparams("parallel", "parallel"))(q, k, v)


def attn_bwd_dq(q, k, v, o, lse, do, *, name):
    s = q.shape[0]
    t, nb, tile, whole, row, _ = _att_specs(s)

    def body(q_ref, k_ref, v_ref, o_ref, lse_ref, do_ref, dq_ref, doh_ref, dt_ref, lb_scr, db_scr, acc):
        qb = pl.program_id(1)
        qv = q_ref[...]
        dov = do_ref[...]
        dob = dov.astype(BF16)
        doh_ref[...] = dob
        delta = jnp.sum(dov * o_ref[...], axis=-1, keepdims=True)
        dt_ref[0, 0] = _row_of(delta)
        lb_scr[...] = jnp.broadcast_to(lse_ref[...][:, 0:1], (t, t))
        db_scr[...] = jnp.broadcast_to(delta, (t, t))
        acc[...] = jnp.zeros_like(acc)

        def step(j, diag):
            r0 = pl.multiple_of(j * t, t)
            kj = k_ref[pl.ds(r0, t), :]
            sv = lax.dot_general(qv, kj, NT, preferred_element_type=F32) * ATT_SCALE
            if diag:
                sv = _causal(sv, True)
            p = jnp.exp(sv - lb_scr[...])
            dp = lax.dot_general(dob, v_ref[pl.ds(r0, t), :], NT, preferred_element_type=F32)
            ds = (p * (dp - db_scr[...])).astype(BF16)
            acc[...] += jnp.dot(ds, kj, preferred_element_type=F32)

        def off_diag(j, carry):
            step(j, False)
            return carry

        lax.fori_loop(0, qb, off_diag, 0)
        step(qb, True)
        dq_ref[...] = acc[...] * ATT_SCALE

    return pl.pallas_call(
        body, name=name, grid=(HEADS, nb), in_specs=[tile, whole, whole, tile, tile, tile],
        out_specs=[tile, tile, row],
        out_shape=[jax.ShapeDtypeStruct((s, HEADS * HP), F32), jax.ShapeDtypeStruct((s, HEADS * HP), BF16),
                   jax.ShapeDtypeStruct((HEADS, nb, SLAB, t), F32)],
        scratch_shapes=[pltpu.VMEM((t, t), F32), pltpu.VMEM((t, t), F32), pltpu.VMEM((t, HP), F32)],
        compiler_params=_cparams("parallel", "parallel"))(q, k, v, o, lse, do)


def attn_bwd_dkv(q, k, v, doh, lset, deltat, *, name):
    s = q.shape[0]
    t, nb, tile, whole, _, rows = _att_specs(s)

    def body(q_ref, k_ref, v_ref, do_ref, lt_ref, dt_ref, dk_ref, dv_ref, dk_acc, dv_acc):
        kb = pl.program_id(1)
        kt, vt = k_ref[...], v_ref[...]
        dk_acc[...] = jnp.zeros_like(dk_acc)
        dv_acc[...] = jnp.zeros_like(dv_acc)

        def step(i, diag):
            r0 = pl.multiple_of(i * t, t)
            qi, doi = q_ref[pl.ds(r0, t), :], do_ref[pl.ds(r0, t), :]
            st = lax.dot_general(kt, qi, NT, preferred_element_type=F32) * ATT_SCALE
            if diag:
                kr = lax.broadcasted_iota(jnp.int32, st.shape, 0)
                qc = lax.broadcasted_iota(jnp.int32, st.shape, 1)
                st = jnp.where(kr > qc, NEG, st)
            pt = jnp.exp(st - lt_ref[0, i][0:1, :])
            dpt = lax.dot_general(vt, doi, NT, preferred_element_type=F32)
            dst = (pt * (dpt - dt_ref[0, i][0:1, :])).astype(BF16)
            dv_acc[...] += jnp.dot(pt.astype(BF16), doi, preferred_element_type=F32)
            dk_acc[...] += jnp.dot(dst, qi, preferred_element_type=F32)

        def off_diag(i, carry):
            step(i, False)
            return carry

        step(kb, True)
        lax.fori_loop(kb + 1, nb, off_diag, 0)
        dk_ref[...] = dk_acc[...] * ATT_SCALE
        dvv = dv_acc[...]
        lane = lax.broadcasted_iota(jnp.int32, dvv.shape, 1)
        dv_ref[...] = jnp.where(lane == VD, 0.0, dvv).astype(BF16)

    return pl.pallas_call(
        body, name=name, grid=(HEADS, nb), in_specs=[whole, tile, tile, whole, rows, rows], out_specs=[tile, tile],
        out_shape=[jax.ShapeDtypeStruct((s, HEADS * HP), F32), jax.ShapeDtypeStruct((s, HEADS * HP), BF16)],
        scratch_shapes=[pltpu.VMEM((t, HP), F32), pltpu.VMEM((t, HP), F32)],
        compiler_params=_cparams("parallel", "parallel"))(q, k, v, doh, lset, deltat)


HALO = 8
CW = 256


def _conv3(zw, w):
    return w[2:3] * zw + w[1:2] * pltpu.roll(zw, 1, 0) + w[0:1] * pltpu.roll(zw, 2, 0)


def _conv3_t(dc, w):
    n = dc.shape[0]
    return w[2:3] * dc + w[1:2] * pltpu.roll(dc, n - 1, 0) + w[0:1] * pltpu.roll(dc, n - 2, 0)


def _conv3_dw(dc, zw, r):
    z0 = zw[HALO:HALO + r]
    z1 = pltpu.roll(zw, 1, 0)[HALO:HALO + r]
    z2 = pltpu.roll(zw, 2, 0)[HALO:HALO + r]
    return [jnp.sum(dc * z, axis=0, keepdims=True) for z in (z2, z1, z0)]


def _halo_specs(r, colfn):
    rb = r // HALO
    cur = pl.BlockSpec((r, CW), lambda j, i: (i, colfn(j)))
    prev = pl.BlockSpec((HALO, CW), lambda j, i: (jnp.maximum(i * rb - 1, 0), colfn(j)))

    def nxt(nrow_blocks):
        return pl.BlockSpec((HALO, CW), lambda j, i: (jnp.minimum((i + 1) * rb, nrow_blocks * rb - 1), colfn(j)))

    return cur, prev, nxt


def ffnact_fwd(up, w, *, name):
    s, c2 = up.shape
    hh = c2 // 2
    nj = hh // CW
    r = _rows(s)
    nt = s // r

    def body(g_ref, gp_ref, v_ref, vp_ref, wg_ref, wv_ref, o_ref):
        pm = (pl.program_id(1) > 0).astype(F32)
        cg = _conv3(jnp.concatenate([gp_ref[...] * pm, g_ref[...]], axis=0), wg_ref[...])[HALO:]
        cv = _conv3(jnp.concatenate([vp_ref[...] * pm, v_ref[...]], axis=0), wv_ref[...])[HALO:]
        o_ref[...] = (cg * jax.nn.sigmoid(cg) * cv).astype(BF16)

    gcur, gprev, _ = _halo_specs(r, lambda j: j)
    vcur, vprev, _ = _halo_specs(r, lambda j: nj + j)
    wg = pl.BlockSpec((3, CW), lambda j, i: (0, j))
    wv = pl.BlockSpec((3, CW), lambda j, i: (0, nj + j))
    return pl.pallas_call(
        body, name=name, grid=(nj, nt), in_specs=[gcur, gprev, vcur, vprev, wg, wv],
        out_specs=pl.BlockSpec((r, CW), lambda j, i: (i, j)),
        out_shape=jax.ShapeDtypeStruct((s, hh), BF16),
        compiler_params=_cparams("parallel", "parallel"))(up, up, up, up, w, w)


def ffnact_bwd(up, w, dact, *, name):
    s, c2 = up.shape
    hh = c2 // 2
    nj = hh // CW
    r = _rows(s)
    nt = s // r

    def body(g_ref, gp_ref, gn_ref, v_ref, vp_ref, vn_ref, wg_ref, wv_ref, da_ref, dan_ref,
             dg_ref, dv_ref, dwg_ref, dwv_ref):
        i = pl.program_id(1)
        pm = (i > 0).astype(F32)
        nm = (i < nt - 1).astype(F32)

        @pl.when(i == 0)
        def _():
            dwg_ref[...] = jnp.zeros_like(dwg_ref)
            dwv_ref[...] = jnp.zeros_like(dwv_ref)

        wg, wv = wg_ref[...], wv_ref[...]
        zg = jnp.concatenate([gp_ref[...] * pm, g_ref[...], gn_ref[...]], axis=0)
        zv = jnp.concatenate([vp_ref[...] * pm, v_ref[...], vn_ref[...]], axis=0)
        cg = _conv3(zg, wg)[HALO:]
        cv = _conv3(zv, wv)[HALO:]
        da = jnp.concatenate([da_ref[...], dan_ref[...] * nm], axis=0)
        sg = jax.nn.sigmoid(cg)
        dcg = da * cv * (sg * (1.0 + cg * (1.0 - sg)))
        dcv = da * (cg * sg)
        dg_ref[...] = _conv3_t(dcg, wg)[:r].astype(BF16)
        dv_ref[...] = _conv3_t(dcv, wv)[:r].astype(BF16)
        for kk, (a, b) in enumerate(zip(_conv3_dw(dcg[:r], zg, r), _conv3_dw(dcv[:r], zv, r))):
            dwg_ref[kk:kk + 1, :] += a
            dwv_ref[kk:kk + 1, :] += b

    gcur, gprev, gnext = _halo_specs(r, lambda j: j)
    vcur, vprev, vnext = _halo_specs(r, lambda j: nj + j)
    acur, _, anext = _halo_specs(r, lambda j: j)
    wg = pl.BlockSpec((3, CW), lambda j, i: (0, j))
    wv = pl.BlockSpec((3, CW), lambda j, i: (0, nj + j))
    dupg, dupv, dwg, dwv = pl.pallas_call(
        body, name=name, grid=(nj, nt),
        in_specs=[gcur, gprev, gnext(nt), vcur, vprev, vnext(nt), wg, wv, acur, anext(nt)],
        out_specs=[acur, acur, wg, wg],
        out_shape=[jax.ShapeDtypeStruct((s, hh), BF16), jax.ShapeDtypeStruct((s, hh), BF16),
                   jax.ShapeDtypeStruct((3, hh), F32), jax.ShapeDtypeStruct((3, hh), F32)],
        compiler_params=_cparams("parallel", "arbitrary"))(up, up, up, up, up, up, w, w, dact, dact)
    return jnp.concatenate([dupg, dupv], axis=1), jnp.concatenate([dwg, dwv], axis=1)


def sconv_fwd(proj, w, *, name):
    s = proj.shape[0]
    nj = CONVC // CW
    r = _rows(s)
    nt = s // r

    def body(b_ref, c_ref, cp_ref, x_ref, xp_ref, w_ref, o_ref):
        pm = (pl.program_id(1) > 0).astype(F32)
        zw = jnp.concatenate([cp_ref[...] * xp_ref[...] * pm, c_ref[...] * x_ref[...]], axis=0)
        o_ref[...] = (b_ref[...] * _conv3(zw, w_ref[...])[HALO:]).astype(BF16)

    bcur, _, _ = _halo_specs(r, lambda j: (QR + KVR) // CW + j)
    ccur, cprev, _ = _halo_specs(r, lambda j: (QR + KVR + CONVC) // CW + j)
    xcur, xprev, _ = _halo_specs(r, lambda j: (QR + KVR + 2 * CONVC) // CW + j)
    ws = pl.BlockSpec((3, CW), lambda j, i: (0, j))
    return pl.pallas_call(
        body, name=name, grid=(nj, nt), in_specs=[bcur, ccur, cprev, xcur, xprev, ws],
        out_specs=pl.BlockSpec((r, CW), lambda j, i: (i, j)),
        out_shape=jax.ShapeDtypeStruct((s, CONVC), BF16),
        compiler_params=_cparams("parallel", "parallel"))(proj, proj, proj, proj, proj, w)


def sconv_bwd(proj, w, dy, *, name):
    s = proj.shape[0]
    nj = CONVC // CW
    r = _rows(s)
    nt = s // r

    def body(b_ref, bn_ref, c_ref, cp_ref, x_ref, xp_ref, w_ref, dy_ref, dyn_ref, db_ref, dc_ref, dx_ref, dw_ref):
        i = pl.program_id(1)
        pm = (i > 0).astype(F32)
        nm = (i < nt - 1).astype(F32)

        @pl.when(i == 0)
        def _():
            dw_ref[...] = jnp.zeros_like(dw_ref)

        wv = w_ref[...]
        zw = jnp.concatenate([cp_ref[...] * xp_ref[...] * pm, c_ref[...] * x_ref[...]], axis=0)
        conv = _conv3(zw, wv)[HALO:]
        dyv = dy_ref[...]
        db_ref[...] = (dyv * conv).astype(BF16)
        dconv = jnp.concatenate([dyv * b_ref[...], dyn_ref[...] * bn_ref[...] * nm], axis=0)
        dz = _conv3_t(dconv, wv)[:r]
        dc_ref[...] = (dz * x_ref[...]).astype(BF16)
        dx_ref[...] = (dz * c_ref[...]).astype(BF16)
        for kk, a in enumerate(_conv3_dw(dconv[:r], zw, r)):
            dw_ref[kk:kk + 1, :] += a

    bcur, _, bnext = _halo_specs(r, lambda j: (QR + KVR) // CW + j)
    ccur, cprev, _ = _halo_specs(r, lambda j: (QR + KVR + CONVC) // CW + j)
    xcur, xprev, _ = _halo_specs(r, lambda j: (QR + KVR + 2 * CONVC) // CW + j)
    ycur, _, ynext = _halo_specs(r, lambda j: j)
    ws = pl.BlockSpec((3, CW), lambda j, i: (0, j))
    out = pl.BlockSpec((r, CW), lambda j, i: (i, j))
    db, dc, dx, dw = pl.pallas_call(
        body, name=name, grid=(nj, nt),
        in_specs=[bcur, bnext(nt), ccur, cprev, xcur, xprev, ws, ycur, ynext(nt)],
        out_specs=[out, out, out, ws],
        out_shape=[jax.ShapeDtypeStruct((s, CONVC), BF16)] * 3 + [jax.ShapeDtypeStruct((3, CONVC), F32)],
        compiler_params=_cparams("parallel", "arbitrary"))(proj, proj, proj, proj, proj, proj, w, dy, dy)
    return jnp.concatenate([db, dc, dx], axis=1), dw


SW = 512
NJ = NST // SW


def _scan_tables(ar, ai):
    def cmul(x, y):
        return x[0] * y[0] - x[1] * y[1], x[0] * y[1] + x[1] * y[0]

    def build(a, reverse):
        pw = [a]
        for _ in range(SLAB - 1):
            pw.append(cmul(pw[-1], a))
        row = jnp.arange(SLAB)[:, None]
        tabs = []
        for kk in (1, 2, 4):
            mask = ((row < SLAB - kk) if reverse else (row >= kk)).astype(F32)
            tabs += [mask * pw[kk - 1][0][None, :], mask * pw[kk - 1][1][None, :]]
        order = list(range(SLAB - 1, -1, -1)) if reverse else list(range(SLAB))
        tabs += [jnp.stack([pw[o][0] for o in order]), jnp.stack([pw[o][1] for o in order])]
        return jnp.stack(tabs)

    return build((ar, ai), False), build((ar, -ai), True)


def _slab_scan(xr, xi, tabs, cr, ci, reverse):
    for n, kk in enumerate((1, 2, 4)):
        sh = SLAB - kk if reverse else kk
        tr, ti = tabs[2 * n], tabs[2 * n + 1]
        sr, si = pltpu.roll(xr, sh, 0), pltpu.roll(xi, sh, 0)
        xr, xi = xr + tr * sr - ti * si, xi + tr * si + ti * sr
    tr, ti = tabs[6], tabs[7]
    return xr + tr * cr - ti * ci, xi + tr * ci + ti * cr


def s5_fwd(u, bbd_r, bbd_i, cbd_r, cbd_i, tab, *, name):
    s = u.shape[0]
    tbk = _rows(s)
    nt = s // tbk
    nsl = tbk // SLAB

    def body(u_ref, br_ref, bi_ref, cr_ref, ci_ref, tab_ref, y_ref, sr_ref, si_ref, bur, bui, carry):
        @pl.when(pl.program_id(1) == 0)
        def _():
            carry[...] = jnp.zeros_like(carry)

        ub = u_ref[...].astype(BF16)
        bur[...] = jnp.dot(ub, br_ref[0], preferred_element_type=F32)
        bui[...] = jnp.dot(ub, bi_ref[0], preferred_element_type=F32)
        tabs = [tab_ref[n] for n in range(8)]

        def slab(n, c):
            r0 = pl.multiple_of(n * SLAB, SLAB)
            sr, si = _slab_scan(bur[pl.ds(r0, SLAB), :], bui[pl.ds(r0, SLAB), :], tabs, c[0], c[1], False)
            sr_ref[pl.ds(r0, SLAB), :] = sr
            si_ref[pl.ds(r0, SLAB), :] = si
            return (jnp.broadcast_to(sr[SLAB - 1:SLAB], sr.shape), jnp.broadcast_to(si[SLAB - 1:SLAB], si.shape))

        cr, ci = lax.fori_loop(0, nsl, slab, (carry[0], carry[1]))
        carry[0] = cr
        carry[1] = ci
        y_ref[...] = (jnp.dot(sr_ref[...].astype(BF16), cr_ref[0], preferred_element_type=F32)
                      - jnp.dot(si_ref[...].astype(BF16), ci_ref[0], preferred_element_type=F32))

    us = pl.BlockSpec((tbk, LANE), lambda j, t: (t, j))
    bs = pl.BlockSpec((1, LANE, SW), lambda j, t: (j, 0, 0))
    cs = pl.BlockSpec((1, SW, LANE), lambda j, t: (j, 0, 0))
    ts = pl.BlockSpec((8, SLAB, SW), lambda j, t: (0, 0, j))
    ss = pl.BlockSpec((tbk, SW), lambda j, t: (t, j))
    return pl.pallas_call(
        body, name=name, grid=(NJ, nt), in_specs=[us, bs, bs, cs, cs, ts], out_specs=[us, ss, ss],
        out_shape=[jax.ShapeDtypeStruct((s, D_MODEL), F32), jax.ShapeDtypeStruct((s, NST), F32),
                   jax.ShapeDtypeStruct((s, NST), F32)],
        scratch_shapes=[pltpu.VMEM((tbk, SW), F32), pltpu.VMEM((tbk, SW), F32), pltpu.VMEM((2, SLAB, SW), F32)],
        compiler_params=_cparams("parallel", "arbitrary"))(u, bbd_r, bbd_i, cbd_r, cbd_i, tab)


def s5_bwd(u, dy, dskip, st_r, st_i, bbd_r, bbd_i, cbd_r, cbd_i, tabrev, *, name):
    s = u.shape[0]
    tbk = _rows(s)
    nt = s // tbk
    nsl = tbk // SLAB
    rbk = tbk // SLAB

    def body(u_ref, dy_ref, d_ref, sr_ref, si_ref, pr_ref, pi_ref, br_ref, bi_ref, cr_ref, ci_ref, tab_ref,
             du_ref, dbr_ref, dbi_ref, dcr_ref, dci_ref, da_ref, lam_r, lam_i, carry):
        t = pl.program_id(1)

        @pl.when(t == 0)
        def _():
            carry[...] = jnp.zeros_like(carry)
            dbr_ref[...] = jnp.zeros_like(dbr_ref)
            dbi_ref[...] = jnp.zeros_like(dbi_ref)
            dcr_ref[...] = jnp.zeros_like(dcr_ref)
            dci_ref[...] = jnp.zeros_like(dci_ref)
            da_ref[...] = jnp.zeros_like(da_ref)

        dyv = dy_ref[...]
        dyh = dyv.astype(BF16)
        lam_r[...] = lax.dot_general(dyh, cr_ref[0], NT, preferred_element_type=F32)
        lam_i[...] = -lax.dot_general(dyh, ci_ref[0], NT, preferred_element_type=F32)
        tabs = [tab_ref[n] for n in range(8)]

        def slab(n, c):
            r0 = pl.multiple_of((nsl - 1 - n) * SLAB, SLAB)
            lr, li = _slab_scan(lam_r[pl.ds(r0, SLAB), :], lam_i[pl.ds(r0, SLAB), :], tabs, c[0], c[1], True)
            lam_r[pl.ds(r0, SLAB), :] = lr
            lam_i[pl.ds(r0, SLAB), :] = li
            return (jnp.broadcast_to(lr[0:1], lr.shape), jnp.broadcast_to(li[0:1], li.shape))

        cr, ci = lax.fori_loop(0, nsl, slab, (carry[0], carry[1]))
        carry[0] = cr
        carry[1] = ci
        lr, li = lam_r[...], lam_i[...]
        lrh, lih = lr.astype(BF16), li.astype(BF16)
        du = (dyv * d_ref[...] + lax.dot_general(lrh, br_ref[0], NT, preferred_element_type=F32)
              + lax.dot_general(lih, bi_ref[0], NT, preferred_element_type=F32))
        du_ref[...] = du.astype(BF16)
        ub = u_ref[...].astype(BF16)
        dbr_ref[0] += lax.dot_general(ub, lrh, TN, preferred_element_type=F32)
        dbi_ref[0] += lax.dot_general(ub, lih, TN, preferred_element_type=F32)
        srv, siv = sr_ref[...], si_ref[...]
        dcr_ref[0] += lax.dot_general(srv.astype(BF16), dyh, TN, preferred_element_type=F32)
        dci_ref[0] -= lax.dot_general(siv.astype(BF16), dyh, TN, preferred_element_type=F32)
        first = lax.broadcasted_iota(jnp.int32, srv.shape, 0) == 0
        pm = (t < nt - 1).astype(F32)
        spr = jnp.where(first, pr_ref[SLAB - 1:SLAB, :] * pm, pltpu.roll(srv, 1, 0))
        spi = jnp.where(first, pi_ref[SLAB - 1:SLAB, :] * pm, pltpu.roll(siv, 1, 0))
        da_ref[0:1, :] += jnp.sum(lr * spr + li * spi, axis=0, keepdims=True)
        da_ref[1:2, :] += jnp.sum(li * spr - lr * spi, axis=0, keepdims=True)

    rv = lambda t: nt - 1 - t
    us = pl.BlockSpec((tbk, LANE), lambda j, t: (rv(t), j))
    ds = pl.BlockSpec((1, LANE), lambda j, t: (0, j))
    ss = pl.BlockSpec((tbk, SW), lambda j, t: (rv(t), j))
    ps = pl.BlockSpec((SLAB, SW), lambda j, t: (jnp.maximum(rv(t) * rbk - 1, 0), j))
    bs = pl.BlockSpec((1, LANE, SW), lambda j, t: (j, 0, 0))
    cs = pl.BlockSpec((1, SW, LANE), lambda j, t: (j, 0, 0))
    ts = pl.BlockSpec((8, SLAB, SW), lambda j, t: (0, 0, j))
    das = pl.BlockSpec((2, SW), lambda j, t: (0, j))
    return pl.pallas_call(
        body, name=name, grid=(NJ, nt),
        in_specs=[us, us, ds, ss, ss, ps, ps, bs, bs, cs, cs, ts],
        out_specs=[us, bs, bs, cs, cs, das],
        out_shape=[jax.ShapeDtypeStruct((s, D_MODEL), BF16),
                   jax.ShapeDtypeStruct((NJ, LANE, SW), F32), jax.ShapeDtypeStruct((NJ, LANE, SW), F32),
                   jax.ShapeDtypeStruct((NJ, SW, LANE), F32), jax.ShapeDtypeStruct((NJ, SW, LANE), F32),
                   jax.ShapeDtypeStruct((2, NST), F32)],
        scratch_shapes=[pltpu.VMEM((tbk, SW), F32), pltpu.VMEM((tbk, SW), F32), pltpu.VMEM((2, SLAB, SW), F32)],
        compiler_params=_cparams("parallel", "arbitrary"))(
            u, dy, dskip, st_r, st_i, st_r, st_i, bbd_r, bbd_i, cbd_r, cbd_i, tabrev)


GELU_C = math.sqrt(2.0 / math.pi)
GELU_A = 0.044715


def s5post_fwd(y, u, dskip, *, name):
    s = y.shape[0]
    tm = _rows(s)

    def body(y_ref, u_ref, d_ref, o_ref):
        z = y_ref[...] + d_ref[...] * u_ref[...]
        o_ref[...] = (0.5 * z * (1.0 + jnp.tanh(GELU_C * (z + GELU_A * z * z * z)))).astype(BF16)

    row = pl.BlockSpec((tm, D_MODEL), lambda i: (i, 0))
    vec = pl.BlockSpec((1, D_MODEL), lambda i: (0, 0))
    return pl.pallas_call(body, name=name, grid=(s // tm,), in_specs=[row, row, vec], out_specs=row,
                          out_shape=jax.ShapeDtypeStruct((s, D_MODEL), BF16),
                          compiler_params=_cparams("parallel"))(y, u, dskip)


def s5post_bwd(y, u, dskip, dg, *, name):
    s = y.shape[0]
    tm = _rows(s)

    def body(y_ref, u_ref, d_ref, dg_ref, dz_ref, dd_ref):
        @pl.when(pl.program_id(0) == 0)
        def _():
            dd_ref[...] = jnp.zeros_like(dd_ref)

        uv = u_ref[...]
        z = y_ref[...] + d_ref[...] * uv
        th = jnp.tanh(GELU_C * (z + GELU_A * z * z * z))
        dgelu = 0.5 * (1.0 + th) + 0.5 * z * (1.0 - th * th) * (GELU_C * (1.0 + 3.0 * GELU_A * z * z))
        dz = dg_ref[...] * dgelu
        dz_ref[...] = dz
        dd_ref[...] += jnp.sum(dz * uv, axis=0, keepdims=True)

    row = pl.BlockSpec((tm, D_MODEL), lambda i: (i, 0))
    vec = pl.BlockSpec((1, D_MODEL), lambda i: (0, 0))
    return pl.pallas_call(body, name=name, grid=(s // tm,), in_specs=[row, row, vec, row], out_specs=[row, vec],
                          out_shape=[jax.ShapeDtypeStruct((s, D_MODEL), F32), jax.ShapeDtypeStruct((1, D_MODEL), F32)],
                          compiler_params=_cparams("arbitrary"))(y, u, dskip, dg)


def glu_fwd(glu, x, *, name):
    s = x.shape[0]
    tm = _rows(s)

    def body(a_ref, b_ref, x_ref, o_ref):
        o_ref[...] = x_ref[...] + a_ref[...] * jax.nn.sigmoid(b_ref[...])

    row = pl.BlockSpec((tm, D_MODEL), lambda i: (i, 0))
    return pl.pallas_call(body, name=name, grid=(s // tm,),
                          in_specs=[row, pl.BlockSpec((tm, D_MODEL), lambda i: (i, 1)), row], out_specs=row,
                          out_shape=jax.ShapeDtypeStruct((s, D_MODEL), F32),
                          compiler_params=_cparams("parallel"))(glu, glu, x)


def glu_bwd(glu, dx, *, name):
    s = dx.shape[0]
    tm = _rows(s)

    def body(a_ref, b_ref, dx_ref, o_ref):
        sg = jax.nn.sigmoid(b_ref[...])
        dxv = dx_ref[...]
        o_ref[:, :D_MODEL] = (dxv * sg).astype(BF16)
        o_ref[:, D_MODEL:] = (dxv * a_ref[...] * sg * (1.0 - sg)).astype(BF16)

    row = pl.BlockSpec((tm, D_MODEL), lambda i: (i, 0))
    return pl.pallas_call(body, name=name, grid=(s // tm,),
                          in_specs=[row, pl.BlockSpec((tm, D_MODEL), lambda i: (i, 1)), row],
                          out_specs=pl.BlockSpec((tm, 2 * D_MODEL), lambda i: (i, 0)),
                          out_shape=jax.ShapeDtypeStruct((s, 2 * D_MODEL), BF16),
                          compiler_params=_cparams("parallel"))(glu, glu, dx)


def loss_head(y, target, *, name):
    s = y.shape[0]
    tm = _rows(s)

    def body(y_ref, t_ref, dy_ref, dyh_ref, l_ref):
        @pl.when(pl.program_id(0) == 0)
        def _():
            l_ref[...] = jnp.zeros_like(l_ref)

        e = y_ref[...] - t_ref[...]
        dy_ref[...] = e * (1.0 / D_MODEL)
        dyh_ref[...] = (e * (1.0 / D_MODEL)).astype(BF16)
        e2 = jnp.sum((e * e).reshape(tm // 8, 8, D_MODEL), axis=0)
        acc = e2[:, 0:LANE]
        for kk in range(1, D_MODEL // LANE):
            acc = acc + e2[:, kk * LANE:(kk + 1) * LANE]
        l_ref[...] += acc

    row = pl.BlockSpec((tm, D_MODEL), lambda i: (i, 0))
    return pl.pallas_call(body, name=name, grid=(s // tm,), in_specs=[row, row],
                          out_specs=[row, row, pl.BlockSpec((8, LANE), lambda i: (0, 0))],
                          out_shape=[jax.ShapeDtypeStruct((s, D_MODEL), F32), jax.ShapeDtypeStruct((s, D_MODEL), BF16),
                                     jax.ShapeDtypeStruct((8, LANE), F32)],
                          compiler_params=_cparams("arbitrary"))(y, target)


PACKW = 1024
NCHIP = 4


def _mesh_pos():
    return lax.axis_index("x"), lax.axis_index("y"), lax.axis_index("c")


def _chip_exchange(bufs, scatter, name):
    n = len(bufs)
    shapes = [b.shape[1:] if scatter else b.shape for b in bufs]

    def body(*refs):
        ins, outs = refs[:n], refs[n:2 * n]
        send_sems, recv_sems, local_sems = refs[2 * n:]
        x, y, c = _mesh_pos()
        me = 2 * x + y
        peers = [(1 - x, y), (x, 1 - y), (1 - x, 1 - y)]

        def copy(a, j, px, py, dst_slot):
            src = ins[a].at[2 * px + py] if scatter else ins[a]
            return pltpu.make_async_remote_copy(src_ref=src, dst_ref=outs[a].at[dst_slot],
                                                send_sem=send_sems.at[3 * a + j], recv_sem=recv_sems.at[3 * a + j],
                                                device_id=(px, py, c), device_id_type=MESH)

        mine = [pltpu.make_async_copy(ins[a].at[me] if scatter else ins[a], outs[a].at[me], local_sems.at[a])
                for a in range(n)]
        sends = [copy(a, j, px, py, me) for a in range(n) for j, (px, py) in enumerate(peers)]
        for cp in mine + sends:
            cp.start()
        for a in range(n):
            for j, (px, py) in enumerate(peers):
                copy(a, j, px, py, 2 * px + py).wait_recv()
        for cp in sends:
            cp.wait_send()
        for cp in mine:
            cp.wait()

    return pl.pallas_call(
        body, name=name, in_specs=[ANY] * n, out_specs=[ANY] * n,
        out_shape=[jax.ShapeDtypeStruct((NCHIP,) + tuple(shp), b.dtype) for shp, b in zip(shapes, bufs)],
        scratch_shapes=[pltpu.SemaphoreType.DMA((3 * n,)), pltpu.SemaphoreType.DMA((3 * n,)),
                        pltpu.SemaphoreType.DMA((n,))],
    )(*bufs)


HBM_SPEC = pl.BlockSpec(memory_space=pltpu.HBM)
SEM_SPEC = pl.BlockSpec(memory_space=pltpu.SEMAPHORE)
DATAFLOW = pltpu.SideEffectType.DATAFLOW_SIDE_EFFECTING


def _exchange_copy(ins, lands, send_sems, recv_sems, scatter, a, j, px, py, c, dst_slot):
    src = ins[a].at[2 * px + py] if scatter else ins[a]
    return pltpu.make_async_remote_copy(src_ref=src, dst_ref=lands[a].at[dst_slot],
                                        send_sem=send_sems.at[3 * a + j], recv_sem=recv_sems.at[3 * a + j],
                                        device_id=(px, py, c), device_id_type=MESH)


def exchange_start(bufs, lands, scatter, name):
    n = len(bufs)

    def body(*refs):
        ins, lnd, send_sems, recv_sems, token = refs[:n], refs[n:2 * n], refs[2 * n], refs[2 * n + 1], refs[-1]
        x, y, c = _mesh_pos()
        me = 2 * x + y
        for a in range(n):
            for j, (px, py) in enumerate([(1 - x, y), (x, 1 - y), (1 - x, 1 - y)]):
                _exchange_copy(ins, lnd, send_sems, recv_sems, scatter, a, j, px, py, c, me).start()
        token[...] = jnp.zeros_like(token)

    thru = [pltpu.HBM(b.shape, b.dtype) for b in list(bufs) + list(lands)]
    out = pl.pallas_call(
        body, name=name, in_specs=[HBM_SPEC] * (2 * n),
        out_specs=[SEM_SPEC, SEM_SPEC] + [HBM_SPEC] * (2 * n) + [pl.BlockSpec(memory_space=pltpu.VMEM)],
        out_shape=[pltpu.SemaphoreType.DMA((3 * n,)), pltpu.SemaphoreType.DMA((3 * n,))] + thru
        + [jax.ShapeDtypeStruct((SLAB, LANE), F32)],
        input_output_aliases={k: 2 + k for k in range(2 * n)},
        compiler_params=pltpu.CompilerParams(has_side_effects=DATAFLOW),
    )(*[pltpu.with_memory_space_constraint(b, pltpu.HBM) for b in list(bufs) + list(lands)])
    return out[0], out[1], out[2:2 + n], out[2 + n:2 + 2 * n], out[-1][0, 0]


def exchange_wait(send_sems, recv_sems, bufs, lands, after, scatter, name):
    n = len(bufs)

    def body(*refs):
        ins, lnd, ssem, rsem = refs[:n], refs[n:2 * n], refs[2 * n], refs[2 * n + 1]
        x, y, c = _mesh_pos()
        for a in range(n):
            for j, (px, py) in enumerate([(1 - x, y), (x, 1 - y), (1 - x, 1 - y)]):
                cp = _exchange_copy(ins, lnd, ssem, rsem, scatter, a, j, px, py, c, 2 * px + py)
                cp.wait_send()
                cp.wait_recv()

    thru = [pltpu.HBM(b.shape, b.dtype) for b in list(bufs) + list(lands)]
    out = pl.pallas_call(
        body, name=name, in_specs=[HBM_SPEC] * (2 * n) + [SEM_SPEC, SEM_SPEC, ANY],
        out_specs=[HBM_SPEC] * (2 * n), out_shape=thru,
        input_output_aliases={k: k for k in range(2 * n)},
        compiler_params=pltpu.CompilerParams(has_side_effects=DATAFLOW),
    )(*bufs, *lands, send_sems, recv_sems, after)
    return out[n:]


def sibling_swap(bufs, name):
    n = len(bufs)

    def body(*refs):
        ins, outs, send_sems, recv_sems = refs[:n], refs[n:2 * n], refs[2 * n], refs[2 * n + 1]
        x, y, c = _mesh_pos()
        cps = [pltpu.make_async_remote_copy(src_ref=ins[k], dst_ref=outs[k], send_sem=send_sems.at[k],
                                            recv_sem=recv_sems.at[k], device_id=(x, y, 1 - c), device_id_type=MESH)
               for k in range(n)]
        for cp in cps:
            cp.start()
        for cp in cps:
            cp.wait()

    return pl.pallas_call(
        body, name=name, in_specs=[ANY] * n, out_specs=[ANY] * n,
        out_shape=[jax.ShapeDtypeStruct(b.shape, b.dtype) for b in bufs],
        scratch_shapes=[pltpu.SemaphoreType.DMA((n,)), pltpu.SemaphoreType.DMA((n,))],
    )(*bufs)


EW_VMEM_BUDGET = 20 * 1024 * 1024


def _ew_rows(rows, w, bytes_per_elem):
    wpad = -(-w // LANE) * LANE
    for t in (1024, 512, 256, 128, 64, 32, 16, 8):
        if rows % t == 0 and 2 * t * wpad * bytes_per_elem <= EW_VMEM_BUDGET:
            return t
    return rows


def sum_slots(buf, *, name):
    _, rows, w = buf.shape
    tm = _ew_rows(rows, w, NCHIP * buf.dtype.itemsize + 4)

    def body(b_ref, o_ref):
        acc = b_ref[0].astype(F32)
        for kk in range(1, NCHIP):
            acc = acc + b_ref[kk].astype(F32)
        o_ref[...] = acc

    return pl.pallas_call(body, name=name, grid=(rows // tm,),
                          in_specs=[pl.BlockSpec((NCHIP, tm, w), lambda i: (0, i, 0))],
                          out_specs=pl.BlockSpec((tm, w), lambda i: (i, 0)),
                          out_shape=jax.ShapeDtypeStruct((rows, w), F32),
                          compiler_params=_cparams("parallel"))(buf)


def adamw(p_mine, p_other, w, m, v, *, name):
    rows, wd = w.shape
    tm = _ew_rows(rows, wd, 9 * 4)
    c1 = 1.0 - ADAM_B1 ** ADAM_STEP
    c2 = 1.0 - ADAM_B2 ** ADAM_STEP

    def body(a_ref, b_ref, w_ref, m_ref, v_ref, g_ref, d_ref, nm_ref, nv_ref):
        g = a_ref[...] + b_ref[...]
        nm = ADAM_B1 * m_ref[...] + (1.0 - ADAM_B1) * g
        nv = ADAM_B2 * v_ref[...] + (1.0 - ADAM_B2) * (g * g)
        g_ref[...] = g
        nm_ref[...] = nm
        nv_ref[...] = nv
        d_ref[...] = -ADAM_LR * ((nm / c1) / (jnp.sqrt(nv / c2) + ADAM_EPS) + ADAM_WD * w_ref[...])

    row = pl.BlockSpec((tm, wd), lambda i: (i, 0))
    return pl.pallas_call(body, name=name, grid=(rows // tm,), in_specs=[row] * 5, out_specs=[row] * 4,
                          out_shape=[jax.ShapeDtypeStruct((rows, wd), F32)] * 4,
                          compiler_params=_cparams("parallel"))(p_mine, p_other, w, m, v)


def _rows2d(a, lead=0):
    tail = a.shape[lead:]
    n = int(np.prod(tail))
    if tail[-1] < LANE // 2 and n % (8 * LANE) == 0:
        return a.reshape(a.shape[:lead] + (n // (8 * LANE), 8 * LANE))
    return a.reshape(a.shape[:lead] + (-1, tail[-1]))


BIG = [("mix_w_in", 2), ("w_uq", 2), ("w_ukv", 2), ("mix_w_out", 1), ("ssm_w_in", 1), ("w_glu", 2),
       ("ffn_w_up", 2), ("ffn_w_down", 1)]
SMALL = [("sconv_w", 2), ("ssm_norm", 1), ("d_skip", 1), ("ffn_conv_w", 2)]
REPL = ["attn_norm", "cq_norm", "ckv_norm", "q_gain", "k_gain", "lambda_re", "lambda_im", "log_step",
        "b_re", "b_im", "c_re", "c_im", "ffn_norm"]
ORDER = ["attn_norm", "mix_w_in", "cq_norm", "ckv_norm", "w_uq", "w_ukv", "q_gain", "k_gain", "sconv_w", "mix_w_out",
         "ssm_norm", "ssm_w_in", "lambda_re", "lambda_im", "log_step", "b_re", "b_im", "c_re", "c_im", "d_skip",
         "w_glu", "ffn_norm", "ffn_w_up", "ffn_conv_w", "ffn_w_down"]


def _join(g, axis):
    return jnp.concatenate([g[k] for k in range(NCHIP)], axis=axis)


def _split(full, axis):
    return jnp.stack(jnp.split(full, NCHIP, axis=axis))


def _discretize(lr, li, ls, b_re, b_im):
    dt = jnp.exp(ls)[:, None]
    mag = jnp.exp(lr * dt)
    ar, ai = mag * jnp.cos(li * dt), mag * jnp.sin(li * dt)
    nr, ni = ar - 1.0, ai
    den = lr * lr + li * li
    zr, zi = (nr * lr + ni * li) / den, (ni * lr - nr * li) / den
    bbar_r = zr[..., None] * b_re - zi[..., None] * b_im
    bbar_i = zr[..., None] * b_im + zi[..., None] * b_re
    return ar, ai, bbar_r, bbar_i


def _b_blockdiag(bbar):
    gl = G // NJ
    bb = bbar.reshape(NJ, gl, P, GC).transpose(0, 1, 3, 2)
    return jnp.einsum("jgcp,gh->jgchp", bb, jnp.eye(gl, dtype=bbar.dtype)).reshape(NJ, gl * GC, gl * P)


def _b_blockdiag_t(dbd):
    gl = G // NJ
    d = jnp.einsum("jgchp,gh->jgcp", dbd.reshape(NJ, gl, GC, gl, P), jnp.eye(gl, dtype=dbd.dtype))
    return d.transpose(0, 1, 3, 2).reshape(G, P, GC)


def _c_blockdiag(cmat):
    gl = G // NJ
    cc = cmat.reshape(NJ, gl, GC, P).transpose(0, 1, 3, 2)
    return jnp.einsum("jgpc,gh->jgphc", cc, jnp.eye(gl, dtype=cmat.dtype)).reshape(NJ, gl * P, gl * GC)


def _c_blockdiag_t(dbd):
    gl = G // NJ
    d = jnp.einsum("jgphc,gh->jgpc", dbd.reshape(NJ, gl, P, gl, GC), jnp.eye(gl, dtype=dbd.dtype))
    return d.transpose(0, 1, 3, 2).reshape(G, GC, P)


def _pad_heads_cols(w, width):
    r = w.shape[0]
    return jnp.pad(w.reshape(r, HEADS, width), ((0, 0), (0, 0), (0, HP - width))).reshape(r, HEADS * HP)


def _unpad_heads_cols(w, width):
    r = w.shape[0]
    return w.reshape(r, HEADS, HP)[:, :, :width].reshape(r, HEADS * width)


W_IN_SPLIT = (QR + KVR, QR + KVR + ROPE)


def _w_in_layout(w):
    a, b = W_IN_SPLIT
    kr = jnp.pad(w[:, a:b], ((0, 0), (NOPE, HP - QK)))
    return jnp.concatenate([w[:, :a], w[:, b:], kr], axis=1)


def _w_in_layout_t(dw):
    a, b = W_IN_SPLIT
    n = dw.shape[1] - HP
    return jnp.concatenate([dw[:, :a], dw[:, n + NOPE:n + QK], dw[:, a:n]], axis=1)


def _ffn_fwd(x, l, wt, name):
    h = rms_fwd(x, wt["ffn_norm"][l][None], name=f"{name}_norm")
    up = mm(h, wt["ffn_w_up"][l], name=f"{name}_up")
    act = ffnact_fwd(up, wt["ffn_conv_w"][l], name=f"{name}_act")
    out = mm(act, wt["ffn_w_down"][l], add=x, name=f"{name}_down")
    return out, (x, h, up, act)


def _ffn_bwd(dout, douth, saved, l, wt, name):
    x, h, up, act = saved
    g = {}
    dact = mm(douth, wt["ffn_w_down"][l], tb=True, name=f"{name}_ddown")
    g["ffn_w_down"] = mm(act, douth, ta=True, out_dtype=BF16, name=f"{name}_dwdown")
    dup, g["ffn_conv_w"] = ffnact_bwd(up, wt["ffn_conv_w"][l], dact, name=f"{name}_dact")
    g["ffn_w_up"] = mm(h, dup, ta=True, out_dtype=BF16, name=f"{name}_dwup")
    dh = mm(dup, wt["ffn_w_up"][l], tb=True, name=f"{name}_dup")
    dx, dxh, dg = rms_bwd(x, wt["ffn_norm"][l][None], dh, add=dout, twin=True, name=f"{name}_dnorm")
    g["ffn_norm"] = dg[0]
    return dx, dxh, g


def _even_fwd(x, i, wt, tabs, name):
    h = rms_fwd(x, wt["attn_norm"][i][None], name=f"{name}_norm")
    proj = mm(h, wt["w_in2"][i], name=f"{name}_in")
    cqn = rms_fwd(proj, wt["cq_norm"][i][None], col=0, name=f"{name}_cqnorm")
    ckvn = rms_fwd(proj, wt["ckv_norm"][i][None], col=1, name=f"{name}_ckvnorm")
    qraw = mm(cqn, wt["w_uq_p"][i], name=f"{name}_uq")
    kv = mm(ckvn, wt["w_ukv_p"][i], name=f"{name}_ukv")
    q, k, v = qkprep_fwd(qraw, kv, proj, wt["q_gain_p"][i], wt["k_gain_p"][i], tabs, name=f"{name}_qkprep")
    o, oh, lse, lset = attn_fwd(q, k, v, name=f"{name}_attn")
    conv = sconv_fwd(proj, wt["sconv_w"][i], name=f"{name}_sconv")
    t = mm(oh, wt["w_out_a"][i], add=x, name=f"{name}_outa")
    out = mm(conv, wt["w_out_c"][i], add=t, name=f"{name}_outc")
    return out, (x, h, proj, cqn, ckvn, qraw, kv, q, k, v, o, oh, lse, lset, conv)


def _even_bwd(dout, douth, saved, i, wt, tabs, name):
    x, h, proj, cqn, ckvn, qraw, kv, q, k, v, o, oh, lse, lset, conv = saved
    g = {}
    do = mm(douth, wt["w_out_a"][i], tb=True, name=f"{name}_douta")
    dconv = mm(douth, wt["w_out_c"][i], tb=True, name=f"{name}_doutc")
    g["w_out_a"] = mm(oh, douth, ta=True, out_dtype=BF16, name=f"{name}_dwouta")
    g["w_out_c"] = mm(conv, douth, ta=True, out_dtype=BF16, name=f"{name}_dwoutc")
    dgates, g["sconv_w"] = sconv_bwd(proj, wt["sconv_w"][i], dconv, name=f"{name}_dsconv")
    dq, doh, deltat = attn_bwd_dq(q, k, v, o, lse, do, name=f"{name}_dattn_q")
    dk, dv = attn_bwd_dkv(q, k, v, doh, lset, deltat, name=f"{name}_dattn_kv")
    dqraw, dkraw, dkrope, dqg, dkg = qkprep_bwd(qraw, kv, proj, wt["q_gain_p"][i], wt["k_gain_p"][i], tabs, dq, dk,
                                                name=f"{name}_dqkprep")
    g["q_gain"], g["k_gain"] = dqg[0, :QK], dkg[0, :QK]
    dcqn = mm(dqraw, wt["w_uq_p"][i], tb=True, name=f"{name}_duq")
    g["w_uq_p"] = mm(cqn, dqraw, ta=True, out_dtype=BF16, name=f"{name}_dwuq")
    dkv = jnp.concatenate([dkraw, dv], axis=1)
    dckvn = mm(dkv, wt["w_ukv_p"][i], tb=True, name=f"{name}_dukv")
    g["w_ukv_p"] = mm(ckvn, dkv, ta=True, out_dtype=BF16, name=f"{name}_dwukv")
    dcq, dgq = rms_bwd(proj, wt["cq_norm"][i][None], dcqn, col=0, out_dtype=BF16, name=f"{name}_dcqnorm")
    dckv, dgkv = rms_bwd(proj, wt["ckv_norm"][i][None], dckvn, col=1, out_dtype=BF16, name=f"{name}_dckvnorm")
    g["cq_norm"], g["ckv_norm"] = dgq[0], dgkv[0]
    dproj = jnp.concatenate([dcq, dckv, dgates, dkrope.astype(BF16)], axis=1)
    g["w_in2"] = mm(h, dproj, ta=True, out_dtype=BF16, name=f"{name}_dwin")
    dh = mm(dproj, wt["w_in2"][i], tb=True, name=f"{name}_din")
    dx, dxh, dg = rms_bwd(x, wt["attn_norm"][i][None], dh, add=dout, twin=True, name=f"{name}_dnorm")
    g["attn_norm"] = dg[0]
    return dx, dxh, g


def _odd_fwd(x, i, wt, name):
    h = rms_fwd(x, wt["ssm_norm"][i][None], name=f"{name}_norm")
    u = mm(h, wt["ssm_w_in"][i], name=f"{name}_in")
    y, st_r, st_i = s5_fwd(u, wt["bbd_r"][i], wt["bbd_i"][i], wt["cbd_r"][i], wt["cbd_i"][i], wt["tab_f"][i],
                           name=f"{name}_scan")
    gl = s5post_fwd(y, u, wt["d_skip"][i][None], name=f"{name}_gelu")
    glu = mm(gl, wt["w_glu"][i], name=f"{name}_glu")
    out = glu_fwd(glu, x, name=f"{name}_gate")
    return out, (x, h, u, y, st_r, st_i, gl, glu)


def _odd_bwd(dout, douth, saved, i, wt, name):
    x, h, u, y, st_r, st_i, gl, glu = saved
    g = {}
    dglu = glu_bwd(glu, dout, name=f"{name}_dgate")
    g["w_glu"] = mm(gl, dglu, ta=True, out_dtype=BF16, name=f"{name}_dwglu")
    dgl = mm(dglu, wt["w_glu"][i], tb=True, name=f"{name}_dglu")
    dz, dd = s5post_bwd(y, u, wt["d_skip"][i][None], dgl, name=f"{name}_dgelu")
    g["d_skip"] = dd[0]
    du, g["bbd_r"], g["bbd_i"], g["cbd_r"], g["cbd_i"], g["a"] = s5_bwd(
        u, dz, wt["d_skip"][i][None], st_r, st_i, wt["bbd_r"][i], wt["bbd_i"][i], wt["cbd_r"][i], wt["cbd_i"][i],
        wt["tab_r"][i], name=f"{name}_dscan")
    g["ssm_w_in"] = mm(h, du, ta=True, out_dtype=BF16, name=f"{name}_dwin")
    dh = mm(du, wt["ssm_w_in"][i], tb=True, name=f"{name}_din")
    dx, dxh, dg = rms_bwd(x, wt["ssm_norm"][i][None], dh, add=dout, twin=True, name=f"{name}_dnorm")
    g["ssm_norm"] = dg[0]
    return dx, dxh, g


MATMUL_WEIGHTS = {"even": ("mix_w_in", "w_uq", "w_ukv", "mix_w_out"), "odd": ("ssm_w_in", "w_glu"),
                  "ffn": ("ffn_w_up", "ffn_w_down")}


def _even_layouts(fw, wt, i):
    wt["w_in2"][i] = _w_in_layout(fw["mix_w_in"])
    wt["w_uq_p"][i] = _pad_heads_cols(fw["w_uq"], QK)
    ukv = fw["w_ukv"].reshape(KVR, HEADS, NOPE + VD)
    wt["w_ukv_p"][i] = jnp.concatenate(
        [_pad_heads_cols(ukv[:, :, :NOPE].reshape(KVR, HEADS * NOPE), NOPE),
         _pad_heads_cols(ukv[:, :, NOPE:].reshape(KVR, HEADS * VD), VD)], axis=1)
    wt["w_out_a"][i] = _pad_heads_cols(fw["mix_w_out"][:HEADS * VD].T, VD).T
    wt["w_out_c"][i] = fw["mix_w_out"][HEADS * VD:]


def _even_layouts_t(g):
    dk_, dv_ = g["w_ukv_p"][:, :HEADS * HP], g["w_ukv_p"][:, HEADS * HP:]
    return {"mix_w_in": _w_in_layout_t(g["w_in2"]), "w_uq": _unpad_heads_cols(g["w_uq_p"], QK),
            "w_ukv": jnp.concatenate([dk_.reshape(KVR, HEADS, HP)[:, :, :NOPE], dv_.reshape(KVR, HEADS, HP)[:, :, :VD]],
                                     axis=2).reshape(KVR, HEADS * (NOPE + VD)),
            "mix_w_out": jnp.concatenate([_unpad_heads_cols(g["w_out_a"].T, VD).T, g["w_out_c"]], axis=0)}


def _local_step(x, target, full, getw, putg, tok):
    s = x.shape[0]
    n_even = (DEPTH + 1) // 2
    n_odd = DEPTH // 2
    tabs = _rope_tables(s)
    wt = dict(full)
    for key in ("w_in2", "w_uq_p", "w_ukv_p", "w_out_a", "w_out_c") + sum(MATMUL_WEIGHTS.values(), ()):
        wt[key] = {}
    wt["attn_norm"] = full["attn_norm"] + tok
    wt["q_gain_p"] = jnp.pad(full["q_gain"], ((0, 0), (0, HP - QK)))[:, None, :]
    wt["k_gain_p"] = jnp.pad(full["k_gain"], ((0, 0), (0, HP - QK)))[:, None, :]

    disc_vjp = []
    for key in ("bbd_r", "bbd_i", "cbd_r", "cbd_i", "tab_f", "tab_r"):
        wt[key] = []
    for i in range(n_odd):
        (ar, ai, bbr, bbi), vjp = jax.vjp(_discretize, full["lambda_re"][i], full["lambda_im"][i], full["log_step"][i],
                                          full["b_re"][i], full["b_im"][i])
        disc_vjp.append(vjp)
        tf, tr = _scan_tables(ar.reshape(-1), ai.reshape(-1))
        wt["tab_f"].append(tf)
        wt["tab_r"].append(tr)
        wt["bbd_r"].append(_b_blockdiag(bbr).astype(BF16))
        wt["bbd_i"].append(_b_blockdiag(bbi).astype(BF16))
        wt["cbd_r"].append(_c_blockdiag(full["c_re"][i]).astype(BF16))
        wt["cbd_i"].append(_c_blockdiag(full["c_im"][i]).astype(BF16))

    saved = []
    for layer in range(DEPTH):
        i = layer // 2
        if layer % 2 == 0:
            _even_layouts(getw("even", i, x), wt, i)
            x, sm = _even_fwd(x, i, wt, tabs, f"l{layer}_mla")
        else:
            for n, a in getw("odd", i, x).items():
                wt[n][i] = a
            x, sm = _odd_fwd(x, i, wt, f"l{layer}_s5")
        for n, a in getw("ffn", layer, x).items():
            wt[n][layer] = a
        x, sf = _ffn_fwd(x, layer, wt, f"l{layer}_ffn")
        saved.append((sm, sf))
    dx, dxh, lslab = loss_head(x, target, name="loss_head")

    own = [n for n in ORDER if n not in sum(MATMUL_WEIGHTS.values(), ())]
    grads = {n: [None] * (DEPTH if n.startswith("ffn") else n_even) for n in own}
    tok = 0.0
    for layer in reversed(range(DEPTH)):
        i = layer // 2
        sm, sf = saved[layer]
        wt["ffn_conv_w"] = full["ffn_conv_w"] + tok
        dx, dxh, g = _ffn_bwd(dx, dxh, sf, layer, wt, f"l{layer}_ffn")
        tok = putg("ffn", layer, {n: g[n] for n in MATMUL_WEIGHTS["ffn"]})
        for n in ("ffn_norm", "ffn_conv_w"):
            grads[n][layer] = g[n]
        if layer % 2 == 0:
            wt["sconv_w"] = full["sconv_w"] + tok
            dx, dxh, g = _even_bwd(dx, dxh, sm, i, wt, tabs, f"l{layer}_mla")
            tok = putg("even", i, _even_layouts_t(g))
            for n in ("attn_norm", "cq_norm", "ckv_norm", "q_gain", "k_gain", "sconv_w"):
                grads[n][i] = g[n]
        else:
            wt["d_skip"] = full["d_skip"] + tok
            dx, dxh, g = _odd_bwd(dx, dxh, sm, i, wt, f"l{layer}_s5")
            tok = putg("odd", i, {n: g[n] for n in MATMUL_WEIGHTS["odd"]})
            dlr, dli, dls, dbr, dbi = disc_vjp[i]((g["a"][0].reshape(G, P), g["a"][1].reshape(G, P),
                                                    _b_blockdiag_t(g["bbd_r"]), _b_blockdiag_t(g["bbd_i"])))
            grads["lambda_re"][i], grads["lambda_im"][i], grads["log_step"][i] = dlr, dli, dls
            grads["b_re"][i], grads["b_im"][i] = dbr, dbi
            grads["c_re"][i], grads["c_im"][i] = _c_blockdiag_t(g["cbd_r"]), _c_blockdiag_t(g["cbd_i"])
            for n in ("ssm_norm", "d_skip"):
                grads[n][i] = g[n]
    grads = {n: jnp.stack(v) for n, v in grads.items()}
    return jnp.sum(lslab), dx, grads


def kernel(x, attn_norm, mix_w_in, cq_norm, ckv_norm, w_uq, w_ukv, q_gain, k_gain, sconv_w, mix_w_out, ssm_norm, ssm_w_in, lambda_re, lambda_im, log_step, b_re, b_im, c_re, c_im, d_skip, w_glu, ffn_norm, ffn_w_up, ffn_conv_w, ffn_w_down, loss_target, m_attn_norm, m_mix_w_in, m_cq_norm, m_ckv_norm, m_w_uq, m_w_ukv, m_q_gain, m_k_gain, m_sconv_w, m_mix_w_out, m_ssm_norm, m_ssm_w_in, m_lambda_re, m_lambda_im, m_log_step, m_b_re, m_b_im, m_c_re, m_c_im, m_d_skip, m_w_glu, m_ffn_norm, m_ffn_w_up, m_ffn_conv_w, m_ffn_w_down, v_attn_norm, v_mix_w_in, v_cq_norm, v_ckv_norm, v_w_uq, v_w_ukv, v_q_gain, v_k_gain, v_sconv_w, v_mix_w_out, v_ssm_norm, v_ssm_w_in, v_lambda_re, v_lambda_im, v_log_step, v_b_re, v_b_im, v_c_re, v_c_im, v_d_skip, v_w_glu, v_ffn_norm, v_ffn_w_up, v_ffn_conv_w, v_ffn_w_down):
    args = dict(locals())
    w = {n: args[n] for n in ORDER}
    m = {n: args["m_" + n] for n in ORDER}
    v = {n: args["v_" + n] for n in ORDER}
    me = 2 * lax.axis_index("x") + lax.axis_index("y")

    axis = dict(BIG)

    def landing(own):
        return lax.dynamic_update_index_in_dim(jnp.zeros((NCHIP,) + own.shape, own.dtype), own, me, 0)

    gs = _chip_exchange([w[n] for n, _ in SMALL], False, "gather_w_f32")
    full = {n: w[n] for n in REPL}
    for (n, ax), g in zip(SMALL, gs):
        full[n] = _join(g, ax)
    parts = [(("even", "odd")[layer % 2], layer // 2) for layer in range(DEPTH)]
    parts = [p for layer, mixer in enumerate(parts) for p in (mixer, ("ffn", layer))]
    gathers, tok = {}, 0.0
    for kind, idx in parts:
        shards = [w[n][idx].astype(BF16) for n in MATMUL_WEIGHTS[kind]]
        gathers[kind, idx] = exchange_start(shards, [landing(sh) for sh in shards], False, f"gather_start_{kind}{idx}")
        tok = tok + gathers[kind, idx][4]

    def getw(kind, idx, after):
        ssem, rsem, bufs, lands, _ = gathers[kind, idx]
        got = exchange_wait(ssem, rsem, bufs, lands, after, False, f"gather_wait_{kind}{idx}")
        return {n: _join(g, axis[n] - 1) for n, g in zip(MATMUL_WEIGHTS[kind], got)}

    scatters = []

    def putg(kind, idx, g):
        blocks = [_split(g[n], axis[n] - 1) for n in MATMUL_WEIGHTS[kind]]
        lands = [landing(lax.dynamic_index_in_dim(b, me, 0, keepdims=False)) for b in blocks]
        scatters.append((kind, idx, exchange_start(blocks, lands, True, f"scatter_start_{kind}{idx}")))
        return scatters[-1][2][4]

    sq, dx, grads = _local_step(x[0], loss_target[0], full, getw, putg, tok)
    loss = lax.psum(0.5 * sq / D_MODEL, ("x", "y", "c"))

    rep_names = REPL + [n for n, _ in SMALL]
    names = [n for n, _ in BIG] + rep_names
    summed = {}
    for kind, idx, (ssem, rsem, bufs, lands, _) in scatters:
        got = exchange_wait(ssem, rsem, bufs, lands, dx, True, f"scatter_wait_{kind}{idx}")
        for n, sl in zip(MATMUL_WEIGHTS[kind], got):
            summed[n, idx] = sum_slots(_rows2d(sl, 1), name=f"sum_{n}{idx}")
    mine = [jnp.concatenate([summed[n, idx] for idx in range(w[n].shape[0])], axis=0) for n, _ in BIG]
    slots = _chip_exchange([_rows2d(grads[n]) for n in rep_names], False, "gather_g_f32")
    mine += [sum_slots(_rows2d(sl, 1), name=f"sum_{n}") for n, sl in zip(rep_names, slots)]
    other = sibling_swap(mine, "swap_g")

    def local(n, p):
        ax = dict(SMALL).get(n)
        if ax is None:
            return p
        part = lax.dynamic_index_in_dim(_split(p.reshape(grads[n].shape), ax), me, 0, keepdims=False)
        return _rows2d(part)

    outs = {}
    for n, p, q in zip(names, mine, other):
        res = adamw(local(n, p), local(n, q), _rows2d(w[n]), _rows2d(m[n]), _rows2d(v[n]), name=f"adamw_{n}")
        outs[n] = [r.reshape(w[n].shape) for r in res]
    return (loss, dx[None], *[outs[n][0] for n in ORDER], *[outs[n][1] for n in ORDER],
            *[outs[n][2] for n in ORDER], *[outs[n][3] for n in ORDER])
```

```python
import functools
import math

import numpy as np
import jax
import jax.numpy as jnp
from jax import lax
from jax.experimental import pallas as pl
from jax.experimental.pallas import tpu as pltpu

F32, BF16 = jnp.float32, jnp.bfloat16

D_MODEL = 1024
DEPTH = 4
HEADS = 8
NOPE, ROPE, QK, VD = 64, 32, 96, 64
HP = 128
QR, KVR = 256, 256
CONVC = 512
FFN_H = 2816
G, P, GC = 64, 64, 16
NST = G * P
SLAB = 8
LANE = 128
EPS = 1e-6
ROPE_THETA = 10000.0
ADAM_LR, ADAM_B1, ADAM_B2, ADAM_EPS, ADAM_WD, ADAM_STEP = 0.001, 0.9, 0.999, 1e-08, 0.01, 10
VMEM_LIMIT = 48 * 1024 * 1024
MESH = pl.DeviceIdType.MESH
ANY = pl.BlockSpec(memory_space=pl.ANY)


def _cparams(*sem):
    return pltpu.CompilerParams(dimension_semantics=sem, vmem_limit_bytes=VMEM_LIMIT)


def _pick(dim, prefs):
    for p in prefs:
        if dim % p == 0:
            return p
    return dim


def _rows(s):
    return _pick(s, (512, 256, 128, 64, 32, 16, 8))


MM_VMEM_BUDGET = 36 * 1024 * 1024
MM_MAX_TILE_ELEMS = 640 * 1024
HBM_BYTES_PER_US = 3.0e6
STEP_OVERHEAD_US = 0.35


def _lane_tiles(n):
    c = {t for t in range(LANE, min(n, 1536) + 1, LANE) if n % t == 0}
    if n <= 2304 or not c:
        c.add(n)
    return sorted(c, reverse=True)


def _mm_tiles(m, n, k, sa, sb, so):
    best = None
    for tm in [t for t in (1024, 512, 256) if m % t == 0] or [m]:
        for tn in _lane_tiles(n):
            if tm * tn > MM_MAX_TILE_ELEMS:
                continue
            if 2 * (tm * k * sa + k * tn * sb + tm * tn * so) + 4 * tm * tn > MM_VMEM_BUDGET:
                continue
            steps = (m // tm) * (n // tn)
            for inner_n in (True, False):
                moved = (m * k * sa + (m // tm) * k * n * sb) if inner_n else (k * n * sb + (n // tn) * m * k * sa)
                cost = (moved + m * n * so) / HBM_BYTES_PER_US + steps * STEP_OVERHEAD_US
                if best is None or cost < best[0]:
                    best = (cost, tm, tn, inner_n)
    return best[1:]


def mm(a, b, *, ta=False, tb=False, add=None, out_dtype=F32, name):
    if ta:
        kdim, m = a.shape
    else:
        m, kdim = a.shape
    n = b.shape[0] if tb else b.shape[1]
    so = jnp.dtype(out_dtype).itemsize + (0 if add is None else add.dtype.itemsize)
    tm, tn, inner_n = _mm_tiles(m, n, kdim, a.dtype.itemsize, b.dtype.itemsize, so)
    dn = (((0 if ta else 1,), (1 if tb else 0,)), ((), ()))

    def body(*refs):
        if add is None:
            a_ref, b_ref, o_ref = refs
        else:
            a_ref, b_ref, add_ref, o_ref = refs
        r = lax.dot_general(a_ref[...].astype(BF16), b_ref[...].astype(BF16), dn, preferred_element_type=F32)
        if add is not None:
            r = r + add_ref[...].astype(F32)
        o_ref[...] = r.astype(out_dtype)

    ij = (lambda g0, g1: (g0, g1)) if inner_n else (lambda g0, g1: (g1, g0))
    a_spec = (pl.BlockSpec((kdim, tm), lambda g0, g1: (0, ij(g0, g1)[0])) if ta
              else pl.BlockSpec((tm, kdim), lambda g0, g1: (ij(g0, g1)[0], 0)))
    b_spec = (pl.BlockSpec((tn, kdim), lambda g0, g1: (ij(g0, g1)[1], 0)) if tb
              else pl.BlockSpec((kdim, tn), lambda g0, g1: (0, ij(g0, g1)[1])))
    o_spec = pl.BlockSpec((tm, tn), lambda g0, g1: ij(g0, g1))
    ins, specs = [a, b], [a_spec, b_spec]
    if add is not None:
        ins.append(add)
        specs.append(o_spec)
    grid = (m // tm, n // tn) if inner_n else (n // tn, m // tm)
    return pl.pallas_call(
        body, name=name, grid=grid, in_specs=specs, out_specs=o_spec,
        out_shape=jax.ShapeDtypeStruct((m, n), out_dtype),
        compiler_params=_cparams("parallel", "parallel"))(*ins)


def rms_fwd(x, g, *, col=0, out_dtype=BF16, name):
    s = x.shape[0]
    d = g.shape[1]
    tm = _rows(s)

    def body(x_ref, g_ref, o_ref):
        xv = x_ref[...]
        r = lax.rsqrt(jnp.mean(xv * xv, axis=-1, keepdims=True) + EPS)
        o_ref[...] = (xv * r * g_ref[...]).astype(out_dtype)

    return pl.pallas_call(
        body, name=name, grid=(s // tm,),
        in_specs=[pl.BlockSpec((tm, d), lambda i: (i, col)), pl.BlockSpec((1, d), lambda i: (0, 0))],
        out_specs=pl.BlockSpec((tm, d), lambda i: (i, 0)),
        out_shape=jax.ShapeDtypeStruct((s, d), out_dtype),
        compiler_params=_cparams("parallel"))(x, g)


def rms_bwd(x, g, dy, *, col=0, add=None, out_dtype=F32, twin=False, name):
    s = x.shape[0]
    d = g.shape[1]
    tm = _rows(s)

    def body(*refs):
        refs = list(refs)
        dg_ref = refs.pop()
        dxh_ref = refs.pop() if twin else None
        dx_ref = refs.pop()
        add_ref = refs.pop() if add is not None else None
        x_ref, g_ref, dy_ref = refs

        @pl.when(pl.program_id(0) == 0)
        def _():
            dg_ref[...] = jnp.zeros_like(dg_ref)

        xv = x_ref[...]
        dyv = dy_ref[...].astype(F32)
        r = lax.rsqrt(jnp.mean(xv * xv, axis=-1, keepdims=True) + EPS)
        xh = xv * r
        dg_ref[...] += jnp.sum(dyv * xh, axis=0, keepdims=True)
        dxh = dyv * g_ref[...]
        dx = r * (dxh - xh * jnp.mean(dxh * xh, axis=-1, keepdims=True))
        if add is not None:
            dx = dx + add_ref[...]
        dx_ref[...] = dx.astype(out_dtype)
        if twin:
            dxh_ref[...] = dx.astype(BF16)

    row = pl.BlockSpec((tm, d), lambda i: (i, 0))
    vec = pl.BlockSpec((1, d), lambda i: (0, 0))
    ins = [x, g, dy]
    specs = [pl.BlockSpec((tm, d), lambda i: (i, col)), vec, row]
    if add is not None:
        ins.append(add)
        specs.append(row)
    dxs = [jax.ShapeDtypeStruct((s, d), out_dtype)] + ([jax.ShapeDtypeStruct((s, d), BF16)] if twin else [])
    return pl.pallas_call(
        body, name=name, grid=(s // tm,), in_specs=specs,
        out_specs=[row] * len(dxs) + [vec],
        out_shape=dxs + [jax.ShapeDtypeStruct((1, d), F32)],
        compiler_params=_cparams("arbitrary"))(*ins)


def _rope_tables(s):
    inv = 1.0 / (ROPE_THETA ** (jnp.arange(0, ROPE, 2, dtype=F32) / ROPE))
    ang = jnp.arange(s, dtype=F32)[:, None] * inv[None, :]
    cos, sin = jnp.cos(ang), jnp.sin(ang)
    z = lambda w: jnp.zeros((s, w), F32)
    c = jnp.concatenate([jnp.ones((s, NOPE), F32), cos, cos, z(HP - QK)], axis=1)
    s1 = jnp.concatenate([z(NOPE), -sin, z(HP - NOPE - ROPE // 2)], axis=1)
    s2 = jnp.concatenate([z(NOPE + ROPE // 2), sin, z(HP - QK)], axis=1)
    return c, s1, s2


def qkprep_fwd(qraw, kv, proj, qg, kg, tabs, *, name):
    s = qraw.shape[0]
    tm = _rows(s)
    kr_col = (proj.shape[1] - HP) // HP

    def body(q_ref, k_ref, v_ref, kr_ref, qg_ref, kg_ref, c_ref, s1_ref, s2_ref, qo_ref, ko_ref, vo_ref):
        c, s1, s2 = c_ref[...], s1_ref[...], s2_ref[...]

        def f(xv, gain):
            r = lax.rsqrt(jnp.sum(xv * xv, axis=-1, keepdims=True) * (1.0 / QK) + EPS)
            xn = xv * r * gain
            return xn * c + pltpu.roll(xn, HP - ROPE // 2, 1) * s1 + pltpu.roll(xn, ROPE // 2, 1) * s2

        qo_ref[...] = f(q_ref[...], qg_ref[...]).astype(BF16)
        ko_ref[...] = f(k_ref[...] + kr_ref[...], kg_ref[...]).astype(BF16)
        vv = v_ref[...]
        lane = lax.broadcasted_iota(jnp.int32, vv.shape, 1)
        vo_ref[...] = jnp.where(lane == VD, 1.0, vv).astype(BF16)

    head = pl.BlockSpec((tm, HP), lambda i, h: (i, h))
    tab = pl.BlockSpec((tm, HP), lambda i, h: (i, 0))
    gain = pl.BlockSpec((1, HP), lambda i, h: (0, 0))
    return pl.pallas_call(
        body, name=name, grid=(s // tm, HEADS),
        in_specs=[head, head, pl.BlockSpec((tm, HP), lambda i, h: (i, HEADS + h)),
                  pl.BlockSpec((tm, HP), lambda i, h: (i, kr_col)), gain, gain, tab, tab, tab],
        out_specs=[head, head, head],
        out_shape=[jax.ShapeDtypeStruct((s, HEADS * HP), BF16)] * 3,
        compiler_params=_cparams("parallel", "parallel"))(qraw, kv, kv, proj, qg, kg, *tabs)


def qkprep_bwd(qraw, kv, proj, qg, kg, tabs, dq, dk, *, name):
    s = qraw.shape[0]
    tm = _rows(s)
    kr_col = (proj.shape[1] - HP) // HP

    def body(q_ref, k_ref, kr_ref, qg_ref, kg_ref, c_ref, s1_ref, s2_ref, dq_ref, dk_ref,
             dqr_ref, dkr_ref, dkrope_ref, dqg_ref, dkg_ref):
        i, h = pl.program_id(0), pl.program_id(1)
        c, s1, s2 = c_ref[...], s1_ref[...], s2_ref[...]

        @pl.when((i == 0) & (h == 0))
        def _():
            dqg_ref[...] = jnp.zeros_like(dqg_ref)
            dkg_ref[...] = jnp.zeros_like(dkg_ref)

        @pl.when(h == 0)
        def _():
            dkrope_ref[...] = jnp.zeros_like(dkrope_ref)

        def f(xv, gain, dout):
            r = lax.rsqrt(jnp.sum(xv * xv, axis=-1, keepdims=True) * (1.0 / QK) + EPS)
            xh = xv * r
            dxn = dout * c + pltpu.roll(dout * s1, ROPE // 2, 1) + pltpu.roll(dout * s2, HP - ROPE // 2, 1)
            dgain = jnp.sum(dxn * xh, axis=0, keepdims=True)
            dxh = dxn * gain
            dx = r * (dxh - xh * (jnp.sum(dxh * xh, axis=-1, keepdims=True) * (1.0 / QK)))
            return dx, dgain

        dxq, dgq = f(q_ref[...], qg_ref[...], dq_ref[...])
        dxk, dgk = f(k_ref[...] + kr_ref[...], kg_ref[...], dk_ref[...])
        dqr_ref[...] = dxq.astype(BF16)
        dkr_ref[...] = dxk.astype(BF16)
        dqg_ref[...] += dgq
        dkg_ref[...] += dgk
        lane = lax.broadcasted_iota(jnp.int32, dxk.shape, 1)
        dkrope_ref[...] += jnp.where((lane >= NOPE) & (lane < QK), dxk, 0.0)

    head = pl.BlockSpec((tm, HP), lambda i, h: (i, h))
    tab = pl.BlockSpec((tm, HP), lambda i, h: (i, 0))
    gain = pl.BlockSpec((1, HP), lambda i, h: (0, 0))
    return pl.pallas_call(
        body, name=name, grid=(s // tm, HEADS),
        in_specs=[head, head, pl.BlockSpec((tm, HP), lambda i, h: (i, kr_col)), gain, gain, tab, tab, tab, head, head],
        out_specs=[head, head, tab, gain, gain],
        out_shape=[jax.ShapeDtypeStruct((s, HEADS * HP), BF16)] * 2
        + [jax.ShapeDtypeStruct((s, HP), F32), jax.ShapeDtypeStruct((1, HP), F32), jax.ShapeDtypeStruct((1, HP), F32)],
        compiler_params=_cparams("arbitrary", "arbitrary"))(qraw, kv, proj, qg, kg, *tabs, dq, dk)


ATT_SCALE = QK ** -0.5
NEG = -1e30


def _att_tile(s):
    return _pick(s, (512, 256, 128))


def _causal(sv, diag):
    r = lax.broadcasted_iota(jnp.int32, sv.shape, 0)
    c = lax.broadcasted_iota(jnp.int32, sv.shape, 1)
    return jnp.where(diag & (c > r), NEG, sv)


NT = (((1,), (1,)), ((), ()))
TN = (((0,), (0,)), ((), ()))


def _row_of(col):
    return jnp.broadcast_to(col, (col.shape[0], LANE)).T[0:SLAB, :]


def _att_specs(s):
    t = _att_tile(s)
    nb = s // t
    tile = pl.BlockSpec((t, HP), lambda h, i: (i, h))
    whole = pl.BlockSpec((s, HP), lambda h, i: (0, h))
    row = pl.BlockSpec((1, 1, SLAB, t), lambda h, i: (h, i, 0, 0))
    rows = pl.BlockSpec((1, nb, SLAB, t), lambda h, i: (h, 0, 0, 0))
    return t, nb, tile, whole, row, rows


def attn_fwd(q, k, v, *, name):
    s = q.shape[0]
    t, nb, tile, whole, row, _ = _att_specs(s)

    def body(q_ref, k_ref, v_ref, o_ref, oh_ref, lse_ref, lset_ref, s_scr, mb_scr, acc):
        qb = pl.program_id(1)
        qv = q_ref[...]

        def scores(j):
            r0 = pl.multiple_of(j * t, t)
            return lax.dot_general(qv, k_ref[pl.ds(r0, t), :], NT, preferred_element_type=F32) * ATT_SCALE

        def fold(sv):
            m = sv[:, 0:LANE]
            for kk in range(1, t // LANE):
                m = jnp.maximum(m, sv[:, kk * LANE:(kk + 1) * LANE])
            return m

        def first(j, m):
            sv = scores(j)
            s_scr[j] = sv
            return jnp.maximum(m, fold(sv))

        m = lax.fori_loop(0, qb, first, jnp.full((t, LANE), NEG, F32))
        sd = _causal(scores(qb), True)
        s_scr[qb] = sd
        mcol = jnp.max(jnp.maximum(m, fold(sd)), axis=-1, keepdims=True)
        mb_scr[...] = jnp.broadcast_to(mcol, (t, t))
        acc[...] = jnp.zeros_like(acc)

        def second(j, carry):
            r0 = pl.multiple_of(j * t, t)
            p = jnp.exp(s_scr[j] - mb_scr[...]).astype(BF16)
            acc[...] += jnp.dot(p, v_ref[pl.ds(r0, t), :], preferred_element_type=F32)
            return carry

        lax.fori_loop(0, qb + 1, second, 0)
        av = acc[...]
        lsum = av[:, VD:VD + 1]
        lane = lax.broadcasted_iota(jnp.int32, av.shape, 1)
        ov = jnp.where(lane == VD, 0.0, av / lsum)
        o_ref[...] = ov
        oh_ref[...] = ov.astype(BF16)
        lse = mcol + jnp.log(lsum)
        lse_ref[...] = jnp.broadcast_to(lse, lse_ref.shape)
        lset_ref[0, 0] = _row_of(lse)

    return pl.pallas_call(
        body, name=name, grid=(HEADS, nb), in_specs=[tile, whole, whole], out_specs=[tile, tile, tile, row],
        out_shape=[jax.ShapeDtypeStruct((s, HEADS * HP), F32), jax.ShapeDtypeStruct((s, HEADS * HP), BF16),
                   jax.ShapeDtypeStruct((s, HEADS * HP), F32), jax.ShapeDtypeStruct((HEADS, nb, SLAB, t), F32)],
        scratch_shapes=[pltpu.VMEM((nb, t, t), F32), pltpu.VMEM((t, t), F32), pltpu.VMEM((t, HP), F32)],
        compiler_params=_cparams("parallel", "parallel"))(q, k, v)


def attn_bwd_dq(q, k, v, o, lse, do, *, name):
    s = q.shape[0]
    t, nb, tile, whole, row, _ = _att_specs(s)

    def body(q_ref, k_ref, v_ref, o_ref, lse_ref, do_ref, dq_ref, doh_ref, dt_ref, lb_scr, db_scr, acc):
        qb = pl.program_id(1)
        qv = q_ref[...]
        dov = do_ref[...]
        dob = dov.astype(BF16)
        doh_ref[...] = dob
        delta = jnp.sum(dov * o_ref[...], axis=-1, keepdims=True)
        dt_ref[0, 0] = _row_of(delta)
        lb_scr[...] = jnp.broadcast_to(lse_ref[...][:, 0:1], (t, t))
        db_scr[...] = jnp.broadcast_to(delta, (t, t))
        acc[...] = jnp.zeros_like(acc)

        def step(j, diag):
            r0 = pl.multiple_of(j * t, t)
            kj = k_ref[pl.ds(r0, t), :]
            sv = lax.dot_general(qv, kj, NT, preferred_element_type=F32) * ATT_SCALE
            if diag:
                sv = _causal(sv, True)
            p = jnp.exp(sv - lb_scr[...])
            dp = lax.dot_general(dob, v_ref[pl.ds(r0, t), :], NT, preferred_element_type=F32)
            ds = (p * (dp - db_scr[...])).astype(BF16)
            acc[...] += jnp.dot(ds, kj, preferred_element_type=F32)

        def off_diag(j, carry):
            step(j, False)
            return carry

        lax.fori_loop(0, qb, off_diag, 0)
        step(qb, True)
        dq_ref[...] = acc[...] * ATT_SCALE

    return pl.pallas_call(
        body, name=name, grid=(HEADS, nb), in_specs=[tile, whole, whole, tile, tile, tile],
        out_specs=[tile, tile, row],
        out_shape=[jax.ShapeDtypeStruct((s, HEADS * HP), F32), jax.ShapeDtypeStruct((s, HEADS * HP), BF16),
                   jax.ShapeDtypeStruct((HEADS, nb, SLAB, t), F32)],
        scratch_shapes=[pltpu.VMEM((t, t), F32), pltpu.VMEM((t, t), F32), pltpu.VMEM((t, HP), F32)],
        compiler_params=_cparams("parallel", "parallel"))(q, k, v, o, lse, do)


def attn_bwd_dkv(q, k, v, doh, lset, deltat, *, name):
    s = q.shape[0]
    t, nb, tile, whole, _, rows = _att_specs(s)

    def body(q_ref, k_ref, v_ref, do_ref, lt_ref, dt_ref, dk_ref, dv_ref, dk_acc, dv_acc):
        kb = pl.program_id(1)
        kt, vt = k_ref[...], v_ref[...]
        dk_acc[...] = jnp.zeros_like(dk_acc)
        dv_acc[...] = jnp.zeros_like(dv_acc)

        def step(i, diag):
            r0 = pl.multiple_of(i * t, t)
            qi, doi = q_ref[pl.ds(r0, t), :], do_ref[pl.ds(r0, t), :]
            st = lax.dot_general(kt, qi, NT, preferred_element_type=F32) * ATT_SCALE
            if diag:
                kr = lax.broadcasted_iota(jnp.int32, st.shape, 0)
                qc = lax.broadcasted_iota(jnp.int32, st.shape, 1)
                st = jnp.where(kr > qc, NEG, st)
            pt = jnp.exp(st - lt_ref[0, i][0:1, :])
            dpt = lax.dot_general(vt, doi, NT, preferred_element_type=F32)
            dst = (pt * (dpt - dt_ref[0, i][0:1, :])).astype(BF16)
            dv_acc[...] += jnp.dot(pt.astype(BF16), doi, preferred_element_type=F32)
            dk_acc[...] += jnp.dot(dst, qi, preferred_element_type=F32)

        def off_diag(i, carry):
            step(i, False)
            return carry

        step(kb, True)
        lax.fori_loop(kb + 1, nb, off_diag, 0)
        dk_ref[...] = dk_acc[...] * ATT_SCALE
        dvv = dv_acc[...]
        lane = lax.broadcasted_iota(jnp.int32, dvv.shape, 1)
        dv_ref[...] = jnp.where(lane == VD, 0.0, dvv).astype(BF16)

    return pl.pallas_call(
        body, name=name, grid=(HEADS, nb), in_specs=[whole, tile, tile, whole, rows, rows], out_specs=[tile, tile],
        out_shape=[jax.ShapeDtypeStruct((s, HEADS * HP), F32), jax.ShapeDtypeStruct((s, HEADS * HP), BF16)],
        scratch_shapes=[pltpu.VMEM((t, HP), F32), pltpu.VMEM((t, HP), F32)],
        compiler_params=_cparams("parallel", "parallel"))(q, k, v, doh, lset, deltat)


HALO = 8
CW = 256


def _conv3(zw, w):
    return w[2:3] * zw + w[1:2] * pltpu.roll(zw, 1, 0) + w[0:1] * pltpu.roll(zw, 2, 0)


def _conv3_t(dc, w):
    n = dc.shape[0]
    return w[2:3] * dc + w[1:2] * pltpu.roll(dc, n - 1, 0) + w[0:1] * pltpu.roll(dc, n - 2, 0)


def _conv3_dw(dc, zw, r):
    z0 = zw[HALO:HALO + r]
    z1 = pltpu.roll(zw, 1, 0)[HALO:HALO + r]
    z2 = pltpu.roll(zw, 2, 0)[HALO:HALO + r]
    return [jnp.sum(dc * z, axis=0, keepdims=True) for z in (z2, z1, z0)]


def _halo_specs(r, colfn):
    rb = r // HALO
    cur = pl.BlockSpec((r, CW), lambda j, i: (i, colfn(j)))
    prev = pl.BlockSpec((HALO, CW), lambda j, i: (jnp.maximum(i * rb - 1, 0), colfn(j)))

    def nxt(nrow_blocks):
        return pl.BlockSpec((HALO, CW), lambda j, i: (jnp.minimum((i + 1) * rb, nrow_blocks * rb - 1), colfn(j)))

    return cur, prev, nxt


def ffnact_fwd(up, w, *, name):
    s, c2 = up.shape
    hh = c2 // 2
    nj = hh // CW
    r = _rows(s)
    nt = s // r

    def body(g_ref, gp_ref, v_ref, vp_ref, wg_ref, wv_ref, o_ref):
        pm = (pl.program_id(1) > 0).astype(F32)
        cg = _conv3(jnp.concatenate([gp_ref[...] * pm, g_ref[...]], axis=0), wg_ref[...])[HALO:]
        cv = _conv3(jnp.concatenate([vp_ref[...] * pm, v_ref[...]], axis=0), wv_ref[...])[HALO:]
        o_ref[...] = (cg * jax.nn.sigmoid(cg) * cv).astype(BF16)

    gcur, gprev, _ = _halo_specs(r, lambda j: j)
    vcur, vprev, _ = _halo_specs(r, lambda j: nj + j)
    wg = pl.BlockSpec((3, CW), lambda j, i: (0, j))
    wv = pl.BlockSpec((3, CW), lambda j, i: (0, nj + j))
    return pl.pallas_call(
        body, name=name, grid=(nj, nt), in_specs=[gcur, gprev, vcur, vprev, wg, wv],
        out_specs=pl.BlockSpec((r, CW), lambda j, i: (i, j)),
        out_shape=jax.ShapeDtypeStruct((s, hh), BF16),
        compiler_params=_cparams("parallel", "parallel"))(up, up, up, up, w, w)


def ffnact_bwd(up, w, dact, *, name):
    s, c2 = up.shape
    hh = c2 // 2
    nj = hh // CW
    r = _rows(s)
    nt = s // r

    def body(g_ref, gp_ref, gn_ref, v_ref, vp_ref, vn_ref, wg_ref, wv_ref, da_ref, dan_ref,
             dg_ref, dv_ref, dwg_ref, dwv_ref):
        i = pl.program_id(1)
        pm = (i > 0).astype(F32)
        nm = (i < nt - 1).astype(F32)

        @pl.when(i == 0)
        def _():
            dwg_ref[...] = jnp.zeros_like(dwg_ref)
            dwv_ref[...] = jnp.zeros_like(dwv_ref)

        wg, wv = wg_ref[...], wv_ref[...]
        zg = jnp.concatenate([gp_ref[...] * pm, g_ref[...], gn_ref[...]], axis=0)
        zv = jnp.concatenate([vp_ref[...] * pm, v_ref[...], vn_ref[...]], axis=0)
        cg = _conv3(zg, wg)[HALO:]
        cv = _conv3(zv, wv)[HALO:]
        da = jnp.concatenate([da_ref[...], dan_ref[...] * nm], axis=0)
        sg = jax.nn.sigmoid(cg)
        dcg = da * cv * (sg * (1.0 + cg * (1.0 - sg)))
        dcv = da * (cg * sg)
        dg_ref[...] = _conv3_t(dcg, wg)[:r].astype(BF16)
        dv_ref[...] = _conv3_t(dcv, wv)[:r].astype(BF16)
        for kk, (a, b) in enumerate(zip(_conv3_dw(dcg[:r], zg, r), _conv3_dw(dcv[:r], zv, r))):
            dwg_ref[kk:kk + 1, :] += a
            dwv_ref[kk:kk + 1, :] += b

    gcur, gprev, gnext = _halo_specs(r, lambda j: j)
    vcur, vprev, vnext = _halo_specs(r, lambda j: nj + j)
    acur, _, anext = _halo_specs(r, lambda j: j)
    wg = pl.BlockSpec((3, CW), lambda j, i: (0, j))
    wv = pl.BlockSpec((3, CW), lambda j, i: (0, nj + j))
    dupg, dupv, dwg, dwv = pl.pallas_call(
        body, name=name, grid=(nj, nt),
        in_specs=[gcur, gprev, gnext(nt), vcur, vprev, vnext(nt), wg, wv, acur, anext(nt)],
        out_specs=[acur, acur, wg, wg],
        out_shape=[jax.ShapeDtypeStruct((s, hh), BF16), jax.ShapeDtypeStruct((s, hh), BF16),
                   jax.ShapeDtypeStruct((3, hh), F32), jax.ShapeDtypeStruct((3, hh), F32)],
        compiler_params=_cparams("parallel", "arbitrary"))(up, up, up, up, up, up, w, w, dact, dact)
    return jnp.concatenate([dupg, dupv], axis=1), jnp.concatenate([dwg, dwv], axis=1)


def sconv_fwd(proj, w, *, name):
    s = proj.shape[0]
    nj = CONVC // CW
    r = _rows(s)
    nt = s // r

    def body(b_ref, c_ref, cp_ref, x_ref, xp_ref, w_ref, o_ref):
        pm = (pl.program_id(1) > 0).astype(F32)
        zw = jnp.concatenate([cp_ref[...] * xp_ref[...] * pm, c_ref[...] * x_ref[...]], axis=0)
        o_ref[...] = (b_ref[...] * _conv3(zw, w_ref[...])[HALO:]).astype(BF16)

    bcur, _, _ = _halo_specs(r, lambda j: (QR + KVR) // CW + j)
    ccur, cprev, _ = _halo_specs(r, lambda j: (QR + KVR + CONVC) // CW + j)
    xcur, xprev, _ = _halo_specs(r, lambda j: (QR + KVR + 2 * CONVC) // CW + j)
    ws = pl.BlockSpec((3, CW), lambda j, i: (0, j))
    return pl.pallas_call(
        body, name=name, grid=(nj, nt), in_specs=[bcur, ccur, cprev, xcur, xprev, ws],
        out_specs=pl.BlockSpec((r, CW), lambda j, i: (i, j)),
        out_shape=jax.ShapeDtypeStruct((s, CONVC), BF16),
        compiler_params=_cparams("parallel", "parallel"))(proj, proj, proj, proj, proj, w)


def sconv_bwd(proj, w, dy, *, name):
    s = proj.shape[0]
    nj = CONVC // CW
    r = _rows(s)
    nt = s // r

    def body(b_ref, bn_ref, c_ref, cp_ref, x_ref, xp_ref, w_ref, dy_ref, dyn_ref, db_ref, dc_ref, dx_ref, dw_ref):
        i = pl.program_id(1)
        pm = (i > 0).astype(F32)
        nm = (i < nt - 1).astype(F32)

        @pl.when(i == 0)
        def _():
            dw_ref[...] = jnp.zeros_like(dw_ref)

        wv = w_ref[...]
        zw = jnp.concatenate([cp_ref[...] * xp_ref[...] * pm, c_ref[...] * x_ref[...]], axis=0)
        conv = _conv3(zw, wv)[HALO:]
        dyv = dy_ref[...]
        db_ref[...] = (dyv * conv).astype(BF16)
        dconv = jnp.concatenate([dyv * b_ref[...], dyn_ref[...] * bn_ref[...] * nm], axis=0)
        dz = _conv3_t(dconv, wv)[:r]
        dc_ref[...] = (dz * x_ref[...]).astype(BF16)
        dx_ref[...] = (dz * c_ref[...]).astype(BF16)
        for kk, a in enumerate(_conv3_dw(dconv[:r], zw, r)):
            dw_ref[kk:kk + 1, :] += a

    bcur, _, bnext = _halo_specs(r, lambda j: (QR + KVR) // CW + j)
    ccur, cprev, _ = _halo_specs(r, lambda j: (QR + KVR + CONVC) // CW + j)
    xcur, xprev, _ = _halo_specs(r, lambda j: (QR + KVR + 2 * CONVC) // CW + j)
    ycur, _, ynext = _halo_specs(r, lambda j: j)
    ws = pl.BlockSpec((3, CW), lambda j, i: (0, j))
    out = pl.BlockSpec((r, CW), lambda j, i: (i, j))
    db, dc, dx, dw = pl.pallas_call(
        body, name=name, grid=(nj, nt),
        in_specs=[bcur, bnext(nt), ccur, cprev, xcur, xprev, ws, ycur, ynext(nt)],
        out_specs=[out, out, out, ws],
        out_shape=[jax.ShapeDtypeStruct((s, CONVC), BF16)] * 3 + [jax.ShapeDtypeStruct((3, CONVC), F32)],
        compiler_params=_cparams("parallel", "arbitrary"))(proj, proj, proj, proj, proj, proj, w, dy, dy)
    return jnp.concatenate([db, dc, dx], axis=1), dw


SW = 512
NJ = NST // SW


def _scan_tables(ar, ai):
    def cmul(x, y):
        return x[0] * y[0] - x[1] * y[1], x[0] * y[1] + x[1] * y[0]

    def build(a, reverse):
        pw = [a]
        for _ in range(SLAB - 1):
            pw.append(cmul(pw[-1], a))
        row = jnp.arange(SLAB)[:, None]
        tabs = []
        for kk in (1, 2, 4):
            mask = ((row < SLAB - kk) if reverse else (row >= kk)).astype(F32)
            tabs += [mask * pw[kk - 1][0][None, :], mask * pw[kk - 1][1][None, :]]
        order = list(range(SLAB - 1, -1, -1)) if reverse else list(range(SLAB))
        tabs += [jnp.stack([pw[o][0] for o in order]), jnp.stack([pw[o][1] for o in order])]
        return jnp.stack(tabs)

    return build((ar, ai), False), build((ar, -ai), True)


def _slab_scan(xr, xi, tabs, cr, ci, reverse):
    for n, kk in enumerate((1, 2, 4)):
        sh = SLAB - kk if reverse else kk
        tr, ti = tabs[2 * n], tabs[2 * n + 1]
        sr, si = pltpu.roll(xr, sh, 0), pltpu.roll(xi, sh, 0)
        xr, xi = xr + tr * sr - ti * si, xi + tr * si + ti * sr
    tr, ti = tabs[6], tabs[7]
    return xr + tr * cr - ti * ci, xi + tr * ci + ti * cr


def s5_fwd(u, bbd_r, bbd_i, cbd_r, cbd_i, tab, *, name):
    s = u.shape[0]
    tbk = _rows(s)
    nt = s // tbk
    nsl = tbk // SLAB

    def body(u_ref, br_ref, bi_ref, cr_ref, ci_ref, tab_ref, y_ref, sr_ref, si_ref, bur, bui, carry):
        @pl.when(pl.program_id(1) == 0)
        def _():
            carry[...] = jnp.zeros_like(carry)

        ub = u_ref[...].astype(BF16)
        bur[...] = jnp.dot(ub, br_ref[0], preferred_element_type=F32)
        bui[...] = jnp.dot(ub, bi_ref[0], preferred_element_type=F32)
        tabs = [tab_ref[n] for n in range(8)]

        def slab(n, c):
            r0 = pl.multiple_of(n * SLAB, SLAB)
            sr, si = _slab_scan(bur[pl.ds(r0, SLAB), :], bui[pl.ds(r0, SLAB), :], tabs, c[0], c[1], False)
            sr_ref[pl.ds(r0, SLAB), :] = sr
            si_ref[pl.ds(r0, SLAB), :] = si
            return (jnp.broadcast_to(sr[SLAB - 1:SLAB], sr.shape), jnp.broadcast_to(si[SLAB - 1:SLAB], si.shape))

        cr, ci = lax.fori_loop(0, nsl, slab, (carry[0], carry[1]))
        carry[0] = cr
        carry[1] = ci
        y_ref[...] = (jnp.dot(sr_ref[...].astype(BF16), cr_ref[0], preferred_element_type=F32)
                      - jnp.dot(si_ref[...].astype(BF16), ci_ref[0], preferred_element_type=F32))

    us = pl.BlockSpec((tbk, LANE), lambda j, t: (t, j))
    bs = pl.BlockSpec((1, LANE, SW), lambda j, t: (j, 0, 0))
    cs = pl.BlockSpec((1, SW, LANE), lambda j, t: (j, 0, 0))
    ts = pl.BlockSpec((8, SLAB, SW), lambda j, t: (0, 0, j))
    ss = pl.BlockSpec((tbk, SW), lambda j, t: (t, j))
    return pl.pallas_call(
        body, name=name, grid=(NJ, nt), in_specs=[us, bs, bs, cs, cs, ts], out_specs=[us, ss, ss],
        out_shape=[jax.ShapeDtypeStruct((s, D_MODEL), F32), jax.ShapeDtypeStruct((s, NST), F32),
                   jax.ShapeDtypeStruct((s, NST), F32)],
        scratch_shapes=[pltpu.VMEM((tbk, SW), F32), pltpu.VMEM((tbk, SW), F32), pltpu.VMEM((2, SLAB, SW), F32)],
        compiler_params=_cparams("parallel", "arbitrary"))(u, bbd_r, bbd_i, cbd_r, cbd_i, tab)


def s5_bwd(u, dy, dskip, st_r, st_i, bbd_r, bbd_i, cbd_r, cbd_i, tabrev, *, name):
    s = u.shape[0]
    tbk = _rows(s)
    nt = s // tbk
    nsl = tbk // SLAB
    rbk = tbk // SLAB

    def body(u_ref, dy_ref, d_ref, sr_ref, si_ref, pr_ref, pi_ref, br_ref, bi_ref, cr_ref, ci_ref, tab_ref,
             du_ref, dbr_ref, dbi_ref, dcr_ref, dci_ref, da_ref, lam_r, lam_i, carry):
        t = pl.program_id(1)

        @pl.when(t == 0)
        def _():
            carry[...] = jnp.zeros_like(carry)
            dbr_ref[...] = jnp.zeros_like(dbr_ref)
            dbi_ref[...] = jnp.zeros_like(dbi_ref)
            dcr_ref[...] = jnp.zeros_like(dcr_ref)
            dci_ref[...] = jnp.zeros_like(dci_ref)
            da_ref[...] = jnp.zeros_like(da_ref)

        dyv = dy_ref[...]
        dyh = dyv.astype(BF16)
        lam_r[...] = lax.dot_general(dyh, cr_ref[0], NT, preferred_element_type=F32)
        lam_i[...] = -lax.dot_general(dyh, ci_ref[0], NT, preferred_element_type=F32)
        tabs = [tab_ref[n] for n in range(8)]

        def slab(n, c):
            r0 = pl.multiple_of((nsl - 1 - n) * SLAB, SLAB)
            lr, li = _slab_scan(lam_r[pl.ds(r0, SLAB), :], lam_i[pl.ds(r0, SLAB), :], tabs, c[0], c[1], True)
            lam_r[pl.ds(r0, SLAB), :] = lr
            lam_i[pl.ds(r0, SLAB), :] = li
            return (jnp.broadcast_to(lr[0:1], lr.shape), jnp.broadcast_to(li[0:1], li.shape))

        cr, ci = lax.fori_loop(0, nsl, slab, (carry[0], carry[1]))
        carry[0] = cr
        carry[1] = ci
        lr, li = lam_r[...], lam_i[...]
        lrh, lih = lr.astype(BF16), li.astype(BF16)
        du = (dyv * d_ref[...] + lax.dot_general(lrh, br_ref[0], NT, preferred_element_type=F32)
              + lax.dot_general(lih, bi_ref[0], NT, preferred_element_type=F32))
        du_ref[...] = du.astype(BF16)
        ub = u_ref[...].astype(BF16)
        dbr_ref[0] += lax.dot_general(ub, lrh, TN, preferred_element_type=F32)
        dbi_ref[0] += lax.dot_general(ub, lih, TN, preferred_element_type=F32)
        srv, siv = sr_ref[...], si_ref[...]
        dcr_ref[0] += lax.dot_general(srv.astype(BF16), dyh, TN, preferred_element_type=F32)
        dci_ref[0] -= lax.dot_general(siv.astype(BF16), dyh, TN, preferred_element_type=F32)
        first = lax.broadcasted_iota(jnp.int32, srv.shape, 0) == 0
        pm = (t < nt - 1).astype(F32)
        spr = jnp.where(first, pr_ref[SLAB - 1:SLAB, :] * pm, pltpu.roll(srv, 1, 0))
        spi = jnp.where(first, pi_ref[SLAB - 1:SLAB, :] * pm, pltpu.roll(siv, 1, 0))
        da_ref[0:1, :] += jnp.sum(lr * spr + li * spi, axis=0, keepdims=True)
        da_ref[1:2, :] += jnp.sum(li * spr - lr * spi, axis=0, keepdims=True)

    rv = lambda t: nt - 1 - t
    us = pl.BlockSpec((tbk, LANE), lambda j, t: (rv(t), j))
    ds = pl.BlockSpec((1, LANE), lambda j, t: (0, j))
    ss = pl.BlockSpec((tbk, SW), lambda j, t: (rv(t), j))
    ps = pl.BlockSpec((SLAB, SW), lambda j, t: (jnp.maximum(rv(t) * rbk - 1, 0), j))
    bs = pl.BlockSpec((1, LANE, SW), lambda j, t: (j, 0, 0))
    cs = pl.BlockSpec((1, SW, LANE), lambda j, t: (j, 0, 0))
    ts = pl.BlockSpec((8, SLAB, SW), lambda j, t: (0, 0, j))
    das = pl.BlockSpec((2, SW), lambda j, t: (0, j))
    return pl.pallas_call(
        body, name=name, grid=(NJ, nt),
        in_specs=[us, us, ds, ss, ss, ps, ps, bs, bs, cs, cs, ts],
        out_specs=[us, bs, bs, cs, cs, das],
        out_shape=[jax.ShapeDtypeStruct((s, D_MODEL), BF16),
                   jax.ShapeDtypeStruct((NJ, LANE, SW), F32), jax.ShapeDtypeStruct((NJ, LANE, SW), F32),
                   jax.ShapeDtypeStruct((NJ, SW, LANE), F32), jax.ShapeDtypeStruct((NJ, SW, LANE), F32),
                   jax.ShapeDtypeStruct((2, NST), F32)],
        scratch_shapes=[pltpu.VMEM((tbk, SW), F32), pltpu.VMEM((tbk, SW), F32), pltpu.VMEM((2, SLAB, SW), F32)],
        compiler_params=_cparams("parallel", "arbitrary"))(
            u, dy, dskip, st_r, st_i, st_r, st_i, bbd_r, bbd_i, cbd_r, cbd_i, tabrev)


GELU_C = math.sqrt(2.0 / math.pi)
GELU_A = 0.044715


def s5post_fwd(y, u, dskip, *, name):
    s = y.shape[0]
    tm = _rows(s)

    def body(y_ref, u_ref, d_ref, o_ref):
        z = y_ref[...] + d_ref[...] * u_ref[...]
        o_ref[...] = (0.5 * z * (1.0 + jnp.tanh(GELU_C * (z + GELU_A * z * z * z)))).astype(BF16)

    row = pl.BlockSpec((tm, D_MODEL), lambda i: (i, 0))
    vec = pl.BlockSpec((1, D_MODEL), lambda i: (0, 0))
    return pl.pallas_call(body, name=name, grid=(s // tm,), in_specs=[row, row, vec], out_specs=row,
                          out_shape=jax.ShapeDtypeStruct((s, D_MODEL), BF16),
                          compiler_params=_cparams("parallel"))(y, u, dskip)


def s5post_bwd(y, u, dskip, dg, *, name):
    s = y.shape[0]
    tm = _rows(s)

    def body(y_ref, u_ref, d_ref, dg_ref, dz_ref, dd_ref):
        @pl.when(pl.program_id(0) == 0)
        def _():
            dd_ref[...] = jnp.zeros_like(dd_ref)

        uv = u_ref[...]
        z = y_ref[...] + d_ref[...] * uv
        th = jnp.tanh(GELU_C * (z + GELU_A * z * z * z))
        dgelu = 0.5 * (1.0 + th) + 0.5 * z * (1.0 - th * th) * (GELU_C * (1.0 + 3.0 * GELU_A * z * z))
        dz = dg_ref[...] * dgelu
        dz_ref[...] = dz
        dd_ref[...] += jnp.sum(dz * uv, axis=0, keepdims=True)

    row = pl.BlockSpec((tm, D_MODEL), lambda i: (i, 0))
    vec = pl.BlockSpec((1, D_MODEL), lambda i: (0, 0))
    return pl.pallas_call(body, name=name, grid=(s // tm,), in_specs=[row, row, vec, row], out_specs=[row, vec],
                          out_shape=[jax.ShapeDtypeStruct((s, D_MODEL), F32), jax.ShapeDtypeStruct((1, D_MODEL), F32)],
                          compiler_params=_cparams("arbitrary"))(y, u, dskip, dg)


def glu_fwd(glu, x, *, name):
    s = x.shape[0]
    tm = _rows(s)

    def body(a_ref, b_ref, x_ref, o_ref):
        o_ref[...] = x_ref[...] + a_ref[...] * jax.nn.sigmoid(b_ref[...])

    row = pl.BlockSpec((tm, D_MODEL), lambda i: (i, 0))
    return pl.pallas_call(body, name=name, grid=(s // tm,),
                          in_specs=[row, pl.BlockSpec((tm, D_MODEL), lambda i: (i, 1)), row], out_specs=row,
                          out_shape=jax.ShapeDtypeStruct((s, D_MODEL), F32),
                          compiler_params=_cparams("parallel"))(glu, glu, x)


def glu_bwd(glu, dx, *, name):
    s = dx.shape[0]
    tm = _rows(s)

    def body(a_ref, b_ref, dx_ref, o_ref):
        sg = jax.nn.sigmoid(b_ref[...])
        dxv = dx_ref[...]
        o_ref[:, :D_MODEL] = (dxv * sg).astype(BF16)
        o_ref[:, D_MODEL:] = (dxv * a_ref[...] * sg * (1.0 - sg)).astype(BF16)

    row = pl.BlockSpec((tm, D_MODEL), lambda i: (i, 0))
    return pl.pallas_call(body, name=name, grid=(s // tm,),
                          in_specs=[row, pl.BlockSpec((tm, D_MODEL), lambda i: (i, 1)), row],
                          out_specs=pl.BlockSpec((tm, 2 * D_MODEL), lambda i: (i, 0)),
                          out_shape=jax.ShapeDtypeStruct((s, 2 * D_MODEL), BF16),
                          compiler_params=_cparams("parallel"))(glu, glu, dx)


def loss_head(y, target, *, name):
    s = y.shape[0]
    tm = _rows(s)

    def body(y_ref, t_ref, dy_ref, dyh_ref, l_ref):
        @pl.when(pl.program_id(0) == 0)
        def _():
            l_ref[...] = jnp.zeros_like(l_ref)

        e = y_ref[...] - t_ref[...]
        dy_ref[...] = e * (1.0 / D_MODEL)
        dyh_ref[...] = (e * (1.0 / D_MODEL)).astype(BF16)
        e2 = jnp.sum((e * e).reshape(tm // 8, 8, D_MODEL), axis=0)
        acc = e2[:, 0:LANE]
        for kk in range(1, D_MODEL // LANE):
            acc = acc + e2[:, kk * LANE:(kk + 1) * LANE]
        l_ref[...] += acc

    row = pl.BlockSpec((tm, D_MODEL), lambda i: (i, 0))
    return pl.pallas_call(body, name=name, grid=(s // tm,), in_specs=[row, row],
                          out_specs=[row, row, pl.BlockSpec((8, LANE), lambda i: (0, 0))],
                          out_shape=[jax.ShapeDtypeStruct((s, D_MODEL), F32), jax.ShapeDtypeStruct((s, D_MODEL), BF16),
                                     jax.ShapeDtypeStruct((8, LANE), F32)],
                          compiler_params=_cparams("arbitrary"))(y, target)


PACKW = 1024
NCHIP = 4


def _mesh_pos():
    return lax.axis_index("x"), lax.axis_index("y"), lax.axis_index("c")


def _chip_exchange(bufs, scatter, name):
    n = len(bufs)
    shapes = [b.shape[1:] if scatter else b.shape for b in bufs]

    def body(*refs):
        ins, outs = refs[:n], refs[n:2 * n]
        send_sems, recv_sems, local_sems = refs[2 * n:]
        x, y, c = _mesh_pos()
        me = 2 * x + y
        peers = [(1 - x, y), (x, 1 - y), (1 - x, 1 - y)]

        def copy(a, j, px, py, dst_slot):
            src = ins[a].at[2 * px + py] if scatter else ins[a]
            return pltpu.make_async_remote_copy(src_ref=src, dst_ref=outs[a].at[dst_slot],
                                                send_sem=send_sems.at[3 * a + j], recv_sem=recv_sems.at[3 * a + j],
                                                device_id=(px, py, c), device_id_type=MESH)

        mine = [pltpu.make_async_copy(ins[a].at[me] if scatter else ins[a], outs[a].at[me], local_sems.at[a])
                for a in range(n)]
        sends = [copy(a, j, px, py, me) for a in range(n) for j, (px, py) in enumerate(peers)]
        for cp in mine + sends:
            cp.start()
        for a in range(n):
            for j, (px, py) in enumerate(peers):
                copy(a, j, px, py, 2 * px + py).wait_recv()
        for cp in sends:
            cp.wait_send()
        for cp in mine:
            cp.wait()

    return pl.pallas_call(
        body, name=name, in_specs=[ANY] * n, out_specs=[ANY] * n,
        out_shape=[jax.ShapeDtypeStruct((NCHIP,) + tuple(shp), b.dtype) for shp, b in zip(shapes, bufs)],
        scratch_shapes=[pltpu.SemaphoreType.DMA((3 * n,)), pltpu.SemaphoreType.DMA((3 * n,)),
                        pltpu.SemaphoreType.DMA((n,))],
    )(*bufs)


HBM_SPEC = pl.BlockSpec(memory_space=pltpu.HBM)
SEM_SPEC = pl.BlockSpec(memory_space=pltpu.SEMAPHORE)
DATAFLOW = pltpu.SideEffectType.DATAFLOW_SIDE_EFFECTING


def _exchange_copy(ins, lands, send_sems, recv_sems, scatter, a, j, px, py, c, dst_slot):
    src = ins[a].at[2 * px + py] if scatter else ins[a]
    return pltpu.make_async_remote_copy(src_ref=src, dst_ref=lands[a].at[dst_slot],
                                        send_sem=send_sems.at[3 * a + j], recv_sem=recv_sems.at[3 * a + j],
                                        device_id=(px, py, c), device_id_type=MESH)


def exchange_start(bufs, lands, scatter, name):
    n = len(bufs)

    def body(*refs):
        ins, lnd, send_sems, recv_sems, token = refs[:n], refs[n:2 * n], refs[2 * n], refs[2 * n + 1], refs[-1]
        x, y, c = _mesh_pos()
        me = 2 * x + y
        for a in range(n):
            for j, (px, py) in enumerate([(1 - x, y), (x, 1 - y), (1 - x, 1 - y)]):
                _exchange_copy(ins, lnd, send_sems, recv_sems, scatter, a, j, px, py, c, me).start()
        token[...] = jnp.zeros_like(token)

    thru = [pltpu.HBM(b.shape, b.dtype) for b in list(bufs) + list(lands)]
    out = pl.pallas_call(
        body, name=name, in_specs=[HBM_SPEC] * (2 * n),
        out_specs=[SEM_SPEC, SEM_SPEC] + [HBM_SPEC] * (2 * n) + [pl.BlockSpec(memory_space=pltpu.VMEM)],
        out_shape=[pltpu.SemaphoreType.DMA((3 * n,)), pltpu.SemaphoreType.DMA((3 * n,))] + thru
        + [jax.ShapeDtypeStruct((SLAB, LANE), F32)],
        input_output_aliases={k: 2 + k for k in range(2 * n)},
        compiler_params=pltpu.CompilerParams(has_side_effects=DATAFLOW),
    )(*[pltpu.with_memory_space_constraint(b, pltpu.HBM) for b in list(bufs) + list(lands)])
    return out[0], out[1], out[2:2 + n], out[2 + n:2 + 2 * n], out[-1][0, 0]


def exchange_wait(send_sems, recv_sems, bufs, lands, after, scatter, name):
    n = len(bufs)

    def body(*refs):
        ins, lnd, ssem, rsem = refs[:n], refs[n:2 * n], refs[2 * n], refs[2 * n + 1]
        x, y, c = _mesh_pos()
        for a in range(n):
            for j, (px, py) in enumerate([(1 - x, y), (x, 1 - y), (1 - x, 1 - y)]):
                cp = _exchange_copy(ins, lnd, ssem, rsem, scatter, a, j, px, py, c, 2 * px + py)
                cp.wait_send()
                cp.wait_recv()

    thru = [pltpu.HBM(b.shape, b.dtype) for b in list(bufs) + list(lands)]
    out = pl.pallas_call(
        body, name=name, in_specs=[HBM_SPEC] * (2 * n) + [SEM_SPEC, SEM_SPEC, ANY],
        out_specs=[HBM_SPEC] * (2 * n), out_shape=thru,
        input_output_aliases={k: k for k in range(2 * n)},
        compiler_params=pltpu.CompilerParams(has_side_effects=DATAFLOW),
    )(*bufs, *lands, send_sems, recv_sems, after)
    return out[n:]


def sibling_swap(bufs, name):
    n = len(bufs)

    def body(*refs):
        ins, outs, send_sems, recv_sems = refs[:n], refs[n:2 * n], refs[2 * n], refs[2 * n + 1]
        x, y, c = _mesh_pos()
        cps = [pltpu.make_async_remote_copy(src_ref=ins[k], dst_ref=outs[k], send_sem=send_sems.at[k],
                                            recv_sem=recv_sems.at[k], device_id=(x, y, 1 - c), device_id_type=MESH)
               for k in range(n)]
        for cp in cps:
            cp.start()
        for cp in cps:
            cp.wait()

    return pl.pallas_call(
        body, name=name, in_specs=[ANY] * n, out_specs=[ANY] * n,
        out_shape=[jax.ShapeDtypeStruct(b.shape, b.dtype) for b in bufs],
        scratch_shapes=[pltpu.SemaphoreType.DMA((n,)), pltpu.SemaphoreType.DMA((n,))],
    )(*bufs)


EW_VMEM_BUDGET = 20 * 1024 * 1024


def _ew_rows(rows, w, bytes_per_elem):
    wpad = -(-w // LANE) * LANE
    for t in (1024, 512, 256, 128, 64, 32, 16, 8):
        if rows % t == 0 and 2 * t * wpad * bytes_per_elem <= EW_VMEM_BUDGET:
            return t
    return rows


def sum_slots(buf, *, name):
    _, rows, w = buf.shape
    tm = _ew_rows(rows, w, NCHIP * buf.dtype.itemsize + 4)

    def body(b_ref, o_ref):
        acc = b_ref[0].astype(F32)
        for kk in range(1, NCHIP):
            acc = acc + b_ref[kk].astype(F32)
        o_ref[...] = acc

    return pl.pallas_call(body, name=name, grid=(rows // tm,),
                          in_specs=[pl.BlockSpec((NCHIP, tm, w), lambda i: (0, i, 0))],
                          out_specs=pl.BlockSpec((tm, w), lambda i: (i, 0)),
                          out_shape=jax.ShapeDtypeStruct((rows, w), F32),
                          compiler_params=_cparams("parallel"))(buf)


def adamw(p_mine, p_other, w, m, v, *, name):
    rows, wd = w.shape
    tm = _ew_rows(rows, wd, 9 * 4)
    c1 = 1.0 - ADAM_B1 ** ADAM_STEP
    c2 = 1.0 - ADAM_B2 ** ADAM_STEP

    def body(a_ref, b_ref, w_ref, m_ref, v_ref, g_ref, d_ref, nm_ref, nv_ref):
        g = a_ref[...] + b_ref[...]
        nm = ADAM_B1 * m_ref[...] + (1.0 - ADAM_B1) * g
        nv = ADAM_B2 * v_ref[...] + (1.0 - ADAM_B2) * (g * g)
        g_ref[...] = g
        nm_ref[...] = nm
        nv_ref[...] = nv
        d_ref[...] = -ADAM_LR * ((nm / c1) / (jnp.sqrt(nv / c2) + ADAM_EPS) + ADAM_WD * w_ref[...])

    row = pl.BlockSpec((tm, wd), lambda i: (i, 0))
    return pl.pallas_call(body, name=name, grid=(rows // tm,), in_specs=[row] * 5, out_specs=[row] * 4,
                          out_shape=[jax.ShapeDtypeStruct((rows, wd), F32)] * 4,
                          compiler_params=_cparams("parallel"))(p_mine, p_other, w, m, v)


def _rows2d(a, lead=0):
    tail = a.shape[lead:]
    n = int(np.prod(tail))
    if tail[-1] < LANE // 2 and n % (8 * LANE) == 0:
        return a.reshape(a.shape[:lead] + (n // (8 * LANE), 8 * LANE))
    return a.reshape(a.shape[:lead] + (-1, tail[-1]))


BIG = [("mix_w_in", 2), ("w_uq", 2), ("w_ukv", 2), ("mix_w_out", 1), ("ssm_w_in", 1), ("w_glu", 2),
       ("ffn_w_up", 2), ("ffn_w_down", 1)]
SMALL = [("sconv_w", 2), ("ssm_norm", 1), ("d_skip", 1), ("ffn_conv_w", 2)]
REPL = ["attn_norm", "cq_norm", "ckv_norm", "q_gain", "k_gain", "lambda_re", "lambda_im", "log_step",
        "b_re", "b_im", "c_re", "c_im", "ffn_norm"]
ORDER = ["attn_norm", "mix_w_in", "cq_norm", "ckv_norm", "w_uq", "w_ukv", "q_gain", "k_gain", "sconv_w", "mix_w_out",
         "ssm_norm", "ssm_w_in", "lambda_re", "lambda_im", "log_step", "b_re", "b_im", "c_re", "c_im", "d_skip",
         "w_glu", "ffn_norm", "ffn_w_up", "ffn_conv_w", "ffn_w_down"]


def _join(g, axis):
    return jnp.concatenate([g[k] for k in range(NCHIP)], axis=axis)


def _split(full, axis):
    return jnp.stack(jnp.split(full, NCHIP, axis=axis))


def _discretize(lr, li, ls, b_re, b_im):
    dt = jnp.exp(ls)[:, None]
    mag = jnp.exp(lr * dt)
    ar, ai = mag * jnp.cos(li * dt), mag * jnp.sin(li * dt)
    nr, ni = ar - 1.0, ai
    den = lr * lr + li * li
    zr, zi = (nr * lr + ni * li) / den, (ni * lr - nr * li) / den
    bbar_r = zr[..., None] * b_re - zi[..., None] * b_im
    bbar_i = zr[..., None] * b_im + zi[..., None] * b_re
    return ar, ai, bbar_r, bbar_i


def _b_blockdiag(bbar):
    gl = G // NJ
    bb = bbar.reshape(NJ, gl, P, GC).transpose(0, 1, 3, 2)
    return jnp.einsum("jgcp,gh->jgchp", bb, jnp.eye(gl, dtype=bbar.dtype)).reshape(NJ, gl * GC, gl * P)


def _b_blockdiag_t(dbd):
    gl = G // NJ
    d = jnp.einsum("jgchp,gh->jgcp", dbd.reshape(NJ, gl, GC, gl, P), jnp.eye(gl, dtype=dbd.dtype))
    return d.transpose(0, 1, 3, 2).reshape(G, P, GC)


def _c_blockdiag(cmat):
    gl = G // NJ
    cc = cmat.reshape(NJ, gl, GC, P).transpose(0, 1, 3, 2)
    return jnp.einsum("jgpc,gh->jgphc", cc, jnp.eye(gl, dtype=cmat.dtype)).reshape(NJ, gl * P, gl * GC)


def _c_blockdiag_t(dbd):
    gl = G // NJ
    d = jnp.einsum("jgphc,gh->jgpc", dbd.reshape(NJ, gl, P, gl, GC), jnp.eye(gl, dtype=dbd.dtype))
    return d.transpose(0, 1, 3, 2).reshape(G, GC, P)


def _pad_heads_cols(w, width):
    r = w.shape[0]
    return jnp.pad(w.reshape(r, HEADS, width), ((0, 0), (0, 0), (0, HP - width))).reshape(r, HEADS * HP)


def _unpad_heads_cols(w, width):
    r = w.shape[0]
    return w.reshape(r, HEADS, HP)[:, :, :width].reshape(r, HEADS * width)


W_IN_SPLIT = (QR + KVR, QR + KVR + ROPE)


def _w_in_layout(w):
    a, b = W_IN_SPLIT
    kr = jnp.pad(w[:, a:b], ((0, 0), (NOPE, HP - QK)))
    return jnp.concatenate([w[:, :a], w[:, b:], kr], axis=1)


def _w_in_layout_t(dw):
    a, b = W_IN_SPLIT
    n = dw.shape[1] - HP
    return jnp.concatenate([dw[:, :a], dw[:, n + NOPE:n + QK], dw[:, a:n]], axis=1)


def _ffn_fwd(x, l, wt, name):
    h = rms_fwd(x, wt["ffn_norm"][l][None], name=f"{name}_norm")
    up = mm(h, wt["ffn_w_up"][l], name=f"{name}_up")
    act = ffnact_fwd(up, wt["ffn_conv_w"][l], name=f"{name}_act")
    out = mm(act, wt["ffn_w_down"][l], add=x, name=f"{name}_down")
    return out, (x, h, up, act)


def _ffn_bwd(dout, douth, saved, l, wt, name):
    x, h, up, act = saved
    g = {}
    dact = mm(douth, wt["ffn_w_down"][l], tb=True, name=f"{name}_ddown")
    g["ffn_w_down"] = mm(act, douth, ta=True, out_dtype=BF16, name=f"{name}_dwdown")
    dup, g["ffn_conv_w"] = ffnact_bwd(up, wt["ffn_conv_w"][l], dact, name=f"{name}_dact")
    g["ffn_w_up"] = mm(h, dup, ta=True, out_dtype=BF16, name=f"{name}_dwup")
    dh = mm(dup, wt["ffn_w_up"][l], tb=True, name=f"{name}_dup")
    dx, dxh, dg = rms_bwd(x, wt["ffn_norm"][l][None], dh, add=dout, twin=True, name=f"{name}_dnorm")
    g["ffn_norm"] = dg[0]
    return dx, dxh, g


def _even_fwd(x, i, wt, tabs, name):
    h = rms_fwd(x, wt["attn_norm"][i][None], name=f"{name}_norm")
    proj = mm(h, wt["w_in2"][i], name=f"{name}_in")
    cqn = rms_fwd(proj, wt["cq_norm"][i][None], col=0, name=f"{name}_cqnorm")
    ckvn = rms_fwd(proj, wt["ckv_norm"][i][None], col=1, name=f"{name}_ckvnorm")
    qraw = mm(cqn, wt["w_uq_p"][i], name=f"{name}_uq")
    kv = mm(ckvn, wt["w_ukv_p"][i], name=f"{name}_ukv")
    q, k, v = qkprep_fwd(qraw, kv, proj, wt["q_gain_p"][i], wt["k_gain_p"][i], tabs, name=f"{name}_qkprep")
    o, oh, lse, lset = attn_fwd(q, k, v, name=f"{name}_attn")
    conv = sconv_fwd(proj, wt["sconv_w"][i], name=f"{name}_sconv")
    t = mm(oh, wt["w_out_a"][i], add=x, name=f"{name}_outa")
    out = mm(conv, wt["w_out_c"][i], add=t, name=f"{name}_outc")
    return out, (x, h, proj, cqn, ckvn, qraw, kv, q, k, v, o, oh, lse, lset, conv)


def _even_bwd(dout, douth, saved, i, wt, tabs, name):
    x, h, proj, cqn, ckvn, qraw, kv, q, k, v, o, oh, lse, lset, conv = saved
    g = {}
    do = mm(douth, wt["w_out_a"][i], tb=True, name=f"{name}_douta")
    dconv = mm(douth, wt["w_out_c"][i], tb=True, name=f"{name}_doutc")
    g["w_out_a"] = mm(oh, douth, ta=True, out_dtype=BF16, name=f"{name}_dwouta")
    g["w_out_c"] = mm(conv, douth, ta=True, out_dtype=BF16, name=f"{name}_dwoutc")
    dgates, g["sconv_w"] = sconv_bwd(proj, wt["sconv_w"][i], dconv, name=f"{name}_dsconv")
    dq, doh, deltat = attn_bwd_dq(q, k, v, o, lse, do, name=f"{name}_dattn_q")
    dk, dv = attn_bwd_dkv(q, k, v, doh, lset, deltat, name=f"{name}_dattn_kv")
    dqraw, dkraw, dkrope, dqg, dkg = qkprep_bwd(qraw, kv, proj, wt["q_gain_p"][i], wt["k_gain_p"][i], tabs, dq, dk,
                                                name=f"{name}_dqkprep")
    g["q_gain"], g["k_gain"] = dqg[0, :QK], dkg[0, :QK]
    dcqn = mm(dqraw, wt["w_uq_p"][i], tb=True, name=f"{name}_duq")
    g["w_uq_p"] = mm(cqn, dqraw, ta=True, out_dtype=BF16, name=f"{name}_dwuq")
    dkv = jnp.concatenate([dkraw, dv], axis=1)
    dckvn = mm(dkv, wt["w_ukv_p"][i], tb=True, name=f"{name}_dukv")
    g["w_ukv_p"] = mm(ckvn, dkv, ta=True, out_dtype=BF16, name=f"{name}_dwukv")
    dcq, dgq = rms_bwd(proj, wt["cq_norm"][i][None], dcqn, col=0, out_dtype=BF16, name=f"{name}_dcqnorm")
    dckv, dgkv = rms_bwd(proj, wt["ckv_norm"][i][None], dckvn, col=1, out_dtype=BF16, name=f"{name}_dckvnorm")
    g["cq_norm"], g["ckv_norm"] = dgq[0], dgkv[0]
    dproj = jnp.concatenate([dcq, dckv, dgates, dkrope.astype(BF16)], axis=1)
    g["w_in2"] = mm(h, dproj, ta=True, out_dtype=BF16, name=f"{name}_dwin")
    dh = mm(dproj, wt["w_in2"][i], tb=True, name=f"{name}_din")
    dx, dxh, dg = rms_bwd(x, wt["attn_norm"][i][None], dh, add=dout, twin=True, name=f"{name}_dnorm")
    g["attn_norm"] = dg[0]
    return dx, dxh, g


def _odd_fwd(x, i, wt, name):
    h = rms_fwd(x, wt["ssm_norm"][i][None], name=f"{name}_norm")
    u = mm(h, wt["ssm_w_in"][i], name=f"{name}_in")
    y, st_r, st_i = s5_fwd(u, wt["bbd_r"][i], wt["bbd_i"][i], wt["cbd_r"][i], wt["cbd_i"][i], wt["tab_f"][i],
                           name=f"{name}_scan")
    gl = s5post_fwd(y, u, wt["d_skip"][i][None], name=f"{name}_gelu")
    glu = mm(gl, wt["w_glu"][i], name=f"{name}_glu")
    out = glu_fwd(glu, x, name=f"{name}_gate")
    return out, (x, h, u, y, st_r, st_i, gl, glu)


def _odd_bwd(dout, douth, saved, i, wt, name):
    x, h, u, y, st_r, st_i, gl, glu = saved
    g = {}
    dglu = glu_bwd(glu, dout, name=f"{name}_dgate")
    g["w_glu"] = mm(gl, dglu, ta=True, out_dtype=BF16, name=f"{name}_dwglu")
    dgl = mm(dglu, wt["w_glu"][i], tb=True, name=f"{name}_dglu")
    dz, dd = s5post_bwd(y, u, wt["d_skip"][i][None], dgl, name=f"{name}_dgelu")
    g["d_skip"] = dd[0]
    du, g["bbd_r"], g["bbd_i"], g["cbd_r"], g["cbd_i"], g["a"] = s5_bwd(
        u, dz, wt["d_skip"][i][None], st_r, st_i, wt["bbd_r"][i], wt["bbd_i"][i], wt["cbd_r"][i], wt["cbd_i"][i],
        wt["tab_r"][i], name=f"{name}_dscan")
    g["ssm_w_in"] = mm(h, du, ta=True, out_dtype=BF16, name=f"{name}_dwin")
    dh = mm(du, wt["ssm_w_in"][i], tb=True, name=f"{name}_din")
    dx, dxh, dg = rms_bwd(x, wt["ssm_norm"][i][None], dh, add=dout, twin=True, name=f"{name}_dnorm")
    g["ssm_norm"] = dg[0]
    return dx, dxh, g


MATMUL_WEIGHTS = {"even": ("mix_w_in", "w_uq", "w_ukv", "mix_w_out"), "odd": ("ssm_w_in", "w_glu"),
                  "ffn": ("ffn_w_up", "ffn_w_down")}


def _even_layouts(fw, wt, i):
    wt["w_in2"][i] = _w_in_layout(fw["mix_w_in"])
    wt["w_uq_p"][i] = _pad_heads_cols(fw["w_uq"], QK)
    ukv = fw["w_ukv"].reshape(KVR, HEADS, NOPE + VD)
    wt["w_ukv_p"][i] = jnp.concatenate(
        [_pad_heads_cols(ukv[:, :, :NOPE].reshape(KVR, HEADS * NOPE), NOPE),
         _pad_heads_cols(ukv[:, :, NOPE:].reshape(KVR, HEADS * VD), VD)], axis=1)
    wt["w_out_a"][i] = _pad_heads_cols(fw["mix_w_out"][:HEADS * VD].T, VD).T
    wt["w_out_c"][i] = fw["mix_w_out"][HEADS * VD:]


def _even_layouts_t(g):
    dk_, dv_ = g["w_ukv_p"][:, :HEADS * HP], g["w_ukv_p"][:, HEADS * HP:]
    return {"mix_w_in": _w_in_layout_t(g["w_in2"]), "w_uq": _unpad_heads_cols(g["w_uq_p"], QK),
            "w_ukv": jnp.concatenate([dk_.reshape(KVR, HEADS, HP)[:, :, :NOPE], dv_.reshape(KVR, HEADS, HP)[:, :, :VD]],
                                     axis=2).reshape(KVR, HEADS * (NOPE + VD)),
            "mix_w_out": jnp.concatenate([_unpad_heads_cols(g["w_out_a"].T, VD).T, g["w_out_c"]], axis=0)}


def _local_step(x, target, full, getw, putg, tok):
    s = x.shape[0]
    n_even = (DEPTH + 1) // 2
    n_odd = DEPTH // 2
    tabs = _rope_tables(s)
    wt = dict(full)
    for key in ("w_in2", "w_uq_p", "w_ukv_p", "w_out_a", "w_out_c") + sum(MATMUL_WEIGHTS.values(), ()):
        wt[key] = {}
    wt["attn_norm"] = full["attn_norm"] + tok
    wt["q_gain_p"] = jnp.pad(full["q_gain"], ((0, 0), (0, HP - QK)))[:, None, :]
    wt["k_gain_p"] = jnp.pad(full["k_gain"], ((0, 0), (0, HP - QK)))[:, None, :]

    disc_vjp = []
    for key in ("bbd_r", "bbd_i", "cbd_r", "cbd_i", "tab_f", "tab_r"):
        wt[key] = []
    for i in range(n_odd):
        (ar, ai, bbr, bbi), vjp = jax.vjp(_discretize, full["lambda_re"][i], full["lambda_im"][i], full["log_step"][i],
                                          full["b_re"][i], full["b_im"][i])
        disc_vjp.append(vjp)
        tf, tr = _scan_tables(ar.reshape(-1), ai.reshape(-1))
        wt["tab_f"].append(tf)
        wt["tab_r"].append(tr)
        wt["bbd_r"].append(_b_blockdiag(bbr).astype(BF16))
        wt["bbd_i"].append(_b_blockdiag(bbi).astype(BF16))
        wt["cbd_r"].append(_c_blockdiag(full["c_re"][i]).astype(BF16))
        wt["cbd_i"].append(_c_blockdiag(full["c_im"][i]).astype(BF16))

    saved = []
    for layer in range(DEPTH):
        i = layer // 2
        if layer % 2 == 0:
            _even_layouts(getw("even", i, x), wt, i)
            x, sm = _even_fwd(x, i, wt, tabs, f"l{layer}_mla")
        else:
            for n, a in getw("odd", i, x).items():
                wt[n][i] = a
            x, sm = _odd_fwd(x, i, wt, f"l{layer}_s5")
        for n, a in getw("ffn", layer, x).items():
            wt[n][layer] = a
        x, sf = _ffn_fwd(x, layer, wt, f"l{layer}_ffn")
        saved.append((sm, sf))
    dx, dxh, lslab = loss_head(x, target, name="loss_head")

    own = [n for n in ORDER if n not in sum(MATMUL_WEIGHTS.values(), ())]
    grads = {n: [None] * (DEPTH if n.startswith("ffn") else n_even) for n in own}
    tok = 0.0
    for layer in reversed(range(DEPTH)):
        i = layer // 2
        sm, sf = saved[layer]
        wt["ffn_conv_w"] = full["ffn_conv_w"] + tok
        dx, dxh, g = _ffn_bwd(dx, dxh, sf, layer, wt, f"l{layer}_ffn")
        tok = putg("ffn", layer, {n: g[n] for n in MATMUL_WEIGHTS["ffn"]})
        for n in ("ffn_norm", "ffn_conv_w"):
            grads[n][layer] = g[n]
        if layer % 2 == 0:
            wt["sconv_w"] = full["sconv_w"] + tok
            dx, dxh, g = _even_bwd(dx, dxh, sm, i, wt, tabs, f"l{layer}_mla")
            tok = putg("even", i, _even_layouts_t(g))
            for n in ("attn_norm", "cq_norm", "ckv_norm", "q_gain", "k_gain", "sconv_w"):
                grads[n][i] = g[n]
        else:
            wt["d_skip"] = full["d_skip"] + tok
            dx, dxh, g = _odd_bwd(dx, dxh, sm, i, wt, f"l{layer}_s5")
            tok = putg("odd", i, {n: g[n] for n in MATMUL_WEIGHTS["odd"]})
            dlr, dli, dls, dbr, dbi = disc_vjp[i]((g["a"][0].reshape(G, P), g["a"][1].reshape(G, P),
                                                    _b_blockdiag_t(g["bbd_r"]), _b_blockdiag_t(g["bbd_i"])))
            grads["lambda_re"][i], grads["lambda_im"][i], grads["log_step"][i] = dlr, dli, dls
            grads["b_re"][i], grads["b_im"][i] = dbr, dbi
            grads["c_re"][i], grads["c_im"][i] = _c_blockdiag_t(g["cbd_r"]), _c_blockdiag_t(g["cbd_i"])
            for n in ("ssm_norm", "d_skip"):
                grads[n][i] = g[n]
    grads = {n: jnp.stack(v) for n, v in grads.items()}
    return jnp.sum(lslab), dx, grads


def kernel(x, attn_norm, mix_w_in, cq_norm, ckv_norm, w_uq, w_ukv, q_gain, k_gain, sconv_w, mix_w_out, ssm_norm, ssm_w_in, lambda_re, lambda_im, log_step, b_re, b_im, c_re, c_im, d_skip, w_glu, ffn_norm, ffn_w_up, ffn_conv_w, ffn_w_down, loss_target, m_attn_norm, m_mix_w_in, m_cq_norm, m_ckv_norm, m_w_uq, m_w_ukv, m_q_gain, m_k_gain, m_sconv_w, m_mix_w_out, m_ssm_norm, m_ssm_w_in, m_lambda_re, m_lambda_im, m_log_step, m_b_re, m_b_im, m_c_re, m_c_im, m_d_skip, m_w_glu, m_ffn_norm, m_ffn_w_up, m_ffn_conv_w, m_ffn_w_down, v_attn_norm, v_mix_w_in, v_cq_norm, v_ckv_norm, v_w_uq, v_w_ukv, v_q_gain, v_k_gain, v_sconv_w, v_mix_w_out, v_ssm_norm, v_ssm_w_in, v_lambda_re, v_lambda_im, v_log_step, v_b_re, v_b_im, v_c_re, v_c_im, v_d_skip, v_w_glu, v_ffn_norm, v_ffn_w_up, v_ffn_conv_w, v_ffn_w_down):
    args = dict(locals())
    w = {n: args[n] for n in ORDER}
    m = {n: args["m_" + n] for n in ORDER}
    v = {n: args["v_" + n] for n in ORDER}
    me = 2 * lax.axis_index("x") + lax.axis_index("y")

    axis = dict(BIG)

    def landing(own):
        return lax.dynamic_update_index_in_dim(lax.empty((NCHIP,) + own.shape, own.dtype), own, me, 0)

    gs = _chip_exchange([w[n] for n, _ in SMALL], False, "gather_w_f32")
    full = {n: w[n] for n in REPL}
    for (n, ax), g in zip(SMALL, gs):
        full[n] = _join(g, ax)
    parts = [(("even", "odd")[layer % 2], layer // 2) for layer in range(DEPTH)]
    parts = [p for layer, mixer in enumerate(parts) for p in (mixer, ("ffn", layer))]
    gathers, tok = {}, 0.0
    after_small = 0.0 * gs[0].reshape(-1)[0]
    for kind, idx in parts:
        shards = [(w[n][idx] + after_small).astype(BF16) for n in MATMUL_WEIGHTS[kind]]
        gathers[kind, idx] = exchange_start(shards, [landing(sh) for sh in shards], False, f"gather_start_{kind}{idx}")
        tok = tok + gathers[kind, idx][4]

    def getw(kind, idx, after):
        ssem, rsem, bufs, lands, _ = gathers[kind, idx]
        got = exchange_wait(ssem, rsem, bufs, lands, after, False, f"gather_wait_{kind}{idx}")
        return {n: _join(g, axis[n] - 1) for n, g in zip(MATMUL_WEIGHTS[kind], got)}

    scatters = []

    def putg(kind, idx, g):
        blocks = [_split(g[n], axis[n] - 1) for n in MATMUL_WEIGHTS[kind]]
        lands = [landing(lax.dynamic_index_in_dim(b, me, 0, keepdims=False)) for b in blocks]
        scatters.append((kind, idx, exchange_start(blocks, lands, True, f"scatter_start_{kind}{idx}")))
        return scatters[-1][2][4]

    sq, dx, grads = _local_step(x[0], loss_target[0], full, getw, putg, tok)
    loss = lax.psum(0.5 * sq / D_MODEL, ("x", "y", "c"))

    rep_names = REPL + [n for n, _ in SMALL]
    names = [n for n, _ in BIG] + rep_names
    small = [_rows2d(grads[n]) for n in rep_names]
    ssem_s, rsem_s, bufs_s, lands_s, _ = exchange_start(small, [landing(g) for g in small], False, "gather_start_g_f32")
    summed = {}
    for kind, idx, (ssem, rsem, bufs, lands, _) in scatters:
        got = exchange_wait(ssem, rsem, bufs, lands, dx, True, f"scatter_wait_{kind}{idx}")
        for n, sl in zip(MATMUL_WEIGHTS[kind], got):
            summed[n, idx] = sum_slots(_rows2d(sl, 1), name=f"sum_{n}{idx}")
    mine = [jnp.concatenate([summed[n, idx] for idx in range(w[n].shape[0])], axis=0) for n, _ in BIG]
    slots = exchange_wait(ssem_s, rsem_s, bufs_s, lands_s, mine[-1], False, "gather_wait_g_f32")
    mine += [sum_slots(_rows2d(sl, 1), name=f"sum_{n}") for n, sl in zip(rep_names, slots)]
    other = sibling_swap(mine, "swap_g")

    def local(n, p):
        ax = dict(SMALL).get(n)
        if ax is None:
            return p
        part = lax.dynamic_index_in_dim(_split(p.reshape(grads[n].shape), ax), me, 0, keepdims=False)
        return _rows2d(part)

    outs = {}
    for n, p, q in zip(names, mine, other):
        res = adamw(local(n, p), local(n, q), _rows2d(w[n]), _rows2d(m[n]), _rows2d(v[n]), name=f"adamw_{n}")
        outs[n] = [r.reshape(w[n].shape) for r in res]
    return (loss, dx[None], *[outs[n][0] for n in ORDER], *[outs[n][1] for n in ORDER],
            *[outs[n][2] for n in ORDER], *[outs[n][3] for n in ORDER])
```

```python
import functools
import math

import numpy as np
import jax
import jax.numpy as jnp
from jax import lax
from jax.experimental import pallas as pl
from jax.experimental.pallas import tpu as pltpu

F32, BF16 = jnp.float32, jnp.bfloat16

D_MODEL = 1024
DEPTH = 4
HEADS = 8
NOPE, ROPE, QK, VD = 64, 32, 96, 64
HP = 128
QR, KVR = 256, 256
CONVC = 512
FFN_H = 2816
G, P, GC = 64, 64, 16
NST = G * P
SLAB = 8
LANE = 128
EPS = 1e-6
ROPE_THETA = 10000.0
ADAM_LR, ADAM_B1, ADAM_B2, ADAM_EPS, ADAM_WD, ADAM_STEP = 0.001, 0.9, 0.999, 1e-08, 0.01, 10
VMEM_LIMIT = 48 * 1024 * 1024
MESH = pl.DeviceIdType.MESH
ANY = pl.BlockSpec(memory_space=pl.ANY)


def _cparams(*sem):
    return pltpu.CompilerParams(dimension_semantics=sem, vmem_limit_bytes=VMEM_LIMIT)


def _pick(dim, prefs):
    for p in prefs:
        if dim % p == 0:
            return p
    return dim


def _rows(s):
    return _pick(s, (512, 256, 128, 64, 32, 16, 8))


MM_VMEM_BUDGET = 36 * 1024 * 1024
MM_MAX_TILE_ELEMS = 640 * 1024
HBM_BYTES_PER_US = 3.0e6
STEP_OVERHEAD_US = 0.35


def _lane_tiles(n):
    c = {t for t in range(LANE, min(n, 1536) + 1, LANE) if n % t == 0}
    if n <= 2304 or not c:
        c.add(n)
    return sorted(c, reverse=True)


def _mm_tiles(m, n, k, sa, sb, so):
    best = None
    for tm in [t for t in (1024, 512, 256) if m % t == 0] or [m]:
        for tn in _lane_tiles(n):
            if tm * tn > MM_MAX_TILE_ELEMS:
                continue
            if 2 * (tm * k * sa + k * tn * sb + tm * tn * so) + 4 * tm * tn > MM_VMEM_BUDGET:
                continue
            steps = (m // tm) * (n // tn)
            for inner_n in (True, False):
                moved = (m * k * sa + (m // tm) * k * n * sb) if inner_n else (k * n * sb + (n // tn) * m * k * sa)
                cost = (moved + m * n * so) / HBM_BYTES_PER_US + steps * STEP_OVERHEAD_US
                if best is None or cost < best[0]:
                    best = (cost, tm, tn, inner_n)
    return best[1:]


def mm(a, b, *, ta=False, tb=False, add=None, out_dtype=F32, name):
    if ta:
        kdim, m = a.shape
    else:
        m, kdim = a.shape
    n = b.shape[0] if tb else b.shape[1]
    so = jnp.dtype(out_dtype).itemsize + (0 if add is None else add.dtype.itemsize)
    tm, tn, inner_n = _mm_tiles(m, n, kdim, a.dtype.itemsize, b.dtype.itemsize, so)
    dn = (((0 if ta else 1,), (1 if tb else 0,)), ((), ()))

    def body(*refs):
        if add is None:
            a_ref, b_ref, o_ref = refs
        else:
            a_ref, b_ref, add_ref, o_ref = refs
        r = lax.dot_general(a_ref[...].astype(BF16), b_ref[...].astype(BF16), dn, preferred_element_type=F32)
        if add is not None:
            r = r + add_ref[...].astype(F32)
        o_ref[...] = r.astype(out_dtype)

    ij = (lambda g0, g1: (g0, g1)) if inner_n else (lambda g0, g1: (g1, g0))
    a_spec = (pl.BlockSpec((kdim, tm), lambda g0, g1: (0, ij(g0, g1)[0])) if ta
              else pl.BlockSpec((tm, kdim), lambda g0, g1: (ij(g0, g1)[0], 0)))
    b_spec = (pl.BlockSpec((tn, kdim), lambda g0, g1: (ij(g0, g1)[1], 0)) if tb
              else pl.BlockSpec((kdim, tn), lambda g0, g1: (0, ij(g0, g1)[1])))
    o_spec = pl.BlockSpec((tm, tn), lambda g0, g1: ij(g0, g1))
    ins, specs = [a, b], [a_spec, b_spec]
    if add is not None:
        ins.append(add)
        specs.append(o_spec)
    grid = (m // tm, n // tn) if inner_n else (n // tn, m // tm)
    return pl.pallas_call(
        body, name=name, grid=grid, in_specs=specs, out_specs=o_spec,
        out_shape=jax.ShapeDtypeStruct((m, n), out_dtype),
        compiler_params=_cparams("parallel", "parallel"))(*ins)


def rms_fwd(x, g, *, col=0, out_dtype=BF16, name):
    s = x.shape[0]
    d = g.shape[1]
    tm = _rows(s)

    def body(x_ref, g_ref, o_ref):
        xv = x_ref[...]
        r = lax.rsqrt(jnp.mean(xv * xv, axis=-1, keepdims=True) + EPS)
        o_ref[...] = (xv * r * g_ref[...]).astype(out_dtype)

    return pl.pallas_call(
        body, name=name, grid=(s // tm,),
        in_specs=[pl.BlockSpec((tm, d), lambda i: (i, col)), pl.BlockSpec((1, d), lambda i: (0, 0))],
        out_specs=pl.BlockSpec((tm, d), lambda i: (i, 0)),
        out_shape=jax.ShapeDtypeStruct((s, d), out_dtype),
        compiler_params=_cparams("parallel"))(x, g)


def rms_bwd(x, g, dy, *, col=0, add=None, out_dtype=F32, twin=False, name):
    s = x.shape[0]
    d = g.shape[1]
    tm = _rows(s)

    def body(*refs):
        refs = list(refs)
        dg_ref = refs.pop()
        dxh_ref = refs.pop() if twin else None
        dx_ref = refs.pop()
        add_ref = refs.pop() if add is not None else None
        x_ref, g_ref, dy_ref = refs

        @pl.when(pl.program_id(0) == 0)
        def _():
            dg_ref[...] = jnp.zeros_like(dg_ref)

        xv = x_ref[...]
        dyv = dy_ref[...].astype(F32)
        r = lax.rsqrt(jnp.mean(xv * xv, axis=-1, keepdims=True) + EPS)
        xh = xv * r
        dg_ref[...] += jnp.sum(dyv * xh, axis=0, keepdims=True)
        dxh = dyv * g_ref[...]
        dx = r * (dxh - xh * jnp.mean(dxh * xh, axis=-1, keepdims=True))
        if add is not None:
            dx = dx + add_ref[...]
        dx_ref[...] = dx.astype(out_dtype)
        if twin:
            dxh_ref[...] = dx.astype(BF16)

    row = pl.BlockSpec((tm, d), lambda i: (i, 0))
    vec = pl.BlockSpec((1, d), lambda i: (0, 0))
    ins = [x, g, dy]
    specs = [pl.BlockSpec((tm, d), lambda i: (i, col)), vec, row]
    if add is not None:
        ins.append(add)
        specs.append(row)
    dxs = [jax.ShapeDtypeStruct((s, d), out_dtype)] + ([jax.ShapeDtypeStruct((s, d), BF16)] if twin else [])
    return pl.pallas_call(
        body, name=name, grid=(s // tm,), in_specs=specs,
        out_specs=[row] * len(dxs) + [vec],
        out_shape=dxs + [jax.ShapeDtypeStruct((1, d), F32)],
        compiler_params=_cparams("arbitrary"))(*ins)


def _rope_tables(s):
    inv = 1.0 / (ROPE_THETA ** (jnp.arange(0, ROPE, 2, dtype=F32) / ROPE))
    ang = jnp.arange(s, dtype=F32)[:, None] * inv[None, :]
    cos, sin = jnp.cos(ang), jnp.sin(ang)
    z = lambda w: jnp.zeros((s, w), F32)
    c = jnp.concatenate([jnp.ones((s, NOPE), F32), cos, cos, z(HP - QK)], axis=1)
    s1 = jnp.concatenate([z(NOPE), -sin, z(HP - NOPE - ROPE // 2)], axis=1)
    s2 = jnp.concatenate([z(NOPE + ROPE // 2), sin, z(HP - QK)], axis=1)
    return c, s1, s2


def qkprep_fwd(qraw, kv, proj, qg, kg, tabs, *, name):
    s = qraw.shape[0]
    tm = _rows(s)
    kr_col = (proj.shape[1] - HP) // HP

    def body(q_ref, k_ref, v_ref, kr_ref, qg_ref, kg_ref, c_ref, s1_ref, s2_ref, qo_ref, ko_ref, vo_ref):
        c, s1, s2 = c_ref[...], s1_ref[...], s2_ref[...]

        def f(xv, gain):
            r = lax.rsqrt(jnp.sum(xv * xv, axis=-1, keepdims=True) * (1.0 / QK) + EPS)
            xn = xv * r * gain
            return xn * c + pltpu.roll(xn, HP - ROPE // 2, 1) * s1 + pltpu.roll(xn, ROPE // 2, 1) * s2

        qo_ref[...] = f(q_ref[...], qg_ref[...]).astype(BF16)
        ko_ref[...] = f(k_ref[...] + kr_ref[...], kg_ref[...]).astype(BF16)
        vv = v_ref[...]
        lane = lax.broadcasted_iota(jnp.int32, vv.shape, 1)
        vo_ref[...] = jnp.where(lane == VD, 1.0, vv).astype(BF16)

    head = pl.BlockSpec((tm, HP), lambda i, h: (i, h))
    tab = pl.BlockSpec((tm, HP), lambda i, h: (i, 0))
    gain = pl.BlockSpec((1, HP), lambda i, h: (0, 0))
    return pl.pallas_call(
        body, name=name, grid=(s // tm, HEADS),
        in_specs=[head, head, pl.BlockSpec((tm, HP), lambda i, h: (i, HEADS + h)),
                  pl.BlockSpec((tm, HP), lambda i, h: (i, kr_col)), gain, gain, tab, tab, tab],
        out_specs=[head, head, head],
        out_shape=[jax.ShapeDtypeStruct((s, HEADS * HP), BF16)] * 3,
        compiler_params=_cparams("parallel", "parallel"))(qraw, kv, kv, proj, qg, kg, *tabs)


def qkprep_bwd(qraw, kv, proj, qg, kg, tabs, dq, dk, *, name):
    s = qraw.shape[0]
    tm = _rows(s)
    kr_col = (proj.shape[1] - HP) // HP

    def body(q_ref, k_ref, kr_ref, qg_ref, kg_ref, c_ref, s1_ref, s2_ref, dq_ref, dk_ref,
             dqr_ref, dkr_ref, dkrope_ref, dqg_ref, dkg_ref):
        i, h = pl.program_id(0), pl.program_id(1)
        c, s1, s2 = c_ref[...], s1_ref[...], s2_ref[...]

        @pl.when((i == 0) & (h == 0))
        def _():
            dqg_ref[...] = jnp.zeros_like(dqg_ref)
            dkg_ref[...] = jnp.zeros_like(dkg_ref)

        @pl.when(h == 0)
        def _():
            dkrope_ref[...] = jnp.zeros_like(dkrope_ref)

        def f(xv, gain, dout):
            r = lax.rsqrt(jnp.sum(xv * xv, axis=-1, keepdims=True) * (1.0 / QK) + EPS)
            xh = xv * r
            dxn = dout * c + pltpu.roll(dout * s1, ROPE // 2, 1) + pltpu.roll(dout * s2, HP - ROPE // 2, 1)
            dgain = jnp.sum(dxn * xh, axis=0, keepdims=True)
            dxh = dxn * gain
            dx = r * (dxh - xh * (jnp.sum(dxh * xh, axis=-1, keepdims=True) * (1.0 / QK)))
            return dx, dgain

        dxq, dgq = f(q_ref[...], qg_ref[...], dq_ref[...])
        dxk, dgk = f(k_ref[...] + kr_ref[...], kg_ref[...], dk_ref[...])
        dqr_ref[...] = dxq.astype(BF16)
        dkr_ref[...] = dxk.astype(BF16)
        dqg_ref[...] += dgq
        dkg_ref[...] += dgk
        lane = lax.broadcasted_iota(jnp.int32, dxk.shape, 1)
        dkrope_ref[...] += jnp.where((lane >= NOPE) & (lane < QK), dxk, 0.0)

    head = pl.BlockSpec((tm, HP), lambda i, h: (i, h))
    tab = pl.BlockSpec((tm, HP), lambda i, h: (i, 0))
    gain = pl.BlockSpec((1, HP), lambda i, h: (0, 0))
    return pl.pallas_call(
        body, name=name, grid=(s // tm, HEADS),
        in_specs=[head, head, pl.BlockSpec((tm, HP), lambda i, h: (i, kr_col)), gain, gain, tab, tab, tab, head, head],
        out_specs=[head, head, tab, gain, gain],
        out_shape=[jax.ShapeDtypeStruct((s, HEADS * HP), BF16)] * 2
        + [jax.ShapeDtypeStruct((s, HP), F32), jax.ShapeDtypeStruct((1, HP), F32), jax.ShapeDtypeStruct((1, HP), F32)],
        compiler_params=_cparams("arbitrary", "arbitrary"))(qraw, kv, proj, qg, kg, *tabs, dq, dk)


ATT_SCALE = QK ** -0.5
NEG = -1e30


def _att_tile(s):
    return _pick(s, (512, 256, 128))


def _causal(sv, diag):
    r = lax.broadcasted_iota(jnp.int32, sv.shape, 0)
    c = lax.broadcasted_iota(jnp.int32, sv.shape, 1)
    return jnp.where(diag & (c > r), NEG, sv)


NT = (((1,), (1,)), ((), ()))
TN = (((0,), (0,)), ((), ()))


def _row_of(col):
    return jnp.broadcast_to(col, (col.shape[0], LANE)).T[0:SLAB, :]


def _att_specs(s):
    t = _att_tile(s)
    nb = s // t
    tile = pl.BlockSpec((t, HP), lambda h, i: (i, h))
    whole = pl.BlockSpec((s, HP), lambda h, i: (0, h))
    row = pl.BlockSpec((1, 1, SLAB, t), lambda h, i: (h, i, 0, 0))
    rows = pl.BlockSpec((1, nb, SLAB, t), lambda h, i: (h, 0, 0, 0))
    return t, nb, tile, whole, row, rows


def attn_fwd(q, k, v, *, name):
    s = q.shape[0]
    t, nb, tile, whole, row, _ = _att_specs(s)

    def body(q_ref, k_ref, v_ref, o_ref, oh_ref, lse_ref, lset_ref, s_scr, mb_scr, acc):
        qb = pl.program_id(1)
        qv = q_ref[...]

        def scores(j):
            r0 = pl.multiple_of(j * t, t)
            return lax.dot_general(qv, k_ref[pl.ds(r0, t), :], NT, preferred_element_type=F32) * ATT_SCALE

        def fold(sv):
            m = sv[:, 0:LANE]
            for kk in range(1, t // LANE):
                m = jnp.maximum(m, sv[:, kk * LANE:(kk + 1) * LANE])
            return m

        def first(j, m):
            sv = scores(j)
            s_scr[j] = sv
            return jnp.maximum(m, fold(sv))

        m = lax.fori_loop(0, qb, first, jnp.full((t, LANE), NEG, F32))
        sd = _causal(scores(qb), True)
        s_scr[qb] = sd
        mcol = jnp.max(jnp.maximum(m, fold(sd)), axis=-1, keepdims=True)
        mb_scr[...] = jnp.broadcast_to(mcol, (t, t))
        acc[...] = jnp.zeros_like(acc)

        def second(j, carry):
            r0 = pl.multiple_of(j * t, t)
            p = jnp.exp(s_scr[j] - mb_scr[...]).astype(BF16)
            acc[...] += jnp.dot(p, v_ref[pl.ds(r0, t), :], preferred_element_type=F32)
            return carry

        lax.fori_loop(0, qb + 1, second, 0)
        av = acc[...]
        lsum = av[:, VD:VD + 1]
        lane = lax.broadcasted_iota(jnp.int32, av.shape, 1)
        ov = jnp.where(lane == VD, 0.0, av / lsum)
        o_ref[...] = ov
        oh_ref[...] = ov.astype(BF16)
        lse = mcol + jnp.log(lsum)
        lse_ref[...] = jnp.broadcast_to(lse, lse_ref.shape)
        lset_ref[0, 0] = _row_of(lse)

    return pl.pallas_call(
        body, name=name, grid=(HEADS, nb), in_specs=[tile, whole, whole], out_specs=[tile, tile, tile, row],
        out_shape=[jax.ShapeDtypeStruct((s, HEADS * HP), F32), jax.ShapeDtypeStruct((s, HEADS * HP), BF16),
                   jax.ShapeDtypeStruct((s, HEADS * HP), F32), jax.ShapeDtypeStruct((HEADS, nb, SLAB, t), F32)],
        scratch_shapes=[pltpu.VMEM((nb, t, t), F32), pltpu.VMEM((t, t), F32), pltpu.VMEM((t, HP), F32)],
        compiler_params=_cparams("parallel", "parallel"))(q, k, v)


def attn_bwd_dq(q, k, v, o, lse, do, *, name):
    s = q.shape[0]
    t, nb, tile, whole, row, _ = _att_specs(s)

    def body(q_ref, k_ref, v_ref, o_ref, lse_ref, do_ref, dq_ref, doh_ref, dt_ref, lb_scr, db_scr, acc):
        qb = pl.program_id(1)
        qv = q_ref[...]
        dov = do_ref[...]
        dob = dov.astype(BF16)
        doh_ref[...] = dob
        delta = jnp.sum(dov * o_ref[...], axis=-1, keepdims=True)
        dt_ref[0, 0] = _row_of(delta)
        lb_scr[...] = jnp.broadcast_to(lse_ref[...][:, 0:1], (t, t))
        db_scr[...] = jnp.broadcast_to(delta, (t, t))
        acc[...] = jnp.zeros_like(acc)

        def step(j, diag):
            r0 = pl.multiple_of(j * t, t)
            kj = k_ref[pl.ds(r0, t), :]
            sv = lax.dot_general(qv, kj, NT, preferred_element_type=F32) * ATT_SCALE
            if diag:
                sv = _causal(sv, True)
            p = jnp.exp(sv - lb_scr[...])
            dp = lax.dot_general(dob, v_ref[pl.ds(r0, t), :], NT, preferred_element_type=F32)
            ds = (p * (dp - db_scr[...])).astype(BF16)
            acc[...] += jnp.dot(ds, kj, preferred_element_type=F32)

        def off_diag(j, carry):
            step(j, False)
            return carry

        lax.fori_loop(0, qb, off_diag, 0)
        step(qb, True)
        dq_ref[...] = acc[...] * ATT_SCALE

    return pl.pallas_call(
        body, name=name, grid=(HEADS, nb), in_specs=[tile, whole, whole, tile, tile, tile],
        out_specs=[tile, tile, row],
        out_shape=[jax.ShapeDtypeStruct((s, HEADS * HP), F32), jax.ShapeDtypeStruct((s, HEADS * HP), BF16),
                   jax.ShapeDtypeStruct((HEADS, nb, SLAB, t), F32)],
        scratch_shapes=[pltpu.VMEM((t, t), F32), pltpu.VMEM((t, t), F32), pltpu.VMEM((t, HP), F32)],
        compiler_params=_cparams("parallel", "parallel"))(q, k, v, o, lse, do)


def attn_bwd_dkv(q, k, v, doh, lset, deltat, *, name):
    s = q.shape[0]
    t, nb, tile, whole, _, rows = _att_specs(s)

    def body(q_ref, k_ref, v_ref, do_ref, lt_ref, dt_ref, dk_ref, dv_ref, dk_acc, dv_acc):
        kb = pl.program_id(1)
        kt, vt = k_ref[...], v_ref[...]
        dk_acc[...] = jnp.zeros_like(dk_acc)
        dv_acc[...] = jnp.zeros_like(dv_acc)

        def step(i, diag):
            r0 = pl.multiple_of(i * t, t)
            qi, doi = q_ref[pl.ds(r0, t), :], do_ref[pl.ds(r0, t), :]
            st = lax.dot_general(kt, qi, NT, preferred_element_type=F32) * ATT_SCALE
            if diag:
                kr = lax.broadcasted_iota(jnp.int32, st.shape, 0)
                qc = lax.broadcasted_iota(jnp.int32, st.shape, 1)
                st = jnp.where(kr > qc, NEG, st)
            pt = jnp.exp(st - lt_ref[0, i][0:1, :])
            dpt = lax.dot_general(vt, doi, NT, preferred_element_type=F32)
            dst = (pt * (dpt - dt_ref[0, i][0:1, :])).astype(BF16)
            dv_acc[...] += jnp.dot(pt.astype(BF16), doi, preferred_element_type=F32)
            dk_acc[...] += jnp.dot(dst, qi, preferred_element_type=F32)

        def off_diag(i, carry):
            step(i, False)
            return carry

        step(kb, True)
        lax.fori_loop(kb + 1, nb, off_diag, 0)
        dk_ref[...] = dk_acc[...] * ATT_SCALE
        dvv = dv_acc[...]
        lane = lax.broadcasted_iota(jnp.int32, dvv.shape, 1)
        dv_ref[...] = jnp.where(lane == VD, 0.0, dvv).astype(BF16)

    return pl.pallas_call(
        body, name=name, grid=(HEADS, nb), in_specs=[whole, tile, tile, whole, rows, rows], out_specs=[tile, tile],
        out_shape=[jax.ShapeDtypeStruct((s, HEADS * HP), F32), jax.ShapeDtypeStruct((s, HEADS * HP), BF16)],
        scratch_shapes=[pltpu.VMEM((t, HP), F32), pltpu.VMEM((t, HP), F32)],
        compiler_params=_cparams("parallel", "parallel"))(q, k, v, doh, lset, deltat)


HALO = 8
CW = 256


def _conv3(zw, w):
    return w[2:3] * zw + w[1:2] * pltpu.roll(zw, 1, 0) + w[0:1] * pltpu.roll(zw, 2, 0)


def _conv3_t(dc, w):
    n = dc.shape[0]
    return w[2:3] * dc + w[1:2] * pltpu.roll(dc, n - 1, 0) + w[0:1] * pltpu.roll(dc, n - 2, 0)


def _conv3_dw(dc, zw, r):
    z0 = zw[HALO:HALO + r]
    z1 = pltpu.roll(zw, 1, 0)[HALO:HALO + r]
    z2 = pltpu.roll(zw, 2, 0)[HALO:HALO + r]
    return [jnp.sum(dc * z, axis=0, keepdims=True) for z in (z2, z1, z0)]


def _halo_specs(r, colfn):
    rb = r // HALO
    cur = pl.BlockSpec((r, CW), lambda j, i: (i, colfn(j)))
    prev = pl.BlockSpec((HALO, CW), lambda j, i: (jnp.maximum(i * rb - 1, 0), colfn(j)))

    def nxt(nrow_blocks):
        return pl.BlockSpec((HALO, CW), lambda j, i: (jnp.minimum((i + 1) * rb, nrow_blocks * rb - 1), colfn(j)))

    return cur, prev, nxt


def ffnact_fwd(up, w, *, name):
    s, c2 = up.shape
    hh = c2 // 2
    nj = hh // CW
    r = _rows(s)
    nt = s // r

    def body(g_ref, gp_ref, v_ref, vp_ref, wg_ref, wv_ref, o_ref):
        pm = (pl.program_id(1) > 0).astype(F32)
        cg = _conv3(jnp.concatenate([gp_ref[...] * pm, g_ref[...]], axis=0), wg_ref[...])[HALO:]
        cv = _conv3(jnp.concatenate([vp_ref[...] * pm, v_ref[...]], axis=0), wv_ref[...])[HALO:]
        o_ref[...] = (cg * jax.nn.sigmoid(cg) * cv).astype(BF16)

    gcur, gprev, _ = _halo_specs(r, lambda j: j)
    vcur, vprev, _ = _halo_specs(r, lambda j: nj + j)
    wg = pl.BlockSpec((3, CW), lambda j, i: (0, j))
    wv = pl.BlockSpec((3, CW), lambda j, i: (0, nj + j))
    return pl.pallas_call(
        body, name=name, grid=(nj, nt), in_specs=[gcur, gprev, vcur, vprev, wg, wv],
        out_specs=pl.BlockSpec((r, CW), lambda j, i: (i, j)),
        out_shape=jax.ShapeDtypeStruct((s, hh), BF16),
        compiler_params=_cparams("parallel", "parallel"))(up, up, up, up, w, w)


def ffnact_bwd(up, w, dact, *, name):
    s, c2 = up.shape
    hh = c2 // 2
    nj = hh // CW
    r = _rows(s)
    nt = s // r

    def body(g_ref, gp_ref, gn_ref, v_ref, vp_ref, vn_ref, wg_ref, wv_ref, da_ref, dan_ref,
             dg_ref, dv_ref, dwg_ref, dwv_ref):
        i = pl.program_id(1)
        pm = (i > 0).astype(F32)
        nm = (i < nt - 1).astype(F32)

        @pl.when(i == 0)
        def _():
            dwg_ref[...] = jnp.zeros_like(dwg_ref)
            dwv_ref[...] = jnp.zeros_like(dwv_ref)

        wg, wv = wg_ref[...], wv_ref[...]
        zg = jnp.concatenate([gp_ref[...] * pm, g_ref[...], gn_ref[...]], axis=0)
        zv = jnp.concatenate([vp_ref[...] * pm, v_ref[...], vn_ref[...]], axis=0)
        cg = _conv3(zg, wg)[HALO:]
        cv = _conv3(zv, wv)[HALO:]
        da = jnp.concatenate([da_ref[...], dan_ref[...] * nm], axis=0)
        sg = jax.nn.sigmoid(cg)
        dcg = da * cv * (sg * (1.0 + cg * (1.0 - sg)))
        dcv = da * (cg * sg)
        dg_ref[...] = _conv3_t(dcg, wg)[:r].astype(BF16)
        dv_ref[...] = _conv3_t(dcv, wv)[:r].astype(BF16)
        for kk, (a, b) in enumerate(zip(_conv3_dw(dcg[:r], zg, r), _conv3_dw(dcv[:r], zv, r))):
            dwg_ref[kk:kk + 1, :] += a
            dwv_ref[kk:kk + 1, :] += b

    gcur, gprev, gnext = _halo_specs(r, lambda j: j)
    vcur, vprev, vnext = _halo_specs(r, lambda j: nj + j)
    acur, _, anext = _halo_specs(r, lambda j: j)
    wg = pl.BlockSpec((3, CW), lambda j, i: (0, j))
    wv = pl.BlockSpec((3, CW), lambda j, i: (0, nj + j))
    dupg, dupv, dwg, dwv = pl.pallas_call(
        body, name=name, grid=(nj, nt),
        in_specs=[gcur, gprev, gnext(nt), vcur, vprev, vnext(nt), wg, wv, acur, anext(nt)],
        out_specs=[acur, acur, wg, wg],
        out_shape=[jax.ShapeDtypeStruct((s, hh), BF16), jax.ShapeDtypeStruct((s, hh), BF16),
                   jax.ShapeDtypeStruct((3, hh), F32), jax.ShapeDtypeStruct((3, hh), F32)],
        compiler_params=_cparams("parallel", "arbitrary"))(up, up, up, up, up, up, w, w, dact, dact)
    return jnp.concatenate([dupg, dupv], axis=1), jnp.concatenate([dwg, dwv], axis=1)


def sconv_fwd(proj, w, *, name):
    s = proj.shape[0]
    nj = CONVC // CW
    r = _rows(s)
    nt = s // r

    def body(b_ref, c_ref, cp_ref, x_ref, xp_ref, w_ref, o_ref):
        pm = (pl.program_id(1) > 0).astype(F32)
        zw = jnp.concatenate([cp_ref[...] * xp_ref[...] * pm, c_ref[...] * x_ref[...]], axis=0)
        o_ref[...] = (b_ref[...] * _conv3(zw, w_ref[...])[HALO:]).astype(BF16)

    bcur, _, _ = _halo_specs(r, lambda j: (QR + KVR) // CW + j)
    ccur, cprev, _ = _halo_specs(r, lambda j: (QR + KVR + CONVC) // CW + j)
    xcur, xprev, _ = _halo_specs(r, lambda j: (QR + KVR + 2 * CONVC) // CW + j)
    ws = pl.BlockSpec((3, CW), lambda j, i: (0, j))
    return pl.pallas_call(
        body, name=name, grid=(nj, nt), in_specs=[bcur, ccur, cprev, xcur, xprev, ws],
        out_specs=pl.BlockSpec((r, CW), lambda j, i: (i, j)),
        out_shape=jax.ShapeDtypeStruct((s, CONVC), BF16),
        compiler_params=_cparams("parallel", "parallel"))(proj, proj, proj, proj, proj, w)


def sconv_bwd(proj, w, dy, *, name):
    s = proj.shape[0]
    nj = CONVC // CW
    r = _rows(s)
    nt = s // r

    def body(b_ref, bn_ref, c_ref, cp_ref, x_ref, xp_ref, w_ref, dy_ref, dyn_ref, db_ref, dc_ref, dx_ref, dw_ref):
        i = pl.program_id(1)
        pm = (i > 0).astype(F32)
        nm = (i < nt - 1).astype(F32)

        @pl.when(i == 0)
        def _():
            dw_ref[...] = jnp.zeros_like(dw_ref)

        wv = w_ref[...]
        zw = jnp.concatenate([cp_ref[...] * xp_ref[...] * pm, c_ref[...] * x_ref[...]], axis=0)
        conv = _conv3(zw, wv)[HALO:]
        dyv = dy_ref[...]
        db_ref[...] = (dyv * conv).astype(BF16)
        dconv = jnp.concatenate([dyv * b_ref[...], dyn_ref[...] * bn_ref[...] * nm], axis=0)
        dz = _conv3_t(dconv, wv)[:r]
        dc_ref[...] = (dz * x_ref[...]).astype(BF16)
        dx_ref[...] = (dz * c_ref[...]).astype(BF16)
        for kk, a in enumerate(_conv3_dw(dconv[:r], zw, r)):
            dw_ref[kk:kk + 1, :] += a

    bcur, _, bnext = _halo_specs(r, lambda j: (QR + KVR) // CW + j)
    ccur, cprev, _ = _halo_specs(r, lambda j: (QR + KVR + CONVC) // CW + j)
    xcur, xprev, _ = _halo_specs(r, lambda j: (QR + KVR + 2 * CONVC) // CW + j)
    ycur, _, ynext = _halo_specs(r, lambda j: j)
    ws = pl.BlockSpec((3, CW), lambda j, i: (0, j))
    out = pl.BlockSpec((r, CW), lambda j, i: (i, j))
    db, dc, dx, dw = pl.pallas_call(
        body, name=name, grid=(nj, nt),
        in_specs=[bcur, bnext(nt), ccur, cprev, xcur, xprev, ws, ycur, ynext(nt)],
        out_specs=[out, out, out, ws],
        out_shape=[jax.ShapeDtypeStruct((s, CONVC), BF16)] * 3 + [jax.ShapeDtypeStruct((3, CONVC), F32)],
        compiler_params=_cparams("parallel", "arbitrary"))(proj, proj, proj, proj, proj, proj, w, dy, dy)
    return jnp.concatenate([db, dc, dx], axis=1), dw


SW = 512
NJ = NST // SW


def _scan_tables(ar, ai):
    def cmul(x, y):
        return x[0] * y[0] - x[1] * y[1], x[0] * y[1] + x[1] * y[0]

    def build(a, reverse):
        pw = [a]
        for _ in range(SLAB - 1):
            pw.append(cmul(pw[-1], a))
        row = jnp.arange(SLAB)[:, None]
        tabs = []
        for kk in (1, 2, 4):
            mask = ((row < SLAB - kk) if reverse else (row >= kk)).astype(F32)
            tabs += [mask * pw[kk - 1][0][None, :], mask * pw[kk - 1][1][None, :]]
        order = list(range(SLAB - 1, -1, -1)) if reverse else list(range(SLAB))
        tabs += [jnp.stack([pw[o][0] for o in order]), jnp.stack([pw[o][1] for o in order])]
        return jnp.stack(tabs)

    return build((ar, ai), False), build((ar, -ai), True)


def _slab_scan(xr, xi, tabs, cr, ci, reverse):
    for n, kk in enumerate((1, 2, 4)):
        sh = SLAB - kk if reverse else kk
        tr, ti = tabs[2 * n], tabs[2 * n + 1]
        sr, si = pltpu.roll(xr, sh, 0), pltpu.roll(xi, sh, 0)
        xr, xi = xr + tr * sr - ti * si, xi + tr * si + ti * sr
    tr, ti = tabs[6], tabs[7]
    return xr + tr * cr - ti * ci, xi + tr * ci + ti * cr


def s5_fwd(u, bbd_r, bbd_i, cbd_r, cbd_i, tab, *, name):
    s = u.shape[0]
    tbk = _rows(s)
    nt = s // tbk
    nsl = tbk // SLAB

    def body(u_ref, br_ref, bi_ref, cr_ref, ci_ref, tab_ref, y_ref, sr_ref, si_ref, bur, bui, carry):
        @pl.when(pl.program_id(1) == 0)
        def _():
            carry[...] = jnp.zeros_like(carry)

        ub = u_ref[...].astype(BF16)
        bur[...] = jnp.dot(ub, br_ref[0], preferred_element_type=F32)
        bui[...] = jnp.dot(ub, bi_ref[0], preferred_element_type=F32)
        tabs = [tab_ref[n] for n in range(8)]

        def slab(n, c):
            r0 = pl.multiple_of(n * SLAB, SLAB)
            sr, si = _slab_scan(bur[pl.ds(r0, SLAB), :], bui[pl.ds(r0, SLAB), :], tabs, c[0], c[1], False)
            sr_ref[pl.ds(r0, SLAB), :] = sr
            si_ref[pl.ds(r0, SLAB), :] = si
            return (jnp.broadcast_to(sr[SLAB - 1:SLAB], sr.shape), jnp.broadcast_to(si[SLAB - 1:SLAB], si.shape))

        cr, ci = lax.fori_loop(0, nsl, slab, (carry[0], carry[1]))
        carry[0] = cr
        carry[1] = ci
        y_ref[...] = (jnp.dot(sr_ref[...].astype(BF16), cr_ref[0], preferred_element_type=F32)
                      - jnp.dot(si_ref[...].astype(BF16), ci_ref[0], preferred_element_type=F32))

    us = pl.BlockSpec((tbk, LANE), lambda j, t: (t, j))
    bs = pl.BlockSpec((1, LANE, SW), lambda j, t: (j, 0, 0))
    cs = pl.BlockSpec((1, SW, LANE), lambda j, t: (j, 0, 0))
    ts = pl.BlockSpec((8, SLAB, SW), lambda j, t: (0, 0, j))
    ss = pl.BlockSpec((tbk, SW), lambda j, t: (t, j))
    return pl.pallas_call(
        body, name=name, grid=(NJ, nt), in_specs=[us, bs, bs, cs, cs, ts], out_specs=[us, ss, ss],
        out_shape=[jax.ShapeDtypeStruct((s, D_MODEL), F32), jax.ShapeDtypeStruct((s, NST), F32),
                   jax.ShapeDtypeStruct((s, NST), F32)],
        scratch_shapes=[pltpu.VMEM((tbk, SW), F32), pltpu.VMEM((tbk, SW), F32), pltpu.VMEM((2, SLAB, SW), F32)],
        compiler_params=_cparams("parallel", "arbitrary"))(u, bbd_r, bbd_i, cbd_r, cbd_i, tab)


def s5_bwd(u, dy, dskip, st_r, st_i, bbd_r, bbd_i, cbd_r, cbd_i, tabrev, *, name):
    s = u.shape[0]
    tbk = _rows(s)
    nt = s // tbk
    nsl = tbk // SLAB
    rbk = tbk // SLAB

    def body(u_ref, dy_ref, d_ref, sr_ref, si_ref, pr_ref, pi_ref, br_ref, bi_ref, cr_ref, ci_ref, tab_ref,
             du_ref, dbr_ref, dbi_ref, dcr_ref, dci_ref, da_ref, lam_r, lam_i, carry):
        t = pl.program_id(1)

        @pl.when(t == 0)
        def _():
            carry[...] = jnp.zeros_like(carry)
            dbr_ref[...] = jnp.zeros_like(dbr_ref)
            dbi_ref[...] = jnp.zeros_like(dbi_ref)
            dcr_ref[...] = jnp.zeros_like(dcr_ref)
            dci_ref[...] = jnp.zeros_like(dci_ref)
            da_ref[...] = jnp.zeros_like(da_ref)

        dyv = dy_ref[...]
        dyh = dyv.astype(BF16)
        lam_r[...] = lax.dot_general(dyh, cr_ref[0], NT, preferred_element_type=F32)
        lam_i[...] = -lax.dot_general(dyh, ci_ref[0], NT, preferred_element_type=F32)
        tabs = [tab_ref[n] for n in range(8)]

        def slab(n, c):
            r0 = pl.multiple_of((nsl - 1 - n) * SLAB, SLAB)
            lr, li = _slab_scan(lam_r[pl.ds(r0, SLAB), :], lam_i[pl.ds(r0, SLAB), :], tabs, c[0], c[1], True)
            lam_r[pl.ds(r0, SLAB), :] = lr
            lam_i[pl.ds(r0, SLAB), :] = li
            return (jnp.broadcast_to(lr[0:1], lr.shape), jnp.broadcast_to(li[0:1], li.shape))

        cr, ci = lax.fori_loop(0, nsl, slab, (carry[0], carry[1]))
        carry[0] = cr
        carry[1] = ci
        lr, li = lam_r[...], lam_i[...]
        lrh, lih = lr.astype(BF16), li.astype(BF16)
        du = (dyv * d_ref[...] + lax.dot_general(lrh, br_ref[0], NT, preferred_element_type=F32)
              + lax.dot_general(lih, bi_ref[0], NT, preferred_element_type=F32))
        du_ref[...] = du.astype(BF16)
        ub = u_ref[...].astype(BF16)
        dbr_ref[0] += lax.dot_general(ub, lrh, TN, preferred_element_type=F32)
        dbi_ref[0] += lax.dot_general(ub, lih, TN, preferred_element_type=F32)
        srv, siv = sr_ref[...], si_ref[...]
        dcr_ref[0] += lax.dot_general(srv.astype(BF16), dyh, TN, preferred_element_type=F32)
        dci_ref[0] -= lax.dot_general(siv.astype(BF16), dyh, TN, preferred_element_type=F32)
        first = lax.broadcasted_iota(jnp.int32, srv.shape, 0) == 0
        pm = (t < nt - 1).astype(F32)
        spr = jnp.where(first, pr_ref[SLAB - 1:SLAB, :] * pm, pltpu.roll(srv, 1, 0))
        spi = jnp.where(first, pi_ref[SLAB - 1:SLAB, :] * pm, pltpu.roll(siv, 1, 0))
        da_ref[0:1, :] += jnp.sum(lr * spr + li * spi, axis=0, keepdims=True)
        da_ref[1:2, :] += jnp.sum(li * spr - lr * spi, axis=0, keepdims=True)

    rv = lambda t: nt - 1 - t
    us = pl.BlockSpec((tbk, LANE), lambda j, t: (rv(t), j))
    ds = pl.BlockSpec((1, LANE), lambda j, t: (0, j))
    ss = pl.BlockSpec((tbk, SW), lambda j, t: (rv(t), j))
    ps = pl.BlockSpec((SLAB, SW), lambda j, t: (jnp.maximum(rv(t) * rbk - 1, 0), j))
    bs = pl.BlockSpec((1, LANE, SW), lambda j, t: (j, 0, 0))
    cs = pl.BlockSpec((1, SW, LANE), lambda j, t: (j, 0, 0))
    ts = pl.BlockSpec((8, SLAB, SW), lambda j, t: (0, 0, j))
    das = pl.BlockSpec((2, SW), lambda j, t: (0, j))
    return pl.pallas_call(
        body, name=name, grid=(NJ, nt),
        in_specs=[us, us, ds, ss, ss, ps, ps, bs, bs, cs, cs, ts],
        out_specs=[us, bs, bs, cs, cs, das],
        out_shape=[jax.ShapeDtypeStruct((s, D_MODEL), BF16),
                   jax.ShapeDtypeStruct((NJ, LANE, SW), F32), jax.ShapeDtypeStruct((NJ, LANE, SW), F32),
                   jax.ShapeDtypeStruct((NJ, SW, LANE), F32), jax.ShapeDtypeStruct((NJ, SW, LANE), F32),
                   jax.ShapeDtypeStruct((2, NST), F32)],
        scratch_shapes=[pltpu.VMEM((tbk, SW), F32), pltpu.VMEM((tbk, SW), F32), pltpu.VMEM((2, SLAB, SW), F32)],
        compiler_params=_cparams("parallel", "arbitrary"))(
            u, dy, dskip, st_r, st_i, st_r, st_i, bbd_r, bbd_i, cbd_r, cbd_i, tabrev)


GELU_C = math.sqrt(2.0 / math.pi)
GELU_A = 0.044715


def s5post_fwd(y, u, dskip, *, name):
    s = y.shape[0]
    tm = _rows(s)

    def body(y_ref, u_ref, d_ref, o_ref):
        z = y_ref[...] + d_ref[...] * u_ref[...]
        o_ref[...] = (0.5 * z * (1.0 + jnp.tanh(GELU_C * (z + GELU_A * z * z * z)))).astype(BF16)

    row = pl.BlockSpec((tm, D_MODEL), lambda i: (i, 0))
    vec = pl.BlockSpec((1, D_MODEL), lambda i: (0, 0))
    return pl.pallas_call(body, name=name, grid=(s // tm,), in_specs=[row, row, vec], out_specs=row,
                          out_shape=jax.ShapeDtypeStruct((s, D_MODEL), BF16),
                          compiler_params=_cparams("parallel"))(y, u, dskip)


def s5post_bwd(y, u, dskip, dg, *, name):
    s = y.shape[0]
    tm = _rows(s)

    def body(y_ref, u_ref, d_ref, dg_ref, dz_ref, dd_ref):
        @pl.when(pl.program_id(0) == 0)
        def _():
            dd_ref[...] = jnp.zeros_like(dd_ref)

        uv = u_ref[...]
        z = y_ref[...] + d_ref[...] * uv
        th = jnp.tanh(GELU_C * (z + GELU_A * z * z * z))
        dgelu = 0.5 * (1.0 + th) + 0.5 * z * (1.0 - th * th) * (GELU_C * (1.0 + 3.0 * GELU_A * z * z))
        dz = dg_ref[...] * dgelu
        dz_ref[...] = dz
        dd_ref[...] += jnp.sum(dz * uv, axis=0, keepdims=True)

    row = pl.BlockSpec((tm, D_MODEL), lambda i: (i, 0))
    vec = pl.BlockSpec((1, D_MODEL), lambda i: (0, 0))
    return pl.pallas_call(body, name=name, grid=(s // tm,), in_specs=[row, row, vec, row], out_specs=[row, vec],
                          out_shape=[jax.ShapeDtypeStruct((s, D_MODEL), F32), jax.ShapeDtypeStruct((1, D_MODEL), F32)],
                          compiler_params=_cparams("arbitrary"))(y, u, dskip, dg)


def glu_fwd(glu, x, *, name):
    s = x.shape[0]
    tm = _rows(s)

    def body(a_ref, b_ref, x_ref, o_ref):
        o_ref[...] = x_ref[...] + a_ref[...] * jax.nn.sigmoid(b_ref[...])

    row = pl.BlockSpec((tm, D_MODEL), lambda i: (i, 0))
    return pl.pallas_call(body, name=name, grid=(s // tm,),
                          in_specs=[row, pl.BlockSpec((tm, D_MODEL), lambda i: (i, 1)), row], out_specs=row,
                          out_shape=jax.ShapeDtypeStruct((s, D_MODEL), F32),
                          compiler_params=_cparams("parallel"))(glu, glu, x)


def glu_bwd(glu, dx, *, name):
    s = dx.shape[0]
    tm = _rows(s)

    def body(a_ref, b_ref, dx_ref, o_ref):
        sg = jax.nn.sigmoid(b_ref[...])
        dxv = dx_ref[...]
        o_ref[:, :D_MODEL] = (dxv * sg).astype(BF16)
        o_ref[:, D_MODEL:] = (dxv * a_ref[...] * sg * (1.0 - sg)).astype(BF16)

    row = pl.BlockSpec((tm, D_MODEL), lambda i: (i, 0))
    return pl.pallas_call(body, name=name, grid=(s // tm,),
                          in_specs=[row, pl.BlockSpec((tm, D_MODEL), lambda i: (i, 1)), row],
                          out_specs=pl.BlockSpec((tm, 2 * D_MODEL), lambda i: (i, 0)),
                          out_shape=jax.ShapeDtypeStruct((s, 2 * D_MODEL), BF16),
                          compiler_params=_cparams("parallel"))(glu, glu, dx)


def loss_head(y, target, *, name):
    s = y.shape[0]
    tm = _rows(s)

    def body(y_ref, t_ref, dy_ref, dyh_ref, l_ref):
        @pl.when(pl.program_id(0) == 0)
        def _():
            l_ref[...] = jnp.zeros_like(l_ref)

        e = y_ref[...] - t_ref[...]
        dy_ref[...] = e * (1.0 / D_MODEL)
        dyh_ref[...] = (e * (1.0 / D_MODEL)).astype(BF16)
        e2 = jnp.sum((e * e).reshape(tm // 8, 8, D_MODEL), axis=0)
        acc = e2[:, 0:LANE]
        for kk in range(1, D_MODEL // LANE):
            acc = acc + e2[:, kk * LANE:(kk + 1) * LANE]
        l_ref[...] += acc

    row = pl.BlockSpec((tm, D_MODEL), lambda i: (i, 0))
    return pl.pallas_call(body, name=name, grid=(s // tm,), in_specs=[row, row],
                          out_specs=[row, row, pl.BlockSpec((8, LANE), lambda i: (0, 0))],
                          out_shape=[jax.ShapeDtypeStruct((s, D_MODEL), F32), jax.ShapeDtypeStruct((s, D_MODEL), BF16),
                                     jax.ShapeDtypeStruct((8, LANE), F32)],
                          compiler_params=_cparams("arbitrary"))(y, target)


PACKW = 1024
NCHIP = 4


def _mesh_pos():
    return lax.axis_index("x"), lax.axis_index("y"), lax.axis_index("c")


def _chip_exchange(bufs, scatter, name):
    n = len(bufs)
    shapes = [b.shape[1:] if scatter else b.shape for b in bufs]

    def body(*refs):
        ins, outs = refs[:n], refs[n:2 * n]
        send_sems, recv_sems, local_sems = refs[2 * n:]
        x, y, c = _mesh_pos()
        me = 2 * x + y
        peers = [(1 - x, y), (x, 1 - y), (1 - x, 1 - y)]

        def copy(a, j, px, py, dst_slot):
            src = ins[a].at[2 * px + py] if scatter else ins[a]
            return pltpu.make_async_remote_copy(src_ref=src, dst_ref=outs[a].at[dst_slot],
                                                send_sem=send_sems.at[3 * a + j], recv_sem=recv_sems.at[3 * a + j],
                                                device_id=(px, py, c), device_id_type=MESH)

        mine = [pltpu.make_async_copy(ins[a].at[me] if scatter else ins[a], outs[a].at[me], local_sems.at[a])
                for a in range(n)]
        sends = [copy(a, j, px, py, me) for a in range(n) for j, (px, py) in enumerate(peers)]
        for cp in mine + sends:
            cp.start()
        for a in range(n):
            for j, (px, py) in enumerate(peers):
                copy(a, j, px, py, 2 * px + py).wait_recv()
        for cp in sends:
            cp.wait_send()
        for cp in mine:
            cp.wait()

    return pl.pallas_call(
        body, name=name, in_specs=[ANY] * n, out_specs=[ANY] * n,
        out_shape=[jax.ShapeDtypeStruct((NCHIP,) + tuple(shp), b.dtype) for shp, b in zip(shapes, bufs)],
        scratch_shapes=[pltpu.SemaphoreType.DMA((3 * n,)), pltpu.SemaphoreType.DMA((3 * n,)),
                        pltpu.SemaphoreType.DMA((n,))],
    )(*bufs)


HBM_SPEC = pl.BlockSpec(memory_space=pltpu.HBM)
SEM_SPEC = pl.BlockSpec(memory_space=pltpu.SEMAPHORE)
DATAFLOW = pltpu.SideEffectType.DATAFLOW_SIDE_EFFECTING


def _exchange_copy(ins, lands, send_sems, recv_sems, scatter, a, j, px, py, c, dst_slot):
    src = ins[a].at[2 * px + py] if scatter else ins[a]
    return pltpu.make_async_remote_copy(src_ref=src, dst_ref=lands[a].at[dst_slot],
                                        send_sem=send_sems.at[3 * a + j], recv_sem=recv_sems.at[3 * a + j],
                                        device_id=(px, py, c), device_id_type=MESH)


def exchange_start(bufs, lands, scatter, name):
    n = len(bufs)

    def body(*refs):
        ins, lnd, send_sems, recv_sems, token = refs[:n], refs[n:2 * n], refs[2 * n], refs[2 * n + 1], refs[-1]
        x, y, c = _mesh_pos()
        me = 2 * x + y
        for a in range(n):
            for j, (px, py) in enumerate([(1 - x, y), (x, 1 - y), (1 - x, 1 - y)]):
                _exchange_copy(ins, lnd, send_sems, recv_sems, scatter, a, j, px, py, c, me).start()
        token[...] = jnp.zeros_like(token)

    thru = [pltpu.HBM(b.shape, b.dtype) for b in list(bufs) + list(lands)]
    out = pl.pallas_call(
        body, name=name, in_specs=[HBM_SPEC] * (2 * n),
        out_specs=[SEM_SPEC, SEM_SPEC] + [HBM_SPEC] * (2 * n) + [pl.BlockSpec(memory_space=pltpu.VMEM)],
        out_shape=[pltpu.SemaphoreType.DMA((3 * n,)), pltpu.SemaphoreType.DMA((3 * n,))] + thru
        + [jax.ShapeDtypeStruct((SLAB, LANE), F32)],
        input_output_aliases={k: 2 + k for k in range(2 * n)},
        compiler_params=pltpu.CompilerParams(has_side_effects=DATAFLOW),
    )(*[pltpu.with_memory_space_constraint(b, pltpu.HBM) for b in list(bufs) + list(lands)])
    return out[0], out[1], out[2:2 + n], out[2 + n:2 + 2 * n], out[-1][0, 0]


def exchange_wait(send_sems, recv_sems, bufs, lands, after, scatter, name):
    n = len(bufs)

    def body(*refs):
        ins, lnd, ssem, rsem = refs[:n], refs[n:2 * n], refs[2 * n], refs[2 * n + 1]
        x, y, c = _mesh_pos()
        for a in range(n):
            for j, (px, py) in enumerate([(1 - x, y), (x, 1 - y), (1 - x, 1 - y)]):
                cp = _exchange_copy(ins, lnd, ssem, rsem, scatter, a, j, px, py, c, 2 * px + py)
                cp.wait_send()
                cp.wait_recv()

    thru = [pltpu.HBM(b.shape, b.dtype) for b in list(bufs) + list(lands)]
    out = pl.pallas_call(
        body, name=name, in_specs=[HBM_SPEC] * (2 * n) + [SEM_SPEC, SEM_SPEC, ANY],
        out_specs=[HBM_SPEC] * (2 * n), out_shape=thru,
        input_output_aliases={k: k for k in range(2 * n)},
        compiler_params=pltpu.CompilerParams(has_side_effects=DATAFLOW),
    )(*bufs, *lands, send_sems, recv_sems, after)
    return out[n:]


def sibling_swap(bufs, name):
    n = len(bufs)

    def body(*refs):
        ins, outs, send_sems, recv_sems = refs[:n], refs[n:2 * n], refs[2 * n], refs[2 * n + 1]
        x, y, c = _mesh_pos()
        cps = [pltpu.make_async_remote_copy(src_ref=ins[k], dst_ref=outs[k], send_sem=send_sems.at[k],
                                            recv_sem=recv_sems.at[k], device_id=(x, y, 1 - c), device_id_type=MESH)
               for k in range(n)]
        for cp in cps:
            cp.start()
        for cp in cps:
            cp.wait()

    return pl.pallas_call(
        body, name=name, in_specs=[ANY] * n, out_specs=[ANY] * n,
        out_shape=[jax.ShapeDtypeStruct(b.shape, b.dtype) for b in bufs],
        scratch_shapes=[pltpu.SemaphoreType.DMA((n,)), pltpu.SemaphoreType.DMA((n,))],
    )(*bufs)


EW_VMEM_BUDGET = 20 * 1024 * 1024


def _ew_rows(rows, w, bytes_per_elem):
    wpad = -(-w // LANE) * LANE
    for t in (1024, 512, 256, 128, 64, 32, 16, 8):
        if rows % t == 0 and 2 * t * wpad * bytes_per_elem <= EW_VMEM_BUDGET:
            return t
    return rows


def sum_slots(buf, *, name):
    _, rows, w = buf.shape
    tm = _ew_rows(rows, w, NCHIP * buf.dtype.itemsize + 4)

    def body(b_ref, o_ref):
        acc = b_ref[0].astype(F32)
        for kk in range(1, NCHIP):
            acc = acc + b_ref[kk].astype(F32)
        o_ref[...] = acc

    return pl.pallas_call(body, name=name, grid=(rows // tm,),
                          in_specs=[pl.BlockSpec((NCHIP, tm, w), lambda i: (0, i, 0))],
                          out_specs=pl.BlockSpec((tm, w), lambda i: (i, 0)),
                          out_shape=jax.ShapeDtypeStruct((rows, w), F32),
                          compiler_params=_cparams("parallel"))(buf)


def adamw(p_mine, p_other, w, m, v, *, name):
    rows, wd = w.shape
    tm = _ew_rows(rows, wd, 9 * 4)
    c1 = 1.0 - ADAM_B1 ** ADAM_STEP
    c2 = 1.0 - ADAM_B2 ** ADAM_STEP

    def body(a_ref, b_ref, w_ref, m_ref, v_ref, g_ref, d_ref, nm_ref, nv_ref):
        g = a_ref[...] + b_ref[...]
        nm = ADAM_B1 * m_ref[...] + (1.0 - ADAM_B1) * g
        nv = ADAM_B2 * v_ref[...] + (1.0 - ADAM_B2) * (g * g)
        g_ref[...] = g
        nm_ref[...] = nm
        nv_ref[...] = nv
        d_ref[...] = -ADAM_LR * ((nm / c1) / (jnp.sqrt(nv / c2) + ADAM_EPS) + ADAM_WD * w_ref[...])

    row = pl.BlockSpec((tm, wd), lambda i: (i, 0))
    return pl.pallas_call(body, name=name, grid=(rows // tm,), in_specs=[row] * 5, out_specs=[row] * 4,
                          out_shape=[jax.ShapeDtypeStruct((rows, wd), F32)] * 4,
                          compiler_params=_cparams("parallel"))(p_mine, p_other, w, m, v)


def _rows2d(a, lead=0):
    tail = a.shape[lead:]
    n = int(np.prod(tail))
    if tail[-1] < LANE // 2 and n % (8 * LANE) == 0:
        return a.reshape(a.shape[:lead] + (n // (8 * LANE), 8 * LANE))
    return a.reshape(a.shape[:lead] + (-1, tail[-1]))


BIG = [("mix_w_in", 2), ("w_uq", 2), ("w_ukv", 2), ("mix_w_out", 1), ("ssm_w_in", 1), ("w_glu", 2),
       ("ffn_w_up", 2), ("ffn_w_down", 1)]
SMALL = [("sconv_w", 2), ("ssm_norm", 1), ("d_skip", 1), ("ffn_conv_w", 2)]
REPL = ["attn_norm", "cq_norm", "ckv_norm", "q_gain", "k_gain", "lambda_re", "lambda_im", "log_step",
        "b_re", "b_im", "c_re", "c_im", "ffn_norm"]
ORDER = ["attn_norm", "mix_w_in", "cq_norm", "ckv_norm", "w_uq", "w_ukv", "q_gain", "k_gain", "sconv_w", "mix_w_out",
         "ssm_norm", "ssm_w_in", "lambda_re", "lambda_im", "log_step", "b_re", "b_im", "c_re", "c_im", "d_skip",
         "w_glu", "ffn_norm", "ffn_w_up", "ffn_conv_w", "ffn_w_down"]


def _join(g, axis):
    return jnp.concatenate([g[k] for k in range(NCHIP)], axis=axis)


def _split(full, axis):
    return jnp.stack(jnp.split(full, NCHIP, axis=axis))


def _discretize(lr, li, ls, b_re, b_im):
    dt = jnp.exp(ls)[:, None]
    mag = jnp.exp(lr * dt)
    ar, ai = mag * jnp.cos(li * dt), mag * jnp.sin(li * dt)
    nr, ni = ar - 1.0, ai
    den = lr * lr + li * li
    zr, zi = (nr * lr + ni * li) / den, (ni * lr - nr * li) / den
    bbar_r = zr[..., None] * b_re - zi[..., None] * b_im
    bbar_i = zr[..., None] * b_im + zi[..., None] * b_re
    return ar, ai, bbar_r, bbar_i


def _b_blockdiag(bbar):
    gl = G // NJ
    bb = bbar.reshape(NJ, gl, P, GC).transpose(0, 1, 3, 2)
    return jnp.einsum("jgcp,gh->jgchp", bb, jnp.eye(gl, dtype=bbar.dtype)).reshape(NJ, gl * GC, gl * P)


def _b_blockdiag_t(dbd):
    gl = G // NJ
    d = jnp.einsum("jgchp,gh->jgcp", dbd.reshape(NJ, gl, GC, gl, P), jnp.eye(gl, dtype=dbd.dtype))
    return d.transpose(0, 1, 3, 2).reshape(G, P, GC)


def _c_blockdiag(cmat):
    gl = G // NJ
    cc = cmat.reshape(NJ, gl, GC, P).transpose(0, 1, 3, 2)
    return jnp.einsum("jgpc,gh->jgphc", cc, jnp.eye(gl, dtype=cmat.dtype)).reshape(NJ, gl * P, gl * GC)


def _c_blockdiag_t(dbd):
    gl = G // NJ
    d = jnp.einsum("jgphc,gh->jgpc", dbd.reshape(NJ, gl, P, gl, GC), jnp.eye(gl, dtype=dbd.dtype))
    return d.transpose(0, 1, 3, 2).reshape(G, GC, P)


def _pad_heads_cols(w, width):
    r = w.shape[0]
    return jnp.pad(w.reshape(r, HEADS, width), ((0, 0), (0, 0), (0, HP - width))).reshape(r, HEADS * HP)


def _unpad_heads_cols(w, width):
    r = w.shape[0]
    return w.reshape(r, HEADS, HP)[:, :, :width].reshape(r, HEADS * width)


W_IN_SPLIT = (QR + KVR, QR + KVR + ROPE)


def _w_in_layout(w):
    a, b = W_IN_SPLIT
    kr = jnp.pad(w[:, a:b], ((0, 0), (NOPE, HP - QK)))
    return jnp.concatenate([w[:, :a], w[:, b:], kr], axis=1)


def _w_in_layout_t(dw):
    a, b = W_IN_SPLIT
    n = dw.shape[1] - HP
    return jnp.concatenate([dw[:, :a], dw[:, n + NOPE:n + QK], dw[:, a:n]], axis=1)


def _ffn_fwd(x, l, wt, name):
    h = rms_fwd(x, wt["ffn_norm"][l][None], name=f"{name}_norm")
    up = mm(h, wt["ffn_w_up"][l], name=f"{name}_up")
    act = ffnact_fwd(up, wt["ffn_conv_w"][l], name=f"{name}_act")
    out = mm(act, wt["ffn_w_down"][l], add=x, name=f"{name}_down")
    return out, (x, h, up, act)


def _ffn_bwd(dout, douth, saved, l, wt, name):
    x, h, up, act = saved
    g = {}
    dact = mm(douth, wt["ffn_w_down"][l], tb=True, name=f"{name}_ddown")
    g["ffn_w_down"] = mm(act, douth, ta=True, out_dtype=BF16, name=f"{name}_dwdown")
    dup, g["ffn_conv_w"] = ffnact_bwd(up, wt["ffn_conv_w"][l], dact, name=f"{name}_dact")
    g["ffn_w_up"] = mm(h, dup, ta=True, out_dtype=BF16, name=f"{name}_dwup")
    dh = mm(dup, wt["ffn_w_up"][l], tb=True, name=f"{name}_dup")
    dx, dxh, dg = rms_bwd(x, wt["ffn_norm"][l][None], dh, add=dout, twin=True, name=f"{name}_dnorm")
    g["ffn_norm"] = dg[0]
    return dx, dxh, g


def _even_fwd(x, i, wt, tabs, name):
    h = rms_fwd(x, wt["attn_norm"][i][None], name=f"{name}_norm")
    proj = mm(h, wt["w_in2"][i], name=f"{name}_in")
    cqn = rms_fwd(proj, wt["cq_norm"][i][None], col=0, name=f"{name}_cqnorm")
    ckvn = rms_fwd(proj, wt["ckv_norm"][i][None], col=1, name=f"{name}_ckvnorm")
    qraw = mm(cqn, wt["w_uq_p"][i], name=f"{name}_uq")
    kv = mm(ckvn, wt["w_ukv_p"][i], name=f"{name}_ukv")
    q, k, v = qkprep_fwd(qraw, kv, proj, wt["q_gain_p"][i], wt["k_gain_p"][i], tabs, name=f"{name}_qkprep")
    o, oh, lse, lset = attn_fwd(q, k, v, name=f"{name}_attn")
    conv = sconv_fwd(proj, wt["sconv_w"][i], name=f"{name}_sconv")
    t = mm(oh, wt["w_out_a"][i], add=x, name=f"{name}_outa")
    out = mm(conv, wt["w_out_c"][i], add=t, name=f"{name}_outc")
    return out, (x, h, proj, cqn, ckvn, qraw, kv, q, k, v, o, oh, lse, lset, conv)


def _even_bwd(dout, douth, saved, i, wt, tabs, name):
    x, h, proj, cqn, ckvn, qraw, kv, q, k, v, o, oh, lse, lset, conv = saved
    g = {}
    do = mm(douth, wt["w_out_a"][i], tb=True, name=f"{name}_douta")
    dconv = mm(douth, wt["w_out_c"][i], tb=True, name=f"{name}_doutc")
    g["w_out_a"] = mm(oh, douth, ta=True, out_dtype=BF16, name=f"{name}_dwouta")
    g["w_out_c"] = mm(conv, douth, ta=True, out_dtype=BF16, name=f"{name}_dwoutc")
    dgates, g["sconv_w"] = sconv_bwd(proj, wt["sconv_w"][i], dconv, name=f"{name}_dsconv")
    dq, doh, deltat = attn_bwd_dq(q, k, v, o, lse, do, name=f"{name}_dattn_q")
    dk, dv = attn_bwd_dkv(q, k, v, doh, lset, deltat, name=f"{name}_dattn_kv")
    dqraw, dkraw, dkrope, dqg, dkg = qkprep_bwd(qraw, kv, proj, wt["q_gain_p"][i], wt["k_gain_p"][i], tabs, dq, dk,
                                                name=f"{name}_dqkprep")
    g["q_gain"], g["k_gain"] = dqg[0, :QK], dkg[0, :QK]
    dcqn = mm(dqraw, wt["w_uq_p"][i], tb=True, name=f"{name}_duq")
    g["w_uq_p"] = mm(cqn, dqraw, ta=True, out_dtype=BF16, name=f"{name}_dwuq")
    dkv = jnp.concatenate([dkraw, dv], axis=1)
    dckvn = mm(dkv, wt["w_ukv_p"][i], tb=True, name=f"{name}_dukv")
    g["w_ukv_p"] = mm(ckvn, dkv, ta=True, out_dtype=BF16, name=f"{name}_dwukv")
    dcq, dgq = rms_bwd(proj, wt["cq_norm"][i][None], dcqn, col=0, out_dtype=BF16, name=f"{name}_dcqnorm")
    dckv, dgkv = rms_bwd(proj, wt["ckv_norm"][i][None], dckvn, col=1, out_dtype=BF16, name=f"{name}_dckvnorm")
    g["cq_norm"], g["ckv_norm"] = dgq[0], dgkv[0]
    dproj = jnp.concatenate([dcq, dckv, dgates, dkrope.astype(BF16)], axis=1)
    g["w_in2"] = mm(h, dproj, ta=True, out_dtype=BF16, name=f"{name}_dwin")
    dh = mm(dproj, wt["w_in2"][i], tb=True, name=f"{name}_din")
    dx, dxh, dg = rms_bwd(x, wt["attn_norm"][i][None], dh, add=dout, twin=True, name=f"{name}_dnorm")
    g["attn_norm"] = dg[0]
    return dx, dxh, g


def _odd_fwd(x, i, wt, name):
    h = rms_fwd(x, wt["ssm_norm"][i][None], name=f"{name}_norm")
    u = mm(h, wt["ssm_w_in"][i], name=f"{name}_in")
    y, st_r, st_i = s5_fwd(u, wt["bbd_r"][i], wt["bbd_i"][i], wt["cbd_r"][i], wt["cbd_i"][i], wt["tab_f"][i],
                           name=f"{name}_scan")
    gl = s5post_fwd(y, u, wt["d_skip"][i][None], name=f"{name}_gelu")
    glu = mm(gl, wt["w_glu"][i], name=f"{name}_glu")
    out = glu_fwd(glu, x, name=f"{name}_gate")
    return out, (x, h, u, y, st_r, st_i, gl, glu)


def _odd_bwd(dout, douth, saved, i, wt, name):
    x, h, u, y, st_r, st_i, gl, glu = saved
    g = {}
    dglu = glu_bwd(glu, dout, name=f"{name}_dgate")
    g["w_glu"] = mm(gl, dglu, ta=True, out_dtype=BF16, name=f"{name}_dwglu")
    dgl = mm(dglu, wt["w_glu"][i], tb=True, name=f"{name}_dglu")
    dz, dd = s5post_bwd(y, u, wt["d_skip"][i][None], dgl, name=f"{name}_dgelu")
    g["d_skip"] = dd[0]
    du, g["bbd_r"], g["bbd_i"], g["cbd_r"], g["cbd_i"], g["a"] = s5_bwd(
        u, dz, wt["d_skip"][i][None], st_r, st_i, wt["bbd_r"][i], wt["bbd_i"][i], wt["cbd_r"][i], wt["cbd_i"][i],
        wt["tab_r"][i], name=f"{name}_dscan")
    g["ssm_w_in"] = mm(h, du, ta=True, out_dtype=BF16, name=f"{name}_dwin")
    dh = mm(du, wt["ssm_w_in"][i], tb=True, name=f"{name}_din")
    dx, dxh, dg = rms_bwd(x, wt["ssm_norm"][i][None], dh, add=dout, twin=True, name=f"{name}_dnorm")
    g["ssm_norm"] = dg[0]
    return dx, dxh, g


MATMUL_WEIGHTS = {"even": ("mix_w_in", "w_uq", "w_ukv", "mix_w_out"), "odd": ("ssm_w_in", "w_glu"),
                  "ffn": ("ffn_w_up", "ffn_w_down")}
ODD_SMALL = ("lambda_re", "lambda_im", "log_step", "b_re", "b_im", "c_re", "c_im", "ssm_norm", "d_skip")


def _even_layouts(fw, wt, i):
    wt["w_in2"][i] = _w_in_layout(fw["mix_w_in"])
    wt["w_uq_p"][i] = _pad_heads_cols(fw["w_uq"], QK)
    ukv = fw["w_ukv"].reshape(KVR, HEADS, NOPE + VD)
    wt["w_ukv_p"][i] = jnp.concatenate(
        [_pad_heads_cols(ukv[:, :, :NOPE].reshape(KVR, HEADS * NOPE), NOPE),
         _pad_heads_cols(ukv[:, :, NOPE:].reshape(KVR, HEADS * VD), VD)], axis=1)
    wt["w_out_a"][i] = _pad_heads_cols(fw["mix_w_out"][:HEADS * VD].T, VD).T
    wt["w_out_c"][i] = fw["mix_w_out"][HEADS * VD:]


def _even_layouts_t(g):
    dk_, dv_ = g["w_ukv_p"][:, :HEADS * HP], g["w_ukv_p"][:, HEADS * HP:]
    return {"mix_w_in": _w_in_layout_t(g["w_in2"]), "w_uq": _unpad_heads_cols(g["w_uq_p"], QK),
            "w_ukv": jnp.concatenate([dk_.reshape(KVR, HEADS, HP)[:, :, :NOPE], dv_.reshape(KVR, HEADS, HP)[:, :, :VD]],
                                     axis=2).reshape(KVR, HEADS * (NOPE + VD)),
            "mix_w_out": jnp.concatenate([_unpad_heads_cols(g["w_out_a"].T, VD).T, g["w_out_c"]], axis=0)}


def _local_step(x, target, full, getw, putg):
    s = x.shape[0]
    n_even = (DEPTH + 1) // 2
    n_odd = DEPTH // 2
    tabs = _rope_tables(s)
    wt = dict(full)
    for key in ("w_in2", "w_uq_p", "w_ukv_p", "w_out_a", "w_out_c") + sum(MATMUL_WEIGHTS.values(), ()):
        wt[key] = {}
    wt["q_gain_p"] = jnp.pad(full["q_gain"], ((0, 0), (0, HP - QK)))[:, None, :]
    wt["k_gain_p"] = jnp.pad(full["k_gain"], ((0, 0), (0, HP - QK)))[:, None, :]

    disc_vjp = []
    for key in ("bbd_r", "bbd_i", "cbd_r", "cbd_i", "tab_f", "tab_r"):
        wt[key] = []
    for i in range(n_odd):
        (ar, ai, bbr, bbi), vjp = jax.vjp(_discretize, full["lambda_re"][i], full["lambda_im"][i], full["log_step"][i],
                                          full["b_re"][i], full["b_im"][i])
        disc_vjp.append(vjp)
        tf, tr = _scan_tables(ar.reshape(-1), ai.reshape(-1))
        wt["tab_f"].append(tf)
        wt["tab_r"].append(tr)
        wt["bbd_r"].append(_b_blockdiag(bbr).astype(BF16))
        wt["bbd_i"].append(_b_blockdiag(bbi).astype(BF16))
        wt["cbd_r"].append(_c_blockdiag(full["c_re"][i]).astype(BF16))
        wt["cbd_i"].append(_c_blockdiag(full["c_im"][i]).astype(BF16))

    saved = []
    for layer in range(DEPTH):
        i = layer // 2
        if layer % 2 == 0:
            fw, tok = getw("even", i, x)
            wt["attn_norm"] = full["attn_norm"] + tok
            _even_layouts(fw, wt, i)
            x, sm = _even_fwd(x, i, wt, tabs, f"l{layer}_mla")
        else:
            fw, tok = getw("odd", i, x)
            wt["ssm_norm"] = full["ssm_norm"] + tok
            for n, a in fw.items():
                wt[n][i] = a
            x, sm = _odd_fwd(x, i, wt, f"l{layer}_s5")
        fw, tok = getw("ffn", layer, x)
        wt["ffn_norm"] = full["ffn_norm"] + tok
        for n, a in fw.items():
            wt[n][layer] = a
        x, sf = _ffn_fwd(x, layer, wt, f"l{layer}_ffn")
        saved.append((sm, sf))
    dx, dxh, lslab = loss_head(x, target, name="loss_head")

    own = [n for n in ORDER if n not in sum(MATMUL_WEIGHTS.values(), ())]
    grads = {n: [None] * (DEPTH if n.startswith("ffn") else n_even) for n in own}
    tok = 0.0
    for layer in reversed(range(DEPTH)):
        i = layer // 2
        sm, sf = saved[layer]
        wt["ffn_conv_w"] = full["ffn_conv_w"] + tok
        dx, dxh, g = _ffn_bwd(dx, dxh, sf, layer, wt, f"l{layer}_ffn")
        tok = putg("ffn", layer, {n: g[n] for n in MATMUL_WEIGHTS["ffn"]})
        for n in ("ffn_norm", "ffn_conv_w"):
            grads[n][layer] = g[n]
        if layer % 2 == 0:
            wt["sconv_w"] = full["sconv_w"] + tok
            dx, dxh, g = _even_bwd(dx, dxh, sm, i, wt, tabs, f"l{layer}_mla")
            tok = putg("even", i, _even_layouts_t(g))
            for n in ("attn_norm", "cq_norm", "ckv_norm", "q_gain", "k_gain", "sconv_w"):
                grads[n][i] = g[n]
        else:
            wt["d_skip"] = full["d_skip"] + tok
            dx, dxh, g = _odd_bwd(dx, dxh, sm, i, wt, f"l{layer}_s5")
            tok = putg("odd", i, {n: g[n] for n in MATMUL_WEIGHTS["odd"]})
            dlr, dli, dls, dbr, dbi = disc_vjp[i]((g["a"][0].reshape(G, P), g["a"][1].reshape(G, P),
                                                    _b_blockdiag_t(g["bbd_r"]), _b_blockdiag_t(g["bbd_i"])))
            grads["lambda_re"][i], grads["lambda_im"][i], grads["log_step"][i] = dlr, dli, dls
            grads["b_re"][i], grads["b_im"][i] = dbr, dbi
            grads["c_re"][i], grads["c_im"][i] = _c_blockdiag_t(g["cbd_r"]), _c_blockdiag_t(g["cbd_i"])
            for n in ("ssm_norm", "d_skip"):
                grads[n][i] = g[n]
            if i == 0:
                tok = tok + putg("odd_small", 0, {n: jnp.stack(grads[n]) for n in ODD_SMALL})
    grads = {n: jnp.stack(v) for n, v in grads.items()}
    return jnp.sum(lslab), dx, grads


def kernel(x, attn_norm, mix_w_in, cq_norm, ckv_norm, w_uq, w_ukv, q_gain, k_gain, sconv_w, mix_w_out, ssm_norm, ssm_w_in, lambda_re, lambda_im, log_step, b_re, b_im, c_re, c_im, d_skip, w_glu, ffn_norm, ffn_w_up, ffn_conv_w, ffn_w_down, loss_target, m_attn_norm, m_mix_w_in, m_cq_norm, m_ckv_norm, m_w_uq, m_w_ukv, m_q_gain, m_k_gain, m_sconv_w, m_mix_w_out, m_ssm_norm, m_ssm_w_in, m_lambda_re, m_lambda_im, m_log_step, m_b_re, m_b_im, m_c_re, m_c_im, m_d_skip, m_w_glu, m_ffn_norm, m_ffn_w_up, m_ffn_conv_w, m_ffn_w_down, v_attn_norm, v_mix_w_in, v_cq_norm, v_ckv_norm, v_w_uq, v_w_ukv, v_q_gain, v_k_gain, v_sconv_w, v_mix_w_out, v_ssm_norm, v_ssm_w_in, v_lambda_re, v_lambda_im, v_log_step, v_b_re, v_b_im, v_c_re, v_c_im, v_d_skip, v_w_glu, v_ffn_norm, v_ffn_w_up, v_ffn_conv_w, v_ffn_w_down):
    args = dict(locals())
    w = {n: args[n] for n in ORDER}
    m = {n: args["m_" + n] for n in ORDER}
    v = {n: args["v_" + n] for n in ORDER}
    me = 2 * lax.axis_index("x") + lax.axis_index("y")

    axis = dict(BIG)

    def landing(own):
        return lax.dynamic_update_index_in_dim(lax.empty((NCHIP,) + own.shape, own.dtype), own, me, 0)

    gs = _chip_exchange([w[n] for n, _ in SMALL], False, "gather_w_f32")
    full = {n: w[n] for n in REPL}
    for (n, ax), g in zip(SMALL, gs):
        full[n] = _join(g, ax)
    parts = [(("even", "odd")[layer % 2], layer // 2) for layer in range(DEPTH)]
    parts = [p for layer, mixer in enumerate(parts) for p in (mixer, ("ffn", layer))]
    gathers = {}

    def start_gather(kind, idx, zero):
        shards = [(w[n][idx] + zero).astype(BF16) for n in MATMUL_WEIGHTS[kind]]
        gathers[kind, idx] = exchange_start(shards, [landing(sh) for sh in shards], False, f"gather_start_{kind}{idx}")

    start_gather(*parts[0], 0.0 * gs[0].reshape(-1)[0])

    def getw(kind, idx, after):
        ssem, rsem, bufs, lands, _ = gathers[kind, idx]
        got = exchange_wait(ssem, rsem, bufs, lands, after, False, f"gather_wait_{kind}{idx}")
        nxt = parts.index((kind, idx)) + 1
        tok = 0.0
        if nxt < len(parts):
            start_gather(*parts[nxt], 0.0 * got[0].reshape(-1)[0].astype(F32))
            tok = gathers[parts[nxt]][4]
        return {n: _join(g, axis[n] - 1) for n, g in zip(MATMUL_WEIGHTS[kind], got)}, tok

    scatters, early = [], []

    def putg(kind, idx, g):
        if kind == "odd_small":
            arrs = [_rows2d(g[n]) for n in ODD_SMALL]
            early.append(exchange_start(arrs, [landing(a) for a in arrs], False, "gather_start_g_odd"))
            return early[0][4]
        blocks = [_split(g[n], axis[n] - 1) for n in MATMUL_WEIGHTS[kind]]
        lands = [landing(lax.dynamic_index_in_dim(b, me, 0, keepdims=False)) for b in blocks]
        scatters.append((kind, idx, exchange_start(blocks, lands, True, f"scatter_start_{kind}{idx}")))
        return scatters[-1][2][4]

    sq, dx, grads = _local_step(x[0], loss_target[0], full, getw, putg)
    loss = lax.psum(0.5 * sq / D_MODEL, ("x", "y", "c"))

    late_names = [n for n in REPL + [n for n, _ in SMALL] if n not in ODD_SMALL]
    rep_names = list(ODD_SMALL) + late_names
    names = [n for n, _ in BIG] + rep_names
    late = [_rows2d(grads[n]) for n in late_names]
    ssem_s, rsem_s, bufs_s, lands_s, _ = exchange_start(late, [landing(g) for g in late], False, "gather_start_g_f32")
    summed = {}
    for kind, idx, (ssem, rsem, bufs, lands, _) in scatters:
        got = exchange_wait(ssem, rsem, bufs, lands, dx, True, f"scatter_wait_{kind}{idx}")
        for n, sl in zip(MATMUL_WEIGHTS[kind], got):
            summed[n, idx] = sum_slots(_rows2d(sl, 1), name=f"sum_{n}{idx}")
    mine = [jnp.concatenate([summed[n, idx] for idx in range(w[n].shape[0])], axis=0) for n, _ in BIG]
    ssem_e, rsem_e, bufs_e, lands_e, _ = early[0]
    slots = (exchange_wait(ssem_e, rsem_e, bufs_e, lands_e, dx, False, "gather_wait_g_odd")
             + exchange_wait(ssem_s, rsem_s, bufs_s, lands_s, mine[-1], False, "gather_wait_g_f32"))
    mine += [sum_slots(_rows2d(sl, 1), name=f"sum_{n}") for n, sl in zip(rep_names, slots)]
    other = sibling_swap(mine, "swap_g")

    def local(n, p):
        ax = dict(SMALL).get(n)
        if ax is None:
            return p
        part = lax.dynamic_index_in_dim(_split(p.reshape(grads[n].shape), ax), me, 0, keepdims=False)
        return _rows2d(part)

    outs = {}
    for n, p, q in zip(names, mine, other):
        res = adamw(local(n, p), local(n, q), _rows2d(w[n]), _rows2d(m[n]), _rows2d(v[n]), name=f"adamw_{n}")
        outs[n] = [r.reshape(w[n].shape) for r in res]
    return (loss, dx[None], *[outs[n][0] for n in ORDER], *[outs[n][1] for n in ORDER],
            *[outs[n][2] for n in ORDER], *[outs[n][3] for n in ORDER])
```

```python
import functools
import math

import numpy as np
import jax
import jax.numpy as jnp
from jax import lax
from jax.experimental import pallas as pl
from jax.experimental.pallas import tpu as pltpu

F32, BF16 = jnp.float32, jnp.bfloat16

D_MODEL = 1024
DEPTH = 4
HEADS = 8
NOPE, ROPE, QK, VD = 64, 32, 96, 64
HP = 128
QR, KVR = 256, 256
CONVC = 512
FFN_H = 2816
G, P, GC = 64, 64, 16
NST = G * P
SLAB = 8
LANE = 128
EPS = 1e-6
ROPE_THETA = 10000.0
ADAM_LR, ADAM_B1, ADAM_B2, ADAM_EPS, ADAM_WD, ADAM_STEP = 0.001, 0.9, 0.999, 1e-08, 0.01, 10
VMEM_LIMIT = 48 * 1024 * 1024
MESH = pl.DeviceIdType.MESH
ANY = pl.BlockSpec(memory_space=pl.ANY)


def _cparams(*sem):
    return pltpu.CompilerParams(dimension_semantics=sem, vmem_limit_bytes=VMEM_LIMIT)


def _pick(dim, prefs):
    for p in prefs:
        if dim % p == 0:
            return p
    return dim


def _rows(s):
    return _pick(s, (512, 256, 128, 64, 32, 16, 8))


MM_VMEM_BUDGET = 36 * 1024 * 1024
MM_MAX_TILE_ELEMS = 640 * 1024
HBM_BYTES_PER_US = 3.0e6
STEP_OVERHEAD_US = 0.35


def _lane_tiles(n):
    c = {t for t in range(LANE, min(n, 1536) + 1, LANE) if n % t == 0}
    if n <= 2304 or not c:
        c.add(n)
    return sorted(c, reverse=True)


def _mm_tiles(m, n, k, sa, sb, so):
    best = None
    for tm in [t for t in (1024, 512, 256) if m % t == 0] or [m]:
        for tn in _lane_tiles(n):
            if tm * tn > MM_MAX_TILE_ELEMS:
                continue
            if 2 * (tm * k * sa + k * tn * sb + tm * tn * so) + 4 * tm * tn > MM_VMEM_BUDGET:
                continue
            steps = (m // tm) * (n // tn)
            for inner_n in (True, False):
                moved = (m * k * sa + (m // tm) * k * n * sb) if inner_n else (k * n * sb + (n // tn) * m * k * sa)
                cost = (moved + m * n * so) / HBM_BYTES_PER_US + steps * STEP_OVERHEAD_US
                if best is None or cost < best[0]:
                    best = (cost, tm, tn, inner_n)
    return best[1:]


def mm(a, b, *, ta=False, tb=False, add=None, out_dtype=F32, name):
    if ta:
        kdim, m = a.shape
    else:
        m, kdim = a.shape
    n = b.shape[0] if tb else b.shape[1]
    so = jnp.dtype(out_dtype).itemsize + (0 if add is None else add.dtype.itemsize)
    tm, tn, inner_n = _mm_tiles(m, n, kdim, a.dtype.itemsize, b.dtype.itemsize, so)
    dn = (((0 if ta else 1,), (1 if tb else 0,)), ((), ()))

    def body(*refs):
        if add is None:
            a_ref, b_ref, o_ref = refs
        else:
            a_ref, b_ref, add_ref, o_ref = refs
        r = lax.dot_general(a_ref[...].astype(BF16), b_ref[...].astype(BF16), dn, preferred_element_type=F32)
        if add is not None:
            r = r + add_ref[...].astype(F32)
        o_ref[...] = r.astype(out_dtype)

    ij = (lambda g0, g1: (g0, g1)) if inner_n else (lambda g0, g1: (g1, g0))
    a_spec = (pl.BlockSpec((kdim, tm), lambda g0, g1: (0, ij(g0, g1)[0])) if ta
              else pl.BlockSpec((tm, kdim), lambda g0, g1: (ij(g0, g1)[0], 0)))
    b_spec = (pl.BlockSpec((tn, kdim), lambda g0, g1: (ij(g0, g1)[1], 0)) if tb
              else pl.BlockSpec((kdim, tn), lambda g0, g1: (0, ij(g0, g1)[1])))
    o_spec = pl.BlockSpec((tm, tn), lambda g0, g1: ij(g0, g1))
    ins, specs = [a, b], [a_spec, b_spec]
    if add is not None:
        ins.append(add)
        specs.append(o_spec)
    grid = (m // tm, n // tn) if inner_n else (n // tn, m // tm)
    return pl.pallas_call(
        body, name=name, grid=grid, in_specs=specs, out_specs=o_spec,
        out_shape=jax.ShapeDtypeStruct((m, n), out_dtype),
        compiler_params=_cparams("parallel", "parallel"))(*ins)


def rms_fwd(x, g, *, col=0, out_dtype=BF16, name):
    s = x.shape[0]
    d = g.shape[1]
    tm = _rows(s)

    def body(x_ref, g_ref, o_ref):
        xv = x_ref[...]
        r = lax.rsqrt(jnp.mean(xv * xv, axis=-1, keepdims=True) + EPS)
        o_ref[...] = (xv * r * g_ref[...]).astype(out_dtype)

    return pl.pallas_call(
        body, name=name, grid=(s // tm,),
        in_specs=[pl.BlockSpec((tm, d), lambda i: (i, col)), pl.BlockSpec((1, d), lambda i: (0, 0))],
        out_specs=pl.BlockSpec((tm, d), lambda i: (i, 0)),
        out_shape=jax.ShapeDtypeStruct((s, d), out_dtype),
        compiler_params=_cparams("parallel"))(x, g)


def rms_bwd(x, g, dy, *, col=0, add=None, out_dtype=F32, twin=False, name):
    s = x.shape[0]
    d = g.shape[1]
    tm = _rows(s)

    def body(*refs):
        refs = list(refs)
        dg_ref = refs.pop()
        dxh_ref = refs.pop() if twin else None
        dx_ref = refs.pop()
        add_ref = refs.pop() if add is not None else None
        x_ref, g_ref, dy_ref = refs

        @pl.when(pl.program_id(0) == 0)
        def _():
            dg_ref[...] = jnp.zeros_like(dg_ref)

        xv = x_ref[...]
        dyv = dy_ref[...].astype(F32)
        r = lax.rsqrt(jnp.mean(xv * xv, axis=-1, keepdims=True) + EPS)
        xh = xv * r
        dg_ref[...] += jnp.sum(dyv * xh, axis=0, keepdims=True)
        dxh = dyv * g_ref[...]
        dx = r * (dxh - xh * jnp.mean(dxh * xh, axis=-1, keepdims=True))
        if add is not None:
            dx = dx + add_ref[...]
        dx_ref[...] = dx.astype(out_dtype)
        if twin:
            dxh_ref[...] = dx.astype(BF16)

    row = pl.BlockSpec((tm, d), lambda i: (i, 0))
    vec = pl.BlockSpec((1, d), lambda i: (0, 0))
    ins = [x, g, dy]
    specs = [pl.BlockSpec((tm, d), lambda i: (i, col)), vec, row]
    if add is not None:
        ins.append(add)
        specs.append(row)
    dxs = [jax.ShapeDtypeStruct((s, d), out_dtype)] + ([jax.ShapeDtypeStruct((s, d), BF16)] if twin else [])
    return pl.pallas_call(
        body, name=name, grid=(s // tm,), in_specs=specs,
        out_specs=[row] * len(dxs) + [vec],
        out_shape=dxs + [jax.ShapeDtypeStruct((1, d), F32)],
        compiler_params=_cparams("arbitrary"))(*ins)


def _rope_tables(s):
    inv = 1.0 / (ROPE_THETA ** (jnp.arange(0, ROPE, 2, dtype=F32) / ROPE))
    ang = jnp.arange(s, dtype=F32)[:, None] * inv[None, :]
    cos, sin = jnp.cos(ang), jnp.sin(ang)
    z = lambda w: jnp.zeros((s, w), F32)
    c = jnp.concatenate([jnp.ones((s, NOPE), F32), cos, cos, z(HP - QK)], axis=1)
    s1 = jnp.concatenate([z(NOPE), -sin, z(HP - NOPE - ROPE // 2)], axis=1)
    s2 = jnp.concatenate([z(NOPE + ROPE // 2), sin, z(HP - QK)], axis=1)
    return c, s1, s2


def qkprep_fwd(qraw, kv, proj, qg, kg, tabs, *, name):
    s = qraw.shape[0]
    tm = _rows(s)
    kr_col = (proj.shape[1] - HP) // HP

    def body(q_ref, k_ref, v_ref, kr_ref, qg_ref, kg_ref, c_ref, s1_ref, s2_ref, qo_ref, ko_ref, vo_ref):
        c, s1, s2 = c_ref[...], s1_ref[...], s2_ref[...]

        def f(xv, gain):
            r = lax.rsqrt(jnp.sum(xv * xv, axis=-1, keepdims=True) * (1.0 / QK) + EPS)
            xn = xv * r * gain
            return xn * c + pltpu.roll(xn, HP - ROPE // 2, 1) * s1 + pltpu.roll(xn, ROPE // 2, 1) * s2

        qo_ref[...] = f(q_ref[...], qg_ref[...]).astype(BF16)
        ko_ref[...] = f(k_ref[...] + kr_ref[...], kg_ref[...]).astype(BF16)
        vv = v_ref[...]
        lane = lax.broadcasted_iota(jnp.int32, vv.shape, 1)
        vo_ref[...] = jnp.where(lane == VD, 1.0, vv).astype(BF16)

    head = pl.BlockSpec((tm, HP), lambda i, h: (i, h))
    tab = pl.BlockSpec((tm, HP), lambda i, h: (i, 0))
    gain = pl.BlockSpec((1, HP), lambda i, h: (0, 0))
    return pl.pallas_call(
        body, name=name, grid=(s // tm, HEADS),
        in_specs=[head, head, pl.BlockSpec((tm, HP), lambda i, h: (i, HEADS + h)),
                  pl.BlockSpec((tm, HP), lambda i, h: (i, kr_col)), gain, gain, tab, tab, tab],
        out_specs=[head, head, head],
        out_shape=[jax.ShapeDtypeStruct((s, HEADS * HP), BF16)] * 3,
        compiler_params=_cparams("parallel", "parallel"))(qraw, kv, kv, proj, qg, kg, *tabs)


def qkprep_bwd(qraw, kv, proj, qg, kg, tabs, dq, dk, *, name):
    s = qraw.shape[0]
    tm = _rows(s)
    kr_col = (proj.shape[1] - HP) // HP

    def body(q_ref, k_ref, kr_ref, qg_ref, kg_ref, c_ref, s1_ref, s2_ref, dq_ref, dk_ref,
             dqr_ref, dkr_ref, dkrope_ref, dqg_ref, dkg_ref):
        i, h = pl.program_id(0), pl.program_id(1)
        c, s1, s2 = c_ref[...], s1_ref[...], s2_ref[...]

        @pl.when((i == 0) & (h == 0))
        def _():
            dqg_ref[...] = jnp.zeros_like(dqg_ref)
            dkg_ref[...] = jnp.zeros_like(dkg_ref)

        @pl.when(h == 0)
        def _():
            dkrope_ref[...] = jnp.zeros_like(dkrope_ref)

        def f(xv, gain, dout):
            r = lax.rsqrt(jnp.sum(xv * xv, axis=-1, keepdims=True) * (1.0 / QK) + EPS)
            xh = xv * r
            dxn = dout * c + pltpu.roll(dout * s1, ROPE // 2, 1) + pltpu.roll(dout * s2, HP - ROPE // 2, 1)
            dgain = jnp.sum(dxn * xh, axis=0, keepdims=True)
            dxh = dxn * gain
            dx = r * (dxh - xh * (jnp.sum(dxh * xh, axis=-1, keepdims=True) * (1.0 / QK)))
            return dx, dgain

        dxq, dgq = f(q_ref[...], qg_ref[...], dq_ref[...])
        dxk, dgk = f(k_ref[...] + kr_ref[...], kg_ref[...], dk_ref[...])
        dqr_ref[...] = dxq.astype(BF16)
        dkr_ref[...] = dxk.astype(BF16)
        dqg_ref[...] += dgq
        dkg_ref[...] += dgk
        lane = lax.broadcasted_iota(jnp.int32, dxk.shape, 1)
        dkrope_ref[...] += jnp.where((lane >= NOPE) & (lane < QK), dxk, 0.0)

    head = pl.BlockSpec((tm, HP), lambda i, h: (i, h))
    tab = pl.BlockSpec((tm, HP), lambda i, h: (i, 0))
    gain = pl.BlockSpec((1, HP), lambda i, h: (0, 0))
    return pl.pallas_call(
        body, name=name, grid=(s // tm, HEADS),
        in_specs=[head, head, pl.BlockSpec((tm, HP), lambda i, h: (i, kr_col)), gain, gain, tab, tab, tab, head, head],
        out_specs=[head, head, tab, gain, gain],
        out_shape=[jax.ShapeDtypeStruct((s, HEADS * HP), BF16)] * 2
        + [jax.ShapeDtypeStruct((s, HP), F32), jax.ShapeDtypeStruct((1, HP), F32), jax.ShapeDtypeStruct((1, HP), F32)],
        compiler_params=_cparams("arbitrary", "arbitrary"))(qraw, kv, proj, qg, kg, *tabs, dq, dk)


ATT_SCALE = QK ** -0.5
NEG = -1e30


def _att_tile(s):
    return _pick(s, (512, 256, 128))


def _causal(sv, diag):
    r = lax.broadcasted_iota(jnp.int32, sv.shape, 0)
    c = lax.broadcasted_iota(jnp.int32, sv.shape, 1)
    return jnp.where(diag & (c > r), NEG, sv)


NT = (((1,), (1,)), ((), ()))
TN = (((0,), (0,)), ((), ()))


def _row_of(col):
    return jnp.broadcast_to(col, (col.shape[0], LANE)).T[0:SLAB, :]


def _att_specs(s):
    t = _att_tile(s)
    nb = s // t
    tile = pl.BlockSpec((t, HP), lambda h, i: (i, h))
    whole = pl.BlockSpec((s, HP), lambda h, i: (0, h))
    row = pl.BlockSpec((1, 1, SLAB, t), lambda h, i: (h, i, 0, 0))
    rows = pl.BlockSpec((1, nb, SLAB, t), lambda h, i: (h, 0, 0, 0))
    return t, nb, tile, whole, row, rows


def attn_fwd(q, k, v, *, name):
    s = q.shape[0]
    t, nb, tile, whole, row, _ = _att_specs(s)

    def body(q_ref, k_ref, v_ref, o_ref, oh_ref, lse_ref, lset_ref, s_scr, mb_scr, acc):
        qb = pl.program_id(1)
        qv = q_ref[...]

        def scores(j):
            r0 = pl.multiple_of(j * t, t)
            return lax.dot_general(qv, k_ref[pl.ds(r0, t), :], NT, preferred_element_type=F32) * ATT_SCALE

        def fold(sv):
            m = sv[:, 0:LANE]
            for kk in range(1, t // LANE):
                m = jnp.maximum(m, sv[:, kk * LANE:(kk + 1) * LANE])
            return m

        def first(j, m):
            sv = scores(j)
            s_scr[j] = sv
            return jnp.maximum(m, fold(sv))

        m = lax.fori_loop(0, qb, first, jnp.full((t, LANE), NEG, F32))
        sd = _causal(scores(qb), True)
        s_scr[qb] = sd
        mcol = jnp.max(jnp.maximum(m, fold(sd)), axis=-1, keepdims=True)
        mb_scr[...] = jnp.broadcast_to(mcol, (t, t))
        acc[...] = jnp.zeros_like(acc)

        def second(j, carry):
            r0 = pl.multiple_of(j * t, t)
            p = jnp.exp(s_scr[j] - mb_scr[...]).astype(BF16)
            acc[...] += jnp.dot(p, v_ref[pl.ds(r0, t), :], preferred_element_type=F32)
            return carry

        lax.fori_loop(0, qb + 1, second, 0)
        av = acc[...]
        lsum = av[:, VD:VD + 1]
        lane = lax.broadcasted_iota(jnp.int32, av.shape, 1)
        ov = jnp.where(lane == VD, 0.0, av / lsum)
        o_ref[...] = ov
        oh_ref[...] = ov.astype(BF16)
        lse = mcol + jnp.log(lsum)
        lse_ref[...] = jnp.broadcast_to(lse, lse_ref.shape)
        lset_ref[0, 0] = _row_of(lse)

    return pl.pallas_call(
        body, name=name, grid=(HEADS, nb), in_specs=[tile, whole, whole], out_specs=[tile, tile, tile, row],
        out_shape=[jax.ShapeDtypeStruct((s, HEADS * HP), F32), jax.ShapeDtypeStruct((s, HEADS * HP), BF16),
                   jax.ShapeDtypeStruct((s, HEADS * HP), F32), jax.ShapeDtypeStruct((HEADS, nb, SLAB, t), F32)],
        scratch_shapes=[pltpu.VMEM((nb, t, t), F32), pltpu.VMEM((t, t), F32), pltpu.VMEM((t, HP), F32)],
        compiler_params=_cparams("parallel", "parallel"))(q, k, v)


def attn_bwd_dq(q, k, v, o, lse, do, *, name):
    s = q.shape[0]
    t, nb, tile, whole, row, _ = _att_specs(s)

    def body(q_ref, k_ref, v_ref, o_ref, lse_ref, do_ref, dq_ref, doh_ref, dt_ref, lb_scr, db_scr, acc):
        qb = pl.program_id(1)
        qv = q_ref[...]
        dov = do_ref[...]
        dob = dov.astype(BF16)
        doh_ref[...] = dob
        delta = jnp.sum(dov * o_ref[...], axis=-1, keepdims=True)
        dt_ref[0, 0] = _row_of(delta)
        lb_scr[...] = jnp.broadcast_to(lse_ref[...][:, 0:1], (t, t))
        db_scr[...] = jnp.broadcast_to(delta, (t, t))
        acc[...] = jnp.zeros_like(acc)

        def step(j, diag):
            r0 = pl.multiple_of(j * t, t)
            kj = k_ref[pl.ds(r0, t), :]
            sv = lax.dot_general(qv, kj, NT, preferred_element_type=F32) * ATT_SCALE
            if diag:
                sv = _causal(sv, True)
            p = jnp.exp(sv - lb_scr[...])
            dp = lax.dot_general(dob, v_ref[pl.ds(r0, t), :], NT, preferred_element_type=F32)
            ds = (p * (dp - db_scr[...])).astype(BF16)
            acc[...] += jnp.dot(ds, kj, preferred_element_type=F32)

        def off_diag(j, carry):
            step(j, False)
            return carry

        lax.fori_loop(0, qb, off_diag, 0)
        step(qb, True)
        dq_ref[...] = acc[...] * ATT_SCALE

    return pl.pallas_call(
        body, name=name, grid=(HEADS, nb), in_specs=[tile, whole, whole, tile, tile, tile],
        out_specs=[tile, tile, row],
        out_shape=[jax.ShapeDtypeStruct((s, HEADS * HP), F32), jax.ShapeDtypeStruct((s, HEADS * HP), BF16),
                   jax.ShapeDtypeStruct((HEADS, nb, SLAB, t), F32)],
        scratch_shapes=[pltpu.VMEM((t, t), F32), pltpu.VMEM((t, t), F32), pltpu.VMEM((t, HP), F32)],
        compiler_params=_cparams("parallel", "parallel"))(q, k, v, o, lse, do)


def attn_bwd_dkv(q, k, v, doh, lset, deltat, *, name):
    s = q.shape[0]
    t, nb, tile, whole, _, rows = _att_specs(s)

    def body(q_ref, k_ref, v_ref, do_ref, lt_ref, dt_ref, dk_ref, dv_ref, dk_acc, dv_acc):
        kb = pl.program_id(1)
        kt, vt = k_ref[...], v_ref[...]
        dk_acc[...] = jnp.zeros_like(dk_acc)
        dv_acc[...] = jnp.zeros_like(dv_acc)

        def step(i, diag):
            r0 = pl.multiple_of(i * t, t)
            qi, doi = q_ref[pl.ds(r0, t), :], do_ref[pl.ds(r0, t), :]
            st = lax.dot_general(kt, qi, NT, preferred_element_type=F32) * ATT_SCALE
            if diag:
                kr = lax.broadcasted_iota(jnp.int32, st.shape, 0)
                qc = lax.broadcasted_iota(jnp.int32, st.shape, 1)
                st = jnp.where(kr > qc, NEG, st)
            pt = jnp.exp(st - lt_ref[0, i][0:1, :])
            dpt = lax.dot_general(vt, doi, NT, preferred_element_type=F32)
            dst = (pt * (dpt - dt_ref[0, i][0:1, :])).astype(BF16)
            dv_acc[...] += jnp.dot(pt.astype(BF16), doi, preferred_element_type=F32)
            dk_acc[...] += jnp.dot(dst, qi, preferred_element_type=F32)

        def off_diag(i, carry):
            step(i, False)
            return carry

        step(kb, True)
        lax.fori_loop(kb + 1, nb, off_diag, 0)
        dk_ref[...] = dk_acc[...] * ATT_SCALE
        dvv = dv_acc[...]
        lane = lax.broadcasted_iota(jnp.int32, dvv.shape, 1)
        dv_ref[...] = jnp.where(lane == VD, 0.0, dvv).astype(BF16)

    return pl.pallas_call(
        body, name=name, grid=(HEADS, nb), in_specs=[whole, tile, tile, whole, rows, rows], out_specs=[tile, tile],
        out_shape=[jax.ShapeDtypeStruct((s, HEADS * HP), F32), jax.ShapeDtypeStruct((s, HEADS * HP), BF16)],
        scratch_shapes=[pltpu.VMEM((t, HP), F32), pltpu.VMEM((t, HP), F32)],
        compiler_params=_cparams("parallel", "parallel"))(q, k, v, doh, lset, deltat)


HALO = 8
CW = 256


def _conv3(zw, w):
    return w[2:3] * zw + w[1:2] * pltpu.roll(zw, 1, 0) + w[0:1] * pltpu.roll(zw, 2, 0)


def _conv3_t(dc, w):
    n = dc.shape[0]
    return w[2:3] * dc + w[1:2] * pltpu.roll(dc, n - 1, 0) + w[0:1] * pltpu.roll(dc, n - 2, 0)


def _conv3_dw(dc, zw, r):
    z0 = zw[HALO:HALO + r]
    z1 = pltpu.roll(zw, 1, 0)[HALO:HALO + r]
    z2 = pltpu.roll(zw, 2, 0)[HALO:HALO + r]
    return [jnp.sum(dc * z, axis=0, keepdims=True) for z in (z2, z1, z0)]


def _halo_specs(r, colfn):
    rb = r // HALO
    cur = pl.BlockSpec((r, CW), lambda j, i: (i, colfn(j)))
    prev = pl.BlockSpec((HALO, CW), lambda j, i: (jnp.maximum(i * rb - 1, 0), colfn(j)))

    def nxt(nrow_blocks):
        return pl.BlockSpec((HALO, CW), lambda j, i: (jnp.minimum((i + 1) * rb, nrow_blocks * rb - 1), colfn(j)))

    return cur, prev, nxt


def ffnact_fwd(up, w, *, name):
    s, c2 = up.shape
    hh = c2 // 2
    nj = hh // CW
    r = _rows(s)
    nt = s // r

    def body(g_ref, gp_ref, v_ref, vp_ref, wg_ref, wv_ref, o_ref):
        pm = (pl.program_id(1) > 0).astype(F32)
        cg = _conv3(jnp.concatenate([gp_ref[...] * pm, g_ref[...]], axis=0), wg_ref[...])[HALO:]
        cv = _conv3(jnp.concatenate([vp_ref[...] * pm, v_ref[...]], axis=0), wv_ref[...])[HALO:]
        o_ref[...] = (cg * jax.nn.sigmoid(cg) * cv).astype(BF16)

    gcur, gprev, _ = _halo_specs(r, lambda j: j)
    vcur, vprev, _ = _halo_specs(r, lambda j: nj + j)
    wg = pl.BlockSpec((3, CW), lambda j, i: (0, j))
    wv = pl.BlockSpec((3, CW), lambda j, i: (0, nj + j))
    return pl.pallas_call(
        body, name=name, grid=(nj, nt), in_specs=[gcur, gprev, vcur, vprev, wg, wv],
        out_specs=pl.BlockSpec((r, CW), lambda j, i: (i, j)),
        out_shape=jax.ShapeDtypeStruct((s, hh), BF16),
        compiler_params=_cparams("parallel", "parallel"))(up, up, up, up, w, w)


def ffnact_bwd(up, w, dact, *, name):
    s, c2 = up.shape
    hh = c2 // 2
    nj = hh // CW
    r = _rows(s)
    nt = s // r

    def body(g_ref, gp_ref, gn_ref, v_ref, vp_ref, vn_ref, wg_ref, wv_ref, da_ref, dan_ref,
             dg_ref, dv_ref, dwg_ref, dwv_ref):
        i = pl.program_id(1)
        pm = (i > 0).astype(F32)
        nm = (i < nt - 1).astype(F32)

        @pl.when(i == 0)
        def _():
            dwg_ref[...] = jnp.zeros_like(dwg_ref)
            dwv_ref[...] = jnp.zeros_like(dwv_ref)

        wg, wv = wg_ref[...], wv_ref[...]
        zg = jnp.concatenate([gp_ref[...] * pm, g_ref[...], gn_ref[...]], axis=0)
        zv = jnp.concatenate([vp_ref[...] * pm, v_ref[...], vn_ref[...]], axis=0)
        cg = _conv3(zg, wg)[HALO:]
        cv = _conv3(zv, wv)[HALO:]
        da = jnp.concatenate([da_ref[...], dan_ref[...] * nm], axis=0)
        sg = jax.nn.sigmoid(cg)
        dcg = da * cv * (sg * (1.0 + cg * (1.0 - sg)))
        dcv = da * (cg * sg)
        dg_ref[...] = _conv3_t(dcg, wg)[:r].astype(BF16)
        dv_ref[...] = _conv3_t(dcv, wv)[:r].astype(BF16)
        for kk, (a, b) in enumerate(zip(_conv3_dw(dcg[:r], zg, r), _conv3_dw(dcv[:r], zv, r))):
            dwg_ref[kk:kk + 1, :] += a
            dwv_ref[kk:kk + 1, :] += b

    gcur, gprev, gnext = _halo_specs(r, lambda j: j)
    vcur, vprev, vnext = _halo_specs(r, lambda j: nj + j)
    acur, _, anext = _halo_specs(r, lambda j: j)
    wg = pl.BlockSpec((3, CW), lambda j, i: (0, j))
    wv = pl.BlockSpec((3, CW), lambda j, i: (0, nj + j))
    dupg, dupv, dwg, dwv = pl.pallas_call(
        body, name=name, grid=(nj, nt),
        in_specs=[gcur, gprev, gnext(nt), vcur, vprev, vnext(nt), wg, wv, acur, anext(nt)],
        out_specs=[acur, acur, wg, wg],
        out_shape=[jax.ShapeDtypeStruct((s, hh), BF16), jax.ShapeDtypeStruct((s, hh), BF16),
                   jax.ShapeDtypeStruct((3, hh), F32), jax.ShapeDtypeStruct((3, hh), F32)],
        compiler_params=_cparams("parallel", "arbitrary"))(up, up, up, up, up, up, w, w, dact, dact)
    return jnp.concatenate([dupg, dupv], axis=1), jnp.concatenate([dwg, dwv], axis=1)


def sconv_fwd(proj, w, *, name):
    s = proj.shape[0]
    nj = CONVC // CW
    r = _rows(s)
    nt = s // r

    def body(b_ref, c_ref, cp_ref, x_ref, xp_ref, w_ref, o_ref):
        pm = (pl.program_id(1) > 0).astype(F32)
        zw = jnp.concatenate([cp_ref[...] * xp_ref[...] * pm, c_ref[...] * x_ref[...]], axis=0)
        o_ref[...] = (b_ref[...] * _conv3(zw, w_ref[...])[HALO:]).astype(BF16)

    bcur, _, _ = _halo_specs(r, lambda j: (QR + KVR) // CW + j)
    ccur, cprev, _ = _halo_specs(r, lambda j: (QR + KVR + CONVC) // CW + j)
    xcur, xprev, _ = _halo_specs(r, lambda j: (QR + KVR + 2 * CONVC) // CW + j)
    ws = pl.BlockSpec((3, CW), lambda j, i: (0, j))
    return pl.pallas_call(
        body, name=name, grid=(nj, nt), in_specs=[bcur, ccur, cprev, xcur, xprev, ws],
        out_specs=pl.BlockSpec((r, CW), lambda j, i: (i, j)),
        out_shape=jax.ShapeDtypeStruct((s, CONVC), BF16),
        compiler_params=_cparams("parallel", "parallel"))(proj, proj, proj, proj, proj, w)


def sconv_bwd(proj, w, dy, *, name):
    s = proj.shape[0]
    nj = CONVC // CW
    r = _rows(s)
    nt = s // r

    def body(b_ref, bn_ref, c_ref, cp_ref, x_ref, xp_ref, w_ref, dy_ref, dyn_ref, db_ref, dc_ref, dx_ref, dw_ref):
        i = pl.program_id(1)
        pm = (i > 0).astype(F32)
        nm = (i < nt - 1).astype(F32)

        @pl.when(i == 0)
        def _():
            dw_ref[...] = jnp.zeros_like(dw_ref)

        wv = w_ref[...]
        zw = jnp.concatenate([cp_ref[...] * xp_ref[...] * pm, c_ref[...] * x_ref[...]], axis=0)
        conv = _conv3(zw, wv)[HALO:]
        dyv = dy_ref[...]
        db_ref[...] = (dyv * conv).astype(BF16)
        dconv = jnp.concatenate([dyv * b_ref[...], dyn_ref[...] * bn_ref[...] * nm], axis=0)
        dz = _conv3_t(dconv, wv)[:r]
        dc_ref[...] = (dz * x_ref[...]).astype(BF16)
        dx_ref[...] = (dz * c_ref[...]).astype(BF16)
        for kk, a in enumerate(_conv3_dw(dconv[:r], zw, r)):
            dw_ref[kk:kk + 1, :] += a

    bcur, _, bnext = _halo_specs(r, lambda j: (QR + KVR) // CW + j)
    ccur, cprev, _ = _halo_specs(r, lambda j: (QR + KVR + CONVC) // CW + j)
    xcur, xprev, _ = _halo_specs(r, lambda j: (QR + KVR + 2 * CONVC) // CW + j)
    ycur, _, ynext = _halo_specs(r, lambda j: j)
    ws = pl.BlockSpec((3, CW), lambda j, i: (0, j))
    out = pl.BlockSpec((r, CW), lambda j, i: (i, j))
    db, dc, dx, dw = pl.pallas_call(
        body, name=name, grid=(nj, nt),
        in_specs=[bcur, bnext(nt), ccur, cprev, xcur, xprev, ws, ycur, ynext(nt)],
        out_specs=[out, out, out, ws],
        out_shape=[jax.ShapeDtypeStruct((s, CONVC), BF16)] * 3 + [jax.ShapeDtypeStruct((3, CONVC), F32)],
        compiler_params=_cparams("parallel", "arbitrary"))(proj, proj, proj, proj, proj, proj, w, dy, dy)
    return jnp.concatenate([db, dc, dx], axis=1), dw


SW = 512
NJ = NST // SW


def _scan_tables(ar, ai):
    def cmul(x, y):
        return x[0] * y[0] - x[1] * y[1], x[0] * y[1] + x[1] * y[0]

    def build(a, reverse):
        pw = [a]
        for _ in range(SLAB - 1):
            pw.append(cmul(pw[-1], a))
        row = jnp.arange(SLAB)[:, None]
        tabs = []
        for kk in (1, 2, 4):
            mask = ((row < SLAB - kk) if reverse else (row >= kk)).astype(F32)
            tabs += [mask * pw[kk - 1][0][None, :], mask * pw[kk - 1][1][None, :]]
        order = list(range(SLAB - 1, -1, -1)) if reverse else list(range(SLAB))
        tabs += [jnp.stack([pw[o][0] for o in order]), jnp.stack([pw[o][1] for o in order])]
        return jnp.stack(tabs)

    return build((ar, ai), False), build((ar, -ai), True)


def _slab_scan(xr, xi, tabs, cr, ci, reverse):
    for n, kk in enumerate((1, 2, 4)):
        sh = SLAB - kk if reverse else kk
        tr, ti = tabs[2 * n], tabs[2 * n + 1]
        sr, si = pltpu.roll(xr, sh, 0), pltpu.roll(xi, sh, 0)
        xr, xi = xr + tr * sr - ti * si, xi + tr * si + ti * sr
    tr, ti = tabs[6], tabs[7]
    return xr + tr * cr - ti * ci, xi + tr * ci + ti * cr


def s5_fwd(u, bbd_r, bbd_i, cbd_r, cbd_i, tab, *, name):
    s = u.shape[0]
    tbk = _rows(s)
    nt = s // tbk
    nsl = tbk // SLAB

    def body(u_ref, br_ref, bi_ref, cr_ref, ci_ref, tab_ref, y_ref, sr_ref, si_ref, bur, bui, carry):
        @pl.when(pl.program_id(1) == 0)
        def _():
            carry[...] = jnp.zeros_like(carry)

        ub = u_ref[...].astype(BF16)
        bur[...] = jnp.dot(ub, br_ref[0], preferred_element_type=F32)
        bui[...] = jnp.dot(ub, bi_ref[0], preferred_element_type=F32)
        tabs = [tab_ref[n] for n in range(8)]

        def slab(n, c):
            r0 = pl.multiple_of(n * SLAB, SLAB)
            sr, si = _slab_scan(bur[pl.ds(r0, SLAB), :], bui[pl.ds(r0, SLAB), :], tabs, c[0], c[1], False)
            sr_ref[pl.ds(r0, SLAB), :] = sr
            si_ref[pl.ds(r0, SLAB), :] = si
            return (jnp.broadcast_to(sr[SLAB - 1:SLAB], sr.shape), jnp.broadcast_to(si[SLAB - 1:SLAB], si.shape))

        cr, ci = lax.fori_loop(0, nsl, slab, (carry[0], carry[1]))
        carry[0] = cr
        carry[1] = ci
        y_ref[...] = (jnp.dot(sr_ref[...].astype(BF16), cr_ref[0], preferred_element_type=F32)
                      - jnp.dot(si_ref[...].astype(BF16), ci_ref[0], preferred_element_type=F32))

    us = pl.BlockSpec((tbk, LANE), lambda j, t: (t, j))
    bs = pl.BlockSpec((1, LANE, SW), lambda j, t: (j, 0, 0))
    cs = pl.BlockSpec((1, SW, LANE), lambda j, t: (j, 0, 0))
    ts = pl.BlockSpec((8, SLAB, SW), lambda j, t: (0, 0, j))
    ss = pl.BlockSpec((tbk, SW), lambda j, t: (t, j))
    return pl.pallas_call(
        body, name=name, grid=(NJ, nt), in_specs=[us, bs, bs, cs, cs, ts], out_specs=[us, ss, ss],
        out_shape=[jax.ShapeDtypeStruct((s, D_MODEL), F32), jax.ShapeDtypeStruct((s, NST), F32),
                   jax.ShapeDtypeStruct((s, NST), F32)],
        scratch_shapes=[pltpu.VMEM((tbk, SW), F32), pltpu.VMEM((tbk, SW), F32), pltpu.VMEM((2, SLAB, SW), F32)],
        compiler_params=_cparams("parallel", "arbitrary"))(u, bbd_r, bbd_i, cbd_r, cbd_i, tab)


def s5_bwd(u, dy, dskip, st_r, st_i, bbd_r, bbd_i, cbd_r, cbd_i, tabrev, *, name):
    s = u.shape[0]
    tbk = _rows(s)
    nt = s // tbk
    nsl = tbk // SLAB
    rbk = tbk // SLAB

    def body(u_ref, dy_ref, d_ref, sr_ref, si_ref, pr_ref, pi_ref, br_ref, bi_ref, cr_ref, ci_ref, tab_ref,
             du_ref, dbr_ref, dbi_ref, dcr_ref, dci_ref, da_ref, lam_r, lam_i, carry):
        t = pl.program_id(1)

        @pl.when(t == 0)
        def _():
            carry[...] = jnp.zeros_like(carry)
            dbr_ref[...] = jnp.zeros_like(dbr_ref)
            dbi_ref[...] = jnp.zeros_like(dbi_ref)
            dcr_ref[...] = jnp.zeros_like(dcr_ref)
            dci_ref[...] = jnp.zeros_like(dci_ref)
            da_ref[...] = jnp.zeros_like(da_ref)

        dyv = dy_ref[...]
        dyh = dyv.astype(BF16)
        lam_r[...] = lax.dot_general(dyh, cr_ref[0], NT, preferred_element_type=F32)
        lam_i[...] = -lax.dot_general(dyh, ci_ref[0], NT, preferred_element_type=F32)
        tabs = [tab_ref[n] for n in range(8)]

        def slab(n, c):
            r0 = pl.multiple_of((nsl - 1 - n) * SLAB, SLAB)
            lr, li = _slab_scan(lam_r[pl.ds(r0, SLAB), :], lam_i[pl.ds(r0, SLAB), :], tabs, c[0], c[1], True)
            lam_r[pl.ds(r0, SLAB), :] = lr
            lam_i[pl.ds(r0, SLAB), :] = li
            return (jnp.broadcast_to(lr[0:1], lr.shape), jnp.broadcast_to(li[0:1], li.shape))

        cr, ci = lax.fori_loop(0, nsl, slab, (carry[0], carry[1]))
        carry[0] = cr
        carry[1] = ci
        lr, li = lam_r[...], lam_i[...]
        lrh, lih = lr.astype(BF16), li.astype(BF16)
        du = (dyv * d_ref[...] + lax.dot_general(lrh, br_ref[0], NT, preferred_element_type=F32)
              + lax.dot_general(lih, bi_ref[0], NT, preferred_element_type=F32))
        du_ref[...] = du.astype(BF16)
        ub = u_ref[...].astype(BF16)
        dbr_ref[0] += lax.dot_general(ub, lrh, TN, preferred_element_type=F32)
        dbi_ref[0] += lax.dot_general(ub, lih, TN, preferred_element_type=F32)
        srv, siv = sr_ref[...], si_ref[...]
        dcr_ref[0] += lax.dot_general(srv.astype(BF16), dyh, TN, preferred_element_type=F32)
        dci_ref[0] -= lax.dot_general(siv.astype(BF16), dyh, TN, preferred_element_type=F32)
        first = lax.broadcasted_iota(jnp.int32, srv.shape, 0) == 0
        pm = (t < nt - 1).astype(F32)
        spr = jnp.where(first, pr_ref[SLAB - 1:SLAB, :] * pm, pltpu.roll(srv, 1, 0))
        spi = jnp.where(first, pi_ref[SLAB - 1:SLAB, :] * pm, pltpu.roll(siv, 1, 0))
        da_ref[0:1, :] += jnp.sum(lr * spr + li * spi, axis=0, keepdims=True)
        da_ref[1:2, :] += jnp.sum(li * spr - lr * spi, axis=0, keepdims=True)

    rv = lambda t: nt - 1 - t
    us = pl.BlockSpec((tbk, LANE), lambda j, t: (rv(t), j))
    ds = pl.BlockSpec((1, LANE), lambda j, t: (0, j))
    ss = pl.BlockSpec((tbk, SW), lambda j, t: (rv(t), j))
    ps = pl.BlockSpec((SLAB, SW), lambda j, t: (jnp.maximum(rv(t) * rbk - 1, 0), j))
    bs = pl.BlockSpec((1, LANE, SW), lambda j, t: (j, 0, 0))
    cs = pl.BlockSpec((1, SW, LANE), lambda j, t: (j, 0, 0))
    ts = pl.BlockSpec((8, SLAB, SW), lambda j, t: (0, 0, j))
    das = pl.BlockSpec((2, SW), lambda j, t: (0, j))
    return pl.pallas_call(
        body, name=name, grid=(NJ, nt),
        in_specs=[us, us, ds, ss, ss, ps, ps, bs, bs, cs, cs, ts],
        out_specs=[us, bs, bs, cs, cs, das],
        out_shape=[jax.ShapeDtypeStruct((s, D_MODEL), BF16),
                   jax.ShapeDtypeStruct((NJ, LANE, SW), F32), jax.ShapeDtypeStruct((NJ, LANE, SW), F32),
                   jax.ShapeDtypeStruct((NJ, SW, LANE), F32), jax.ShapeDtypeStruct((NJ, SW, LANE), F32),
                   jax.ShapeDtypeStruct((2, NST), F32)],
        scratch_shapes=[pltpu.VMEM((tbk, SW), F32), pltpu.VMEM((tbk, SW), F32), pltpu.VMEM((2, SLAB, SW), F32)],
        compiler_params=_cparams("parallel", "arbitrary"))(
            u, dy, dskip, st_r, st_i, st_r, st_i, bbd_r, bbd_i, cbd_r, cbd_i, tabrev)


GELU_C = math.sqrt(2.0 / math.pi)
GELU_A = 0.044715


def s5post_fwd(y, u, dskip, *, name):
    s = y.shape[0]
    tm = _rows(s)

    def body(y_ref, u_ref, d_ref, o_ref):
        z = y_ref[...] + d_ref[...] * u_ref[...]
        o_ref[...] = (0.5 * z * (1.0 + jnp.tanh(GELU_C * (z + GELU_A * z * z * z)))).astype(BF16)

    row = pl.BlockSpec((tm, D_MODEL), lambda i: (i, 0))
    vec = pl.BlockSpec((1, D_MODEL), lambda i: (0, 0))
    return pl.pallas_call(body, name=name, grid=(s // tm,), in_specs=[row, row, vec], out_specs=row,
                          out_shape=jax.ShapeDtypeStruct((s, D_MODEL), BF16),
                          compiler_params=_cparams("parallel"))(y, u, dskip)


def s5post_bwd(y, u, dskip, dg, *, name):
    s = y.shape[0]
    tm = _rows(s)

    def body(y_ref, u_ref, d_ref, dg_ref, dz_ref, dd_ref):
        @pl.when(pl.program_id(0) == 0)
        def _():
            dd_ref[...] = jnp.zeros_like(dd_ref)

        uv = u_ref[...]
        z = y_ref[...] + d_ref[...] * uv
        th = jnp.tanh(GELU_C * (z + GELU_A * z * z * z))
        dgelu = 0.5 * (1.0 + th) + 0.5 * z * (1.0 - th * th) * (GELU_C * (1.0 + 3.0 * GELU_A * z * z))
        dz = dg_ref[...] * dgelu
        dz_ref[...] = dz
        dd_ref[...] += jnp.sum(dz * uv, axis=0, keepdims=True)

    row = pl.BlockSpec((tm, D_MODEL), lambda i: (i, 0))
    vec = pl.BlockSpec((1, D_MODEL), lambda i: (0, 0))
    return pl.pallas_call(body, name=name, grid=(s // tm,), in_specs=[row, row, vec, row], out_specs=[row, vec],
                          out_shape=[jax.ShapeDtypeStruct((s, D_MODEL), F32), jax.ShapeDtypeStruct((1, D_MODEL), F32)],
                          compiler_params=_cparams("arbitrary"))(y, u, dskip, dg)


def glu_fwd(glu, x, *, name):
    s = x.shape[0]
    tm = _rows(s)

    def body(a_ref, b_ref, x_ref, o_ref):
        o_ref[...] = x_ref[...] + a_ref[...] * jax.nn.sigmoid(b_ref[...])

    row = pl.BlockSpec((tm, D_MODEL), lambda i: (i, 0))
    return pl.pallas_call(body, name=name, grid=(s // tm,),
                          in_specs=[row, pl.BlockSpec((tm, D_MODEL), lambda i: (i, 1)), row], out_specs=row,
                          out_shape=jax.ShapeDtypeStruct((s, D_MODEL), F32),
                          compiler_params=_cparams("parallel"))(glu, glu, x)


def glu_bwd(glu, dx, *, name):
    s = dx.shape[0]
    tm = _rows(s)

    def body(a_ref, b_ref, dx_ref, o_ref):
        sg = jax.nn.sigmoid(b_ref[...])
        dxv = dx_ref[...]
        o_ref[:, :D_MODEL] = (dxv * sg).astype(BF16)
        o_ref[:, D_MODEL:] = (dxv * a_ref[...] * sg * (1.0 - sg)).astype(BF16)

    row = pl.BlockSpec((tm, D_MODEL), lambda i: (i, 0))
    return pl.pallas_call(body, name=name, grid=(s // tm,),
                          in_specs=[row, pl.BlockSpec((tm, D_MODEL), lambda i: (i, 1)), row],
                          out_specs=pl.BlockSpec((tm, 2 * D_MODEL), lambda i: (i, 0)),
                          out_shape=jax.ShapeDtypeStruct((s, 2 * D_MODEL), BF16),
                          compiler_params=_cparams("parallel"))(glu, glu, dx)


def loss_head(y, target, *, name):
    s = y.shape[0]
    tm = _rows(s)

    def body(y_ref, t_ref, dy_ref, dyh_ref, l_ref):
        @pl.when(pl.program_id(0) == 0)
        def _():
            l_ref[...] = jnp.zeros_like(l_ref)

        e = y_ref[...] - t_ref[...]
        dy_ref[...] = e * (1.0 / D_MODEL)
        dyh_ref[...] = (e * (1.0 / D_MODEL)).astype(BF16)
        e2 = jnp.sum((e * e).reshape(tm // 8, 8, D_MODEL), axis=0)
        acc = e2[:, 0:LANE]
        for kk in range(1, D_MODEL // LANE):
            acc = acc + e2[:, kk * LANE:(kk + 1) * LANE]
        l_ref[...] += acc

    row = pl.BlockSpec((tm, D_MODEL), lambda i: (i, 0))
    return pl.pallas_call(body, name=name, grid=(s // tm,), in_specs=[row, row],
                          out_specs=[row, row, pl.BlockSpec((8, LANE), lambda i: (0, 0))],
                          out_shape=[jax.ShapeDtypeStruct((s, D_MODEL), F32), jax.ShapeDtypeStruct((s, D_MODEL), BF16),
                                     jax.ShapeDtypeStruct((8, LANE), F32)],
                          compiler_params=_cparams("arbitrary"))(y, target)


PACKW = 1024
NCHIP = 4


def _mesh_pos():
    return lax.axis_index("x"), lax.axis_index("y"), lax.axis_index("c")


def _chip_exchange(bufs, scatter, name):
    n = len(bufs)
    shapes = [b.shape[1:] if scatter else b.shape for b in bufs]

    def body(*refs):
        ins, outs = refs[:n], refs[n:2 * n]
        send_sems, recv_sems, local_sems = refs[2 * n:]
        x, y, c = _mesh_pos()
        me = 2 * x + y
        peers = [(1 - x, y), (x, 1 - y), (1 - x, 1 - y)]

        def copy(a, j, px, py, dst_slot):
            src = ins[a].at[2 * px + py] if scatter else ins[a]
            return pltpu.make_async_remote_copy(src_ref=src, dst_ref=outs[a].at[dst_slot],
                                                send_sem=send_sems.at[3 * a + j], recv_sem=recv_sems.at[3 * a + j],
                                                device_id=(px, py, c), device_id_type=MESH)

        mine = [pltpu.make_async_copy(ins[a].at[me] if scatter else ins[a], outs[a].at[me], local_sems.at[a])
                for a in range(n)]
        sends = [copy(a, j, px, py, me) for a in range(n) for j, (px, py) in enumerate(peers)]
        for cp in mine + sends:
            cp.start()
        for a in range(n):
            for j, (px, py) in enumerate(peers):
                copy(a, j, px, py, 2 * px + py).wait_recv()
        for cp in sends:
            cp.wait_send()
        for cp in mine:
            cp.wait()

    return pl.pallas_call(
        body, name=name, in_specs=[ANY] * n, out_specs=[ANY] * n,
        out_shape=[jax.ShapeDtypeStruct((NCHIP,) + tuple(shp), b.dtype) for shp, b in zip(shapes, bufs)],
        scratch_shapes=[pltpu.SemaphoreType.DMA((3 * n,)), pltpu.SemaphoreType.DMA((3 * n,)),
                        pltpu.SemaphoreType.DMA((n,))],
    )(*bufs)


HBM_SPEC = pl.BlockSpec(memory_space=pltpu.HBM)
SEM_SPEC = pl.BlockSpec(memory_space=pltpu.SEMAPHORE)
DATAFLOW = pltpu.SideEffectType.DATAFLOW_SIDE_EFFECTING


def _exchange_copy(ins, lands, send_sems, recv_sems, scatter, a, j, px, py, c, dst_slot):
    src = ins[a].at[2 * px + py] if scatter else ins[a]
    return pltpu.make_async_remote_copy(src_ref=src, dst_ref=lands[a].at[dst_slot],
                                        send_sem=send_sems.at[3 * a + j], recv_sem=recv_sems.at[3 * a + j],
                                        device_id=(px, py, c), device_id_type=MESH)


def _own_copy(ins, lands, local_sems, scatter, a, me):
    return pltpu.make_async_copy(ins[a].at[me] if scatter else ins[a], lands[a].at[me], local_sems.at[a])


def exchange_start(bufs, scatter, name):
    n = len(bufs)
    lands = [lax.empty((NCHIP,) + tuple(b.shape[1:] if scatter else b.shape), b.dtype) for b in bufs]

    def body(*refs):
        ins, lnd, send_sems, recv_sems, local_sems, token = (refs[:n], refs[n:2 * n], refs[2 * n], refs[2 * n + 1],
                                                             refs[2 * n + 2], refs[-1])
        x, y, c = _mesh_pos()
        me = 2 * x + y
        for a in range(n):
            _own_copy(ins, lnd, local_sems, scatter, a, me).start()
            for j, (px, py) in enumerate([(1 - x, y), (x, 1 - y), (1 - x, 1 - y)]):
                _exchange_copy(ins, lnd, send_sems, recv_sems, scatter, a, j, px, py, c, me).start()
        token[...] = jnp.zeros_like(token)

    thru = [pltpu.HBM(b.shape, b.dtype) for b in list(bufs) + lands]
    out = pl.pallas_call(
        body, name=name, in_specs=[HBM_SPEC] * (2 * n),
        out_specs=[SEM_SPEC] * 3 + [HBM_SPEC] * (2 * n) + [pl.BlockSpec(memory_space=pltpu.VMEM)],
        out_shape=[pltpu.SemaphoreType.DMA((3 * n,)), pltpu.SemaphoreType.DMA((3 * n,)), pltpu.SemaphoreType.DMA((n,))]
        + thru + [jax.ShapeDtypeStruct((SLAB, LANE), F32)],
        input_output_aliases={k: 3 + k for k in range(2 * n)},
        compiler_params=pltpu.CompilerParams(has_side_effects=DATAFLOW),
    )(*[pltpu.with_memory_space_constraint(b, pltpu.HBM) for b in list(bufs) + lands])
    return tuple(out[:3]), out[3:3 + n], out[3 + n:3 + 2 * n], out[-1][0, 0]


def exchange_wait(started, after, scatter, name):
    sems, bufs, lands, _ = started
    n = len(bufs)

    def body(*refs):
        ins, lnd, ssem, rsem, lsem = refs[:n], refs[n:2 * n], refs[2 * n], refs[2 * n + 1], refs[2 * n + 2]
        x, y, c = _mesh_pos()
        for a in range(n):
            _own_copy(ins, lnd, lsem, scatter, a, 2 * x + y).wait()
            for j, (px, py) in enumerate([(1 - x, y), (x, 1 - y), (1 - x, 1 - y)]):
                cp = _exchange_copy(ins, lnd, ssem, rsem, scatter, a, j, px, py, c, 2 * px + py)
                cp.wait_send()
                cp.wait_recv()

    thru = [pltpu.HBM(b.shape, b.dtype) for b in list(bufs) + list(lands)]
    out = pl.pallas_call(
        body, name=name, in_specs=[HBM_SPEC] * (2 * n) + [SEM_SPEC] * 3 + [ANY],
        out_specs=[HBM_SPEC] * (2 * n), out_shape=thru,
        input_output_aliases={k: k for k in range(2 * n)},
        compiler_params=pltpu.CompilerParams(has_side_effects=DATAFLOW),
    )(*bufs, *lands, *sems, after)
    return out[n:]


def sibling_swap(bufs, name):
    n = len(bufs)

    def body(*refs):
        ins, outs, send_sems, recv_sems = refs[:n], refs[n:2 * n], refs[2 * n], refs[2 * n + 1]
        x, y, c = _mesh_pos()
        cps = [pltpu.make_async_remote_copy(src_ref=ins[k], dst_ref=outs[k], send_sem=send_sems.at[k],
                                            recv_sem=recv_sems.at[k], device_id=(x, y, 1 - c), device_id_type=MESH)
               for k in range(n)]
        for cp in cps:
            cp.start()
        for cp in cps:
            cp.wait()

    return pl.pallas_call(
        body, name=name, in_specs=[ANY] * n, out_specs=[ANY] * n,
        out_shape=[jax.ShapeDtypeStruct(b.shape, b.dtype) for b in bufs],
        scratch_shapes=[pltpu.SemaphoreType.DMA((n,)), pltpu.SemaphoreType.DMA((n,))],
    )(*bufs)


EW_VMEM_BUDGET = 20 * 1024 * 1024


def _ew_rows(rows, w, bytes_per_elem):
    wpad = -(-w // LANE) * LANE
    for t in (1024, 512, 256, 128, 64, 32, 16, 8):
        if rows % t == 0 and 2 * t * wpad * bytes_per_elem <= EW_VMEM_BUDGET:
            return t
    return rows


def sum_slots(buf, *, name):
    _, rows, w = buf.shape
    tm = _ew_rows(rows, w, NCHIP * buf.dtype.itemsize + 4)

    def body(b_ref, o_ref):
        acc = b_ref[0].astype(F32)
        for kk in range(1, NCHIP):
            acc = acc + b_ref[kk].astype(F32)
        o_ref[...] = acc

    return pl.pallas_call(body, name=name, grid=(rows // tm,),
                          in_specs=[pl.BlockSpec((NCHIP, tm, w), lambda i: (0, i, 0))],
                          out_specs=pl.BlockSpec((tm, w), lambda i: (i, 0)),
                          out_shape=jax.ShapeDtypeStruct((rows, w), F32),
                          compiler_params=_cparams("parallel"))(buf)


def adamw(p_mine, p_other, w, m, v, *, name):
    rows, wd = w.shape
    tm = _ew_rows(rows, wd, 9 * 4)
    c1 = 1.0 - ADAM_B1 ** ADAM_STEP
    c2 = 1.0 - ADAM_B2 ** ADAM_STEP

    def body(a_ref, b_ref, w_ref, m_ref, v_ref, g_ref, d_ref, nm_ref, nv_ref):
        g = a_ref[...] + b_ref[...]
        nm = ADAM_B1 * m_ref[...] + (1.0 - ADAM_B1) * g
        nv = ADAM_B2 * v_ref[...] + (1.0 - ADAM_B2) * (g * g)
        g_ref[...] = g
        nm_ref[...] = nm
        nv_ref[...] = nv
        d_ref[...] = -ADAM_LR * ((nm / c1) / (jnp.sqrt(nv / c2) + ADAM_EPS) + ADAM_WD * w_ref[...])

    row = pl.BlockSpec((tm, wd), lambda i: (i, 0))
    return pl.pallas_call(body, name=name, grid=(rows // tm,), in_specs=[row] * 5, out_specs=[row] * 4,
                          out_shape=[jax.ShapeDtypeStruct((rows, wd), F32)] * 4,
                          compiler_params=_cparams("parallel"))(p_mine, p_other, w, m, v)


def _rows2d(a, lead=0):
    tail = a.shape[lead:]
    n = int(np.prod(tail))
    if tail[-1] < LANE // 2 and n % (8 * LANE) == 0:
        return a.reshape(a.shape[:lead] + (n // (8 * LANE), 8 * LANE))
    return a.reshape(a.shape[:lead] + (-1, tail[-1]))


BIG = [("mix_w_in", 2), ("w_uq", 2), ("w_ukv", 2), ("mix_w_out", 1), ("ssm_w_in", 1), ("w_glu", 2),
       ("ffn_w_up", 2), ("ffn_w_down", 1)]
SMALL = [("sconv_w", 2), ("ssm_norm", 1), ("d_skip", 1), ("ffn_conv_w", 2)]
REPL = ["attn_norm", "cq_norm", "ckv_norm", "q_gain", "k_gain", "lambda_re", "lambda_im", "log_step",
        "b_re", "b_im", "c_re", "c_im", "ffn_norm"]
ORDER = ["attn_norm", "mix_w_in", "cq_norm", "ckv_norm", "w_uq", "w_ukv", "q_gain", "k_gain", "sconv_w", "mix_w_out",
         "ssm_norm", "ssm_w_in", "lambda_re", "lambda_im", "log_step", "b_re", "b_im", "c_re", "c_im", "d_skip",
         "w_glu", "ffn_norm", "ffn_w_up", "ffn_conv_w", "ffn_w_down"]


def _join(g, axis):
    return jnp.concatenate([g[k] for k in range(NCHIP)], axis=axis)


def _split(full, axis):
    return jnp.stack(jnp.split(full, NCHIP, axis=axis))


def _discretize(lr, li, ls, b_re, b_im):
    dt = jnp.exp(ls)[:, None]
    mag = jnp.exp(lr * dt)
    ar, ai = mag * jnp.cos(li * dt), mag * jnp.sin(li * dt)
    nr, ni = ar - 1.0, ai
    den = lr * lr + li * li
    zr, zi = (nr * lr + ni * li) / den, (ni * lr - nr * li) / den
    bbar_r = zr[..., None] * b_re - zi[..., None] * b_im
    bbar_i = zr[..., None] * b_im + zi[..., None] * b_re
    return ar, ai, bbar_r, bbar_i


def _b_blockdiag(bbar):
    gl = G // NJ
    bb = bbar.reshape(NJ, gl, P, GC).transpose(0, 1, 3, 2)
    return jnp.einsum("jgcp,gh->jgchp", bb, jnp.eye(gl, dtype=bbar.dtype)).reshape(NJ, gl * GC, gl * P)


def _b_blockdiag_t(dbd):
    gl = G // NJ
    d = jnp.einsum("jgchp,gh->jgcp", dbd.reshape(NJ, gl, GC, gl, P), jnp.eye(gl, dtype=dbd.dtype))
    return d.transpose(0, 1, 3, 2).reshape(G, P, GC)


def _c_blockdiag(cmat):
    gl = G // NJ
    cc = cmat.reshape(NJ, gl, GC, P).transpose(0, 1, 3, 2)
    return jnp.einsum("jgpc,gh->jgphc", cc, jnp.eye(gl, dtype=cmat.dtype)).reshape(NJ, gl * P, gl * GC)


def _c_blockdiag_t(dbd):
    gl = G // NJ
    d = jnp.einsum("jgphc,gh->jgpc", dbd.reshape(NJ, gl, P, gl, GC), jnp.eye(gl, dtype=dbd.dtype))
    return d.transpose(0, 1, 3, 2).reshape(G, GC, P)


def _pad_heads_cols(w, width):
    r = w.shape[0]
    return jnp.pad(w.reshape(r, HEADS, width), ((0, 0), (0, 0), (0, HP - width))).reshape(r, HEADS * HP)


def _unpad_heads_cols(w, width):
    r = w.shape[0]
    return w.reshape(r, HEADS, HP)[:, :, :width].reshape(r, HEADS * width)


W_IN_SPLIT = (QR + KVR, QR + KVR + ROPE)


def _w_in_layout(w):
    a, b = W_IN_SPLIT
    kr = jnp.pad(w[:, a:b], ((0, 0), (NOPE, HP - QK)))
    return jnp.concatenate([w[:, :a], w[:, b:], kr], axis=1)


def _w_in_layout_t(dw):
    a, b = W_IN_SPLIT
    n = dw.shape[1] - HP
    return jnp.concatenate([dw[:, :a], dw[:, n + NOPE:n + QK], dw[:, a:n]], axis=1)


def _ffn_fwd(x, l, wt, name):
    h = rms_fwd(x, wt["ffn_norm"][l][None], name=f"{name}_norm")
    up = mm(h, wt["ffn_w_up"][l], name=f"{name}_up")
    act = ffnact_fwd(up, wt["ffn_conv_w"][l], name=f"{name}_act")
    out = mm(act, wt["ffn_w_down"][l], add=x, name=f"{name}_down")
    return out, (x, h, up, act)


def _ffn_bwd(dout, douth, saved, l, wt, name):
    x, h, up, act = saved
    g = {}
    dact = mm(douth, wt["ffn_w_down"][l], tb=True, name=f"{name}_ddown")
    g["ffn_w_down"] = mm(act, douth, ta=True, out_dtype=BF16, name=f"{name}_dwdown")
    dup, g["ffn_conv_w"] = ffnact_bwd(up, wt["ffn_conv_w"][l], dact, name=f"{name}_dact")
    g["ffn_w_up"] = mm(h, dup, ta=True, out_dtype=BF16, name=f"{name}_dwup")
    dh = mm(dup, wt["ffn_w_up"][l], tb=True, name=f"{name}_dup")
    dx, dxh, dg = rms_bwd(x, wt["ffn_norm"][l][None], dh, add=dout, twin=True, name=f"{name}_dnorm")
    g["ffn_norm"] = dg[0]
    return dx, dxh, g


def _even_fwd(x, i, wt, tabs, name):
    h = rms_fwd(x, wt["attn_norm"][i][None], name=f"{name}_norm")
    proj = mm(h, wt["w_in2"][i], name=f"{name}_in")
    cqn = rms_fwd(proj, wt["cq_norm"][i][None], col=0, name=f"{name}_cqnorm")
    ckvn = rms_fwd(proj, wt["ckv_norm"][i][None], col=1, name=f"{name}_ckvnorm")
    qraw = mm(cqn, wt["w_uq_p"][i], name=f"{name}_uq")
    kv = mm(ckvn, wt["w_ukv_p"][i], name=f"{name}_ukv")
    q, k, v = qkprep_fwd(qraw, kv, proj, wt["q_gain_p"][i], wt["k_gain_p"][i], tabs, name=f"{name}_qkprep")
    o, oh, lse, lset = attn_fwd(q, k, v, name=f"{name}_attn")
    conv = sconv_fwd(proj, wt["sconv_w"][i], name=f"{name}_sconv")
    t = mm(oh, wt["w_out_a"][i], add=x, name=f"{name}_outa")
    out = mm(conv, wt["w_out_c"][i], add=t, name=f"{name}_outc")
    return out, (x, h, proj, cqn, ckvn, qraw, kv, q, k, v, o, oh, lse, lset, conv)


def _even_bwd(dout, douth, saved, i, wt, tabs, name):
    x, h, proj, cqn, ckvn, qraw, kv, q, k, v, o, oh, lse, lset, conv = saved
    g = {}
    do = mm(douth, wt["w_out_a"][i], tb=True, name=f"{name}_douta")
    dconv = mm(douth, wt["w_out_c"][i], tb=True, name=f"{name}_doutc")
    g["w_out_a"] = mm(oh, douth, ta=True, out_dtype=BF16, name=f"{name}_dwouta")
    g["w_out_c"] = mm(conv, douth, ta=True, out_dtype=BF16, name=f"{name}_dwoutc")
    dgates, g["sconv_w"] = sconv_bwd(proj, wt["sconv_w"][i], dconv, name=f"{name}_dsconv")
    dq, doh, deltat = attn_bwd_dq(q, k, v, o, lse, do, name=f"{name}_dattn_q")
    dk, dv = attn_bwd_dkv(q, k, v, doh, lset, deltat, name=f"{name}_dattn_kv")
    dqraw, dkraw, dkrope, dqg, dkg = qkprep_bwd(qraw, kv, proj, wt["q_gain_p"][i], wt["k_gain_p"][i], tabs, dq, dk,
                                                name=f"{name}_dqkprep")
    g["q_gain"], g["k_gain"] = dqg[0, :QK], dkg[0, :QK]
    dcqn = mm(dqraw, wt["w_uq_p"][i], tb=True, name=f"{name}_duq")
    g["w_uq_p"] = mm(cqn, dqraw, ta=True, out_dtype=BF16, name=f"{name}_dwuq")
    dkv = jnp.concatenate([dkraw, dv], axis=1)
    dckvn = mm(dkv, wt["w_ukv_p"][i], tb=True, name=f"{name}_dukv")
    g["w_ukv_p"] = mm(ckvn, dkv, ta=True, out_dtype=BF16, name=f"{name}_dwukv")
    dcq, dgq = rms_bwd(proj, wt["cq_norm"][i][None], dcqn, col=0, out_dtype=BF16, name=f"{name}_dcqnorm")
    dckv, dgkv = rms_bwd(proj, wt["ckv_norm"][i][None], dckvn, col=1, out_dtype=BF16, name=f"{name}_dckvnorm")
    g["cq_norm"], g["ckv_norm"] = dgq[0], dgkv[0]
    dproj = jnp.concatenate([dcq, dckv, dgates, dkrope.astype(BF16)], axis=1)
    g["w_in2"] = mm(h, dproj, ta=True, out_dtype=BF16, name=f"{name}_dwin")
    dh = mm(dproj, wt["w_in2"][i], tb=True, name=f"{name}_din")
    dx, dxh, dg = rms_bwd(x, wt["attn_norm"][i][None], dh, add=dout, twin=True, name=f"{name}_dnorm")
    g["attn_norm"] = dg[0]
    return dx, dxh, g


def _odd_fwd(x, i, wt, name):
    h = rms_fwd(x, wt["ssm_norm"][i][None], name=f"{name}_norm")
    u = mm(h, wt["ssm_w_in"][i], name=f"{name}_in")
    y, st_r, st_i = s5_fwd(u, wt["bbd_r"][i], wt["bbd_i"][i], wt["cbd_r"][i], wt["cbd_i"][i], wt["tab_f"][i],
                           name=f"{name}_scan")
    gl = s5post_fwd(y, u, wt["d_skip"][i][None], name=f"{name}_gelu")
    glu = mm(gl, wt["w_glu"][i], name=f"{name}_glu")
    out = glu_fwd(glu, x, name=f"{name}_gate")
    return out, (x, h, u, y, st_r, st_i, gl, glu)


def _odd_bwd(dout, douth, saved, i, wt, name):
    x, h, u, y, st_r, st_i, gl, glu = saved
    g = {}
    dglu = glu_bwd(glu, dout, name=f"{name}_dgate")
    g["w_glu"] = mm(gl, dglu, ta=True, out_dtype=BF16, name=f"{name}_dwglu")
    dgl = mm(dglu, wt["w_glu"][i], tb=True, name=f"{name}_dglu")
    dz, dd = s5post_bwd(y, u, wt["d_skip"][i][None], dgl, name=f"{name}_dgelu")
    g["d_skip"] = dd[0]
    du, g["bbd_r"], g["bbd_i"], g["cbd_r"], g["cbd_i"], g["a"] = s5_bwd(
        u, dz, wt["d_skip"][i][None], st_r, st_i, wt["bbd_r"][i], wt["bbd_i"][i], wt["cbd_r"][i], wt["cbd_i"][i],
        wt["tab_r"][i], name=f"{name}_dscan")
    g["ssm_w_in"] = mm(h, du, ta=True, out_dtype=BF16, name=f"{name}_dwin")
    dh = mm(du, wt["ssm_w_in"][i], tb=True, name=f"{name}_din")
    dx, dxh, dg = rms_bwd(x, wt["ssm_norm"][i][None], dh, add=dout, twin=True, name=f"{name}_dnorm")
    g["ssm_norm"] = dg[0]
    return dx, dxh, g


MATMUL_WEIGHTS = {"even": ("mix_w_in", "w_uq", "w_ukv", "mix_w_out"), "odd": ("ssm_w_in", "w_glu"),
                  "ffn": ("ffn_w_up", "ffn_w_down")}
ODD_SMALL = ("lambda_re", "lambda_im", "log_step", "b_re", "b_im", "c_re", "c_im", "ssm_norm", "d_skip")


def _even_layouts(fw, wt, i):
    wt["w_in2"][i] = _w_in_layout(fw["mix_w_in"])
    wt["w_uq_p"][i] = _pad_heads_cols(fw["w_uq"], QK)
    ukv = fw["w_ukv"].reshape(KVR, HEADS, NOPE + VD)
    wt["w_ukv_p"][i] = jnp.concatenate(
        [_pad_heads_cols(ukv[:, :, :NOPE].reshape(KVR, HEADS * NOPE), NOPE),
         _pad_heads_cols(ukv[:, :, NOPE:].reshape(KVR, HEADS * VD), VD)], axis=1)
    wt["w_out_a"][i] = _pad_heads_cols(fw["mix_w_out"][:HEADS * VD].T, VD).T
    wt["w_out_c"][i] = fw["mix_w_out"][HEADS * VD:]


def _even_layouts_t(g):
    dk_, dv_ = g["w_ukv_p"][:, :HEADS * HP], g["w_ukv_p"][:, HEADS * HP:]
    return {"mix_w_in": _w_in_layout_t(g["w_in2"]), "w_uq": _unpad_heads_cols(g["w_uq_p"], QK),
            "w_ukv": jnp.concatenate([dk_.reshape(KVR, HEADS, HP)[:, :, :NOPE], dv_.reshape(KVR, HEADS, HP)[:, :, :VD]],
                                     axis=2).reshape(KVR, HEADS * (NOPE + VD)),
            "mix_w_out": jnp.concatenate([_unpad_heads_cols(g["w_out_a"].T, VD).T, g["w_out_c"]], axis=0)}


def _local_step(x, target, full, getw, putg):
    s = x.shape[0]
    n_even = (DEPTH + 1) // 2
    n_odd = DEPTH // 2
    tabs = _rope_tables(s)
    wt = dict(full)
    for key in ("w_in2", "w_uq_p", "w_ukv_p", "w_out_a", "w_out_c") + sum(MATMUL_WEIGHTS.values(), ()):
        wt[key] = {}
    wt["q_gain_p"] = jnp.pad(full["q_gain"], ((0, 0), (0, HP - QK)))[:, None, :]
    wt["k_gain_p"] = jnp.pad(full["k_gain"], ((0, 0), (0, HP - QK)))[:, None, :]

    disc_vjp = []
    for key in ("bbd_r", "bbd_i", "cbd_r", "cbd_i", "tab_f", "tab_r"):
        wt[key] = []
    for i in range(n_odd):
        (ar, ai, bbr, bbi), vjp = jax.vjp(_discretize, full["lambda_re"][i], full["lambda_im"][i], full["log_step"][i],
                                          full["b_re"][i], full["b_im"][i])
        disc_vjp.append(vjp)
        tf, tr = _scan_tables(ar.reshape(-1), ai.reshape(-1))
        wt["tab_f"].append(tf)
        wt["tab_r"].append(tr)
        wt["bbd_r"].append(_b_blockdiag(bbr).astype(BF16))
        wt["bbd_i"].append(_b_blockdiag(bbi).astype(BF16))
        wt["cbd_r"].append(_c_blockdiag(full["c_re"][i]).astype(BF16))
        wt["cbd_i"].append(_c_blockdiag(full["c_im"][i]).astype(BF16))

    saved = []
    for layer in range(DEPTH):
        i = layer // 2
        if layer % 2 == 0:
            fw, tok = getw("even", i, x)
            wt["attn_norm"] = full["attn_norm"] + tok
            _even_layouts(fw, wt, i)
            x, sm = _even_fwd(x, i, wt, tabs, f"l{layer}_mla")
        else:
            fw, tok = getw("odd", i, x)
            wt["ssm_norm"] = full["ssm_norm"] + tok
            for n, a in fw.items():
                wt[n][i] = a
            x, sm = _odd_fwd(x, i, wt, f"l{layer}_s5")
        fw, tok = getw("ffn", layer, x)
        wt["ffn_norm"] = full["ffn_norm"] + tok
        for n, a in fw.items():
            wt[n][layer] = a
        x, sf = _ffn_fwd(x, layer, wt, f"l{layer}_ffn")
        saved.append((sm, sf))
    dx, dxh, lslab = loss_head(x, target, name="loss_head")

    own = [n for n in ORDER if n not in sum(MATMUL_WEIGHTS.values(), ())]
    grads = {n: [None] * (DEPTH if n.startswith("ffn") else n_even) for n in own}
    tok = 0.0
    for layer in reversed(range(DEPTH)):
        i = layer // 2
        sm, sf = saved[layer]
        wt["ffn_conv_w"] = full["ffn_conv_w"] + tok
        dx, dxh, g = _ffn_bwd(dx, dxh, sf, layer, wt, f"l{layer}_ffn")
        tok = putg("ffn", layer, {n: g[n] for n in MATMUL_WEIGHTS["ffn"]})
        for n in ("ffn_norm", "ffn_conv_w"):
            grads[n][layer] = g[n]
        if layer % 2 == 0:
            wt["sconv_w"] = full["sconv_w"] + tok
            dx, dxh, g = _even_bwd(dx, dxh, sm, i, wt, tabs, f"l{layer}_mla")
            tok = putg("even", i, _even_layouts_t(g))
            for n in ("attn_norm", "cq_norm", "ckv_norm", "q_gain", "k_gain", "sconv_w"):
                grads[n][i] = g[n]
        else:
            wt["d_skip"] = full["d_skip"] + tok
            dx, dxh, g = _odd_bwd(dx, dxh, sm, i, wt, f"l{layer}_s5")
            tok = putg("odd", i, {n: g[n] for n in MATMUL_WEIGHTS["odd"]})
            dlr, dli, dls, dbr, dbi = disc_vjp[i]((g["a"][0].reshape(G, P), g["a"][1].reshape(G, P),
                                                    _b_blockdiag_t(g["bbd_r"]), _b_blockdiag_t(g["bbd_i"])))
            grads["lambda_re"][i], grads["lambda_im"][i], grads["log_step"][i] = dlr, dli, dls
            grads["b_re"][i], grads["b_im"][i] = dbr, dbi
            grads["c_re"][i], grads["c_im"][i] = _c_blockdiag_t(g["cbd_r"]), _c_blockdiag_t(g["cbd_i"])
            for n in ("ssm_norm", "d_skip"):
                grads[n][i] = g[n]
            if i == 0:
                tok = tok + putg("odd_small", 0, {n: jnp.stack(grads[n]) for n in ODD_SMALL})
    grads = {n: jnp.stack(v) for n, v in grads.items()}
    return jnp.sum(lslab), dx, grads


def kernel(x, attn_norm, mix_w_in, cq_norm, ckv_norm, w_uq, w_ukv, q_gain, k_gain, sconv_w, mix_w_out, ssm_norm, ssm_w_in, lambda_re, lambda_im, log_step, b_re, b_im, c_re, c_im, d_skip, w_glu, ffn_norm, ffn_w_up, ffn_conv_w, ffn_w_down, loss_target, m_attn_norm, m_mix_w_in, m_cq_norm, m_ckv_norm, m_w_uq, m_w_ukv, m_q_gain, m_k_gain, m_sconv_w, m_mix_w_out, m_ssm_norm, m_ssm_w_in, m_lambda_re, m_lambda_im, m_log_step, m_b_re, m_b_im, m_c_re, m_c_im, m_d_skip, m_w_glu, m_ffn_norm, m_ffn_w_up, m_ffn_conv_w, m_ffn_w_down, v_attn_norm, v_mix_w_in, v_cq_norm, v_ckv_norm, v_w_uq, v_w_ukv, v_q_gain, v_k_gain, v_sconv_w, v_mix_w_out, v_ssm_norm, v_ssm_w_in, v_lambda_re, v_lambda_im, v_log_step, v_b_re, v_b_im, v_c_re, v_c_im, v_d_skip, v_w_glu, v_ffn_norm, v_ffn_w_up, v_ffn_conv_w, v_ffn_w_down):
    args = dict(locals())
    w = {n: args[n] for n in ORDER}
    m = {n: args["m_" + n] for n in ORDER}
    v = {n: args["v_" + n] for n in ORDER}
    me = 2 * lax.axis_index("x") + lax.axis_index("y")

    axis = dict(BIG)

    gs = _chip_exchange([w[n] for n, _ in SMALL], False, "gather_w_f32")
    full = {n: w[n] for n in REPL}
    for (n, ax), g in zip(SMALL, gs):
        full[n] = _join(g, ax)
    parts = [(("even", "odd")[layer % 2], layer // 2) for layer in range(DEPTH)]
    parts = [p for layer, mixer in enumerate(parts) for p in (mixer, ("ffn", layer))]
    gathers = {}

    def start_gather(kind, idx, zero):
        shards = [(w[n][idx] + zero).astype(BF16) for n in MATMUL_WEIGHTS[kind]]
        gathers[kind, idx] = exchange_start(shards, False, f"gather_start_{kind}{idx}")

    start_gather(*parts[0], 0.0 * gs[0][(0,) * gs[0].ndim])

    def getw(kind, idx, after):
        got = exchange_wait(gathers[kind, idx], after, False, f"gather_wait_{kind}{idx}")
        nxt = parts.index((kind, idx)) + 1
        tok = 0.0
        if nxt < len(parts):
            start_gather(*parts[nxt], 0.0 * got[0][(0,) * got[0].ndim].astype(F32))
            tok = gathers[parts[nxt]][3]
        return {n: _join(g, axis[n] - 1) for n, g in zip(MATMUL_WEIGHTS[kind], got)}, tok

    scatters, early = [], []

    def putg(kind, idx, g):
        if kind == "odd_small":
            arrs = [_rows2d(g[n]) for n in ODD_SMALL]
            early.append(exchange_start(arrs, False, "gather_start_g_odd"))
            return early[0][3]
        blocks = [_split(g[n], axis[n] - 1) for n in MATMUL_WEIGHTS[kind]]
        scatters.append((kind, idx, exchange_start(blocks, True, f"scatter_start_{kind}{idx}")))
        return scatters[-1][2][3]

    sq, dx, grads = _local_step(x[0], loss_target[0], full, getw, putg)
    loss = lax.psum(0.5 * sq / D_MODEL, ("x", "y", "c"))

    late_names = [n for n in REPL + [n for n, _ in SMALL] if n not in ODD_SMALL]
    rep_names = list(ODD_SMALL) + late_names
    names = [n for n, _ in BIG] + rep_names
    late = [_rows2d(grads[n]) for n in late_names]
    late_started = exchange_start(late, False, "gather_start_g_f32")
    summed = {}
    for kind, idx, started in scatters:
        got = exchange_wait(started, dx, True, f"scatter_wait_{kind}{idx}")
        for n, sl in zip(MATMUL_WEIGHTS[kind], got):
            summed[n, idx] = sum_slots(_rows2d(sl, 1), name=f"sum_{n}{idx}")
    mine = [jnp.concatenate([summed[n, idx] for idx in range(w[n].shape[0])], axis=0) for n, _ in BIG]
    slots = (exchange_wait(early[0], dx, False, "gather_wait_g_odd")
             + exchange_wait(late_started, mine[-1], False, "gather_wait_g_f32"))
    mine += [sum_slots(_rows2d(sl, 1), name=f"sum_{n}") for n, sl in zip(rep_names, slots)]
    other = sibling_swap(mine, "swap_g")

    def local(n, p):
        ax = dict(SMALL).get(n)
        if ax is None:
            return p
        part = lax.dynamic_index_in_dim(_split(p.reshape(grads[n].shape), ax), me, 0, keepdims=False)
        return _rows2d(part)

    outs = {}
    for n, p, q in zip(names, mine, other):
        res = adamw(local(n, p), local(n, q), _rows2d(w[n]), _rows2d(m[n]), _rows2d(v[n]), name=f"adamw_{n}")
        outs[n] = [r.reshape(w[n].shape) for r in res]
    return (loss, dx[None], *[outs[n][0] for n in ORDER], *[outs[n][1] for n in ORDER],
            *[outs[n][2] for n in ORDER], *[outs[n][3] for n in ORDER])
```

```python
import functools
import math

import numpy as np
import jax
import jax.numpy as jnp
from jax import lax
from jax.experimental import pallas as pl
from jax.experimental.pallas import tpu as pltpu

F32, BF16 = jnp.float32, jnp.bfloat16

D_MODEL = 1024
DEPTH = 4
HEADS = 8
NOPE, ROPE, QK, VD = 64, 32, 96, 64
HP = 128
QR, KVR = 256, 256
CONVC = 512
FFN_H = 2816
G, P, GC = 64, 64, 16
NST = G * P
SLAB = 8
LANE = 128
EPS = 1e-6
ROPE_THETA = 10000.0
ADAM_LR, ADAM_B1, ADAM_B2, ADAM_EPS, ADAM_WD, ADAM_STEP = 0.001, 0.9, 0.999, 1e-08, 0.01, 10
VMEM_LIMIT = 48 * 1024 * 1024
MESH = pl.DeviceIdType.MESH
ANY = pl.BlockSpec(memory_space=pl.ANY)


def _cparams(*sem):
    return pltpu.CompilerParams(dimension_semantics=sem, vmem_limit_bytes=VMEM_LIMIT)


def _pick(dim, prefs):
    for p in prefs:
        if dim % p == 0:
            return p
    return dim


def _rows(s):
    return _pick(s, (512, 256, 128, 64, 32, 16, 8))


MM_VMEM_BUDGET = 36 * 1024 * 1024
MM_MAX_TILE_ELEMS = 640 * 1024
HBM_BYTES_PER_US = 3.0e6
STEP_OVERHEAD_US = 0.35


def _lane_tiles(n):
    c = {t for t in range(LANE, min(n, 1536) + 1, LANE) if n % t == 0}
    if n <= 2304 or not c:
        c.add(n)
    return sorted(c, reverse=True)


def _mm_tiles(m, n, k, sa, sb, so):
    best = None
    for tm in [t for t in (1024, 512, 256) if m % t == 0] or [m]:
        for tn in _lane_tiles(n):
            if tm * tn > MM_MAX_TILE_ELEMS:
                continue
            if 2 * (tm * k * sa + k * tn * sb + tm * tn * so) + 4 * tm * tn > MM_VMEM_BUDGET:
                continue
            steps = (m // tm) * (n // tn)
            for inner_n in (True, False):
                moved = (m * k * sa + (m // tm) * k * n * sb) if inner_n else (k * n * sb + (n // tn) * m * k * sa)
                cost = (moved + m * n * so) / HBM_BYTES_PER_US + steps * STEP_OVERHEAD_US
                if best is None or cost < best[0]:
                    best = (cost, tm, tn, inner_n)
    return best[1:]


def mm(a, b, *, ta=False, tb=False, add=None, out_dtype=F32, name):
    if ta:
        kdim, m = a.shape
    else:
        m, kdim = a.shape
    n = b.shape[0] if tb else b.shape[1]
    so = jnp.dtype(out_dtype).itemsize + (0 if add is None else add.dtype.itemsize)
    tm, tn, inner_n = _mm_tiles(m, n, kdim, a.dtype.itemsize, b.dtype.itemsize, so)
    dn = (((0 if ta else 1,), (1 if tb else 0,)), ((), ()))

    def body(*refs):
        if add is None:
            a_ref, b_ref, o_ref = refs
        else:
            a_ref, b_ref, add_ref, o_ref = refs
        r = lax.dot_general(a_ref[...].astype(BF16), b_ref[...].astype(BF16), dn, preferred_element_type=F32)
        if add is not None:
            r = r + add_ref[...].astype(F32)
        o_ref[...] = r.astype(out_dtype)

    ij = (lambda g0, g1: (g0, g1)) if inner_n else (lambda g0, g1: (g1, g0))
    a_spec = (pl.BlockSpec((kdim, tm), lambda g0, g1: (0, ij(g0, g1)[0])) if ta
              else pl.BlockSpec((tm, kdim), lambda g0, g1: (ij(g0, g1)[0], 0)))
    b_spec = (pl.BlockSpec((tn, kdim), lambda g0, g1: (ij(g0, g1)[1], 0)) if tb
              else pl.BlockSpec((kdim, tn), lambda g0, g1: (0, ij(g0, g1)[1])))
    o_spec = pl.BlockSpec((tm, tn), lambda g0, g1: ij(g0, g1))
    ins, specs = [a, b], [a_spec, b_spec]
    if add is not None:
        ins.append(add)
        specs.append(o_spec)
    grid = (m // tm, n // tn) if inner_n else (n // tn, m // tm)
    return pl.pallas_call(
        body, name=name, grid=grid, in_specs=specs, out_specs=o_spec,
        out_shape=jax.ShapeDtypeStruct((m, n), out_dtype),
        compiler_params=_cparams("parallel", "parallel"))(*ins)


def rms_fwd(x, g, *, col=0, out_dtype=BF16, name):
    s = x.shape[0]
    d = g.shape[1]
    tm = _rows(s)

    def body(x_ref, g_ref, o_ref):
        xv = x_ref[...]
        r = lax.rsqrt(jnp.mean(xv * xv, axis=-1, keepdims=True) + EPS)
        o_ref[...] = (xv * r * g_ref[...]).astype(out_dtype)

    return pl.pallas_call(
        body, name=name, grid=(s // tm,),
        in_specs=[pl.BlockSpec((tm, d), lambda i: (i, col)), pl.BlockSpec((1, d), lambda i: (0, 0))],
        out_specs=pl.BlockSpec((tm, d), lambda i: (i, 0)),
        out_shape=jax.ShapeDtypeStruct((s, d), out_dtype),
        compiler_params=_cparams("parallel"))(x, g)


def rms_bwd(x, g, dy, *, col=0, add=None, out_dtype=F32, twin=False, name):
    s = x.shape[0]
    d = g.shape[1]
    tm = _rows(s)

    def body(*refs):
        refs = list(refs)
        dg_ref = refs.pop()
        dxh_ref = refs.pop() if twin else None
        dx_ref = refs.pop()
        add_ref = refs.pop() if add is not None else None
        x_ref, g_ref, dy_ref = refs

        @pl.when(pl.program_id(0) == 0)
        def _():
            dg_ref[...] = jnp.zeros_like(dg_ref)

        xv = x_ref[...]
        dyv = dy_ref[...].astype(F32)
        r = lax.rsqrt(jnp.mean(xv * xv, axis=-1, keepdims=True) + EPS)
        xh = xv * r
        dg_ref[...] += jnp.sum(dyv * xh, axis=0, keepdims=True)
        dxh = dyv * g_ref[...]
        dx = r * (dxh - xh * jnp.mean(dxh * xh, axis=-1, keepdims=True))
        if add is not None:
            dx = dx + add_ref[...]
        dx_ref[...] = dx.astype(out_dtype)
        if twin:
            dxh_ref[...] = dx.astype(BF16)

    row = pl.BlockSpec((tm, d), lambda i: (i, 0))
    vec = pl.BlockSpec((1, d), lambda i: (0, 0))
    ins = [x, g, dy]
    specs = [pl.BlockSpec((tm, d), lambda i: (i, col)), vec, row]
    if add is not None:
        ins.append(add)
        specs.append(row)
    dxs = [jax.ShapeDtypeStruct((s, d), out_dtype)] + ([jax.ShapeDtypeStruct((s, d), BF16)] if twin else [])
    return pl.pallas_call(
        body, name=name, grid=(s // tm,), in_specs=specs,
        out_specs=[row] * len(dxs) + [vec],
        out_shape=dxs + [jax.ShapeDtypeStruct((1, d), F32)],
        compiler_params=_cparams("arbitrary"))(*ins)


def _rope_tables(s):
    inv = 1.0 / (ROPE_THETA ** (jnp.arange(0, ROPE, 2, dtype=F32) / ROPE))
    ang = jnp.arange(s, dtype=F32)[:, None] * inv[None, :]
    cos, sin = jnp.cos(ang), jnp.sin(ang)
    z = lambda w: jnp.zeros((s, w), F32)
    c = jnp.concatenate([jnp.ones((s, NOPE), F32), cos, cos, z(HP - QK)], axis=1)
    s1 = jnp.concatenate([z(NOPE), -sin, z(HP - NOPE - ROPE // 2)], axis=1)
    s2 = jnp.concatenate([z(NOPE + ROPE // 2), sin, z(HP - QK)], axis=1)
    return c, s1, s2


def qkprep_fwd(qraw, kv, proj, qg, kg, tabs, *, name):
    s = qraw.shape[0]
    tm = _rows(s)
    kr_col = (proj.shape[1] - HP) // HP

    def body(q_ref, k_ref, v_ref, kr_ref, qg_ref, kg_ref, c_ref, s1_ref, s2_ref, qo_ref, ko_ref, vo_ref):
        c, s1, s2 = c_ref[...], s1_ref[...], s2_ref[...]

        def f(xv, gain):
            r = lax.rsqrt(jnp.sum(xv * xv, axis=-1, keepdims=True) * (1.0 / QK) + EPS)
            xn = xv * r * gain
            return xn * c + pltpu.roll(xn, HP - ROPE // 2, 1) * s1 + pltpu.roll(xn, ROPE // 2, 1) * s2

        qo_ref[...] = (f(q_ref[...], qg_ref[...]) * Q_SCALE).astype(BF16)
        ko_ref[...] = f(k_ref[...] + kr_ref[...], kg_ref[...]).astype(BF16)
        vv = v_ref[...]
        lane = lax.broadcasted_iota(jnp.int32, vv.shape, 1)
        vo_ref[...] = jnp.where(lane == VD, 1.0, vv).astype(BF16)

    head = pl.BlockSpec((tm, HP), lambda i, h: (i, h))
    tab = pl.BlockSpec((tm, HP), lambda i, h: (i, 0))
    gain = pl.BlockSpec((1, HP), lambda i, h: (0, 0))
    return pl.pallas_call(
        body, name=name, grid=(s // tm, HEADS),
        in_specs=[head, head, pl.BlockSpec((tm, HP), lambda i, h: (i, HEADS + h)),
                  pl.BlockSpec((tm, HP), lambda i, h: (i, kr_col)), gain, gain, tab, tab, tab],
        out_specs=[head, head, head],
        out_shape=[jax.ShapeDtypeStruct((s, HEADS * HP), BF16)] * 3,
        compiler_params=_cparams("parallel", "parallel"))(qraw, kv, kv, proj, qg, kg, *tabs)


def qkprep_bwd(qraw, kv, proj, qg, kg, tabs, dq, dk, *, name):
    s = qraw.shape[0]
    tm = _rows(s)
    kr_col = (proj.shape[1] - HP) // HP

    def body(q_ref, k_ref, kr_ref, qg_ref, kg_ref, c_ref, s1_ref, s2_ref, dq_ref, dk_ref,
             dqr_ref, dkr_ref, dkrope_ref, dqg_ref, dkg_ref):
        i, h = pl.program_id(0), pl.program_id(1)
        c, s1, s2 = c_ref[...], s1_ref[...], s2_ref[...]

        @pl.when((i == 0) & (h == 0))
        def _():
            dqg_ref[...] = jnp.zeros_like(dqg_ref)
            dkg_ref[...] = jnp.zeros_like(dkg_ref)

        @pl.when(h == 0)
        def _():
            dkrope_ref[...] = jnp.zeros_like(dkrope_ref)

        def f(xv, gain, dout):
            r = lax.rsqrt(jnp.sum(xv * xv, axis=-1, keepdims=True) * (1.0 / QK) + EPS)
            xh = xv * r
            dxn = dout * c + pltpu.roll(dout * s1, ROPE // 2, 1) + pltpu.roll(dout * s2, HP - ROPE // 2, 1)
            dgain = jnp.sum(dxn * xh, axis=0, keepdims=True)
            dxh = dxn * gain
            dx = r * (dxh - xh * (jnp.sum(dxh * xh, axis=-1, keepdims=True) * (1.0 / QK)))
            return dx, dgain

        dxq, dgq = f(q_ref[...], qg_ref[...], dq_ref[...])
        dxk, dgk = f(k_ref[...] + kr_ref[...], kg_ref[...], dk_ref[...])
        dqr_ref[...] = dxq.astype(BF16)
        dkr_ref[...] = dxk.astype(BF16)
        dqg_ref[...] += dgq
        dkg_ref[...] += dgk
        lane = lax.broadcasted_iota(jnp.int32, dxk.shape, 1)
        dkrope_ref[...] += jnp.where((lane >= NOPE) & (lane < QK), dxk, 0.0)

    head = pl.BlockSpec((tm, HP), lambda i, h: (i, h))
    tab = pl.BlockSpec((tm, HP), lambda i, h: (i, 0))
    gain = pl.BlockSpec((1, HP), lambda i, h: (0, 0))
    return pl.pallas_call(
        body, name=name, grid=(s // tm, HEADS),
        in_specs=[head, head, pl.BlockSpec((tm, HP), lambda i, h: (i, kr_col)), gain, gain, tab, tab, tab, head, head],
        out_specs=[head, head, tab, gain, gain],
        out_shape=[jax.ShapeDtypeStruct((s, HEADS * HP), BF16)] * 2
        + [jax.ShapeDtypeStruct((s, HP), F32), jax.ShapeDtypeStruct((1, HP), F32), jax.ShapeDtypeStruct((1, HP), F32)],
        compiler_params=_cparams("arbitrary", "arbitrary"))(qraw, kv, proj, qg, kg, *tabs, dq, dk)


ATT_SCALE = QK ** -0.5
LOG2E = math.log2(math.e)
Q_SCALE = ATT_SCALE * LOG2E
NEG = -1e30


def _att_tile(s):
    return _pick(s, (512, 256, 128))


def _causal(sv, diag):
    r = lax.broadcasted_iota(jnp.int32, sv.shape, 0)
    c = lax.broadcasted_iota(jnp.int32, sv.shape, 1)
    return jnp.where(diag & (c > r), NEG, sv)


NT = (((1,), (1,)), ((), ()))
TN = (((0,), (0,)), ((), ()))


def _row_of(col):
    return jnp.broadcast_to(col, (col.shape[0], LANE)).T[0:SLAB, :]


def _att_specs(s):
    t = _att_tile(s)
    nb = s // t
    tile = pl.BlockSpec((t, HP), lambda h, i: (i, h))
    whole = pl.BlockSpec((s, HP), lambda h, i: (0, h))
    row = pl.BlockSpec((1, 1, SLAB, t), lambda h, i: (h, i, 0, 0))
    rows = pl.BlockSpec((1, nb, SLAB, t), lambda h, i: (h, 0, 0, 0))
    return t, nb, tile, whole, row, rows


def attn_fwd(q, k, v, *, name):
    s = q.shape[0]
    t, nb, tile, whole, row, _ = _att_specs(s)

    def body(q_ref, k_ref, v_ref, o_ref, oh_ref, lset_ref, s_scr, mb_scr, acc):
        qb = pl.program_id(1)
        qv = q_ref[...]

        def scores(j):
            r0 = pl.multiple_of(j * t, t)
            return lax.dot_general(qv, k_ref[pl.ds(r0, t), :], NT, preferred_element_type=F32)

        def fold(sv):
            m = sv[:, 0:LANE]
            for kk in range(1, t // LANE):
                m = jnp.maximum(m, sv[:, kk * LANE:(kk + 1) * LANE])
            return m

        def first(j, m):
            sv = scores(j)
            s_scr[j] = sv
            return jnp.maximum(m, fold(sv))

        m = lax.fori_loop(0, qb, first, jnp.full((t, LANE), NEG, F32))
        sd = _causal(scores(qb), True)
        s_scr[qb] = sd
        mcol = jnp.max(jnp.maximum(m, fold(sd)), axis=-1, keepdims=True)
        mb_scr[...] = jnp.broadcast_to(mcol, (t, t))
        acc[...] = jnp.zeros_like(acc)

        def second(j, carry):
            r0 = pl.multiple_of(j * t, t)
            p = jnp.exp2(s_scr[j] - mb_scr[...]).astype(BF16)
            acc[...] += jnp.dot(p, v_ref[pl.ds(r0, t), :], preferred_element_type=F32)
            return carry

        lax.fori_loop(0, qb + 1, second, 0)
        av = acc[...]
        lsum = av[:, VD:VD + 1]
        lane = lax.broadcasted_iota(jnp.int32, av.shape, 1)
        ov = jnp.where(lane == VD, 0.0, av / lsum)
        o_ref[...] = ov
        oh_ref[...] = ov.astype(BF16)
        lset_ref[0, 0] = _row_of(mcol + jnp.log2(lsum))

    return pl.pallas_call(
        body, name=name, grid=(HEADS, nb), in_specs=[tile, whole, whole], out_specs=[tile, tile, row],
        out_shape=[jax.ShapeDtypeStruct((s, HEADS * HP), F32), jax.ShapeDtypeStruct((s, HEADS * HP), BF16),
                   jax.ShapeDtypeStruct((HEADS, nb, SLAB, t), F32)],
        scratch_shapes=[pltpu.VMEM((nb, t, t), F32), pltpu.VMEM((t, t), F32), pltpu.VMEM((t, HP), F32)],
        compiler_params=_cparams("parallel", "parallel"))(q, k, v)


def attn_delta(o, do, *, name):
    s = o.shape[0]
    t, nb, tile, _, row, _ = _att_specs(s)

    def body(o_ref, do_ref, doh_ref, dt_ref):
        dov = do_ref[...]
        doh_ref[...] = dov.astype(BF16)
        dt_ref[0, 0] = _row_of(jnp.sum(dov * o_ref[...], axis=-1, keepdims=True))

    return pl.pallas_call(
        body, name=name, grid=(HEADS, nb), in_specs=[tile, tile], out_specs=[tile, row],
        out_shape=[jax.ShapeDtypeStruct((s, HEADS * HP), BF16), jax.ShapeDtypeStruct((HEADS, nb, SLAB, t), F32)],
        compiler_params=_cparams("parallel", "parallel"))(o, do)


def attn_bwd(q, k, v, doh, lset, deltat, *, name):
    s = q.shape[0]
    t, nb, tile, whole, _, rows = _att_specs(s)

    def body(q_ref, k_ref, v_ref, do_ref, lt_ref, dt_ref, dq_ref, dk_ref, dv_ref, dk_acc, dv_acc):
        kb = pl.program_id(1)
        kt, vt = k_ref[...], v_ref[...]
        dk_acc[...] = jnp.zeros_like(dk_acc)
        dv_acc[...] = jnp.zeros_like(dv_acc)

        @pl.when(kb == 0)
        def _():
            dq_ref[...] = jnp.zeros_like(dq_ref)

        def step(i, diag):
            r0 = pl.multiple_of(i * t, t)
            qi, doi = q_ref[pl.ds(r0, t), :], do_ref[pl.ds(r0, t), :]
            st = lax.dot_general(kt, qi, NT, preferred_element_type=F32)
            if diag:
                kr = lax.broadcasted_iota(jnp.int32, st.shape, 0)
                qc = lax.broadcasted_iota(jnp.int32, st.shape, 1)
                st = jnp.where(kr > qc, NEG, st)
            pt = jnp.exp2(st - lt_ref[0, i][0:1, :])
            dpt = lax.dot_general(vt, doi, NT, preferred_element_type=F32)
            dst = (pt * (dpt - dt_ref[0, i][0:1, :])).astype(BF16)
            dv_acc[...] += jnp.dot(pt.astype(BF16), doi, preferred_element_type=F32)
            dk_acc[...] += jnp.dot(dst, qi, preferred_element_type=F32)
            dq_ref[pl.ds(r0, t), :] += lax.dot_general(dst, kt, TN, preferred_element_type=F32) * ATT_SCALE

        def off_diag(i, carry):
            step(i, False)
            return carry

        step(kb, True)
        lax.fori_loop(kb + 1, nb, off_diag, 0)
        dk_ref[...] = dk_acc[...] * (1.0 / LOG2E)
        dvv = dv_acc[...]
        lane = lax.broadcasted_iota(jnp.int32, dvv.shape, 1)
        dv_ref[...] = jnp.where(lane == VD, 0.0, dvv).astype(BF16)

    return pl.pallas_call(
        body, name=name, grid=(HEADS, nb), in_specs=[whole, tile, tile, whole, rows, rows],
        out_specs=[whole, tile, tile],
        out_shape=[jax.ShapeDtypeStruct((s, HEADS * HP), F32), jax.ShapeDtypeStruct((s, HEADS * HP), F32),
                   jax.ShapeDtypeStruct((s, HEADS * HP), BF16)],
        scratch_shapes=[pltpu.VMEM((t, HP), F32), pltpu.VMEM((t, HP), F32)],
        compiler_params=_cparams("parallel", "arbitrary"))(q, k, v, doh, lset, deltat)


HALO = 8
CW = 256


def _conv3(zw, w):
    return w[2:3] * zw + w[1:2] * pltpu.roll(zw, 1, 0) + w[0:1] * pltpu.roll(zw, 2, 0)


def _conv3_t(dc, w):
    n = dc.shape[0]
    return w[2:3] * dc + w[1:2] * pltpu.roll(dc, n - 1, 0) + w[0:1] * pltpu.roll(dc, n - 2, 0)


def _conv3_dw(dc, zw, r):
    z0 = zw[HALO:HALO + r]
    z1 = pltpu.roll(zw, 1, 0)[HALO:HALO + r]
    z2 = pltpu.roll(zw, 2, 0)[HALO:HALO + r]
    return [jnp.sum(dc * z, axis=0, keepdims=True) for z in (z2, z1, z0)]


def _halo_specs(r, colfn):
    rb = r // HALO
    cur = pl.BlockSpec((r, CW), lambda j, i: (i, colfn(j)))
    prev = pl.BlockSpec((HALO, CW), lambda j, i: (jnp.maximum(i * rb - 1, 0), colfn(j)))

    def nxt(nrow_blocks):
        return pl.BlockSpec((HALO, CW), lambda j, i: (jnp.minimum((i + 1) * rb, nrow_blocks * rb - 1), colfn(j)))

    return cur, prev, nxt


def ffnact_fwd(up, w, *, name):
    s, c2 = up.shape
    hh = c2 // 2
    nj = hh // CW
    r = _rows(s)
    nt = s // r

    def body(g_ref, gp_ref, v_ref, vp_ref, wg_ref, wv_ref, o_ref):
        pm = (pl.program_id(1) > 0).astype(F32)
        cg = _conv3(jnp.concatenate([gp_ref[...] * pm, g_ref[...]], axis=0), wg_ref[...])[HALO:]
        cv = _conv3(jnp.concatenate([vp_ref[...] * pm, v_ref[...]], axis=0), wv_ref[...])[HALO:]
        o_ref[...] = (cg * jax.nn.sigmoid(cg) * cv).astype(BF16)

    gcur, gprev, _ = _halo_specs(r, lambda j: j)
    vcur, vprev, _ = _halo_specs(r, lambda j: nj + j)
    wg = pl.BlockSpec((3, CW), lambda j, i: (0, j))
    wv = pl.BlockSpec((3, CW), lambda j, i: (0, nj + j))
    return pl.pallas_call(
        body, name=name, grid=(nj, nt), in_specs=[gcur, gprev, vcur, vprev, wg, wv],
        out_specs=pl.BlockSpec((r, CW), lambda j, i: (i, j)),
        out_shape=jax.ShapeDtypeStruct((s, hh), BF16),
        compiler_params=_cparams("parallel", "parallel"))(up, up, up, up, w, w)


def ffnact_bwd(up, w, dact, *, name):
    s, c2 = up.shape
    hh = c2 // 2
    nj = hh // CW
    r = _rows(s)
    nt = s // r

    def body(g_ref, gp_ref, gn_ref, v_ref, vp_ref, vn_ref, wg_ref, wv_ref, da_ref, dan_ref,
             dg_ref, dv_ref, dwg_ref, dwv_ref):
        i = pl.program_id(1)
        pm = (i > 0).astype(F32)
        nm = (i < nt - 1).astype(F32)

        @pl.when(i == 0)
        def _():
            dwg_ref[...] = jnp.zeros_like(dwg_ref)
            dwv_ref[...] = jnp.zeros_like(dwv_ref)

        wg, wv = wg_ref[...], wv_ref[...]
        zg = jnp.concatenate([gp_ref[...] * pm, g_ref[...], gn_ref[...]], axis=0)
        zv = jnp.concatenate([vp_ref[...] * pm, v_ref[...], vn_ref[...]], axis=0)
        cg = _conv3(zg, wg)[HALO:]
        cv = _conv3(zv, wv)[HALO:]
        da = jnp.concatenate([da_ref[...], dan_ref[...] * nm], axis=0)
        sg = jax.nn.sigmoid(cg)
        dcg = da * cv * (sg * (1.0 + cg * (1.0 - sg)))
        dcv = da * (cg * sg)
        dg_ref[...] = _conv3_t(dcg, wg)[:r].astype(BF16)
        dv_ref[...] = _conv3_t(dcv, wv)[:r].astype(BF16)
        for kk, (a, b) in enumerate(zip(_conv3_dw(dcg[:r], zg, r), _conv3_dw(dcv[:r], zv, r))):
            dwg_ref[kk:kk + 1, :] += a
            dwv_ref[kk:kk + 1, :] += b

    gcur, gprev, gnext = _halo_specs(r, lambda j: j)
    vcur, vprev, vnext = _halo_specs(r, lambda j: nj + j)
    acur, _, anext = _halo_specs(r, lambda j: j)
    wg = pl.BlockSpec((3, CW), lambda j, i: (0, j))
    wv = pl.BlockSpec((3, CW), lambda j, i: (0, nj + j))
    dupg, dupv, dwg, dwv = pl.pallas_call(
        body, name=name, grid=(nj, nt),
        in_specs=[gcur, gprev, gnext(nt), vcur, vprev, vnext(nt), wg, wv, acur, anext(nt)],
        out_specs=[acur, acur, wg, wg],
        out_shape=[jax.ShapeDtypeStruct((s, hh), BF16), jax.ShapeDtypeStruct((s, hh), BF16),
                   jax.ShapeDtypeStruct((3, hh), F32), jax.ShapeDtypeStruct((3, hh), F32)],
        compiler_params=_cparams("parallel", "arbitrary"))(up, up, up, up, up, up, w, w, dact, dact)
    return jnp.concatenate([dupg, dupv], axis=1), jnp.concatenate([dwg, dwv], axis=1)


def sconv_fwd(proj, w, *, name):
    s = proj.shape[0]
    nj = CONVC // CW
    r = _rows(s)
    nt = s // r

    def body(b_ref, c_ref, cp_ref, x_ref, xp_ref, w_ref, o_ref):
        pm = (pl.program_id(1) > 0).astype(F32)
        zw = jnp.concatenate([cp_ref[...] * xp_ref[...] * pm, c_ref[...] * x_ref[...]], axis=0)
        o_ref[...] = (b_ref[...] * _conv3(zw, w_ref[...])[HALO:]).astype(BF16)

    bcur, _, _ = _halo_specs(r, lambda j: (QR + KVR) // CW + j)
    ccur, cprev, _ = _halo_specs(r, lambda j: (QR + KVR + CONVC) // CW + j)
    xcur, xprev, _ = _halo_specs(r, lambda j: (QR + KVR + 2 * CONVC) // CW + j)
    ws = pl.BlockSpec((3, CW), lambda j, i: (0, j))
    return pl.pallas_call(
        body, name=name, grid=(nj, nt), in_specs=[bcur, ccur, cprev, xcur, xprev, ws],
        out_specs=pl.BlockSpec((r, CW), lambda j, i: (i, j)),
        out_shape=jax.ShapeDtypeStruct((s, CONVC), BF16),
        compiler_params=_cparams("parallel", "parallel"))(proj, proj, proj, proj, proj, w)


def sconv_bwd(proj, w, dy, *, name):
    s = proj.shape[0]
    nj = CONVC // CW
    r = _rows(s)
    nt = s // r

    def body(b_ref, bn_ref, c_ref, cp_ref, x_ref, xp_ref, w_ref, dy_ref, dyn_ref, db_ref, dc_ref, dx_ref, dw_ref):
        i = pl.program_id(1)
        pm = (i > 0).astype(F32)
        nm = (i < nt - 1).astype(F32)

        @pl.when(i == 0)
        def _():
            dw_ref[...] = jnp.zeros_like(dw_ref)

        wv = w_ref[...]
        zw = jnp.concatenate([cp_ref[...] * xp_ref[...] * pm, c_ref[...] * x_ref[...]], axis=0)
        conv = _conv3(zw, wv)[HALO:]
        dyv = dy_ref[...]
        db_ref[...] = (dyv * conv).astype(BF16)
        dconv = jnp.concatenate([dyv * b_ref[...], dyn_ref[...] * bn_ref[...] * nm], axis=0)
        dz = _conv3_t(dconv, wv)[:r]
        dc_ref[...] = (dz * x_ref[...]).astype(BF16)
        dx_ref[...] = (dz * c_ref[...]).astype(BF16)
        for kk, a in enumerate(_conv3_dw(dconv[:r], zw, r)):
            dw_ref[kk:kk + 1, :] += a

    bcur, _, bnext = _halo_specs(r, lambda j: (QR + KVR) // CW + j)
    ccur, cprev, _ = _halo_specs(r, lambda j: (QR + KVR + CONVC) // CW + j)
    xcur, xprev, _ = _halo_specs(r, lambda j: (QR + KVR + 2 * CONVC) // CW + j)
    ycur, _, ynext = _halo_specs(r, lambda j: j)
    ws = pl.BlockSpec((3, CW), lambda j, i: (0, j))
    out = pl.BlockSpec((r, CW), lambda j, i: (i, j))
    db, dc, dx, dw = pl.pallas_call(
        body, name=name, grid=(nj, nt),
        in_specs=[bcur, bnext(nt), ccur, cprev, xcur, xprev, ws, ycur, ynext(nt)],
        out_specs=[out, out, out, ws],
        out_shape=[jax.ShapeDtypeStruct((s, CONVC), BF16)] * 3 + [jax.ShapeDtypeStruct((3, CONVC), F32)],
        compiler_params=_cparams("parallel", "arbitrary"))(proj, proj, proj, proj, proj, proj, w, dy, dy)
    return jnp.concatenate([db, dc, dx], axis=1), dw


SW = 512
NJ = NST // SW


def _scan_tables(ar, ai):
    def cmul(x, y):
        return x[0] * y[0] - x[1] * y[1], x[0] * y[1] + x[1] * y[0]

    def build(a, reverse):
        pw = [a]
        for _ in range(SLAB - 1):
            pw.append(cmul(pw[-1], a))
        row = jnp.arange(SLAB)[:, None]
        tabs = []
        for kk in (1, 2, 4):
            mask = ((row < SLAB - kk) if reverse else (row >= kk)).astype(F32)
            tabs += [mask * pw[kk - 1][0][None, :], mask * pw[kk - 1][1][None, :]]
        order = list(range(SLAB - 1, -1, -1)) if reverse else list(range(SLAB))
        tabs += [jnp.stack([pw[o][0] for o in order]), jnp.stack([pw[o][1] for o in order])]
        return jnp.stack(tabs)

    return build((ar, ai), False), build((ar, -ai), True)


def _slab_scan(xr, xi, tabs, cr, ci, reverse):
    for n, kk in enumerate((1, 2, 4)):
        sh = SLAB - kk if reverse else kk
        tr, ti = tabs[2 * n], tabs[2 * n + 1]
        sr, si = pltpu.roll(xr, sh, 0), pltpu.roll(xi, sh, 0)
        xr, xi = xr + tr * sr - ti * si, xi + tr * si + ti * sr
    tr, ti = tabs[6], tabs[7]
    return xr + tr * cr - ti * ci, xi + tr * ci + ti * cr


def s5_fwd(u, bbd_r, bbd_i, cbd_r, cbd_i, tab, *, name):
    s = u.shape[0]
    tbk = _rows(s)
    nt = s // tbk
    nsl = tbk // SLAB

    def body(u_ref, br_ref, bi_ref, cr_ref, ci_ref, tab_ref, y_ref, sr_ref, si_ref, bur, bui, carry):
        @pl.when(pl.program_id(1) == 0)
        def _():
            carry[...] = jnp.zeros_like(carry)

        ub = u_ref[...].astype(BF16)
        bur[...] = jnp.dot(ub, br_ref[0], preferred_element_type=F32)
        bui[...] = jnp.dot(ub, bi_ref[0], preferred_element_type=F32)
        tabs = [tab_ref[n] for n in range(8)]

        def slab(n, c):
            r0 = pl.multiple_of(n * SLAB, SLAB)
            sr, si = _slab_scan(bur[pl.ds(r0, SLAB), :], bui[pl.ds(r0, SLAB), :], tabs, c[0], c[1], False)
            sr_ref[pl.ds(r0, SLAB), :] = sr
            si_ref[pl.ds(r0, SLAB), :] = si
            return (jnp.broadcast_to(sr[SLAB - 1:SLAB], sr.shape), jnp.broadcast_to(si[SLAB - 1:SLAB], si.shape))

        cr, ci = lax.fori_loop(0, nsl, slab, (carry[0], carry[1]))
        carry[0] = cr
        carry[1] = ci
        y_ref[...] = (jnp.dot(sr_ref[...].astype(BF16), cr_ref[0], preferred_element_type=F32)
                      - jnp.dot(si_ref[...].astype(BF16), ci_ref[0], preferred_element_type=F32))

    us = pl.BlockSpec((tbk, LANE), lambda j, t: (t, j))
    bs = pl.BlockSpec((1, LANE, SW), lambda j, t: (j, 0, 0))
    cs = pl.BlockSpec((1, SW, LANE), lambda j, t: (j, 0, 0))
    ts = pl.BlockSpec((8, SLAB, SW), lambda j, t: (0, 0, j))
    ss = pl.BlockSpec((tbk, SW), lambda j, t: (t, j))
    return pl.pallas_call(
        body, name=name, grid=(NJ, nt), in_specs=[us, bs, bs, cs, cs, ts], out_specs=[us, ss, ss],
        out_shape=[jax.ShapeDtypeStruct((s, D_MODEL), F32), jax.ShapeDtypeStruct((s, NST), F32),
                   jax.ShapeDtypeStruct((s, NST), F32)],
        scratch_shapes=[pltpu.VMEM((tbk, SW), F32), pltpu.VMEM((tbk, SW), F32), pltpu.VMEM((2, SLAB, SW), F32)],
        compiler_params=_cparams("parallel", "arbitrary"))(u, bbd_r, bbd_i, cbd_r, cbd_i, tab)


def s5_bwd(u, dy, dskip, st_r, st_i, bbd_r, bbd_i, cbd_r, cbd_i, tabrev, *, name):
    s = u.shape[0]
    tbk = _rows(s)
    nt = s // tbk
    nsl = tbk // SLAB
    rbk = tbk // SLAB

    def body(u_ref, dy_ref, d_ref, sr_ref, si_ref, pr_ref, pi_ref, br_ref, bi_ref, cr_ref, ci_ref, tab_ref,
             du_ref, dbr_ref, dbi_ref, dcr_ref, dci_ref, da_ref, lam_r, lam_i, carry):
        t = pl.program_id(1)

        @pl.when(t == 0)
        def _():
            carry[...] = jnp.zeros_like(carry)
            dbr_ref[...] = jnp.zeros_like(dbr_ref)
            dbi_ref[...] = jnp.zeros_like(dbi_ref)
            dcr_ref[...] = jnp.zeros_like(dcr_ref)
            dci_ref[...] = jnp.zeros_like(dci_ref)
            da_ref[...] = jnp.zeros_like(da_ref)

        dyv = dy_ref[...]
        dyh = dyv.astype(BF16)
        lam_r[...] = lax.dot_general(dyh, cr_ref[0], NT, preferred_element_type=F32)
        lam_i[...] = -lax.dot_general(dyh, ci_ref[0], NT, preferred_element_type=F32)
        tabs = [tab_ref[n] for n in range(8)]

        def slab(n, c):
            r0 = pl.multiple_of((nsl - 1 - n) * SLAB, SLAB)
            lr, li = _slab_scan(lam_r[pl.ds(r0, SLAB), :], lam_i[pl.ds(r0, SLAB), :], tabs, c[0], c[1], True)
            lam_r[pl.ds(r0, SLAB), :] = lr
            lam_i[pl.ds(r0, SLAB), :] = li
            return (jnp.broadcast_to(lr[0:1], lr.shape), jnp.broadcast_to(li[0:1], li.shape))

        cr, ci = lax.fori_loop(0, nsl, slab, (carry[0], carry[1]))
        carry[0] = cr
        carry[1] = ci
        lr, li = lam_r[...], lam_i[...]
        lrh, lih = lr.astype(BF16), li.astype(BF16)
        du = (dyv * d_ref[...] + lax.dot_general(lrh, br_ref[0], NT, preferred_element_type=F32)
              + lax.dot_general(lih, bi_ref[0], NT, preferred_element_type=F32))
        du_ref[...] = du.astype(BF16)
        ub = u_ref[...].astype(BF16)
        dbr_ref[0] += lax.dot_general(ub, lrh, TN, preferred_element_type=F32)
        dbi_ref[0] += lax.dot_general(ub, lih, TN, preferred_element_type=F32)
        srv, siv = sr_ref[...], si_ref[...]
        dcr_ref[0] += lax.dot_general(srv.astype(BF16), dyh, TN, preferred_element_type=F32)
        dci_ref[0] -= lax.dot_general(siv.astype(BF16), dyh, TN, preferred_element_type=F32)
        first = lax.broadcasted_iota(jnp.int32, srv.shape, 0) == 0
        pm = (t < nt - 1).astype(F32)
        spr = jnp.where(first, pr_ref[SLAB - 1:SLAB, :] * pm, pltpu.roll(srv, 1, 0))
        spi = jnp.where(first, pi_ref[SLAB - 1:SLAB, :] * pm, pltpu.roll(siv, 1, 0))
        da_ref[0:1, :] += jnp.sum(lr * spr + li * spi, axis=0, keepdims=True)
        da_ref[1:2, :] += jnp.sum(li * spr - lr * spi, axis=0, keepdims=True)

    rv = lambda t: nt - 1 - t
    us = pl.BlockSpec((tbk, LANE), lambda j, t: (rv(t), j))
    ds = pl.BlockSpec((1, LANE), lambda j, t: (0, j))
    ss = pl.BlockSpec((tbk, SW), lambda j, t: (rv(t), j))
    ps = pl.BlockSpec((SLAB, SW), lambda j, t: (jnp.maximum(rv(t) * rbk - 1, 0), j))
    bs = pl.BlockSpec((1, LANE, SW), lambda j, t: (j, 0, 0))
    cs = pl.BlockSpec((1, SW, LANE), lambda j, t: (j, 0, 0))
    ts = pl.BlockSpec((8, SLAB, SW), lambda j, t: (0, 0, j))
    das = pl.BlockSpec((2, SW), lambda j, t: (0, j))
    return pl.pallas_call(
        body, name=name, grid=(NJ, nt),
        in_specs=[us, us, ds, ss, ss, ps, ps, bs, bs, cs, cs, ts],
        out_specs=[us, bs, bs, cs, cs, das],
        out_shape=[jax.ShapeDtypeStruct((s, D_MODEL), BF16),
                   jax.ShapeDtypeStruct((NJ, LANE, SW), F32), jax.ShapeDtypeStruct((NJ, LANE, SW), F32),
                   jax.ShapeDtypeStruct((NJ, SW, LANE), F32), jax.ShapeDtypeStruct((NJ, SW, LANE), F32),
                   jax.ShapeDtypeStruct((2, NST), F32)],
        scratch_shapes=[pltpu.VMEM((tbk, SW), F32), pltpu.VMEM((tbk, SW), F32), pltpu.VMEM((2, SLAB, SW), F32)],
        compiler_params=_cparams("parallel", "arbitrary"))(
            u, dy, dskip, st_r, st_i, st_r, st_i, bbd_r, bbd_i, cbd_r, cbd_i, tabrev)


GELU_C = math.sqrt(2.0 / math.pi)
GELU_A = 0.044715


def s5post_fwd(y, u, dskip, *, name):
    s = y.shape[0]
    tm = _rows(s)

    def body(y_ref, u_ref, d_ref, o_ref):
        z = y_ref[...] + d_ref[...] * u_ref[...]
        o_ref[...] = (0.5 * z * (1.0 + jnp.tanh(GELU_C * (z + GELU_A * z * z * z)))).astype(BF16)

    row = pl.BlockSpec((tm, D_MODEL), lambda i: (i, 0))
    vec = pl.BlockSpec((1, D_MODEL), lambda i: (0, 0))
    return pl.pallas_call(body, name=name, grid=(s // tm,), in_specs=[row, row, vec], out_specs=row,
                          out_shape=jax.ShapeDtypeStruct((s, D_MODEL), BF16),
                          compiler_params=_cparams("parallel"))(y, u, dskip)


def s5post_bwd(y, u, dskip, dg, *, name):
    s = y.shape[0]
    tm = _rows(s)

    def body(y_ref, u_ref, d_ref, dg_ref, dz_ref, dd_ref):
        @pl.when(pl.program_id(0) == 0)
        def _():
            dd_ref[...] = jnp.zeros_like(dd_ref)

        uv = u_ref[...]
        z = y_ref[...] + d_ref[...] * uv
        th = jnp.tanh(GELU_C * (z + GELU_A * z * z * z))
        dgelu = 0.5 * (1.0 + th) + 0.5 * z * (1.0 - th * th) * (GELU_C * (1.0 + 3.0 * GELU_A * z * z))
        dz = dg_ref[...] * dgelu
        dz_ref[...] = dz
        dd_ref[...] += jnp.sum(dz * uv, axis=0, keepdims=True)

    row = pl.BlockSpec((tm, D_MODEL), lambda i: (i, 0))
    vec = pl.BlockSpec((1, D_MODEL), lambda i: (0, 0))
    return pl.pallas_call(body, name=name, grid=(s // tm,), in_specs=[row, row, vec, row], out_specs=[row, vec],
                          out_shape=[jax.ShapeDtypeStruct((s, D_MODEL), F32), jax.ShapeDtypeStruct((1, D_MODEL), F32)],
                          compiler_params=_cparams("arbitrary"))(y, u, dskip, dg)


def glu_fwd(glu, x, *, name):
    s = x.shape[0]
    tm = _rows(s)

    def body(a_ref, b_ref, x_ref, o_ref):
        o_ref[...] = x_ref[...] + a_ref[...] * jax.nn.sigmoid(b_ref[...])

    row = pl.BlockSpec((tm, D_MODEL), lambda i: (i, 0))
    return pl.pallas_call(body, name=name, grid=(s // tm,),
                          in_specs=[row, pl.BlockSpec((tm, D_MODEL), lambda i: (i, 1)), row], out_specs=row,
                          out_shape=jax.ShapeDtypeStruct((s, D_MODEL), F32),
                          compiler_params=_cparams("parallel"))(glu, glu, x)


def glu_bwd(glu, dx, *, name):
    s = dx.shape[0]
    tm = _rows(s)

    def body(a_ref, b_ref, dx_ref, o_ref):
        sg = jax.nn.sigmoid(b_ref[...])
        dxv = dx_ref[...]
        o_ref[:, :D_MODEL] = (dxv * sg).astype(BF16)
        o_ref[:, D_MODEL:] = (dxv * a_ref[...] * sg * (1.0 - sg)).astype(BF16)

    row = pl.BlockSpec((tm, D_MODEL), lambda i: (i, 0))
    return pl.pallas_call(body, name=name, grid=(s // tm,),
                          in_specs=[row, pl.BlockSpec((tm, D_MODEL), lambda i: (i, 1)), row],
                          out_specs=pl.BlockSpec((tm, 2 * D_MODEL), lambda i: (i, 0)),
                          out_shape=jax.ShapeDtypeStruct((s, 2 * D_MODEL), BF16),
                          compiler_params=_cparams("parallel"))(glu, glu, dx)


def loss_head(y, target, *, name):
    s = y.shape[0]
    tm = _rows(s)

    def body(y_ref, t_ref, dy_ref, dyh_ref, l_ref):
        @pl.when(pl.program_id(0) == 0)
        def _():
            l_ref[...] = jnp.zeros_like(l_ref)

        e = y_ref[...] - t_ref[...]
        dy_ref[...] = e * (1.0 / D_MODEL)
        dyh_ref[...] = (e * (1.0 / D_MODEL)).astype(BF16)
        e2 = jnp.sum((e * e).reshape(tm // 8, 8, D_MODEL), axis=0)
        acc = e2[:, 0:LANE]
        for kk in range(1, D_MODEL // LANE):
            acc = acc + e2[:, kk * LANE:(kk + 1) * LANE]
        l_ref[...] += acc

    row = pl.BlockSpec((tm, D_MODEL), lambda i: (i, 0))
    return pl.pallas_call(body, name=name, grid=(s // tm,), in_specs=[row, row],
                          out_specs=[row, row, pl.BlockSpec((8, LANE), lambda i: (0, 0))],
                          out_shape=[jax.ShapeDtypeStruct((s, D_MODEL), F32), jax.ShapeDtypeStruct((s, D_MODEL), BF16),
                                     jax.ShapeDtypeStruct((8, LANE), F32)],
                          compiler_params=_cparams("arbitrary"))(y, target)


PACKW = 1024
NCHIP = 4


def _mesh_pos():
    return lax.axis_index("x"), lax.axis_index("y"), lax.axis_index("c")


def _chip_exchange(bufs, scatter, name):
    n = len(bufs)
    shapes = [b.shape[1:] if scatter else b.shape for b in bufs]

    def body(*refs):
        ins, outs = refs[:n], refs[n:2 * n]
        send_sems, recv_sems, local_sems = refs[2 * n:]
        x, y, c = _mesh_pos()
        me = 2 * x + y
        peers = [(1 - x, y), (x, 1 - y), (1 - x, 1 - y)]

        def copy(a, j, px, py, dst_slot):
            src = ins[a].at[2 * px + py] if scatter else ins[a]
            return pltpu.make_async_remote_copy(src_ref=src, dst_ref=outs[a].at[dst_slot],
                                                send_sem=send_sems.at[3 * a + j], recv_sem=recv_sems.at[3 * a + j],
                                                device_id=(px, py, c), device_id_type=MESH)

        mine = [pltpu.make_async_copy(ins[a].at[me] if scatter else ins[a], outs[a].at[me], local_sems.at[a])
                for a in range(n)]
        sends = [copy(a, j, px, py, me) for a in range(n) for j, (px, py) in enumerate(peers)]
        for cp in mine + sends:
            cp.start()
        for a in range(n):
            for j, (px, py) in enumerate(peers):
                copy(a, j, px, py, 2 * px + py).wait_recv()
        for cp in sends:
            cp.wait_send()
        for cp in mine:
            cp.wait()

    return pl.pallas_call(
        body, name=name, in_specs=[ANY] * n, out_specs=[ANY] * n,
        out_shape=[jax.ShapeDtypeStruct((NCHIP,) + tuple(shp), b.dtype) for shp, b in zip(shapes, bufs)],
        scratch_shapes=[pltpu.SemaphoreType.DMA((3 * n,)), pltpu.SemaphoreType.DMA((3 * n,)),
                        pltpu.SemaphoreType.DMA((n,))],
    )(*bufs)


HBM_SPEC = pl.BlockSpec(memory_space=pltpu.HBM)
SEM_SPEC = pl.BlockSpec(memory_space=pltpu.SEMAPHORE)
DATAFLOW = pltpu.SideEffectType.DATAFLOW_SIDE_EFFECTING


def _exchange_copy(ins, lands, send_sems, recv_sems, scatter, a, j, px, py, c, dst_slot):
    src = ins[a].at[2 * px + py] if scatter else ins[a]
    return pltpu.make_async_remote_copy(src_ref=src, dst_ref=lands[a].at[dst_slot],
                                        send_sem=send_sems.at[3 * a + j], recv_sem=recv_sems.at[3 * a + j],
                                        device_id=(px, py, c), device_id_type=MESH)


def _own_copy(ins, lands, local_sems, scatter, a, me):
    return pltpu.make_async_copy(ins[a].at[me] if scatter else ins[a], lands[a].at[me], local_sems.at[a])


def exchange_start(bufs, scatter, name):
    n = len(bufs)
    lands = [lax.empty((NCHIP,) + tuple(b.shape[1:] if scatter else b.shape), b.dtype) for b in bufs]

    def body(*refs):
        ins, lnd, send_sems, recv_sems, local_sems, token = (refs[:n], refs[n:2 * n], refs[2 * n], refs[2 * n + 1],
                                                             refs[2 * n + 2], refs[-1])
        x, y, c = _mesh_pos()
        me = 2 * x + y
        for a in range(n):
            _own_copy(ins, lnd, local_sems, scatter, a, me).start()
            for j, (px, py) in enumerate([(1 - x, y), (x, 1 - y), (1 - x, 1 - y)]):
                _exchange_copy(ins, lnd, send_sems, recv_sems, scatter, a, j, px, py, c, me).start()
        token[...] = jnp.zeros_like(token)

    thru = [pltpu.HBM(b.shape, b.dtype) for b in list(bufs) + lands]
    out = pl.pallas_call(
        body, name=name, in_specs=[HBM_SPEC] * (2 * n),
        out_specs=[SEM_SPEC] * 3 + [HBM_SPEC] * (2 * n) + [pl.BlockSpec(memory_space=pltpu.VMEM)],
        out_shape=[pltpu.SemaphoreType.DMA((3 * n,)), pltpu.SemaphoreType.DMA((3 * n,)), pltpu.SemaphoreType.DMA((n,))]
        + thru + [jax.ShapeDtypeStruct((SLAB, LANE), F32)],
        input_output_aliases={k: 3 + k for k in range(2 * n)},
        compiler_params=pltpu.CompilerParams(has_side_effects=DATAFLOW),
    )(*[pltpu.with_memory_space_constraint(b, pltpu.HBM) for b in list(bufs) + lands])
    return tuple(out[:3]), out[3:3 + n], out[3 + n:3 + 2 * n], out[-1][0, 0]


def exchange_wait(started, after, scatter, name):
    sems, bufs, lands, _ = started
    n = len(bufs)

    def body(*refs):
        ins, lnd, ssem, rsem, lsem = refs[:n], refs[n:2 * n], refs[2 * n], refs[2 * n + 1], refs[2 * n + 2]
        x, y, c = _mesh_pos()
        for a in range(n):
            _own_copy(ins, lnd, lsem, scatter, a, 2 * x + y).wait()
            for j, (px, py) in enumerate([(1 - x, y), (x, 1 - y), (1 - x, 1 - y)]):
                cp = _exchange_copy(ins, lnd, ssem, rsem, scatter, a, j, px, py, c, 2 * px + py)
                cp.wait_send()
                cp.wait_recv()

    thru = [pltpu.HBM(b.shape, b.dtype) for b in list(bufs) + list(lands)]
    out = pl.pallas_call(
        body, name=name, in_specs=[HBM_SPEC] * (2 * n) + [SEM_SPEC] * 3 + [ANY],
        out_specs=[HBM_SPEC] * (2 * n), out_shape=thru,
        input_output_aliases={k: k for k in range(2 * n)},
        compiler_params=pltpu.CompilerParams(has_side_effects=DATAFLOW),
    )(*bufs, *lands, *sems, after)
    return out[n:]


def sibling_swap(bufs, name):
    n = len(bufs)

    def body(*refs):
        ins, outs, send_sems, recv_sems = refs[:n], refs[n:2 * n], refs[2 * n], refs[2 * n + 1]
        x, y, c = _mesh_pos()
        cps = [pltpu.make_async_remote_copy(src_ref=ins[k], dst_ref=outs[k], send_sem=send_sems.at[k],
                                            recv_sem=recv_sems.at[k], device_id=(x, y, 1 - c), device_id_type=MESH)
               for k in range(n)]
        for cp in cps:
            cp.start()
        for cp in cps:
            cp.wait()

    return pl.pallas_call(
        body, name=name, in_specs=[ANY] * n, out_specs=[ANY] * n,
        out_shape=[jax.ShapeDtypeStruct(b.shape, b.dtype) for b in bufs],
        scratch_shapes=[pltpu.SemaphoreType.DMA((n,)), pltpu.SemaphoreType.DMA((n,))],
    )(*bufs)


EW_VMEM_BUDGET = 20 * 1024 * 1024


def _ew_rows(rows, w, bytes_per_elem):
    wpad = -(-w // LANE) * LANE
    for t in (1024, 512, 256, 128, 64, 32, 16, 8):
        if rows % t == 0 and 2 * t * wpad * bytes_per_elem <= EW_VMEM_BUDGET:
            return t
    return rows


def sum_slots(buf, *, name):
    _, rows, w = buf.shape
    tm = _ew_rows(rows, w, NCHIP * buf.dtype.itemsize + 4)

    def body(b_ref, o_ref):
        acc = b_ref[0].astype(F32)
        for kk in range(1, NCHIP):
            acc = acc + b_ref[kk].astype(F32)
        o_ref[...] = acc

    return pl.pallas_call(body, name=name, grid=(rows // tm,),
                          in_specs=[pl.BlockSpec((NCHIP, tm, w), lambda i: (0, i, 0))],
                          out_specs=pl.BlockSpec((tm, w), lambda i: (i, 0)),
                          out_shape=jax.ShapeDtypeStruct((rows, w), F32),
                          compiler_params=_cparams("parallel"))(buf)


def adamw(p_mine, p_other, w, m, v, *, name):
    rows, wd = w.shape
    tm = _ew_rows(rows, wd, 9 * 4)
    c1 = 1.0 - ADAM_B1 ** ADAM_STEP
    c2 = 1.0 - ADAM_B2 ** ADAM_STEP

    def body(a_ref, b_ref, w_ref, m_ref, v_ref, g_ref, d_ref, nm_ref, nv_ref):
        g = a_ref[...] + b_ref[...]
        nm = ADAM_B1 * m_ref[...] + (1.0 - ADAM_B1) * g
        nv = ADAM_B2 * v_ref[...] + (1.0 - ADAM_B2) * (g * g)
        g_ref[...] = g
        nm_ref[...] = nm
        nv_ref[...] = nv
        d_ref[...] = -ADAM_LR * ((nm / c1) / (jnp.sqrt(nv / c2) + ADAM_EPS) + ADAM_WD * w_ref[...])

    row = pl.BlockSpec((tm, wd), lambda i: (i, 0))
    return pl.pallas_call(body, name=name, grid=(rows // tm,), in_specs=[row] * 5, out_specs=[row] * 4,
                          out_shape=[jax.ShapeDtypeStruct((rows, wd), F32)] * 4,
                          compiler_params=_cparams("parallel"))(p_mine, p_other, w, m, v)


def _rows2d(a, lead=0):
    tail = a.shape[lead:]
    n = int(np.prod(tail))
    if tail[-1] < LANE // 2 and n % (8 * LANE) == 0:
        return a.reshape(a.shape[:lead] + (n // (8 * LANE), 8 * LANE))
    return a.reshape(a.shape[:lead] + (-1, tail[-1]))


BIG = [("mix_w_in", 2), ("w_uq", 2), ("w_ukv", 2), ("mix_w_out", 1), ("ssm_w_in", 1), ("w_glu", 2),
       ("ffn_w_up", 2), ("ffn_w_down", 1)]
SMALL = [("sconv_w", 2), ("ssm_norm", 1), ("d_skip", 1), ("ffn_conv_w", 2)]
REPL = ["attn_norm", "cq_norm", "ckv_norm", "q_gain", "k_gain", "lambda_re", "lambda_im", "log_step",
        "b_re", "b_im", "c_re", "c_im", "ffn_norm"]
ORDER = ["attn_norm", "mix_w_in", "cq_norm", "ckv_norm", "w_uq", "w_ukv", "q_gain", "k_gain", "sconv_w", "mix_w_out",
         "ssm_norm", "ssm_w_in", "lambda_re", "lambda_im", "log_step", "b_re", "b_im", "c_re", "c_im", "d_skip",
         "w_glu", "ffn_norm", "ffn_w_up", "ffn_conv_w", "ffn_w_down"]


def _join(g, axis):
    return jnp.concatenate([g[k] for k in range(NCHIP)], axis=axis)


def _split(full, axis):
    return jnp.stack(jnp.split(full, NCHIP, axis=axis))


def _discretize(lr, li, ls, b_re, b_im):
    dt = jnp.exp(ls)[:, None]
    mag = jnp.exp(lr * dt)
    ar, ai = mag * jnp.cos(li * dt), mag * jnp.sin(li * dt)
    nr, ni = ar - 1.0, ai
    den = lr * lr + li * li
    zr, zi = (nr * lr + ni * li) / den, (ni * lr - nr * li) / den
    bbar_r = zr[..., None] * b_re - zi[..., None] * b_im
    bbar_i = zr[..., None] * b_im + zi[..., None] * b_re
    return ar, ai, bbar_r, bbar_i


def _b_blockdiag(bbar):
    gl = G // NJ
    bb = bbar.reshape(NJ, gl, P, GC).transpose(0, 1, 3, 2)
    return jnp.einsum("jgcp,gh->jgchp", bb, jnp.eye(gl, dtype=bbar.dtype)).reshape(NJ, gl * GC, gl * P)


def _b_blockdiag_t(dbd):
    gl = G // NJ
    d = jnp.einsum("jgchp,gh->jgcp", dbd.reshape(NJ, gl, GC, gl, P), jnp.eye(gl, dtype=dbd.dtype))
    return d.transpose(0, 1, 3, 2).reshape(G, P, GC)


def _c_blockdiag(cmat):
    gl = G // NJ
    cc = cmat.reshape(NJ, gl, GC, P).transpose(0, 1, 3, 2)
    return jnp.einsum("jgpc,gh->jgphc", cc, jnp.eye(gl, dtype=cmat.dtype)).reshape(NJ, gl * P, gl * GC)


def _c_blockdiag_t(dbd):
    gl = G // NJ
    d = jnp.einsum("jgphc,gh->jgpc", dbd.reshape(NJ, gl, P, gl, GC), jnp.eye(gl, dtype=dbd.dtype))
    return d.transpose(0, 1, 3, 2).reshape(G, GC, P)


def _pad_heads_cols(w, width):
    r = w.shape[0]
    return jnp.pad(w.reshape(r, HEADS, width), ((0, 0), (0, 0), (0, HP - width))).reshape(r, HEADS * HP)


def _unpad_heads_cols(w, width):
    r = w.shape[0]
    return w.reshape(r, HEADS, HP)[:, :, :width].reshape(r, HEADS * width)


W_IN_SPLIT = (QR + KVR, QR + KVR + ROPE)


def _w_in_layout(w):
    a, b = W_IN_SPLIT
    kr = jnp.pad(w[:, a:b], ((0, 0), (NOPE, HP - QK)))
    return jnp.concatenate([w[:, :a], w[:, b:], kr], axis=1)


def _w_in_layout_t(dw):
    a, b = W_IN_SPLIT
    n = dw.shape[1] - HP
    return jnp.concatenate([dw[:, :a], dw[:, n + NOPE:n + QK], dw[:, a:n]], axis=1)


def _ffn_fwd(x, l, wt, name):
    h = rms_fwd(x, wt["ffn_norm"][l][None], name=f"{name}_norm")
    up = mm(h, wt["ffn_w_up"][l], name=f"{name}_up")
    act = ffnact_fwd(up, wt["ffn_conv_w"][l], name=f"{name}_act")
    out = mm(act, wt["ffn_w_down"][l], add=x, name=f"{name}_down")
    return out, (x, h, up, act)


def _ffn_bwd(dout, douth, saved, l, wt, name):
    x, h, up, act = saved
    g = {}
    dact = mm(douth, wt["ffn_w_down"][l], tb=True, name=f"{name}_ddown")
    g["ffn_w_down"] = mm(act, douth, ta=True, out_dtype=BF16, name=f"{name}_dwdown")
    dup, g["ffn_conv_w"] = ffnact_bwd(up, wt["ffn_conv_w"][l], dact, name=f"{name}_dact")
    g["ffn_w_up"] = mm(h, dup, ta=True, out_dtype=BF16, name=f"{name}_dwup")
    dh = mm(dup, wt["ffn_w_up"][l], tb=True, name=f"{name}_dup")
    dx, dxh, dg = rms_bwd(x, wt["ffn_norm"][l][None], dh, add=dout, twin=True, name=f"{name}_dnorm")
    g["ffn_norm"] = dg[0]
    return dx, dxh, g


def _even_fwd(x, i, wt, tabs, name):
    h = rms_fwd(x, wt["attn_norm"][i][None], name=f"{name}_norm")
    proj = mm(h, wt["w_in2"][i], name=f"{name}_in")
    cqn = rms_fwd(proj, wt["cq_norm"][i][None], col=0, name=f"{name}_cqnorm")
    ckvn = rms_fwd(proj, wt["ckv_norm"][i][None], col=1, name=f"{name}_ckvnorm")
    qraw = mm(cqn, wt["w_uq_p"][i], name=f"{name}_uq")
    kv = mm(ckvn, wt["w_ukv_p"][i], name=f"{name}_ukv")
    q, k, v = qkprep_fwd(qraw, kv, proj, wt["q_gain_p"][i], wt["k_gain_p"][i], tabs, name=f"{name}_qkprep")
    o, oh, lset = attn_fwd(q, k, v, name=f"{name}_attn")
    conv = sconv_fwd(proj, wt["sconv_w"][i], name=f"{name}_sconv")
    t = mm(oh, wt["w_out_a"][i], add=x, name=f"{name}_outa")
    out = mm(conv, wt["w_out_c"][i], add=t, name=f"{name}_outc")
    return out, (x, h, proj, cqn, ckvn, qraw, kv, q, k, v, o, oh, lset, conv)


def _even_bwd(dout, douth, saved, i, wt, tabs, name):
    x, h, proj, cqn, ckvn, qraw, kv, q, k, v, o, oh, lset, conv = saved
    g = {}
    do = mm(douth, wt["w_out_a"][i], tb=True, name=f"{name}_douta")
    dconv = mm(douth, wt["w_out_c"][i], tb=True, name=f"{name}_doutc")
    g["w_out_a"] = mm(oh, douth, ta=True, out_dtype=BF16, name=f"{name}_dwouta")
    g["w_out_c"] = mm(conv, douth, ta=True, out_dtype=BF16, name=f"{name}_dwoutc")
    dgates, g["sconv_w"] = sconv_bwd(proj, wt["sconv_w"][i], dconv, name=f"{name}_dsconv")
    doh, deltat = attn_delta(o, do, name=f"{name}_dattn_delta")
    dq, dk, dv = attn_bwd(q, k, v, doh, lset, deltat, name=f"{name}_dattn")
    dqraw, dkraw, dkrope, dqg, dkg = qkprep_bwd(qraw, kv, proj, wt["q_gain_p"][i], wt["k_gain_p"][i], tabs, dq, dk,
                                                name=f"{name}_dqkprep")
    g["q_gain"], g["k_gain"] = dqg[0, :QK], dkg[0, :QK]
    dcqn = mm(dqraw, wt["w_uq_p"][i], tb=True, name=f"{name}_duq")
    g["w_uq_p"] = mm(cqn, dqraw, ta=True, out_dtype=BF16, name=f"{name}_dwuq")
    dkv = jnp.concatenate([dkraw, dv], axis=1)
    dckvn = mm(dkv, wt["w_ukv_p"][i], tb=True, name=f"{name}_dukv")
    g["w_ukv_p"] = mm(ckvn, dkv, ta=True, out_dtype=BF16, name=f"{name}_dwukv")
    dcq, dgq = rms_bwd(proj, wt["cq_norm"][i][None], dcqn, col=0, out_dtype=BF16, name=f"{name}_dcqnorm")
    dckv, dgkv = rms_bwd(proj, wt["ckv_norm"][i][None], dckvn, col=1, out_dtype=BF16, name=f"{name}_dckvnorm")
    g["cq_norm"], g["ckv_norm"] = dgq[0], dgkv[0]
    dproj = jnp.concatenate([dcq, dckv, dgates, dkrope.astype(BF16)], axis=1)
    g["w_in2"] = mm(h, dproj, ta=True, out_dtype=BF16, name=f"{name}_dwin")
    dh = mm(dproj, wt["w_in2"][i], tb=True, name=f"{name}_din")
    dx, dxh, dg = rms_bwd(x, wt["attn_norm"][i][None], dh, add=dout, twin=True, name=f"{name}_dnorm")
    g["attn_norm"] = dg[0]
    return dx, dxh, g


def _odd_fwd(x, i, wt, name):
    h = rms_fwd(x, wt["ssm_norm"][i][None], name=f"{name}_norm")
    u = mm(h, wt["ssm_w_in"][i], name=f"{name}_in")
    y, st_r, st_i = s5_fwd(u, wt["bbd_r"][i], wt["bbd_i"][i], wt["cbd_r"][i], wt["cbd_i"][i], wt["tab_f"][i],
                           name=f"{name}_scan")
    gl = s5post_fwd(y, u, wt["d_skip"][i][None], name=f"{name}_gelu")
    glu = mm(gl, wt["w_glu"][i], name=f"{name}_glu")
    out = glu_fwd(glu, x, name=f"{name}_gate")
    return out, (x, h, u, y, st_r, st_i, gl, glu)


def _odd_bwd(dout, douth, saved, i, wt, name):
    x, h, u, y, st_r, st_i, gl, glu = saved
    g = {}
    dglu = glu_bwd(glu, dout, name=f"{name}_dgate")
    g["w_glu"] = mm(gl, dglu, ta=True, out_dtype=BF16, name=f"{name}_dwglu")
    dgl = mm(dglu, wt["w_glu"][i], tb=True, name=f"{name}_dglu")
    dz, dd = s5post_bwd(y, u, wt["d_skip"][i][None], dgl, name=f"{name}_dgelu")
    g["d_skip"] = dd[0]
    du, g["bbd_r"], g["bbd_i"], g["cbd_r"], g["cbd_i"], g["a"] = s5_bwd(
        u, dz, wt["d_skip"][i][None], st_r, st_i, wt["bbd_r"][i], wt["bbd_i"][i], wt["cbd_r"][i], wt["cbd_i"][i],
        wt["tab_r"][i], name=f"{name}_dscan")
    g["ssm_w_in"] = mm(h, du, ta=True, out_dtype=BF16, name=f"{name}_dwin")
    dh = mm(du, wt["ssm_w_in"][i], tb=True, name=f"{name}_din")
    dx, dxh, dg = rms_bwd(x, wt["ssm_norm"][i][None], dh, add=dout, twin=True, name=f"{name}_dnorm")
    g["ssm_norm"] = dg[0]
    return dx, dxh, g


MATMUL_WEIGHTS = {"even": ("mix_w_in", "w_uq", "w_ukv", "mix_w_out"), "odd": ("ssm_w_in", "w_glu"),
                  "ffn": ("ffn_w_up", "ffn_w_down")}
ODD_SMALL = ("lambda_re", "lambda_im", "log_step", "b_re", "b_im", "c_re", "c_im", "ssm_norm", "d_skip")


def _even_layouts(fw, wt, i):
    wt["w_in2"][i] = _w_in_layout(fw["mix_w_in"])
    wt["w_uq_p"][i] = _pad_heads_cols(fw["w_uq"], QK)
    ukv = fw["w_ukv"].reshape(KVR, HEADS, NOPE + VD)
    wt["w_ukv_p"][i] = jnp.concatenate(
        [_pad_heads_cols(ukv[:, :, :NOPE].reshape(KVR, HEADS * NOPE), NOPE),
         _pad_heads_cols(ukv[:, :, NOPE:].reshape(KVR, HEADS * VD), VD)], axis=1)
    wt["w_out_a"][i] = _pad_heads_cols(fw["mix_w_out"][:HEADS * VD].T, VD).T
    wt["w_out_c"][i] = fw["mix_w_out"][HEADS * VD:]


def _even_layouts_t(g):
    dk_, dv_ = g["w_ukv_p"][:, :HEADS * HP], g["w_ukv_p"][:, HEADS * HP:]
    return {"mix_w_in": _w_in_layout_t(g["w_in2"]), "w_uq": _unpad_heads_cols(g["w_uq_p"], QK),
            "w_ukv": jnp.concatenate([dk_.reshape(KVR, HEADS, HP)[:, :, :NOPE], dv_.reshape(KVR, HEADS, HP)[:, :, :VD]],
                                     axis=2).reshape(KVR, HEADS * (NOPE + VD)),
            "mix_w_out": jnp.concatenate([_unpad_heads_cols(g["w_out_a"].T, VD).T, g["w_out_c"]], axis=0)}


def _local_step(x, target, full, getw, putg):
    s = x.shape[0]
    n_even = (DEPTH + 1) // 2
    n_odd = DEPTH // 2
    tabs = _rope_tables(s)
    wt = dict(full)
    for key in ("w_in2", "w_uq_p", "w_ukv_p", "w_out_a", "w_out_c") + sum(MATMUL_WEIGHTS.values(), ()):
        wt[key] = {}
    wt["q_gain_p"] = jnp.pad(full["q_gain"], ((0, 0), (0, HP - QK)))[:, None, :]
    wt["k_gain_p"] = jnp.pad(full["k_gain"], ((0, 0), (0, HP - QK)))[:, None, :]

    disc_vjp = []
    for key in ("bbd_r", "bbd_i", "cbd_r", "cbd_i", "tab_f", "tab_r"):
        wt[key] = []
    for i in range(n_odd):
        (ar, ai, bbr, bbi), vjp = jax.vjp(_discretize, full["lambda_re"][i], full["lambda_im"][i], full["log_step"][i],
                                          full["b_re"][i], full["b_im"][i])
        disc_vjp.append(vjp)
        tf, tr = _scan_tables(ar.reshape(-1), ai.reshape(-1))
        wt["tab_f"].append(tf)
        wt["tab_r"].append(tr)
        wt["bbd_r"].append(_b_blockdiag(bbr).astype(BF16))
        wt["bbd_i"].append(_b_blockdiag(bbi).astype(BF16))
        wt["cbd_r"].append(_c_blockdiag(full["c_re"][i]).astype(BF16))
        wt["cbd_i"].append(_c_blockdiag(full["c_im"][i]).astype(BF16))

    saved = []
    for layer in range(DEPTH):
        i = layer // 2
        if layer % 2 == 0:
            fw, tok = getw("even", i, x)
            wt["attn_norm"] = full["attn_norm"] + tok
            _even_layouts(fw, wt, i)
            x, sm = _even_fwd(x, i, wt, tabs, f"l{layer}_mla")
        else:
            fw, tok = getw("odd", i, x)
            wt["ssm_norm"] = full["ssm_norm"] + tok
            for n, a in fw.items():
                wt[n][i] = a
            x, sm = _odd_fwd(x, i, wt, f"l{layer}_s5")
        fw, tok = getw("ffn", layer, x)
        wt["ffn_norm"] = full["ffn_norm"] + tok
        for n, a in fw.items():
            wt[n][layer] = a
        x, sf = _ffn_fwd(x, layer, wt, f"l{layer}_ffn")
        saved.append((sm, sf))
    dx, dxh, lslab = loss_head(x, target, name="loss_head")

    own = [n for n in ORDER if n not in sum(MATMUL_WEIGHTS.values(), ())]
    grads = {n: [None] * (DEPTH if n.startswith("ffn") else n_even) for n in own}
    tok = 0.0
    for layer in reversed(range(DEPTH)):
        i = layer // 2
        sm, sf = saved[layer]
        wt["ffn_conv_w"] = full["ffn_conv_w"] + tok
        dx, dxh, g = _ffn_bwd(dx, dxh, sf, layer, wt, f"l{layer}_ffn")
        tok = putg("ffn", layer, {n: g[n] for n in MATMUL_WEIGHTS["ffn"]})
        for n in ("ffn_norm", "ffn_conv_w"):
            grads[n][layer] = g[n]
        if layer % 2 == 0:
            wt["sconv_w"] = full["sconv_w"] + tok
            dx, dxh, g = _even_bwd(dx, dxh, sm, i, wt, tabs, f"l{layer}_mla")
            tok = putg("even", i, _even_layouts_t(g))
            for n in ("attn_norm", "cq_norm", "ckv_norm", "q_gain", "k_gain", "sconv_w"):
                grads[n][i] = g[n]
        else:
            wt["d_skip"] = full["d_skip"] + tok
            dx, dxh, g = _odd_bwd(dx, dxh, sm, i, wt, f"l{layer}_s5")
            tok = putg("odd", i, {n: g[n] for n in MATMUL_WEIGHTS["odd"]})
            dlr, dli, dls, dbr, dbi = disc_vjp[i]((g["a"][0].reshape(G, P), g["a"][1].reshape(G, P),
                                                    _b_blockdiag_t(g["bbd_r"]), _b_blockdiag_t(g["bbd_i"])))
            grads["lambda_re"][i], grads["lambda_im"][i], grads["log_step"][i] = dlr, dli, dls
            grads["b_re"][i], grads["b_im"][i] = dbr, dbi
            grads["c_re"][i], grads["c_im"][i] = _c_blockdiag_t(g["cbd_r"]), _c_blockdiag_t(g["cbd_i"])
            for n in ("ssm_norm", "d_skip"):
                grads[n][i] = g[n]
            if i == 0:
                tok = tok + putg("odd_small", 0, {n: jnp.stack(grads[n]) for n in ODD_SMALL})
    grads = {n: jnp.stack(v) for n, v in grads.items()}
    return jnp.sum(lslab), dx, grads


def kernel(x, attn_norm, mix_w_in, cq_norm, ckv_norm, w_uq, w_ukv, q_gain, k_gain, sconv_w, mix_w_out, ssm_norm, ssm_w_in, lambda_re, lambda_im, log_step, b_re, b_im, c_re, c_im, d_skip, w_glu, ffn_norm, ffn_w_up, ffn_conv_w, ffn_w_down, loss_target, m_attn_norm, m_mix_w_in, m_cq_norm, m_ckv_norm, m_w_uq, m_w_ukv, m_q_gain, m_k_gain, m_sconv_w, m_mix_w_out, m_ssm_norm, m_ssm_w_in, m_lambda_re, m_lambda_im, m_log_step, m_b_re, m_b_im, m_c_re, m_c_im, m_d_skip, m_w_glu, m_ffn_norm, m_ffn_w_up, m_ffn_conv_w, m_ffn_w_down, v_attn_norm, v_mix_w_in, v_cq_norm, v_ckv_norm, v_w_uq, v_w_ukv, v_q_gain, v_k_gain, v_sconv_w, v_mix_w_out, v_ssm_norm, v_ssm_w_in, v_lambda_re, v_lambda_im, v_log_step, v_b_re, v_b_im, v_c_re, v_c_im, v_d_skip, v_w_glu, v_ffn_norm, v_ffn_w_up, v_ffn_conv_w, v_ffn_w_down):
    args = dict(locals())
    w = {n: args[n] for n in ORDER}
    m = {n: args["m_" + n] for n in ORDER}
    v = {n: args["v_" + n] for n in ORDER}
    me = 2 * lax.axis_index("x") + lax.axis_index("y")

    axis = dict(BIG)

    gs = _chip_exchange([w[n] for n, _ in SMALL], False, "gather_w_f32")
    full = {n: w[n] for n in REPL}
    for (n, ax), g in zip(SMALL, gs):
        full[n] = _join(g, ax)
    parts = [(("even", "odd")[layer % 2], layer // 2) for layer in range(DEPTH)]
    parts = [p for layer, mixer in enumerate(parts) for p in (mixer, ("ffn", layer))]
    gathers = {}

    def start_gather(kind, idx, zero):
        shards = [(w[n][idx] + zero).astype(BF16) for n in MATMUL_WEIGHTS[kind]]
        gathers[kind, idx] = exchange_start(shards, False, f"gather_start_{kind}{idx}")

    start_gather(*parts[0], 0.0 * gs[0][(0,) * gs[0].ndim])

    def getw(kind, idx, after):
        got = exchange_wait(gathers[kind, idx], after, False, f"gather_wait_{kind}{idx}")
        nxt = parts.index((kind, idx)) + 1
        tok = 0.0
        if nxt < len(parts):
            start_gather(*parts[nxt], 0.0 * got[0][(0,) * got[0].ndim].astype(F32))
            tok = gathers[parts[nxt]][3]
        return {n: _join(g, axis[n] - 1) for n, g in zip(MATMUL_WEIGHTS[kind], got)}, tok

    scatters, early = [], []

    def putg(kind, idx, g):
        if kind == "odd_small":
            arrs = [_rows2d(g[n]) for n in ODD_SMALL]
            early.append(exchange_start(arrs, False, "gather_start_g_odd"))
            return early[0][3]
        blocks = [_split(g[n], axis[n] - 1) for n in MATMUL_WEIGHTS[kind]]
        scatters.append((kind, idx, exchange_start(blocks, True, f"scatter_start_{kind}{idx}")))
        return scatters[-1][2][3]

    sq, dx, grads = _local_step(x[0], loss_target[0], full, getw, putg)
    loss = lax.psum(0.5 * sq / D_MODEL, ("x", "y", "c"))

    late_names = [n for n in REPL + [n for n, _ in SMALL] if n not in ODD_SMALL]
    rep_names = list(ODD_SMALL) + late_names
    names = [n for n, _ in BIG] + rep_names
    late = [_rows2d(grads[n]) for n in late_names]
    late_started = exchange_start(late, False, "gather_start_g_f32")
    summed = {}
    for kind, idx, started in scatters:
        got = exchange_wait(started, dx, True, f"scatter_wait_{kind}{idx}")
        for n, sl in zip(MATMUL_WEIGHTS[kind], got):
            summed[n, idx] = sum_slots(_rows2d(sl, 1), name=f"sum_{n}{idx}")
    mine = [jnp.concatenate([summed[n, idx] for idx in range(w[n].shape[0])], axis=0) for n, _ in BIG]
    slots = (exchange_wait(early[0], dx, False, "gather_wait_g_odd")
             + exchange_wait(late_started, mine[-1], False, "gather_wait_g_f32"))
    mine += [sum_slots(_rows2d(sl, 1), name=f"sum_{n}") for n, sl in zip(rep_names, slots)]
    other = sibling_swap(mine, "swap_g")

    def local(n, p):
        ax = dict(SMALL).get(n)
        if ax is None:
            return p
        part = lax.dynamic_index_in_dim(_split(p.reshape(grads[n].shape), ax), me, 0, keepdims=False)
        return _rows2d(part)

    outs = {}
    for n, p, q in zip(names, mine, other):
        res = adamw(local(n, p), local(n, q), _rows2d(w[n]), _rows2d(m[n]), _rows2d(v[n]), name=f"adamw_{n}")
        outs[n] = [r.reshape(w[n].shape) for r in res]
    return (loss, dx[None], *[outs[n][0] for n in ORDER], *[outs[n][1] for n in ORDER],
            *[outs[n][2] for n in ORDER], *[outs[n][3] for n in ORDER])
```

```python
import functools
import math

import numpy as np
import jax
import jax.numpy as jnp
from jax import lax
from jax.experimental import pallas as pl
from jax.experimental.pallas import tpu as pltpu

F32, BF16 = jnp.float32, jnp.bfloat16

D_MODEL = 1024
DEPTH = 4
HEADS = 8
NOPE, ROPE, QK, VD = 64, 32, 96, 64
HP = 128
QR, KVR = 256, 256
CONVC = 512
FFN_H = 2816
G, P, GC = 64, 64, 16
NST = G * P
SLAB = 8
LANE = 128
EPS = 1e-6
ROPE_THETA = 10000.0
ADAM_LR, ADAM_B1, ADAM_B2, ADAM_EPS, ADAM_WD, ADAM_STEP = 0.001, 0.9, 0.999, 1e-08, 0.01, 10
VMEM_LIMIT = 48 * 1024 * 1024
MESH = pl.DeviceIdType.MESH
ANY = pl.BlockSpec(memory_space=pl.ANY)


def _cparams(*sem):
    return pltpu.CompilerParams(dimension_semantics=sem, vmem_limit_bytes=VMEM_LIMIT)


def _pick(dim, prefs):
    for p in prefs:
        if dim % p == 0:
            return p
    return dim


def _rows(s):
    return _pick(s, (512, 256, 128, 64, 32, 16, 8))


def _long_rows(s):
    return _pick(s, (1024, 512, 256, 128, 64, 32, 16, 8))


MM_VMEM_BUDGET = 36 * 1024 * 1024
MM_MAX_TILE_ELEMS = 640 * 1024
HBM_BYTES_PER_US = 3.0e6
STEP_OVERHEAD_US = 0.35


def _lane_tiles(n):
    c = {t for t in range(LANE, min(n, 1536) + 1, LANE) if n % t == 0}
    if n <= 2304 or not c:
        c.add(n)
    return sorted(c, reverse=True)


def _mm_tiles(m, n, k, sa, sb, so):
    best = None
    for tm in [t for t in (1024, 512, 256) if m % t == 0] or [m]:
        for tn in _lane_tiles(n):
            if tm * tn > MM_MAX_TILE_ELEMS:
                continue
            if 2 * (tm * k * sa + k * tn * sb + tm * tn * so) + 4 * tm * tn > MM_VMEM_BUDGET:
                continue
            steps = (m // tm) * (n // tn)
            for inner_n in (True, False):
                moved = (m * k * sa + (m // tm) * k * n * sb) if inner_n else (k * n * sb + (n // tn) * m * k * sa)
                cost = (moved + m * n * so) / HBM_BYTES_PER_US + steps * STEP_OVERHEAD_US
                if best is None or cost < best[0]:
                    best = (cost, tm, tn, inner_n)
    return best[1:]


def mm(a, b, *, ta=False, tb=False, add=None, out_dtype=F32, name):
    if ta:
        kdim, m = a.shape
    else:
        m, kdim = a.shape
    n = b.shape[0] if tb else b.shape[1]
    so = jnp.dtype(out_dtype).itemsize + (0 if add is None else add.dtype.itemsize)
    tm, tn, inner_n = _mm_tiles(m, n, kdim, a.dtype.itemsize, b.dtype.itemsize, so)
    dn = (((0 if ta else 1,), (1 if tb else 0,)), ((), ()))

    def body(*refs):
        if add is None:
            a_ref, b_ref, o_ref = refs
        else:
            a_ref, b_ref, add_ref, o_ref = refs
        r = lax.dot_general(a_ref[...].astype(BF16), b_ref[...].astype(BF16), dn, preferred_element_type=F32)
        if add is not None:
            r = r + add_ref[...].astype(F32)
        o_ref[...] = r.astype(out_dtype)

    ij = (lambda g0, g1: (g0, g1)) if inner_n else (lambda g0, g1: (g1, g0))
    a_spec = (pl.BlockSpec((kdim, tm), lambda g0, g1: (0, ij(g0, g1)[0])) if ta
              else pl.BlockSpec((tm, kdim), lambda g0, g1: (ij(g0, g1)[0], 0)))
    b_spec = (pl.BlockSpec((tn, kdim), lambda g0, g1: (ij(g0, g1)[1], 0)) if tb
              else pl.BlockSpec((kdim, tn), lambda g0, g1: (0, ij(g0, g1)[1])))
    o_spec = pl.BlockSpec((tm, tn), lambda g0, g1: ij(g0, g1))
    ins, specs = [a, b], [a_spec, b_spec]
    if add is not None:
        ins.append(add)
        specs.append(o_spec)
    grid = (m // tm, n // tn) if inner_n else (n // tn, m // tm)
    return pl.pallas_call(
        body, name=name, grid=grid, in_specs=specs, out_specs=o_spec,
        out_shape=jax.ShapeDtypeStruct((m, n), out_dtype),
        compiler_params=_cparams("parallel", "parallel"))(*ins)


def mm_nt_segments(segs, b, *, name):
    m = segs[0].shape[0]
    n, kdim = b.shape
    widths = [sg.shape[1] for sg in segs]
    offs = [sum(widths[:i]) for i in range(len(segs))]
    assert sum(widths) == kdim and all(o % wd == 0 for o, wd in zip(offs, widths))
    sa = max(sg.dtype.itemsize for sg in segs)
    tm, tn, inner_n = _mm_tiles(m, n, kdim, sa, b.dtype.itemsize, 4)
    ns = len(segs)

    def body(*refs):
        o_ref = refs[-1]
        r = None
        for a_ref, b_ref in zip(refs[:ns], refs[ns:2 * ns]):
            d = lax.dot_general(a_ref[...].astype(BF16), b_ref[...].astype(BF16), NT, preferred_element_type=F32)
            r = d if r is None else r + d
        o_ref[...] = r

    ij = (lambda g0, g1: (g0, g1)) if inner_n else (lambda g0, g1: (g1, g0))
    a_specs = [pl.BlockSpec((tm, wd), lambda g0, g1: (ij(g0, g1)[0], 0)) for wd in widths]
    b_specs = [pl.BlockSpec((tn, wd), lambda g0, g1, blk=o // wd: (ij(g0, g1)[1], blk)) for o, wd in zip(offs, widths)]
    grid = (m // tm, n // tn) if inner_n else (n // tn, m // tm)
    return pl.pallas_call(
        body, name=name, grid=grid, in_specs=a_specs + b_specs,
        out_specs=pl.BlockSpec((tm, tn), lambda g0, g1: ij(g0, g1)),
        out_shape=jax.ShapeDtypeStruct((m, n), F32),
        compiler_params=_cparams("parallel", "parallel"))(*segs, *([b] * ns))


def rms_fwd(x, g, *, col=0, out_dtype=BF16, name):
    s = x.shape[0]
    d = g.shape[1]
    tm = _rows(s)

    def body(x_ref, g_ref, o_ref):
        xv = x_ref[...]
        r = lax.rsqrt(jnp.mean(xv * xv, axis=-1, keepdims=True) + EPS)
        o_ref[...] = (xv * r * g_ref[...]).astype(out_dtype)

    return pl.pallas_call(
        body, name=name, grid=(s // tm,),
        in_specs=[pl.BlockSpec((tm, d), lambda i: (i, col)), pl.BlockSpec((1, d), lambda i: (0, 0))],
        out_specs=pl.BlockSpec((tm, d), lambda i: (i, 0)),
        out_shape=jax.ShapeDtypeStruct((s, d), out_dtype),
        compiler_params=_cparams("parallel"))(x, g)


def rms_bwd(x, g, dy, *, col=0, add=None, out_dtype=F32, twin=False, name):
    s = x.shape[0]
    d = g.shape[1]
    tm = _rows(s)

    def body(*refs):
        refs = list(refs)
        dg_ref = refs.pop()
        dxh_ref = refs.pop() if twin else None
        dx_ref = refs.pop()
        add_ref = refs.pop() if add is not None else None
        x_ref, g_ref, dy_ref = refs

        @pl.when(pl.program_id(0) == 0)
        def _():
            dg_ref[...] = jnp.zeros_like(dg_ref)

        xv = x_ref[...]
        dyv = dy_ref[...].astype(F32)
        r = lax.rsqrt(jnp.mean(xv * xv, axis=-1, keepdims=True) + EPS)
        xh = xv * r
        dg_ref[...] += jnp.sum(dyv * xh, axis=0, keepdims=True)
        dxh = dyv * g_ref[...]
        dx = r * (dxh - xh * jnp.mean(dxh * xh, axis=-1, keepdims=True))
        if add is not None:
            dx = dx + add_ref[...]
        dx_ref[...] = dx.astype(out_dtype)
        if twin:
            dxh_ref[...] = dx.astype(BF16)

    row = pl.BlockSpec((tm, d), lambda i: (i, 0))
    vec = pl.BlockSpec((1, d), lambda i: (0, 0))
    ins = [x, g, dy]
    specs = [pl.BlockSpec((tm, d), lambda i: (i, col)), vec, row]
    if add is not None:
        ins.append(add)
        specs.append(row)
    dxs = [jax.ShapeDtypeStruct((s, d), out_dtype)] + ([jax.ShapeDtypeStruct((s, d), BF16)] if twin else [])
    return pl.pallas_call(
        body, name=name, grid=(s // tm,), in_specs=specs,
        out_specs=[row] * len(dxs) + [vec],
        out_shape=dxs + [jax.ShapeDtypeStruct((1, d), F32)],
        compiler_params=_cparams("arbitrary"))(*ins)


def _rope_tables(s):
    inv = 1.0 / (ROPE_THETA ** (jnp.arange(0, ROPE, 2, dtype=F32) / ROPE))
    ang = jnp.arange(s, dtype=F32)[:, None] * inv[None, :]
    cos, sin = jnp.cos(ang), jnp.sin(ang)
    z = lambda w: jnp.zeros((s, w), F32)
    c = jnp.concatenate([jnp.ones((s, NOPE), F32), cos, cos, z(HP - QK)], axis=1)
    s1 = jnp.concatenate([z(NOPE), -sin, z(HP - NOPE - ROPE // 2)], axis=1)
    s2 = jnp.concatenate([z(NOPE + ROPE // 2), sin, z(HP - QK)], axis=1)
    return c, s1, s2


def qkprep_fwd(qraw, kv, proj, qg, kg, tabs, *, name):
    s = qraw.shape[0]
    tm = _rows(s)
    kr_col = (proj.shape[1] - HP) // HP

    def body(q_ref, k_ref, v_ref, kr_ref, qg_ref, kg_ref, c_ref, s1_ref, s2_ref, qo_ref, ko_ref, vo_ref):
        c, s1, s2 = c_ref[...], s1_ref[...], s2_ref[...]

        def f(xv, gain):
            r = lax.rsqrt(jnp.sum(xv * xv, axis=-1, keepdims=True) * (1.0 / QK) + EPS)
            xn = xv * r * gain
            return xn * c + pltpu.roll(xn, HP - ROPE // 2, 1) * s1 + pltpu.roll(xn, ROPE // 2, 1) * s2

        qo_ref[...] = (f(q_ref[...], qg_ref[...]) * Q_SCALE).astype(BF16)
        ko_ref[...] = f(k_ref[...] + kr_ref[...], kg_ref[...]).astype(BF16)
        vv = v_ref[...]
        lane = lax.broadcasted_iota(jnp.int32, vv.shape, 1)
        vo_ref[...] = jnp.where(lane == VD, 1.0, vv).astype(BF16)

    head = pl.BlockSpec((tm, HP), lambda i, h: (i, h))
    tab = pl.BlockSpec((tm, HP), lambda i, h: (i, 0))
    gain = pl.BlockSpec((1, HP), lambda i, h: (0, 0))
    return pl.pallas_call(
        body, name=name, grid=(s // tm, HEADS),
        in_specs=[head, head, pl.BlockSpec((tm, HP), lambda i, h: (i, HEADS + h)),
                  pl.BlockSpec((tm, HP), lambda i, h: (i, kr_col)), gain, gain, tab, tab, tab],
        out_specs=[head, head, head],
        out_shape=[jax.ShapeDtypeStruct((s, HEADS * HP), BF16)] * 3,
        compiler_params=_cparams("parallel", "parallel"))(qraw, kv, kv, proj, qg, kg, *tabs)


def qkprep_bwd(qraw, kv, proj, qg, kg, tabs, dq, dk, *, name):
    s = qraw.shape[0]
    tm = _rows(s)
    kr_col = (proj.shape[1] - HP) // HP

    def body(q_ref, k_ref, kr_ref, qg_ref, kg_ref, c_ref, s1_ref, s2_ref, dq_ref, dk_ref,
             dqr_ref, dkr_ref, dkrope_ref, dqg_ref, dkg_ref):
        i, h = pl.program_id(0), pl.program_id(1)
        c, s1, s2 = c_ref[...], s1_ref[...], s2_ref[...]

        @pl.when((i == 0) & (h == 0))
        def _():
            dqg_ref[...] = jnp.zeros_like(dqg_ref)
            dkg_ref[...] = jnp.zeros_like(dkg_ref)

        @pl.when(h == 0)
        def _():
            dkrope_ref[...] = jnp.zeros_like(dkrope_ref)

        def f(xv, gain, dout):
            r = lax.rsqrt(jnp.sum(xv * xv, axis=-1, keepdims=True) * (1.0 / QK) + EPS)
            xh = xv * r
            dxn = dout * c + pltpu.roll(dout * s1, ROPE // 2, 1) + pltpu.roll(dout * s2, HP - ROPE // 2, 1)
            dgain = jnp.sum(dxn * xh, axis=0, keepdims=True)
            dxh = dxn * gain
            dx = r * (dxh - xh * (jnp.sum(dxh * xh, axis=-1, keepdims=True) * (1.0 / QK)))
            return dx, dgain

        dxq, dgq = f(q_ref[...], qg_ref[...], dq_ref[...])
        dxk, dgk = f(k_ref[...] + kr_ref[...], kg_ref[...], dk_ref[...])
        dqr_ref[...] = dxq.astype(BF16)
        dkr_ref[...] = dxk.astype(BF16)
        dqg_ref[...] += dgq
        dkg_ref[...] += dgk
        lane = lax.broadcasted_iota(jnp.int32, dxk.shape, 1)
        dkrope_ref[...] += jnp.where((lane >= NOPE) & (lane < QK), dxk, 0.0)

    head = pl.BlockSpec((tm, HP), lambda i, h: (i, h))
    tab = pl.BlockSpec((tm, HP), lambda i, h: (i, 0))
    gain = pl.BlockSpec((1, HP), lambda i, h: (0, 0))
    return pl.pallas_call(
        body, name=name, grid=(s // tm, HEADS),
        in_specs=[head, head, pl.BlockSpec((tm, HP), lambda i, h: (i, kr_col)), gain, gain, tab, tab, tab, head, head],
        out_specs=[head, head, tab, gain, gain],
        out_shape=[jax.ShapeDtypeStruct((s, HEADS * HP), BF16)] * 2
        + [jax.ShapeDtypeStruct((s, HP), F32), jax.ShapeDtypeStruct((1, HP), F32), jax.ShapeDtypeStruct((1, HP), F32)],
        compiler_params=_cparams("arbitrary", "arbitrary"))(qraw, kv, proj, qg, kg, *tabs, dq, dk)


ATT_SCALE = QK ** -0.5
LOG2E = math.log2(math.e)
Q_SCALE = ATT_SCALE * LOG2E
NEG = -1e30


def _att_tile(s):
    return _pick(s, (512, 256, 128))


def _causal(sv, diag):
    r = lax.broadcasted_iota(jnp.int32, sv.shape, 0)
    c = lax.broadcasted_iota(jnp.int32, sv.shape, 1)
    return jnp.where(diag & (c > r), NEG, sv)


NT = (((1,), (1,)), ((), ()))
TN = (((0,), (0,)), ((), ()))


def _row_of(col):
    return jnp.broadcast_to(col, (col.shape[0], LANE)).T[0:SLAB, :]


def _att_specs(s):
    t = _att_tile(s)
    nb = s // t
    tile = pl.BlockSpec((t, HP), lambda h, i: (i, h))
    whole = pl.BlockSpec((s, HP), lambda h, i: (0, h))
    row = pl.BlockSpec((1, 1, SLAB, t), lambda h, i: (h, i, 0, 0))
    rows = pl.BlockSpec((1, nb, SLAB, t), lambda h, i: (h, 0, 0, 0))
    return t, nb, tile, whole, row, rows


def attn_fwd(q, k, v, *, name):
    s = q.shape[0]
    t, nb, tile, whole, row, _ = _att_specs(s)

    def body(q_ref, k_ref, v_ref, o_ref, oh_ref, lset_ref, s_scr, mb_scr, acc):
        qb = pl.program_id(1)
        qv = q_ref[...]

        def scores(j):
            r0 = pl.multiple_of(j * t, t)
            return lax.dot_general(qv, k_ref[pl.ds(r0, t), :], NT, preferred_element_type=F32)

        def fold(sv):
            m = sv[:, 0:LANE]
            for kk in range(1, t // LANE):
                m = jnp.maximum(m, sv[:, kk * LANE:(kk + 1) * LANE])
            return m

        def first(j, m):
            sv = scores(j)
            s_scr[j] = sv
            return jnp.maximum(m, fold(sv))

        m = lax.fori_loop(0, qb, first, jnp.full((t, LANE), NEG, F32))
        sd = _causal(scores(qb), True)
        s_scr[qb] = sd
        mcol = jnp.max(jnp.maximum(m, fold(sd)), axis=-1, keepdims=True)
        mb_scr[...] = jnp.broadcast_to(mcol, (t, t))
        acc[...] = jnp.zeros_like(acc)

        def second(j, carry):
            r0 = pl.multiple_of(j * t, t)
            p = jnp.exp2(s_scr[j] - mb_scr[...]).astype(BF16)
            acc[...] += jnp.dot(p, v_ref[pl.ds(r0, t), :], preferred_element_type=F32)
            return carry

        lax.fori_loop(0, qb + 1, second, 0)
        av = acc[...]
        lsum = av[:, VD:VD + 1]
        lane = lax.broadcasted_iota(jnp.int32, av.shape, 1)
        ov = jnp.where(lane == VD, 0.0, av / lsum)
        o_ref[...] = ov
        oh_ref[...] = ov.astype(BF16)
        lset_ref[0, 0] = _row_of(mcol + jnp.log2(lsum))

    return pl.pallas_call(
        body, name=name, grid=(HEADS, nb), in_specs=[tile, whole, whole], out_specs=[tile, tile, row],
        out_shape=[jax.ShapeDtypeStruct((s, HEADS * HP), F32), jax.ShapeDtypeStruct((s, HEADS * HP), BF16),
                   jax.ShapeDtypeStruct((HEADS, nb, SLAB, t), F32)],
        scratch_shapes=[pltpu.VMEM((nb, t, t), F32), pltpu.VMEM((t, t), F32), pltpu.VMEM((t, HP), F32)],
        compiler_params=_cparams("parallel", "parallel"))(q, k, v)


def attn_delta(o, do, *, name):
    s = o.shape[0]
    t, nb, tile, _, row, _ = _att_specs(s)

    def body(o_ref, do_ref, doh_ref, dt_ref):
        dov = do_ref[...]
        doh_ref[...] = dov.astype(BF16)
        dt_ref[0, 0] = _row_of(jnp.sum(dov * o_ref[...], axis=-1, keepdims=True))

    return pl.pallas_call(
        body, name=name, grid=(HEADS, nb), in_specs=[tile, tile], out_specs=[tile, row],
        out_shape=[jax.ShapeDtypeStruct((s, HEADS * HP), BF16), jax.ShapeDtypeStruct((HEADS, nb, SLAB, t), F32)],
        compiler_params=_cparams("parallel", "parallel"))(o, do)


def attn_bwd(q, k, v, doh, lset, deltat, *, name):
    s = q.shape[0]
    t, nb, tile, whole, _, rows = _att_specs(s)

    def body(q_ref, k_ref, v_ref, do_ref, lt_ref, dt_ref, dq_ref, dk_ref, dv_ref, dk_acc, dv_acc):
        kb = pl.program_id(1)
        kt, vt = k_ref[...], v_ref[...]
        dk_acc[...] = jnp.zeros_like(dk_acc)
        dv_acc[...] = jnp.zeros_like(dv_acc)

        @pl.when(kb == 0)
        def _():
            dq_ref[...] = jnp.zeros_like(dq_ref)

        def step(i, diag):
            r0 = pl.multiple_of(i * t, t)
            qi, doi = q_ref[pl.ds(r0, t), :], do_ref[pl.ds(r0, t), :]
            st = lax.dot_general(kt, qi, NT, preferred_element_type=F32)
            if diag:
                kr = lax.broadcasted_iota(jnp.int32, st.shape, 0)
                qc = lax.broadcasted_iota(jnp.int32, st.shape, 1)
                st = jnp.where(kr > qc, NEG, st)
            pt = jnp.exp2(st - lt_ref[0, i][0:1, :])
            dpt = lax.dot_general(vt, doi, NT, preferred_element_type=F32)
            dst = (pt * (dpt - dt_ref[0, i][0:1, :])).astype(BF16)
            dv_acc[...] += jnp.dot(pt.astype(BF16), doi, preferred_element_type=F32)
            dk_acc[...] += jnp.dot(dst, qi, preferred_element_type=F32)
            dq_ref[pl.ds(r0, t), :] += lax.dot_general(dst, kt, TN, preferred_element_type=F32) * ATT_SCALE

        def off_diag(i, carry):
            step(i, False)
            return carry

        step(kb, True)
        lax.fori_loop(kb + 1, nb, off_diag, 0)
        dk_ref[...] = dk_acc[...] * (1.0 / LOG2E)
        dvv = dv_acc[...]
        lane = lax.broadcasted_iota(jnp.int32, dvv.shape, 1)
        dv_ref[...] = jnp.where(lane == VD, 0.0, dvv).astype(BF16)

    return pl.pallas_call(
        body, name=name, grid=(HEADS, nb), in_specs=[whole, tile, tile, whole, rows, rows],
        out_specs=[whole, tile, tile],
        out_shape=[jax.ShapeDtypeStruct((s, HEADS * HP), F32), jax.ShapeDtypeStruct((s, HEADS * HP), F32),
                   jax.ShapeDtypeStruct((s, HEADS * HP), BF16)],
        scratch_shapes=[pltpu.VMEM((t, HP), F32), pltpu.VMEM((t, HP), F32)],
        compiler_params=_cparams("parallel", "arbitrary"))(q, k, v, doh, lset, deltat)


HALO = 8
CW = 256


def _shifts(zw):
    return zw, pltpu.roll(zw, 1, 0), pltpu.roll(zw, 2, 0)


def _conv3(sh, w):
    return w[2:3] * sh[0] + w[1:2] * sh[1] + w[0:1] * sh[2]


def _conv3_t(dc, w):
    n = dc.shape[0]
    return w[2:3] * dc + w[1:2] * pltpu.roll(dc, n - 1, 0) + w[0:1] * pltpu.roll(dc, n - 2, 0)


def _conv3_dw(dc, sh, r):
    return [jnp.sum(dc * z[HALO:HALO + r], axis=0, keepdims=True) for z in (sh[2], sh[1], sh[0])]


def _halo_specs(r, colfn):
    rb = r // HALO
    cur = pl.BlockSpec((r, CW), lambda j, i: (i, colfn(j)))
    prev = pl.BlockSpec((HALO, CW), lambda j, i: (jnp.maximum(i * rb - 1, 0), colfn(j)))

    def nxt(nrow_blocks):
        return pl.BlockSpec((HALO, CW), lambda j, i: (jnp.minimum((i + 1) * rb, nrow_blocks * rb - 1), colfn(j)))

    return cur, prev, nxt


def ffnact_fwd(up, w, *, name):
    s, c2 = up.shape
    hh = c2 // 2
    nj = hh // CW
    r = _long_rows(s)
    nt = s // r

    def body(g_ref, gp_ref, v_ref, vp_ref, wg_ref, wv_ref, o_ref):
        pm = (pl.program_id(1) > 0).astype(F32)
        cg = _conv3(_shifts(jnp.concatenate([gp_ref[...] * pm, g_ref[...]], axis=0)), wg_ref[...])[HALO:]
        cv = _conv3(_shifts(jnp.concatenate([vp_ref[...] * pm, v_ref[...]], axis=0)), wv_ref[...])[HALO:]
        o_ref[...] = (cg * jax.nn.sigmoid(cg) * cv).astype(BF16)

    gcur, gprev, _ = _halo_specs(r, lambda j: j)
    vcur, vprev, _ = _halo_specs(r, lambda j: nj + j)
    wg = pl.BlockSpec((3, CW), lambda j, i: (0, j))
    wv = pl.BlockSpec((3, CW), lambda j, i: (0, nj + j))
    return pl.pallas_call(
        body, name=name, grid=(nj, nt), in_specs=[gcur, gprev, vcur, vprev, wg, wv],
        out_specs=pl.BlockSpec((r, CW), lambda j, i: (i, j)),
        out_shape=jax.ShapeDtypeStruct((s, hh), BF16),
        compiler_params=_cparams("parallel", "parallel"))(up, up, up, up, w, w)


def ffnact_bwd(up, w, dact, *, name):
    s, c2 = up.shape
    hh = c2 // 2
    nj = hh // CW
    r = _rows(s)
    nt = s // r

    def body(g_ref, gp_ref, gn_ref, v_ref, vp_ref, vn_ref, wg_ref, wv_ref, da_ref, dan_ref,
             dg_ref, dv_ref, dwg_ref, dwv_ref):
        i = pl.program_id(1)
        pm = (i > 0).astype(F32)
        nm = (i < nt - 1).astype(F32)

        @pl.when(i == 0)
        def _():
            dwg_ref[...] = jnp.zeros_like(dwg_ref)
            dwv_ref[...] = jnp.zeros_like(dwv_ref)

        wg, wv = wg_ref[...], wv_ref[...]
        zg = jnp.concatenate([gp_ref[...] * pm, g_ref[...], gn_ref[...]], axis=0)
        zv = jnp.concatenate([vp_ref[...] * pm, v_ref[...], vn_ref[...]], axis=0)
        zg, zv = _shifts(zg), _shifts(zv)
        cg = _conv3(zg, wg)[HALO:]
        cv = _conv3(zv, wv)[HALO:]
        da = jnp.concatenate([da_ref[...], dan_ref[...] * nm], axis=0)
        sg = jax.nn.sigmoid(cg)
        dcg = da * cv * (sg * (1.0 + cg * (1.0 - sg)))
        dcv = da * (cg * sg)
        dg_ref[...] = _conv3_t(dcg, wg)[:r].astype(BF16)
        dv_ref[...] = _conv3_t(dcv, wv)[:r].astype(BF16)
        for kk, (a, b) in enumerate(zip(_conv3_dw(dcg[:r], zg, r), _conv3_dw(dcv[:r], zv, r))):
            dwg_ref[kk:kk + 1, :] += a
            dwv_ref[kk:kk + 1, :] += b

    gcur, gprev, gnext = _halo_specs(r, lambda j: j)
    vcur, vprev, vnext = _halo_specs(r, lambda j: nj + j)
    acur, _, anext = _halo_specs(r, lambda j: j)
    wg = pl.BlockSpec((3, CW), lambda j, i: (0, j))
    wv = pl.BlockSpec((3, CW), lambda j, i: (0, nj + j))
    dupg, dupv, dwg, dwv = pl.pallas_call(
        body, name=name, grid=(nj, nt),
        in_specs=[gcur, gprev, gnext(nt), vcur, vprev, vnext(nt), wg, wv, acur, anext(nt)],
        out_specs=[acur, acur, wg, wg],
        out_shape=[jax.ShapeDtypeStruct((s, hh), BF16), jax.ShapeDtypeStruct((s, hh), BF16),
                   jax.ShapeDtypeStruct((3, hh), F32), jax.ShapeDtypeStruct((3, hh), F32)],
        compiler_params=_cparams("parallel", "arbitrary"))(up, up, up, up, up, up, w, w, dact, dact)
    return (dupg, dupv), jnp.concatenate([dwg, dwv], axis=1)


def sconv_fwd(proj, w, *, name):
    s = proj.shape[0]
    nj = CONVC // CW
    r = _long_rows(s)
    nt = s // r

    def body(b_ref, c_ref, cp_ref, x_ref, xp_ref, w_ref, o_ref):
        pm = (pl.program_id(1) > 0).astype(F32)
        zw = jnp.concatenate([cp_ref[...] * xp_ref[...] * pm, c_ref[...] * x_ref[...]], axis=0)
        o_ref[...] = (b_ref[...] * _conv3(_shifts(zw), w_ref[...])[HALO:]).astype(BF16)

    bcur, _, _ = _halo_specs(r, lambda j: (QR + KVR) // CW + j)
    ccur, cprev, _ = _halo_specs(r, lambda j: (QR + KVR + CONVC) // CW + j)
    xcur, xprev, _ = _halo_specs(r, lambda j: (QR + KVR + 2 * CONVC) // CW + j)
    ws = pl.BlockSpec((3, CW), lambda j, i: (0, j))
    return pl.pallas_call(
        body, name=name, grid=(nj, nt), in_specs=[bcur, ccur, cprev, xcur, xprev, ws],
        out_specs=pl.BlockSpec((r, CW), lambda j, i: (i, j)),
        out_shape=jax.ShapeDtypeStruct((s, CONVC), BF16),
        compiler_params=_cparams("parallel", "parallel"))(proj, proj, proj, proj, proj, w)


def sconv_bwd(proj, w, dy, *, name):
    s = proj.shape[0]
    nj = CONVC // CW
    r = _rows(s)
    nt = s // r

    def body(b_ref, bn_ref, c_ref, cp_ref, x_ref, xp_ref, w_ref, dy_ref, dyn_ref, db_ref, dc_ref, dx_ref, dw_ref):
        i = pl.program_id(1)
        pm = (i > 0).astype(F32)
        nm = (i < nt - 1).astype(F32)

        @pl.when(i == 0)
        def _():
            dw_ref[...] = jnp.zeros_like(dw_ref)

        wv = w_ref[...]
        zw = jnp.concatenate([cp_ref[...] * xp_ref[...] * pm, c_ref[...] * x_ref[...]], axis=0)
        zw = _shifts(zw)
        conv = _conv3(zw, wv)[HALO:]
        dyv = dy_ref[...]
        db_ref[...] = (dyv * conv).astype(BF16)
        dconv = jnp.concatenate([dyv * b_ref[...], dyn_ref[...] * bn_ref[...] * nm], axis=0)
        dz = _conv3_t(dconv, wv)[:r]
        dc_ref[...] = (dz * x_ref[...]).astype(BF16)
        dx_ref[...] = (dz * c_ref[...]).astype(BF16)
        for kk, a in enumerate(_conv3_dw(dconv[:r], zw, r)):
            dw_ref[kk:kk + 1, :] += a

    bcur, _, bnext = _halo_specs(r, lambda j: (QR + KVR) // CW + j)
    ccur, cprev, _ = _halo_specs(r, lambda j: (QR + KVR + CONVC) // CW + j)
    xcur, xprev, _ = _halo_specs(r, lambda j: (QR + KVR + 2 * CONVC) // CW + j)
    ycur, _, ynext = _halo_specs(r, lambda j: j)
    ws = pl.BlockSpec((3, CW), lambda j, i: (0, j))
    out = pl.BlockSpec((r, CW), lambda j, i: (i, j))
    db, dc, dx, dw = pl.pallas_call(
        body, name=name, grid=(nj, nt),
        in_specs=[bcur, bnext(nt), ccur, cprev, xcur, xprev, ws, ycur, ynext(nt)],
        out_specs=[out, out, out, ws],
        out_shape=[jax.ShapeDtypeStruct((s, CONVC), BF16)] * 3 + [jax.ShapeDtypeStruct((3, CONVC), F32)],
        compiler_params=_cparams("parallel", "arbitrary"))(proj, proj, proj, proj, proj, proj, w, dy, dy)
    return (db, dc, dx), dw


SW = 512
NJ = NST // SW


def _scan_tables(ar, ai):
    def cmul(x, y):
        return x[0] * y[0] - x[1] * y[1], x[0] * y[1] + x[1] * y[0]

    def build(a, reverse):
        pw = [a]
        for _ in range(SLAB - 1):
            pw.append(cmul(pw[-1], a))
        row = jnp.arange(SLAB)[:, None]
        tabs = []
        for kk in (1, 2, 4):
            mask = ((row < SLAB - kk) if reverse else (row >= kk)).astype(F32)
            tabs += [mask * pw[kk - 1][0][None, :], mask * pw[kk - 1][1][None, :]]
        order = list(range(SLAB - 1, -1, -1)) if reverse else list(range(SLAB))
        tabs += [jnp.stack([pw[o][0] for o in order]), jnp.stack([pw[o][1] for o in order])]
        return jnp.stack(tabs)

    return build((ar, ai), False), build((ar, -ai), True)


def _slab_scan(xr, xi, tabs, cr, ci, reverse):
    for n, kk in enumerate((1, 2, 4)):
        sh = SLAB - kk if reverse else kk
        tr, ti = tabs[2 * n], tabs[2 * n + 1]
        sr, si = pltpu.roll(xr, sh, 0), pltpu.roll(xi, sh, 0)
        xr, xi = xr + tr * sr - ti * si, xi + tr * si + ti * sr
    tr, ti = tabs[6], tabs[7]
    return xr + tr * cr - ti * ci, xi + tr * ci + ti * cr


def s5_fwd(u, bbd_r, bbd_i, cbd_r, cbd_i, tab, *, name):
    s = u.shape[0]
    tbk = _long_rows(s)
    nt = s // tbk
    nsl = tbk // SLAB

    def body(u_ref, br_ref, bi_ref, cr_ref, ci_ref, tab_ref, y_ref, sr_ref, si_ref, bur, bui, carry):
        @pl.when(pl.program_id(1) == 0)
        def _():
            carry[...] = jnp.zeros_like(carry)

        ub = u_ref[...].astype(BF16)
        bur[...] = jnp.dot(ub, br_ref[0], preferred_element_type=F32)
        bui[...] = jnp.dot(ub, bi_ref[0], preferred_element_type=F32)
        tabs = [tab_ref[n] for n in range(8)]

        def slab(n, c):
            r0 = pl.multiple_of(n * SLAB, SLAB)
            sr, si = _slab_scan(bur[pl.ds(r0, SLAB), :], bui[pl.ds(r0, SLAB), :], tabs, c[0], c[1], False)
            sr_ref[pl.ds(r0, SLAB), :] = sr
            si_ref[pl.ds(r0, SLAB), :] = si
            return (jnp.broadcast_to(sr[SLAB - 1:SLAB], sr.shape), jnp.broadcast_to(si[SLAB - 1:SLAB], si.shape))

        cr, ci = lax.fori_loop(0, nsl, slab, (carry[0], carry[1]))
        carry[0] = cr
        carry[1] = ci
        y_ref[...] = (jnp.dot(sr_ref[...].astype(BF16), cr_ref[0], preferred_element_type=F32)
                      - jnp.dot(si_ref[...].astype(BF16), ci_ref[0], preferred_element_type=F32))

    us = pl.BlockSpec((tbk, LANE), lambda j, t: (t, j))
    bs = pl.BlockSpec((1, LANE, SW), lambda j, t: (j, 0, 0))
    cs = pl.BlockSpec((1, SW, LANE), lambda j, t: (j, 0, 0))
    ts = pl.BlockSpec((8, SLAB, SW), lambda j, t: (0, 0, j))
    ss = pl.BlockSpec((tbk, SW), lambda j, t: (t, j))
    return pl.pallas_call(
        body, name=name, grid=(NJ, nt), in_specs=[us, bs, bs, cs, cs, ts], out_specs=[us, ss, ss],
        out_shape=[jax.ShapeDtypeStruct((s, D_MODEL), F32), jax.ShapeDtypeStruct((s, NST), F32),
                   jax.ShapeDtypeStruct((s, NST), F32)],
        scratch_shapes=[pltpu.VMEM((tbk, SW), F32), pltpu.VMEM((tbk, SW), F32), pltpu.VMEM((2, SLAB, SW), F32)],
        compiler_params=_cparams("parallel", "arbitrary"))(u, bbd_r, bbd_i, cbd_r, cbd_i, tab)


def s5_bwd(u, dy, dskip, st_r, st_i, bbd_r, bbd_i, cbd_r, cbd_i, tabrev, *, name):
    s = u.shape[0]
    tbk = _long_rows(s)
    nt = s // tbk
    nsl = tbk // SLAB
    rbk = tbk // SLAB

    def body(u_ref, dy_ref, d_ref, sr_ref, si_ref, pr_ref, pi_ref, br_ref, bi_ref, cr_ref, ci_ref, tab_ref,
             du_ref, dbr_ref, dbi_ref, dcr_ref, dci_ref, da_ref, lam_r, lam_i, carry):
        t = pl.program_id(1)

        @pl.when(t == 0)
        def _():
            carry[...] = jnp.zeros_like(carry)
            dbr_ref[...] = jnp.zeros_like(dbr_ref)
            dbi_ref[...] = jnp.zeros_like(dbi_ref)
            dcr_ref[...] = jnp.zeros_like(dcr_ref)
            dci_ref[...] = jnp.zeros_like(dci_ref)
            da_ref[...] = jnp.zeros_like(da_ref)

        dyv = dy_ref[...]
        dyh = dyv.astype(BF16)
        lam_r[...] = lax.dot_general(dyh, cr_ref[0], NT, preferred_element_type=F32)
        lam_i[...] = -lax.dot_general(dyh, ci_ref[0], NT, preferred_element_type=F32)
        tabs = [tab_ref[n] for n in range(8)]

        def slab(n, c):
            r0 = pl.multiple_of((nsl - 1 - n) * SLAB, SLAB)
            lr, li = _slab_scan(lam_r[pl.ds(r0, SLAB), :], lam_i[pl.ds(r0, SLAB), :], tabs, c[0], c[1], True)
            lam_r[pl.ds(r0, SLAB), :] = lr
            lam_i[pl.ds(r0, SLAB), :] = li
            return (jnp.broadcast_to(lr[0:1], lr.shape), jnp.broadcast_to(li[0:1], li.shape))

        cr, ci = lax.fori_loop(0, nsl, slab, (carry[0], carry[1]))
        carry[0] = cr
        carry[1] = ci
        lr, li = lam_r[...], lam_i[...]
        lrh, lih = lr.astype(BF16), li.astype(BF16)
        du = (dyv * d_ref[...] + lax.dot_general(lrh, br_ref[0], NT, preferred_element_type=F32)
              + lax.dot_general(lih, bi_ref[0], NT, preferred_element_type=F32))
        du_ref[...] = du.astype(BF16)
        ub = u_ref[...].astype(BF16)
        dbr_ref[0] += lax.dot_general(ub, lrh, TN, preferred_element_type=F32)
        dbi_ref[0] += lax.dot_general(ub, lih, TN, preferred_element_type=F32)
        srv, siv = sr_ref[...], si_ref[...]
        dcr_ref[0] += lax.dot_general(srv.astype(BF16), dyh, TN, preferred_element_type=F32)
        dci_ref[0] -= lax.dot_general(siv.astype(BF16), dyh, TN, preferred_element_type=F32)
        first = lax.broadcasted_iota(jnp.int32, srv.shape, 0) == 0
        pm = (t < nt - 1).astype(F32)
        spr = jnp.where(first, pr_ref[SLAB - 1:SLAB, :] * pm, pltpu.roll(srv, 1, 0))
        spi = jnp.where(first, pi_ref[SLAB - 1:SLAB, :] * pm, pltpu.roll(siv, 1, 0))
        da_ref[0:1, :] += jnp.sum(lr * spr + li * spi, axis=0, keepdims=True)
        da_ref[1:2, :] += jnp.sum(li * spr - lr * spi, axis=0, keepdims=True)

    rv = lambda t: nt - 1 - t
    us = pl.BlockSpec((tbk, LANE), lambda j, t: (rv(t), j))
    ds = pl.BlockSpec((1, LANE), lambda j, t: (0, j))
    ss = pl.BlockSpec((tbk, SW), lambda j, t: (rv(t), j))
    ps = pl.BlockSpec((SLAB, SW), lambda j, t: (jnp.maximum(rv(t) * rbk - 1, 0), j))
    bs = pl.BlockSpec((1, LANE, SW), lambda j, t: (j, 0, 0))
    cs = pl.BlockSpec((1, SW, LANE), lambda j, t: (j, 0, 0))
    ts = pl.BlockSpec((8, SLAB, SW), lambda j, t: (0, 0, j))
    das = pl.BlockSpec((2, SW), lambda j, t: (0, j))
    return pl.pallas_call(
        body, name=name, grid=(NJ, nt),
        in_specs=[us, us, ds, ss, ss, ps, ps, bs, bs, cs, cs, ts],
        out_specs=[us, bs, bs, cs, cs, das],
        out_shape=[jax.ShapeDtypeStruct((s, D_MODEL), BF16),
                   jax.ShapeDtypeStruct((NJ, LANE, SW), F32), jax.ShapeDtypeStruct((NJ, LANE, SW), F32),
                   jax.ShapeDtypeStruct((NJ, SW, LANE), F32), jax.ShapeDtypeStruct((NJ, SW, LANE), F32),
                   jax.ShapeDtypeStruct((2, NST), F32)],
        scratch_shapes=[pltpu.VMEM((tbk, SW), F32), pltpu.VMEM((tbk, SW), F32), pltpu.VMEM((2, SLAB, SW), F32)],
        compiler_params=_cparams("parallel", "arbitrary"))(
            u, dy, dskip, st_r, st_i, st_r, st_i, bbd_r, bbd_i, cbd_r, cbd_i, tabrev)


GELU_C = math.sqrt(2.0 / math.pi)
GELU_A = 0.044715


def s5post_fwd(y, u, dskip, *, name):
    s = y.shape[0]
    tm = _rows(s)

    def body(y_ref, u_ref, d_ref, o_ref):
        z = y_ref[...] + d_ref[...] * u_ref[...]
        o_ref[...] = (0.5 * z * (1.0 + jnp.tanh(GELU_C * (z + GELU_A * z * z * z)))).astype(BF16)

    row = pl.BlockSpec((tm, D_MODEL), lambda i: (i, 0))
    vec = pl.BlockSpec((1, D_MODEL), lambda i: (0, 0))
    return pl.pallas_call(body, name=name, grid=(s // tm,), in_specs=[row, row, vec], out_specs=row,
                          out_shape=jax.ShapeDtypeStruct((s, D_MODEL), BF16),
                          compiler_params=_cparams("parallel"))(y, u, dskip)


def s5post_bwd(y, u, dskip, dg, *, name):
    s = y.shape[0]
    tm = _rows(s)

    def body(y_ref, u_ref, d_ref, dg_ref, dz_ref, dd_ref):
        @pl.when(pl.program_id(0) == 0)
        def _():
            dd_ref[...] = jnp.zeros_like(dd_ref)

        uv = u_ref[...]
        z = y_ref[...] + d_ref[...] * uv
        th = jnp.tanh(GELU_C * (z + GELU_A * z * z * z))
        dgelu = 0.5 * (1.0 + th) + 0.5 * z * (1.0 - th * th) * (GELU_C * (1.0 + 3.0 * GELU_A * z * z))
        dz = dg_ref[...] * dgelu
        dz_ref[...] = dz
        dd_ref[...] += jnp.sum(dz * uv, axis=0, keepdims=True)

    row = pl.BlockSpec((tm, D_MODEL), lambda i: (i, 0))
    vec = pl.BlockSpec((1, D_MODEL), lambda i: (0, 0))
    return pl.pallas_call(body, name=name, grid=(s // tm,), in_specs=[row, row, vec, row], out_specs=[row, vec],
                          out_shape=[jax.ShapeDtypeStruct((s, D_MODEL), F32), jax.ShapeDtypeStruct((1, D_MODEL), F32)],
                          compiler_params=_cparams("arbitrary"))(y, u, dskip, dg)


def glu_fwd(glu, x, *, name):
    s = x.shape[0]
    tm = _rows(s)

    def body(a_ref, b_ref, x_ref, o_ref):
        o_ref[...] = x_ref[...] + a_ref[...] * jax.nn.sigmoid(b_ref[...])

    row = pl.BlockSpec((tm, D_MODEL), lambda i: (i, 0))
    return pl.pallas_call(body, name=name, grid=(s // tm,),
                          in_specs=[row, pl.BlockSpec((tm, D_MODEL), lambda i: (i, 1)), row], out_specs=row,
                          out_shape=jax.ShapeDtypeStruct((s, D_MODEL), F32),
                          compiler_params=_cparams("parallel"))(glu, glu, x)


def glu_bwd(glu, dx, *, name):
    s = dx.shape[0]
    tm = _rows(s)

    def body(a_ref, b_ref, dx_ref, o_ref):
        sg = jax.nn.sigmoid(b_ref[...])
        dxv = dx_ref[...]
        o_ref[:, :D_MODEL] = (dxv * sg).astype(BF16)
        o_ref[:, D_MODEL:] = (dxv * a_ref[...] * sg * (1.0 - sg)).astype(BF16)

    row = pl.BlockSpec((tm, D_MODEL), lambda i: (i, 0))
    return pl.pallas_call(body, name=name, grid=(s // tm,),
                          in_specs=[row, pl.BlockSpec((tm, D_MODEL), lambda i: (i, 1)), row],
                          out_specs=pl.BlockSpec((tm, 2 * D_MODEL), lambda i: (i, 0)),
                          out_shape=jax.ShapeDtypeStruct((s, 2 * D_MODEL), BF16),
                          compiler_params=_cparams("parallel"))(glu, glu, dx)


def loss_head(y, target, *, name):
    s = y.shape[0]
    tm = _rows(s)

    def body(y_ref, t_ref, dy_ref, dyh_ref, l_ref):
        @pl.when(pl.program_id(0) == 0)
        def _():
            l_ref[...] = jnp.zeros_like(l_ref)

        e = y_ref[...] - t_ref[...]
        dy_ref[...] = e * (1.0 / D_MODEL)
        dyh_ref[...] = (e * (1.0 / D_MODEL)).astype(BF16)
        e2 = jnp.sum((e * e).reshape(tm // 8, 8, D_MODEL), axis=0)
        acc = e2[:, 0:LANE]
        for kk in range(1, D_MODEL // LANE):
            acc = acc + e2[:, kk * LANE:(kk + 1) * LANE]
        l_ref[...] += acc

    row = pl.BlockSpec((tm, D_MODEL), lambda i: (i, 0))
    return pl.pallas_call(body, name=name, grid=(s // tm,), in_specs=[row, row],
                          out_specs=[row, row, pl.BlockSpec((8, LANE), lambda i: (0, 0))],
                          out_shape=[jax.ShapeDtypeStruct((s, D_MODEL), F32), jax.ShapeDtypeStruct((s, D_MODEL), BF16),
                                     jax.ShapeDtypeStruct((8, LANE), F32)],
                          compiler_params=_cparams("arbitrary"))(y, target)


PACKW = 1024
NCHIP = 4


def _mesh_pos():
    return lax.axis_index("x"), lax.axis_index("y"), lax.axis_index("c")


def _chip_exchange(bufs, scatter, name):
    n = len(bufs)
    shapes = [b.shape[1:] if scatter else b.shape for b in bufs]

    def body(*refs):
        ins, outs = refs[:n], refs[n:2 * n]
        send_sems, recv_sems, local_sems = refs[2 * n:]
        x, y, c = _mesh_pos()
        me = 2 * x + y
        peers = [(1 - x, y), (x, 1 - y), (1 - x, 1 - y)]

        def copy(a, j, px, py, dst_slot):
            src = ins[a].at[2 * px + py] if scatter else ins[a]
            return pltpu.make_async_remote_copy(src_ref=src, dst_ref=outs[a].at[dst_slot],
                                                send_sem=send_sems.at[3 * a + j], recv_sem=recv_sems.at[3 * a + j],
                                                device_id=(px, py, c), device_id_type=MESH)

        mine = [pltpu.make_async_copy(ins[a].at[me] if scatter else ins[a], outs[a].at[me], local_sems.at[a])
                for a in range(n)]
        sends = [copy(a, j, px, py, me) for a in range(n) for j, (px, py) in enumerate(peers)]
        for cp in mine + sends:
            cp.start()
        for a in range(n):
            for j, (px, py) in enumerate(peers):
                copy(a, j, px, py, 2 * px + py).wait_recv()
        for cp in sends:
            cp.wait_send()
        for cp in mine:
            cp.wait()

    return pl.pallas_call(
        body, name=name, in_specs=[ANY] * n, out_specs=[ANY] * n,
        out_shape=[jax.ShapeDtypeStruct((NCHIP,) + tuple(shp), b.dtype) for shp, b in zip(shapes, bufs)],
        scratch_shapes=[pltpu.SemaphoreType.DMA((3 * n,)), pltpu.SemaphoreType.DMA((3 * n,)),
                        pltpu.SemaphoreType.DMA((n,))],
    )(*bufs)


HBM_SPEC = pl.BlockSpec(memory_space=pltpu.HBM)
SEM_SPEC = pl.BlockSpec(memory_space=pltpu.SEMAPHORE)
DATAFLOW = pltpu.SideEffectType.DATAFLOW_SIDE_EFFECTING


def _exchange_copy(ins, lands, send_sems, recv_sems, scatter, a, j, px, py, c, dst_slot):
    src = ins[a].at[2 * px + py] if scatter else ins[a]
    return pltpu.make_async_remote_copy(src_ref=src, dst_ref=lands[a].at[dst_slot],
                                        send_sem=send_sems.at[3 * a + j], recv_sem=recv_sems.at[3 * a + j],
                                        device_id=(px, py, c), device_id_type=MESH)


def _own_copy(ins, lands, local_sems, scatter, a, me):
    return pltpu.make_async_copy(ins[a].at[me] if scatter else ins[a], lands[a].at[me], local_sems.at[a])


def exchange_start(bufs, scatter, name):
    n = len(bufs)
    lands = [lax.empty((NCHIP,) + tuple(b.shape[1:] if scatter else b.shape), b.dtype) for b in bufs]

    def body(*refs):
        ins, lnd, send_sems, recv_sems, local_sems, token = (refs[:n], refs[n:2 * n], refs[2 * n], refs[2 * n + 1],
                                                             refs[2 * n + 2], refs[-1])
        x, y, c = _mesh_pos()
        me = 2 * x + y
        for a in range(n):
            _own_copy(ins, lnd, local_sems, scatter, a, me).start()
            for j, (px, py) in enumerate([(1 - x, y), (x, 1 - y), (1 - x, 1 - y)]):
                _exchange_copy(ins, lnd, send_sems, recv_sems, scatter, a, j, px, py, c, me).start()
        token[...] = jnp.zeros_like(token)

    thru = [pltpu.HBM(b.shape, b.dtype) for b in list(bufs) + lands]
    out = pl.pallas_call(
        body, name=name, in_specs=[HBM_SPEC] * (2 * n),
        out_specs=[SEM_SPEC] * 3 + [HBM_SPEC] * (2 * n) + [pl.BlockSpec(memory_space=pltpu.VMEM)],
        out_shape=[pltpu.SemaphoreType.DMA((3 * n,)), pltpu.SemaphoreType.DMA((3 * n,)), pltpu.SemaphoreType.DMA((n,))]
        + thru + [jax.ShapeDtypeStruct((SLAB, LANE), F32)],
        input_output_aliases={k: 3 + k for k in range(2 * n)},
        compiler_params=pltpu.CompilerParams(has_side_effects=DATAFLOW),
    )(*[pltpu.with_memory_space_constraint(b, pltpu.HBM) for b in list(bufs) + lands])
    return tuple(out[:3]), out[3:3 + n], out[3 + n:3 + 2 * n], out[-1][0, 0]


def exchange_wait(started, after, scatter, name):
    sems, bufs, lands, _ = started
    n = len(bufs)

    def body(*refs):
        ins, lnd, ssem, rsem, lsem = refs[:n], refs[n:2 * n], refs[2 * n], refs[2 * n + 1], refs[2 * n + 2]
        x, y, c = _mesh_pos()
        for a in range(n):
            _own_copy(ins, lnd, lsem, scatter, a, 2 * x + y).wait()
            for j, (px, py) in enumerate([(1 - x, y), (x, 1 - y), (1 - x, 1 - y)]):
                cp = _exchange_copy(ins, lnd, ssem, rsem, scatter, a, j, px, py, c, 2 * px + py)
                cp.wait_send()
                cp.wait_recv()

    thru = [pltpu.HBM(b.shape, b.dtype) for b in list(bufs) + list(lands)]
    out = pl.pallas_call(
        body, name=name, in_specs=[HBM_SPEC] * (2 * n) + [SEM_SPEC] * 3 + [ANY],
        out_specs=[HBM_SPEC] * (2 * n), out_shape=thru,
        input_output_aliases={k: k for k in range(2 * n)},
        compiler_params=pltpu.CompilerParams(has_side_effects=DATAFLOW),
    )(*bufs, *lands, *sems, after)
    return out[n:]


def sibling_swap(bufs, name):
    n = len(bufs)

    def body(*refs):
        ins, outs, send_sems, recv_sems = refs[:n], refs[n:2 * n], refs[2 * n], refs[2 * n + 1]
        x, y, c = _mesh_pos()
        cps = [pltpu.make_async_remote_copy(src_ref=ins[k], dst_ref=outs[k], send_sem=send_sems.at[k],
                                            recv_sem=recv_sems.at[k], device_id=(x, y, 1 - c), device_id_type=MESH)
               for k in range(n)]
        for cp in cps:
            cp.start()
        for cp in cps:
            cp.wait()

    return pl.pallas_call(
        body, name=name, in_specs=[ANY] * n, out_specs=[ANY] * n,
        out_shape=[jax.ShapeDtypeStruct(b.shape, b.dtype) for b in bufs],
        scratch_shapes=[pltpu.SemaphoreType.DMA((n,)), pltpu.SemaphoreType.DMA((n,))],
    )(*bufs)


EW_VMEM_BUDGET = 20 * 1024 * 1024


def _ew_rows(rows, w, bytes_per_elem):
    wpad = -(-w // LANE) * LANE
    for t in (1024, 512, 256, 128, 64, 32, 16, 8):
        if rows % t == 0 and 2 * t * wpad * bytes_per_elem <= EW_VMEM_BUDGET:
            return t
    return rows


def sum_slots(buf, *, name):
    _, rows, w = buf.shape
    tm = _ew_rows(rows, w, NCHIP * buf.dtype.itemsize + 4)

    def body(b_ref, o_ref):
        acc = b_ref[0].astype(F32)
        for kk in range(1, NCHIP):
            acc = acc + b_ref[kk].astype(F32)
        o_ref[...] = acc

    return pl.pallas_call(body, name=name, grid=(rows // tm,),
                          in_specs=[pl.BlockSpec((NCHIP, tm, w), lambda i: (0, i, 0))],
                          out_specs=pl.BlockSpec((tm, w), lambda i: (i, 0)),
                          out_shape=jax.ShapeDtypeStruct((rows, w), F32),
                          compiler_params=_cparams("parallel"))(buf)


def adamw(p_mine, p_other, w, m, v, *, name):
    rows, wd = w.shape
    tm = _ew_rows(rows, wd, 9 * 4)
    c1 = 1.0 - ADAM_B1 ** ADAM_STEP
    c2 = 1.0 - ADAM_B2 ** ADAM_STEP

    def body(a_ref, b_ref, w_ref, m_ref, v_ref, g_ref, d_ref, nm_ref, nv_ref):
        g = a_ref[...] + b_ref[...]
        nm = ADAM_B1 * m_ref[...] + (1.0 - ADAM_B1) * g
        nv = ADAM_B2 * v_ref[...] + (1.0 - ADAM_B2) * (g * g)
        g_ref[...] = g
        nm_ref[...] = nm
        nv_ref[...] = nv
        d_ref[...] = -ADAM_LR * ((nm / c1) / (jnp.sqrt(nv / c2) + ADAM_EPS) + ADAM_WD * w_ref[...])

    row = pl.BlockSpec((tm, wd), lambda i: (i, 0))
    return pl.pallas_call(body, name=name, grid=(rows // tm,), in_specs=[row] * 5, out_specs=[row] * 4,
                          out_shape=[jax.ShapeDtypeStruct((rows, wd), F32)] * 4,
                          compiler_params=_cparams("parallel"))(p_mine, p_other, w, m, v)


def _rows2d(a, lead=0):
    tail = a.shape[lead:]
    n = int(np.prod(tail))
    if tail[-1] < LANE // 2 and n % (8 * LANE) == 0:
        return a.reshape(a.shape[:lead] + (n // (8 * LANE), 8 * LANE))
    return a.reshape(a.shape[:lead] + (-1, tail[-1]))


BIG = [("mix_w_in", 2), ("w_uq", 2), ("w_ukv", 2), ("mix_w_out", 1), ("ssm_w_in", 1), ("w_glu", 2),
       ("ffn_w_up", 2), ("ffn_w_down", 1)]
SMALL = [("sconv_w", 2), ("ssm_norm", 1), ("d_skip", 1), ("ffn_conv_w", 2)]
REPL = ["attn_norm", "cq_norm", "ckv_norm", "q_gain", "k_gain", "lambda_re", "lambda_im", "log_step",
        "b_re", "b_im", "c_re", "c_im", "ffn_norm"]
ORDER = ["attn_norm", "mix_w_in", "cq_norm", "ckv_norm", "w_uq", "w_ukv", "q_gain", "k_gain", "sconv_w", "mix_w_out",
         "ssm_norm", "ssm_w_in", "lambda_re", "lambda_im", "log_step", "b_re", "b_im", "c_re", "c_im", "d_skip",
         "w_glu", "ffn_norm", "ffn_w_up", "ffn_conv_w", "ffn_w_down"]


def _join(g, axis):
    return jnp.concatenate([g[k] for k in range(NCHIP)], axis=axis)


def _split(full, axis, parts=NCHIP):
    return jnp.stack(jnp.split(full, parts, axis=axis))


def _discretize(lr, li, ls, b_re, b_im):
    dt = jnp.exp(ls)[:, None]
    mag = jnp.exp(lr * dt)
    ar, ai = mag * jnp.cos(li * dt), mag * jnp.sin(li * dt)
    nr, ni = ar - 1.0, ai
    den = lr * lr + li * li
    zr, zi = (nr * lr + ni * li) / den, (ni * lr - nr * li) / den
    bbar_r = zr[..., None] * b_re - zi[..., None] * b_im
    bbar_i = zr[..., None] * b_im + zi[..., None] * b_re
    return ar, ai, bbar_r, bbar_i


def _b_blockdiag(bbar):
    gl = G // NJ
    bb = bbar.reshape(NJ, gl, P, GC).transpose(0, 1, 3, 2)
    return jnp.einsum("jgcp,gh->jgchp", bb, jnp.eye(gl, dtype=bbar.dtype)).reshape(NJ, gl * GC, gl * P)


def _b_blockdiag_t(dbd):
    gl = G // NJ
    d = jnp.einsum("jgchp,gh->jgcp", dbd.reshape(NJ, gl, GC, gl, P), jnp.eye(gl, dtype=dbd.dtype))
    return d.transpose(0, 1, 3, 2).reshape(G, P, GC)


def _c_blockdiag(cmat):
    gl = G // NJ
    cc = cmat.reshape(NJ, gl, GC, P).transpose(0, 1, 3, 2)
    return jnp.einsum("jgpc,gh->jgphc", cc, jnp.eye(gl, dtype=cmat.dtype)).reshape(NJ, gl * P, gl * GC)


def _c_blockdiag_t(dbd):
    gl = G // NJ
    d = jnp.einsum("jgphc,gh->jgpc", dbd.reshape(NJ, gl, P, gl, GC), jnp.eye(gl, dtype=dbd.dtype))
    return d.transpose(0, 1, 3, 2).reshape(G, GC, P)


def _pad_heads_cols(w, width):
    r = w.shape[0]
    return jnp.pad(w.reshape(r, HEADS, width), ((0, 0), (0, 0), (0, HP - width))).reshape(r, HEADS * HP)


def _unpad_heads_cols(w, width):
    r = w.shape[0]
    return w.reshape(r, HEADS, HP)[:, :, :width].reshape(r, HEADS * width)


W_IN_SPLIT = (QR + KVR, QR + KVR + ROPE)


def _w_in_layout(w):
    a, b = W_IN_SPLIT
    kr = jnp.pad(w[:, a:b], ((0, 0), (NOPE, HP - QK)))
    return jnp.concatenate([w[:, :a], w[:, b:], kr], axis=1)


def _ffn_fwd(x, l, wt, name):
    h = rms_fwd(x, wt["ffn_norm"][l][None], name=f"{name}_norm")
    up = mm(h, wt["ffn_w_up"][l], name=f"{name}_up")
    act = ffnact_fwd(up, wt["ffn_conv_w"][l], name=f"{name}_act")
    out = mm(act, wt["ffn_w_down"][l], add=x, name=f"{name}_down")
    return out, (x, h, up, act)


def _ffn_bwd(dout, douth, saved, l, wt, name):
    x, h, up, act = saved
    g = {}
    dact = mm(douth, wt["ffn_w_down"][l], tb=True, name=f"{name}_ddown")
    g["ffn_w_down"] = mm(act, douth, ta=True, out_dtype=BF16, name=f"{name}_dwdown")
    dup, g["ffn_conv_w"] = ffnact_bwd(up, wt["ffn_conv_w"][l], dact, name=f"{name}_dact")
    g["ffn_w_up"] = tuple(mm(h, d, ta=True, out_dtype=BF16, name=f"{name}_dwup{kk}") for kk, d in enumerate(dup))
    dh = mm_nt_segments(dup, wt["ffn_w_up"][l], name=f"{name}_dup")
    dx, dxh, dg = rms_bwd(x, wt["ffn_norm"][l][None], dh, add=dout, twin=True, name=f"{name}_dnorm")
    g["ffn_norm"] = dg[0]
    return dx, dxh, g


def _even_fwd(x, i, wt, tabs, name):
    h = rms_fwd(x, wt["attn_norm"][i][None], name=f"{name}_norm")
    proj = mm(h, wt["w_in2"][i], name=f"{name}_in")
    cqn = rms_fwd(proj, wt["cq_norm"][i][None], col=0, name=f"{name}_cqnorm")
    ckvn = rms_fwd(proj, wt["ckv_norm"][i][None], col=1, name=f"{name}_ckvnorm")
    qraw = mm(cqn, wt["w_uq_p"][i], name=f"{name}_uq")
    kv = mm(ckvn, wt["w_ukv_p"][i], name=f"{name}_ukv")
    q, k, v = qkprep_fwd(qraw, kv, proj, wt["q_gain_p"][i], wt["k_gain_p"][i], tabs, name=f"{name}_qkprep")
    o, oh, lset = attn_fwd(q, k, v, name=f"{name}_attn")
    conv = sconv_fwd(proj, wt["sconv_w"][i], name=f"{name}_sconv")
    t = mm(oh, wt["w_out_a"][i], add=x, name=f"{name}_outa")
    out = mm(conv, wt["w_out_c"][i], add=t, name=f"{name}_outc")
    return out, (x, h, proj, cqn, ckvn, qraw, kv, q, k, v, o, oh, lset, conv)


def _even_bwd(dout, douth, saved, i, wt, tabs, name):
    x, h, proj, cqn, ckvn, qraw, kv, q, k, v, o, oh, lset, conv = saved
    g = {}
    do = mm(douth, wt["w_out_a"][i], tb=True, name=f"{name}_douta")
    dconv = mm(douth, wt["w_out_c"][i], tb=True, name=f"{name}_doutc")
    g["w_out_a"] = mm(oh, douth, ta=True, out_dtype=BF16, name=f"{name}_dwouta")
    g["w_out_c"] = mm(conv, douth, ta=True, out_dtype=BF16, name=f"{name}_dwoutc")
    dgates, g["sconv_w"] = sconv_bwd(proj, wt["sconv_w"][i], dconv, name=f"{name}_dsconv")
    doh, deltat = attn_delta(o, do, name=f"{name}_dattn_delta")
    dq, dk, dv = attn_bwd(q, k, v, doh, lset, deltat, name=f"{name}_dattn")
    dqraw, dkraw, dkrope, dqg, dkg = qkprep_bwd(qraw, kv, proj, wt["q_gain_p"][i], wt["k_gain_p"][i], tabs, dq, dk,
                                                name=f"{name}_dqkprep")
    g["q_gain"], g["k_gain"] = dqg[0, :QK], dkg[0, :QK]
    dcqn = mm(dqraw, wt["w_uq_p"][i], tb=True, name=f"{name}_duq")
    g["w_uq_p"] = mm(cqn, dqraw, ta=True, out_dtype=BF16, name=f"{name}_dwuq")
    dkv = (dkraw, dv)
    dckvn = mm_nt_segments(dkv, wt["w_ukv_p"][i], name=f"{name}_dukv")
    g["w_ukv_p"] = tuple(mm(ckvn, d, ta=True, out_dtype=BF16, name=f"{name}_dwukv{kk}") for kk, d in enumerate(dkv))
    dcq, dgq = rms_bwd(proj, wt["cq_norm"][i][None], dcqn, col=0, out_dtype=BF16, name=f"{name}_dcqnorm")
    dckv, dgkv = rms_bwd(proj, wt["ckv_norm"][i][None], dckvn, col=1, out_dtype=BF16, name=f"{name}_dckvnorm")
    g["cq_norm"], g["ckv_norm"] = dgq[0], dgkv[0]
    dproj = (dcq, dckv, *dgates, dkrope)
    g["w_in2"] = tuple(mm(h, d, ta=True, out_dtype=BF16, name=f"{name}_dwin{kk}") for kk, d in enumerate(dproj))
    dh = mm_nt_segments(dproj, wt["w_in2"][i], name=f"{name}_din")
    dx, dxh, dg = rms_bwd(x, wt["attn_norm"][i][None], dh, add=dout, twin=True, name=f"{name}_dnorm")
    g["attn_norm"] = dg[0]
    return dx, dxh, g


def _odd_fwd(x, i, wt, name):
    h = rms_fwd(x, wt["ssm_norm"][i][None], name=f"{name}_norm")
    u = mm(h, wt["ssm_w_in"][i], name=f"{name}_in")
    y, st_r, st_i = s5_fwd(u, wt["bbd_r"][i], wt["bbd_i"][i], wt["cbd_r"][i], wt["cbd_i"][i], wt["tab_f"][i],
                           name=f"{name}_scan")
    gl = s5post_fwd(y, u, wt["d_skip"][i][None], name=f"{name}_gelu")
    glu = mm(gl, wt["w_glu"][i], name=f"{name}_glu")
    out = glu_fwd(glu, x, name=f"{name}_gate")
    return out, (x, h, u, y, st_r, st_i, gl, glu)


def _odd_bwd(dout, douth, saved, i, wt, name):
    x, h, u, y, st_r, st_i, gl, glu = saved
    g = {}
    dglu = glu_bwd(glu, dout, name=f"{name}_dgate")
    g["w_glu"] = mm(gl, dglu, ta=True, out_dtype=BF16, name=f"{name}_dwglu")
    dgl = mm(dglu, wt["w_glu"][i], tb=True, name=f"{name}_dglu")
    dz, dd = s5post_bwd(y, u, wt["d_skip"][i][None], dgl, name=f"{name}_dgelu")
    g["d_skip"] = dd[0]
    du, g["bbd_r"], g["bbd_i"], g["cbd_r"], g["cbd_i"], g["a"] = s5_bwd(
        u, dz, wt["d_skip"][i][None], st_r, st_i, wt["bbd_r"][i], wt["bbd_i"][i], wt["cbd_r"][i], wt["cbd_i"][i],
        wt["tab_r"][i], name=f"{name}_dscan")
    g["ssm_w_in"] = mm(h, du, ta=True, out_dtype=BF16, name=f"{name}_dwin")
    dh = mm(du, wt["ssm_w_in"][i], tb=True, name=f"{name}_din")
    dx, dxh, dg = rms_bwd(x, wt["ssm_norm"][i][None], dh, add=dout, twin=True, name=f"{name}_dnorm")
    g["ssm_norm"] = dg[0]
    return dx, dxh, g


MATMUL_WEIGHTS = {"even": ("mix_w_in", "w_uq", "w_ukv", "mix_w_out"), "odd": ("ssm_w_in", "w_glu"),
                  "ffn": ("ffn_w_up", "ffn_w_down")}
ODD_SMALL = ("lambda_re", "lambda_im", "log_step", "b_re", "b_im", "c_re", "c_im", "ssm_norm", "d_skip")


def _even_layouts(fw, wt, i):
    wt["w_in2"][i] = _w_in_layout(fw["mix_w_in"])
    wt["w_uq_p"][i] = _pad_heads_cols(fw["w_uq"], QK)
    ukv = fw["w_ukv"].reshape(KVR, HEADS, NOPE + VD)
    wt["w_ukv_p"][i] = jnp.concatenate(
        [_pad_heads_cols(ukv[:, :, :NOPE].reshape(KVR, HEADS * NOPE), NOPE),
         _pad_heads_cols(ukv[:, :, NOPE:].reshape(KVR, HEADS * VD), VD)], axis=1)
    wt["w_out_a"][i] = _pad_heads_cols(fw["mix_w_out"][:HEADS * VD].T, VD).T
    wt["w_out_c"][i] = fw["mix_w_out"][HEADS * VD:]


def _even_layouts_t(g):
    dk_, dv_ = g["w_ukv_p"]
    dcq, dckv, dgb, dgc, dci, dkr = g["w_in2"]
    return {"mix_w_in": jnp.concatenate([dcq, dckv, dkr[:, NOPE:QK], dgb, dgc, dci], axis=1),
            "w_uq": _unpad_heads_cols(g["w_uq_p"], QK),
            "w_ukv": jnp.concatenate([dk_.reshape(KVR, HEADS, HP)[:, :, :NOPE], dv_.reshape(KVR, HEADS, HP)[:, :, :VD]],
                                     axis=2).reshape(KVR, HEADS * (NOPE + VD)),
            "mix_w_out": jnp.concatenate([_unpad_heads_cols(g["w_out_a"].T, VD).T, g["w_out_c"]], axis=0)}


def _local_step(x, target, full, getw, putg):
    s = x.shape[0]
    n_even = (DEPTH + 1) // 2
    n_odd = DEPTH // 2
    tabs = _rope_tables(s)
    wt = dict(full)
    for key in ("w_in2", "w_uq_p", "w_ukv_p", "w_out_a", "w_out_c") + sum(MATMUL_WEIGHTS.values(), ()):
        wt[key] = {}
    wt["q_gain_p"] = jnp.pad(full["q_gain"], ((0, 0), (0, HP - QK)))[:, None, :]
    wt["k_gain_p"] = jnp.pad(full["k_gain"], ((0, 0), (0, HP - QK)))[:, None, :]

    disc_vjp = []
    for key in ("bbd_r", "bbd_i", "cbd_r", "cbd_i", "tab_f", "tab_r"):
        wt[key] = []
    for i in range(n_odd):
        (ar, ai, bbr, bbi), vjp = jax.vjp(_discretize, full["lambda_re"][i], full["lambda_im"][i], full["log_step"][i],
                                          full["b_re"][i], full["b_im"][i])
        disc_vjp.append(vjp)
        tf, tr = _scan_tables(ar.reshape(-1), ai.reshape(-1))
        wt["tab_f"].append(tf)
        wt["tab_r"].append(tr)
        wt["bbd_r"].append(_b_blockdiag(bbr).astype(BF16))
        wt["bbd_i"].append(_b_blockdiag(bbi).astype(BF16))
        wt["cbd_r"].append(_c_blockdiag(full["c_re"][i]).astype(BF16))
        wt["cbd_i"].append(_c_blockdiag(full["c_im"][i]).astype(BF16))

    saved = []
    for layer in range(DEPTH):
        i = layer // 2
        if layer % 2 == 0:
            fw, tok = getw("even", i, x)
            wt["attn_norm"] = full["attn_norm"] + tok
            _even_layouts(fw, wt, i)
            x, sm = _even_fwd(x, i, wt, tabs, f"l{layer}_mla")
        else:
            fw, tok = getw("odd", i, x)
            wt["ssm_norm"] = full["ssm_norm"] + tok
            for n, a in fw.items():
                wt[n][i] = a
            x, sm = _odd_fwd(x, i, wt, f"l{layer}_s5")
        fw, tok = getw("ffn", layer, x)
        wt["ffn_norm"] = full["ffn_norm"] + tok
        for n, a in fw.items():
            wt[n][layer] = a
        x, sf = _ffn_fwd(x, layer, wt, f"l{layer}_ffn")
        saved.append((sm, sf))
    dx, dxh, lslab = loss_head(x, target, name="loss_head")

    own = [n for n in ORDER if n not in sum(MATMUL_WEIGHTS.values(), ())]
    grads = {n: [None] * (DEPTH if n.startswith("ffn") else n_even) for n in own}
    tok = 0.0
    for layer in reversed(range(DEPTH)):
        i = layer // 2
        sm, sf = saved[layer]
        wt["ffn_conv_w"] = full["ffn_conv_w"] + tok
        dx, dxh, g = _ffn_bwd(dx, dxh, sf, layer, wt, f"l{layer}_ffn")
        tok = putg("ffn", layer, {n: g[n] for n in MATMUL_WEIGHTS["ffn"]})
        for n in ("ffn_norm", "ffn_conv_w"):
            grads[n][layer] = g[n]
        if layer % 2 == 0:
            wt["sconv_w"] = full["sconv_w"] + tok
            dx, dxh, g = _even_bwd(dx, dxh, sm, i, wt, tabs, f"l{layer}_mla")
            tok = putg("even", i, _even_layouts_t(g))
            for n in ("attn_norm", "cq_norm", "ckv_norm", "q_gain", "k_gain", "sconv_w"):
                grads[n][i] = g[n]
        else:
            wt["d_skip"] = full["d_skip"] + tok
            dx, dxh, g = _odd_bwd(dx, dxh, sm, i, wt, f"l{layer}_s5")
            tok = putg("odd", i, {n: g[n] for n in MATMUL_WEIGHTS["odd"]})
            dlr, dli, dls, dbr, dbi = disc_vjp[i]((g["a"][0].reshape(G, P), g["a"][1].reshape(G, P),
                                                    _b_blockdiag_t(g["bbd_r"]), _b_blockdiag_t(g["bbd_i"])))
            grads["lambda_re"][i], grads["lambda_im"][i], grads["log_step"][i] = dlr, dli, dls
            grads["b_re"][i], grads["b_im"][i] = dbr, dbi
            grads["c_re"][i], grads["c_im"][i] = _c_blockdiag_t(g["cbd_r"]), _c_blockdiag_t(g["cbd_i"])
            for n in ("ssm_norm", "d_skip"):
                grads[n][i] = g[n]
            if i == 0:
                tok = tok + putg("odd_small", 0, {n: jnp.stack(grads[n]) for n in ODD_SMALL})
    grads = {n: jnp.stack(v) for n, v in grads.items()}
    return jnp.sum(lslab), dx, grads


def kernel(x, attn_norm, mix_w_in, cq_norm, ckv_norm, w_uq, w_ukv, q_gain, k_gain, sconv_w, mix_w_out, ssm_norm, ssm_w_in, lambda_re, lambda_im, log_step, b_re, b_im, c_re, c_im, d_skip, w_glu, ffn_norm, ffn_w_up, ffn_conv_w, ffn_w_down, loss_target, m_attn_norm, m_mix_w_in, m_cq_norm, m_ckv_norm, m_w_uq, m_w_ukv, m_q_gain, m_k_gain, m_sconv_w, m_mix_w_out, m_ssm_norm, m_ssm_w_in, m_lambda_re, m_lambda_im, m_log_step, m_b_re, m_b_im, m_c_re, m_c_im, m_d_skip, m_w_glu, m_ffn_norm, m_ffn_w_up, m_ffn_conv_w, m_ffn_w_down, v_attn_norm, v_mix_w_in, v_cq_norm, v_ckv_norm, v_w_uq, v_w_ukv, v_q_gain, v_k_gain, v_sconv_w, v_mix_w_out, v_ssm_norm, v_ssm_w_in, v_lambda_re, v_lambda_im, v_log_step, v_b_re, v_b_im, v_c_re, v_c_im, v_d_skip, v_w_glu, v_ffn_norm, v_ffn_w_up, v_ffn_conv_w, v_ffn_w_down):
    args = dict(locals())
    w = {n: args[n] for n in ORDER}
    m = {n: args["m_" + n] for n in ORDER}
    v = {n: args["v_" + n] for n in ORDER}
    me = 2 * lax.axis_index("x") + lax.axis_index("y")

    axis = dict(BIG)

    gs = _chip_exchange([w[n] for n, _ in SMALL], False, "gather_w_f32")
    full = {n: w[n] for n in REPL}
    for (n, ax), g in zip(SMALL, gs):
        full[n] = _join(g, ax)
    parts = [(("even", "odd")[layer % 2], layer // 2) for layer in range(DEPTH)]
    parts = [p for layer, mixer in enumerate(parts) for p in (mixer, ("ffn", layer))]
    gathers = {}

    def start_gather(kind, idx, zero):
        shards = [(w[n][idx] + zero).astype(BF16) for n in MATMUL_WEIGHTS[kind]]
        gathers[kind, idx] = exchange_start(shards, False, f"gather_start_{kind}{idx}")

    start_gather(*parts[0], 0.0 * gs[0][(0,) * gs[0].ndim])

    def getw(kind, idx, after):
        got = exchange_wait(gathers[kind, idx], after, False, f"gather_wait_{kind}{idx}")
        nxt = parts.index((kind, idx)) + 1
        tok = 0.0
        if nxt < len(parts):
            start_gather(*parts[nxt], 0.0 * got[0][(0,) * got[0].ndim].astype(F32))
            tok = gathers[parts[nxt]][3]
        return {n: _join(g, axis[n] - 1) for n, g in zip(MATMUL_WEIGHTS[kind], got)}, tok

    scatters, early = [], []

    def putg(kind, idx, g):
        if kind == "odd_small":
            arrs = [_rows2d(g[n]) for n in ODD_SMALL]
            early.append(exchange_start(arrs, False, "gather_start_g_odd"))
            return early[0][3]
        blocks = [jnp.concatenate([_split(part, axis[n] - 1, NCHIP // len(g[n])) for part in g[n]])
                  if isinstance(g[n], tuple) else _split(g[n], axis[n] - 1) for n in MATMUL_WEIGHTS[kind]]
        scatters.append((kind, idx, exchange_start(blocks, True, f"scatter_start_{kind}{idx}")))
        return scatters[-1][2][3]

    sq, dx, grads = _local_step(x[0], loss_target[0], full, getw, putg)
    loss = lax.psum(0.5 * sq / D_MODEL, ("x", "y", "c"))

    late_names = [n for n in REPL + [n for n, _ in SMALL] if n not in ODD_SMALL]
    rep_names = list(ODD_SMALL) + late_names
    names = [n for n, _ in BIG] + rep_names
    late = [_rows2d(grads[n]) for n in late_names]
    late_started = exchange_start(late, False, "gather_start_g_f32")
    summed = {}
    for kind, idx, started in scatters:
        got = exchange_wait(started, dx, True, f"scatter_wait_{kind}{idx}")
        for n, sl in zip(MATMUL_WEIGHTS[kind], got):
            summed[n, idx] = sum_slots(_rows2d(sl, 1), name=f"sum_{n}{idx}")
    mine = [jnp.concatenate([summed[n, idx] for idx in range(w[n].shape[0])], axis=0) for n, _ in BIG]
    slots = (exchange_wait(early[0], dx, False, "gather_wait_g_odd")
             + exchange_wait(late_started, mine[-1], False, "gather_wait_g_f32"))
    mine += [sum_slots(_rows2d(sl, 1), name=f"sum_{n}") for n, sl in zip(rep_names, slots)]
    other = sibling_swap(mine, "swap_g")

    def local(n, p):
        ax = dict(SMALL).get(n)
        if ax is None:
            return p
        part = lax.dynamic_index_in_dim(_split(p.reshape(grads[n].shape), ax), me, 0, keepdims=False)
        return _rows2d(part)

    outs = {}
    for n, p, q in zip(names, mine, other):
        res = adamw(local(n, p), local(n, q), _rows2d(w[n]), _rows2d(m[n]), _rows2d(v[n]), name=f"adamw_{n}")
        outs[n] = [r.reshape(w[n].shape) for r in res]
    return (loss, dx[None], *[outs[n][0] for n in ORDER], *[outs[n][1] for n in ORDER],
            *[outs[n][2] for n in ORDER], *[outs[n][3] for n in ORDER])
```

```python
import functools
import math

import numpy as np
import jax
import jax.numpy as jnp
from jax import lax
from jax.experimental import pallas as pl
from jax.experimental.pallas import tpu as pltpu

F32, BF16 = jnp.float32, jnp.bfloat16

D_MODEL = 1024
DEPTH = 4
HEADS = 8
NOPE, ROPE, QK, VD = 64, 32, 96, 64
HP = 128
QR, KVR = 256, 256
CONVC = 512
FFN_H = 2816
G, P, GC = 64, 64, 16
NST = G * P
SLAB = 8
LANE = 128
EPS = 1e-6
ROPE_THETA = 10000.0
ADAM_LR, ADAM_B1, ADAM_B2, ADAM_EPS, ADAM_WD, ADAM_STEP = 0.001, 0.9, 0.999, 1e-08, 0.01, 10
VMEM_LIMIT = 48 * 1024 * 1024
MESH = pl.DeviceIdType.MESH
ANY = pl.BlockSpec(memory_space=pl.ANY)


def _cparams(*sem):
    return pltpu.CompilerParams(dimension_semantics=sem, vmem_limit_bytes=VMEM_LIMIT)


def _pick(dim, prefs):
    for p in prefs:
        if dim % p == 0:
            return p
    return dim


def _rows(s):
    return _pick(s, (512, 256, 128, 64, 32, 16, 8))


def _long_rows(s):
    return _pick(s, (1024, 512, 256, 128, 64, 32, 16, 8))


MM_VMEM_BUDGET = 36 * 1024 * 1024
MM_MAX_TILE_ELEMS = 640 * 1024
HBM_BYTES_PER_US = 3.0e6
STEP_OVERHEAD_US = 0.35


def _lane_tiles(n):
    c = {t for t in range(LANE, min(n, 1536) + 1, LANE) if n % t == 0}
    if n <= 2304 or not c:
        c.add(n)
    return sorted(c, reverse=True)


def _mm_tiles(m, n, k, sa, sb, so):
    best = None
    for tm in [t for t in (1024, 512, 256) if m % t == 0] or [m]:
        for tn in _lane_tiles(n):
            if tm * tn > MM_MAX_TILE_ELEMS:
                continue
            if 2 * (tm * k * sa + k * tn * sb + tm * tn * so) + 4 * tm * tn > MM_VMEM_BUDGET:
                continue
            steps = (m // tm) * (n // tn)
            for inner_n in (True, False):
                moved = (m * k * sa + (m // tm) * k * n * sb) if inner_n else (k * n * sb + (n // tn) * m * k * sa)
                cost = (moved + m * n * so) / HBM_BYTES_PER_US + steps * STEP_OVERHEAD_US
                if best is None or cost < best[0]:
                    best = (cost, tm, tn, inner_n)
    return best[1:]


def mm(a, b, *, ta=False, tb=False, add=None, out_dtype=F32, name):
    if ta:
        kdim, m = a.shape
    else:
        m, kdim = a.shape
    n = b.shape[0] if tb else b.shape[1]
    so = jnp.dtype(out_dtype).itemsize + (0 if add is None else add.dtype.itemsize)
    tm, tn, inner_n = _mm_tiles(m, n, kdim, a.dtype.itemsize, b.dtype.itemsize, so)
    dn = (((0 if ta else 1,), (1 if tb else 0,)), ((), ()))

    def body(*refs):
        if add is None:
            a_ref, b_ref, o_ref = refs
        else:
            a_ref, b_ref, add_ref, o_ref = refs
        r = lax.dot_general(a_ref[...].astype(BF16), b_ref[...].astype(BF16), dn, preferred_element_type=F32)
        if add is not None:
            r = r + add_ref[...].astype(F32)
        o_ref[...] = r.astype(out_dtype)

    ij = (lambda g0, g1: (g0, g1)) if inner_n else (lambda g0, g1: (g1, g0))
    a_spec = (pl.BlockSpec((kdim, tm), lambda g0, g1: (0, ij(g0, g1)[0])) if ta
              else pl.BlockSpec((tm, kdim), lambda g0, g1: (ij(g0, g1)[0], 0)))
    b_spec = (pl.BlockSpec((tn, kdim), lambda g0, g1: (ij(g0, g1)[1], 0)) if tb
              else pl.BlockSpec((kdim, tn), lambda g0, g1: (0, ij(g0, g1)[1])))
    o_spec = pl.BlockSpec((tm, tn), lambda g0, g1: ij(g0, g1))
    ins, specs = [a, b], [a_spec, b_spec]
    if add is not None:
        ins.append(add)
        specs.append(o_spec)
    grid = (m // tm, n // tn) if inner_n else (n // tn, m // tm)
    return pl.pallas_call(
        body, name=name, grid=grid, in_specs=specs, out_specs=o_spec,
        out_shape=jax.ShapeDtypeStruct((m, n), out_dtype),
        compiler_params=_cparams("parallel", "parallel"))(*ins)


def mm_nt_segments(segs, b, *, name):
    m = segs[0].shape[0]
    n, kdim = b.shape
    widths = [sg.shape[1] for sg in segs]
    offs = [sum(widths[:i]) for i in range(len(segs))]
    assert sum(widths) == kdim and all(o % wd == 0 for o, wd in zip(offs, widths))
    sa = max(sg.dtype.itemsize for sg in segs)
    tm, tn, inner_n = _mm_tiles(m, n, kdim, sa, b.dtype.itemsize, 4)
    ns = len(segs)

    def body(*refs):
        o_ref = refs[-1]
        r = None
        for a_ref, b_ref in zip(refs[:ns], refs[ns:2 * ns]):
            d = lax.dot_general(a_ref[...].astype(BF16), b_ref[...].astype(BF16), NT, preferred_element_type=F32)
            r = d if r is None else r + d
        o_ref[...] = r

    ij = (lambda g0, g1: (g0, g1)) if inner_n else (lambda g0, g1: (g1, g0))
    a_specs = [pl.BlockSpec((tm, wd), lambda g0, g1: (ij(g0, g1)[0], 0)) for wd in widths]
    b_specs = [pl.BlockSpec((tn, wd), lambda g0, g1, blk=o // wd: (ij(g0, g1)[1], blk)) for o, wd in zip(offs, widths)]
    grid = (m // tm, n // tn) if inner_n else (n // tn, m // tm)
    return pl.pallas_call(
        body, name=name, grid=grid, in_specs=a_specs + b_specs,
        out_specs=pl.BlockSpec((tm, tn), lambda g0, g1: ij(g0, g1)),
        out_shape=jax.ShapeDtypeStruct((m, n), F32),
        compiler_params=_cparams("parallel", "parallel"))(*segs, *([b] * ns))


def rms_fwd(x, g, *, col=0, out_dtype=BF16, name):
    s = x.shape[0]
    d = g.shape[1]
    tm = _rows(s)

    def body(x_ref, g_ref, o_ref):
        xv = x_ref[...]
        r = lax.rsqrt(jnp.mean(xv * xv, axis=-1, keepdims=True) + EPS)
        o_ref[...] = (xv * r * g_ref[...]).astype(out_dtype)

    return pl.pallas_call(
        body, name=name, grid=(s // tm,),
        in_specs=[pl.BlockSpec((tm, d), lambda i: (i, col)), pl.BlockSpec((1, d), lambda i: (0, 0))],
        out_specs=pl.BlockSpec((tm, d), lambda i: (i, 0)),
        out_shape=jax.ShapeDtypeStruct((s, d), out_dtype),
        compiler_params=_cparams("parallel"))(x, g)


def rms_bwd(x, g, dy, *, col=0, add=None, out_dtype=F32, twin=False, name):
    s = x.shape[0]
    d = g.shape[1]
    tm = _rows(s)

    def body(*refs):
        refs = list(refs)
        dg_ref = refs.pop()
        dxh_ref = refs.pop() if twin else None
        dx_ref = refs.pop()
        add_ref = refs.pop() if add is not None else None
        x_ref, g_ref, dy_ref = refs

        @pl.when(pl.program_id(0) == 0)
        def _():
            dg_ref[...] = jnp.zeros_like(dg_ref)

        xv = x_ref[...]
        dyv = dy_ref[...].astype(F32)
        r = lax.rsqrt(jnp.mean(xv * xv, axis=-1, keepdims=True) + EPS)
        xh = xv * r
        dg_ref[...] += jnp.sum(dyv * xh, axis=0, keepdims=True)
        dxh = dyv * g_ref[...]
        dx = r * (dxh - xh * jnp.mean(dxh * xh, axis=-1, keepdims=True))
        if add is not None:
            dx = dx + add_ref[...]
        dx_ref[...] = dx.astype(out_dtype)
        if twin:
            dxh_ref[...] = dx.astype(BF16)

    row = pl.BlockSpec((tm, d), lambda i: (i, 0))
    vec = pl.BlockSpec((1, d), lambda i: (0, 0))
    ins = [x, g, dy]
    specs = [pl.BlockSpec((tm, d), lambda i: (i, col)), vec, row]
    if add is not None:
        ins.append(add)
        specs.append(row)
    dxs = [jax.ShapeDtypeStruct((s, d), out_dtype)] + ([jax.ShapeDtypeStruct((s, d), BF16)] if twin else [])
    return pl.pallas_call(
        body, name=name, grid=(s // tm,), in_specs=specs,
        out_specs=[row] * len(dxs) + [vec],
        out_shape=dxs + [jax.ShapeDtypeStruct((1, d), F32)],
        compiler_params=_cparams("arbitrary"))(*ins)


def _rope_tables(s):
    inv = 1.0 / (ROPE_THETA ** (jnp.arange(0, ROPE, 2, dtype=F32) / ROPE))
    ang = jnp.arange(s, dtype=F32)[:, None] * inv[None, :]
    cos, sin = jnp.cos(ang), jnp.sin(ang)
    z = lambda w: jnp.zeros((s, w), F32)
    c = jnp.concatenate([jnp.ones((s, NOPE), F32), cos, cos, z(HP - QK)], axis=1)
    s1 = jnp.concatenate([z(NOPE), -sin, z(HP - NOPE - ROPE // 2)], axis=1)
    s2 = jnp.concatenate([z(NOPE + ROPE // 2), sin, z(HP - QK)], axis=1)
    return c, s1, s2


def qkprep_fwd(qraw, kv, proj, qg, kg, tabs, *, name):
    s = qraw.shape[0]
    tm = _rows(s)
    kr_col = (proj.shape[1] - HP) // HP

    def body(q_ref, k_ref, v_ref, kr_ref, qg_ref, kg_ref, c_ref, s1_ref, s2_ref, qo_ref, ko_ref, vo_ref):
        c, s1, s2 = c_ref[...], s1_ref[...], s2_ref[...]

        def f(xv, gain):
            r = lax.rsqrt(jnp.sum(xv * xv, axis=-1, keepdims=True) * (1.0 / QK) + EPS)
            xn = xv * r * gain
            return xn * c + pltpu.roll(xn, HP - ROPE // 2, 1) * s1 + pltpu.roll(xn, ROPE // 2, 1) * s2

        qo_ref[...] = (f(q_ref[...], qg_ref[...]) * Q_SCALE).astype(BF16)
        ko_ref[...] = f(k_ref[...] + kr_ref[...], kg_ref[...]).astype(BF16)
        vv = v_ref[...]
        lane = lax.broadcasted_iota(jnp.int32, vv.shape, 1)
        vo_ref[...] = jnp.where(lane == VD, 1.0, vv).astype(BF16)

    head = pl.BlockSpec((tm, HP), lambda i, h: (i, h))
    tab = pl.BlockSpec((tm, HP), lambda i, h: (i, 0))
    gain = pl.BlockSpec((1, HP), lambda i, h: (0, 0))
    return pl.pallas_call(
        body, name=name, grid=(s // tm, HEADS),
        in_specs=[head, head, pl.BlockSpec((tm, HP), lambda i, h: (i, HEADS + h)),
                  pl.BlockSpec((tm, HP), lambda i, h: (i, kr_col)), gain, gain, tab, tab, tab],
        out_specs=[head, head, head],
        out_shape=[jax.ShapeDtypeStruct((s, HEADS * HP), BF16)] * 3,
        compiler_params=_cparams("parallel", "parallel"))(qraw, kv, kv, proj, qg, kg, *tabs)


def qkprep_bwd(qraw, kv, proj, qg, kg, tabs, dq, dk, *, name):
    s = qraw.shape[0]
    tm = _rows(s)
    kr_col = (proj.shape[1] - HP) // HP

    def body(q_ref, k_ref, kr_ref, qg_ref, kg_ref, c_ref, s1_ref, s2_ref, dq_ref, dk_ref,
             dqr_ref, dkr_ref, dkrope_ref, dqg_ref, dkg_ref):
        i, h = pl.program_id(0), pl.program_id(1)
        c, s1, s2 = c_ref[...], s1_ref[...], s2_ref[...]

        @pl.when((i == 0) & (h == 0))
        def _():
            dqg_ref[...] = jnp.zeros_like(dqg_ref)
            dkg_ref[...] = jnp.zeros_like(dkg_ref)

        @pl.when(h == 0)
        def _():
            dkrope_ref[...] = jnp.zeros_like(dkrope_ref)

        def f(xv, gain, dout):
            r = lax.rsqrt(jnp.sum(xv * xv, axis=-1, keepdims=True) * (1.0 / QK) + EPS)
            xh = xv * r
            dxn = dout * c + pltpu.roll(dout * s1, ROPE // 2, 1) + pltpu.roll(dout * s2, HP - ROPE // 2, 1)
            dgain = jnp.sum(dxn * xh, axis=0, keepdims=True)
            dxh = dxn * gain
            dx = r * (dxh - xh * (jnp.sum(dxh * xh, axis=-1, keepdims=True) * (1.0 / QK)))
            return dx, dgain

        dxq, dgq = f(q_ref[...], qg_ref[...], dq_ref[...])
        dxk, dgk = f(k_ref[...] + kr_ref[...], kg_ref[...], dk_ref[...])
        dqr_ref[...] = dxq.astype(BF16)
        dkr_ref[...] = dxk.astype(BF16)
        dqg_ref[...] += dgq
        dkg_ref[...] += dgk
        lane = lax.broadcasted_iota(jnp.int32, dxk.shape, 1)
        dkrope_ref[...] += jnp.where((lane >= NOPE) & (lane < QK), dxk, 0.0)

    head = pl.BlockSpec((tm, HP), lambda i, h: (i, h))
    tab = pl.BlockSpec((tm, HP), lambda i, h: (i, 0))
    gain = pl.BlockSpec((1, HP), lambda i, h: (0, 0))
    return pl.pallas_call(
        body, name=name, grid=(s // tm, HEADS),
        in_specs=[head, head, pl.BlockSpec((tm, HP), lambda i, h: (i, kr_col)), gain, gain, tab, tab, tab, head, head],
        out_specs=[head, head, tab, gain, gain],
        out_shape=[jax.ShapeDtypeStruct((s, HEADS * HP), BF16)] * 2
        + [jax.ShapeDtypeStruct((s, HP), F32), jax.ShapeDtypeStruct((1, HP), F32), jax.ShapeDtypeStruct((1, HP), F32)],
        compiler_params=_cparams("arbitrary", "arbitrary"))(qraw, kv, proj, qg, kg, *tabs, dq, dk)


ATT_SCALE = QK ** -0.5
LOG2E = math.log2(math.e)
Q_SCALE = ATT_SCALE * LOG2E
NEG = -1e30


def _att_tile(s):
    return _pick(s, (512, 256, 128))


def _causal(sv, diag):
    r = lax.broadcasted_iota(jnp.int32, sv.shape, 0)
    c = lax.broadcasted_iota(jnp.int32, sv.shape, 1)
    return jnp.where(diag & (c > r), NEG, sv)


NT = (((1,), (1,)), ((), ()))
TN = (((0,), (0,)), ((), ()))


def _row_of(col):
    return jnp.broadcast_to(col, (col.shape[0], LANE)).T[0:SLAB, :]


def _att_specs(s):
    t = _att_tile(s)
    nb = s // t
    tile = pl.BlockSpec((t, HP), lambda h, i: (i, h))
    whole = pl.BlockSpec((s, HP), lambda h, i: (0, h))
    row = pl.BlockSpec((1, 1, SLAB, t), lambda h, i: (h, i, 0, 0))
    rows = pl.BlockSpec((1, nb, SLAB, t), lambda h, i: (h, 0, 0, 0))
    return t, nb, tile, whole, row, rows


def attn_fwd(q, k, v, *, name):
    s = q.shape[0]
    t, nb, tile, whole, row, _ = _att_specs(s)

    def body(q_ref, k_ref, v_ref, o_ref, oh_ref, lset_ref, s_scr, mb_scr, acc):
        qb = pl.program_id(1)
        qv = q_ref[...]

        def scores(j):
            r0 = pl.multiple_of(j * t, t)
            return lax.dot_general(qv, k_ref[pl.ds(r0, t), :], NT, preferred_element_type=F32)

        def fold(sv):
            m = sv[:, 0:LANE]
            for kk in range(1, t // LANE):
                m = jnp.maximum(m, sv[:, kk * LANE:(kk + 1) * LANE])
            return m

        def first(j, m):
            sv = scores(j)
            s_scr[j] = sv
            return jnp.maximum(m, fold(sv))

        m = lax.fori_loop(0, qb, first, jnp.full((t, LANE), NEG, F32))
        sd = _causal(scores(qb), True)
        s_scr[qb] = sd
        mcol = jnp.max(jnp.maximum(m, fold(sd)), axis=-1, keepdims=True)
        mb_scr[...] = jnp.broadcast_to(mcol, (t, t))
        acc[...] = jnp.zeros_like(acc)

        def second(j, carry):
            r0 = pl.multiple_of(j * t, t)
            p = jnp.exp2(s_scr[j] - mb_scr[...]).astype(BF16)
            acc[...] += jnp.dot(p, v_ref[pl.ds(r0, t), :], preferred_element_type=F32)
            return carry

        lax.fori_loop(0, qb + 1, second, 0)
        av = acc[...]
        lsum = av[:, VD:VD + 1]
        lane = lax.broadcasted_iota(jnp.int32, av.shape, 1)
        ov = jnp.where(lane == VD, 0.0, av / lsum)
        o_ref[...] = ov
        oh_ref[...] = ov.astype(BF16)
        lset_ref[0, 0] = _row_of(mcol + jnp.log2(lsum))

    return pl.pallas_call(
        body, name=name, grid=(HEADS, nb), in_specs=[tile, whole, whole], out_specs=[tile, tile, row],
        out_shape=[jax.ShapeDtypeStruct((s, HEADS * HP), F32), jax.ShapeDtypeStruct((s, HEADS * HP), BF16),
                   jax.ShapeDtypeStruct((HEADS, nb, SLAB, t), F32)],
        scratch_shapes=[pltpu.VMEM((nb, t, t), F32), pltpu.VMEM((t, t), F32), pltpu.VMEM((t, HP), F32)],
        compiler_params=_cparams("parallel", "parallel"))(q, k, v)


def attn_delta(o, do, *, name):
    s = o.shape[0]
    t, nb, tile, _, row, _ = _att_specs(s)

    def body(o_ref, do_ref, doh_ref, dt_ref):
        dov = do_ref[...]
        doh_ref[...] = dov.astype(BF16)
        dt_ref[0, 0] = _row_of(jnp.sum(dov * o_ref[...], axis=-1, keepdims=True))

    return pl.pallas_call(
        body, name=name, grid=(HEADS, nb), in_specs=[tile, tile], out_specs=[tile, row],
        out_shape=[jax.ShapeDtypeStruct((s, HEADS * HP), BF16), jax.ShapeDtypeStruct((HEADS, nb, SLAB, t), F32)],
        compiler_params=_cparams("parallel", "parallel"))(o, do)


def attn_bwd(q, k, v, doh, lset, deltat, *, name):
    s = q.shape[0]
    t, nb, tile, whole, _, rows = _att_specs(s)

    def body(q_ref, k_ref, v_ref, do_ref, lt_ref, dt_ref, dq_ref, dk_ref, dv_ref, dk_acc, dv_acc):
        kb = pl.program_id(1)
        kt, vt = k_ref[...], v_ref[...]
        dk_acc[...] = jnp.zeros_like(dk_acc)
        dv_acc[...] = jnp.zeros_like(dv_acc)

        @pl.when(kb == 0)
        def _():
            dq_ref[...] = jnp.zeros_like(dq_ref)

        def step(i, diag):
            r0 = pl.multiple_of(i * t, t)
            qi, doi = q_ref[pl.ds(r0, t), :], do_ref[pl.ds(r0, t), :]
            st = lax.dot_general(kt, qi, NT, preferred_element_type=F32)
            if diag:
                kr = lax.broadcasted_iota(jnp.int32, st.shape, 0)
                qc = lax.broadcasted_iota(jnp.int32, st.shape, 1)
                st = jnp.where(kr > qc, NEG, st)
            pt = jnp.exp2(st - lt_ref[0, i][0:1, :])
            dpt = lax.dot_general(vt, doi, NT, preferred_element_type=F32)
            dst = (pt * (dpt - dt_ref[0, i][0:1, :])).astype(BF16)
            dv_acc[...] += jnp.dot(pt.astype(BF16), doi, preferred_element_type=F32)
            dk_acc[...] += jnp.dot(dst, qi, preferred_element_type=F32)
            dq_ref[pl.ds(r0, t), :] += lax.dot_general(dst, kt, TN, preferred_element_type=F32) * ATT_SCALE

        def off_diag(i, carry):
            step(i, False)
            return carry

        step(kb, True)
        lax.fori_loop(kb + 1, nb, off_diag, 0)
        dk_ref[...] = dk_acc[...] * (1.0 / LOG2E)
        dvv = dv_acc[...]
        lane = lax.broadcasted_iota(jnp.int32, dvv.shape, 1)
        dv_ref[...] = jnp.where(lane == VD, 0.0, dvv).astype(BF16)

    return pl.pallas_call(
        body, name=name, grid=(HEADS, nb), in_specs=[whole, tile, tile, whole, rows, rows],
        out_specs=[whole, tile, tile],
        out_shape=[jax.ShapeDtypeStruct((s, HEADS * HP), F32), jax.ShapeDtypeStruct((s, HEADS * HP), F32),
                   jax.ShapeDtypeStruct((s, HEADS * HP), BF16)],
        scratch_shapes=[pltpu.VMEM((t, HP), F32), pltpu.VMEM((t, HP), F32)],
        compiler_params=_cparams("parallel", "arbitrary"))(q, k, v, doh, lset, deltat)


HALO = 8
CW = 256


def _shifts(zw):
    return zw, pltpu.roll(zw, 1, 0), pltpu.roll(zw, 2, 0)


def _conv3(sh, w):
    return w[2:3] * sh[0] + w[1:2] * sh[1] + w[0:1] * sh[2]


def _conv3_t(dc, w):
    n = dc.shape[0]
    return w[2:3] * dc + w[1:2] * pltpu.roll(dc, n - 1, 0) + w[0:1] * pltpu.roll(dc, n - 2, 0)


def _conv3_dw(dc, sh, r):
    return [jnp.sum(dc * z[HALO:HALO + r], axis=0, keepdims=True) for z in (sh[2], sh[1], sh[0])]


def _halo_specs(r, colfn):
    rb = r // HALO
    cur = pl.BlockSpec((r, CW), lambda j, i: (i, colfn(j)))
    prev = pl.BlockSpec((HALO, CW), lambda j, i: (jnp.maximum(i * rb - 1, 0), colfn(j)))

    def nxt(nrow_blocks):
        return pl.BlockSpec((HALO, CW), lambda j, i: (jnp.minimum((i + 1) * rb, nrow_blocks * rb - 1), colfn(j)))

    return cur, prev, nxt


def ffnact_fwd(up, w, *, name):
    s, c2 = up.shape
    hh = c2 // 2
    nj = hh // CW
    r = _long_rows(s)
    nt = s // r

    def body(g_ref, gp_ref, v_ref, vp_ref, wg_ref, wv_ref, o_ref):
        pm = (pl.program_id(1) > 0).astype(F32)
        cg = _conv3(_shifts(jnp.concatenate([gp_ref[...] * pm, g_ref[...]], axis=0)), wg_ref[...])[HALO:]
        cv = _conv3(_shifts(jnp.concatenate([vp_ref[...] * pm, v_ref[...]], axis=0)), wv_ref[...])[HALO:]
        o_ref[...] = (cg * jax.nn.sigmoid(cg) * cv).astype(BF16)

    gcur, gprev, _ = _halo_specs(r, lambda j: j)
    vcur, vprev, _ = _halo_specs(r, lambda j: nj + j)
    wg = pl.BlockSpec((3, CW), lambda j, i: (0, j))
    wv = pl.BlockSpec((3, CW), lambda j, i: (0, nj + j))
    return pl.pallas_call(
        body, name=name, grid=(nj, nt), in_specs=[gcur, gprev, vcur, vprev, wg, wv],
        out_specs=pl.BlockSpec((r, CW), lambda j, i: (i, j)),
        out_shape=jax.ShapeDtypeStruct((s, hh), BF16),
        compiler_params=_cparams("parallel", "parallel"))(up, up, up, up, w, w)


def ffnact_bwd(up, w, dact, *, name):
    s, c2 = up.shape
    hh = c2 // 2
    nj = hh // CW
    r = _rows(s)
    nt = s // r

    def body(g_ref, gp_ref, gn_ref, v_ref, vp_ref, vn_ref, wg_ref, wv_ref, da_ref, dan_ref,
             dg_ref, dv_ref, dwg_ref, dwv_ref):
        i = pl.program_id(1)
        pm = (i > 0).astype(F32)
        nm = (i < nt - 1).astype(F32)

        @pl.when(i == 0)
        def _():
            dwg_ref[...] = jnp.zeros_like(dwg_ref)
            dwv_ref[...] = jnp.zeros_like(dwv_ref)

        wg, wv = wg_ref[...], wv_ref[...]
        zg = jnp.concatenate([gp_ref[...] * pm, g_ref[...], gn_ref[...]], axis=0)
        zv = jnp.concatenate([vp_ref[...] * pm, v_ref[...], vn_ref[...]], axis=0)
        zg, zv = _shifts(zg), _shifts(zv)
        cg = _conv3(zg, wg)[HALO:]
        cv = _conv3(zv, wv)[HALO:]
        da = jnp.concatenate([da_ref[...], dan_ref[...] * nm], axis=0)
        sg = jax.nn.sigmoid(cg)
        dcg = da * cv * (sg * (1.0 + cg * (1.0 - sg)))
        dcv = da * (cg * sg)
        dg_ref[...] = _conv3_t(dcg, wg)[:r].astype(BF16)
        dv_ref[...] = _conv3_t(dcv, wv)[:r].astype(BF16)
        for kk, (a, b) in enumerate(zip(_conv3_dw(dcg[:r], zg, r), _conv3_dw(dcv[:r], zv, r))):
            dwg_ref[kk:kk + 1, :] += a
            dwv_ref[kk:kk + 1, :] += b

    gcur, gprev, gnext = _halo_specs(r, lambda j: j)
    vcur, vprev, vnext = _halo_specs(r, lambda j: nj + j)
    acur, _, anext = _halo_specs(r, lambda j: j)
    wg = pl.BlockSpec((3, CW), lambda j, i: (0, j))
    wv = pl.BlockSpec((3, CW), lambda j, i: (0, nj + j))
    dupg, dupv, dwg, dwv = pl.pallas_call(
        body, name=name, grid=(nj, nt),
        in_specs=[gcur, gprev, gnext(nt), vcur, vprev, vnext(nt), wg, wv, acur, anext(nt)],
        out_specs=[acur, acur, wg, wg],
        out_shape=[jax.ShapeDtypeStruct((s, hh), BF16), jax.ShapeDtypeStruct((s, hh), BF16),
                   jax.ShapeDtypeStruct((3, hh), F32), jax.ShapeDtypeStruct((3, hh), F32)],
        compiler_params=_cparams("parallel", "arbitrary"))(up, up, up, up, up, up, w, w, dact, dact)
    return (dupg, dupv), jnp.concatenate([dwg, dwv], axis=1)


def sconv_fwd(proj, w, *, name):
    s = proj.shape[0]
    nj = CONVC // CW
    r = _long_rows(s)
    nt = s // r

    def body(b_ref, c_ref, cp_ref, x_ref, xp_ref, w_ref, o_ref):
        pm = (pl.program_id(1) > 0).astype(F32)
        zw = jnp.concatenate([cp_ref[...] * xp_ref[...] * pm, c_ref[...] * x_ref[...]], axis=0)
        o_ref[...] = (b_ref[...] * _conv3(_shifts(zw), w_ref[...])[HALO:]).astype(BF16)

    bcur, _, _ = _halo_specs(r, lambda j: (QR + KVR) // CW + j)
    ccur, cprev, _ = _halo_specs(r, lambda j: (QR + KVR + CONVC) // CW + j)
    xcur, xprev, _ = _halo_specs(r, lambda j: (QR + KVR + 2 * CONVC) // CW + j)
    ws = pl.BlockSpec((3, CW), lambda j, i: (0, j))
    return pl.pallas_call(
        body, name=name, grid=(nj, nt), in_specs=[bcur, ccur, cprev, xcur, xprev, ws],
        out_specs=pl.BlockSpec((r, CW), lambda j, i: (i, j)),
        out_shape=jax.ShapeDtypeStruct((s, CONVC), BF16),
        compiler_params=_cparams("parallel", "parallel"))(proj, proj, proj, proj, proj, w)


def sconv_bwd(proj, w, dy, *, name):
    s = proj.shape[0]
    nj = CONVC // CW
    r = _rows(s)
    nt = s // r

    def body(b_ref, bn_ref, c_ref, cp_ref, x_ref, xp_ref, w_ref, dy_ref, dyn_ref, db_ref, dc_ref, dx_ref, dw_ref):
        i = pl.program_id(1)
        pm = (i > 0).astype(F32)
        nm = (i < nt - 1).astype(F32)

        @pl.when(i == 0)
        def _():
            dw_ref[...] = jnp.zeros_like(dw_ref)

        wv = w_ref[...]
        zw = jnp.concatenate([cp_ref[...] * xp_ref[...] * pm, c_ref[...] * x_ref[...]], axis=0)
        zw = _shifts(zw)
        conv = _conv3(zw, wv)[HALO:]
        dyv = dy_ref[...]
        db_ref[...] = (dyv * conv).astype(BF16)
        dconv = jnp.concatenate([dyv * b_ref[...], dyn_ref[...] * bn_ref[...] * nm], axis=0)
        dz = _conv3_t(dconv, wv)[:r]
        dc_ref[...] = (dz * x_ref[...]).astype(BF16)
        dx_ref[...] = (dz * c_ref[...]).astype(BF16)
        for kk, a in enumerate(_conv3_dw(dconv[:r], zw, r)):
            dw_ref[kk:kk + 1, :] += a

    bcur, _, bnext = _halo_specs(r, lambda j: (QR + KVR) // CW + j)
    ccur, cprev, _ = _halo_specs(r, lambda j: (QR + KVR + CONVC) // CW + j)
    xcur, xprev, _ = _halo_specs(r, lambda j: (QR + KVR + 2 * CONVC) // CW + j)
    ycur, _, ynext = _halo_specs(r, lambda j: j)
    ws = pl.BlockSpec((3, CW), lambda j, i: (0, j))
    out = pl.BlockSpec((r, CW), lambda j, i: (i, j))
    db, dc, dx, dw = pl.pallas_call(
        body, name=name, grid=(nj, nt),
        in_specs=[bcur, bnext(nt), ccur, cprev, xcur, xprev, ws, ycur, ynext(nt)],
        out_specs=[out, out, out, ws],
        out_shape=[jax.ShapeDtypeStruct((s, CONVC), BF16)] * 3 + [jax.ShapeDtypeStruct((3, CONVC), F32)],
        compiler_params=_cparams("parallel", "arbitrary"))(proj, proj, proj, proj, proj, proj, w, dy, dy)
    return (db, dc, dx), dw


SW = 512
NJ = NST // SW


def _scan_tables(ar, ai):
    def cmul(x, y):
        return x[0] * y[0] - x[1] * y[1], x[0] * y[1] + x[1] * y[0]

    def build(a, reverse):
        pw = [a]
        for _ in range(SLAB - 1):
            pw.append(cmul(pw[-1], a))
        row = jnp.arange(SLAB)[:, None]
        tabs = []
        for kk in (1, 2, 4):
            mask = ((row < SLAB - kk) if reverse else (row >= kk)).astype(F32)
            tabs += [mask * pw[kk - 1][0][None, :], mask * pw[kk - 1][1][None, :]]
        order = list(range(SLAB - 1, -1, -1)) if reverse else list(range(SLAB))
        tabs += [jnp.stack([pw[o][0] for o in order]), jnp.stack([pw[o][1] for o in order])]
        return jnp.stack(tabs)

    return build((ar, ai), False), build((ar, -ai), True)


def _slab_scan(xr, xi, tabs, cr, ci, reverse):
    for n, kk in enumerate((1, 2, 4)):
        sh = SLAB - kk if reverse else kk
        tr, ti = tabs[2 * n], tabs[2 * n + 1]
        sr, si = pltpu.roll(xr, sh, 0), pltpu.roll(xi, sh, 0)
        xr, xi = xr + tr * sr - ti * si, xi + tr * si + ti * sr
    tr, ti = tabs[6], tabs[7]
    return xr + tr * cr - ti * ci, xi + tr * ci + ti * cr


def s5_fwd(u, bbd_r, bbd_i, cbd_r, cbd_i, tab, *, name):
    s = u.shape[0]
    tbk = _long_rows(s)
    nt = s // tbk
    nsl = tbk // SLAB

    def body(u_ref, br_ref, bi_ref, cr_ref, ci_ref, tab_ref, y_ref, sr_ref, si_ref, bur, bui, carry):
        @pl.when(pl.program_id(1) == 0)
        def _():
            carry[...] = jnp.zeros_like(carry)

        ub = u_ref[...].astype(BF16)
        bur[...] = jnp.dot(ub, br_ref[0], preferred_element_type=F32)
        bui[...] = jnp.dot(ub, bi_ref[0], preferred_element_type=F32)
        tabs = [tab_ref[n] for n in range(8)]

        def slab(n, c):
            r0 = pl.multiple_of(n * SLAB, SLAB)
            sr, si = _slab_scan(bur[pl.ds(r0, SLAB), :], bui[pl.ds(r0, SLAB), :], tabs, c[0], c[1], False)
            sr_ref[pl.ds(r0, SLAB), :] = sr
            si_ref[pl.ds(r0, SLAB), :] = si
            return (jnp.broadcast_to(sr[SLAB - 1:SLAB], sr.shape), jnp.broadcast_to(si[SLAB - 1:SLAB], si.shape))

        cr, ci = lax.fori_loop(0, nsl, slab, (carry[0], carry[1]))
        carry[0] = cr
        carry[1] = ci
        y_ref[...] = (jnp.dot(sr_ref[...].astype(BF16), cr_ref[0], preferred_element_type=F32)
                      - jnp.dot(si_ref[...].astype(BF16), ci_ref[0], preferred_element_type=F32))

    us = pl.BlockSpec((tbk, LANE), lambda j, t: (t, j))
    bs = pl.BlockSpec((1, LANE, SW), lambda j, t: (j, 0, 0))
    cs = pl.BlockSpec((1, SW, LANE), lambda j, t: (j, 0, 0))
    ts = pl.BlockSpec((8, SLAB, SW), lambda j, t: (0, 0, j))
    ss = pl.BlockSpec((tbk, SW), lambda j, t: (t, j))
    return pl.pallas_call(
        body, name=name, grid=(NJ, nt), in_specs=[us, bs, bs, cs, cs, ts], out_specs=[us, ss, ss],
        out_shape=[jax.ShapeDtypeStruct((s, D_MODEL), F32), jax.ShapeDtypeStruct((s, NST), F32),
                   jax.ShapeDtypeStruct((s, NST), F32)],
        scratch_shapes=[pltpu.VMEM((tbk, SW), F32), pltpu.VMEM((tbk, SW), F32), pltpu.VMEM((2, SLAB, SW), F32)],
        compiler_params=_cparams("parallel", "arbitrary"))(u, bbd_r, bbd_i, cbd_r, cbd_i, tab)


def s5_bwd(u, dy, dskip, st_r, st_i, bbd_r, bbd_i, cbd_r, cbd_i, tabrev, *, name):
    s = u.shape[0]
    tbk = _long_rows(s)
    nt = s // tbk
    nsl = tbk // SLAB
    rbk = tbk // SLAB

    def body(u_ref, dy_ref, d_ref, sr_ref, si_ref, pr_ref, pi_ref, br_ref, bi_ref, cr_ref, ci_ref, tab_ref,
             du_ref, dbr_ref, dbi_ref, dcr_ref, dci_ref, da_ref, lam_r, lam_i, carry):
        t = pl.program_id(1)

        @pl.when(t == 0)
        def _():
            carry[...] = jnp.zeros_like(carry)
            dbr_ref[...] = jnp.zeros_like(dbr_ref)
            dbi_ref[...] = jnp.zeros_like(dbi_ref)
            dcr_ref[...] = jnp.zeros_like(dcr_ref)
            dci_ref[...] = jnp.zeros_like(dci_ref)
            da_ref[...] = jnp.zeros_like(da_ref)

        dyv = dy_ref[...]
        dyh = dyv.astype(BF16)
        lam_r[...] = lax.dot_general(dyh, cr_ref[0], NT, preferred_element_type=F32)
        lam_i[...] = -lax.dot_general(dyh, ci_ref[0], NT, preferred_element_type=F32)
        tabs = [tab_ref[n] for n in range(8)]

        def slab(n, c):
            r0 = pl.multiple_of((nsl - 1 - n) * SLAB, SLAB)
            lr, li = _slab_scan(lam_r[pl.ds(r0, SLAB), :], lam_i[pl.ds(r0, SLAB), :], tabs, c[0], c[1], True)
            lam_r[pl.ds(r0, SLAB), :] = lr
            lam_i[pl.ds(r0, SLAB), :] = li
            return (jnp.broadcast_to(lr[0:1], lr.shape), jnp.broadcast_to(li[0:1], li.shape))

        cr, ci = lax.fori_loop(0, nsl, slab, (carry[0], carry[1]))
        carry[0] = cr
        carry[1] = ci
        lr, li = lam_r[...], lam_i[...]
        lrh, lih = lr.astype(BF16), li.astype(BF16)
        du = (dyv * d_ref[...] + lax.dot_general(lrh, br_ref[0], NT, preferred_element_type=F32)
              + lax.dot_general(lih, bi_ref[0], NT, preferred_element_type=F32))
        du_ref[...] = du.astype(BF16)
        ub = u_ref[...].astype(BF16)
        dbr_ref[0] += lax.dot_general(ub, lrh, TN, preferred_element_type=F32)
        dbi_ref[0] += lax.dot_general(ub, lih, TN, preferred_element_type=F32)
        srv, siv = sr_ref[...], si_ref[...]
        dcr_ref[0] += lax.dot_general(srv.astype(BF16), dyh, TN, preferred_element_type=F32)
        dci_ref[0] -= lax.dot_general(siv.astype(BF16), dyh, TN, preferred_element_type=F32)
        first = lax.broadcasted_iota(jnp.int32, srv.shape, 0) == 0
        pm = (t < nt - 1).astype(F32)
        spr = jnp.where(first, pr_ref[SLAB - 1:SLAB, :] * pm, pltpu.roll(srv, 1, 0))
        spi = jnp.where(first, pi_ref[SLAB - 1:SLAB, :] * pm, pltpu.roll(siv, 1, 0))
        da_ref[0:1, :] += jnp.sum(lr * spr + li * spi, axis=0, keepdims=True)
        da_ref[1:2, :] += jnp.sum(li * spr - lr * spi, axis=0, keepdims=True)

    rv = lambda t: nt - 1 - t
    us = pl.BlockSpec((tbk, LANE), lambda j, t: (rv(t), j))
    ds = pl.BlockSpec((1, LANE), lambda j, t: (0, j))
    ss = pl.BlockSpec((tbk, SW), lambda j, t: (rv(t), j))
    ps = pl.BlockSpec((SLAB, SW), lambda j, t: (jnp.maximum(rv(t) * rbk - 1, 0), j))
    bs = pl.BlockSpec((1, LANE, SW), lambda j, t: (j, 0, 0))
    cs = pl.BlockSpec((1, SW, LANE), lambda j, t: (j, 0, 0))
    ts = pl.BlockSpec((8, SLAB, SW), lambda j, t: (0, 0, j))
    das = pl.BlockSpec((2, SW), lambda j, t: (0, j))
    return pl.pallas_call(
        body, name=name, grid=(NJ, nt),
        in_specs=[us, us, ds, ss, ss, ps, ps, bs, bs, cs, cs, ts],
        out_specs=[us, bs, bs, cs, cs, das],
        out_shape=[jax.ShapeDtypeStruct((s, D_MODEL), BF16),
                   jax.ShapeDtypeStruct((NJ, LANE, SW), F32), jax.ShapeDtypeStruct((NJ, LANE, SW), F32),
                   jax.ShapeDtypeStruct((NJ, SW, LANE), F32), jax.ShapeDtypeStruct((NJ, SW, LANE), F32),
                   jax.ShapeDtypeStruct((2, NST), F32)],
        scratch_shapes=[pltpu.VMEM((tbk, SW), F32), pltpu.VMEM((tbk, SW), F32), pltpu.VMEM((2, SLAB, SW), F32)],
        compiler_params=_cparams("parallel", "arbitrary"))(
            u, dy, dskip, st_r, st_i, st_r, st_i, bbd_r, bbd_i, cbd_r, cbd_i, tabrev)


GELU_C = math.sqrt(2.0 / math.pi)
GELU_A = 0.044715


def s5post_fwd(y, u, dskip, *, name):
    s = y.shape[0]
    tm = _rows(s)

    def body(y_ref, u_ref, d_ref, o_ref):
        z = y_ref[...] + d_ref[...] * u_ref[...]
        o_ref[...] = (0.5 * z * (1.0 + jnp.tanh(GELU_C * (z + GELU_A * z * z * z)))).astype(BF16)

    row = pl.BlockSpec((tm, D_MODEL), lambda i: (i, 0))
    vec = pl.BlockSpec((1, D_MODEL), lambda i: (0, 0))
    return pl.pallas_call(body, name=name, grid=(s // tm,), in_specs=[row, row, vec], out_specs=row,
                          out_shape=jax.ShapeDtypeStruct((s, D_MODEL), BF16),
                          compiler_params=_cparams("parallel"))(y, u, dskip)


def s5post_bwd(y, u, dskip, dg, *, name):
    s = y.shape[0]
    tm = _rows(s)

    def body(y_ref, u_ref, d_ref, dg_ref, dz_ref, dd_ref):
        @pl.when(pl.program_id(0) == 0)
        def _():
            dd_ref[...] = jnp.zeros_like(dd_ref)

        uv = u_ref[...]
        z = y_ref[...] + d_ref[...] * uv
        th = jnp.tanh(GELU_C * (z + GELU_A * z * z * z))
        dgelu = 0.5 * (1.0 + th) + 0.5 * z * (1.0 - th * th) * (GELU_C * (1.0 + 3.0 * GELU_A * z * z))
        dz = dg_ref[...] * dgelu
        dz_ref[...] = dz
        dd_ref[...] += jnp.sum(dz * uv, axis=0, keepdims=True)

    row = pl.BlockSpec((tm, D_MODEL), lambda i: (i, 0))
    vec = pl.BlockSpec((1, D_MODEL), lambda i: (0, 0))
    return pl.pallas_call(body, name=name, grid=(s // tm,), in_specs=[row, row, vec, row], out_specs=[row, vec],
                          out_shape=[jax.ShapeDtypeStruct((s, D_MODEL), F32), jax.ShapeDtypeStruct((1, D_MODEL), F32)],
                          compiler_params=_cparams("arbitrary"))(y, u, dskip, dg)


def glu_fwd(glu, x, *, name):
    s = x.shape[0]
    tm = _rows(s)

    def body(a_ref, b_ref, x_ref, o_ref):
        o_ref[...] = x_ref[...] + a_ref[...] * jax.nn.sigmoid(b_ref[...])

    row = pl.BlockSpec((tm, D_MODEL), lambda i: (i, 0))
    return pl.pallas_call(body, name=name, grid=(s // tm,),
                          in_specs=[row, pl.BlockSpec((tm, D_MODEL), lambda i: (i, 1)), row], out_specs=row,
                          out_shape=jax.ShapeDtypeStruct((s, D_MODEL), F32),
                          compiler_params=_cparams("parallel"))(glu, glu, x)


def glu_bwd(glu, dx, *, name):
    s = dx.shape[0]
    tm = _rows(s)

    def body(a_ref, b_ref, dx_ref, o_ref):
        sg = jax.nn.sigmoid(b_ref[...])
        dxv = dx_ref[...]
        o_ref[:, :D_MODEL] = (dxv * sg).astype(BF16)
        o_ref[:, D_MODEL:] = (dxv * a_ref[...] * sg * (1.0 - sg)).astype(BF16)

    row = pl.BlockSpec((tm, D_MODEL), lambda i: (i, 0))
    return pl.pallas_call(body, name=name, grid=(s // tm,),
                          in_specs=[row, pl.BlockSpec((tm, D_MODEL), lambda i: (i, 1)), row],
                          out_specs=pl.BlockSpec((tm, 2 * D_MODEL), lambda i: (i, 0)),
                          out_shape=jax.ShapeDtypeStruct((s, 2 * D_MODEL), BF16),
                          compiler_params=_cparams("parallel"))(glu, glu, dx)


def loss_head(y, target, *, name):
    s = y.shape[0]
    tm = _rows(s)

    def body(y_ref, t_ref, dy_ref, dyh_ref, l_ref):
        @pl.when(pl.program_id(0) == 0)
        def _():
            l_ref[...] = jnp.zeros_like(l_ref)

        e = y_ref[...] - t_ref[...]
        dy_ref[...] = e * (1.0 / D_MODEL)
        dyh_ref[...] = (e * (1.0 / D_MODEL)).astype(BF16)
        e2 = jnp.sum((e * e).reshape(tm // 8, 8, D_MODEL), axis=0)
        acc = e2[:, 0:LANE]
        for kk in range(1, D_MODEL // LANE):
            acc = acc + e2[:, kk * LANE:(kk + 1) * LANE]
        l_ref[...] += acc

    row = pl.BlockSpec((tm, D_MODEL), lambda i: (i, 0))
    return pl.pallas_call(body, name=name, grid=(s // tm,), in_specs=[row, row],
                          out_specs=[row, row, pl.BlockSpec((8, LANE), lambda i: (0, 0))],
                          out_shape=[jax.ShapeDtypeStruct((s, D_MODEL), F32), jax.ShapeDtypeStruct((s, D_MODEL), BF16),
                                     jax.ShapeDtypeStruct((8, LANE), F32)],
                          compiler_params=_cparams("arbitrary"))(y, target)


PACKW = 1024
NCHIP = 4


def _mesh_pos():
    return lax.axis_index("x"), lax.axis_index("y"), lax.axis_index("c")


def _chip_exchange(bufs, scatter, name):
    n = len(bufs)
    shapes = [b.shape[1:] if scatter else b.shape for b in bufs]

    def body(*refs):
        ins, outs = refs[:n], refs[n:2 * n]
        send_sems, recv_sems, local_sems = refs[2 * n:]
        x, y, c = _mesh_pos()
        me = 2 * x + y
        peers = [(1 - x, y), (x, 1 - y), (1 - x, 1 - y)]

        def copy(a, j, px, py, dst_slot):
            src = ins[a].at[2 * px + py] if scatter else ins[a]
            return pltpu.make_async_remote_copy(src_ref=src, dst_ref=outs[a].at[dst_slot],
                                                send_sem=send_sems.at[3 * a + j], recv_sem=recv_sems.at[3 * a + j],
                                                device_id=(px, py, c), device_id_type=MESH)

        mine = [pltpu.make_async_copy(ins[a].at[me] if scatter else ins[a], outs[a].at[me], local_sems.at[a])
                for a in range(n)]
        sends = [copy(a, j, px, py, me) for a in range(n) for j, (px, py) in enumerate(peers)]
        for cp in mine + sends:
            cp.start()
        for a in range(n):
            for j, (px, py) in enumerate(peers):
                copy(a, j, px, py, 2 * px + py).wait_recv()
        for cp in sends:
            cp.wait_send()
        for cp in mine:
            cp.wait()

    return pl.pallas_call(
        body, name=name, in_specs=[ANY] * n, out_specs=[ANY] * n,
        out_shape=[jax.ShapeDtypeStruct((NCHIP,) + tuple(shp), b.dtype) for shp, b in zip(shapes, bufs)],
        scratch_shapes=[pltpu.SemaphoreType.DMA((3 * n,)), pltpu.SemaphoreType.DMA((3 * n,)),
                        pltpu.SemaphoreType.DMA((n,))],
    )(*bufs)


HBM_SPEC = pl.BlockSpec(memory_space=pltpu.HBM)
SEM_SPEC = pl.BlockSpec(memory_space=pltpu.SEMAPHORE)
DATAFLOW = pltpu.SideEffectType.DATAFLOW_SIDE_EFFECTING


def _exchange_copy(ins, lands, send_sems, recv_sems, scatter, a, j, px, py, c, dst_slot):
    src = ins[a].at[2 * px + py] if scatter else ins[a]
    return pltpu.make_async_remote_copy(src_ref=src, dst_ref=lands[a].at[dst_slot],
                                        send_sem=send_sems.at[3 * a + j], recv_sem=recv_sems.at[3 * a + j],
                                        device_id=(px, py, c), device_id_type=MESH)


def _own_copy(ins, lands, local_sems, scatter, a, me):
    return pltpu.make_async_copy(ins[a].at[me] if scatter else ins[a], lands[a].at[me], local_sems.at[a])


def exchange_start(bufs, scatter, name):
    n = len(bufs)
    lands = [lax.empty((NCHIP,) + tuple(b.shape[1:] if scatter else b.shape), b.dtype) for b in bufs]

    def body(*refs):
        ins, lnd, send_sems, recv_sems, local_sems, token = (refs[:n], refs[n:2 * n], refs[2 * n], refs[2 * n + 1],
                                                             refs[2 * n + 2], refs[-1])
        x, y, c = _mesh_pos()
        me = 2 * x + y
        for a in range(n):
            _own_copy(ins, lnd, local_sems, scatter, a, me).start()
            for j, (px, py) in enumerate([(1 - x, y), (x, 1 - y), (1 - x, 1 - y)]):
                _exchange_copy(ins, lnd, send_sems, recv_sems, scatter, a, j, px, py, c, me).start()
        token[...] = jnp.zeros_like(token)

    thru = [pltpu.HBM(b.shape, b.dtype) for b in list(bufs) + lands]
    out = pl.pallas_call(
        body, name=name, in_specs=[HBM_SPEC] * (2 * n),
        out_specs=[SEM_SPEC] * 3 + [HBM_SPEC] * (2 * n) + [pl.BlockSpec(memory_space=pltpu.VMEM)],
        out_shape=[pltpu.SemaphoreType.DMA((3 * n,)), pltpu.SemaphoreType.DMA((3 * n,)), pltpu.SemaphoreType.DMA((n,))]
        + thru + [jax.ShapeDtypeStruct((SLAB, LANE), F32)],
        input_output_aliases={k: 3 + k for k in range(2 * n)},
        compiler_params=pltpu.CompilerParams(has_side_effects=DATAFLOW),
    )(*[pltpu.with_memory_space_constraint(b, pltpu.HBM) for b in list(bufs) + lands])
    return tuple(out[:3]), out[3:3 + n], out[3 + n:3 + 2 * n], out[-1][0, 0]


def exchange_wait(started, after, scatter, name):
    sems, bufs, lands, _ = started
    n = len(bufs)

    def body(*refs):
        ins, lnd, ssem, rsem, lsem = refs[:n], refs[n:2 * n], refs[2 * n], refs[2 * n + 1], refs[2 * n + 2]
        x, y, c = _mesh_pos()
        for a in range(n):
            _own_copy(ins, lnd, lsem, scatter, a, 2 * x + y).wait()
            for j, (px, py) in enumerate([(1 - x, y), (x, 1 - y), (1 - x, 1 - y)]):
                cp = _exchange_copy(ins, lnd, ssem, rsem, scatter, a, j, px, py, c, 2 * px + py)
                cp.wait_send()
                cp.wait_recv()

    thru = [pltpu.HBM(b.shape, b.dtype) for b in list(bufs) + list(lands)]
    out = pl.pallas_call(
        body, name=name, in_specs=[HBM_SPEC] * (2 * n) + [SEM_SPEC] * 3 + [ANY],
        out_specs=[HBM_SPEC] * (2 * n), out_shape=thru,
        input_output_aliases={k: k for k in range(2 * n)},
        compiler_params=pltpu.CompilerParams(has_side_effects=DATAFLOW),
    )(*bufs, *lands, *sems, after)
    return out[n:]


def sibling_swap(bufs, name):
    n = len(bufs)

    def body(*refs):
        ins, outs, send_sems, recv_sems = refs[:n], refs[n:2 * n], refs[2 * n], refs[2 * n + 1]
        x, y, c = _mesh_pos()
        cps = [pltpu.make_async_remote_copy(src_ref=ins[k], dst_ref=outs[k], send_sem=send_sems.at[k],
                                            recv_sem=recv_sems.at[k], device_id=(x, y, 1 - c), device_id_type=MESH)
               for k in range(n)]
        for cp in cps:
            cp.start()
        for cp in cps:
            cp.wait()

    return pl.pallas_call(
        body, name=name, in_specs=[ANY] * n, out_specs=[ANY] * n,
        out_shape=[jax.ShapeDtypeStruct(b.shape, b.dtype) for b in bufs],
        scratch_shapes=[pltpu.SemaphoreType.DMA((n,)), pltpu.SemaphoreType.DMA((n,))],
    )(*bufs)


def _swap_copy(ins, lands, send_sems, recv_sems, a):
    x, y, c = _mesh_pos()
    return pltpu.make_async_remote_copy(src_ref=ins[a], dst_ref=lands[a], send_sem=send_sems.at[a],
                                        recv_sem=recv_sems.at[a], device_id=(x, y, 1 - c), device_id_type=MESH)


def swap_start(bufs, name):
    n = len(bufs)
    lands = [lax.empty(b.shape, b.dtype) for b in bufs]

    def body(*refs):
        for a in range(n):
            _swap_copy(refs[:n], refs[n:2 * n], refs[2 * n], refs[2 * n + 1], a).start()
        refs[-1][...] = jnp.zeros_like(refs[-1])

    thru = [pltpu.HBM(b.shape, b.dtype) for b in list(bufs) + lands]
    out = pl.pallas_call(
        body, name=name, in_specs=[HBM_SPEC] * (2 * n),
        out_specs=[SEM_SPEC] * 2 + [HBM_SPEC] * (2 * n) + [pl.BlockSpec(memory_space=pltpu.VMEM)],
        out_shape=[pltpu.SemaphoreType.DMA((n,)), pltpu.SemaphoreType.DMA((n,))] + thru
        + [jax.ShapeDtypeStruct((SLAB, LANE), F32)],
        input_output_aliases={k: 2 + k for k in range(2 * n)},
        compiler_params=pltpu.CompilerParams(has_side_effects=DATAFLOW),
    )(*[pltpu.with_memory_space_constraint(b, pltpu.HBM) for b in list(bufs) + lands])
    return tuple(out[:2]), out[2:2 + n], out[2 + n:2 + 2 * n], out[-1][0, 0]


def swap_wait(started, after, name):
    sems, bufs, lands, _ = started
    n = len(bufs)

    def body(*refs):
        for a in range(n):
            _swap_copy(refs[:n], refs[n:2 * n], refs[2 * n], refs[2 * n + 1], a).wait()

    thru = [pltpu.HBM(b.shape, b.dtype) for b in list(bufs) + list(lands)]
    out = pl.pallas_call(
        body, name=name, in_specs=[HBM_SPEC] * (2 * n) + [SEM_SPEC] * 2 + [ANY],
        out_specs=[HBM_SPEC] * (2 * n), out_shape=thru,
        input_output_aliases={k: k for k in range(2 * n)},
        compiler_params=pltpu.CompilerParams(has_side_effects=DATAFLOW),
    )(*bufs, *lands, *sems, after)
    return out[n:]


EW_VMEM_BUDGET = 20 * 1024 * 1024


def _ew_rows(rows, w, bytes_per_elem):
    wpad = -(-w // LANE) * LANE
    for t in (1024, 512, 256, 128, 64, 32, 16, 8):
        if rows % t == 0 and 2 * t * wpad * bytes_per_elem <= EW_VMEM_BUDGET:
            return t
    return rows


def sum_slots(buf, *, name):
    _, rows, w = buf.shape
    tm = _ew_rows(rows, w, NCHIP * buf.dtype.itemsize + 4)

    def body(b_ref, o_ref):
        acc = b_ref[0].astype(F32)
        for kk in range(1, NCHIP):
            acc = acc + b_ref[kk].astype(F32)
        o_ref[...] = acc

    return pl.pallas_call(body, name=name, grid=(rows // tm,),
                          in_specs=[pl.BlockSpec((NCHIP, tm, w), lambda i: (0, i, 0))],
                          out_specs=pl.BlockSpec((tm, w), lambda i: (i, 0)),
                          out_shape=jax.ShapeDtypeStruct((rows, w), F32),
                          compiler_params=_cparams("parallel"))(buf)


def adamw(p_mine, p_other, w, m, v, *, name):
    rows, wd = w.shape
    tm = _ew_rows(rows, wd, 9 * 4)
    c1 = 1.0 - ADAM_B1 ** ADAM_STEP
    c2 = 1.0 - ADAM_B2 ** ADAM_STEP

    def body(a_ref, b_ref, w_ref, m_ref, v_ref, g_ref, d_ref, nm_ref, nv_ref):
        g = a_ref[...] + b_ref[...]
        nm = ADAM_B1 * m_ref[...] + (1.0 - ADAM_B1) * g
        nv = ADAM_B2 * v_ref[...] + (1.0 - ADAM_B2) * (g * g)
        g_ref[...] = g
        nm_ref[...] = nm
        nv_ref[...] = nv
        d_ref[...] = -ADAM_LR * ((nm / c1) / (jnp.sqrt(nv / c2) + ADAM_EPS) + ADAM_WD * w_ref[...])

    row = pl.BlockSpec((tm, wd), lambda i: (i, 0))
    return pl.pallas_call(body, name=name, grid=(rows // tm,), in_specs=[row] * 5, out_specs=[row] * 4,
                          out_shape=[jax.ShapeDtypeStruct((rows, wd), F32)] * 4,
                          compiler_params=_cparams("parallel"))(p_mine, p_other, w, m, v)


def _rows2d(a, lead=0):
    tail = a.shape[lead:]
    n = int(np.prod(tail))
    if tail[-1] < LANE // 2 and n % (8 * LANE) == 0:
        return a.reshape(a.shape[:lead] + (n // (8 * LANE), 8 * LANE))
    return a.reshape(a.shape[:lead] + (-1, tail[-1]))


BIG = [("mix_w_in", 2), ("w_uq", 2), ("w_ukv", 2), ("mix_w_out", 1), ("ssm_w_in", 1), ("w_glu", 2),
       ("ffn_w_up", 2), ("ffn_w_down", 1)]
SMALL = [("sconv_w", 2), ("ssm_norm", 1), ("d_skip", 1), ("ffn_conv_w", 2)]
REPL = ["attn_norm", "cq_norm", "ckv_norm", "q_gain", "k_gain", "lambda_re", "lambda_im", "log_step",
        "b_re", "b_im", "c_re", "c_im", "ffn_norm"]
ORDER = ["attn_norm", "mix_w_in", "cq_norm", "ckv_norm", "w_uq", "w_ukv", "q_gain", "k_gain", "sconv_w", "mix_w_out",
         "ssm_norm", "ssm_w_in", "lambda_re", "lambda_im", "log_step", "b_re", "b_im", "c_re", "c_im", "d_skip",
         "w_glu", "ffn_norm", "ffn_w_up", "ffn_conv_w", "ffn_w_down"]


def _join(g, axis):
    return jnp.concatenate([g[k] for k in range(NCHIP)], axis=axis)


def _split(full, axis, parts=NCHIP):
    return jnp.stack(jnp.split(full, parts, axis=axis))


def _discretize(lr, li, ls, b_re, b_im):
    dt = jnp.exp(ls)[:, None]
    mag = jnp.exp(lr * dt)
    ar, ai = mag * jnp.cos(li * dt), mag * jnp.sin(li * dt)
    nr, ni = ar - 1.0, ai
    den = lr * lr + li * li
    zr, zi = (nr * lr + ni * li) / den, (ni * lr - nr * li) / den
    bbar_r = zr[..., None] * b_re - zi[..., None] * b_im
    bbar_i = zr[..., None] * b_im + zi[..., None] * b_re
    return ar, ai, bbar_r, bbar_i


def _b_blockdiag(bbar):
    gl = G // NJ
    bb = bbar.reshape(NJ, gl, P, GC).transpose(0, 1, 3, 2)
    return jnp.einsum("jgcp,gh->jgchp", bb, jnp.eye(gl, dtype=bbar.dtype)).reshape(NJ, gl * GC, gl * P)


def _b_blockdiag_t(dbd):
    gl = G // NJ
    d = jnp.einsum("jgchp,gh->jgcp", dbd.reshape(NJ, gl, GC, gl, P), jnp.eye(gl, dtype=dbd.dtype))
    return d.transpose(0, 1, 3, 2).reshape(G, P, GC)


def _c_blockdiag(cmat):
    gl = G // NJ
    cc = cmat.reshape(NJ, gl, GC, P).transpose(0, 1, 3, 2)
    return jnp.einsum("jgpc,gh->jgphc", cc, jnp.eye(gl, dtype=cmat.dtype)).reshape(NJ, gl * P, gl * GC)


def _c_blockdiag_t(dbd):
    gl = G // NJ
    d = jnp.einsum("jgphc,gh->jgpc", dbd.reshape(NJ, gl, P, gl, GC), jnp.eye(gl, dtype=dbd.dtype))
    return d.transpose(0, 1, 3, 2).reshape(G, GC, P)


def _pad_heads_cols(w, width):
    r = w.shape[0]
    return jnp.pad(w.reshape(r, HEADS, width), ((0, 0), (0, 0), (0, HP - width))).reshape(r, HEADS * HP)


def _unpad_heads_cols(w, width):
    r = w.shape[0]
    return w.reshape(r, HEADS, HP)[:, :, :width].reshape(r, HEADS * width)


W_IN_SPLIT = (QR + KVR, QR + KVR + ROPE)


def _w_in_layout(w):
    a, b = W_IN_SPLIT
    kr = jnp.pad(w[:, a:b], ((0, 0), (NOPE, HP - QK)))
    return jnp.concatenate([w[:, :a], w[:, b:], kr], axis=1)


def _ffn_fwd(x, l, wt, name):
    h = rms_fwd(x, wt["ffn_norm"][l][None], name=f"{name}_norm")
    up = mm(h, wt["ffn_w_up"][l], name=f"{name}_up")
    act = ffnact_fwd(up, wt["ffn_conv_w"][l], name=f"{name}_act")
    out = mm(act, wt["ffn_w_down"][l], add=x, name=f"{name}_down")
    return out, (x, h, up, act)


def _ffn_bwd(dout, douth, saved, l, wt, name):
    x, h, up, act = saved
    g = {}
    dact = mm(douth, wt["ffn_w_down"][l], tb=True, name=f"{name}_ddown")
    g["ffn_w_down"] = mm(act, douth, ta=True, out_dtype=BF16, name=f"{name}_dwdown")
    dup, g["ffn_conv_w"] = ffnact_bwd(up, wt["ffn_conv_w"][l], dact, name=f"{name}_dact")
    g["ffn_w_up"] = tuple(mm(h, d, ta=True, out_dtype=BF16, name=f"{name}_dwup{kk}") for kk, d in enumerate(dup))
    dh = mm_nt_segments(dup, wt["ffn_w_up"][l], name=f"{name}_dup")
    dx, dxh, dg = rms_bwd(x, wt["ffn_norm"][l][None], dh, add=dout, twin=True, name=f"{name}_dnorm")
    g["ffn_norm"] = dg[0]
    return dx, dxh, g


def _even_fwd(x, i, wt, tabs, name):
    h = rms_fwd(x, wt["attn_norm"][i][None], name=f"{name}_norm")
    proj = mm(h, wt["w_in2"][i], name=f"{name}_in")
    cqn = rms_fwd(proj, wt["cq_norm"][i][None], col=0, name=f"{name}_cqnorm")
    ckvn = rms_fwd(proj, wt["ckv_norm"][i][None], col=1, name=f"{name}_ckvnorm")
    qraw = mm(cqn, wt["w_uq_p"][i], name=f"{name}_uq")
    kv = mm(ckvn, wt["w_ukv_p"][i], name=f"{name}_ukv")
    q, k, v = qkprep_fwd(qraw, kv, proj, wt["q_gain_p"][i], wt["k_gain_p"][i], tabs, name=f"{name}_qkprep")
    o, oh, lset = attn_fwd(q, k, v, name=f"{name}_attn")
    conv = sconv_fwd(proj, wt["sconv_w"][i], name=f"{name}_sconv")
    t = mm(oh, wt["w_out_a"][i], add=x, name=f"{name}_outa")
    out = mm(conv, wt["w_out_c"][i], add=t, name=f"{name}_outc")
    return out, (x, h, proj, cqn, ckvn, qraw, kv, q, k, v, o, oh, lset, conv)


def _even_bwd(dout, douth, saved, i, wt, tabs, name):
    x, h, proj, cqn, ckvn, qraw, kv, q, k, v, o, oh, lset, conv = saved
    g = {}
    do = mm(douth, wt["w_out_a"][i], tb=True, name=f"{name}_douta")
    dconv = mm(douth, wt["w_out_c"][i], tb=True, name=f"{name}_doutc")
    g["w_out_a"] = mm(oh, douth, ta=True, out_dtype=BF16, name=f"{name}_dwouta")
    g["w_out_c"] = mm(conv, douth, ta=True, out_dtype=BF16, name=f"{name}_dwoutc")
    dgates, g["sconv_w"] = sconv_bwd(proj, wt["sconv_w"][i], dconv, name=f"{name}_dsconv")
    doh, deltat = attn_delta(o, do, name=f"{name}_dattn_delta")
    dq, dk, dv = attn_bwd(q, k, v, doh, lset, deltat, name=f"{name}_dattn")
    dqraw, dkraw, dkrope, dqg, dkg = qkprep_bwd(qraw, kv, proj, wt["q_gain_p"][i], wt["k_gain_p"][i], tabs, dq, dk,
                                                name=f"{name}_dqkprep")
    g["q_gain"], g["k_gain"] = dqg[0, :QK], dkg[0, :QK]
    dcqn = mm(dqraw, wt["w_uq_p"][i], tb=True, name=f"{name}_duq")
    g["w_uq_p"] = mm(cqn, dqraw, ta=True, out_dtype=BF16, name=f"{name}_dwuq")
    dkv = (dkraw, dv)
    dckvn = mm_nt_segments(dkv, wt["w_ukv_p"][i], name=f"{name}_dukv")
    g["w_ukv_p"] = tuple(mm(ckvn, d, ta=True, out_dtype=BF16, name=f"{name}_dwukv{kk}") for kk, d in enumerate(dkv))
    dcq, dgq = rms_bwd(proj, wt["cq_norm"][i][None], dcqn, col=0, out_dtype=BF16, name=f"{name}_dcqnorm")
    dckv, dgkv = rms_bwd(proj, wt["ckv_norm"][i][None], dckvn, col=1, out_dtype=BF16, name=f"{name}_dckvnorm")
    g["cq_norm"], g["ckv_norm"] = dgq[0], dgkv[0]
    dproj = (dcq, dckv, *dgates, dkrope)
    g["w_in2"] = tuple(mm(h, d, ta=True, out_dtype=BF16, name=f"{name}_dwin{kk}") for kk, d in enumerate(dproj))
    dh = mm_nt_segments(dproj, wt["w_in2"][i], name=f"{name}_din")
    dx, dxh, dg = rms_bwd(x, wt["attn_norm"][i][None], dh, add=dout, twin=True, name=f"{name}_dnorm")
    g["attn_norm"] = dg[0]
    return dx, dxh, g


def _odd_fwd(x, i, wt, name):
    h = rms_fwd(x, wt["ssm_norm"][i][None], name=f"{name}_norm")
    u = mm(h, wt["ssm_w_in"][i], name=f"{name}_in")
    y, st_r, st_i = s5_fwd(u, wt["bbd_r"][i], wt["bbd_i"][i], wt["cbd_r"][i], wt["cbd_i"][i], wt["tab_f"][i],
                           name=f"{name}_scan")
    gl = s5post_fwd(y, u, wt["d_skip"][i][None], name=f"{name}_gelu")
    glu = mm(gl, wt["w_glu"][i], name=f"{name}_glu")
    out = glu_fwd(glu, x, name=f"{name}_gate")
    return out, (x, h, u, y, st_r, st_i, gl, glu)


def _odd_bwd(dout, douth, saved, i, wt, name):
    x, h, u, y, st_r, st_i, gl, glu = saved
    g = {}
    dglu = glu_bwd(glu, dout, name=f"{name}_dgate")
    g["w_glu"] = mm(gl, dglu, ta=True, out_dtype=BF16, name=f"{name}_dwglu")
    dgl = mm(dglu, wt["w_glu"][i], tb=True, name=f"{name}_dglu")
    dz, dd = s5post_bwd(y, u, wt["d_skip"][i][None], dgl, name=f"{name}_dgelu")
    g["d_skip"] = dd[0]
    du, g["bbd_r"], g["bbd_i"], g["cbd_r"], g["cbd_i"], g["a"] = s5_bwd(
        u, dz, wt["d_skip"][i][None], st_r, st_i, wt["bbd_r"][i], wt["bbd_i"][i], wt["cbd_r"][i], wt["cbd_i"][i],
        wt["tab_r"][i], name=f"{name}_dscan")
    g["ssm_w_in"] = mm(h, du, ta=True, out_dtype=BF16, name=f"{name}_dwin")
    dh = mm(du, wt["ssm_w_in"][i], tb=True, name=f"{name}_din")
    dx, dxh, dg = rms_bwd(x, wt["ssm_norm"][i][None], dh, add=dout, twin=True, name=f"{name}_dnorm")
    g["ssm_norm"] = dg[0]
    return dx, dxh, g


MATMUL_WEIGHTS = {"even": ("mix_w_in", "w_uq", "w_ukv", "mix_w_out"), "odd": ("ssm_w_in", "w_glu"),
                  "ffn": ("ffn_w_up", "ffn_w_down")}
ODD_SMALL = ("lambda_re", "lambda_im", "log_step", "b_re", "b_im", "c_re", "c_im", "ssm_norm", "d_skip")


def _even_layouts(fw, wt, i):
    wt["w_in2"][i] = _w_in_layout(fw["mix_w_in"])
    wt["w_uq_p"][i] = _pad_heads_cols(fw["w_uq"], QK)
    ukv = fw["w_ukv"].reshape(KVR, HEADS, NOPE + VD)
    wt["w_ukv_p"][i] = jnp.concatenate(
        [_pad_heads_cols(ukv[:, :, :NOPE].reshape(KVR, HEADS * NOPE), NOPE),
         _pad_heads_cols(ukv[:, :, NOPE:].reshape(KVR, HEADS * VD), VD)], axis=1)
    wt["w_out_a"][i] = _pad_heads_cols(fw["mix_w_out"][:HEADS * VD].T, VD).T
    wt["w_out_c"][i] = fw["mix_w_out"][HEADS * VD:]


def _even_layouts_t(g):
    dk_, dv_ = g["w_ukv_p"]
    dcq, dckv, dgb, dgc, dci, dkr = g["w_in2"]
    return {"mix_w_in": jnp.concatenate([dcq, dckv, dkr[:, NOPE:QK], dgb, dgc, dci], axis=1),
            "w_uq": _unpad_heads_cols(g["w_uq_p"], QK),
            "w_ukv": jnp.concatenate([dk_.reshape(KVR, HEADS, HP)[:, :, :NOPE], dv_.reshape(KVR, HEADS, HP)[:, :, :VD]],
                                     axis=2).reshape(KVR, HEADS * (NOPE + VD)),
            "mix_w_out": jnp.concatenate([_unpad_heads_cols(g["w_out_a"].T, VD).T, g["w_out_c"]], axis=0)}


def _local_step(x, target, full, getw, putg):
    s = x.shape[0]
    n_even = (DEPTH + 1) // 2
    n_odd = DEPTH // 2
    tabs = _rope_tables(s)
    wt = dict(full)
    for key in ("w_in2", "w_uq_p", "w_ukv_p", "w_out_a", "w_out_c") + sum(MATMUL_WEIGHTS.values(), ()):
        wt[key] = {}
    wt["q_gain_p"] = jnp.pad(full["q_gain"], ((0, 0), (0, HP - QK)))[:, None, :]
    wt["k_gain_p"] = jnp.pad(full["k_gain"], ((0, 0), (0, HP - QK)))[:, None, :]

    disc_vjp = []
    for key in ("bbd_r", "bbd_i", "cbd_r", "cbd_i", "tab_f", "tab_r"):
        wt[key] = []
    for i in range(n_odd):
        (ar, ai, bbr, bbi), vjp = jax.vjp(_discretize, full["lambda_re"][i], full["lambda_im"][i], full["log_step"][i],
                                          full["b_re"][i], full["b_im"][i])
        disc_vjp.append(vjp)
        tf, tr = _scan_tables(ar.reshape(-1), ai.reshape(-1))
        wt["tab_f"].append(tf)
        wt["tab_r"].append(tr)
        wt["bbd_r"].append(_b_blockdiag(bbr).astype(BF16))
        wt["bbd_i"].append(_b_blockdiag(bbi).astype(BF16))
        wt["cbd_r"].append(_c_blockdiag(full["c_re"][i]).astype(BF16))
        wt["cbd_i"].append(_c_blockdiag(full["c_im"][i]).astype(BF16))

    saved = []
    for layer in range(DEPTH):
        i = layer // 2
        if layer % 2 == 0:
            fw, tok = getw("even", i, x)
            wt["attn_norm"] = full["attn_norm"] + tok
            _even_layouts(fw, wt, i)
            x, sm = _even_fwd(x, i, wt, tabs, f"l{layer}_mla")
        else:
            fw, tok = getw("odd", i, x)
            wt["ssm_norm"] = full["ssm_norm"] + tok
            for n, a in fw.items():
                wt[n][i] = a
            x, sm = _odd_fwd(x, i, wt, f"l{layer}_s5")
        fw, tok = getw("ffn", layer, x)
        wt["ffn_norm"] = full["ffn_norm"] + tok
        for n, a in fw.items():
            wt[n][layer] = a
        x, sf = _ffn_fwd(x, layer, wt, f"l{layer}_ffn")
        saved.append((sm, sf))
    dx, dxh, lslab = loss_head(x, target, name="loss_head")

    own = [n for n in ORDER if n not in sum(MATMUL_WEIGHTS.values(), ())]
    grads = {n: [None] * (DEPTH if n.startswith("ffn") else n_even) for n in own}
    tok = 0.0
    for layer in reversed(range(DEPTH)):
        i = layer // 2
        sm, sf = saved[layer]
        wt["ffn_conv_w"] = full["ffn_conv_w"] + tok
        dx, dxh, g = _ffn_bwd(dx, dxh, sf, layer, wt, f"l{layer}_ffn")
        tok = putg("ffn", layer, {n: g[n] for n in MATMUL_WEIGHTS["ffn"]})
        for n in ("ffn_norm", "ffn_conv_w"):
            grads[n][layer] = g[n]
        if layer % 2 == 0:
            wt["sconv_w"] = full["sconv_w"] + tok
            dx, dxh, g = _even_bwd(dx, dxh, sm, i, wt, tabs, f"l{layer}_mla")
            tok = putg("even", i, _even_layouts_t(g))
            for n in ("attn_norm", "cq_norm", "ckv_norm", "q_gain", "k_gain", "sconv_w"):
                grads[n][i] = g[n]
        else:
            wt["d_skip"] = full["d_skip"] + tok
            dx, dxh, g = _odd_bwd(dx, dxh, sm, i, wt, f"l{layer}_s5")
            tok = putg("odd", i, {n: g[n] for n in MATMUL_WEIGHTS["odd"]})
            dlr, dli, dls, dbr, dbi = disc_vjp[i]((g["a"][0].reshape(G, P), g["a"][1].reshape(G, P),
                                                    _b_blockdiag_t(g["bbd_r"]), _b_blockdiag_t(g["bbd_i"])))
            grads["lambda_re"][i], grads["lambda_im"][i], grads["log_step"][i] = dlr, dli, dls
            grads["b_re"][i], grads["b_im"][i] = dbr, dbi
            grads["c_re"][i], grads["c_im"][i] = _c_blockdiag_t(g["cbd_r"]), _c_blockdiag_t(g["cbd_i"])
            for n in ("ssm_norm", "d_skip"):
                grads[n][i] = g[n]
            if i == 0:
                tok = tok + putg("odd_small", 0, {n: jnp.stack(grads[n]) for n in ODD_SMALL})
    grads = {n: jnp.stack(v) for n, v in grads.items()}
    return jnp.sum(lslab), dx, grads


def kernel(x, attn_norm, mix_w_in, cq_norm, ckv_norm, w_uq, w_ukv, q_gain, k_gain, sconv_w, mix_w_out, ssm_norm, ssm_w_in, lambda_re, lambda_im, log_step, b_re, b_im, c_re, c_im, d_skip, w_glu, ffn_norm, ffn_w_up, ffn_conv_w, ffn_w_down, loss_target, m_attn_norm, m_mix_w_in, m_cq_norm, m_ckv_norm, m_w_uq, m_w_ukv, m_q_gain, m_k_gain, m_sconv_w, m_mix_w_out, m_ssm_norm, m_ssm_w_in, m_lambda_re, m_lambda_im, m_log_step, m_b_re, m_b_im, m_c_re, m_c_im, m_d_skip, m_w_glu, m_ffn_norm, m_ffn_w_up, m_ffn_conv_w, m_ffn_w_down, v_attn_norm, v_mix_w_in, v_cq_norm, v_ckv_norm, v_w_uq, v_w_ukv, v_q_gain, v_k_gain, v_sconv_w, v_mix_w_out, v_ssm_norm, v_ssm_w_in, v_lambda_re, v_lambda_im, v_log_step, v_b_re, v_b_im, v_c_re, v_c_im, v_d_skip, v_w_glu, v_ffn_norm, v_ffn_w_up, v_ffn_conv_w, v_ffn_w_down):
    args = dict(locals())
    w = {n: args[n] for n in ORDER}
    m = {n: args["m_" + n] for n in ORDER}
    v = {n: args["v_" + n] for n in ORDER}
    me = 2 * lax.axis_index("x") + lax.axis_index("y")

    axis = dict(BIG)

    gs = _chip_exchange([w[n] for n, _ in SMALL], False, "gather_w_f32")
    full = {n: w[n] for n in REPL}
    for (n, ax), g in zip(SMALL, gs):
        full[n] = _join(g, ax)
    parts = [(("even", "odd")[layer % 2], layer // 2) for layer in range(DEPTH)]
    parts = [p for layer, mixer in enumerate(parts) for p in (mixer, ("ffn", layer))]
    gathers = {}

    def start_gather(kind, idx, zero):
        shards = [(w[n][idx] + zero).astype(BF16) for n in MATMUL_WEIGHTS[kind]]
        gathers[kind, idx] = exchange_start(shards, False, f"gather_start_{kind}{idx}")

    start_gather(*parts[0], 0.0 * gs[0][(0,) * gs[0].ndim])

    def getw(kind, idx, after):
        got = exchange_wait(gathers[kind, idx], after, False, f"gather_wait_{kind}{idx}")
        nxt = parts.index((kind, idx)) + 1
        tok = 0.0
        if nxt < len(parts):
            start_gather(*parts[nxt], 0.0 * got[0][(0,) * got[0].ndim].astype(F32))
            tok = gathers[parts[nxt]][3]
        return {n: _join(g, axis[n] - 1) for n, g in zip(MATMUL_WEIGHTS[kind], got)}, tok

    scatters, early = [], []

    def putg(kind, idx, g):
        if kind == "odd_small":
            arrs = [_rows2d(g[n]) for n in ODD_SMALL]
            early.append(exchange_start(arrs, False, "gather_start_g_odd"))
            return early[0][3]
        blocks = [jnp.concatenate([_split(part, axis[n] - 1, NCHIP // len(g[n])) for part in g[n]])
                  if isinstance(g[n], tuple) else _split(g[n], axis[n] - 1) for n in MATMUL_WEIGHTS[kind]]
        tok = 0.0
        if scatters:
            tok = reduce_part(*scatters.pop(), blocks[0])
        scatters.append((kind, idx, exchange_start(blocks, True, f"scatter_start_{kind}{idx}")))
        return tok + scatters[-1][2][3]

    swaps = []

    def reduce_part(kind, idx, started, after):
        got = exchange_wait(started, after, True, f"scatter_wait_{kind}{idx}")
        sums = [sum_slots(_rows2d(sl, 1), name=f"sum_{n}{idx}") for n, sl in zip(MATMUL_WEIGHTS[kind], got)]
        swaps.append((kind, idx, sums, swap_start(sums, f"swap_start_{kind}{idx}")))
        return swaps[-1][3][3]

    sq, dx, grads = _local_step(x[0], loss_target[0], full, getw, putg)
    loss = lax.psum(0.5 * sq / D_MODEL, ("x", "y", "c"))

    late_names = [n for n in REPL + [n for n, _ in SMALL] if n not in ODD_SMALL]
    rep_names = list(ODD_SMALL) + late_names
    names = [n for n, _ in BIG] + rep_names
    late = [_rows2d(grads[n]) for n in late_names]
    late_started = exchange_start(late, False, "gather_start_g_f32")
    reduce_part(*scatters.pop(), dx)
    slots = (exchange_wait(early[0], dx, False, "gather_wait_g_odd")
             + exchange_wait(late_started, swaps[-1][2][0], False, "gather_wait_g_f32"))
    small_mine = [sum_slots(_rows2d(sl, 1), name=f"sum_{n}") for n, sl in zip(rep_names, slots)]
    small_other = list(sibling_swap(small_mine, "swap_g_small"))
    summed, swapped = {}, {}
    for kind, idx, sums, started in swaps:
        for n, p, q in zip(MATMUL_WEIGHTS[kind], sums, swap_wait(started, small_other[0], f"swap_wait_{kind}{idx}")):
            summed[n, idx], swapped[n, idx] = p, q
    mine = [jnp.concatenate([summed[n, idx] for idx in range(w[n].shape[0])], axis=0) for n, _ in BIG] + small_mine
    other = [jnp.concatenate([swapped[n, idx] for idx in range(w[n].shape[0])], axis=0) for n, _ in BIG] + small_other

    def local(n, p):
        ax = dict(SMALL).get(n)
        if ax is None:
            return p
        part = lax.dynamic_index_in_dim(_split(p.reshape(grads[n].shape), ax), me, 0, keepdims=False)
        return _rows2d(part)

    outs = {}
    for n, p, q in zip(names, mine, other):
        res = adamw(local(n, p), local(n, q), _rows2d(w[n]), _rows2d(m[n]), _rows2d(v[n]), name=f"adamw_{n}")
        outs[n] = [r.reshape(w[n].shape) for r in res]
    return (loss, dx[None], *[outs[n][0] for n in ORDER], *[outs[n][1] for n in ORDER],
            *[outs[n][2] for n in ORDER], *[outs[n][3] for n in ORDER])
```

```python
import functools
import math

import numpy as np
import jax
import jax.numpy as jnp
from jax import lax
from jax.experimental import pallas as pl
from jax.experimental.pallas import tpu as pltpu

F32, BF16 = jnp.float32, jnp.bfloat16

D_MODEL = 1024
DEPTH = 4
HEADS = 8
NOPE, ROPE, QK, VD = 64, 32, 96, 64
HP = 128
QR, KVR = 256, 256
CONVC = 512
FFN_H = 2816
G, P, GC = 64, 64, 16
NST = G * P
SLAB = 8
LANE = 128
EPS = 1e-6
ROPE_THETA = 10000.0
ADAM_LR, ADAM_B1, ADAM_B2, ADAM_EPS, ADAM_WD, ADAM_STEP = 0.001, 0.9, 0.999, 1e-08, 0.01, 10
VMEM_LIMIT = 48 * 1024 * 1024
MESH = pl.DeviceIdType.MESH
ANY = pl.BlockSpec(memory_space=pl.ANY)


def _cparams(*sem):
    return pltpu.CompilerParams(dimension_semantics=sem, vmem_limit_bytes=VMEM_LIMIT)


def _pick(dim, prefs):
    for p in prefs:
        if dim % p == 0:
            return p
    return dim


def _rows(s):
    return _pick(s, (512, 256, 128, 64, 32, 16, 8))


def _long_rows(s):
    return _pick(s, (1024, 512, 256, 128, 64, 32, 16, 8))


MM_VMEM_BUDGET = 36 * 1024 * 1024
MM_MAX_TILE_ELEMS = 640 * 1024
HBM_BYTES_PER_US = 3.0e6
STEP_OVERHEAD_US = 0.35


def _lane_tiles(n):
    c = {t for t in range(LANE, min(n, 1536) + 1, LANE) if n % t == 0}
    if n <= 2304 or not c:
        c.add(n)
    return sorted(c, reverse=True)


def _mm_tiles(m, n, k, sa, sb, so):
    best = None
    for tm in [t for t in (1024, 512, 256) if m % t == 0] or [m]:
        for tn in _lane_tiles(n):
            if tm * tn > MM_MAX_TILE_ELEMS:
                continue
            if 2 * (tm * k * sa + k * tn * sb + tm * tn * so) + 4 * tm * tn > MM_VMEM_BUDGET:
                continue
            steps = (m // tm) * (n // tn)
            for inner_n in (True, False):
                moved = (m * k * sa + (m // tm) * k * n * sb) if inner_n else (k * n * sb + (n // tn) * m * k * sa)
                cost = (moved + m * n * so) / HBM_BYTES_PER_US + steps * STEP_OVERHEAD_US
                if best is None or cost < best[0]:
                    best = (cost, tm, tn, inner_n)
    return best[1:]


def mm(a, b, *, ta=False, tb=False, add=None, out_dtype=F32, name):
    if ta:
        kdim, m = a.shape
    else:
        m, kdim = a.shape
    n = b.shape[0] if tb else b.shape[1]
    so = jnp.dtype(out_dtype).itemsize + (0 if add is None else add.dtype.itemsize)
    tm, tn, inner_n = _mm_tiles(m, n, kdim, a.dtype.itemsize, b.dtype.itemsize, so)
    dn = (((0 if ta else 1,), (1 if tb else 0,)), ((), ()))

    def body(*refs):
        if add is None:
            a_ref, b_ref, o_ref = refs
        else:
            a_ref, b_ref, add_ref, o_ref = refs
        r = lax.dot_general(a_ref[...].astype(BF16), b_ref[...].astype(BF16), dn, preferred_element_type=F32)
        if add is not None:
            r = r + add_ref[...].astype(F32)
        o_ref[...] = r.astype(out_dtype)

    ij = (lambda g0, g1: (g0, g1)) if inner_n else (lambda g0, g1: (g1, g0))
    a_spec = (pl.BlockSpec((kdim, tm), lambda g0, g1: (0, ij(g0, g1)[0])) if ta
              else pl.BlockSpec((tm, kdim), lambda g0, g1: (ij(g0, g1)[0], 0)))
    b_spec = (pl.BlockSpec((tn, kdim), lambda g0, g1: (ij(g0, g1)[1], 0)) if tb
              else pl.BlockSpec((kdim, tn), lambda g0, g1: (0, ij(g0, g1)[1])))
    o_spec = pl.BlockSpec((tm, tn), lambda g0, g1: ij(g0, g1))
    ins, specs = [a, b], [a_spec, b_spec]
    if add is not None:
        ins.append(add)
        specs.append(o_spec)
    grid = (m // tm, n // tn) if inner_n else (n // tn, m // tm)
    return pl.pallas_call(
        body, name=name, grid=grid, in_specs=specs, out_specs=o_spec,
        out_shape=jax.ShapeDtypeStruct((m, n), out_dtype),
        compiler_params=_cparams("parallel", "parallel"))(*ins)


def mm_nt_segments(segs, b, *, name):
    m = segs[0].shape[0]
    n, kdim = b.shape
    widths = [sg.shape[1] for sg in segs]
    offs = [sum(widths[:i]) for i in range(len(segs))]
    assert sum(widths) == kdim and all(o % wd == 0 for o, wd in zip(offs, widths))
    sa = max(sg.dtype.itemsize for sg in segs)
    tm, tn, inner_n = _mm_tiles(m, n, kdim, sa, b.dtype.itemsize, 4)
    ns = len(segs)

    def body(*refs):
        o_ref = refs[-1]
        r = None
        for a_ref, b_ref in zip(refs[:ns], refs[ns:2 * ns]):
            d = lax.dot_general(a_ref[...].astype(BF16), b_ref[...].astype(BF16), NT, preferred_element_type=F32)
            r = d if r is None else r + d
        o_ref[...] = r

    ij = (lambda g0, g1: (g0, g1)) if inner_n else (lambda g0, g1: (g1, g0))
    a_specs = [pl.BlockSpec((tm, wd), lambda g0, g1: (ij(g0, g1)[0], 0)) for wd in widths]
    b_specs = [pl.BlockSpec((tn, wd), lambda g0, g1, blk=o // wd: (ij(g0, g1)[1], blk)) for o, wd in zip(offs, widths)]
    grid = (m // tm, n // tn) if inner_n else (n // tn, m // tm)
    return pl.pallas_call(
        body, name=name, grid=grid, in_specs=a_specs + b_specs,
        out_specs=pl.BlockSpec((tm, tn), lambda g0, g1: ij(g0, g1)),
        out_shape=jax.ShapeDtypeStruct((m, n), F32),
        compiler_params=_cparams("parallel", "parallel"))(*segs, *([b] * ns))


def rms_fwd(x, g, *, col=0, out_dtype=BF16, name):
    s = x.shape[0]
    d = g.shape[1]
    tm = _rows(s)

    def body(x_ref, g_ref, o_ref):
        xv = x_ref[...]
        r = lax.rsqrt(jnp.mean(xv * xv, axis=-1, keepdims=True) + EPS)
        o_ref[...] = (xv * r * g_ref[...]).astype(out_dtype)

    return pl.pallas_call(
        body, name=name, grid=(s // tm,),
        in_specs=[pl.BlockSpec((tm, d), lambda i: (i, col)), pl.BlockSpec((1, d), lambda i: (0, 0))],
        out_specs=pl.BlockSpec((tm, d), lambda i: (i, 0)),
        out_shape=jax.ShapeDtypeStruct((s, d), out_dtype),
        compiler_params=_cparams("parallel"))(x, g)


def rms_bwd(x, g, dy, *, col=0, add=None, out_dtype=F32, twin=False, name):
    s = x.shape[0]
    d = g.shape[1]
    tm = _rows(s)

    def body(*refs):
        refs = list(refs)
        dg_ref = refs.pop()
        dxh_ref = refs.pop() if twin else None
        dx_ref = refs.pop()
        add_ref = refs.pop() if add is not None else None
        x_ref, g_ref, dy_ref = refs

        @pl.when(pl.program_id(0) == 0)
        def _():
            dg_ref[...] = jnp.zeros_like(dg_ref)

        xv = x_ref[...]
        dyv = dy_ref[...].astype(F32)
        r = lax.rsqrt(jnp.mean(xv * xv, axis=-1, keepdims=True) + EPS)
        xh = xv * r
        dg_ref[...] += jnp.sum(dyv * xh, axis=0, keepdims=True)
        dxh = dyv * g_ref[...]
        dx = r * (dxh - xh * jnp.mean(dxh * xh, axis=-1, keepdims=True))
        if add is not None:
            dx = dx + add_ref[...]
        dx_ref[...] = dx.astype(out_dtype)
        if twin:
            dxh_ref[...] = dx.astype(BF16)

    row = pl.BlockSpec((tm, d), lambda i: (i, 0))
    vec = pl.BlockSpec((1, d), lambda i: (0, 0))
    ins = [x, g, dy]
    specs = [pl.BlockSpec((tm, d), lambda i: (i, col)), vec, row]
    if add is not None:
        ins.append(add)
        specs.append(row)
    dxs = [jax.ShapeDtypeStruct((s, d), out_dtype)] + ([jax.ShapeDtypeStruct((s, d), BF16)] if twin else [])
    return pl.pallas_call(
        body, name=name, grid=(s // tm,), in_specs=specs,
        out_specs=[row] * len(dxs) + [vec],
        out_shape=dxs + [jax.ShapeDtypeStruct((1, d), F32)],
        compiler_params=_cparams("arbitrary"))(*ins)


def _rope_tables(s):
    inv = 1.0 / (ROPE_THETA ** (jnp.arange(0, ROPE, 2, dtype=F32) / ROPE))
    ang = jnp.arange(s, dtype=F32)[:, None] * inv[None, :]
    cos, sin = jnp.cos(ang), jnp.sin(ang)
    z = lambda w: jnp.zeros((s, w), F32)
    c = jnp.concatenate([jnp.ones((s, NOPE), F32), cos, cos, z(HP - QK)], axis=1)
    s1 = jnp.concatenate([z(NOPE), -sin, z(HP - NOPE - ROPE // 2)], axis=1)
    s2 = jnp.concatenate([z(NOPE + ROPE // 2), sin, z(HP - QK)], axis=1)
    return c, s1, s2


def qkprep_fwd(qraw, kv, proj, qg, kg, tabs, *, name):
    s = qraw.shape[0]
    tm = _rows(s)
    kr_col = (proj.shape[1] - HP) // HP

    def body(q_ref, k_ref, v_ref, kr_ref, qg_ref, kg_ref, c_ref, s1_ref, s2_ref, qo_ref, ko_ref, vo_ref):
        c, s1, s2 = c_ref[...], s1_ref[...], s2_ref[...]

        def f(xv, gain):
            r = lax.rsqrt(jnp.sum(xv * xv, axis=-1, keepdims=True) * (1.0 / QK) + EPS)
            xn = xv * r * gain
            return xn * c + pltpu.roll(xn, HP - ROPE // 2, 1) * s1 + pltpu.roll(xn, ROPE // 2, 1) * s2

        qo_ref[...] = (f(q_ref[...], qg_ref[...]) * Q_SCALE).astype(BF16)
        ko_ref[...] = f(k_ref[...] + kr_ref[...], kg_ref[...]).astype(BF16)
        vv = v_ref[...]
        lane = lax.broadcasted_iota(jnp.int32, vv.shape, 1)
        vo_ref[...] = jnp.where(lane == VD, 1.0, vv).astype(BF16)

    head = pl.BlockSpec((tm, HP), lambda i, h: (i, h))
    tab = pl.BlockSpec((tm, HP), lambda i, h: (i, 0))
    gain = pl.BlockSpec((1, HP), lambda i, h: (0, 0))
    return pl.pallas_call(
        body, name=name, grid=(s // tm, HEADS),
        in_specs=[head, head, pl.BlockSpec((tm, HP), lambda i, h: (i, HEADS + h)),
                  pl.BlockSpec((tm, HP), lambda i, h: (i, kr_col)), gain, gain, tab, tab, tab],
        out_specs=[head, head, head],
        out_shape=[jax.ShapeDtypeStruct((s, HEADS * HP), BF16)] * 3,
        compiler_params=_cparams("parallel", "parallel"))(qraw, kv, kv, proj, qg, kg, *tabs)


def qkprep_bwd(qraw, kv, proj, qg, kg, tabs, dq, dk, *, name):
    s = qraw.shape[0]
    tm = _rows(s)
    kr_col = (proj.shape[1] - HP) // HP

    def body(q_ref, k_ref, kr_ref, qg_ref, kg_ref, c_ref, s1_ref, s2_ref, dq_ref, dk_ref,
             dqr_ref, dkr_ref, dkrope_ref, dqg_ref, dkg_ref):
        i, h = pl.program_id(0), pl.program_id(1)
        c, s1, s2 = c_ref[...], s1_ref[...], s2_ref[...]

        @pl.when((i == 0) & (h == 0))
        def _():
            dqg_ref[...] = jnp.zeros_like(dqg_ref)
            dkg_ref[...] = jnp.zeros_like(dkg_ref)

        @pl.when(h == 0)
        def _():
            dkrope_ref[...] = jnp.zeros_like(dkrope_ref)

        def f(xv, gain, dout):
            r = lax.rsqrt(jnp.sum(xv * xv, axis=-1, keepdims=True) * (1.0 / QK) + EPS)
            xh = xv * r
            dxn = dout * c + pltpu.roll(dout * s1, ROPE // 2, 1) + pltpu.roll(dout * s2, HP - ROPE // 2, 1)
            dgain = jnp.sum(dxn * xh, axis=0, keepdims=True)
            dxh = dxn * gain
            dx = r * (dxh - xh * (jnp.sum(dxh * xh, axis=-1, keepdims=True) * (1.0 / QK)))
            return dx, dgain

        dxq, dgq = f(q_ref[...], qg_ref[...], dq_ref[...])
        dxk, dgk = f(k_ref[...] + kr_ref[...], kg_ref[...], dk_ref[...])
        dqr_ref[...] = dxq.astype(BF16)
        dkr_ref[...] = dxk.astype(BF16)
        dqg_ref[...] += dgq
        dkg_ref[...] += dgk
        lane = lax.broadcasted_iota(jnp.int32, dxk.shape, 1)
        dkrope_ref[...] += jnp.where((lane >= NOPE) & (lane < QK), dxk, 0.0)

    head = pl.BlockSpec((tm, HP), lambda i, h: (i, h))
    tab = pl.BlockSpec((tm, HP), lambda i, h: (i, 0))
    gain = pl.BlockSpec((1, HP), lambda i, h: (0, 0))
    return pl.pallas_call(
        body, name=name, grid=(s // tm, HEADS),
        in_specs=[head, head, pl.BlockSpec((tm, HP), lambda i, h: (i, kr_col)), gain, gain, tab, tab, tab, head, head],
        out_specs=[head, head, tab, gain, gain],
        out_shape=[jax.ShapeDtypeStruct((s, HEADS * HP), BF16)] * 2
        + [jax.ShapeDtypeStruct((s, HP), F32), jax.ShapeDtypeStruct((1, HP), F32), jax.ShapeDtypeStruct((1, HP), F32)],
        compiler_params=_cparams("arbitrary", "arbitrary"))(qraw, kv, proj, qg, kg, *tabs, dq, dk)


ATT_SCALE = QK ** -0.5
LOG2E = math.log2(math.e)
Q_SCALE = ATT_SCALE * LOG2E
NEG = -1e30


def _att_tile(s):
    return _pick(s, (512, 256, 128))


def _causal(sv, diag):
    r = lax.broadcasted_iota(jnp.int32, sv.shape, 0)
    c = lax.broadcasted_iota(jnp.int32, sv.shape, 1)
    return jnp.where(diag & (c > r), NEG, sv)


NT = (((1,), (1,)), ((), ()))
TN = (((0,), (0,)), ((), ()))


def _row_of(col):
    return jnp.broadcast_to(col, (col.shape[0], LANE)).T[0:SLAB, :]


def _att_specs(s):
    t = _att_tile(s)
    nb = s // t
    tile = pl.BlockSpec((t, HP), lambda h, i: (i, h))
    whole = pl.BlockSpec((s, HP), lambda h, i: (0, h))
    row = pl.BlockSpec((1, 1, SLAB, t), lambda h, i: (h, i, 0, 0))
    rows = pl.BlockSpec((1, nb, SLAB, t), lambda h, i: (h, 0, 0, 0))
    return t, nb, tile, whole, row, rows


def attn_fwd(q, k, v, *, name):
    s = q.shape[0]
    t, nb, tile, whole, row, _ = _att_specs(s)

    def body(q_ref, k_ref, v_ref, o_ref, oh_ref, lset_ref, s_scr, mb_scr, acc):
        qb = pl.program_id(1)
        qv = q_ref[...]

        def scores(j):
            r0 = pl.multiple_of(j * t, t)
            return lax.dot_general(qv, k_ref[pl.ds(r0, t), :], NT, preferred_element_type=F32)

        def fold(sv):
            m = sv[:, 0:LANE]
            for kk in range(1, t // LANE):
                m = jnp.maximum(m, sv[:, kk * LANE:(kk + 1) * LANE])
            return m

        def first(j, m):
            sv = scores(j)
            s_scr[j] = sv
            return jnp.maximum(m, fold(sv))

        m = lax.fori_loop(0, qb, first, jnp.full((t, LANE), NEG, F32))
        sd = _causal(scores(qb), True)
        s_scr[qb] = sd
        mcol = jnp.max(jnp.maximum(m, fold(sd)), axis=-1, keepdims=True)
        mb_scr[...] = jnp.broadcast_to(mcol, (t, t))
        acc[...] = jnp.zeros_like(acc)

        def second(j, carry):
            r0 = pl.multiple_of(j * t, t)
            p = jnp.exp2(s_scr[j] - mb_scr[...]).astype(BF16)
            acc[...] += jnp.dot(p, v_ref[pl.ds(r0, t), :], preferred_element_type=F32)
            return carry

        lax.fori_loop(0, qb + 1, second, 0)
        av = acc[...]
        lsum = av[:, VD:VD + 1]
        lane = lax.broadcasted_iota(jnp.int32, av.shape, 1)
        ov = jnp.where(lane == VD, 0.0, av / lsum)
        o_ref[...] = ov
        oh_ref[...] = ov.astype(BF16)
        lset_ref[0, 0] = _row_of(mcol + jnp.log2(lsum))

    return pl.pallas_call(
        body, name=name, grid=(HEADS, nb), in_specs=[tile, whole, whole], out_specs=[tile, tile, row],
        out_shape=[jax.ShapeDtypeStruct((s, HEADS * HP), F32), jax.ShapeDtypeStruct((s, HEADS * HP), BF16),
                   jax.ShapeDtypeStruct((HEADS, nb, SLAB, t), F32)],
        scratch_shapes=[pltpu.VMEM((nb, t, t), F32), pltpu.VMEM((t, t), F32), pltpu.VMEM((t, HP), F32)],
        compiler_params=_cparams("parallel", "parallel"))(q, k, v)


def attn_delta(o, do, *, name):
    s = o.shape[0]
    t, nb, tile, _, row, _ = _att_specs(s)

    def body(o_ref, do_ref, doh_ref, dt_ref):
        dov = do_ref[...]
        doh_ref[...] = dov.astype(BF16)
        dt_ref[0, 0] = _row_of(jnp.sum(dov * o_ref[...], axis=-1, keepdims=True))

    return pl.pallas_call(
        body, name=name, grid=(HEADS, nb), in_specs=[tile, tile], out_specs=[tile, row],
        out_shape=[jax.ShapeDtypeStruct((s, HEADS * HP), BF16), jax.ShapeDtypeStruct((HEADS, nb, SLAB, t), F32)],
        compiler_params=_cparams("parallel", "parallel"))(o, do)


def attn_bwd(q, k, v, doh, lset, deltat, *, name):
    s = q.shape[0]
    t, nb, tile, whole, _, rows = _att_specs(s)

    def body(q_ref, k_ref, v_ref, do_ref, lt_ref, dt_ref, dq_ref, dk_ref, dv_ref, dk_acc, dv_acc):
        kb = pl.program_id(1)
        kt, vt = k_ref[...], v_ref[...]
        dk_acc[...] = jnp.zeros_like(dk_acc)
        dv_acc[...] = jnp.zeros_like(dv_acc)

        @pl.when(kb == 0)
        def _():
            dq_ref[...] = jnp.zeros_like(dq_ref)

        def step(i, diag):
            r0 = pl.multiple_of(i * t, t)
            qi, doi = q_ref[pl.ds(r0, t), :], do_ref[pl.ds(r0, t), :]
            st = lax.dot_general(kt, qi, NT, preferred_element_type=F32)
            if diag:
                kr = lax.broadcasted_iota(jnp.int32, st.shape, 0)
                qc = lax.broadcasted_iota(jnp.int32, st.shape, 1)
                st = jnp.where(kr > qc, NEG, st)
            pt = jnp.exp2(st - lt_ref[0, i][0:1, :])
            dpt = lax.dot_general(vt, doi, NT, preferred_element_type=F32)
            dst = (pt * (dpt - dt_ref[0, i][0:1, :])).astype(BF16)
            dv_acc[...] += jnp.dot(pt.astype(BF16), doi, preferred_element_type=F32)
            dk_acc[...] += jnp.dot(dst, qi, preferred_element_type=F32)
            dq_ref[pl.ds(r0, t), :] += lax.dot_general(dst, kt, TN, preferred_element_type=F32) * ATT_SCALE

        def off_diag(i, carry):
            step(i, False)
            return carry

        step(kb, True)
        lax.fori_loop(kb + 1, nb, off_diag, 0)
        dk_ref[...] = dk_acc[...] * (1.0 / LOG2E)
        dvv = dv_acc[...]
        lane = lax.broadcasted_iota(jnp.int32, dvv.shape, 1)
        dv_ref[...] = jnp.where(lane == VD, 0.0, dvv).astype(BF16)

    return pl.pallas_call(
        body, name=name, grid=(HEADS, nb), in_specs=[whole, tile, tile, whole, rows, rows],
        out_specs=[whole, tile, tile],
        out_shape=[jax.ShapeDtypeStruct((s, HEADS * HP), F32), jax.ShapeDtypeStruct((s, HEADS * HP), F32),
                   jax.ShapeDtypeStruct((s, HEADS * HP), BF16)],
        scratch_shapes=[pltpu.VMEM((t, HP), F32), pltpu.VMEM((t, HP), F32)],
        compiler_params=_cparams("parallel", "arbitrary"))(q, k, v, doh, lset, deltat)


HALO = 8
CW = 256


def _shifts(zw):
    return zw, pltpu.roll(zw, 1, 0), pltpu.roll(zw, 2, 0)


def _conv3(sh, w):
    return w[2:3] * sh[0] + w[1:2] * sh[1] + w[0:1] * sh[2]


def _conv3_t(dc, w):
    n = dc.shape[0]
    return w[2:3] * dc + w[1:2] * pltpu.roll(dc, n - 1, 0) + w[0:1] * pltpu.roll(dc, n - 2, 0)


def _conv3_dw(dc, sh, r):
    return [jnp.sum(dc * z[HALO:HALO + r], axis=0, keepdims=True) for z in (sh[2], sh[1], sh[0])]


def _halo_specs(r, colfn):
    rb = r // HALO
    cur = pl.BlockSpec((r, CW), lambda j, i: (i, colfn(j)))
    prev = pl.BlockSpec((HALO, CW), lambda j, i: (jnp.maximum(i * rb - 1, 0), colfn(j)))

    def nxt(nrow_blocks):
        return pl.BlockSpec((HALO, CW), lambda j, i: (jnp.minimum((i + 1) * rb, nrow_blocks * rb - 1), colfn(j)))

    return cur, prev, nxt


def ffnact_fwd(up, w, *, name):
    s, c2 = up.shape
    hh = c2 // 2
    nj = hh // CW
    r = _long_rows(s)
    nt = s // r

    def body(g_ref, gp_ref, v_ref, vp_ref, wg_ref, wv_ref, o_ref):
        pm = (pl.program_id(1) > 0).astype(F32)
        cg = _conv3(_shifts(jnp.concatenate([gp_ref[...] * pm, g_ref[...]], axis=0)), wg_ref[...])[HALO:]
        cv = _conv3(_shifts(jnp.concatenate([vp_ref[...] * pm, v_ref[...]], axis=0)), wv_ref[...])[HALO:]
        o_ref[...] = (cg * jax.nn.sigmoid(cg) * cv).astype(BF16)

    gcur, gprev, _ = _halo_specs(r, lambda j: j)
    vcur, vprev, _ = _halo_specs(r, lambda j: nj + j)
    wg = pl.BlockSpec((3, CW), lambda j, i: (0, j))
    wv = pl.BlockSpec((3, CW), lambda j, i: (0, nj + j))
    return pl.pallas_call(
        body, name=name, grid=(nj, nt), in_specs=[gcur, gprev, vcur, vprev, wg, wv],
        out_specs=pl.BlockSpec((r, CW), lambda j, i: (i, j)),
        out_shape=jax.ShapeDtypeStruct((s, hh), BF16),
        compiler_params=_cparams("parallel", "parallel"))(up, up, up, up, w, w)


def ffnact_bwd(up, w, dact, *, name):
    s, c2 = up.shape
    hh = c2 // 2
    nj = hh // CW
    r = _rows(s)
    nt = s // r

    def body(g_ref, gp_ref, gn_ref, v_ref, vp_ref, vn_ref, wg_ref, wv_ref, da_ref, dan_ref,
             dg_ref, dv_ref, dwg_ref, dwv_ref):
        i = pl.program_id(1)
        pm = (i > 0).astype(F32)
        nm = (i < nt - 1).astype(F32)

        @pl.when(i == 0)
        def _():
            dwg_ref[...] = jnp.zeros_like(dwg_ref)
            dwv_ref[...] = jnp.zeros_like(dwv_ref)

        wg, wv = wg_ref[...], wv_ref[...]
        zg = jnp.concatenate([gp_ref[...] * pm, g_ref[...], gn_ref[...]], axis=0)
        zv = jnp.concatenate([vp_ref[...] * pm, v_ref[...], vn_ref[...]], axis=0)
        zg, zv = _shifts(zg), _shifts(zv)
        cg = _conv3(zg, wg)[HALO:]
        cv = _conv3(zv, wv)[HALO:]
        da = jnp.concatenate([da_ref[...], dan_ref[...] * nm], axis=0)
        sg = jax.nn.sigmoid(cg)
        dcg = da * cv * (sg * (1.0 + cg * (1.0 - sg)))
        dcv = da * (cg * sg)
        dg_ref[...] = _conv3_t(dcg, wg)[:r].astype(BF16)
        dv_ref[...] = _conv3_t(dcv, wv)[:r].astype(BF16)
        for kk, (a, b) in enumerate(zip(_conv3_dw(dcg[:r], zg, r), _conv3_dw(dcv[:r], zv, r))):
            dwg_ref[kk:kk + 1, :] += a
            dwv_ref[kk:kk + 1, :] += b

    gcur, gprev, gnext = _halo_specs(r, lambda j: j)
    vcur, vprev, vnext = _halo_specs(r, lambda j: nj + j)
    acur, _, anext = _halo_specs(r, lambda j: j)
    wg = pl.BlockSpec((3, CW), lambda j, i: (0, j))
    wv = pl.BlockSpec((3, CW), lambda j, i: (0, nj + j))
    dupg, dupv, dwg, dwv = pl.pallas_call(
        body, name=name, grid=(nj, nt),
        in_specs=[gcur, gprev, gnext(nt), vcur, vprev, vnext(nt), wg, wv, acur, anext(nt)],
        out_specs=[acur, acur, wg, wg],
        out_shape=[jax.ShapeDtypeStruct((s, hh), BF16), jax.ShapeDtypeStruct((s, hh), BF16),
                   jax.ShapeDtypeStruct((3, hh), F32), jax.ShapeDtypeStruct((3, hh), F32)],
        compiler_params=_cparams("parallel", "arbitrary"))(up, up, up, up, up, up, w, w, dact, dact)
    return (dupg, dupv), jnp.concatenate([dwg, dwv], axis=1)


def sconv_fwd(proj, w, *, name):
    s = proj.shape[0]
    nj = CONVC // CW
    r = _long_rows(s)
    nt = s // r

    def body(b_ref, c_ref, cp_ref, x_ref, xp_ref, w_ref, o_ref):
        pm = (pl.program_id(1) > 0).astype(F32)
        zw = jnp.concatenate([cp_ref[...] * xp_ref[...] * pm, c_ref[...] * x_ref[...]], axis=0)
        o_ref[...] = (b_ref[...] * _conv3(_shifts(zw), w_ref[...])[HALO:]).astype(BF16)

    bcur, _, _ = _halo_specs(r, lambda j: (QR + KVR) // CW + j)
    ccur, cprev, _ = _halo_specs(r, lambda j: (QR + KVR + CONVC) // CW + j)
    xcur, xprev, _ = _halo_specs(r, lambda j: (QR + KVR + 2 * CONVC) // CW + j)
    ws = pl.BlockSpec((3, CW), lambda j, i: (0, j))
    return pl.pallas_call(
        body, name=name, grid=(nj, nt), in_specs=[bcur, ccur, cprev, xcur, xprev, ws],
        out_specs=pl.BlockSpec((r, CW), lambda j, i: (i, j)),
        out_shape=jax.ShapeDtypeStruct((s, CONVC), BF16),
        compiler_params=_cparams("parallel", "parallel"))(proj, proj, proj, proj, proj, w)


def sconv_bwd(proj, w, dy, *, name):
    s = proj.shape[0]
    nj = CONVC // CW
    r = _rows(s)
    nt = s // r

    def body(b_ref, bn_ref, c_ref, cp_ref, x_ref, xp_ref, w_ref, dy_ref, dyn_ref, db_ref, dc_ref, dx_ref, dw_ref):
        i = pl.program_id(1)
        pm = (i > 0).astype(F32)
        nm = (i < nt - 1).astype(F32)

        @pl.when(i == 0)
        def _():
            dw_ref[...] = jnp.zeros_like(dw_ref)

        wv = w_ref[...]
        zw = jnp.concatenate([cp_ref[...] * xp_ref[...] * pm, c_ref[...] * x_ref[...]], axis=0)
        zw = _shifts(zw)
        conv = _conv3(zw, wv)[HALO:]
        dyv = dy_ref[...]
        db_ref[...] = (dyv * conv).astype(BF16)
        dconv = jnp.concatenate([dyv * b_ref[...], dyn_ref[...] * bn_ref[...] * nm], axis=0)
        dz = _conv3_t(dconv, wv)[:r]
        dc_ref[...] = (dz * x_ref[...]).astype(BF16)
        dx_ref[...] = (dz * c_ref[...]).astype(BF16)
        for kk, a in enumerate(_conv3_dw(dconv[:r], zw, r)):
            dw_ref[kk:kk + 1, :] += a

    bcur, _, bnext = _halo_specs(r, lambda j: (QR + KVR) // CW + j)
    ccur, cprev, _ = _halo_specs(r, lambda j: (QR + KVR + CONVC) // CW + j)
    xcur, xprev, _ = _halo_specs(r, lambda j: (QR + KVR + 2 * CONVC) // CW + j)
    ycur, _, ynext = _halo_specs(r, lambda j: j)
    ws = pl.BlockSpec((3, CW), lambda j, i: (0, j))
    out = pl.BlockSpec((r, CW), lambda j, i: (i, j))
    db, dc, dx, dw = pl.pallas_call(
        body, name=name, grid=(nj, nt),
        in_specs=[bcur, bnext(nt), ccur, cprev, xcur, xprev, ws, ycur, ynext(nt)],
        out_specs=[out, out, out, ws],
        out_shape=[jax.ShapeDtypeStruct((s, CONVC), BF16)] * 3 + [jax.ShapeDtypeStruct((3, CONVC), F32)],
        compiler_params=_cparams("parallel", "arbitrary"))(proj, proj, proj, proj, proj, proj, w, dy, dy)
    return (db, dc, dx), dw


SW = 512
NJ = NST // SW


def _scan_tables(ar, ai, cl):
    pr, pi = ar[None], ai[None]
    while pr.shape[0] < cl:
        mr, mi = pr[-1:], pi[-1:]
        pr, pi = jnp.concatenate([pr, pr * mr - pi * mi]), jnp.concatenate([pi, pr * mi + pi * mr])
    return jnp.stack([pr, pi, pr[::-1], -pi[::-1]])


def _block_scan(xr_ref, xi_ref, pr, pi, c_r, c_i, reverse, out):
    cl, lanes = pr.shape
    ncol = lanes // LANE
    edge = cl - 1 if reverse else 0
    col = lambda v, c: v[:, c * LANE:(c + 1) * LANE]
    ar = [jnp.broadcast_to(col(pr[edge:edge + 1], c), (SLAB, LANE)) for c in range(ncol)]
    ai = [jnp.broadcast_to(col(pi[edge:edge + 1], c), (SLAB, LANE)) for c in range(ncol)]

    def step(n, carry):
        k = cl - 1 - n if reverse else n
        rows = pl.ds(k, SLAB, stride=cl)
        new = []
        for c in range(ncol):
            sr = ar[c] * carry[2 * c] - ai[c] * carry[2 * c + 1] + xr_ref[c, rows, :]
            si = ar[c] * carry[2 * c + 1] + ai[c] * carry[2 * c] + xi_ref[c, rows, :]
            xr_ref[c, rows, :] = sr
            xi_ref[c, rows, :] = si
            new += [sr, si]
        return tuple(new)

    last = lax.fori_loop(0, cl, step, tuple(jnp.zeros((SLAB, LANE), F32) for _ in range(2 * ncol)))
    lr = jnp.concatenate(last[0::2], axis=1)
    li = jnp.concatenate(last[1::2], axis=1)
    far = cl - 1 - edge
    qr, qi = pr[far:far + 1], pi[far:far + 1]
    for r in (range(SLAB - 1, -1, -1) if reverse else range(SLAB)):
        rows = slice(r * cl, (r + 1) * cl)
        xr = jnp.concatenate([xr_ref[c, rows, :] for c in range(ncol)], axis=1)
        xi = jnp.concatenate([xi_ref[c, rows, :] for c in range(ncol)], axis=1)
        out(rows, xr + pr * c_r - pi * c_i, xi + pr * c_i + pi * c_r)
        c_r, c_i = qr * c_r - qi * c_i + lr[r:r + 1], qr * c_i + qi * c_r + li[r:r + 1]
    return c_r, c_i


def _to_columns(ref, val):
    for c in range(ref.shape[0]):
        ref[c] = val[:, c * LANE:(c + 1) * LANE]


def s5_fwd(u, bbd_r, bbd_i, cbd_r, cbd_i, tab, *, name):
    s = u.shape[0]
    tbk = _long_rows(s)
    nt = s // tbk
    cl = tbk // SLAB

    def body(u_ref, br_ref, bi_ref, cr_ref, ci_ref, tab_ref, y_ref, sr_ref, si_ref, bur, bui, carry):
        @pl.when(pl.program_id(1) == 0)
        def _():
            carry[...] = jnp.zeros_like(carry)

        ub = u_ref[...].astype(BF16)
        _to_columns(bur, jnp.dot(ub, br_ref[0], preferred_element_type=F32))
        _to_columns(bui, jnp.dot(ub, bi_ref[0], preferred_element_type=F32))

        def out(rows, sr, si):
            sr_ref[rows, :] = sr
            si_ref[rows, :] = si

        c_r, c_i = _block_scan(bur, bui, tab_ref[0], tab_ref[1], carry[0, 0:1], carry[1, 0:1], False, out)
        carry[0, 0:1] = c_r
        carry[1, 0:1] = c_i
        y_ref[...] = (jnp.dot(sr_ref[...].astype(BF16), cr_ref[0], preferred_element_type=F32)
                      - jnp.dot(si_ref[...].astype(BF16), ci_ref[0], preferred_element_type=F32))

    us = pl.BlockSpec((tbk, LANE), lambda j, t: (t, j))
    bs = pl.BlockSpec((1, LANE, SW), lambda j, t: (j, 0, 0))
    cs = pl.BlockSpec((1, SW, LANE), lambda j, t: (j, 0, 0))
    ts = pl.BlockSpec((2, cl, SW), lambda j, t: (0, 0, j))
    ss = pl.BlockSpec((tbk, SW), lambda j, t: (t, j))
    return pl.pallas_call(
        body, name=name, grid=(NJ, nt), in_specs=[us, bs, bs, cs, cs, ts], out_specs=[us, ss, ss],
        out_shape=[jax.ShapeDtypeStruct((s, D_MODEL), F32), jax.ShapeDtypeStruct((s, NST), F32),
                   jax.ShapeDtypeStruct((s, NST), F32)],
        scratch_shapes=[pltpu.VMEM((SW // LANE, tbk, LANE), F32), pltpu.VMEM((SW // LANE, tbk, LANE), F32),
                        pltpu.VMEM((2, SLAB, SW), F32)],
        compiler_params=_cparams("parallel", "arbitrary"))(u, bbd_r, bbd_i, cbd_r, cbd_i, tab)


def s5_bwd(u, dy, dskip, st_r, st_i, bbd_r, bbd_i, cbd_r, cbd_i, tabrev, *, name):
    s = u.shape[0]
    tbk = _long_rows(s)
    nt = s // tbk
    cl = tbk // SLAB
    rbk = tbk // SLAB

    def body(u_ref, dy_ref, d_ref, sr_ref, si_ref, pr_ref, pi_ref, br_ref, bi_ref, cr_ref, ci_ref, tab_ref,
             du_ref, dbr_ref, dbi_ref, dcr_ref, dci_ref, da_ref, lam_r, lam_i, lam_f, carry):
        t = pl.program_id(1)

        @pl.when(t == 0)
        def _():
            carry[...] = jnp.zeros_like(carry)
            dbr_ref[...] = jnp.zeros_like(dbr_ref)
            dbi_ref[...] = jnp.zeros_like(dbi_ref)
            dcr_ref[...] = jnp.zeros_like(dcr_ref)
            dci_ref[...] = jnp.zeros_like(dci_ref)
            da_ref[...] = jnp.zeros_like(da_ref)

        dyv = dy_ref[...]
        dyh = dyv.astype(BF16)
        _to_columns(lam_r, lax.dot_general(dyh, cr_ref[0], NT, preferred_element_type=F32))
        _to_columns(lam_i, -lax.dot_general(dyh, ci_ref[0], NT, preferred_element_type=F32))

        def out(rows, vr, vi):
            lam_f[0, rows, :] = vr
            lam_f[1, rows, :] = vi

        c_r, c_i = _block_scan(lam_r, lam_i, tab_ref[0], tab_ref[1], carry[0, 0:1], carry[1, 0:1], True, out)
        carry[0, 0:1] = c_r
        carry[1, 0:1] = c_i
        lr, li = lam_f[0], lam_f[1]
        lrh, lih = lr.astype(BF16), li.astype(BF16)
        du = (dyv * d_ref[...] + lax.dot_general(lrh, br_ref[0], NT, preferred_element_type=F32)
              + lax.dot_general(lih, bi_ref[0], NT, preferred_element_type=F32))
        du_ref[...] = du.astype(BF16)
        ub = u_ref[...].astype(BF16)
        dbr_ref[0] += lax.dot_general(ub, lrh, TN, preferred_element_type=F32)
        dbi_ref[0] += lax.dot_general(ub, lih, TN, preferred_element_type=F32)
        srv, siv = sr_ref[...], si_ref[...]
        dcr_ref[0] += lax.dot_general(srv.astype(BF16), dyh, TN, preferred_element_type=F32)
        dci_ref[0] -= lax.dot_general(siv.astype(BF16), dyh, TN, preferred_element_type=F32)
        first = lax.broadcasted_iota(jnp.int32, srv.shape, 0) == 0
        pm = (t < nt - 1).astype(F32)
        spr = jnp.where(first, pr_ref[SLAB - 1:SLAB, :] * pm, pltpu.roll(srv, 1, 0))
        spi = jnp.where(first, pi_ref[SLAB - 1:SLAB, :] * pm, pltpu.roll(siv, 1, 0))
        da_ref[0:1, :] += jnp.sum(lr * spr + li * spi, axis=0, keepdims=True)
        da_ref[1:2, :] += jnp.sum(li * spr - lr * spi, axis=0, keepdims=True)

    rv = lambda t: nt - 1 - t
    us = pl.BlockSpec((tbk, LANE), lambda j, t: (rv(t), j))
    ds = pl.BlockSpec((1, LANE), lambda j, t: (0, j))
    ss = pl.BlockSpec((tbk, SW), lambda j, t: (rv(t), j))
    ps = pl.BlockSpec((SLAB, SW), lambda j, t: (jnp.maximum(rv(t) * rbk - 1, 0), j))
    bs = pl.BlockSpec((1, LANE, SW), lambda j, t: (j, 0, 0))
    cs = pl.BlockSpec((1, SW, LANE), lambda j, t: (j, 0, 0))
    ts = pl.BlockSpec((2, cl, SW), lambda j, t: (1, 0, j))
    das = pl.BlockSpec((2, SW), lambda j, t: (0, j))
    return pl.pallas_call(
        body, name=name, grid=(NJ, nt),
        in_specs=[us, us, ds, ss, ss, ps, ps, bs, bs, cs, cs, ts],
        out_specs=[us, bs, bs, cs, cs, das],
        out_shape=[jax.ShapeDtypeStruct((s, D_MODEL), BF16),
                   jax.ShapeDtypeStruct((NJ, LANE, SW), F32), jax.ShapeDtypeStruct((NJ, LANE, SW), F32),
                   jax.ShapeDtypeStruct((NJ, SW, LANE), F32), jax.ShapeDtypeStruct((NJ, SW, LANE), F32),
                   jax.ShapeDtypeStruct((2, NST), F32)],
        scratch_shapes=[pltpu.VMEM((SW // LANE, tbk, LANE), F32), pltpu.VMEM((SW // LANE, tbk, LANE), F32),
                        pltpu.VMEM((2, tbk, SW), F32), pltpu.VMEM((2, SLAB, SW), F32)],
        compiler_params=_cparams("parallel", "arbitrary"))(
            u, dy, dskip, st_r, st_i, st_r, st_i, bbd_r, bbd_i, cbd_r, cbd_i, tabrev)


GELU_C = math.sqrt(2.0 / math.pi)
GELU_A = 0.044715


def s5post_fwd(y, u, dskip, *, name):
    s = y.shape[0]
    tm = _rows(s)

    def body(y_ref, u_ref, d_ref, o_ref):
        z = y_ref[...] + d_ref[...] * u_ref[...]
        o_ref[...] = (0.5 * z * (1.0 + jnp.tanh(GELU_C * (z + GELU_A * z * z * z)))).astype(BF16)

    row = pl.BlockSpec((tm, D_MODEL), lambda i: (i, 0))
    vec = pl.BlockSpec((1, D_MODEL), lambda i: (0, 0))
    return pl.pallas_call(body, name=name, grid=(s // tm,), in_specs=[row, row, vec], out_specs=row,
                          out_shape=jax.ShapeDtypeStruct((s, D_MODEL), BF16),
                          compiler_params=_cparams("parallel"))(y, u, dskip)


def s5post_bwd(y, u, dskip, dg, *, name):
    s = y.shape[0]
    tm = _rows(s)

    def body(y_ref, u_ref, d_ref, dg_ref, dz_ref, dd_ref):
        @pl.when(pl.program_id(0) == 0)
        def _():
            dd_ref[...] = jnp.zeros_like(dd_ref)

        uv = u_ref[...]
        z = y_ref[...] + d_ref[...] * uv
        th = jnp.tanh(GELU_C * (z + GELU_A * z * z * z))
        dgelu = 0.5 * (1.0 + th) + 0.5 * z * (1.0 - th * th) * (GELU_C * (1.0 + 3.0 * GELU_A * z * z))
        dz = dg_ref[...] * dgelu
        dz_ref[...] = dz
        dd_ref[...] += jnp.sum(dz * uv, axis=0, keepdims=True)

    row = pl.BlockSpec((tm, D_MODEL), lambda i: (i, 0))
    vec = pl.BlockSpec((1, D_MODEL), lambda i: (0, 0))
    return pl.pallas_call(body, name=name, grid=(s // tm,), in_specs=[row, row, vec, row], out_specs=[row, vec],
                          out_shape=[jax.ShapeDtypeStruct((s, D_MODEL), F32), jax.ShapeDtypeStruct((1, D_MODEL), F32)],
                          compiler_params=_cparams("arbitrary"))(y, u, dskip, dg)


def glu_fwd(glu, x, *, name):
    s = x.shape[0]
    tm = _rows(s)

    def body(a_ref, b_ref, x_ref, o_ref):
        o_ref[...] = x_ref[...] + a_ref[...] * jax.nn.sigmoid(b_ref[...])

    row = pl.BlockSpec((tm, D_MODEL), lambda i: (i, 0))
    return pl.pallas_call(body, name=name, grid=(s // tm,),
                          in_specs=[row, pl.BlockSpec((tm, D_MODEL), lambda i: (i, 1)), row], out_specs=row,
                          out_shape=jax.ShapeDtypeStruct((s, D_MODEL), F32),
                          compiler_params=_cparams("parallel"))(glu, glu, x)


def glu_bwd(glu, dx, *, name):
    s = dx.shape[0]
    tm = _rows(s)

    def body(a_ref, b_ref, dx_ref, o_ref):
        sg = jax.nn.sigmoid(b_ref[...])
        dxv = dx_ref[...]
        o_ref[:, :D_MODEL] = (dxv * sg).astype(BF16)
        o_ref[:, D_MODEL:] = (dxv * a_ref[...] * sg * (1.0 - sg)).astype(BF16)

    row = pl.BlockSpec((tm, D_MODEL), lambda i: (i, 0))
    return pl.pallas_call(body, name=name, grid=(s // tm,),
                          in_specs=[row, pl.BlockSpec((tm, D_MODEL), lambda i: (i, 1)), row],
                          out_specs=pl.BlockSpec((tm, 2 * D_MODEL), lambda i: (i, 0)),
                          out_shape=jax.ShapeDtypeStruct((s, 2 * D_MODEL), BF16),
                          compiler_params=_cparams("parallel"))(glu, glu, dx)


def loss_head(y, target, *, name):
    s = y.shape[0]
    tm = _rows(s)

    def body(y_ref, t_ref, dy_ref, dyh_ref, l_ref):
        @pl.when(pl.program_id(0) == 0)
        def _():
            l_ref[...] = jnp.zeros_like(l_ref)

        e = y_ref[...] - t_ref[...]
        dy_ref[...] = e * (1.0 / D_MODEL)
        dyh_ref[...] = (e * (1.0 / D_MODEL)).astype(BF16)
        e2 = jnp.sum((e * e).reshape(tm // 8, 8, D_MODEL), axis=0)
        acc = e2[:, 0:LANE]
        for kk in range(1, D_MODEL // LANE):
            acc = acc + e2[:, kk * LANE:(kk + 1) * LANE]
        l_ref[...] += acc

    row = pl.BlockSpec((tm, D_MODEL), lambda i: (i, 0))
    return pl.pallas_call(body, name=name, grid=(s // tm,), in_specs=[row, row],
                          out_specs=[row, row, pl.BlockSpec((8, LANE), lambda i: (0, 0))],
                          out_shape=[jax.ShapeDtypeStruct((s, D_MODEL), F32), jax.ShapeDtypeStruct((s, D_MODEL), BF16),
                                     jax.ShapeDtypeStruct((8, LANE), F32)],
                          compiler_params=_cparams("arbitrary"))(y, target)


PACKW = 1024
NCHIP = 4


def _mesh_pos():
    return lax.axis_index("x"), lax.axis_index("y"), lax.axis_index("c")


def _chip_exchange(bufs, scatter, name):
    n = len(bufs)
    shapes = [b.shape[1:] if scatter else b.shape for b in bufs]

    def body(*refs):
        ins, outs = refs[:n], refs[n:2 * n]
        send_sems, recv_sems, local_sems = refs[2 * n:]
        x, y, c = _mesh_pos()
        me = 2 * x + y
        peers = [(1 - x, y), (x, 1 - y), (1 - x, 1 - y)]

        def copy(a, j, px, py, dst_slot):
            src = ins[a].at[2 * px + py] if scatter else ins[a]
            return pltpu.make_async_remote_copy(src_ref=src, dst_ref=outs[a].at[dst_slot],
                                                send_sem=send_sems.at[3 * a + j], recv_sem=recv_sems.at[3 * a + j],
                                                device_id=(px, py, c), device_id_type=MESH)

        mine = [pltpu.make_async_copy(ins[a].at[me] if scatter else ins[a], outs[a].at[me], local_sems.at[a])
                for a in range(n)]
        sends = [copy(a, j, px, py, me) for a in range(n) for j, (px, py) in enumerate(peers)]
        for cp in mine + sends:
            cp.start()
        for a in range(n):
            for j, (px, py) in enumerate(peers):
                copy(a, j, px, py, 2 * px + py).wait_recv()
        for cp in sends:
            cp.wait_send()
        for cp in mine:
            cp.wait()

    return pl.pallas_call(
        body, name=name, in_specs=[ANY] * n, out_specs=[ANY] * n,
        out_shape=[jax.ShapeDtypeStruct((NCHIP,) + tuple(shp), b.dtype) for shp, b in zip(shapes, bufs)],
        scratch_shapes=[pltpu.SemaphoreType.DMA((3 * n,)), pltpu.SemaphoreType.DMA((3 * n,)),
                        pltpu.SemaphoreType.DMA((n,))],
    )(*bufs)


HBM_SPEC = pl.BlockSpec(memory_space=pltpu.HBM)
SEM_SPEC = pl.BlockSpec(memory_space=pltpu.SEMAPHORE)
DATAFLOW = pltpu.SideEffectType.DATAFLOW_SIDE_EFFECTING


def _exchange_copy(ins, lands, send_sems, recv_sems, scatter, a, j, px, py, c, dst_slot):
    src = ins[a].at[2 * px + py] if scatter else ins[a]
    return pltpu.make_async_remote_copy(src_ref=src, dst_ref=lands[a].at[dst_slot],
                                        send_sem=send_sems.at[3 * a + j], recv_sem=recv_sems.at[3 * a + j],
                                        device_id=(px, py, c), device_id_type=MESH)


def _own_copy(ins, lands, local_sems, scatter, a, me):
    return pltpu.make_async_copy(ins[a].at[me] if scatter else ins[a], lands[a].at[me], local_sems.at[a])


def exchange_start(bufs, scatter, name):
    n = len(bufs)
    lands = [lax.empty((NCHIP,) + tuple(b.shape[1:] if scatter else b.shape), b.dtype) for b in bufs]

    def body(*refs):
        ins, lnd, send_sems, recv_sems, local_sems, token = (refs[:n], refs[n:2 * n], refs[2 * n], refs[2 * n + 1],
                                                             refs[2 * n + 2], refs[-1])
        x, y, c = _mesh_pos()
        me = 2 * x + y
        for a in range(n):
            _own_copy(ins, lnd, local_sems, scatter, a, me).start()
            for j, (px, py) in enumerate([(1 - x, y), (x, 1 - y), (1 - x, 1 - y)]):
                _exchange_copy(ins, lnd, send_sems, recv_sems, scatter, a, j, px, py, c, me).start()
        token[...] = jnp.zeros_like(token)

    thru = [pltpu.HBM(b.shape, b.dtype) for b in list(bufs) + lands]
    out = pl.pallas_call(
        body, name=name, in_specs=[HBM_SPEC] * (2 * n),
        out_specs=[SEM_SPEC] * 3 + [HBM_SPEC] * (2 * n) + [pl.BlockSpec(memory_space=pltpu.VMEM)],
        out_shape=[pltpu.SemaphoreType.DMA((3 * n,)), pltpu.SemaphoreType.DMA((3 * n,)), pltpu.SemaphoreType.DMA((n,))]
        + thru + [jax.ShapeDtypeStruct((SLAB, LANE), F32)],
        input_output_aliases={k: 3 + k for k in range(2 * n)},
        compiler_params=pltpu.CompilerParams(has_side_effects=DATAFLOW),
    )(*[pltpu.with_memory_space_constraint(b, pltpu.HBM) for b in list(bufs) + lands])
    return tuple(out[:3]), out[3:3 + n], out[3 + n:3 + 2 * n], out[-1][0, 0]


def exchange_wait(started, after, scatter, name):
    sems, bufs, lands, _ = started
    n = len(bufs)

    def body(*refs):
        ins, lnd, ssem, rsem, lsem = refs[:n], refs[n:2 * n], refs[2 * n], refs[2 * n + 1], refs[2 * n + 2]
        x, y, c = _mesh_pos()
        for a in range(n):
            _own_copy(ins, lnd, lsem, scatter, a, 2 * x + y).wait()
            for j, (px, py) in enumerate([(1 - x, y), (x, 1 - y), (1 - x, 1 - y)]):
                cp = _exchange_copy(ins, lnd, ssem, rsem, scatter, a, j, px, py, c, 2 * px + py)
                cp.wait_send()
                cp.wait_recv()

    thru = [pltpu.HBM(b.shape, b.dtype) for b in list(bufs) + list(lands)]
    out = pl.pallas_call(
        body, name=name, in_specs=[HBM_SPEC] * (2 * n) + [SEM_SPEC] * 3 + [ANY],
        out_specs=[HBM_SPEC] * (2 * n), out_shape=thru,
        input_output_aliases={k: k for k in range(2 * n)},
        compiler_params=pltpu.CompilerParams(has_side_effects=DATAFLOW),
    )(*bufs, *lands, *sems, after)
    return out[n:]


def sibling_swap(bufs, name):
    n = len(bufs)

    def body(*refs):
        ins, outs, send_sems, recv_sems = refs[:n], refs[n:2 * n], refs[2 * n], refs[2 * n + 1]
        x, y, c = _mesh_pos()
        cps = [pltpu.make_async_remote_copy(src_ref=ins[k], dst_ref=outs[k], send_sem=send_sems.at[k],
                                            recv_sem=recv_sems.at[k], device_id=(x, y, 1 - c), device_id_type=MESH)
               for k in range(n)]
        for cp in cps:
            cp.start()
        for cp in cps:
            cp.wait()

    return pl.pallas_call(
        body, name=name, in_specs=[ANY] * n, out_specs=[ANY] * n,
        out_shape=[jax.ShapeDtypeStruct(b.shape, b.dtype) for b in bufs],
        scratch_shapes=[pltpu.SemaphoreType.DMA((n,)), pltpu.SemaphoreType.DMA((n,))],
    )(*bufs)


def _swap_copy(ins, lands, send_sems, recv_sems, a):
    x, y, c = _mesh_pos()
    return pltpu.make_async_remote_copy(src_ref=ins[a], dst_ref=lands[a], send_sem=send_sems.at[a],
                                        recv_sem=recv_sems.at[a], device_id=(x, y, 1 - c), device_id_type=MESH)


def swap_start(bufs, name):
    n = len(bufs)
    lands = [lax.empty(b.shape, b.dtype) for b in bufs]

    def body(*refs):
        for a in range(n):
            _swap_copy(refs[:n], refs[n:2 * n], refs[2 * n], refs[2 * n + 1], a).start()
        refs[-1][...] = jnp.zeros_like(refs[-1])

    thru = [pltpu.HBM(b.shape, b.dtype) for b in list(bufs) + lands]
    out = pl.pallas_call(
        body, name=name, in_specs=[HBM_SPEC] * (2 * n),
        out_specs=[SEM_SPEC] * 2 + [HBM_SPEC] * (2 * n) + [pl.BlockSpec(memory_space=pltpu.VMEM)],
        out_shape=[pltpu.SemaphoreType.DMA((n,)), pltpu.SemaphoreType.DMA((n,))] + thru
        + [jax.ShapeDtypeStruct((SLAB, LANE), F32)],
        input_output_aliases={k: 2 + k for k in range(2 * n)},
        compiler_params=pltpu.CompilerParams(has_side_effects=DATAFLOW),
    )(*[pltpu.with_memory_space_constraint(b, pltpu.HBM) for b in list(bufs) + lands])
    return tuple(out[:2]), out[2:2 + n], out[2 + n:2 + 2 * n], out[-1][0, 0]


def swap_wait(started, after, name):
    sems, bufs, lands, _ = started
    n = len(bufs)

    def body(*refs):
        for a in range(n):
            _swap_copy(refs[:n], refs[n:2 * n], refs[2 * n], refs[2 * n + 1], a).wait()

    thru = [pltpu.HBM(b.shape, b.dtype) for b in list(bufs) + list(lands)]
    out = pl.pallas_call(
        body, name=name, in_specs=[HBM_SPEC] * (2 * n) + [SEM_SPEC] * 2 + [ANY],
        out_specs=[HBM_SPEC] * (2 * n), out_shape=thru,
        input_output_aliases={k: k for k in range(2 * n)},
        compiler_params=pltpu.CompilerParams(has_side_effects=DATAFLOW),
    )(*bufs, *lands, *sems, after)
    return out[n:]


EW_VMEM_BUDGET = 20 * 1024 * 1024


def _ew_rows(rows, w, bytes_per_elem):
    wpad = -(-w // LANE) * LANE
    for t in (1024, 512, 256, 128, 64, 32, 16, 8):
        if rows % t == 0 and 2 * t * wpad * bytes_per_elem <= EW_VMEM_BUDGET:
            return t
    return rows


def sum_slots(buf, *, name):
    _, rows, w = buf.shape
    tm = _ew_rows(rows, w, NCHIP * buf.dtype.itemsize + 4)

    def body(b_ref, o_ref):
        acc = b_ref[0].astype(F32)
        for kk in range(1, NCHIP):
            acc = acc + b_ref[kk].astype(F32)
        o_ref[...] = acc

    return pl.pallas_call(body, name=name, grid=(rows // tm,),
                          in_specs=[pl.BlockSpec((NCHIP, tm, w), lambda i: (0, i, 0))],
                          out_specs=pl.BlockSpec((tm, w), lambda i: (i, 0)),
                          out_shape=jax.ShapeDtypeStruct((rows, w), F32),
                          compiler_params=_cparams("parallel"))(buf)


def adamw(p_mine, p_other, w, m, v, *, name):
    rows, wd = w.shape
    tm = _ew_rows(rows, wd, 9 * 4)
    c1 = 1.0 - ADAM_B1 ** ADAM_STEP
    c2 = 1.0 - ADAM_B2 ** ADAM_STEP

    def body(a_ref, b_ref, w_ref, m_ref, v_ref, g_ref, d_ref, nm_ref, nv_ref):
        g = a_ref[...] + b_ref[...]
        nm = ADAM_B1 * m_ref[...] + (1.0 - ADAM_B1) * g
        nv = ADAM_B2 * v_ref[...] + (1.0 - ADAM_B2) * (g * g)
        g_ref[...] = g
        nm_ref[...] = nm
        nv_ref[...] = nv
        d_ref[...] = -ADAM_LR * ((nm / c1) / (jnp.sqrt(nv / c2) + ADAM_EPS) + ADAM_WD * w_ref[...])

    row = pl.BlockSpec((tm, wd), lambda i: (i, 0))
    return pl.pallas_call(body, name=name, grid=(rows // tm,), in_specs=[row] * 5, out_specs=[row] * 4,
                          out_shape=[jax.ShapeDtypeStruct((rows, wd), F32)] * 4,
                          compiler_params=_cparams("parallel"))(p_mine, p_other, w, m, v)


def _rows2d(a, lead=0):
    tail = a.shape[lead:]
    n = int(np.prod(tail))
    if tail[-1] < LANE // 2 and n % (8 * LANE) == 0:
        return a.reshape(a.shape[:lead] + (n // (8 * LANE), 8 * LANE))
    return a.reshape(a.shape[:lead] + (-1, tail[-1]))


BIG = [("mix_w_in", 2), ("w_uq", 2), ("w_ukv", 2), ("mix_w_out", 1), ("ssm_w_in", 1), ("w_glu", 2),
       ("ffn_w_up", 2), ("ffn_w_down", 1)]
SMALL = [("sconv_w", 2), ("ssm_norm", 1), ("d_skip", 1), ("ffn_conv_w", 2)]
REPL = ["attn_norm", "cq_norm", "ckv_norm", "q_gain", "k_gain", "lambda_re", "lambda_im", "log_step",
        "b_re", "b_im", "c_re", "c_im", "ffn_norm"]
ORDER = ["attn_norm", "mix_w_in", "cq_norm", "ckv_norm", "w_uq", "w_ukv", "q_gain", "k_gain", "sconv_w", "mix_w_out",
         "ssm_norm", "ssm_w_in", "lambda_re", "lambda_im", "log_step", "b_re", "b_im", "c_re", "c_im", "d_skip",
         "w_glu", "ffn_norm", "ffn_w_up", "ffn_conv_w", "ffn_w_down"]


def _join(g, axis):
    return jnp.concatenate([g[k] for k in range(NCHIP)], axis=axis)


def _split(full, axis, parts=NCHIP):
    return jnp.stack(jnp.split(full, parts, axis=axis))


def _discretize(lr, li, ls, b_re, b_im):
    dt = jnp.exp(ls)[:, None]
    mag = jnp.exp(lr * dt)
    ar, ai = mag * jnp.cos(li * dt), mag * jnp.sin(li * dt)
    nr, ni = ar - 1.0, ai
    den = lr * lr + li * li
    zr, zi = (nr * lr + ni * li) / den, (ni * lr - nr * li) / den
    bbar_r = zr[..., None] * b_re - zi[..., None] * b_im
    bbar_i = zr[..., None] * b_im + zi[..., None] * b_re
    return ar, ai, bbar_r, bbar_i


def _b_blockdiag(bbar):
    gl = G // NJ
    bb = bbar.reshape(NJ, gl, P, GC).transpose(0, 1, 3, 2)
    return jnp.einsum("jgcp,gh->jgchp", bb, jnp.eye(gl, dtype=bbar.dtype)).reshape(NJ, gl * GC, gl * P)


def _b_blockdiag_t(dbd):
    gl = G // NJ
    d = jnp.einsum("jgchp,gh->jgcp", dbd.reshape(NJ, gl, GC, gl, P), jnp.eye(gl, dtype=dbd.dtype))
    return d.transpose(0, 1, 3, 2).reshape(G, P, GC)


def _c_blockdiag(cmat):
    gl = G // NJ
    cc = cmat.reshape(NJ, gl, GC, P).transpose(0, 1, 3, 2)
    return jnp.einsum("jgpc,gh->jgphc", cc, jnp.eye(gl, dtype=cmat.dtype)).reshape(NJ, gl * P, gl * GC)


def _c_blockdiag_t(dbd):
    gl = G // NJ
    d = jnp.einsum("jgphc,gh->jgpc", dbd.reshape(NJ, gl, P, gl, GC), jnp.eye(gl, dtype=dbd.dtype))
    return d.transpose(0, 1, 3, 2).reshape(G, GC, P)


def _pad_heads_cols(w, width):
    r = w.shape[0]
    return jnp.pad(w.reshape(r, HEADS, width), ((0, 0), (0, 0), (0, HP - width))).reshape(r, HEADS * HP)


def _unpad_heads_cols(w, width):
    r = w.shape[0]
    return w.reshape(r, HEADS, HP)[:, :, :width].reshape(r, HEADS * width)


W_IN_SPLIT = (QR + KVR, QR + KVR + ROPE)


def _w_in_layout(w):
    a, b = W_IN_SPLIT
    kr = jnp.pad(w[:, a:b], ((0, 0), (NOPE, HP - QK)))
    return jnp.concatenate([w[:, :a], w[:, b:], kr], axis=1)


def _ffn_fwd(x, l, wt, name):
    h = rms_fwd(x, wt["ffn_norm"][l][None], name=f"{name}_norm")
    up = mm(h, wt["ffn_w_up"][l], name=f"{name}_up")
    act = ffnact_fwd(up, wt["ffn_conv_w"][l], name=f"{name}_act")
    out = mm(act, wt["ffn_w_down"][l], add=x, name=f"{name}_down")
    return out, (x, h, up, act)


def _ffn_bwd(dout, douth, saved, l, wt, name):
    x, h, up, act = saved
    g = {}
    dact = mm(douth, wt["ffn_w_down"][l], tb=True, name=f"{name}_ddown")
    g["ffn_w_down"] = mm(act, douth, ta=True, out_dtype=BF16, name=f"{name}_dwdown")
    dup, g["ffn_conv_w"] = ffnact_bwd(up, wt["ffn_conv_w"][l], dact, name=f"{name}_dact")
    g["ffn_w_up"] = tuple(mm(h, d, ta=True, out_dtype=BF16, name=f"{name}_dwup{kk}") for kk, d in enumerate(dup))
    dh = mm_nt_segments(dup, wt["ffn_w_up"][l], name=f"{name}_dup")
    dx, dxh, dg = rms_bwd(x, wt["ffn_norm"][l][None], dh, add=dout, twin=True, name=f"{name}_dnorm")
    g["ffn_norm"] = dg[0]
    return dx, dxh, g


def _even_fwd(x, i, wt, tabs, name):
    h = rms_fwd(x, wt["attn_norm"][i][None], name=f"{name}_norm")
    proj = mm(h, wt["w_in2"][i], name=f"{name}_in")
    cqn = rms_fwd(proj, wt["cq_norm"][i][None], col=0, name=f"{name}_cqnorm")
    ckvn = rms_fwd(proj, wt["ckv_norm"][i][None], col=1, name=f"{name}_ckvnorm")
    qraw = mm(cqn, wt["w_uq_p"][i], name=f"{name}_uq")
    kv = mm(ckvn, wt["w_ukv_p"][i], name=f"{name}_ukv")
    q, k, v = qkprep_fwd(qraw, kv, proj, wt["q_gain_p"][i], wt["k_gain_p"][i], tabs, name=f"{name}_qkprep")
    o, oh, lset = attn_fwd(q, k, v, name=f"{name}_attn")
    conv = sconv_fwd(proj, wt["sconv_w"][i], name=f"{name}_sconv")
    t = mm(oh, wt["w_out_a"][i], add=x, name=f"{name}_outa")
    out = mm(conv, wt["w_out_c"][i], add=t, name=f"{name}_outc")
    return out, (x, h, proj, cqn, ckvn, qraw, kv, q, k, v, o, oh, lset, conv)


def _even_bwd(dout, douth, saved, i, wt, tabs, name):
    x, h, proj, cqn, ckvn, qraw, kv, q, k, v, o, oh, lset, conv = saved
    g = {}
    do = mm(douth, wt["w_out_a"][i], tb=True, name=f"{name}_douta")
    dconv = mm(douth, wt["w_out_c"][i], tb=True, name=f"{name}_doutc")
    g["w_out_a"] = mm(oh, douth, ta=True, out_dtype=BF16, name=f"{name}_dwouta")
    g["w_out_c"] = mm(conv, douth, ta=True, out_dtype=BF16, name=f"{name}_dwoutc")
    dgates, g["sconv_w"] = sconv_bwd(proj, wt["sconv_w"][i], dconv, name=f"{name}_dsconv")
    doh, deltat = attn_delta(o, do, name=f"{name}_dattn_delta")
    dq, dk, dv = attn_bwd(q, k, v, doh, lset, deltat, name=f"{name}_dattn")
    dqraw, dkraw, dkrope, dqg, dkg = qkprep_bwd(qraw, kv, proj, wt["q_gain_p"][i], wt["k_gain_p"][i], tabs, dq, dk,
                                                name=f"{name}_dqkprep")
    g["q_gain"], g["k_gain"] = dqg[0, :QK], dkg[0, :QK]
    dcqn = mm(dqraw, wt["w_uq_p"][i], tb=True, name=f"{name}_duq")
    g["w_uq_p"] = mm(cqn, dqraw, ta=True, out_dtype=BF16, name=f"{name}_dwuq")
    dkv = (dkraw, dv)
    dckvn = mm_nt_segments(dkv, wt["w_ukv_p"][i], name=f"{name}_dukv")
    g["w_ukv_p"] = tuple(mm(ckvn, d, ta=True, out_dtype=BF16, name=f"{name}_dwukv{kk}") for kk, d in enumerate(dkv))
    dcq, dgq = rms_bwd(proj, wt["cq_norm"][i][None], dcqn, col=0, out_dtype=BF16, name=f"{name}_dcqnorm")
    dckv, dgkv = rms_bwd(proj, wt["ckv_norm"][i][None], dckvn, col=1, out_dtype=BF16, name=f"{name}_dckvnorm")
    g["cq_norm"], g["ckv_norm"] = dgq[0], dgkv[0]
    dproj = (dcq, dckv, *dgates, dkrope)
    g["w_in2"] = tuple(mm(h, d, ta=True, out_dtype=BF16, name=f"{name}_dwin{kk}") for kk, d in enumerate(dproj))
    dh = mm_nt_segments(dproj, wt["w_in2"][i], name=f"{name}_din")
    dx, dxh, dg = rms_bwd(x, wt["attn_norm"][i][None], dh, add=dout, twin=True, name=f"{name}_dnorm")
    g["attn_norm"] = dg[0]
    return dx, dxh, g


def _odd_fwd(x, i, wt, name):
    h = rms_fwd(x, wt["ssm_norm"][i][None], name=f"{name}_norm")
    u = mm(h, wt["ssm_w_in"][i], name=f"{name}_in")
    y, st_r, st_i = s5_fwd(u, wt["bbd_r"][i], wt["bbd_i"][i], wt["cbd_r"][i], wt["cbd_i"][i], wt["tab_f"][i],
                           name=f"{name}_scan")
    gl = s5post_fwd(y, u, wt["d_skip"][i][None], name=f"{name}_gelu")
    glu = mm(gl, wt["w_glu"][i], name=f"{name}_glu")
    out = glu_fwd(glu, x, name=f"{name}_gate")
    return out, (x, h, u, y, st_r, st_i, gl, glu)


def _odd_bwd(dout, douth, saved, i, wt, name):
    x, h, u, y, st_r, st_i, gl, glu = saved
    g = {}
    dglu = glu_bwd(glu, dout, name=f"{name}_dgate")
    g["w_glu"] = mm(gl, dglu, ta=True, out_dtype=BF16, name=f"{name}_dwglu")
    dgl = mm(dglu, wt["w_glu"][i], tb=True, name=f"{name}_dglu")
    dz, dd = s5post_bwd(y, u, wt["d_skip"][i][None], dgl, name=f"{name}_dgelu")
    g["d_skip"] = dd[0]
    du, g["bbd_r"], g["bbd_i"], g["cbd_r"], g["cbd_i"], g["a"] = s5_bwd(
        u, dz, wt["d_skip"][i][None], st_r, st_i, wt["bbd_r"][i], wt["bbd_i"][i], wt["cbd_r"][i], wt["cbd_i"][i],
        wt["tab_r"][i], name=f"{name}_dscan")
    g["ssm_w_in"] = mm(h, du, ta=True, out_dtype=BF16, name=f"{name}_dwin")
    dh = mm(du, wt["ssm_w_in"][i], tb=True, name=f"{name}_din")
    dx, dxh, dg = rms_bwd(x, wt["ssm_norm"][i][None], dh, add=dout, twin=True, name=f"{name}_dnorm")
    g["ssm_norm"] = dg[0]
    return dx, dxh, g


MATMUL_WEIGHTS = {"even": ("mix_w_in", "w_uq", "w_ukv", "mix_w_out"), "odd": ("ssm_w_in", "w_glu"),
                  "ffn": ("ffn_w_up", "ffn_w_down")}
ODD_SMALL = ("lambda_re", "lambda_im", "log_step", "b_re", "b_im", "c_re", "c_im", "ssm_norm", "d_skip")


def _even_layouts(fw, wt, i):
    wt["w_in2"][i] = _w_in_layout(fw["mix_w_in"])
    wt["w_uq_p"][i] = _pad_heads_cols(fw["w_uq"], QK)
    ukv = fw["w_ukv"].reshape(KVR, HEADS, NOPE + VD)
    wt["w_ukv_p"][i] = jnp.concatenate(
        [_pad_heads_cols(ukv[:, :, :NOPE].reshape(KVR, HEADS * NOPE), NOPE),
         _pad_heads_cols(ukv[:, :, NOPE:].reshape(KVR, HEADS * VD), VD)], axis=1)
    wt["w_out_a"][i] = _pad_heads_cols(fw["mix_w_out"][:HEADS * VD].T, VD).T
    wt["w_out_c"][i] = fw["mix_w_out"][HEADS * VD:]


def _even_layouts_t(g):
    dk_, dv_ = g["w_ukv_p"]
    dcq, dckv, dgb, dgc, dci, dkr = g["w_in2"]
    return {"mix_w_in": jnp.concatenate([dcq, dckv, dkr[:, NOPE:QK], dgb, dgc, dci], axis=1),
            "w_uq": _unpad_heads_cols(g["w_uq_p"], QK),
            "w_ukv": jnp.concatenate([dk_.reshape(KVR, HEADS, HP)[:, :, :NOPE], dv_.reshape(KVR, HEADS, HP)[:, :, :VD]],
                                     axis=2).reshape(KVR, HEADS * (NOPE + VD)),
            "mix_w_out": jnp.concatenate([_unpad_heads_cols(g["w_out_a"].T, VD).T, g["w_out_c"]], axis=0)}


def _local_step(x, target, full, getw, putg):
    s = x.shape[0]
    n_even = (DEPTH + 1) // 2
    n_odd = DEPTH // 2
    tabs = _rope_tables(s)
    wt = dict(full)
    for key in ("w_in2", "w_uq_p", "w_ukv_p", "w_out_a", "w_out_c") + sum(MATMUL_WEIGHTS.values(), ()):
        wt[key] = {}
    wt["q_gain_p"] = jnp.pad(full["q_gain"], ((0, 0), (0, HP - QK)))[:, None, :]
    wt["k_gain_p"] = jnp.pad(full["k_gain"], ((0, 0), (0, HP - QK)))[:, None, :]

    disc_vjp = []
    for key in ("bbd_r", "bbd_i", "cbd_r", "cbd_i", "tab_f", "tab_r"):
        wt[key] = []
    for i in range(n_odd):
        (ar, ai, bbr, bbi), vjp = jax.vjp(_discretize, full["lambda_re"][i], full["lambda_im"][i], full["log_step"][i],
                                          full["b_re"][i], full["b_im"][i])
        disc_vjp.append(vjp)
        tab = _scan_tables(ar.reshape(-1), ai.reshape(-1), _long_rows(s) // SLAB)
        wt["tab_f"].append(tab)
        wt["tab_r"].append(tab)
        wt["bbd_r"].append(_b_blockdiag(bbr).astype(BF16))
        wt["bbd_i"].append(_b_blockdiag(bbi).astype(BF16))
        wt["cbd_r"].append(_c_blockdiag(full["c_re"][i]).astype(BF16))
        wt["cbd_i"].append(_c_blockdiag(full["c_im"][i]).astype(BF16))

    saved = []
    for layer in range(DEPTH):
        i = layer // 2
        if layer % 2 == 0:
            fw, tok = getw("even", i, x)
            wt["attn_norm"] = full["attn_norm"] + tok
            _even_layouts(fw, wt, i)
            x, sm = _even_fwd(x, i, wt, tabs, f"l{layer}_mla")
        else:
            fw, tok = getw("odd", i, x)
            wt["ssm_norm"] = full["ssm_norm"] + tok
            for n, a in fw.items():
                wt[n][i] = a
            x, sm = _odd_fwd(x, i, wt, f"l{layer}_s5")
        fw, tok = getw("ffn", layer, x)
        wt["ffn_norm"] = full["ffn_norm"] + tok
        for n, a in fw.items():
            wt[n][layer] = a
        x, sf = _ffn_fwd(x, layer, wt, f"l{layer}_ffn")
        saved.append((sm, sf))
    dx, dxh, lslab = loss_head(x, target, name="loss_head")

    own = [n for n in ORDER if n not in sum(MATMUL_WEIGHTS.values(), ())]
    grads = {n: [None] * (DEPTH if n.startswith("ffn") else n_even) for n in own}
    tok = 0.0
    for layer in reversed(range(DEPTH)):
        i = layer // 2
        sm, sf = saved[layer]
        wt["ffn_conv_w"] = full["ffn_conv_w"] + tok
        dx, dxh, g = _ffn_bwd(dx, dxh, sf, layer, wt, f"l{layer}_ffn")
        tok = putg("ffn", layer, {n: g[n] for n in MATMUL_WEIGHTS["ffn"]})
        for n in ("ffn_norm", "ffn_conv_w"):
            grads[n][layer] = g[n]
        if layer % 2 == 0:
            wt["sconv_w"] = full["sconv_w"] + tok
            dx, dxh, g = _even_bwd(dx, dxh, sm, i, wt, tabs, f"l{layer}_mla")
            tok = putg("even", i, _even_layouts_t(g))
            for n in ("attn_norm", "cq_norm", "ckv_norm", "q_gain", "k_gain", "sconv_w"):
                grads[n][i] = g[n]
        else:
            wt["d_skip"] = full["d_skip"] + tok
            dx, dxh, g = _odd_bwd(dx, dxh, sm, i, wt, f"l{layer}_s5")
            tok = putg("odd", i, {n: g[n] for n in MATMUL_WEIGHTS["odd"]})
            dlr, dli, dls, dbr, dbi = disc_vjp[i]((g["a"][0].reshape(G, P), g["a"][1].reshape(G, P),
                                                    _b_blockdiag_t(g["bbd_r"]), _b_blockdiag_t(g["bbd_i"])))
            grads["lambda_re"][i], grads["lambda_im"][i], grads["log_step"][i] = dlr, dli, dls
            grads["b_re"][i], grads["b_im"][i] = dbr, dbi
            grads["c_re"][i], grads["c_im"][i] = _c_blockdiag_t(g["cbd_r"]), _c_blockdiag_t(g["cbd_i"])
            for n in ("ssm_norm", "d_skip"):
                grads[n][i] = g[n]
            if i == 0:
                tok = tok + putg("odd_small", 0, {n: jnp.stack(grads[n]) for n in ODD_SMALL})
    grads = {n: jnp.stack(v) for n, v in grads.items()}
    return jnp.sum(lslab), dx, grads


def kernel(x, attn_norm, mix_w_in, cq_norm, ckv_norm, w_uq, w_ukv, q_gain, k_gain, sconv_w, mix_w_out, ssm_norm, ssm_w_in, lambda_re, lambda_im, log_step, b_re, b_im, c_re, c_im, d_skip, w_glu, ffn_norm, ffn_w_up, ffn_conv_w, ffn_w_down, loss_target, m_attn_norm, m_mix_w_in, m_cq_norm, m_ckv_norm, m_w_uq, m_w_ukv, m_q_gain, m_k_gain, m_sconv_w, m_mix_w_out, m_ssm_norm, m_ssm_w_in, m_lambda_re, m_lambda_im, m_log_step, m_b_re, m_b_im, m_c_re, m_c_im, m_d_skip, m_w_glu, m_ffn_norm, m_ffn_w_up, m_ffn_conv_w, m_ffn_w_down, v_attn_norm, v_mix_w_in, v_cq_norm, v_ckv_norm, v_w_uq, v_w_ukv, v_q_gain, v_k_gain, v_sconv_w, v_mix_w_out, v_ssm_norm, v_ssm_w_in, v_lambda_re, v_lambda_im, v_log_step, v_b_re, v_b_im, v_c_re, v_c_im, v_d_skip, v_w_glu, v_ffn_norm, v_ffn_w_up, v_ffn_conv_w, v_ffn_w_down):
    args = dict(locals())
    w = {n: args[n] for n in ORDER}
    m = {n: args["m_" + n] for n in ORDER}
    v = {n: args["v_" + n] for n in ORDER}
    me = 2 * lax.axis_index("x") + lax.axis_index("y")

    axis = dict(BIG)

    gs = _chip_exchange([w[n] for n, _ in SMALL], False, "gather_w_f32")
    full = {n: w[n] for n in REPL}
    for (n, ax), g in zip(SMALL, gs):
        full[n] = _join(g, ax)
    parts = [(("even", "odd")[layer % 2], layer // 2) for layer in range(DEPTH)]
    parts = [p for layer, mixer in enumerate(parts) for p in (mixer, ("ffn", layer))]
    gathers = {}

    def start_gather(kind, idx, zero):
        shards = [(w[n][idx] + zero).astype(BF16) for n in MATMUL_WEIGHTS[kind]]
        gathers[kind, idx] = exchange_start(shards, False, f"gather_start_{kind}{idx}")

    start_gather(*parts[0], 0.0 * gs[0][(0,) * gs[0].ndim])

    def getw(kind, idx, after):
        got = exchange_wait(gathers[kind, idx], after, False, f"gather_wait_{kind}{idx}")
        nxt = parts.index((kind, idx)) + 1
        tok = 0.0
        if nxt < len(parts):
            start_gather(*parts[nxt], 0.0 * got[0][(0,) * got[0].ndim].astype(F32))
            tok = gathers[parts[nxt]][3]
        return {n: _join(g, axis[n] - 1) for n, g in zip(MATMUL_WEIGHTS[kind], got)}, tok

    scatters, early = [], []

    def putg(kind, idx, g):
        if kind == "odd_small":
            arrs = [_rows2d(g[n]) for n in ODD_SMALL]
            early.append(exchange_start(arrs, False, "gather_start_g_odd"))
            return early[0][3]
        blocks = [jnp.concatenate([_split(part, axis[n] - 1, NCHIP // len(g[n])) for part in g[n]])
                  if isinstance(g[n], tuple) else _split(g[n], axis[n] - 1) for n in MATMUL_WEIGHTS[kind]]
        tok = 0.0
        if scatters:
            tok = reduce_part(*scatters.pop(), blocks[0])
        scatters.append((kind, idx, exchange_start(blocks, True, f"scatter_start_{kind}{idx}")))
        return tok + scatters[-1][2][3]

    swaps = []

    def reduce_part(kind, idx, started, after):
        got = exchange_wait(started, after, True, f"scatter_wait_{kind}{idx}")
        sums = [sum_slots(_rows2d(sl, 1), name=f"sum_{n}{idx}") for n, sl in zip(MATMUL_WEIGHTS[kind], got)]
        swaps.append((kind, idx, sums, swap_start(sums, f"swap_start_{kind}{idx}")))
        return swaps[-1][3][3]

    sq, dx, grads = _local_step(x[0], loss_target[0], full, getw, putg)
    loss = lax.psum(0.5 * sq / D_MODEL, ("x", "y", "c"))

    late_names = [n for n in REPL + [n for n, _ in SMALL] if n not in ODD_SMALL]
    rep_names = list(ODD_SMALL) + late_names
    names = [n for n, _ in BIG] + rep_names
    late = [_rows2d(grads[n]) for n in late_names]
    late_started = exchange_start(late, False, "gather_start_g_f32")
    reduce_part(*scatters.pop(), dx)
    slots = (exchange_wait(early[0], dx, False, "gather_wait_g_odd")
             + exchange_wait(late_started, swaps[-1][2][0], False, "gather_wait_g_f32"))
    small_mine = [sum_slots(_rows2d(sl, 1), name=f"sum_{n}") for n, sl in zip(rep_names, slots)]
    small_other = list(sibling_swap(small_mine, "swap_g_small"))
    summed, swapped = {}, {}
    for kind, idx, sums, started in swaps:
        for n, p, q in zip(MATMUL_WEIGHTS[kind], sums, swap_wait(started, small_other[0], f"swap_wait_{kind}{idx}")):
            summed[n, idx], swapped[n, idx] = p, q
    mine = [jnp.concatenate([summed[n, idx] for idx in range(w[n].shape[0])], axis=0) for n, _ in BIG] + small_mine
    other = [jnp.concatenate([swapped[n, idx] for idx in range(w[n].shape[0])], axis=0) for n, _ in BIG] + small_other

    def local(n, p):
        ax = dict(SMALL).get(n)
        if ax is None:
            return p
        part = lax.dynamic_index_in_dim(_split(p.reshape(grads[n].shape), ax), me, 0, keepdims=False)
        return _rows2d(part)

    outs = {}
    for n, p, q in zip(names, mine, other):
        res = adamw(local(n, p), local(n, q), _rows2d(w[n]), _rows2d(m[n]), _rows2d(v[n]), name=f"adamw_{n}")
        outs[n] = [r.reshape(w[n].shape) for r in res]
    return (loss, dx[None], *[outs[n][0] for n in ORDER], *[outs[n][1] for n in ORDER],
            *[outs[n][2] for n in ORDER], *[outs[n][3] for n in ORDER])
```

```python
import functools
import math

import numpy as np
import jax
import jax.numpy as jnp
from jax import lax
from jax.experimental import pallas as pl
from jax.experimental.pallas import tpu as pltpu

F32, BF16 = jnp.float32, jnp.bfloat16

D_MODEL = 1024
DEPTH = 4
HEADS = 8
NOPE, ROPE, QK, VD = 64, 32, 96, 64
HP = 128
QR, KVR = 256, 256
CONVC = 512
FFN_H = 2816
G, P, GC = 64, 64, 16
NST = G * P
SLAB = 8
LANE = 128
EPS = 1e-6
ROPE_THETA = 10000.0
ADAM_LR, ADAM_B1, ADAM_B2, ADAM_EPS, ADAM_WD, ADAM_STEP = 0.001, 0.9, 0.999, 1e-08, 0.01, 10
VMEM_LIMIT = 48 * 1024 * 1024
MESH = pl.DeviceIdType.MESH
ANY = pl.BlockSpec(memory_space=pl.ANY)


def _cparams(*sem):
    return pltpu.CompilerParams(dimension_semantics=sem, vmem_limit_bytes=VMEM_LIMIT)


def _pick(dim, prefs):
    for p in prefs:
        if dim % p == 0:
            return p
    return dim


def _rows(s):
    return _pick(s, (512, 256, 128, 64, 32, 16, 8))


def _long_rows(s):
    return _pick(s, (1024, 512, 256, 128, 64, 32, 16, 8))


MM_VMEM_BUDGET = 36 * 1024 * 1024
MM_MAX_TILE_ELEMS = 640 * 1024
HBM_BYTES_PER_US = 3.0e6
STEP_OVERHEAD_US = 0.35


def _lane_tiles(n):
    c = {t for t in range(LANE, min(n, 1536) + 1, LANE) if n % t == 0}
    if n <= 2304 or not c:
        c.add(n)
    return sorted(c, reverse=True)


def _mm_tiles(m, n, k, sa, sb, so):
    best = None
    for tm in [t for t in (1024, 512, 256) if m % t == 0] or [m]:
        for tn in _lane_tiles(n):
            if tm * tn > MM_MAX_TILE_ELEMS:
                continue
            if 2 * (tm * k * sa + k * tn * sb + tm * tn * so) + 4 * tm * tn > MM_VMEM_BUDGET:
                continue
            steps = (m // tm) * (n // tn)
            for inner_n in (True, False):
                moved = (m * k * sa + (m // tm) * k * n * sb) if inner_n else (k * n * sb + (n // tn) * m * k * sa)
                cost = (moved + m * n * so) / HBM_BYTES_PER_US + steps * STEP_OVERHEAD_US
                if best is None or cost < best[0]:
                    best = (cost, tm, tn, inner_n)
    return best[1:]


def mm(a, b, *, ta=False, tb=False, add=None, out_dtype=F32, name):
    if ta:
        kdim, m = a.shape
    else:
        m, kdim = a.shape
    n = b.shape[0] if tb else b.shape[1]
    so = jnp.dtype(out_dtype).itemsize + (0 if add is None else add.dtype.itemsize)
    tm, tn, inner_n = _mm_tiles(m, n, kdim, a.dtype.itemsize, b.dtype.itemsize, so)
    dn = (((0 if ta else 1,), (1 if tb else 0,)), ((), ()))

    def body(*refs):
        if add is None:
            a_ref, b_ref, o_ref = refs
        else:
            a_ref, b_ref, add_ref, o_ref = refs
        r = lax.dot_general(a_ref[...].astype(BF16), b_ref[...].astype(BF16), dn, preferred_element_type=F32)
        if add is not None:
            r = r + add_ref[...].astype(F32)
        o_ref[...] = r.astype(out_dtype)

    ij = (lambda g0, g1: (g0, g1)) if inner_n else (lambda g0, g1: (g1, g0))
    a_spec = (pl.BlockSpec((kdim, tm), lambda g0, g1: (0, ij(g0, g1)[0])) if ta
              else pl.BlockSpec((tm, kdim), lambda g0, g1: (ij(g0, g1)[0], 0)))
    b_spec = (pl.BlockSpec((tn, kdim), lambda g0, g1: (ij(g0, g1)[1], 0)) if tb
              else pl.BlockSpec((kdim, tn), lambda g0, g1: (0, ij(g0, g1)[1])))
    o_spec = pl.BlockSpec((tm, tn), lambda g0, g1: ij(g0, g1))
    ins, specs = [a, b], [a_spec, b_spec]
    if add is not None:
        ins.append(add)
        specs.append(o_spec)
    grid = (m // tm, n // tn) if inner_n else (n // tn, m // tm)
    return pl.pallas_call(
        body, name=name, grid=grid, in_specs=specs, out_specs=o_spec,
        out_shape=jax.ShapeDtypeStruct((m, n), out_dtype),
        compiler_params=_cparams("parallel", "parallel"))(*ins)


def mm_nt_segments(segs, b, *, name):
    m = segs[0].shape[0]
    n, kdim = b.shape
    widths = [sg.shape[1] for sg in segs]
    offs = [sum(widths[:i]) for i in range(len(segs))]
    assert sum(widths) == kdim and all(o % wd == 0 for o, wd in zip(offs, widths))
    sa = max(sg.dtype.itemsize for sg in segs)
    tm, tn, inner_n = _mm_tiles(m, n, kdim, sa, b.dtype.itemsize, 4)
    ns = len(segs)

    def body(*refs):
        o_ref = refs[-1]
        r = None
        for a_ref, b_ref in zip(refs[:ns], refs[ns:2 * ns]):
            d = lax.dot_general(a_ref[...].astype(BF16), b_ref[...].astype(BF16), NT, preferred_element_type=F32)
            r = d if r is None else r + d
        o_ref[...] = r

    ij = (lambda g0, g1: (g0, g1)) if inner_n else (lambda g0, g1: (g1, g0))
    a_specs = [pl.BlockSpec((tm, wd), lambda g0, g1: (ij(g0, g1)[0], 0)) for wd in widths]
    b_specs = [pl.BlockSpec((tn, wd), lambda g0, g1, blk=o // wd: (ij(g0, g1)[1], blk)) for o, wd in zip(offs, widths)]
    grid = (m // tm, n // tn) if inner_n else (n // tn, m // tm)
    return pl.pallas_call(
        body, name=name, grid=grid, in_specs=a_specs + b_specs,
        out_specs=pl.BlockSpec((tm, tn), lambda g0, g1: ij(g0, g1)),
        out_shape=jax.ShapeDtypeStruct((m, n), F32),
        compiler_params=_cparams("parallel", "parallel"))(*segs, *([b] * ns))


def rms_fwd(x, g, *, col=0, out_dtype=BF16, name):
    s = x.shape[0]
    d = g.shape[1]
    tm = _rows(s)

    def body(x_ref, g_ref, o_ref):
        xv = x_ref[...]
        r = lax.rsqrt(jnp.mean(xv * xv, axis=-1, keepdims=True) + EPS)
        o_ref[...] = (xv * r * g_ref[...]).astype(out_dtype)

    return pl.pallas_call(
        body, name=name, grid=(s // tm,),
        in_specs=[pl.BlockSpec((tm, d), lambda i: (i, col)), pl.BlockSpec((1, d), lambda i: (0, 0))],
        out_specs=pl.BlockSpec((tm, d), lambda i: (i, 0)),
        out_shape=jax.ShapeDtypeStruct((s, d), out_dtype),
        compiler_params=_cparams("parallel"))(x, g)


def rms_bwd(x, g, dy, *, col=0, add=None, out_dtype=F32, twin=False, name):
    s = x.shape[0]
    d = g.shape[1]
    tm = _rows(s)

    def body(*refs):
        refs = list(refs)
        dg_ref = refs.pop()
        dxh_ref = refs.pop() if twin else None
        dx_ref = refs.pop()
        add_ref = refs.pop() if add is not None else None
        x_ref, g_ref, dy_ref = refs

        @pl.when(pl.program_id(0) == 0)
        def _():
            dg_ref[...] = jnp.zeros_like(dg_ref)

        xv = x_ref[...]
        dyv = dy_ref[...].astype(F32)
        r = lax.rsqrt(jnp.mean(xv * xv, axis=-1, keepdims=True) + EPS)
        xh = xv * r
        dg_ref[...] += jnp.sum(dyv * xh, axis=0, keepdims=True)
        dxh = dyv * g_ref[...]
        dx = r * (dxh - xh * jnp.mean(dxh * xh, axis=-1, keepdims=True))
        if add is not None:
            dx = dx + add_ref[...]
        dx_ref[...] = dx.astype(out_dtype)
        if twin:
            dxh_ref[...] = dx.astype(BF16)

    row = pl.BlockSpec((tm, d), lambda i: (i, 0))
    vec = pl.BlockSpec((1, d), lambda i: (0, 0))
    ins = [x, g, dy]
    specs = [pl.BlockSpec((tm, d), lambda i: (i, col)), vec, row]
    if add is not None:
        ins.append(add)
        specs.append(row)
    dxs = [jax.ShapeDtypeStruct((s, d), out_dtype)] + ([jax.ShapeDtypeStruct((s, d), BF16)] if twin else [])
    return pl.pallas_call(
        body, name=name, grid=(s // tm,), in_specs=specs,
        out_specs=[row] * len(dxs) + [vec],
        out_shape=dxs + [jax.ShapeDtypeStruct((1, d), F32)],
        compiler_params=_cparams("arbitrary"))(*ins)


def _rope_tables(s):
    inv = 1.0 / (ROPE_THETA ** (jnp.arange(0, ROPE, 2, dtype=F32) / ROPE))
    ang = jnp.arange(s, dtype=F32)[:, None] * inv[None, :]
    cos, sin = jnp.cos(ang), jnp.sin(ang)
    z = lambda w: jnp.zeros((s, w), F32)
    c = jnp.concatenate([jnp.ones((s, NOPE), F32), cos, cos, z(HP - QK)], axis=1)
    s1 = jnp.concatenate([z(NOPE), -sin, z(HP - NOPE - ROPE // 2)], axis=1)
    s2 = jnp.concatenate([z(NOPE + ROPE // 2), sin, z(HP - QK)], axis=1)
    return c, s1, s2


def qkprep_fwd(qraw, kv, proj, qg, kg, tabs, *, name):
    s = qraw.shape[0]
    tm = _long_rows(s)
    kr_col = (proj.shape[1] - HP) // HP

    def body(q_ref, k_ref, v_ref, kr_ref, qg_ref, kg_ref, c_ref, s1_ref, s2_ref, qo_ref, ko_ref, vo_ref):
        c, s1, s2 = c_ref[...], s1_ref[...], s2_ref[...]

        def f(xv, gain):
            r = lax.rsqrt(jnp.sum(xv * xv, axis=-1, keepdims=True) * (1.0 / QK) + EPS)
            xn = xv * r * gain
            return xn * c + pltpu.roll(xn, HP - ROPE // 2, 1) * s1 + pltpu.roll(xn, ROPE // 2, 1) * s2

        qo_ref[...] = (f(q_ref[...], qg_ref[...]) * Q_SCALE).astype(BF16)
        ko_ref[...] = f(k_ref[...] + kr_ref[...], kg_ref[...]).astype(BF16)
        vv = v_ref[...]
        lane = lax.broadcasted_iota(jnp.int32, vv.shape, 1)
        vo_ref[...] = jnp.where(lane == VD, 1.0, vv).astype(BF16)

    head = pl.BlockSpec((tm, HP), lambda i, h: (i, h))
    tab = pl.BlockSpec((tm, HP), lambda i, h: (i, 0))
    gain = pl.BlockSpec((1, HP), lambda i, h: (0, 0))
    return pl.pallas_call(
        body, name=name, grid=(s // tm, HEADS),
        in_specs=[head, head, pl.BlockSpec((tm, HP), lambda i, h: (i, HEADS + h)),
                  pl.BlockSpec((tm, HP), lambda i, h: (i, kr_col)), gain, gain, tab, tab, tab],
        out_specs=[head, head, head],
        out_shape=[jax.ShapeDtypeStruct((s, HEADS * HP), BF16)] * 3,
        compiler_params=_cparams("parallel", "parallel"))(qraw, kv, kv, proj, qg, kg, *tabs)


def qkprep_bwd(qraw, kv, proj, qg, kg, tabs, dq, dk, *, name):
    s = qraw.shape[0]
    tm = _long_rows(s)
    kr_col = (proj.shape[1] - HP) // HP

    def body(q_ref, k_ref, kr_ref, qg_ref, kg_ref, c_ref, s1_ref, s2_ref, dq_ref, dk_ref,
             dqr_ref, dkr_ref, dkrope_ref, dqg_ref, dkg_ref):
        i, h = pl.program_id(0), pl.program_id(1)
        c, s1, s2 = c_ref[...], s1_ref[...], s2_ref[...]

        @pl.when((i == 0) & (h == 0))
        def _():
            dqg_ref[...] = jnp.zeros_like(dqg_ref)
            dkg_ref[...] = jnp.zeros_like(dkg_ref)

        @pl.when(h == 0)
        def _():
            dkrope_ref[...] = jnp.zeros_like(dkrope_ref)

        def f(xv, gain, dout):
            r = lax.rsqrt(jnp.sum(xv * xv, axis=-1, keepdims=True) * (1.0 / QK) + EPS)
            xh = xv * r
            dxn = dout * c + pltpu.roll(dout * s1, ROPE // 2, 1) + pltpu.roll(dout * s2, HP - ROPE // 2, 1)
            dgain = jnp.sum(dxn * xh, axis=0, keepdims=True)
            dxh = dxn * gain
            dx = r * (dxh - xh * (jnp.sum(dxh * xh, axis=-1, keepdims=True) * (1.0 / QK)))
            return dx, dgain

        dxq, dgq = f(q_ref[...], qg_ref[...], dq_ref[...])
        dxk, dgk = f(k_ref[...] + kr_ref[...], kg_ref[...], dk_ref[...])
        dqr_ref[...] = dxq.astype(BF16)
        dkr_ref[...] = dxk.astype(BF16)
        dqg_ref[...] += dgq
        dkg_ref[...] += dgk
        lane = lax.broadcasted_iota(jnp.int32, dxk.shape, 1)
        dkrope_ref[...] += jnp.where((lane >= NOPE) & (lane < QK), dxk, 0.0)

    head = pl.BlockSpec((tm, HP), lambda i, h: (i, h))
    tab = pl.BlockSpec((tm, HP), lambda i, h: (i, 0))
    gain = pl.BlockSpec((1, HP), lambda i, h: (0, 0))
    return pl.pallas_call(
        body, name=name, grid=(s // tm, HEADS),
        in_specs=[head, head, pl.BlockSpec((tm, HP), lambda i, h: (i, kr_col)), gain, gain, tab, tab, tab, head, head],
        out_specs=[head, head, tab, gain, gain],
        out_shape=[jax.ShapeDtypeStruct((s, HEADS * HP), BF16)] * 2
        + [jax.ShapeDtypeStruct((s, HP), F32), jax.ShapeDtypeStruct((1, HP), F32), jax.ShapeDtypeStruct((1, HP), F32)],
        compiler_params=_cparams("arbitrary", "arbitrary"))(qraw, kv, proj, qg, kg, *tabs, dq, dk)


ATT_SCALE = QK ** -0.5
LOG2E = math.log2(math.e)
Q_SCALE = ATT_SCALE * LOG2E
NEG = -1e30


def _att_tile(s):
    return _pick(s, (512, 256, 128))


def _causal(sv, diag):
    r = lax.broadcasted_iota(jnp.int32, sv.shape, 0)
    c = lax.broadcasted_iota(jnp.int32, sv.shape, 1)
    return jnp.where(diag & (c > r), NEG, sv)


NT = (((1,), (1,)), ((), ()))
TN = (((0,), (0,)), ((), ()))


def _lane_sums_as_rows(xv):
    ones = jnp.ones((SLAB, LANE), F32)
    return lax.dot_general(ones, xv, NT, precision=lax.Precision.HIGHEST, preferred_element_type=F32)


def _row_of(col):
    wide = jnp.broadcast_to(col, (col.shape[0], LANE))
    lane = lax.broadcasted_iota(jnp.int32, wide.shape, 1)
    return _lane_sums_as_rows(jnp.where(lane == 0, wide, 0.0))


def _att_specs(s):
    t = _att_tile(s)
    nb = s // t
    tile = pl.BlockSpec((t, HP), lambda h, i: (i, h))
    whole = pl.BlockSpec((s, HP), lambda h, i: (0, h))
    row = pl.BlockSpec((1, 1, SLAB, t), lambda h, i: (h, i, 0, 0))
    rows = pl.BlockSpec((1, nb, SLAB, t), lambda h, i: (h, 0, 0, 0))
    return t, nb, tile, whole, row, rows


def attn_fwd(q, k, v, *, name):
    s = q.shape[0]
    t, nb, tile, whole, row, _ = _att_specs(s)

    def body(q_ref, k_ref, v_ref, o_ref, oh_ref, lset_ref, s_scr, mb_scr, acc):
        qb = pl.program_id(1)
        qv = q_ref[...]

        def scores(j):
            r0 = pl.multiple_of(j * t, t)
            return lax.dot_general(qv, k_ref[pl.ds(r0, t), :], NT, preferred_element_type=F32)

        def fold(sv):
            m = sv[:, 0:LANE]
            for kk in range(1, t // LANE):
                m = jnp.maximum(m, sv[:, kk * LANE:(kk + 1) * LANE])
            return m

        def first(j, m):
            sv = scores(j)
            s_scr[j] = sv
            return jnp.maximum(m, fold(sv))

        m = lax.fori_loop(0, qb, first, jnp.full((t, LANE), NEG, F32))
        sd = _causal(scores(qb), True)
        s_scr[qb] = sd
        mcol = jnp.max(jnp.maximum(m, fold(sd)), axis=-1, keepdims=True)
        mb_scr[...] = jnp.broadcast_to(mcol, (t, t))
        acc[...] = jnp.zeros_like(acc)

        def second(j, carry):
            r0 = pl.multiple_of(j * t, t)
            p = jnp.exp2(s_scr[j] - mb_scr[...]).astype(BF16)
            acc[...] += jnp.dot(p, v_ref[pl.ds(r0, t), :], preferred_element_type=F32)
            return carry

        lax.fori_loop(0, qb + 1, second, 0)
        av = acc[...]
        lsum = av[:, VD:VD + 1]
        lane = lax.broadcasted_iota(jnp.int32, av.shape, 1)
        ov = jnp.where(lane == VD, 0.0, av / lsum)
        o_ref[...] = ov
        oh_ref[...] = ov.astype(BF16)
        lset_ref[0, 0] = _row_of(mcol + jnp.log2(lsum))

    return pl.pallas_call(
        body, name=name, grid=(HEADS, nb), in_specs=[tile, whole, whole], out_specs=[tile, tile, row],
        out_shape=[jax.ShapeDtypeStruct((s, HEADS * HP), F32), jax.ShapeDtypeStruct((s, HEADS * HP), BF16),
                   jax.ShapeDtypeStruct((HEADS, nb, SLAB, t), F32)],
        scratch_shapes=[pltpu.VMEM((nb, t, t), F32), pltpu.VMEM((t, t), F32), pltpu.VMEM((t, HP), F32)],
        compiler_params=_cparams("parallel", "parallel"))(q, k, v)


def attn_delta(o, do, *, name):
    s = o.shape[0]
    t, nb, tile, _, row, _ = _att_specs(s)

    def body(o_ref, do_ref, doh_ref, dt_ref):
        dov = do_ref[...]
        doh_ref[...] = dov.astype(BF16)
        dt_ref[0, 0] = _lane_sums_as_rows(dov * o_ref[...])

    return pl.pallas_call(
        body, name=name, grid=(HEADS, nb), in_specs=[tile, tile], out_specs=[tile, row],
        out_shape=[jax.ShapeDtypeStruct((s, HEADS * HP), BF16), jax.ShapeDtypeStruct((HEADS, nb, SLAB, t), F32)],
        compiler_params=_cparams("parallel", "parallel"))(o, do)


def attn_bwd(q, k, v, doh, lset, deltat, *, name):
    s = q.shape[0]
    t, nb, tile, whole, _, rows = _att_specs(s)

    def body(q_ref, k_ref, v_ref, do_ref, lt_ref, dt_ref, dq_ref, dk_ref, dv_ref, dk_acc, dv_acc):
        kb = pl.program_id(1)
        kt, vt = k_ref[...], v_ref[...]
        dk_acc[...] = jnp.zeros_like(dk_acc)
        dv_acc[...] = jnp.zeros_like(dv_acc)

        @pl.when(kb == 0)
        def _():
            dq_ref[...] = jnp.zeros_like(dq_ref)

        def step(i, diag):
            r0 = pl.multiple_of(i * t, t)
            qi, doi = q_ref[pl.ds(r0, t), :], do_ref[pl.ds(r0, t), :]
            st = lax.dot_general(kt, qi, NT, preferred_element_type=F32)
            if diag:
                kr = lax.broadcasted_iota(jnp.int32, st.shape, 0)
                qc = lax.broadcasted_iota(jnp.int32, st.shape, 1)
                st = jnp.where(kr > qc, NEG, st)
            pt = jnp.exp2(st - lt_ref[0, i][0:1, :])
            dpt = lax.dot_general(vt, doi, NT, preferred_element_type=F32)
            dst = (pt * (dpt - dt_ref[0, i][0:1, :])).astype(BF16)
            dv_acc[...] += jnp.dot(pt.astype(BF16), doi, preferred_element_type=F32)
            dk_acc[...] += jnp.dot(dst, qi, preferred_element_type=F32)
            dq_ref[pl.ds(r0, t), :] += lax.dot_general(dst, kt, TN, preferred_element_type=F32) * ATT_SCALE

        def off_diag(i, carry):
            step(i, False)
            return carry

        step(kb, True)
        lax.fori_loop(kb + 1, nb, off_diag, 0)
        dk_ref[...] = dk_acc[...] * (1.0 / LOG2E)
        dvv = dv_acc[...]
        lane = lax.broadcasted_iota(jnp.int32, dvv.shape, 1)
        dv_ref[...] = jnp.where(lane == VD, 0.0, dvv).astype(BF16)

    return pl.pallas_call(
        body, name=name, grid=(HEADS, nb), in_specs=[whole, tile, tile, whole, rows, rows],
        out_specs=[whole, tile, tile],
        out_shape=[jax.ShapeDtypeStruct((s, HEADS * HP), F32), jax.ShapeDtypeStruct((s, HEADS * HP), F32),
                   jax.ShapeDtypeStruct((s, HEADS * HP), BF16)],
        scratch_shapes=[pltpu.VMEM((t, HP), F32), pltpu.VMEM((t, HP), F32)],
        compiler_params=_cparams("parallel", "arbitrary"))(q, k, v, doh, lset, deltat)


HALO = 8
CW = 256


def _shifts(zw):
    return zw, pltpu.roll(zw, 1, 0), pltpu.roll(zw, 2, 0)


def _conv3(sh, w):
    return w[2:3] * sh[0] + w[1:2] * sh[1] + w[0:1] * sh[2]


def _conv3_t(dc, w):
    n = dc.shape[0]
    return w[2:3] * dc + w[1:2] * pltpu.roll(dc, n - 1, 0) + w[0:1] * pltpu.roll(dc, n - 2, 0)


def _conv3_dw(dc, sh, r):
    return [jnp.sum(dc * z[HALO:HALO + r], axis=0, keepdims=True) for z in (sh[2], sh[1], sh[0])]


def _halo_specs(r, colfn):
    rb = r // HALO
    cur = pl.BlockSpec((r, CW), lambda j, i: (i, colfn(j)))
    prev = pl.BlockSpec((HALO, CW), lambda j, i: (jnp.maximum(i * rb - 1, 0), colfn(j)))

    def nxt(nrow_blocks):
        return pl.BlockSpec((HALO, CW), lambda j, i: (jnp.minimum((i + 1) * rb, nrow_blocks * rb - 1), colfn(j)))

    return cur, prev, nxt


def ffnact_fwd(up, w, *, name):
    s, c2 = up.shape
    hh = c2 // 2
    nj = hh // CW
    r = _long_rows(s)
    nt = s // r

    def body(g_ref, gp_ref, v_ref, vp_ref, wg_ref, wv_ref, o_ref):
        pm = (pl.program_id(1) > 0).astype(F32)
        cg = _conv3(_shifts(jnp.concatenate([gp_ref[...] * pm, g_ref[...]], axis=0)), wg_ref[...])[HALO:]
        cv = _conv3(_shifts(jnp.concatenate([vp_ref[...] * pm, v_ref[...]], axis=0)), wv_ref[...])[HALO:]
        o_ref[...] = (cg * jax.nn.sigmoid(cg) * cv).astype(BF16)

    gcur, gprev, _ = _halo_specs(r, lambda j: j)
    vcur, vprev, _ = _halo_specs(r, lambda j: nj + j)
    wg = pl.BlockSpec((3, CW), lambda j, i: (0, j))
    wv = pl.BlockSpec((3, CW), lambda j, i: (0, nj + j))
    return pl.pallas_call(
        body, name=name, grid=(nj, nt), in_specs=[gcur, gprev, vcur, vprev, wg, wv],
        out_specs=pl.BlockSpec((r, CW), lambda j, i: (i, j)),
        out_shape=jax.ShapeDtypeStruct((s, hh), BF16),
        compiler_params=_cparams("parallel", "parallel"))(up, up, up, up, w, w)


def ffnact_bwd(up, w, dact, *, name):
    s, c2 = up.shape
    hh = c2 // 2
    nj = hh // CW
    r = _rows(s)
    nt = s // r

    def body(g_ref, gp_ref, gn_ref, v_ref, vp_ref, vn_ref, wg_ref, wv_ref, da_ref, dan_ref,
             dg_ref, dv_ref, dwg_ref, dwv_ref):
        i = pl.program_id(1)
        pm = (i > 0).astype(F32)
        nm = (i < nt - 1).astype(F32)

        @pl.when(i == 0)
        def _():
            dwg_ref[...] = jnp.zeros_like(dwg_ref)
            dwv_ref[...] = jnp.zeros_like(dwv_ref)

        wg, wv = wg_ref[...], wv_ref[...]
        zg = jnp.concatenate([gp_ref[...] * pm, g_ref[...], gn_ref[...]], axis=0)
        zv = jnp.concatenate([vp_ref[...] * pm, v_ref[...], vn_ref[...]], axis=0)
        zg, zv = _shifts(zg), _shifts(zv)
        cg = _conv3(zg, wg)[HALO:]
        cv = _conv3(zv, wv)[HALO:]
        da = jnp.concatenate([da_ref[...], dan_ref[...] * nm], axis=0)
        sg = jax.nn.sigmoid(cg)
        dcg = da * cv * (sg * (1.0 + cg * (1.0 - sg)))
        dcv = da * (cg * sg)
        dg_ref[...] = _conv3_t(dcg, wg)[:r].astype(BF16)
        dv_ref[...] = _conv3_t(dcv, wv)[:r].astype(BF16)
        for kk, (a, b) in enumerate(zip(_conv3_dw(dcg[:r], zg, r), _conv3_dw(dcv[:r], zv, r))):
            dwg_ref[kk:kk + 1, :] += a
            dwv_ref[kk:kk + 1, :] += b

    gcur, gprev, gnext = _halo_specs(r, lambda j: j)
    vcur, vprev, vnext = _halo_specs(r, lambda j: nj + j)
    acur, _, anext = _halo_specs(r, lambda j: j)
    wg = pl.BlockSpec((3, CW), lambda j, i: (0, j))
    wv = pl.BlockSpec((3, CW), lambda j, i: (0, nj + j))
    dupg, dupv, dwg, dwv = pl.pallas_call(
        body, name=name, grid=(nj, nt),
        in_specs=[gcur, gprev, gnext(nt), vcur, vprev, vnext(nt), wg, wv, acur, anext(nt)],
        out_specs=[acur, acur, wg, wg],
        out_shape=[jax.ShapeDtypeStruct((s, hh), BF16), jax.ShapeDtypeStruct((s, hh), BF16),
                   jax.ShapeDtypeStruct((3, hh), F32), jax.ShapeDtypeStruct((3, hh), F32)],
        compiler_params=_cparams("parallel", "arbitrary"))(up, up, up, up, up, up, w, w, dact, dact)
    return (dupg, dupv), jnp.concatenate([dwg, dwv], axis=1)


def sconv_fwd(proj, w, *, name):
    s = proj.shape[0]
    nj = CONVC // CW
    r = _long_rows(s)
    nt = s // r

    def body(b_ref, c_ref, cp_ref, x_ref, xp_ref, w_ref, o_ref):
        pm = (pl.program_id(1) > 0).astype(F32)
        zw = jnp.concatenate([cp_ref[...] * xp_ref[...] * pm, c_ref[...] * x_ref[...]], axis=0)
        o_ref[...] = (b_ref[...] * _conv3(_shifts(zw), w_ref[...])[HALO:]).astype(BF16)

    bcur, _, _ = _halo_specs(r, lambda j: (QR + KVR) // CW + j)
    ccur, cprev, _ = _halo_specs(r, lambda j: (QR + KVR + CONVC) // CW + j)
    xcur, xprev, _ = _halo_specs(r, lambda j: (QR + KVR + 2 * CONVC) // CW + j)
    ws = pl.BlockSpec((3, CW), lambda j, i: (0, j))
    return pl.pallas_call(
        body, name=name, grid=(nj, nt), in_specs=[bcur, ccur, cprev, xcur, xprev, ws],
        out_specs=pl.BlockSpec((r, CW), lambda j, i: (i, j)),
        out_shape=jax.ShapeDtypeStruct((s, CONVC), BF16),
        compiler_params=_cparams("parallel", "parallel"))(proj, proj, proj, proj, proj, w)


def sconv_bwd(proj, w, dy, *, name):
    s = proj.shape[0]
    nj = CONVC // CW
    r = _rows(s)
    nt = s // r

    def body(b_ref, bn_ref, c_ref, cp_ref, x_ref, xp_ref, w_ref, dy_ref, dyn_ref, db_ref, dc_ref, dx_ref, dw_ref):
        i = pl.program_id(1)
        pm = (i > 0).astype(F32)
        nm = (i < nt - 1).astype(F32)

        @pl.when(i == 0)
        def _():
            dw_ref[...] = jnp.zeros_like(dw_ref)

        wv = w_ref[...]
        zw = jnp.concatenate([cp_ref[...] * xp_ref[...] * pm, c_ref[...] * x_ref[...]], axis=0)
        zw = _shifts(zw)
        conv = _conv3(zw, wv)[HALO:]
        dyv = dy_ref[...]
        db_ref[...] = (dyv * conv).astype(BF16)
        dconv = jnp.concatenate([dyv * b_ref[...], dyn_ref[...] * bn_ref[...] * nm], axis=0)
        dz = _conv3_t(dconv, wv)[:r]
        dc_ref[...] = (dz * x_ref[...]).astype(BF16)
        dx_ref[...] = (dz * c_ref[...]).astype(BF16)
        for kk, a in enumerate(_conv3_dw(dconv[:r], zw, r)):
            dw_ref[kk:kk + 1, :] += a

    bcur, _, bnext = _halo_specs(r, lambda j: (QR + KVR) // CW + j)
    ccur, cprev, _ = _halo_specs(r, lambda j: (QR + KVR + CONVC) // CW + j)
    xcur, xprev, _ = _halo_specs(r, lambda j: (QR + KVR + 2 * CONVC) // CW + j)
    ycur, _, ynext = _halo_specs(r, lambda j: j)
    ws = pl.BlockSpec((3, CW), lambda j, i: (0, j))
    out = pl.BlockSpec((r, CW), lambda j, i: (i, j))
    db, dc, dx, dw = pl.pallas_call(
        body, name=name, grid=(nj, nt),
        in_specs=[bcur, bnext(nt), ccur, cprev, xcur, xprev, ws, ycur, ynext(nt)],
        out_specs=[out, out, out, ws],
        out_shape=[jax.ShapeDtypeStruct((s, CONVC), BF16)] * 3 + [jax.ShapeDtypeStruct((3, CONVC), F32)],
        compiler_params=_cparams("parallel", "arbitrary"))(proj, proj, proj, proj, proj, proj, w, dy, dy)
    return (db, dc, dx), dw


SW = 512
NJ = NST // SW


def _scan_tables(ar, ai):
    def cmul(x, y):
        return x[0] * y[0] - x[1] * y[1], x[0] * y[1] + x[1] * y[0]

    def build(a, reverse):
        pw = [a]
        for _ in range(SLAB - 1):
            pw.append(cmul(pw[-1], a))
        row = jnp.arange(SLAB)[:, None]
        tabs = []
        for kk in (1, 2, 4):
            mask = ((row < SLAB - kk) if reverse else (row >= kk)).astype(F32)
            tabs += [mask * pw[kk - 1][0][None, :], mask * pw[kk - 1][1][None, :]]
        order = list(range(SLAB - 1, -1, -1)) if reverse else list(range(SLAB))
        tabs += [jnp.stack([pw[o][0] for o in order]), jnp.stack([pw[o][1] for o in order])]
        return jnp.stack(tabs)

    return build((ar, ai), False), build((ar, -ai), True)


def _slab_scan(xr, xi, tabs, cr, ci, reverse):
    for n, kk in enumerate((1, 2, 4)):
        sh = SLAB - kk if reverse else kk
        tr, ti = tabs[2 * n], tabs[2 * n + 1]
        sr, si = pltpu.roll(xr, sh, 0), pltpu.roll(xi, sh, 0)
        xr, xi = xr + tr * sr - ti * si, xi + tr * si + ti * sr
    tr, ti = tabs[6], tabs[7]
    return xr + tr * cr - ti * ci, xi + tr * ci + ti * cr


def s5_fwd(u, bbd_r, bbd_i, cbd_r, cbd_i, tab, *, name):
    s = u.shape[0]
    tbk = _long_rows(s)
    nt = s // tbk
    nsl = tbk // SLAB

    def body(u_ref, br_ref, bi_ref, cr_ref, ci_ref, tab_ref, y_ref, sr_ref, si_ref, bur, bui, carry):
        @pl.when(pl.program_id(1) == 0)
        def _():
            carry[...] = jnp.zeros_like(carry)

        ub = u_ref[...].astype(BF16)
        bur[...] = jnp.dot(ub, br_ref[0], preferred_element_type=F32)
        bui[...] = jnp.dot(ub, bi_ref[0], preferred_element_type=F32)
        tabs = [tab_ref[n] for n in range(8)]

        def slab(n, c):
            r0 = pl.multiple_of(n * SLAB, SLAB)
            sr, si = _slab_scan(bur[pl.ds(r0, SLAB), :], bui[pl.ds(r0, SLAB), :], tabs, c[0], c[1], False)
            sr_ref[pl.ds(r0, SLAB), :] = sr
            si_ref[pl.ds(r0, SLAB), :] = si
            return (jnp.broadcast_to(sr[SLAB - 1:SLAB], sr.shape), jnp.broadcast_to(si[SLAB - 1:SLAB], si.shape))

        cr, ci = lax.fori_loop(0, nsl, slab, (carry[0], carry[1]))
        carry[0] = cr
        carry[1] = ci
        y_ref[...] = (jnp.dot(sr_ref[...].astype(BF16), cr_ref[0], preferred_element_type=F32)
                      - jnp.dot(si_ref[...].astype(BF16), ci_ref[0], preferred_element_type=F32))

    us = pl.BlockSpec((tbk, LANE), lambda j, t: (t, j))
    bs = pl.BlockSpec((1, LANE, SW), lambda j, t: (j, 0, 0))
    cs = pl.BlockSpec((1, SW, LANE), lambda j, t: (j, 0, 0))
    ts = pl.BlockSpec((8, SLAB, SW), lambda j, t: (0, 0, j))
    ss = pl.BlockSpec((tbk, SW), lambda j, t: (t, j))
    return pl.pallas_call(
        body, name=name, grid=(NJ, nt), in_specs=[us, bs, bs, cs, cs, ts], out_specs=[us, ss, ss],
        out_shape=[jax.ShapeDtypeStruct((s, D_MODEL), F32), jax.ShapeDtypeStruct((s, NST), F32),
                   jax.ShapeDtypeStruct((s, NST), F32)],
        scratch_shapes=[pltpu.VMEM((tbk, SW), F32), pltpu.VMEM((tbk, SW), F32), pltpu.VMEM((2, SLAB, SW), F32)],
        compiler_params=_cparams("parallel", "arbitrary"))(u, bbd_r, bbd_i, cbd_r, cbd_i, tab)


def s5_bwd(u, dy, dskip, st_r, st_i, bbd_r, bbd_i, cbd_r, cbd_i, tabrev, *, name):
    s = u.shape[0]
    tbk = _long_rows(s)
    nt = s // tbk
    nsl = tbk // SLAB
    rbk = tbk // SLAB

    def body(u_ref, dy_ref, d_ref, sr_ref, si_ref, pr_ref, pi_ref, br_ref, bi_ref, cr_ref, ci_ref, tab_ref,
             du_ref, dbr_ref, dbi_ref, dcr_ref, dci_ref, da_ref, lam_r, lam_i, carry):
        t = pl.program_id(1)

        @pl.when(t == 0)
        def _():
            carry[...] = jnp.zeros_like(carry)
            dbr_ref[...] = jnp.zeros_like(dbr_ref)
            dbi_ref[...] = jnp.zeros_like(dbi_ref)
            dcr_ref[...] = jnp.zeros_like(dcr_ref)
            dci_ref[...] = jnp.zeros_like(dci_ref)
            da_ref[...] = jnp.zeros_like(da_ref)

        dyv = dy_ref[...]
        dyh = dyv.astype(BF16)
        lam_r[...] = lax.dot_general(dyh, cr_ref[0], NT, preferred_element_type=F32)
        lam_i[...] = -lax.dot_general(dyh, ci_ref[0], NT, preferred_element_type=F32)
        tabs = [tab_ref[n] for n in range(8)]

        def slab(n, c):
            r0 = pl.multiple_of((nsl - 1 - n) * SLAB, SLAB)
            lr, li = _slab_scan(lam_r[pl.ds(r0, SLAB), :], lam_i[pl.ds(r0, SLAB), :], tabs, c[0], c[1], True)
            lam_r[pl.ds(r0, SLAB), :] = lr
            lam_i[pl.ds(r0, SLAB), :] = li
            return (jnp.broadcast_to(lr[0:1], lr.shape), jnp.broadcast_to(li[0:1], li.shape))

        cr, ci = lax.fori_loop(0, nsl, slab, (carry[0], carry[1]))
        carry[0] = cr
        carry[1] = ci
        lr, li = lam_r[...], lam_i[...]
        lrh, lih = lr.astype(BF16), li.astype(BF16)
        du = (dyv * d_ref[...] + lax.dot_general(lrh, br_ref[0], NT, preferred_element_type=F32)
              + lax.dot_general(lih, bi_ref[0], NT, preferred_element_type=F32))
        du_ref[...] = du.astype(BF16)
        ub = u_ref[...].astype(BF16)
        dbr_ref[0] += lax.dot_general(ub, lrh, TN, preferred_element_type=F32)
        dbi_ref[0] += lax.dot_general(ub, lih, TN, preferred_element_type=F32)
        srv, siv = sr_ref[...], si_ref[...]
        dcr_ref[0] += lax.dot_general(srv.astype(BF16), dyh, TN, preferred_element_type=F32)
        dci_ref[0] -= lax.dot_general(siv.astype(BF16), dyh, TN, preferred_element_type=F32)
        first = lax.broadcasted_iota(jnp.int32, srv.shape, 0) == 0
        pm = (t < nt - 1).astype(F32)
        spr = jnp.where(first, pr_ref[SLAB - 1:SLAB, :] * pm, pltpu.roll(srv, 1, 0))
        spi = jnp.where(first, pi_ref[SLAB - 1:SLAB, :] * pm, pltpu.roll(siv, 1, 0))
        da_ref[0:1, :] += jnp.sum(lr * spr + li * spi, axis=0, keepdims=True)
        da_ref[1:2, :] += jnp.sum(li * spr - lr * spi, axis=0, keepdims=True)

    rv = lambda t: nt - 1 - t
    us = pl.BlockSpec((tbk, LANE), lambda j, t: (rv(t), j))
    ds = pl.BlockSpec((1, LANE), lambda j, t: (0, j))
    ss = pl.BlockSpec((tbk, SW), lambda j, t: (rv(t), j))
    ps = pl.BlockSpec((SLAB, SW), lambda j, t: (jnp.maximum(rv(t) * rbk - 1, 0), j))
    bs = pl.BlockSpec((1, LANE, SW), lambda j, t: (j, 0, 0))
    cs = pl.BlockSpec((1, SW, LANE), lambda j, t: (j, 0, 0))
    ts = pl.BlockSpec((8, SLAB, SW), lambda j, t: (0, 0, j))
    das = pl.BlockSpec((2, SW), lambda j, t: (0, j))
    return pl.pallas_call(
        body, name=name, grid=(NJ, nt),
        in_specs=[us, us, ds, ss, ss, ps, ps, bs, bs, cs, cs, ts],
        out_specs=[us, bs, bs, cs, cs, das],
        out_shape=[jax.ShapeDtypeStruct((s, D_MODEL), BF16),
                   jax.ShapeDtypeStruct((NJ, LANE, SW), F32), jax.ShapeDtypeStruct((NJ, LANE, SW), F32),
                   jax.ShapeDtypeStruct((NJ, SW, LANE), F32), jax.ShapeDtypeStruct((NJ, SW, LANE), F32),
                   jax.ShapeDtypeStruct((2, NST), F32)],
        scratch_shapes=[pltpu.VMEM((tbk, SW), F32), pltpu.VMEM((tbk, SW), F32), pltpu.VMEM((2, SLAB, SW), F32)],
        compiler_params=_cparams("parallel", "arbitrary"))(
            u, dy, dskip, st_r, st_i, st_r, st_i, bbd_r, bbd_i, cbd_r, cbd_i, tabrev)


GELU_C = math.sqrt(2.0 / math.pi)
GELU_A = 0.044715


def s5post_fwd(y, u, dskip, *, name):
    s = y.shape[0]
    tm = _rows(s)

    def body(y_ref, u_ref, d_ref, o_ref):
        z = y_ref[...] + d_ref[...] * u_ref[...]
        o_ref[...] = (0.5 * z * (1.0 + jnp.tanh(GELU_C * (z + GELU_A * z * z * z)))).astype(BF16)

    row = pl.BlockSpec((tm, D_MODEL), lambda i: (i, 0))
    vec = pl.BlockSpec((1, D_MODEL), lambda i: (0, 0))
    return pl.pallas_call(body, name=name, grid=(s // tm,), in_specs=[row, row, vec], out_specs=row,
                          out_shape=jax.ShapeDtypeStruct((s, D_MODEL), BF16),
                          compiler_params=_cparams("parallel"))(y, u, dskip)


def s5post_bwd(y, u, dskip, dg, *, name):
    s = y.shape[0]
    tm = _rows(s)

    def body(y_ref, u_ref, d_ref, dg_ref, dz_ref, dd_ref):
        @pl.when(pl.program_id(0) == 0)
        def _():
            dd_ref[...] = jnp.zeros_like(dd_ref)

        uv = u_ref[...]
        z = y_ref[...] + d_ref[...] * uv
        th = jnp.tanh(GELU_C * (z + GELU_A * z * z * z))
        dgelu = 0.5 * (1.0 + th) + 0.5 * z * (1.0 - th * th) * (GELU_C * (1.0 + 3.0 * GELU_A * z * z))
        dz = dg_ref[...] * dgelu
        dz_ref[...] = dz
        dd_ref[...] += jnp.sum(dz * uv, axis=0, keepdims=True)

    row = pl.BlockSpec((tm, D_MODEL), lambda i: (i, 0))
    vec = pl.BlockSpec((1, D_MODEL), lambda i: (0, 0))
    return pl.pallas_call(body, name=name, grid=(s // tm,), in_specs=[row, row, vec, row], out_specs=[row, vec],
                          out_shape=[jax.ShapeDtypeStruct((s, D_MODEL), F32), jax.ShapeDtypeStruct((1, D_MODEL), F32)],
                          compiler_params=_cparams("arbitrary"))(y, u, dskip, dg)


def glu_fwd(glu, x, *, name):
    s = x.shape[0]
    tm = _rows(s)

    def body(a_ref, b_ref, x_ref, o_ref):
        o_ref[...] = x_ref[...] + a_ref[...] * jax.nn.sigmoid(b_ref[...])

    row = pl.BlockSpec((tm, D_MODEL), lambda i: (i, 0))
    return pl.pallas_call(body, name=name, grid=(s // tm,),
                          in_specs=[row, pl.BlockSpec((tm, D_MODEL), lambda i: (i, 1)), row], out_specs=row,
                          out_shape=jax.ShapeDtypeStruct((s, D_MODEL), F32),
                          compiler_params=_cparams("parallel"))(glu, glu, x)


def glu_bwd(glu, dx, *, name):
    s = dx.shape[0]
    tm = _rows(s)

    def body(a_ref, b_ref, dx_ref, o_ref):
        sg = jax.nn.sigmoid(b_ref[...])
        dxv = dx_ref[...]
        o_ref[:, :D_MODEL] = (dxv * sg).astype(BF16)
        o_ref[:, D_MODEL:] = (dxv * a_ref[...] * sg * (1.0 - sg)).astype(BF16)

    row = pl.BlockSpec((tm, D_MODEL), lambda i: (i, 0))
    return pl.pallas_call(body, name=name, grid=(s // tm,),
                          in_specs=[row, pl.BlockSpec((tm, D_MODEL), lambda i: (i, 1)), row],
                          out_specs=pl.BlockSpec((tm, 2 * D_MODEL), lambda i: (i, 0)),
                          out_shape=jax.ShapeDtypeStruct((s, 2 * D_MODEL), BF16),
                          compiler_params=_cparams("parallel"))(glu, glu, dx)


def loss_head(y, target, *, name):
    s = y.shape[0]
    tm = _rows(s)

    def body(y_ref, t_ref, dy_ref, dyh_ref, l_ref):
        @pl.when(pl.program_id(0) == 0)
        def _():
            l_ref[...] = jnp.zeros_like(l_ref)

        e = y_ref[...] - t_ref[...]
        dy_ref[...] = e * (1.0 / D_MODEL)
        dyh_ref[...] = (e * (1.0 / D_MODEL)).astype(BF16)
        e2 = jnp.sum((e * e).reshape(tm // 8, 8, D_MODEL), axis=0)
        acc = e2[:, 0:LANE]
        for kk in range(1, D_MODEL // LANE):
            acc = acc + e2[:, kk * LANE:(kk + 1) * LANE]
        l_ref[...] += acc

    row = pl.BlockSpec((tm, D_MODEL), lambda i: (i, 0))
    return pl.pallas_call(body, name=name, grid=(s // tm,), in_specs=[row, row],
                          out_specs=[row, row, pl.BlockSpec((8, LANE), lambda i: (0, 0))],
                          out_shape=[jax.ShapeDtypeStruct((s, D_MODEL), F32), jax.ShapeDtypeStruct((s, D_MODEL), BF16),
                                     jax.ShapeDtypeStruct((8, LANE), F32)],
                          compiler_params=_cparams("arbitrary"))(y, target)


PACKW = 1024
NCHIP = 4


def _mesh_pos():
    return lax.axis_index("x"), lax.axis_index("y"), lax.axis_index("c")


def _chip_exchange(bufs, scatter, name):
    n = len(bufs)
    shapes = [b.shape[1:] if scatter else b.shape for b in bufs]

    def body(*refs):
        ins, outs = refs[:n], refs[n:2 * n]
        send_sems, recv_sems, local_sems = refs[2 * n:]
        x, y, c = _mesh_pos()
        me = 2 * x + y
        peers = [(1 - x, y), (x, 1 - y), (1 - x, 1 - y)]

        def copy(a, j, px, py, dst_slot):
            src = ins[a].at[2 * px + py] if scatter else ins[a]
            return pltpu.make_async_remote_copy(src_ref=src, dst_ref=outs[a].at[dst_slot],
                                                send_sem=send_sems.at[3 * a + j], recv_sem=recv_sems.at[3 * a + j],
                                                device_id=(px, py, c), device_id_type=MESH)

        mine = [pltpu.make_async_copy(ins[a].at[me] if scatter else ins[a], outs[a].at[me], local_sems.at[a])
                for a in range(n)]
        sends = [copy(a, j, px, py, me) for a in range(n) for j, (px, py) in enumerate(peers)]
        for cp in mine + sends:
            cp.start()
        for a in range(n):
            for j, (px, py) in enumerate(peers):
                copy(a, j, px, py, 2 * px + py).wait_recv()
        for cp in sends:
            cp.wait_send()
        for cp in mine:
            cp.wait()

    return pl.pallas_call(
        body, name=name, in_specs=[ANY] * n, out_specs=[ANY] * n,
        out_shape=[jax.ShapeDtypeStruct((NCHIP,) + tuple(shp), b.dtype) for shp, b in zip(shapes, bufs)],
        scratch_shapes=[pltpu.SemaphoreType.DMA((3 * n,)), pltpu.SemaphoreType.DMA((3 * n,)),
                        pltpu.SemaphoreType.DMA((n,))],
    )(*bufs)


HBM_SPEC = pl.BlockSpec(memory_space=pltpu.HBM)
SEM_SPEC = pl.BlockSpec(memory_space=pltpu.SEMAPHORE)
DATAFLOW = pltpu.SideEffectType.DATAFLOW_SIDE_EFFECTING


def _exchange_copy(ins, lands, send_sems, recv_sems, scatter, a, j, px, py, c, dst_slot):
    src = ins[a].at[2 * px + py] if scatter else ins[a]
    return pltpu.make_async_remote_copy(src_ref=src, dst_ref=lands[a].at[dst_slot],
                                        send_sem=send_sems.at[3 * a + j], recv_sem=recv_sems.at[3 * a + j],
                                        device_id=(px, py, c), device_id_type=MESH)


def _own_copy(ins, lands, local_sems, scatter, a, me):
    return pltpu.make_async_copy(ins[a].at[me] if scatter else ins[a], lands[a].at[me], local_sems.at[a])


def exchange_start(bufs, scatter, name):
    n = len(bufs)
    lands = [lax.empty((NCHIP,) + tuple(b.shape[1:] if scatter else b.shape), b.dtype) for b in bufs]

    def body(*refs):
        ins, lnd, send_sems, recv_sems, local_sems, token = (refs[:n], refs[n:2 * n], refs[2 * n], refs[2 * n + 1],
                                                             refs[2 * n + 2], refs[-1])
        x, y, c = _mesh_pos()
        me = 2 * x + y
        for a in range(n):
            _own_copy(ins, lnd, local_sems, scatter, a, me).start()
            for j, (px, py) in enumerate([(1 - x, y), (x, 1 - y), (1 - x, 1 - y)]):
                _exchange_copy(ins, lnd, send_sems, recv_sems, scatter, a, j, px, py, c, me).start()
        token[...] = jnp.zeros_like(token)

    thru = [pltpu.HBM(b.shape, b.dtype) for b in list(bufs) + lands]
    out = pl.pallas_call(
        body, name=name, in_specs=[HBM_SPEC] * (2 * n),
        out_specs=[SEM_SPEC] * 3 + [HBM_SPEC] * (2 * n) + [pl.BlockSpec(memory_space=pltpu.VMEM)],
        out_shape=[pltpu.SemaphoreType.DMA((3 * n,)), pltpu.SemaphoreType.DMA((3 * n,)), pltpu.SemaphoreType.DMA((n,))]
        + thru + [jax.ShapeDtypeStruct((SLAB, LANE), F32)],
        input_output_aliases={k: 3 + k for k in range(2 * n)},
        compiler_params=pltpu.CompilerParams(has_side_effects=DATAFLOW),
    )(*[pltpu.with_memory_space_constraint(b, pltpu.HBM) for b in list(bufs) + lands])
    return tuple(out[:3]), out[3:3 + n], out[3 + n:3 + 2 * n], out[-1][0, 0]


def exchange_wait(started, after, scatter, name):
    sems, bufs, lands, _ = started
    n = len(bufs)

    def body(*refs):
        ins, lnd, ssem, rsem, lsem = refs[:n], refs[n:2 * n], refs[2 * n], refs[2 * n + 1], refs[2 * n + 2]
        x, y, c = _mesh_pos()
        for a in range(n):
            _own_copy(ins, lnd, lsem, scatter, a, 2 * x + y).wait()
            for j, (px, py) in enumerate([(1 - x, y), (x, 1 - y), (1 - x, 1 - y)]):
                cp = _exchange_copy(ins, lnd, ssem, rsem, scatter, a, j, px, py, c, 2 * px + py)
                cp.wait_send()
                cp.wait_recv()

    thru = [pltpu.HBM(b.shape, b.dtype) for b in list(bufs) + list(lands)]
    out = pl.pallas_call(
        body, name=name, in_specs=[HBM_SPEC] * (2 * n) + [SEM_SPEC] * 3 + [ANY],
        out_specs=[HBM_SPEC] * (2 * n), out_shape=thru,
        input_output_aliases={k: k for k in range(2 * n)},
        compiler_params=pltpu.CompilerParams(has_side_effects=DATAFLOW),
    )(*bufs, *lands, *sems, after)
    return out[n:]


def sibling_swap(bufs, name):
    n = len(bufs)

    def body(*refs):
        ins, outs, send_sems, recv_sems = refs[:n], refs[n:2 * n], refs[2 * n], refs[2 * n + 1]
        x, y, c = _mesh_pos()
        cps = [pltpu.make_async_remote_copy(src_ref=ins[k], dst_ref=outs[k], send_sem=send_sems.at[k],
                                            recv_sem=recv_sems.at[k], device_id=(x, y, 1 - c), device_id_type=MESH)
               for k in range(n)]
        for cp in cps:
            cp.start()
        for cp in cps:
            cp.wait()

    return pl.pallas_call(
        body, name=name, in_specs=[ANY] * n, out_specs=[ANY] * n,
        out_shape=[jax.ShapeDtypeStruct(b.shape, b.dtype) for b in bufs],
        scratch_shapes=[pltpu.SemaphoreType.DMA((n,)), pltpu.SemaphoreType.DMA((n,))],
    )(*bufs)


def _swap_copy(ins, lands, send_sems, recv_sems, a):
    x, y, c = _mesh_pos()
    return pltpu.make_async_remote_copy(src_ref=ins[a], dst_ref=lands[a], send_sem=send_sems.at[a],
                                        recv_sem=recv_sems.at[a], device_id=(x, y, 1 - c), device_id_type=MESH)


def swap_start(bufs, name):
    n = len(bufs)
    lands = [lax.empty(b.shape, b.dtype) for b in bufs]

    def body(*refs):
        for a in range(n):
            _swap_copy(refs[:n], refs[n:2 * n], refs[2 * n], refs[2 * n + 1], a).start()
        refs[-1][...] = jnp.zeros_like(refs[-1])

    thru = [pltpu.HBM(b.shape, b.dtype) for b in list(bufs) + lands]
    out = pl.pallas_call(
        body, name=name, in_specs=[HBM_SPEC] * (2 * n),
        out_specs=[SEM_SPEC] * 2 + [HBM_SPEC] * (2 * n) + [pl.BlockSpec(memory_space=pltpu.VMEM)],
        out_shape=[pltpu.SemaphoreType.DMA((n,)), pltpu.SemaphoreType.DMA((n,))] + thru
        + [jax.ShapeDtypeStruct((SLAB, LANE), F32)],
        input_output_aliases={k: 2 + k for k in range(2 * n)},
        compiler_params=pltpu.CompilerParams(has_side_effects=DATAFLOW),
    )(*[pltpu.with_memory_space_constraint(b, pltpu.HBM) for b in list(bufs) + lands])
    return tuple(out[:2]), out[2:2 + n], out[2 + n:2 + 2 * n], out[-1][0, 0]


def swap_wait(started, after, name):
    sems, bufs, lands, _ = started
    n = len(bufs)

    def body(*refs):
        for a in range(n):
            _swap_copy(refs[:n], refs[n:2 * n], refs[2 * n], refs[2 * n + 1], a).wait()

    thru = [pltpu.HBM(b.shape, b.dtype) for b in list(bufs) + list(lands)]
    out = pl.pallas_call(
        body, name=name, in_specs=[HBM_SPEC] * (2 * n) + [SEM_SPEC] * 2 + [ANY],
        out_specs=[HBM_SPEC] * (2 * n), out_shape=thru,
        input_output_aliases={k: k for k in range(2 * n)},
        compiler_params=pltpu.CompilerParams(has_side_effects=DATAFLOW),
    )(*bufs, *lands, *sems, after)
    return out[n:]


EW_VMEM_BUDGET = 20 * 1024 * 1024


def _ew_rows(rows, w, bytes_per_elem):
    wpad = -(-w // LANE) * LANE
    for t in (1024, 512, 256, 128, 64, 32, 16, 8):
        if rows % t == 0 and 2 * t * wpad * bytes_per_elem <= EW_VMEM_BUDGET:
            return t
    return rows


def sum_slots(buf, *, name):
    _, rows, w = buf.shape
    tm = _ew_rows(rows, w, NCHIP * buf.dtype.itemsize + 4)

    def body(b_ref, o_ref):
        acc = b_ref[0].astype(F32)
        for kk in range(1, NCHIP):
            acc = acc + b_ref[kk].astype(F32)
        o_ref[...] = acc

    return pl.pallas_call(body, name=name, grid=(rows // tm,),
                          in_specs=[pl.BlockSpec((NCHIP, tm, w), lambda i: (0, i, 0))],
                          out_specs=pl.BlockSpec((tm, w), lambda i: (i, 0)),
                          out_shape=jax.ShapeDtypeStruct((rows, w), F32),
                          compiler_params=_cparams("parallel"))(buf)


def adamw(p_mine, p_other, w, m, v, *, name):
    rows, wd = w.shape
    tm = _ew_rows(rows, wd, 9 * 4)
    c1 = 1.0 - ADAM_B1 ** ADAM_STEP
    c2 = 1.0 - ADAM_B2 ** ADAM_STEP

    def body(a_ref, b_ref, w_ref, m_ref, v_ref, g_ref, d_ref, nm_ref, nv_ref):
        g = a_ref[...] + b_ref[...]
        nm = ADAM_B1 * m_ref[...] + (1.0 - ADAM_B1) * g
        nv = ADAM_B2 * v_ref[...] + (1.0 - ADAM_B2) * (g * g)
        g_ref[...] = g
        nm_ref[...] = nm
        nv_ref[...] = nv
        d_ref[...] = -ADAM_LR * ((nm / c1) / (jnp.sqrt(nv / c2) + ADAM_EPS) + ADAM_WD * w_ref[...])

    row = pl.BlockSpec((tm, wd), lambda i: (i, 0))
    return pl.pallas_call(body, name=name, grid=(rows // tm,), in_specs=[row] * 5, out_specs=[row] * 4,
                          out_shape=[jax.ShapeDtypeStruct((rows, wd), F32)] * 4,
                          compiler_params=_cparams("parallel"))(p_mine, p_other, w, m, v)


def _rows2d(a, lead=0):
    tail = a.shape[lead:]
    n = int(np.prod(tail))
    if tail[-1] < LANE // 2 and n % (8 * LANE) == 0:
        return a.reshape(a.shape[:lead] + (n // (8 * LANE), 8 * LANE))
    return a.reshape(a.shape[:lead] + (-1, tail[-1]))


BIG = [("mix_w_in", 2), ("w_uq", 2), ("w_ukv", 2), ("mix_w_out", 1), ("ssm_w_in", 1), ("w_glu", 2),
       ("ffn_w_up", 2), ("ffn_w_down", 1)]
SMALL = [("sconv_w", 2), ("ssm_norm", 1), ("d_skip", 1), ("ffn_conv_w", 2)]
REPL = ["attn_norm", "cq_norm", "ckv_norm", "q_gain", "k_gain", "lambda_re", "lambda_im", "log_step",
        "b_re", "b_im", "c_re", "c_im", "ffn_norm"]
ORDER = ["attn_norm", "mix_w_in", "cq_norm", "ckv_norm", "w_uq", "w_ukv", "q_gain", "k_gain", "sconv_w", "mix_w_out",
         "ssm_norm", "ssm_w_in", "lambda_re", "lambda_im", "log_step", "b_re", "b_im", "c_re", "c_im", "d_skip",
         "w_glu", "ffn_norm", "ffn_w_up", "ffn_conv_w", "ffn_w_down"]


def _join(g, axis):
    return jnp.concatenate([g[k] for k in range(NCHIP)], axis=axis)


def _split(full, axis, parts=NCHIP):
    return jnp.stack(jnp.split(full, parts, axis=axis))


def _discretize(lr, li, ls, b_re, b_im):
    dt = jnp.exp(ls)[:, None]
    mag = jnp.exp(lr * dt)
    ar, ai = mag * jnp.cos(li * dt), mag * jnp.sin(li * dt)
    nr, ni = ar - 1.0, ai
    den = lr * lr + li * li
    zr, zi = (nr * lr + ni * li) / den, (ni * lr - nr * li) / den
    bbar_r = zr[..., None] * b_re - zi[..., None] * b_im
    bbar_i = zr[..., None] * b_im + zi[..., None] * b_re
    return ar, ai, bbar_r, bbar_i


def _b_blockdiag(bbar):
    gl = G // NJ
    bb = bbar.reshape(NJ, gl, P, GC).transpose(0, 1, 3, 2)
    return jnp.einsum("jgcp,gh->jgchp", bb, jnp.eye(gl, dtype=bbar.dtype)).reshape(NJ, gl * GC, gl * P)


def _b_blockdiag_t(dbd):
    gl = G // NJ
    d = jnp.einsum("jgchp,gh->jgcp", dbd.reshape(NJ, gl, GC, gl, P), jnp.eye(gl, dtype=dbd.dtype))
    return d.transpose(0, 1, 3, 2).reshape(G, P, GC)


def _c_blockdiag(cmat):
    gl = G // NJ
    cc = cmat.reshape(NJ, gl, GC, P).transpose(0, 1, 3, 2)
    return jnp.einsum("jgpc,gh->jgphc", cc, jnp.eye(gl, dtype=cmat.dtype)).reshape(NJ, gl * P, gl * GC)


def _c_blockdiag_t(dbd):
    gl = G // NJ
    d = jnp.einsum("jgphc,gh->jgpc", dbd.reshape(NJ, gl, P, gl, GC), jnp.eye(gl, dtype=dbd.dtype))
    return d.transpose(0, 1, 3, 2).reshape(G, GC, P)


def _pad_heads_cols(w, width):
    r = w.shape[0]
    return jnp.pad(w.reshape(r, HEADS, width), ((0, 0), (0, 0), (0, HP - width))).reshape(r, HEADS * HP)


def _unpad_heads_cols(w, width):
    r = w.shape[0]
    return w.reshape(r, HEADS, HP)[:, :, :width].reshape(r, HEADS * width)


W_IN_SPLIT = (QR + KVR, QR + KVR + ROPE)


def _w_in_layout(w):
    a, b = W_IN_SPLIT
    kr = jnp.pad(w[:, a:b], ((0, 0), (NOPE, HP - QK)))
    return jnp.concatenate([w[:, :a], w[:, b:], kr], axis=1)


def _ffn_fwd(x, l, wt, name):
    h = rms_fwd(x, wt["ffn_norm"][l][None], name=f"{name}_norm")
    up = mm(h, wt["ffn_w_up"][l], name=f"{name}_up")
    act = ffnact_fwd(up, wt["ffn_conv_w"][l], name=f"{name}_act")
    out = mm(act, wt["ffn_w_down"][l], add=x, name=f"{name}_down")
    return out, (x, h, up, act)


def _ffn_bwd(dout, douth, saved, l, wt, name):
    x, h, up, act = saved
    g = {}
    dact = mm(douth, wt["ffn_w_down"][l], tb=True, name=f"{name}_ddown")
    g["ffn_w_down"] = mm(act, douth, ta=True, out_dtype=BF16, name=f"{name}_dwdown")
    dup, g["ffn_conv_w"] = ffnact_bwd(up, wt["ffn_conv_w"][l], dact, name=f"{name}_dact")
    g["ffn_w_up"] = tuple(mm(h, d, ta=True, out_dtype=BF16, name=f"{name}_dwup{kk}") for kk, d in enumerate(dup))
    dh = mm_nt_segments(dup, wt["ffn_w_up"][l], name=f"{name}_dup")
    dx, dxh, dg = rms_bwd(x, wt["ffn_norm"][l][None], dh, add=dout, twin=True, name=f"{name}_dnorm")
    g["ffn_norm"] = dg[0]
    return dx, dxh, g


def _even_fwd(x, i, wt, tabs, name):
    h = rms_fwd(x, wt["attn_norm"][i][None], name=f"{name}_norm")
    proj = mm(h, wt["w_in2"][i], name=f"{name}_in")
    cqn = rms_fwd(proj, wt["cq_norm"][i][None], col=0, name=f"{name}_cqnorm")
    ckvn = rms_fwd(proj, wt["ckv_norm"][i][None], col=1, name=f"{name}_ckvnorm")
    qraw = mm(cqn, wt["w_uq_p"][i], name=f"{name}_uq")
    kv = mm(ckvn, wt["w_ukv_p"][i], name=f"{name}_ukv")
    q, k, v = qkprep_fwd(qraw, kv, proj, wt["q_gain_p"][i], wt["k_gain_p"][i], tabs, name=f"{name}_qkprep")
    o, oh, lset = attn_fwd(q, k, v, name=f"{name}_attn")
    conv = sconv_fwd(proj, wt["sconv_w"][i], name=f"{name}_sconv")
    t = mm(oh, wt["w_out_a"][i], add=x, name=f"{name}_outa")
    out = mm(conv, wt["w_out_c"][i], add=t, name=f"{name}_outc")
    return out, (x, h, proj, cqn, ckvn, qraw, kv, q, k, v, o, oh, lset, conv)


def _even_bwd(dout, douth, saved, i, wt, tabs, name):
    x, h, proj, cqn, ckvn, qraw, kv, q, k, v, o, oh, lset, conv = saved
    g = {}
    do = mm(douth, wt["w_out_a"][i], tb=True, name=f"{name}_douta")
    dconv = mm(douth, wt["w_out_c"][i], tb=True, name=f"{name}_doutc")
    g["w_out_a"] = mm(oh, douth, ta=True, out_dtype=BF16, name=f"{name}_dwouta")
    g["w_out_c"] = mm(conv, douth, ta=True, out_dtype=BF16, name=f"{name}_dwoutc")
    dgates, g["sconv_w"] = sconv_bwd(proj, wt["sconv_w"][i], dconv, name=f"{name}_dsconv")
    doh, deltat = attn_delta(o, do, name=f"{name}_dattn_delta")
    dq, dk, dv = attn_bwd(q, k, v, doh, lset, deltat, name=f"{name}_dattn")
    dqraw, dkraw, dkrope, dqg, dkg = qkprep_bwd(qraw, kv, proj, wt["q_gain_p"][i], wt["k_gain_p"][i], tabs, dq, dk,
                                                name=f"{name}_dqkprep")
    g["q_gain"], g["k_gain"] = dqg[0, :QK], dkg[0, :QK]
    dcqn = mm(dqraw, wt["w_uq_p"][i], tb=True, name=f"{name}_duq")
    g["w_uq_p"] = mm(cqn, dqraw, ta=True, out_dtype=BF16, name=f"{name}_dwuq")
    dkv = (dkraw, dv)
    dckvn = mm_nt_segments(dkv, wt["w_ukv_p"][i], name=f"{name}_dukv")
    g["w_ukv_p"] = tuple(mm(ckvn, d, ta=True, out_dtype=BF16, name=f"{name}_dwukv{kk}") for kk, d in enumerate(dkv))
    dcq, dgq = rms_bwd(proj, wt["cq_norm"][i][None], dcqn, col=0, out_dtype=BF16, name=f"{name}_dcqnorm")
    dckv, dgkv = rms_bwd(proj, wt["ckv_norm"][i][None], dckvn, col=1, out_dtype=BF16, name=f"{name}_dckvnorm")
    g["cq_norm"], g["ckv_norm"] = dgq[0], dgkv[0]
    dproj = (dcq, dckv, *dgates, dkrope)
    g["w_in2"] = tuple(mm(h, d, ta=True, out_dtype=BF16, name=f"{name}_dwin{kk}") for kk, d in enumerate(dproj))
    dh = mm_nt_segments(dproj, wt["w_in2"][i], name=f"{name}_din")
    dx, dxh, dg = rms_bwd(x, wt["attn_norm"][i][None], dh, add=dout, twin=True, name=f"{name}_dnorm")
    g["attn_norm"] = dg[0]
    return dx, dxh, g


def _odd_fwd(x, i, wt, name):
    h = rms_fwd(x, wt["ssm_norm"][i][None], name=f"{name}_norm")
    u = mm(h, wt["ssm_w_in"][i], name=f"{name}_in")
    y, st_r, st_i = s5_fwd(u, wt["bbd_r"][i], wt["bbd_i"][i], wt["cbd_r"][i], wt["cbd_i"][i], wt["tab_f"][i],
                           name=f"{name}_scan")
    gl = s5post_fwd(y, u, wt["d_skip"][i][None], name=f"{name}_gelu")
    glu = mm(gl, wt["w_glu"][i], name=f"{name}_glu")
    out = glu_fwd(glu, x, name=f"{name}_gate")
    return out, (x, h, u, y, st_r, st_i, gl, glu)


def _odd_bwd(dout, douth, saved, i, wt, name):
    x, h, u, y, st_r, st_i, gl, glu = saved
    g = {}
    dglu = glu_bwd(glu, dout, name=f"{name}_dgate")
    g["w_glu"] = mm(gl, dglu, ta=True, out_dtype=BF16, name=f"{name}_dwglu")
    dgl = mm(dglu, wt["w_glu"][i], tb=True, name=f"{name}_dglu")
    dz, dd = s5post_bwd(y, u, wt["d_skip"][i][None], dgl, name=f"{name}_dgelu")
    g["d_skip"] = dd[0]
    du, g["bbd_r"], g["bbd_i"], g["cbd_r"], g["cbd_i"], g["a"] = s5_bwd(
        u, dz, wt["d_skip"][i][None], st_r, st_i, wt["bbd_r"][i], wt["bbd_i"][i], wt["cbd_r"][i], wt["cbd_i"][i],
        wt["tab_r"][i], name=f"{name}_dscan")
    g["ssm_w_in"] = mm(h, du, ta=True, out_dtype=BF16, name=f"{name}_dwin")
    dh = mm(du, wt["ssm_w_in"][i], tb=True, name=f"{name}_din")
    dx, dxh, dg = rms_bwd(x, wt["ssm_norm"][i][None], dh, add=dout, twin=True, name=f"{name}_dnorm")
    g["ssm_norm"] = dg[0]
    return dx, dxh, g


MATMUL_WEIGHTS = {"even": ("mix_w_in", "w_uq", "w_ukv", "mix_w_out"), "odd": ("ssm_w_in", "w_glu"),
                  "ffn": ("ffn_w_up", "ffn_w_down")}
ODD_SMALL = ("lambda_re", "lambda_im", "log_step", "b_re", "b_im", "c_re", "c_im", "ssm_norm", "d_skip")


def _even_layouts(fw, wt, i):
    wt["w_in2"][i] = _w_in_layout(fw["mix_w_in"])
    wt["w_uq_p"][i] = _pad_heads_cols(fw["w_uq"], QK)
    ukv = fw["w_ukv"].reshape(KVR, HEADS, NOPE + VD)
    wt["w_ukv_p"][i] = jnp.concatenate(
        [_pad_heads_cols(ukv[:, :, :NOPE].reshape(KVR, HEADS * NOPE), NOPE),
         _pad_heads_cols(ukv[:, :, NOPE:].reshape(KVR, HEADS * VD), VD)], axis=1)
    wt["w_out_a"][i] = _pad_heads_cols(fw["mix_w_out"][:HEADS * VD].T, VD).T
    wt["w_out_c"][i] = fw["mix_w_out"][HEADS * VD:]


def _even_layouts_t(g):
    dk_, dv_ = g["w_ukv_p"]
    dcq, dckv, dgb, dgc, dci, dkr = g["w_in2"]
    return {"mix_w_in": jnp.concatenate([dcq, dckv, dkr[:, NOPE:QK], dgb, dgc, dci], axis=1),
            "w_uq": _unpad_heads_cols(g["w_uq_p"], QK),
            "w_ukv": jnp.concatenate([dk_.reshape(KVR, HEADS, HP)[:, :, :NOPE], dv_.reshape(KVR, HEADS, HP)[:, :, :VD]],
                                     axis=2).reshape(KVR, HEADS * (NOPE + VD)),
            "mix_w_out": jnp.concatenate([_unpad_heads_cols(g["w_out_a"].T, VD).T, g["w_out_c"]], axis=0)}


def _local_step(x, target, full, getw, putg):
    s = x.shape[0]
    n_even = (DEPTH + 1) // 2
    n_odd = DEPTH // 2
    tabs = _rope_tables(s)
    wt = dict(full)
    for key in ("w_in2", "w_uq_p", "w_ukv_p", "w_out_a", "w_out_c") + sum(MATMUL_WEIGHTS.values(), ()):
        wt[key] = {}
    wt["q_gain_p"] = jnp.pad(full["q_gain"], ((0, 0), (0, HP - QK)))[:, None, :]
    wt["k_gain_p"] = jnp.pad(full["k_gain"], ((0, 0), (0, HP - QK)))[:, None, :]

    disc_vjp = []
    for key in ("bbd_r", "bbd_i", "cbd_r", "cbd_i", "tab_f", "tab_r"):
        wt[key] = []
    for i in range(n_odd):
        (ar, ai, bbr, bbi), vjp = jax.vjp(_discretize, full["lambda_re"][i], full["lambda_im"][i], full["log_step"][i],
                                          full["b_re"][i], full["b_im"][i])
        disc_vjp.append(vjp)
        tf, tr = _scan_tables(ar.reshape(-1), ai.reshape(-1))
        wt["tab_f"].append(tf)
        wt["tab_r"].append(tr)
        wt["bbd_r"].append(_b_blockdiag(bbr).astype(BF16))
        wt["bbd_i"].append(_b_blockdiag(bbi).astype(BF16))
        wt["cbd_r"].append(_c_blockdiag(full["c_re"][i]).astype(BF16))
        wt["cbd_i"].append(_c_blockdiag(full["c_im"][i]).astype(BF16))

    saved = []
    for layer in range(DEPTH):
        i = layer // 2
        if layer % 2 == 0:
            fw, tok = getw("even", i, x)
            wt["attn_norm"] = full["attn_norm"] + tok
            _even_layouts(fw, wt, i)
            x, sm = _even_fwd(x, i, wt, tabs, f"l{layer}_mla")
        else:
            fw, tok = getw("odd", i, x)
            wt["ssm_norm"] = full["ssm_norm"] + tok
            for n, a in fw.items():
                wt[n][i] = a
            x, sm = _odd_fwd(x, i, wt, f"l{layer}_s5")
        fw, tok = getw("ffn", layer, x)
        wt["ffn_norm"] = full["ffn_norm"] + tok
        for n, a in fw.items():
            wt[n][layer] = a
        x, sf = _ffn_fwd(x, layer, wt, f"l{layer}_ffn")
        saved.append((sm, sf))
    dx, dxh, lslab = loss_head(x, target, name="loss_head")

    own = [n for n in ORDER if n not in sum(MATMUL_WEIGHTS.values(), ())]
    grads = {n: [None] * (DEPTH if n.startswith("ffn") else n_even) for n in own}
    tok = 0.0
    for layer in reversed(range(DEPTH)):
        i = layer // 2
        sm, sf = saved[layer]
        wt["ffn_conv_w"] = full["ffn_conv_w"] + tok
        dx, dxh, g = _ffn_bwd(dx, dxh, sf, layer, wt, f"l{layer}_ffn")
        tok = putg("ffn", layer, {n: g[n] for n in MATMUL_WEIGHTS["ffn"]})
        for n in ("ffn_norm", "ffn_conv_w"):
            grads[n][layer] = g[n]
        if layer % 2 == 0:
            wt["sconv_w"] = full["sconv_w"] + tok
            dx, dxh, g = _even_bwd(dx, dxh, sm, i, wt, tabs, f"l{layer}_mla")
            tok = putg("even", i, _even_layouts_t(g))
            for n in ("attn_norm", "cq_norm", "ckv_norm", "q_gain", "k_gain", "sconv_w"):
                grads[n][i] = g[n]
        else:
            wt["d_skip"] = full["d_skip"] + tok
            dx, dxh, g = _odd_bwd(dx, dxh, sm, i, wt, f"l{layer}_s5")
            tok = putg("odd", i, {n: g[n] for n in MATMUL_WEIGHTS["odd"]})
            dlr, dli, dls, dbr, dbi = disc_vjp[i]((g["a"][0].reshape(G, P), g["a"][1].reshape(G, P),
                                                    _b_blockdiag_t(g["bbd_r"]), _b_blockdiag_t(g["bbd_i"])))
            grads["lambda_re"][i], grads["lambda_im"][i], grads["log_step"][i] = dlr, dli, dls
            grads["b_re"][i], grads["b_im"][i] = dbr, dbi
            grads["c_re"][i], grads["c_im"][i] = _c_blockdiag_t(g["cbd_r"]), _c_blockdiag_t(g["cbd_i"])
            for n in ("ssm_norm", "d_skip"):
                grads[n][i] = g[n]
            if i == 0:
                tok = tok + putg("odd_small", 0, {n: jnp.stack(grads[n]) for n in ODD_SMALL})
    grads = {n: jnp.stack(v) for n, v in grads.items()}
    return jnp.sum(lslab), dx, grads


def kernel(x, attn_norm, mix_w_in, cq_norm, ckv_norm, w_uq, w_ukv, q_gain, k_gain, sconv_w, mix_w_out, ssm_norm, ssm_w_in, lambda_re, lambda_im, log_step, b_re, b_im, c_re, c_im, d_skip, w_glu, ffn_norm, ffn_w_up, ffn_conv_w, ffn_w_down, loss_target, m_attn_norm, m_mix_w_in, m_cq_norm, m_ckv_norm, m_w_uq, m_w_ukv, m_q_gain, m_k_gain, m_sconv_w, m_mix_w_out, m_ssm_norm, m_ssm_w_in, m_lambda_re, m_lambda_im, m_log_step, m_b_re, m_b_im, m_c_re, m_c_im, m_d_skip, m_w_glu, m_ffn_norm, m_ffn_w_up, m_ffn_conv_w, m_ffn_w_down, v_attn_norm, v_mix_w_in, v_cq_norm, v_ckv_norm, v_w_uq, v_w_ukv, v_q_gain, v_k_gain, v_sconv_w, v_mix_w_out, v_ssm_norm, v_ssm_w_in, v_lambda_re, v_lambda_im, v_log_step, v_b_re, v_b_im, v_c_re, v_c_im, v_d_skip, v_w_glu, v_ffn_norm, v_ffn_w_up, v_ffn_conv_w, v_ffn_w_down):
    args = dict(locals())
    w = {n: args[n] for n in ORDER}
    m = {n: args["m_" + n] for n in ORDER}
    v = {n: args["v_" + n] for n in ORDER}
    me = 2 * lax.axis_index("x") + lax.axis_index("y")

    axis = dict(BIG)

    gs = _chip_exchange([w[n] for n, _ in SMALL], False, "gather_w_f32")
    full = {n: w[n] for n in REPL}
    for (n, ax), g in zip(SMALL, gs):
        full[n] = _join(g, ax)
    parts = [(("even", "odd")[layer % 2], layer // 2) for layer in range(DEPTH)]
    parts = [p for layer, mixer in enumerate(parts) for p in (mixer, ("ffn", layer))]
    gathers = {}

    def start_gather(kind, idx, zero):
        shards = [(w[n][idx] + zero).astype(BF16) for n in MATMUL_WEIGHTS[kind]]
        gathers[kind, idx] = exchange_start(shards, False, f"gather_start_{kind}{idx}")

    start_gather(*parts[0], 0.0 * gs[0][(0,) * gs[0].ndim])

    def getw(kind, idx, after):
        got = exchange_wait(gathers[kind, idx], after, False, f"gather_wait_{kind}{idx}")
        nxt = parts.index((kind, idx)) + 1
        tok = 0.0
        if nxt < len(parts):
            start_gather(*parts[nxt], 0.0 * got[0][(0,) * got[0].ndim].astype(F32))
            tok = gathers[parts[nxt]][3]
        return {n: _join(g, axis[n] - 1) for n, g in zip(MATMUL_WEIGHTS[kind], got)}, tok

    scatters, early = [], []

    def putg(kind, idx, g):
        if kind == "odd_small":
            arrs = [_rows2d(g[n]) for n in ODD_SMALL]
            early.append(exchange_start(arrs, False, "gather_start_g_odd"))
            return early[0][3]
        blocks = [jnp.concatenate([_split(part, axis[n] - 1, NCHIP // len(g[n])) for part in g[n]])
                  if isinstance(g[n], tuple) else _split(g[n], axis[n] - 1) for n in MATMUL_WEIGHTS[kind]]
        tok = 0.0
        if scatters:
            tok = reduce_part(*scatters.pop(), blocks[0])
        scatters.append((kind, idx, exchange_start(blocks, True, f"scatter_start_{kind}{idx}")))
        return tok + scatters[-1][2][3]

    swaps = []

    def reduce_part(kind, idx, started, after):
        got = exchange_wait(started, after, True, f"scatter_wait_{kind}{idx}")
        sums = [sum_slots(_rows2d(sl, 1), name=f"sum_{n}{idx}") for n, sl in zip(MATMUL_WEIGHTS[kind], got)]
        swaps.append((kind, idx, sums, swap_start(sums, f"swap_start_{kind}{idx}")))
        return swaps[-1][3][3]

    sq, dx, grads = _local_step(x[0], loss_target[0], full, getw, putg)
    loss = lax.psum(0.5 * sq / D_MODEL, ("x", "y", "c"))

    late_names = [n for n in REPL + [n for n, _ in SMALL] if n not in ODD_SMALL]
    rep_names = list(ODD_SMALL) + late_names
    names = [n for n, _ in BIG] + rep_names
    late = [_rows2d(grads[n]) for n in late_names]
    late_started = exchange_start(late, False, "gather_start_g_f32")
    reduce_part(*scatters.pop(), dx)
    slots = (exchange_wait(early[0], dx, False, "gather_wait_g_odd")
             + exchange_wait(late_started, swaps[-1][2][0], False, "gather_wait_g_f32"))
    small_mine = [sum_slots(_rows2d(sl, 1), name=f"sum_{n}") for n, sl in zip(rep_names, slots)]
    small_other = list(sibling_swap(small_mine, "swap_g_small"))
    summed, swapped = {}, {}
    for kind, idx, sums, started in swaps:
        for n, p, q in zip(MATMUL_WEIGHTS[kind], sums, swap_wait(started, small_other[0], f"swap_wait_{kind}{idx}")):
            summed[n, idx], swapped[n, idx] = p, q
    mine = [jnp.concatenate([summed[n, idx] for idx in range(w[n].shape[0])], axis=0) for n, _ in BIG] + small_mine
    other = [jnp.concatenate([swapped[n, idx] for idx in range(w[n].shape[0])], axis=0) for n, _ in BIG] + small_other

    def local(n, p):
        ax = dict(SMALL).get(n)
        if ax is None:
            return p
        part = lax.dynamic_index_in_dim(_split(p.reshape(grads[n].shape), ax), me, 0, keepdims=False)
        return _rows2d(part)

    outs = {}
    for n, p, q in zip(names, mine, other):
        res = adamw(local(n, p), local(n, q), _rows2d(w[n]), _rows2d(m[n]), _rows2d(v[n]), name=f"adamw_{n}")
        outs[n] = [r.reshape(w[n].shape) for r in res]
    return (loss, dx[None], *[outs[n][0] for n in ORDER], *[outs[n][1] for n in ORDER],
            *[outs[n][2] for n in ORDER], *[outs[n][3] for n in ORDER])
```

```python
import functools
import math

import numpy as np
import jax
import jax.numpy as jnp
from jax import lax
from jax.experimental import pallas as pl
from jax.experimental.pallas import tpu as pltpu

F32, BF16 = jnp.float32, jnp.bfloat16

D_MODEL = 1024
DEPTH = 4
HEADS = 8
NOPE, ROPE, QK, VD = 64, 32, 96, 64
HP = 128
QR, KVR = 256, 256
CONVC = 512
FFN_H = 2816
G, P, GC = 64, 64, 16
NST = G * P
SLAB = 8
LANE = 128
EPS = 1e-6
ROPE_THETA = 10000.0
ADAM_LR, ADAM_B1, ADAM_B2, ADAM_EPS, ADAM_WD, ADAM_STEP = 0.001, 0.9, 0.999, 1e-08, 0.01, 10
VMEM_LIMIT = 48 * 1024 * 1024
MESH = pl.DeviceIdType.MESH
ANY = pl.BlockSpec(memory_space=pl.ANY)


def _cparams(*sem):
    return pltpu.CompilerParams(dimension_semantics=sem, vmem_limit_bytes=VMEM_LIMIT)


def _pick(dim, prefs):
    for p in prefs:
        if dim % p == 0:
            return p
    return dim


def _rows(s):
    return _pick(s, (512, 256, 128, 64, 32, 16, 8))


def _long_rows(s):
    return _pick(s, (1024, 512, 256, 128, 64, 32, 16, 8))


MM_VMEM_BUDGET = 36 * 1024 * 1024
MM_MAX_TILE_ELEMS = 640 * 1024
HBM_BYTES_PER_US = 3.0e6
STEP_OVERHEAD_US = 0.35


def _lane_tiles(n):
    c = {t for t in range(LANE, min(n, 1536) + 1, LANE) if n % t == 0}
    if n <= 2304 or not c:
        c.add(n)
    return sorted(c, reverse=True)


def _mm_tiles(m, n, k, sa, sb, so):
    best = None
    for tm in [t for t in (1024, 512, 256) if m % t == 0] or [m]:
        for tn in _lane_tiles(n):
            if tm * tn > MM_MAX_TILE_ELEMS:
                continue
            if 2 * (tm * k * sa + k * tn * sb + tm * tn * so) + 4 * tm * tn > MM_VMEM_BUDGET:
                continue
            steps = (m // tm) * (n // tn)
            for inner_n in (True, False):
                moved = (m * k * sa + (m // tm) * k * n * sb) if inner_n else (k * n * sb + (n // tn) * m * k * sa)
                cost = (moved + m * n * so) / HBM_BYTES_PER_US + steps * STEP_OVERHEAD_US
                if best is None or cost < best[0]:
                    best = (cost, tm, tn, inner_n)
    return best[1:]


def mm(a, b, *, ta=False, tb=False, add=None, out_dtype=F32, name):
    if ta:
        kdim, m = a.shape
    else:
        m, kdim = a.shape
    n = b.shape[0] if tb else b.shape[1]
    so = jnp.dtype(out_dtype).itemsize + (0 if add is None else add.dtype.itemsize)
    tm, tn, inner_n = _mm_tiles(m, n, kdim, a.dtype.itemsize, b.dtype.itemsize, so)
    dn = (((0 if ta else 1,), (1 if tb else 0,)), ((), ()))

    def body(*refs):
        if add is None:
            a_ref, b_ref, o_ref = refs
        else:
            a_ref, b_ref, add_ref, o_ref = refs
        r = lax.dot_general(a_ref[...].astype(BF16), b_ref[...].astype(BF16), dn, preferred_element_type=F32)
        if add is not None:
            r = r + add_ref[...].astype(F32)
        o_ref[...] = r.astype(out_dtype)

    ij = (lambda g0, g1: (g0, g1)) if inner_n else (lambda g0, g1: (g1, g0))
    a_spec = (pl.BlockSpec((kdim, tm), lambda g0, g1: (0, ij(g0, g1)[0])) if ta
              else pl.BlockSpec((tm, kdim), lambda g0, g1: (ij(g0, g1)[0], 0)))
    b_spec = (pl.BlockSpec((tn, kdim), lambda g0, g1: (ij(g0, g1)[1], 0)) if tb
              else pl.BlockSpec((kdim, tn), lambda g0, g1: (0, ij(g0, g1)[1])))
    o_spec = pl.BlockSpec((tm, tn), lambda g0, g1: ij(g0, g1))
    ins, specs = [a, b], [a_spec, b_spec]
    if add is not None:
        ins.append(add)
        specs.append(o_spec)
    grid = (m // tm, n // tn) if inner_n else (n // tn, m // tm)
    return pl.pallas_call(
        body, name=name, grid=grid, in_specs=specs, out_specs=o_spec,
        out_shape=jax.ShapeDtypeStruct((m, n), out_dtype),
        compiler_params=_cparams("parallel", "parallel"))(*ins)


def mm_nt_segments(segs, b, *, name):
    m = segs[0].shape[0]
    n, kdim = b.shape
    widths = [sg.shape[1] for sg in segs]
    offs = [sum(widths[:i]) for i in range(len(segs))]
    assert sum(widths) == kdim and all(o % wd == 0 for o, wd in zip(offs, widths))
    sa = max(sg.dtype.itemsize for sg in segs)
    tm, tn, inner_n = _mm_tiles(m, n, kdim, sa, b.dtype.itemsize, 4)
    ns = len(segs)

    def body(*refs):
        o_ref = refs[-1]
        r = None
        for a_ref, b_ref in zip(refs[:ns], refs[ns:2 * ns]):
            d = lax.dot_general(a_ref[...].astype(BF16), b_ref[...].astype(BF16), NT, preferred_element_type=F32)
            r = d if r is None else r + d
        o_ref[...] = r

    ij = (lambda g0, g1: (g0, g1)) if inner_n else (lambda g0, g1: (g1, g0))
    a_specs = [pl.BlockSpec((tm, wd), lambda g0, g1: (ij(g0, g1)[0], 0)) for wd in widths]
    b_specs = [pl.BlockSpec((tn, wd), lambda g0, g1, blk=o // wd: (ij(g0, g1)[1], blk)) for o, wd in zip(offs, widths)]
    grid = (m // tm, n // tn) if inner_n else (n // tn, m // tm)
    return pl.pallas_call(
        body, name=name, grid=grid, in_specs=a_specs + b_specs,
        out_specs=pl.BlockSpec((tm, tn), lambda g0, g1: ij(g0, g1)),
        out_shape=jax.ShapeDtypeStruct((m, n), F32),
        compiler_params=_cparams("parallel", "parallel"))(*segs, *([b] * ns))


def rms_fwd(x, g, *, col=0, out_dtype=BF16, name):
    s = x.shape[0]
    d = g.shape[1]
    tm = _rows(s)

    def body(x_ref, g_ref, o_ref):
        xv = x_ref[...]
        r = lax.rsqrt(jnp.mean(xv * xv, axis=-1, keepdims=True) + EPS)
        o_ref[...] = (xv * r * g_ref[...]).astype(out_dtype)

    return pl.pallas_call(
        body, name=name, grid=(s // tm,),
        in_specs=[pl.BlockSpec((tm, d), lambda i: (i, col)), pl.BlockSpec((1, d), lambda i: (0, 0))],
        out_specs=pl.BlockSpec((tm, d), lambda i: (i, 0)),
        out_shape=jax.ShapeDtypeStruct((s, d), out_dtype),
        compiler_params=_cparams("parallel"))(x, g)


def rms_bwd(x, g, dy, *, col=0, add=None, out_dtype=F32, twin=False, name):
    s = x.shape[0]
    d = g.shape[1]
    tm = _rows(s)

    def body(*refs):
        refs = list(refs)
        dg_ref = refs.pop()
        dxh_ref = refs.pop() if twin else None
        dx_ref = refs.pop()
        add_ref = refs.pop() if add is not None else None
        x_ref, g_ref, dy_ref = refs

        @pl.when(pl.program_id(0) == 0)
        def _():
            dg_ref[...] = jnp.zeros_like(dg_ref)

        xv = x_ref[...]
        dyv = dy_ref[...].astype(F32)
        r = lax.rsqrt(jnp.mean(xv * xv, axis=-1, keepdims=True) + EPS)
        xh = xv * r
        dg_ref[...] += jnp.sum(dyv * xh, axis=0, keepdims=True)
        dxh = dyv * g_ref[...]
        dx = r * (dxh - xh * jnp.mean(dxh * xh, axis=-1, keepdims=True))
        if add is not None:
            dx = dx + add_ref[...]
        dx_ref[...] = dx.astype(out_dtype)
        if twin:
            dxh_ref[...] = dx.astype(BF16)

    row = pl.BlockSpec((tm, d), lambda i: (i, 0))
    vec = pl.BlockSpec((1, d), lambda i: (0, 0))
    ins = [x, g, dy]
    specs = [pl.BlockSpec((tm, d), lambda i: (i, col)), vec, row]
    if add is not None:
        ins.append(add)
        specs.append(row)
    dxs = [jax.ShapeDtypeStruct((s, d), out_dtype)] + ([jax.ShapeDtypeStruct((s, d), BF16)] if twin else [])
    return pl.pallas_call(
        body, name=name, grid=(s // tm,), in_specs=specs,
        out_specs=[row] * len(dxs) + [vec],
        out_shape=dxs + [jax.ShapeDtypeStruct((1, d), F32)],
        compiler_params=_cparams("arbitrary"))(*ins)


def _rope_tables(s):
    inv = 1.0 / (ROPE_THETA ** (jnp.arange(0, ROPE, 2, dtype=F32) / ROPE))
    ang = jnp.arange(s, dtype=F32)[:, None] * inv[None, :]
    cos, sin = jnp.cos(ang), jnp.sin(ang)
    z = lambda w: jnp.zeros((s, w), F32)
    c = jnp.concatenate([jnp.ones((s, NOPE), F32), cos, cos, z(HP - QK)], axis=1)
    s1 = jnp.concatenate([z(NOPE), -sin, z(HP - NOPE - ROPE // 2)], axis=1)
    s2 = jnp.concatenate([z(NOPE + ROPE // 2), sin, z(HP - QK)], axis=1)
    return c, s1, s2


def qkprep_fwd(qraw, kv, proj, qg, kg, tabs, *, name):
    s = qraw.shape[0]
    tm = _long_rows(s)
    kr_col = (proj.shape[1] - HP) // HP

    def body(q_ref, k_ref, v_ref, kr_ref, qg_ref, kg_ref, c_ref, s1_ref, s2_ref, qo_ref, ko_ref, vo_ref):
        c, s1, s2 = c_ref[...], s1_ref[...], s2_ref[...]

        def f(xv, gain):
            r = lax.rsqrt(jnp.sum(xv * xv, axis=-1, keepdims=True) * (1.0 / QK) + EPS)
            xn = xv * r * gain
            return xn * c + pltpu.roll(xn, HP - ROPE // 2, 1) * s1 + pltpu.roll(xn, ROPE // 2, 1) * s2

        qo_ref[...] = (f(q_ref[...], qg_ref[...]) * Q_SCALE).astype(BF16)
        ko_ref[...] = f(k_ref[...] + kr_ref[...], kg_ref[...]).astype(BF16)
        vv = v_ref[...]
        lane = lax.broadcasted_iota(jnp.int32, vv.shape, 1)
        vo_ref[...] = jnp.where(lane == VD, 1.0, vv).astype(BF16)

    head = pl.BlockSpec((tm, HP), lambda i, h: (i, h))
    tab = pl.BlockSpec((tm, HP), lambda i, h: (i, 0))
    gain = pl.BlockSpec((1, HP), lambda i, h: (0, 0))
    return pl.pallas_call(
        body, name=name, grid=(s // tm, HEADS),
        in_specs=[head, head, pl.BlockSpec((tm, HP), lambda i, h: (i, HEADS + h)),
                  pl.BlockSpec((tm, HP), lambda i, h: (i, kr_col)), gain, gain, tab, tab, tab],
        out_specs=[head, head, head],
        out_shape=[jax.ShapeDtypeStruct((s, HEADS * HP), BF16)] * 3,
        compiler_params=_cparams("parallel", "parallel"))(qraw, kv, kv, proj, qg, kg, *tabs)


def qkprep_bwd(qraw, kv, proj, qg, kg, tabs, dq, dk, *, name):
    s = qraw.shape[0]
    tm = _long_rows(s)
    kr_col = (proj.shape[1] - HP) // HP

    def body(q_ref, k_ref, kr_ref, qg_ref, kg_ref, c_ref, s1_ref, s2_ref, dq_ref, dk_ref,
             dqr_ref, dkr_ref, dkrope_ref, dqg_ref, dkg_ref):
        i, h = pl.program_id(0), pl.program_id(1)
        c, s1, s2 = c_ref[...], s1_ref[...], s2_ref[...]

        @pl.when((i == 0) & (h == 0))
        def _():
            dqg_ref[...] = jnp.zeros_like(dqg_ref)
            dkg_ref[...] = jnp.zeros_like(dkg_ref)

        @pl.when(h == 0)
        def _():
            dkrope_ref[...] = jnp.zeros_like(dkrope_ref)

        def f(xv, gain, dout):
            r = lax.rsqrt(jnp.sum(xv * xv, axis=-1, keepdims=True) * (1.0 / QK) + EPS)
            xh = xv * r
            dxn = dout * c + pltpu.roll(dout * s1, ROPE // 2, 1) + pltpu.roll(dout * s2, HP - ROPE // 2, 1)
            dgain = jnp.sum(dxn * xh, axis=0, keepdims=True)
            dxh = dxn * gain
            dx = r * (dxh - xh * (jnp.sum(dxh * xh, axis=-1, keepdims=True) * (1.0 / QK)))
            return dx, dgain

        dxq, dgq = f(q_ref[...], qg_ref[...], dq_ref[...])
        dxk, dgk = f(k_ref[...] + kr_ref[...], kg_ref[...], dk_ref[...])
        dqr_ref[...] = dxq.astype(BF16)
        dkr_ref[...] = dxk.astype(BF16)
        dqg_ref[...] += dgq
        dkg_ref[...] += dgk
        lane = lax.broadcasted_iota(jnp.int32, dxk.shape, 1)
        dkrope_ref[...] += jnp.where((lane >= NOPE) & (lane < QK), dxk, 0.0)

    head = pl.BlockSpec((tm, HP), lambda i, h: (i, h))
    tab = pl.BlockSpec((tm, HP), lambda i, h: (i, 0))
    gain = pl.BlockSpec((1, HP), lambda i, h: (0, 0))
    return pl.pallas_call(
        body, name=name, grid=(s // tm, HEADS),
        in_specs=[head, head, pl.BlockSpec((tm, HP), lambda i, h: (i, kr_col)), gain, gain, tab, tab, tab, head, head],
        out_specs=[head, head, tab, gain, gain],
        out_shape=[jax.ShapeDtypeStruct((s, HEADS * HP), BF16)] * 2
        + [jax.ShapeDtypeStruct((s, HP), F32), jax.ShapeDtypeStruct((1, HP), F32), jax.ShapeDtypeStruct((1, HP), F32)],
        compiler_params=_cparams("arbitrary", "arbitrary"))(qraw, kv, proj, qg, kg, *tabs, dq, dk)


ATT_SCALE = QK ** -0.5
LOG2E = math.log2(math.e)
Q_SCALE = ATT_SCALE * LOG2E
NEG = -1e30


def _att_tile(s):
    return _pick(s, (512, 256, 128))


def _causal(sv, diag):
    r = lax.broadcasted_iota(jnp.int32, sv.shape, 0)
    c = lax.broadcasted_iota(jnp.int32, sv.shape, 1)
    return jnp.where(diag & (c > r), NEG, sv)


NT = (((1,), (1,)), ((), ()))
TN = (((0,), (0,)), ((), ()))


def _row_of(col):
    return jnp.broadcast_to(col, (col.shape[0], LANE)).T[0:SLAB, :]


def _att_specs(s):
    t = _att_tile(s)
    nb = s // t
    tile = pl.BlockSpec((t, HP), lambda h, i: (i, h))
    whole = pl.BlockSpec((s, HP), lambda h, i: (0, h))
    row = pl.BlockSpec((1, 1, SLAB, t), lambda h, i: (h, i, 0, 0))
    rows = pl.BlockSpec((1, nb, SLAB, t), lambda h, i: (h, 0, 0, 0))
    return t, nb, tile, whole, row, rows


def attn_fwd(q, k, v, *, name):
    s = q.shape[0]
    t, nb, tile, whole, row, _ = _att_specs(s)

    def body(q_ref, k_ref, v_ref, o_ref, oh_ref, lset_ref, s_scr, mb_scr, acc):
        qb = pl.program_id(1)
        qv = q_ref[...]

        def scores(j):
            r0 = pl.multiple_of(j * t, t)
            return lax.dot_general(qv, k_ref[pl.ds(r0, t), :], NT, preferred_element_type=F32)

        def fold(sv):
            m = sv[:, 0:LANE]
            for kk in range(1, t // LANE):
                m = jnp.maximum(m, sv[:, kk * LANE:(kk + 1) * LANE])
            return m

        def first(j, m):
            sv = scores(j)
            s_scr[j] = sv
            return jnp.maximum(m, fold(sv))

        m = lax.fori_loop(0, qb, first, jnp.full((t, LANE), NEG, F32))
        sd = _causal(scores(qb), True)
        s_scr[qb] = sd
        mcol = jnp.max(jnp.maximum(m, fold(sd)), axis=-1, keepdims=True)
        mb_scr[...] = jnp.broadcast_to(mcol, (t, t))
        acc[...] = jnp.zeros_like(acc)

        def second(j, carry):
            r0 = pl.multiple_of(j * t, t)
            p = jnp.exp2(s_scr[j] - mb_scr[...]).astype(BF16)
            acc[...] += jnp.dot(p, v_ref[pl.ds(r0, t), :], preferred_element_type=F32)
            return carry

        lax.fori_loop(0, qb + 1, second, 0)
        av = acc[...]
        lsum = av[:, VD:VD + 1]
        lane = lax.broadcasted_iota(jnp.int32, av.shape, 1)
        ov = jnp.where(lane == VD, 0.0, av / lsum)
        o_ref[...] = ov
        oh_ref[...] = ov.astype(BF16)
        lset_ref[0, 0] = _row_of(mcol + jnp.log2(lsum))

    return pl.pallas_call(
        body, name=name, grid=(HEADS, nb), in_specs=[tile, whole, whole], out_specs=[tile, tile, row],
        out_shape=[jax.ShapeDtypeStruct((s, HEADS * HP), F32), jax.ShapeDtypeStruct((s, HEADS * HP), BF16),
                   jax.ShapeDtypeStruct((HEADS, nb, SLAB, t), F32)],
        scratch_shapes=[pltpu.VMEM((nb, t, t), F32), pltpu.VMEM((t, t), F32), pltpu.VMEM((t, HP), F32)],
        compiler_params=_cparams("parallel", "parallel"))(q, k, v)


def attn_delta(o, do, *, name):
    s = o.shape[0]
    t, nb, tile, _, row, _ = _att_specs(s)

    def body(o_ref, do_ref, doh_ref, dt_ref):
        dov = do_ref[...]
        doh_ref[...] = dov.astype(BF16)
        dt_ref[0, 0] = _row_of(jnp.sum(dov * o_ref[...], axis=-1, keepdims=True))

    return pl.pallas_call(
        body, name=name, grid=(HEADS, nb), in_specs=[tile, tile], out_specs=[tile, row],
        out_shape=[jax.ShapeDtypeStruct((s, HEADS * HP), BF16), jax.ShapeDtypeStruct((HEADS, nb, SLAB, t), F32)],
        compiler_params=_cparams("parallel", "parallel"))(o, do)


def attn_bwd(q, k, v, doh, lset, deltat, *, name):
    s = q.shape[0]
    t, nb, tile, whole, _, rows = _att_specs(s)

    def body(q_ref, k_ref, v_ref, do_ref, lt_ref, dt_ref, dq_ref, dk_ref, dv_ref, dk_acc, dv_acc):
        kb = pl.program_id(1)
        kt, vt = k_ref[...], v_ref[...]
        dk_acc[...] = jnp.zeros_like(dk_acc)
        dv_acc[...] = jnp.zeros_like(dv_acc)

        @pl.when(kb == 0)
        def _():
            dq_ref[...] = jnp.zeros_like(dq_ref)

        def step(i, diag):
            r0 = pl.multiple_of(i * t, t)
            qi, doi = q_ref[pl.ds(r0, t), :], do_ref[pl.ds(r0, t), :]
            st = lax.dot_general(kt, qi, NT, preferred_element_type=F32)
            if diag:
                kr = lax.broadcasted_iota(jnp.int32, st.shape, 0)
                qc = lax.broadcasted_iota(jnp.int32, st.shape, 1)
                st = jnp.where(kr > qc, NEG, st)
            pt = jnp.exp2(st - lt_ref[0, i][0:1, :])
            dpt = lax.dot_general(vt, doi, NT, preferred_element_type=F32)
            dst = (pt * (dpt - dt_ref[0, i][0:1, :])).astype(BF16)
            dv_acc[...] += jnp.dot(pt.astype(BF16), doi, preferred_element_type=F32)
            dk_acc[...] += jnp.dot(dst, qi, preferred_element_type=F32)
            dq_ref[pl.ds(r0, t), :] += lax.dot_general(dst, kt, TN, preferred_element_type=F32) * ATT_SCALE

        def off_diag(i, carry):
            step(i, False)
            return carry

        step(kb, True)
        lax.fori_loop(kb + 1, nb, off_diag, 0)
        dk_ref[...] = dk_acc[...] * (1.0 / LOG2E)
        dvv = dv_acc[...]
        lane = lax.broadcasted_iota(jnp.int32, dvv.shape, 1)
        dv_ref[...] = jnp.where(lane == VD, 0.0, dvv).astype(BF16)

    return pl.pallas_call(
        body, name=name, grid=(HEADS, nb), in_specs=[whole, tile, tile, whole, rows, rows],
        out_specs=[whole, tile, tile],
        out_shape=[jax.ShapeDtypeStruct((s, HEADS * HP), F32), jax.ShapeDtypeStruct((s, HEADS * HP), F32),
                   jax.ShapeDtypeStruct((s, HEADS * HP), BF16)],
        scratch_shapes=[pltpu.VMEM((t, HP), F32), pltpu.VMEM((t, HP), F32)],
        compiler_params=_cparams("parallel", "arbitrary"))(q, k, v, doh, lset, deltat)


HALO = 8
CW = 256


def _shifts(zw):
    return zw, pltpu.roll(zw, 1, 0), pltpu.roll(zw, 2, 0)


def _conv3(sh, w):
    return w[2:3] * sh[0] + w[1:2] * sh[1] + w[0:1] * sh[2]


def _conv3_t(dc, w):
    n = dc.shape[0]
    return w[2:3] * dc + w[1:2] * pltpu.roll(dc, n - 1, 0) + w[0:1] * pltpu.roll(dc, n - 2, 0)


def _conv3_dw(dc, sh, r):
    return [jnp.sum(dc * z[HALO:HALO + r], axis=0, keepdims=True) for z in (sh[2], sh[1], sh[0])]


def _halo_specs(r, colfn):
    rb = r // HALO
    cur = pl.BlockSpec((r, CW), lambda j, i: (i, colfn(j)))
    prev = pl.BlockSpec((HALO, CW), lambda j, i: (jnp.maximum(i * rb - 1, 0), colfn(j)))

    def nxt(nrow_blocks):
        return pl.BlockSpec((HALO, CW), lambda j, i: (jnp.minimum((i + 1) * rb, nrow_blocks * rb - 1), colfn(j)))

    return cur, prev, nxt


def ffnact_fwd(up, w, *, name):
    s, c2 = up.shape
    hh = c2 // 2
    nj = hh // CW
    r = _long_rows(s)
    nt = s // r

    def body(g_ref, gp_ref, v_ref, vp_ref, wg_ref, wv_ref, o_ref):
        pm = (pl.program_id(1) > 0).astype(F32)
        cg = _conv3(_shifts(jnp.concatenate([gp_ref[...] * pm, g_ref[...]], axis=0)), wg_ref[...])[HALO:]
        cv = _conv3(_shifts(jnp.concatenate([vp_ref[...] * pm, v_ref[...]], axis=0)), wv_ref[...])[HALO:]
        o_ref[...] = (cg * jax.nn.sigmoid(cg) * cv).astype(BF16)

    gcur, gprev, _ = _halo_specs(r, lambda j: j)
    vcur, vprev, _ = _halo_specs(r, lambda j: nj + j)
    wg = pl.BlockSpec((3, CW), lambda j, i: (0, j))
    wv = pl.BlockSpec((3, CW), lambda j, i: (0, nj + j))
    return pl.pallas_call(
        body, name=name, grid=(nj, nt), in_specs=[gcur, gprev, vcur, vprev, wg, wv],
        out_specs=pl.BlockSpec((r, CW), lambda j, i: (i, j)),
        out_shape=jax.ShapeDtypeStruct((s, hh), BF16),
        compiler_params=_cparams("parallel", "parallel"))(up, up, up, up, w, w)


def ffnact_bwd(up, w, dact, *, name):
    s, c2 = up.shape
    hh = c2 // 2
    nj = hh // CW
    r = _rows(s)
    nt = s // r

    def body(g_ref, gp_ref, gn_ref, v_ref, vp_ref, vn_ref, wg_ref, wv_ref, da_ref, dan_ref,
             dg_ref, dv_ref, dwg_ref, dwv_ref):
        i = pl.program_id(1)
        pm = (i > 0).astype(F32)
        nm = (i < nt - 1).astype(F32)

        @pl.when(i == 0)
        def _():
            dwg_ref[...] = jnp.zeros_like(dwg_ref)
            dwv_ref[...] = jnp.zeros_like(dwv_ref)

        wg, wv = wg_ref[...], wv_ref[...]
        zg = jnp.concatenate([gp_ref[...] * pm, g_ref[...], gn_ref[...]], axis=0)
        zv = jnp.concatenate([vp_ref[...] * pm, v_ref[...], vn_ref[...]], axis=0)
        zg, zv = _shifts(zg), _shifts(zv)
        cg = _conv3(zg, wg)[HALO:]
        cv = _conv3(zv, wv)[HALO:]
        da = jnp.concatenate([da_ref[...], dan_ref[...] * nm], axis=0)
        sg = jax.nn.sigmoid(cg)
        dcg = da * cv * (sg * (1.0 + cg * (1.0 - sg)))
        dcv = da * (cg * sg)
        dg_ref[...] = _conv3_t(dcg, wg)[:r].astype(BF16)
        dv_ref[...] = _conv3_t(dcv, wv)[:r].astype(BF16)
        for kk, (a, b) in enumerate(zip(_conv3_dw(dcg[:r], zg, r), _conv3_dw(dcv[:r], zv, r))):
            dwg_ref[kk:kk + 1, :] += a
            dwv_ref[kk:kk + 1, :] += b

    gcur, gprev, gnext = _halo_specs(r, lambda j: j)
    vcur, vprev, vnext = _halo_specs(r, lambda j: nj + j)
    acur, _, anext = _halo_specs(r, lambda j: j)
    wg = pl.BlockSpec((3, CW), lambda j, i: (0, j))
    wv = pl.BlockSpec((3, CW), lambda j, i: (0, nj + j))
    dupg, dupv, dwg, dwv = pl.pallas_call(
        body, name=name, grid=(nj, nt),
        in_specs=[gcur, gprev, gnext(nt), vcur, vprev, vnext(nt), wg, wv, acur, anext(nt)],
        out_specs=[acur, acur, wg, wg],
        out_shape=[jax.ShapeDtypeStruct((s, hh), BF16), jax.ShapeDtypeStruct((s, hh), BF16),
                   jax.ShapeDtypeStruct((3, hh), F32), jax.ShapeDtypeStruct((3, hh), F32)],
        compiler_params=_cparams("parallel", "arbitrary"))(up, up, up, up, up, up, w, w, dact, dact)
    return (dupg, dupv), jnp.concatenate([dwg, dwv], axis=1)


def sconv_fwd(proj, w, *, name):
    s = proj.shape[0]
    nj = CONVC // CW
    r = _long_rows(s)
    nt = s // r

    def body(b_ref, c_ref, cp_ref, x_ref, xp_ref, w_ref, o_ref):
        pm = (pl.program_id(1) > 0).astype(F32)
        zw = jnp.concatenate([cp_ref[...] * xp_ref[...] * pm, c_ref[...] * x_ref[...]], axis=0)
        o_ref[...] = (b_ref[...] * _conv3(_shifts(zw), w_ref[...])[HALO:]).astype(BF16)

    bcur, _, _ = _halo_specs(r, lambda j: (QR + KVR) // CW + j)
    ccur, cprev, _ = _halo_specs(r, lambda j: (QR + KVR + CONVC) // CW + j)
    xcur, xprev, _ = _halo_specs(r, lambda j: (QR + KVR + 2 * CONVC) // CW + j)
    ws = pl.BlockSpec((3, CW), lambda j, i: (0, j))
    return pl.pallas_call(
        body, name=name, grid=(nj, nt), in_specs=[bcur, ccur, cprev, xcur, xprev, ws],
        out_specs=pl.BlockSpec((r, CW), lambda j, i: (i, j)),
        out_shape=jax.ShapeDtypeStruct((s, CONVC), BF16),
        compiler_params=_cparams("parallel", "parallel"))(proj, proj, proj, proj, proj, w)


def sconv_bwd(proj, w, dy, *, name):
    s = proj.shape[0]
    nj = CONVC // CW
    r = _rows(s)
    nt = s // r

    def body(b_ref, bn_ref, c_ref, cp_ref, x_ref, xp_ref, w_ref, dy_ref, dyn_ref, db_ref, dc_ref, dx_ref, dw_ref):
        i = pl.program_id(1)
        pm = (i > 0).astype(F32)
        nm = (i < nt - 1).astype(F32)

        @pl.when(i == 0)
        def _():
            dw_ref[...] = jnp.zeros_like(dw_ref)

        wv = w_ref[...]
        zw = jnp.concatenate([cp_ref[...] * xp_ref[...] * pm, c_ref[...] * x_ref[...]], axis=0)
        zw = _shifts(zw)
        conv = _conv3(zw, wv)[HALO:]
        dyv = dy_ref[...]
        db_ref[...] = (dyv * conv).astype(BF16)
        dconv = jnp.concatenate([dyv * b_ref[...], dyn_ref[...] * bn_ref[...] * nm], axis=0)
        dz = _conv3_t(dconv, wv)[:r]
        dc_ref[...] = (dz * x_ref[...]).astype(BF16)
        dx_ref[...] = (dz * c_ref[...]).astype(BF16)
        for kk, a in enumerate(_conv3_dw(dconv[:r], zw, r)):
            dw_ref[kk:kk + 1, :] += a

    bcur, _, bnext = _halo_specs(r, lambda j: (QR + KVR) // CW + j)
    ccur, cprev, _ = _halo_specs(r, lambda j: (QR + KVR + CONVC) // CW + j)
    xcur, xprev, _ = _halo_specs(r, lambda j: (QR + KVR + 2 * CONVC) // CW + j)
    ycur, _, ynext = _halo_specs(r, lambda j: j)
    ws = pl.BlockSpec((3, CW), lambda j, i: (0, j))
    out = pl.BlockSpec((r, CW), lambda j, i: (i, j))
    db, dc, dx, dw = pl.pallas_call(
        body, name=name, grid=(nj, nt),
        in_specs=[bcur, bnext(nt), ccur, cprev, xcur, xprev, ws, ycur, ynext(nt)],
        out_specs=[out, out, out, ws],
        out_shape=[jax.ShapeDtypeStruct((s, CONVC), BF16)] * 3 + [jax.ShapeDtypeStruct((3, CONVC), F32)],
        compiler_params=_cparams("parallel", "arbitrary"))(proj, proj, proj, proj, proj, proj, w, dy, dy)
    return (db, dc, dx), dw


SW = 512
NJ = NST // SW


def _scan_tables(ar, ai):
    def cmul(x, y):
        return x[0] * y[0] - x[1] * y[1], x[0] * y[1] + x[1] * y[0]

    def build(a, reverse):
        pw = [a]
        for _ in range(SLAB - 1):
            pw.append(cmul(pw[-1], a))
        row = jnp.arange(SLAB)[:, None]
        tabs = []
        for kk in (1, 2, 4):
            mask = ((row < SLAB - kk) if reverse else (row >= kk)).astype(F32)
            tabs += [mask * pw[kk - 1][0][None, :], mask * pw[kk - 1][1][None, :]]
        order = list(range(SLAB - 1, -1, -1)) if reverse else list(range(SLAB))
        tabs += [jnp.stack([pw[o][0] for o in order]), jnp.stack([pw[o][1] for o in order])]
        return jnp.stack(tabs)

    return build((ar, ai), False), build((ar, -ai), True)


def _slab_scan(xr, xi, tabs, cr, ci, reverse):
    for n, kk in enumerate((1, 2, 4)):
        sh = SLAB - kk if reverse else kk
        tr, ti = tabs[2 * n], tabs[2 * n + 1]
        sr, si = pltpu.roll(xr, sh, 0), pltpu.roll(xi, sh, 0)
        xr, xi = xr + tr * sr - ti * si, xi + tr * si + ti * sr
    tr, ti = tabs[6], tabs[7]
    return xr + tr * cr - ti * ci, xi + tr * ci + ti * cr


def s5_fwd(u, bbd_r, bbd_i, cbd_r, cbd_i, tab, *, name):
    s = u.shape[0]
    tbk = _long_rows(s)
    nt = s // tbk
    nsl = tbk // SLAB

    def body(u_ref, br_ref, bi_ref, cr_ref, ci_ref, tab_ref, y_ref, sr_ref, si_ref, bur, bui, carry):
        @pl.when(pl.program_id(1) == 0)
        def _():
            carry[...] = jnp.zeros_like(carry)

        ub = u_ref[...].astype(BF16)
        bur[...] = jnp.dot(ub, br_ref[0], preferred_element_type=F32)
        bui[...] = jnp.dot(ub, bi_ref[0], preferred_element_type=F32)
        tabs = [tab_ref[n] for n in range(8)]

        def slab(n, c):
            r0 = pl.multiple_of(n * SLAB, SLAB)
            sr, si = _slab_scan(bur[pl.ds(r0, SLAB), :], bui[pl.ds(r0, SLAB), :], tabs, c[0], c[1], False)
            sr_ref[pl.ds(r0, SLAB), :] = sr
            si_ref[pl.ds(r0, SLAB), :] = si
            return (jnp.broadcast_to(sr[SLAB - 1:SLAB], sr.shape), jnp.broadcast_to(si[SLAB - 1:SLAB], si.shape))

        cr, ci = lax.fori_loop(0, nsl, slab, (carry[0], carry[1]))
        carry[0] = cr
        carry[1] = ci
        y_ref[...] = (jnp.dot(sr_ref[...].astype(BF16), cr_ref[0], preferred_element_type=F32)
                      - jnp.dot(si_ref[...].astype(BF16), ci_ref[0], preferred_element_type=F32))

    us = pl.BlockSpec((tbk, LANE), lambda j, t: (t, j))
    bs = pl.BlockSpec((1, LANE, SW), lambda j, t: (j, 0, 0))
    cs = pl.BlockSpec((1, SW, LANE), lambda j, t: (j, 0, 0))
    ts = pl.BlockSpec((8, SLAB, SW), lambda j, t: (0, 0, j))
    ss = pl.BlockSpec((tbk, SW), lambda j, t: (t, j))
    return pl.pallas_call(
        body, name=name, grid=(NJ, nt), in_specs=[us, bs, bs, cs, cs, ts], out_specs=[us, ss, ss],
        out_shape=[jax.ShapeDtypeStruct((s, D_MODEL), F32), jax.ShapeDtypeStruct((s, NST), F32),
                   jax.ShapeDtypeStruct((s, NST), F32)],
        scratch_shapes=[pltpu.VMEM((tbk, SW), F32), pltpu.VMEM((tbk, SW), F32), pltpu.VMEM((2, SLAB, SW), F32)],
        compiler_params=_cparams("parallel", "arbitrary"))(u, bbd_r, bbd_i, cbd_r, cbd_i, tab)


def s5_bwd(u, dy, dskip, st_r, st_i, bbd_r, bbd_i, cbd_r, cbd_i, tabrev, *, name):
    s = u.shape[0]
    tbk = _long_rows(s)
    nt = s // tbk
    nsl = tbk // SLAB
    rbk = tbk // SLAB

    def body(u_ref, dy_ref, d_ref, sr_ref, si_ref, pr_ref, pi_ref, br_ref, bi_ref, cr_ref, ci_ref, tab_ref,
             du_ref, dbr_ref, dbi_ref, dcr_ref, dci_ref, da_ref, lam_r, lam_i, carry):
        t = pl.program_id(1)

        @pl.when(t == 0)
        def _():
            carry[...] = jnp.zeros_like(carry)
            dbr_ref[...] = jnp.zeros_like(dbr_ref)
            dbi_ref[...] = jnp.zeros_like(dbi_ref)
            dcr_ref[...] = jnp.zeros_like(dcr_ref)
            dci_ref[...] = jnp.zeros_like(dci_ref)
            da_ref[...] = jnp.zeros_like(da_ref)

        dyv = dy_ref[...]
        dyh = dyv.astype(BF16)
        lam_r[...] = lax.dot_general(dyh, cr_ref[0], NT, preferred_element_type=F32)
        lam_i[...] = -lax.dot_general(dyh, ci_ref[0], NT, preferred_element_type=F32)
        tabs = [tab_ref[n] for n in range(8)]

        def slab(n, c):
            r0 = pl.multiple_of((nsl - 1 - n) * SLAB, SLAB)
            lr, li = _slab_scan(lam_r[pl.ds(r0, SLAB), :], lam_i[pl.ds(r0, SLAB), :], tabs, c[0], c[1], True)
            lam_r[pl.ds(r0, SLAB), :] = lr
            lam_i[pl.ds(r0, SLAB), :] = li
            return (jnp.broadcast_to(lr[0:1], lr.shape), jnp.broadcast_to(li[0:1], li.shape))

        cr, ci = lax.fori_loop(0, nsl, slab, (carry[0], carry[1]))
        carry[0] = cr
        carry[1] = ci
        lr, li = lam_r[...], lam_i[...]
        lrh, lih = lr.astype(BF16), li.astype(BF16)
        du = (dyv * d_ref[...] + lax.dot_general(lrh, br_ref[0], NT, preferred_element_type=F32)
              + lax.dot_general(lih, bi_ref[0], NT, preferred_element_type=F32))
        du_ref[...] = du.astype(BF16)
        ub = u_ref[...].astype(BF16)
        dbr_ref[0] += lax.dot_general(ub, lrh, TN, preferred_element_type=F32)
        dbi_ref[0] += lax.dot_general(ub, lih, TN, preferred_element_type=F32)
        srv, siv = sr_ref[...], si_ref[...]
        dcr_ref[0] += lax.dot_general(srv.astype(BF16), dyh, TN, preferred_element_type=F32)
        dci_ref[0] -= lax.dot_general(siv.astype(BF16), dyh, TN, preferred_element_type=F32)
        first = lax.broadcasted_iota(jnp.int32, srv.shape, 0) == 0
        pm = (t < nt - 1).astype(F32)
        spr = jnp.where(first, pr_ref[SLAB - 1:SLAB, :] * pm, pltpu.roll(srv, 1, 0))
        spi = jnp.where(first, pi_ref[SLAB - 1:SLAB, :] * pm, pltpu.roll(siv, 1, 0))
        da_ref[0:1, :] += jnp.sum(lr * spr + li * spi, axis=0, keepdims=True)
        da_ref[1:2, :] += jnp.sum(li * spr - lr * spi, axis=0, keepdims=True)

    rv = lambda t: nt - 1 - t
    us = pl.BlockSpec((tbk, LANE), lambda j, t: (rv(t), j))
    ds = pl.BlockSpec((1, LANE), lambda j, t: (0, j))
    ss = pl.BlockSpec((tbk, SW), lambda j, t: (rv(t), j))
    ps = pl.BlockSpec((SLAB, SW), lambda j, t: (jnp.maximum(rv(t) * rbk - 1, 0), j))
    bs = pl.BlockSpec((1, LANE, SW), lambda j, t: (j, 0, 0))
    cs = pl.BlockSpec((1, SW, LANE), lambda j, t: (j, 0, 0))
    ts = pl.BlockSpec((8, SLAB, SW), lambda j, t: (0, 0, j))
    das = pl.BlockSpec((2, SW), lambda j, t: (0, j))
    return pl.pallas_call(
        body, name=name, grid=(NJ, nt),
        in_specs=[us, us, ds, ss, ss, ps, ps, bs, bs, cs, cs, ts],
        out_specs=[us, bs, bs, cs, cs, das],
        out_shape=[jax.ShapeDtypeStruct((s, D_MODEL), BF16),
                   jax.ShapeDtypeStruct((NJ, LANE, SW), F32), jax.ShapeDtypeStruct((NJ, LANE, SW), F32),
                   jax.ShapeDtypeStruct((NJ, SW, LANE), F32), jax.ShapeDtypeStruct((NJ, SW, LANE), F32),
                   jax.ShapeDtypeStruct((2, NST), F32)],
        scratch_shapes=[pltpu.VMEM((tbk, SW), F32), pltpu.VMEM((tbk, SW), F32), pltpu.VMEM((2, SLAB, SW), F32)],
        compiler_params=_cparams("parallel", "arbitrary"))(
            u, dy, dskip, st_r, st_i, st_r, st_i, bbd_r, bbd_i, cbd_r, cbd_i, tabrev)


GELU_C = math.sqrt(2.0 / math.pi)
GELU_A = 0.044715


def s5post_fwd(y, u, dskip, *, name):
    s = y.shape[0]
    tm = _rows(s)

    def body(y_ref, u_ref, d_ref, o_ref):
        z = y_ref[...] + d_ref[...] * u_ref[...]
        o_ref[...] = (0.5 * z * (1.0 + jnp.tanh(GELU_C * (z + GELU_A * z * z * z)))).astype(BF16)

    row = pl.BlockSpec((tm, D_MODEL), lambda i: (i, 0))
    vec = pl.BlockSpec((1, D_MODEL), lambda i: (0, 0))
    return pl.pallas_call(body, name=name, grid=(s // tm,), in_specs=[row, row, vec], out_specs=row,
                          out_shape=jax.ShapeDtypeStruct((s, D_MODEL), BF16),
                          compiler_params=_cparams("parallel"))(y, u, dskip)


def s5post_bwd(y, u, dskip, dg, *, name):
    s = y.shape[0]
    tm = _rows(s)

    def body(y_ref, u_ref, d_ref, dg_ref, dz_ref, dd_ref):
        @pl.when(pl.program_id(0) == 0)
        def _():
            dd_ref[...] = jnp.zeros_like(dd_ref)

        uv = u_ref[...]
        z = y_ref[...] + d_ref[...] * uv
        th = jnp.tanh(GELU_C * (z + GELU_A * z * z * z))
        dgelu = 0.5 * (1.0 + th) + 0.5 * z * (1.0 - th * th) * (GELU_C * (1.0 + 3.0 * GELU_A * z * z))
        dz = dg_ref[...] * dgelu
        dz_ref[...] = dz
        dd_ref[...] += jnp.sum(dz * uv, axis=0, keepdims=True)

    row = pl.BlockSpec((tm, D_MODEL), lambda i: (i, 0))
    vec = pl.BlockSpec((1, D_MODEL), lambda i: (0, 0))
    return pl.pallas_call(body, name=name, grid=(s // tm,), in_specs=[row, row, vec, row], out_specs=[row, vec],
                          out_shape=[jax.ShapeDtypeStruct((s, D_MODEL), F32), jax.ShapeDtypeStruct((1, D_MODEL), F32)],
                          compiler_params=_cparams("arbitrary"))(y, u, dskip, dg)


def glu_fwd(glu, x, *, name):
    s = x.shape[0]
    tm = _rows(s)

    def body(a_ref, b_ref, x_ref, o_ref):
        o_ref[...] = x_ref[...] + a_ref[...] * jax.nn.sigmoid(b_ref[...])

    row = pl.BlockSpec((tm, D_MODEL), lambda i: (i, 0))
    return pl.pallas_call(body, name=name, grid=(s // tm,),
                          in_specs=[row, pl.BlockSpec((tm, D_MODEL), lambda i: (i, 1)), row], out_specs=row,
                          out_shape=jax.ShapeDtypeStruct((s, D_MODEL), F32),
                          compiler_params=_cparams("parallel"))(glu, glu, x)


def glu_bwd(glu, dx, *, name):
    s = dx.shape[0]
    tm = _rows(s)

    def body(a_ref, b_ref, dx_ref, o_ref):
        sg = jax.nn.sigmoid(b_ref[...])
        dxv = dx_ref[...]
        o_ref[:, :D_MODEL] = (dxv * sg).astype(BF16)
        o_ref[:, D_MODEL:] = (dxv * a_ref[...] * sg * (1.0 - sg)).astype(BF16)

    row = pl.BlockSpec((tm, D_MODEL), lambda i: (i, 0))
    return pl.pallas_call(body, name=name, grid=(s // tm,),
                          in_specs=[row, pl.BlockSpec((tm, D_MODEL), lambda i: (i, 1)), row],
                          out_specs=pl.BlockSpec((tm, 2 * D_MODEL), lambda i: (i, 0)),
                          out_shape=jax.ShapeDtypeStruct((s, 2 * D_MODEL), BF16),
                          compiler_params=_cparams("parallel"))(glu, glu, dx)


def loss_head(y, target, *, name):
    s = y.shape[0]
    tm = _rows(s)

    def body(y_ref, t_ref, dy_ref, dyh_ref, l_ref):
        @pl.when(pl.program_id(0) == 0)
        def _():
            l_ref[...] = jnp.zeros_like(l_ref)

        e = y_ref[...] - t_ref[...]
        dy_ref[...] = e * (1.0 / D_MODEL)
        dyh_ref[...] = (e * (1.0 / D_MODEL)).astype(BF16)
        e2 = jnp.sum((e * e).reshape(tm // 8, 8, D_MODEL), axis=0)
        acc = e2[:, 0:LANE]
        for kk in range(1, D_MODEL // LANE):
            acc = acc + e2[:, kk * LANE:(kk + 1) * LANE]
        l_ref[...] += acc

    row = pl.BlockSpec((tm, D_MODEL), lambda i: (i, 0))
    return pl.pallas_call(body, name=name, grid=(s // tm,), in_specs=[row, row],
                          out_specs=[row, row, pl.BlockSpec((8, LANE), lambda i: (0, 0))],
                          out_shape=[jax.ShapeDtypeStruct((s, D_MODEL), F32), jax.ShapeDtypeStruct((s, D_MODEL), BF16),
                                     jax.ShapeDtypeStruct((8, LANE), F32)],
                          compiler_params=_cparams("arbitrary"))(y, target)


PACKW = 1024
NCHIP = 4


def _mesh_pos():
    return lax.axis_index("x"), lax.axis_index("y"), lax.axis_index("c")


def _chip_exchange(bufs, scatter, name):
    n = len(bufs)
    shapes = [b.shape[1:] if scatter else b.shape for b in bufs]

    def body(*refs):
        ins, outs = refs[:n], refs[n:2 * n]
        send_sems, recv_sems, local_sems = refs[2 * n:]
        x, y, c = _mesh_pos()
        me = 2 * x + y
        peers = [(1 - x, y), (x, 1 - y), (1 - x, 1 - y)]

        def copy(a, j, px, py, dst_slot):
            src = ins[a].at[2 * px + py] if scatter else ins[a]
            return pltpu.make_async_remote_copy(src_ref=src, dst_ref=outs[a].at[dst_slot],
                                                send_sem=send_sems.at[3 * a + j], recv_sem=recv_sems.at[3 * a + j],
                                                device_id=(px, py, c), device_id_type=MESH)

        mine = [pltpu.make_async_copy(ins[a].at[me] if scatter else ins[a], outs[a].at[me], local_sems.at[a])
                for a in range(n)]
        sends = [copy(a, j, px, py, me) for a in range(n) for j, (px, py) in enumerate(peers)]
        for cp in mine + sends:
            cp.start()
        for a in range(n):
            for j, (px, py) in enumerate(peers):
                copy(a, j, px, py, 2 * px + py).wait_recv()
        for cp in sends:
            cp.wait_send()
        for cp in mine:
            cp.wait()

    return pl.pallas_call(
        body, name=name, in_specs=[ANY] * n, out_specs=[ANY] * n,
        out_shape=[jax.ShapeDtypeStruct((NCHIP,) + tuple(shp), b.dtype) for shp, b in zip(shapes, bufs)],
        scratch_shapes=[pltpu.SemaphoreType.DMA((3 * n,)), pltpu.SemaphoreType.DMA((3 * n,)),
                        pltpu.SemaphoreType.DMA((n,))],
    )(*bufs)


HBM_SPEC = pl.BlockSpec(memory_space=pltpu.HBM)
SEM_SPEC = pl.BlockSpec(memory_space=pltpu.SEMAPHORE)
DATAFLOW = pltpu.SideEffectType.DATAFLOW_SIDE_EFFECTING


def _exchange_copy(ins, lands, send_sems, recv_sems, scatter, a, j, px, py, c, dst_slot):
    src = ins[a].at[2 * px + py] if scatter else ins[a]
    return pltpu.make_async_remote_copy(src_ref=src, dst_ref=lands[a].at[dst_slot],
                                        send_sem=send_sems.at[3 * a + j], recv_sem=recv_sems.at[3 * a + j],
                                        device_id=(px, py, c), device_id_type=MESH)


def _own_copy(ins, lands, local_sems, scatter, a, me):
    return pltpu.make_async_copy(ins[a].at[me] if scatter else ins[a], lands[a].at[me], local_sems.at[a])


def exchange_start(bufs, scatter, name):
    n = len(bufs)
    lands = [lax.empty((NCHIP,) + tuple(b.shape[1:] if scatter else b.shape), b.dtype) for b in bufs]

    def body(*refs):
        ins, lnd, send_sems, recv_sems, local_sems, token = (refs[:n], refs[n:2 * n], refs[2 * n], refs[2 * n + 1],
                                                             refs[2 * n + 2], refs[-1])
        x, y, c = _mesh_pos()
        me = 2 * x + y
        for a in range(n):
            _own_copy(ins, lnd, local_sems, scatter, a, me).start()
            for j, (px, py) in enumerate([(1 - x, y), (x, 1 - y), (1 - x, 1 - y)]):
                _exchange_copy(ins, lnd, send_sems, recv_sems, scatter, a, j, px, py, c, me).start()
        token[...] = jnp.zeros_like(token)

    thru = [pltpu.HBM(b.shape, b.dtype) for b in list(bufs) + lands]
    out = pl.pallas_call(
        body, name=name, in_specs=[HBM_SPEC] * (2 * n),
        out_specs=[SEM_SPEC] * 3 + [HBM_SPEC] * (2 * n) + [pl.BlockSpec(memory_space=pltpu.VMEM)],
        out_shape=[pltpu.SemaphoreType.DMA((3 * n,)), pltpu.SemaphoreType.DMA((3 * n,)), pltpu.SemaphoreType.DMA((n,))]
        + thru + [jax.ShapeDtypeStruct((SLAB, LANE), F32)],
        input_output_aliases={k: 3 + k for k in range(2 * n)},
        compiler_params=pltpu.CompilerParams(has_side_effects=DATAFLOW),
    )(*[pltpu.with_memory_space_constraint(b, pltpu.HBM) for b in list(bufs) + lands])
    return tuple(out[:3]), out[3:3 + n], out[3 + n:3 + 2 * n], out[-1][0, 0]


def exchange_wait(started, after, scatter, name):
    sems, bufs, lands, _ = started
    n = len(bufs)

    def body(*refs):
        ins, lnd, ssem, rsem, lsem = refs[:n], refs[n:2 * n], refs[2 * n], refs[2 * n + 1], refs[2 * n + 2]
        x, y, c = _mesh_pos()
        for a in range(n):
            _own_copy(ins, lnd, lsem, scatter, a, 2 * x + y).wait()
            for j, (px, py) in enumerate([(1 - x, y), (x, 1 - y), (1 - x, 1 - y)]):
                cp = _exchange_copy(ins, lnd, ssem, rsem, scatter, a, j, px, py, c, 2 * px + py)
                cp.wait_send()
                cp.wait_recv()

    thru = [pltpu.HBM(b.shape, b.dtype) for b in list(bufs) + list(lands)]
    out = pl.pallas_call(
        body, name=name, in_specs=[HBM_SPEC] * (2 * n) + [SEM_SPEC] * 3 + [ANY],
        out_specs=[HBM_SPEC] * (2 * n), out_shape=thru,
        input_output_aliases={k: k for k in range(2 * n)},
        compiler_params=pltpu.CompilerParams(has_side_effects=DATAFLOW),
    )(*bufs, *lands, *sems, after)
    return out[n:]


def sibling_swap(bufs, name):
    n = len(bufs)

    def body(*refs):
        ins, outs, send_sems, recv_sems = refs[:n], refs[n:2 * n], refs[2 * n], refs[2 * n + 1]
        x, y, c = _mesh_pos()
        cps = [pltpu.make_async_remote_copy(src_ref=ins[k], dst_ref=outs[k], send_sem=send_sems.at[k],
                                            recv_sem=recv_sems.at[k], device_id=(x, y, 1 - c), device_id_type=MESH)
               for k in range(n)]
        for cp in cps:
            cp.start()
        for cp in cps:
            cp.wait()

    return pl.pallas_call(
        body, name=name, in_specs=[ANY] * n, out_specs=[ANY] * n,
        out_shape=[jax.ShapeDtypeStruct(b.shape, b.dtype) for b in bufs],
        scratch_shapes=[pltpu.SemaphoreType.DMA((n,)), pltpu.SemaphoreType.DMA((n,))],
    )(*bufs)


def _swap_copy(ins, lands, send_sems, recv_sems, a):
    x, y, c = _mesh_pos()
    return pltpu.make_async_remote_copy(src_ref=ins[a], dst_ref=lands[a], send_sem=send_sems.at[a],
                                        recv_sem=recv_sems.at[a], device_id=(x, y, 1 - c), device_id_type=MESH)


def swap_start(bufs, name):
    n = len(bufs)
    lands = [lax.empty(b.shape, b.dtype) for b in bufs]

    def body(*refs):
        for a in range(n):
            _swap_copy(refs[:n], refs[n:2 * n], refs[2 * n], refs[2 * n + 1], a).start()
        refs[-1][...] = jnp.zeros_like(refs[-1])

    thru = [pltpu.HBM(b.shape, b.dtype) for b in list(bufs) + lands]
    out = pl.pallas_call(
        body, name=name, in_specs=[HBM_SPEC] * (2 * n),
        out_specs=[SEM_SPEC] * 2 + [HBM_SPEC] * (2 * n) + [pl.BlockSpec(memory_space=pltpu.VMEM)],
        out_shape=[pltpu.SemaphoreType.DMA((n,)), pltpu.SemaphoreType.DMA((n,))] + thru
        + [jax.ShapeDtypeStruct((SLAB, LANE), F32)],
        input_output_aliases={k: 2 + k for k in range(2 * n)},
        compiler_params=pltpu.CompilerParams(has_side_effects=DATAFLOW),
    )(*[pltpu.with_memory_space_constraint(b, pltpu.HBM) for b in list(bufs) + lands])
    return tuple(out[:2]), out[2:2 + n], out[2 + n:2 + 2 * n], out[-1][0, 0]


def swap_wait(started, after, name):
    sems, bufs, lands, _ = started
    n = len(bufs)

    def body(*refs):
        for a in range(n):
            _swap_copy(refs[:n], refs[n:2 * n], refs[2 * n], refs[2 * n + 1], a).wait()

    thru = [pltpu.HBM(b.shape, b.dtype) for b in list(bufs) + list(lands)]
    out = pl.pallas_call(
        body, name=name, in_specs=[HBM_SPEC] * (2 * n) + [SEM_SPEC] * 2 + [ANY],
        out_specs=[HBM_SPEC] * (2 * n), out_shape=thru,
        input_output_aliases={k: k for k in range(2 * n)},
        compiler_params=pltpu.CompilerParams(has_side_effects=DATAFLOW),
    )(*bufs, *lands, *sems, after)
    return out[n:]


EW_VMEM_BUDGET = 20 * 1024 * 1024


def _ew_rows(rows, w, bytes_per_elem):
    wpad = -(-w // LANE) * LANE
    for t in (1024, 512, 256, 128, 64, 32, 16, 8):
        if rows % t == 0 and 2 * t * wpad * bytes_per_elem <= EW_VMEM_BUDGET:
            return t
    return rows


def sum_slots(buf, *, name):
    _, rows, w = buf.shape
    tm = _ew_rows(rows, w, NCHIP * buf.dtype.itemsize + 4)

    def body(b_ref, o_ref):
        acc = b_ref[0].astype(F32)
        for kk in range(1, NCHIP):
            acc = acc + b_ref[kk].astype(F32)
        o_ref[...] = acc

    return pl.pallas_call(body, name=name, grid=(rows // tm,),
                          in_specs=[pl.BlockSpec((NCHIP, tm, w), lambda i: (0, i, 0))],
                          out_specs=pl.BlockSpec((tm, w), lambda i: (i, 0)),
                          out_shape=jax.ShapeDtypeStruct((rows, w), F32),
                          compiler_params=_cparams("parallel"))(buf)


def adamw(p_mine, p_other, w, m, v, *, name):
    rows, wd = w.shape
    tm = _ew_rows(rows, wd, 9 * 4)
    c1 = 1.0 - ADAM_B1 ** ADAM_STEP
    c2 = 1.0 - ADAM_B2 ** ADAM_STEP

    def body(a_ref, b_ref, w_ref, m_ref, v_ref, g_ref, d_ref, nm_ref, nv_ref):
        g = a_ref[...] + b_ref[...]
        nm = ADAM_B1 * m_ref[...] + (1.0 - ADAM_B1) * g
        nv = ADAM_B2 * v_ref[...] + (1.0 - ADAM_B2) * (g * g)
        g_ref[...] = g
        nm_ref[...] = nm
        nv_ref[...] = nv
        d_ref[...] = -ADAM_LR * ((nm / c1) / (jnp.sqrt(nv / c2) + ADAM_EPS) + ADAM_WD * w_ref[...])

    row = pl.BlockSpec((tm, wd), lambda i: (i, 0))
    return pl.pallas_call(body, name=name, grid=(rows // tm,), in_specs=[row] * 5, out_specs=[row] * 4,
                          out_shape=[jax.ShapeDtypeStruct((rows, wd), F32)] * 4,
                          compiler_params=_cparams("parallel"))(p_mine, p_other, w, m, v)


def _rows2d(a, lead=0):
    tail = a.shape[lead:]
    n = int(np.prod(tail))
    if tail[-1] < LANE // 2 and n % (8 * LANE) == 0:
        return a.reshape(a.shape[:lead] + (n // (8 * LANE), 8 * LANE))
    return a.reshape(a.shape[:lead] + (-1, tail[-1]))


BIG = [("mix_w_in", 2), ("w_uq", 2), ("w_ukv", 2), ("mix_w_out", 1), ("ssm_w_in", 1), ("w_glu", 2),
       ("ffn_w_up", 2), ("ffn_w_down", 1)]
SMALL = [("sconv_w", 2), ("ssm_norm", 1), ("d_skip", 1), ("ffn_conv_w", 2)]
REPL = ["attn_norm", "cq_norm", "ckv_norm", "q_gain", "k_gain", "lambda_re", "lambda_im", "log_step",
        "b_re", "b_im", "c_re", "c_im", "ffn_norm"]
ORDER = ["attn_norm", "mix_w_in", "cq_norm", "ckv_norm", "w_uq", "w_ukv", "q_gain", "k_gain", "sconv_w", "mix_w_out",
         "ssm_norm", "ssm_w_in", "lambda_re", "lambda_im", "log_step", "b_re", "b_im", "c_re", "c_im", "d_skip",
         "w_glu", "ffn_norm", "ffn_w_up", "ffn_conv_w", "ffn_w_down"]


def _join(g, axis):
    if axis == 0:
        return g.reshape((-1,) + g.shape[2:])
    return jnp.concatenate([g[k] for k in range(NCHIP)], axis=axis)


def _split(full, axis, parts=NCHIP):
    if axis == 0:
        return full.reshape((parts, -1) + full.shape[1:])
    return jnp.stack(jnp.split(full, parts, axis=axis))


def _discretize(lr, li, ls, b_re, b_im):
    dt = jnp.exp(ls)[:, None]
    mag = jnp.exp(lr * dt)
    ar, ai = mag * jnp.cos(li * dt), mag * jnp.sin(li * dt)
    nr, ni = ar - 1.0, ai
    den = lr * lr + li * li
    zr, zi = (nr * lr + ni * li) / den, (ni * lr - nr * li) / den
    bbar_r = zr[..., None] * b_re - zi[..., None] * b_im
    bbar_i = zr[..., None] * b_im + zi[..., None] * b_re
    return ar, ai, bbar_r, bbar_i


def _b_blockdiag(bbar):
    gl = G // NJ
    bb = bbar.reshape(NJ, gl, P, GC).transpose(0, 1, 3, 2)
    return jnp.einsum("jgcp,gh->jgchp", bb, jnp.eye(gl, dtype=bbar.dtype)).reshape(NJ, gl * GC, gl * P)


def _b_blockdiag_t(dbd):
    gl = G // NJ
    d = jnp.einsum("jgchp,gh->jgcp", dbd.reshape(NJ, gl, GC, gl, P), jnp.eye(gl, dtype=dbd.dtype))
    return d.transpose(0, 1, 3, 2).reshape(G, P, GC)


def _c_blockdiag(cmat):
    gl = G // NJ
    cc = cmat.reshape(NJ, gl, GC, P).transpose(0, 1, 3, 2)
    return jnp.einsum("jgpc,gh->jgphc", cc, jnp.eye(gl, dtype=cmat.dtype)).reshape(NJ, gl * P, gl * GC)


def _c_blockdiag_t(dbd):
    gl = G // NJ
    d = jnp.einsum("jgphc,gh->jgpc", dbd.reshape(NJ, gl, P, gl, GC), jnp.eye(gl, dtype=dbd.dtype))
    return d.transpose(0, 1, 3, 2).reshape(G, GC, P)


def _pad_heads_cols(w, width):
    r = w.shape[0]
    return jnp.pad(w.reshape(r, HEADS, width), ((0, 0), (0, 0), (0, HP - width))).reshape(r, HEADS * HP)


def _unpad_heads_cols(w, width):
    r = w.shape[0]
    return w.reshape(r, HEADS, HP)[:, :, :width].reshape(r, HEADS * width)


W_IN_SPLIT = (QR + KVR, QR + KVR + ROPE)


def _w_in_layout(w):
    a, b = W_IN_SPLIT
    kr = jnp.pad(w[:, a:b], ((0, 0), (NOPE, HP - QK)))
    return jnp.concatenate([w[:, :a], w[:, b:], kr], axis=1)


def _ffn_fwd(x, l, wt, name):
    h = rms_fwd(x, wt["ffn_norm"][l][None], name=f"{name}_norm")
    up = mm(h, wt["ffn_w_up"][l], name=f"{name}_up")
    act = ffnact_fwd(up, wt["ffn_conv_w"][l], name=f"{name}_act")
    out = mm(act, wt["ffn_w_down"][l], add=x, name=f"{name}_down")
    return out, (x, h, up, act)


def _ffn_bwd(dout, douth, saved, l, wt, name):
    x, h, up, act = saved
    g = {}
    dact = mm(douth, wt["ffn_w_down"][l], tb=True, name=f"{name}_ddown")
    g["ffn_w_down"] = mm(act, douth, ta=True, out_dtype=BF16, name=f"{name}_dwdown")
    dup, g["ffn_conv_w"] = ffnact_bwd(up, wt["ffn_conv_w"][l], dact, name=f"{name}_dact")
    g["ffn_w_up"] = tuple(mm(h, d, ta=True, out_dtype=BF16, name=f"{name}_dwup{kk}") for kk, d in enumerate(dup))
    dh = mm_nt_segments(dup, wt["ffn_w_up"][l], name=f"{name}_dup")
    dx, dxh, dg = rms_bwd(x, wt["ffn_norm"][l][None], dh, add=dout, twin=True, name=f"{name}_dnorm")
    g["ffn_norm"] = dg[0]
    return dx, dxh, g


def _even_fwd(x, i, wt, tabs, name):
    h = rms_fwd(x, wt["attn_norm"][i][None], name=f"{name}_norm")
    proj = mm(h, wt["w_in2"][i], name=f"{name}_in")
    cqn = rms_fwd(proj, wt["cq_norm"][i][None], col=0, name=f"{name}_cqnorm")
    ckvn = rms_fwd(proj, wt["ckv_norm"][i][None], col=1, name=f"{name}_ckvnorm")
    qraw = mm(cqn, wt["w_uq_p"][i], name=f"{name}_uq")
    kv = mm(ckvn, wt["w_ukv_p"][i], name=f"{name}_ukv")
    q, k, v = qkprep_fwd(qraw, kv, proj, wt["q_gain_p"][i], wt["k_gain_p"][i], tabs, name=f"{name}_qkprep")
    o, oh, lset = attn_fwd(q, k, v, name=f"{name}_attn")
    conv = sconv_fwd(proj, wt["sconv_w"][i], name=f"{name}_sconv")
    t = mm(oh, wt["w_out_a"][i], add=x, name=f"{name}_outa")
    out = mm(conv, wt["w_out_c"][i], add=t, name=f"{name}_outc")
    return out, (x, h, proj, cqn, ckvn, qraw, kv, q, k, v, o, oh, lset, conv)


def _even_bwd(dout, douth, saved, i, wt, tabs, name):
    x, h, proj, cqn, ckvn, qraw, kv, q, k, v, o, oh, lset, conv = saved
    g = {}
    do = mm(douth, wt["w_out_a"][i], tb=True, name=f"{name}_douta")
    dconv = mm(douth, wt["w_out_c"][i], tb=True, name=f"{name}_doutc")
    g["w_out_a"] = mm(oh, douth, ta=True, out_dtype=BF16, name=f"{name}_dwouta")
    g["w_out_c"] = mm(conv, douth, ta=True, out_dtype=BF16, name=f"{name}_dwoutc")
    dgates, g["sconv_w"] = sconv_bwd(proj, wt["sconv_w"][i], dconv, name=f"{name}_dsconv")
    doh, deltat = attn_delta(o, do, name=f"{name}_dattn_delta")
    dq, dk, dv = attn_bwd(q, k, v, doh, lset, deltat, name=f"{name}_dattn")
    dqraw, dkraw, dkrope, dqg, dkg = qkprep_bwd(qraw, kv, proj, wt["q_gain_p"][i], wt["k_gain_p"][i], tabs, dq, dk,
                                                name=f"{name}_dqkprep")
    g["q_gain"], g["k_gain"] = dqg[0, :QK], dkg[0, :QK]
    dcqn = mm(dqraw, wt["w_uq_p"][i], tb=True, name=f"{name}_duq")
    g["w_uq_p"] = mm(cqn, dqraw, ta=True, out_dtype=BF16, name=f"{name}_dwuq")
    dkv = (dkraw, dv)
    dckvn = mm_nt_segments(dkv, wt["w_ukv_p"][i], name=f"{name}_dukv")
    g["w_ukv_p"] = tuple(mm(ckvn, d, ta=True, out_dtype=BF16, name=f"{name}_dwukv{kk}") for kk, d in enumerate(dkv))
    dcq, dgq = rms_bwd(proj, wt["cq_norm"][i][None], dcqn, col=0, out_dtype=BF16, name=f"{name}_dcqnorm")
    dckv, dgkv = rms_bwd(proj, wt["ckv_norm"][i][None], dckvn, col=1, out_dtype=BF16, name=f"{name}_dckvnorm")
    g["cq_norm"], g["ckv_norm"] = dgq[0], dgkv[0]
    dproj = (dcq, dckv, *dgates, dkrope)
    g["w_in2"] = tuple(mm(h, d, ta=True, out_dtype=BF16, name=f"{name}_dwin{kk}") for kk, d in enumerate(dproj))
    dh = mm_nt_segments(dproj, wt["w_in2"][i], name=f"{name}_din")
    dx, dxh, dg = rms_bwd(x, wt["attn_norm"][i][None], dh, add=dout, twin=True, name=f"{name}_dnorm")
    g["attn_norm"] = dg[0]
    return dx, dxh, g


def _odd_fwd(x, i, wt, name):
    h = rms_fwd(x, wt["ssm_norm"][i][None], name=f"{name}_norm")
    u = mm(h, wt["ssm_w_in"][i], name=f"{name}_in")
    y, st_r, st_i = s5_fwd(u, wt["bbd_r"][i], wt["bbd_i"][i], wt["cbd_r"][i], wt["cbd_i"][i], wt["tab_f"][i],
                           name=f"{name}_scan")
    gl = s5post_fwd(y, u, wt["d_skip"][i][None], name=f"{name}_gelu")
    glu = mm(gl, wt["w_glu"][i], name=f"{name}_glu")
    out = glu_fwd(glu, x, name=f"{name}_gate")
    return out, (x, h, u, y, st_r, st_i, gl, glu)


def _odd_bwd(dout, douth, saved, i, wt, name):
    x, h, u, y, st_r, st_i, gl, glu = saved
    g = {}
    dglu = glu_bwd(glu, dout, name=f"{name}_dgate")
    g["w_glu"] = mm(gl, dglu, ta=True, out_dtype=BF16, name=f"{name}_dwglu")
    dgl = mm(dglu, wt["w_glu"][i], tb=True, name=f"{name}_dglu")
    dz, dd = s5post_bwd(y, u, wt["d_skip"][i][None], dgl, name=f"{name}_dgelu")
    g["d_skip"] = dd[0]
    du, g["bbd_r"], g["bbd_i"], g["cbd_r"], g["cbd_i"], g["a"] = s5_bwd(
        u, dz, wt["d_skip"][i][None], st_r, st_i, wt["bbd_r"][i], wt["bbd_i"][i], wt["cbd_r"][i], wt["cbd_i"][i],
        wt["tab_r"][i], name=f"{name}_dscan")
    g["ssm_w_in"] = mm(h, du, ta=True, out_dtype=BF16, name=f"{name}_dwin")
    dh = mm(du, wt["ssm_w_in"][i], tb=True, name=f"{name}_din")
    dx, dxh, dg = rms_bwd(x, wt["ssm_norm"][i][None], dh, add=dout, twin=True, name=f"{name}_dnorm")
    g["ssm_norm"] = dg[0]
    return dx, dxh, g


MATMUL_WEIGHTS = {"even": ("mix_w_in", "w_uq", "w_ukv", "mix_w_out"), "odd": ("ssm_w_in", "w_glu"),
                  "ffn": ("ffn_w_up", "ffn_w_down")}
ODD_SMALL = ("lambda_re", "lambda_im", "log_step", "b_re", "b_im", "c_re", "c_im", "ssm_norm", "d_skip")


def _even_layouts(fw, wt, i):
    wt["w_in2"][i] = _w_in_layout(fw["mix_w_in"])
    wt["w_uq_p"][i] = _pad_heads_cols(fw["w_uq"], QK)
    ukv = fw["w_ukv"].reshape(KVR, HEADS, NOPE + VD)
    wt["w_ukv_p"][i] = jnp.concatenate(
        [_pad_heads_cols(ukv[:, :, :NOPE].reshape(KVR, HEADS * NOPE), NOPE),
         _pad_heads_cols(ukv[:, :, NOPE:].reshape(KVR, HEADS * VD), VD)], axis=1)
    wt["w_out_a"][i] = _pad_heads_cols(fw["mix_w_out"][:HEADS * VD].T, VD).T
    wt["w_out_c"][i] = fw["mix_w_out"][HEADS * VD:]


def _even_layouts_t(g):
    dk_, dv_ = g["w_ukv_p"]
    dcq, dckv, dgb, dgc, dci, dkr = g["w_in2"]
    return {"mix_w_in": jnp.concatenate([dcq, dckv, dkr[:, NOPE:QK], dgb, dgc, dci], axis=1),
            "w_uq": _unpad_heads_cols(g["w_uq_p"], QK),
            "w_ukv": jnp.concatenate([dk_.reshape(KVR, HEADS, HP)[:, :, :NOPE], dv_.reshape(KVR, HEADS, HP)[:, :, :VD]],
                                     axis=2).reshape(KVR, HEADS * (NOPE + VD)),
            "mix_w_out": jnp.concatenate([_unpad_heads_cols(g["w_out_a"].T, VD).T, g["w_out_c"]], axis=0)}


def _local_step(x, target, full, getw, putg):
    s = x.shape[0]
    n_even = (DEPTH + 1) // 2
    n_odd = DEPTH // 2
    tabs = _rope_tables(s)
    wt = dict(full)
    for key in ("w_in2", "w_uq_p", "w_ukv_p", "w_out_a", "w_out_c") + sum(MATMUL_WEIGHTS.values(), ()):
        wt[key] = {}
    wt["q_gain_p"] = jnp.pad(full["q_gain"], ((0, 0), (0, HP - QK)))[:, None, :]
    wt["k_gain_p"] = jnp.pad(full["k_gain"], ((0, 0), (0, HP - QK)))[:, None, :]

    disc_vjp = []
    for key in ("bbd_r", "bbd_i", "cbd_r", "cbd_i", "tab_f", "tab_r"):
        wt[key] = []
    for i in range(n_odd):
        (ar, ai, bbr, bbi), vjp = jax.vjp(_discretize, full["lambda_re"][i], full["lambda_im"][i], full["log_step"][i],
                                          full["b_re"][i], full["b_im"][i])
        disc_vjp.append(vjp)
        tf, tr = _scan_tables(ar.reshape(-1), ai.reshape(-1))
        wt["tab_f"].append(tf)
        wt["tab_r"].append(tr)
        wt["bbd_r"].append(_b_blockdiag(bbr).astype(BF16))
        wt["bbd_i"].append(_b_blockdiag(bbi).astype(BF16))
        wt["cbd_r"].append(_c_blockdiag(full["c_re"][i]).astype(BF16))
        wt["cbd_i"].append(_c_blockdiag(full["c_im"][i]).astype(BF16))

    saved = []
    for layer in range(DEPTH):
        i = layer // 2
        if layer % 2 == 0:
            fw, tok = getw("even", i, x)
            wt["attn_norm"] = full["attn_norm"] + tok
            _even_layouts(fw, wt, i)
            x, sm = _even_fwd(x, i, wt, tabs, f"l{layer}_mla")
        else:
            fw, tok = getw("odd", i, x)
            wt["ssm_norm"] = full["ssm_norm"] + tok
            for n, a in fw.items():
                wt[n][i] = a
            x, sm = _odd_fwd(x, i, wt, f"l{layer}_s5")
        fw, tok = getw("ffn", layer, x)
        wt["ffn_norm"] = full["ffn_norm"] + tok
        for n, a in fw.items():
            wt[n][layer] = a
        x, sf = _ffn_fwd(x, layer, wt, f"l{layer}_ffn")
        saved.append((sm, sf))
    dx, dxh, lslab = loss_head(x, target, name="loss_head")

    own = [n for n in ORDER if n not in sum(MATMUL_WEIGHTS.values(), ())]
    grads = {n: [None] * (DEPTH if n.startswith("ffn") else n_even) for n in own}
    tok = 0.0
    for layer in reversed(range(DEPTH)):
        i = layer // 2
        sm, sf = saved[layer]
        wt["ffn_conv_w"] = full["ffn_conv_w"] + tok
        dx, dxh, g = _ffn_bwd(dx, dxh, sf, layer, wt, f"l{layer}_ffn")
        tok = putg("ffn", layer, {n: g[n] for n in MATMUL_WEIGHTS["ffn"]})
        for n in ("ffn_norm", "ffn_conv_w"):
            grads[n][layer] = g[n]
        if layer % 2 == 0:
            wt["sconv_w"] = full["sconv_w"] + tok
            dx, dxh, g = _even_bwd(dx, dxh, sm, i, wt, tabs, f"l{layer}_mla")
            tok = putg("even", i, _even_layouts_t(g))
            for n in ("attn_norm", "cq_norm", "ckv_norm", "q_gain", "k_gain", "sconv_w"):
                grads[n][i] = g[n]
        else:
            wt["d_skip"] = full["d_skip"] + tok
            dx, dxh, g = _odd_bwd(dx, dxh, sm, i, wt, f"l{layer}_s5")
            tok = putg("odd", i, {n: g[n] for n in MATMUL_WEIGHTS["odd"]})
            dlr, dli, dls, dbr, dbi = disc_vjp[i]((g["a"][0].reshape(G, P), g["a"][1].reshape(G, P),
                                                    _b_blockdiag_t(g["bbd_r"]), _b_blockdiag_t(g["bbd_i"])))
            grads["lambda_re"][i], grads["lambda_im"][i], grads["log_step"][i] = dlr, dli, dls
            grads["b_re"][i], grads["b_im"][i] = dbr, dbi
            grads["c_re"][i], grads["c_im"][i] = _c_blockdiag_t(g["cbd_r"]), _c_blockdiag_t(g["cbd_i"])
            for n in ("ssm_norm", "d_skip"):
                grads[n][i] = g[n]
            if i == 0:
                tok = tok + putg("odd_small", 0, {n: jnp.stack(grads[n]) for n in ODD_SMALL})
    grads = {n: jnp.stack(v) for n, v in grads.items()}
    return jnp.sum(lslab), dx, grads


def kernel(x, attn_norm, mix_w_in, cq_norm, ckv_norm, w_uq, w_ukv, q_gain, k_gain, sconv_w, mix_w_out, ssm_norm, ssm_w_in, lambda_re, lambda_im, log_step, b_re, b_im, c_re, c_im, d_skip, w_glu, ffn_norm, ffn_w_up, ffn_conv_w, ffn_w_down, loss_target, m_attn_norm, m_mix_w_in, m_cq_norm, m_ckv_norm, m_w_uq, m_w_ukv, m_q_gain, m_k_gain, m_sconv_w, m_mix_w_out, m_ssm_norm, m_ssm_w_in, m_lambda_re, m_lambda_im, m_log_step, m_b_re, m_b_im, m_c_re, m_c_im, m_d_skip, m_w_glu, m_ffn_norm, m_ffn_w_up, m_ffn_conv_w, m_ffn_w_down, v_attn_norm, v_mix_w_in, v_cq_norm, v_ckv_norm, v_w_uq, v_w_ukv, v_q_gain, v_k_gain, v_sconv_w, v_mix_w_out, v_ssm_norm, v_ssm_w_in, v_lambda_re, v_lambda_im, v_log_step, v_b_re, v_b_im, v_c_re, v_c_im, v_d_skip, v_w_glu, v_ffn_norm, v_ffn_w_up, v_ffn_conv_w, v_ffn_w_down):
    args = dict(locals())
    w = {n: args[n] for n in ORDER}
    m = {n: args["m_" + n] for n in ORDER}
    v = {n: args["v_" + n] for n in ORDER}
    me = 2 * lax.axis_index("x") + lax.axis_index("y")

    axis = dict(BIG)

    gs = _chip_exchange([w[n] for n, _ in SMALL], False, "gather_w_f32")
    full = {n: w[n] for n in REPL}
    for (n, ax), g in zip(SMALL, gs):
        full[n] = _join(g, ax)
    parts = [(("even", "odd")[layer % 2], layer // 2) for layer in range(DEPTH)]
    parts = [p for layer, mixer in enumerate(parts) for p in (mixer, ("ffn", layer))]
    gathers = {}

    def start_gather(kind, idx, zero):
        shards = [(w[n][idx] + zero).astype(BF16) for n in MATMUL_WEIGHTS[kind]]
        gathers[kind, idx] = exchange_start(shards, False, f"gather_start_{kind}{idx}")

    start_gather(*parts[0], 0.0 * gs[0][(0,) * gs[0].ndim])

    def getw(kind, idx, after):
        got = exchange_wait(gathers[kind, idx], after, False, f"gather_wait_{kind}{idx}")
        nxt = parts.index((kind, idx)) + 1
        tok = 0.0
        if nxt < len(parts):
            start_gather(*parts[nxt], 0.0 * got[0][(0,) * got[0].ndim].astype(F32))
            tok = gathers[parts[nxt]][3]
        return {n: _join(g, axis[n] - 1) for n, g in zip(MATMUL_WEIGHTS[kind], got)}, tok

    scatters, early = [], []

    def putg(kind, idx, g):
        if kind == "odd_small":
            arrs = [_rows2d(g[n]) for n in ODD_SMALL]
            early.append(exchange_start(arrs, False, "gather_start_g_odd"))
            return early[0][3]
        blocks = [jnp.concatenate([_split(part, axis[n] - 1, NCHIP // len(g[n])) for part in g[n]])
                  if isinstance(g[n], tuple) else _split(g[n], axis[n] - 1) for n in MATMUL_WEIGHTS[kind]]
        tok = 0.0
        if scatters:
            tok = reduce_part(*scatters.pop(), blocks[0])
        scatters.append((kind, idx, exchange_start(blocks, True, f"scatter_start_{kind}{idx}")))
        return tok + scatters[-1][2][3]

    swaps = []

    def reduce_part(kind, idx, started, after):
        got = exchange_wait(started, after, True, f"scatter_wait_{kind}{idx}")
        sums = [sum_slots(_rows2d(sl, 1), name=f"sum_{n}{idx}") for n, sl in zip(MATMUL_WEIGHTS[kind], got)]
        swaps.append((kind, idx, sums, swap_start(sums, f"swap_start_{kind}{idx}")))
        return swaps[-1][3][3]

    sq, dx, grads = _local_step(x[0], loss_target[0], full, getw, putg)
    loss = lax.psum(0.5 * sq / D_MODEL, ("x", "y", "c"))

    late_names = [n for n in REPL + [n for n, _ in SMALL] if n not in ODD_SMALL]
    rep_names = list(ODD_SMALL) + late_names
    names = [n for n, _ in BIG] + rep_names
    late = [_rows2d(grads[n]) for n in late_names]
    late_started = exchange_start(late, False, "gather_start_g_f32")
    reduce_part(*scatters.pop(), dx)
    slots = (exchange_wait(early[0], dx, False, "gather_wait_g_odd")
             + exchange_wait(late_started, swaps[-1][2][0], False, "gather_wait_g_f32"))
    small_mine = [sum_slots(_rows2d(sl, 1), name=f"sum_{n}") for n, sl in zip(rep_names, slots)]
    small_other = list(sibling_swap(small_mine, "swap_g_small"))
    summed, swapped = {}, {}
    for kind, idx, sums, started in swaps:
        for n, p, q in zip(MATMUL_WEIGHTS[kind], sums, swap_wait(started, small_other[0], f"swap_wait_{kind}{idx}")):
            summed[n, idx], swapped[n, idx] = p, q
    mine = [jnp.concatenate([summed[n, idx] for idx in range(w[n].shape[0])], axis=0) for n, _ in BIG] + small_mine
    other = [jnp.concatenate([swapped[n, idx] for idx in range(w[n].shape[0])], axis=0) for n, _ in BIG] + small_other

    def local(n, p):
        ax = dict(SMALL).get(n)
        if ax is None:
            return p
        part = lax.dynamic_index_in_dim(_split(p.reshape(grads[n].shape), ax), me, 0, keepdims=False)
        return _rows2d(part)

    outs = {}
    for n, p, q in zip(names, mine, other):
        res = adamw(local(n, p), local(n, q), _rows2d(w[n]), _rows2d(m[n]), _rows2d(v[n]), name=f"adamw_{n}")
        outs[n] = [r.reshape(w[n].shape) for r in res]
    return (loss, dx[None], *[outs[n][0] for n in ORDER], *[outs[n][1] for n in ORDER],
            *[outs[n][2] for n in ORDER], *[outs[n][3] for n in ORDER])
```

```python
import math

import numpy as np
import jax
import jax.numpy as jnp
from jax import lax
from jax.experimental import pallas as pl
from jax.experimental.pallas import tpu as pltpu

F32, BF16 = jnp.float32, jnp.bfloat16

D_MODEL = 1024
DEPTH = 4
HEADS = 8
NOPE, ROPE, QK, VD = 64, 32, 96, 64
HP = 128
QR, KVR = 256, 256
CONVC = 512
G, P, GC = 64, 64, 16
NST = G * P
SLAB = 8
LANE = 128
EPS = 1e-6
ROPE_THETA = 10000.0
ADAM_LR, ADAM_B1, ADAM_B2, ADAM_EPS, ADAM_WD, ADAM_STEP = 0.001, 0.9, 0.999, 1e-08, 0.01, 10
VMEM_LIMIT = 48 * 1024 * 1024
MESH = pl.DeviceIdType.MESH
ANY = pl.BlockSpec(memory_space=pl.ANY)


def _cparams(*sem):
    return pltpu.CompilerParams(dimension_semantics=sem, vmem_limit_bytes=VMEM_LIMIT)


def _pick(dim, prefs):
    for p in prefs:
        if dim % p == 0:
            return p
    return dim


def _rows(s):
    return _pick(s, (512, 256, 128, 64, 32, 16, 8))


def _long_rows(s):
    return _pick(s, (1024, 512, 256, 128, 64, 32, 16, 8))


MM_VMEM_BUDGET = 36 * 1024 * 1024
MM_MAX_TILE_ELEMS = 640 * 1024
HBM_BYTES_PER_US = 3.0e6
STEP_OVERHEAD_US = 0.35


def _lane_tiles(n):
    c = {t for t in range(LANE, min(n, 1536) + 1, LANE) if n % t == 0}
    if n <= 2304 or not c:
        c.add(n)
    return sorted(c, reverse=True)


def _mm_tiles(m, n, k, sa, sb, so):
    best = None
    for tm in [t for t in (1024, 512, 256) if m % t == 0] or [m]:
        for tn in _lane_tiles(n):
            if tm * tn > MM_MAX_TILE_ELEMS:
                continue
            if 2 * (tm * k * sa + k * tn * sb + tm * tn * so) + 4 * tm * tn > MM_VMEM_BUDGET:
                continue
            steps = (m // tm) * (n // tn)
            for inner_n in (True, False):
                moved = (m * k * sa + (m // tm) * k * n * sb) if inner_n else (k * n * sb + (n // tn) * m * k * sa)
                cost = (moved + m * n * so) / HBM_BYTES_PER_US + steps * STEP_OVERHEAD_US
                if best is None or cost < best[0]:
                    best = (cost, tm, tn, inner_n)
    return best[1:]


def mm(a, b, *, ta=False, tb=False, add=None, out_dtype=F32, name):
    if ta:
        kdim, m = a.shape
    else:
        m, kdim = a.shape
    n = b.shape[0] if tb else b.shape[1]
    so = jnp.dtype(out_dtype).itemsize + (0 if add is None else add.dtype.itemsize)
    tm, tn, inner_n = _mm_tiles(m, n, kdim, a.dtype.itemsize, b.dtype.itemsize, so)
    dn = (((0 if ta else 1,), (1 if tb else 0,)), ((), ()))

    def body(*refs):
        if add is None:
            a_ref, b_ref, o_ref = refs
        else:
            a_ref, b_ref, add_ref, o_ref = refs
        r = lax.dot_general(a_ref[...].astype(BF16), b_ref[...].astype(BF16), dn, preferred_element_type=F32)
        if add is not None:
            r = r + add_ref[...].astype(F32)
        o_ref[...] = r.astype(out_dtype)

    ij = (lambda g0, g1: (g0, g1)) if inner_n else (lambda g0, g1: (g1, g0))
    a_spec = (pl.BlockSpec((kdim, tm), lambda g0, g1: (0, ij(g0, g1)[0])) if ta
              else pl.BlockSpec((tm, kdim), lambda g0, g1: (ij(g0, g1)[0], 0)))
    b_spec = (pl.BlockSpec((tn, kdim), lambda g0, g1: (ij(g0, g1)[1], 0)) if tb
              else pl.BlockSpec((kdim, tn), lambda g0, g1: (0, ij(g0, g1)[1])))
    o_spec = pl.BlockSpec((tm, tn), lambda g0, g1: ij(g0, g1))
    ins, specs = [a, b], [a_spec, b_spec]
    if add is not None:
        ins.append(add)
        specs.append(o_spec)
    grid = (m // tm, n // tn) if inner_n else (n // tn, m // tm)
    return pl.pallas_call(
        body, name=name, grid=grid, in_specs=specs, out_specs=o_spec,
        out_shape=jax.ShapeDtypeStruct((m, n), out_dtype),
        compiler_params=_cparams("parallel", "parallel"))(*ins)


def mm_nt_segments(segs, b, *, name):
    m = segs[0].shape[0]
    n, kdim = b.shape
    widths = [sg.shape[1] for sg in segs]
    offs = [sum(widths[:i]) for i in range(len(segs))]
    assert sum(widths) == kdim and all(o % wd == 0 for o, wd in zip(offs, widths))
    sa = max(sg.dtype.itemsize for sg in segs)
    tm, tn, inner_n = _mm_tiles(m, n, kdim, sa, b.dtype.itemsize, 4)
    ns = len(segs)

    def body(*refs):
        o_ref = refs[-1]
        r = None
        for a_ref, b_ref in zip(refs[:ns], refs[ns:2 * ns]):
            d = lax.dot_general(a_ref[...].astype(BF16), b_ref[...].astype(BF16), NT, preferred_element_type=F32)
            r = d if r is None else r + d
        o_ref[...] = r

    ij = (lambda g0, g1: (g0, g1)) if inner_n else (lambda g0, g1: (g1, g0))
    a_specs = [pl.BlockSpec((tm, wd), lambda g0, g1: (ij(g0, g1)[0], 0)) for wd in widths]
    b_specs = [pl.BlockSpec((tn, wd), lambda g0, g1, blk=o // wd: (ij(g0, g1)[1], blk)) for o, wd in zip(offs, widths)]
    grid = (m // tm, n // tn) if inner_n else (n // tn, m // tm)
    return pl.pallas_call(
        body, name=name, grid=grid, in_specs=a_specs + b_specs,
        out_specs=pl.BlockSpec((tm, tn), lambda g0, g1: ij(g0, g1)),
        out_shape=jax.ShapeDtypeStruct((m, n), F32),
        compiler_params=_cparams("parallel", "parallel"))(*segs, *([b] * ns))


def rms_fwd(x, g, *, col=0, out_dtype=BF16, name):
    s = x.shape[0]
    d = g.shape[1]
    tm = _rows(s)

    def body(x_ref, g_ref, o_ref):
        xv = x_ref[...]
        r = lax.rsqrt(jnp.mean(xv * xv, axis=-1, keepdims=True) + EPS)
        o_ref[...] = (xv * r * g_ref[...]).astype(out_dtype)

    return pl.pallas_call(
        body, name=name, grid=(s // tm,),
        in_specs=[pl.BlockSpec((tm, d), lambda i: (i, col)), pl.BlockSpec((1, d), lambda i: (0, 0))],
        out_specs=pl.BlockSpec((tm, d), lambda i: (i, 0)),
        out_shape=jax.ShapeDtypeStruct((s, d), out_dtype),
        compiler_params=_cparams("parallel"))(x, g)


def rms_bwd(x, g, dy, *, col=0, add=None, out_dtype=F32, twin=False, name):
    s = x.shape[0]
    d = g.shape[1]
    tm = _rows(s)

    def body(*refs):
        refs = list(refs)
        dg_ref = refs.pop()
        dxh_ref = refs.pop() if twin else None
        dx_ref = refs.pop()
        add_ref = refs.pop() if add is not None else None
        x_ref, g_ref, dy_ref = refs

        @pl.when(pl.program_id(0) == 0)
        def _():
            dg_ref[...] = jnp.zeros_like(dg_ref)

        xv = x_ref[...]
        dyv = dy_ref[...].astype(F32)
        r = lax.rsqrt(jnp.mean(xv * xv, axis=-1, keepdims=True) + EPS)
        xh = xv * r
        dg_ref[...] += jnp.sum(dyv * xh, axis=0, keepdims=True)
        dxh = dyv * g_ref[...]
        dx = r * (dxh - xh * jnp.mean(dxh * xh, axis=-1, keepdims=True))
        if add is not None:
            dx = dx + add_ref[...]
        dx_ref[...] = dx.astype(out_dtype)
        if twin:
            dxh_ref[...] = dx.astype(BF16)

    row = pl.BlockSpec((tm, d), lambda i: (i, 0))
    vec = pl.BlockSpec((1, d), lambda i: (0, 0))
    ins = [x, g, dy]
    specs = [pl.BlockSpec((tm, d), lambda i: (i, col)), vec, row]
    if add is not None:
        ins.append(add)
        specs.append(row)
    dxs = [jax.ShapeDtypeStruct((s, d), out_dtype)] + ([jax.ShapeDtypeStruct((s, d), BF16)] if twin else [])
    return pl.pallas_call(
        body, name=name, grid=(s // tm,), in_specs=specs,
        out_specs=[row] * len(dxs) + [vec],
        out_shape=dxs + [jax.ShapeDtypeStruct((1, d), F32)],
        compiler_params=_cparams("arbitrary"))(*ins)


def _rope_tables(s):
    inv = 1.0 / (ROPE_THETA ** (jnp.arange(0, ROPE, 2, dtype=F32) / ROPE))
    ang = jnp.arange(s, dtype=F32)[:, None] * inv[None, :]
    cos, sin = jnp.cos(ang), jnp.sin(ang)
    z = lambda w: jnp.zeros((s, w), F32)
    c = jnp.concatenate([jnp.ones((s, NOPE), F32), cos, cos, z(HP - QK)], axis=1)
    s1 = jnp.concatenate([z(NOPE), -sin, z(HP - NOPE - ROPE // 2)], axis=1)
    s2 = jnp.concatenate([z(NOPE + ROPE // 2), sin, z(HP - QK)], axis=1)
    return c, s1, s2


def qkprep_fwd(qraw, kv, proj, qg, kg, tabs, *, name):
    s = qraw.shape[0]
    tm = _long_rows(s)
    kr_col = (proj.shape[1] - HP) // HP

    def body(q_ref, k_ref, v_ref, kr_ref, qg_ref, kg_ref, c_ref, s1_ref, s2_ref, qo_ref, ko_ref, vo_ref):
        c, s1, s2 = c_ref[...], s1_ref[...], s2_ref[...]

        def f(xv, gain):
            r = lax.rsqrt(jnp.sum(xv * xv, axis=-1, keepdims=True) * (1.0 / QK) + EPS)
            xn = xv * r * gain
            return xn * c + pltpu.roll(xn, HP - ROPE // 2, 1) * s1 + pltpu.roll(xn, ROPE // 2, 1) * s2

        qo_ref[...] = (f(q_ref[...], qg_ref[...]) * Q_SCALE).astype(BF16)
        ko_ref[...] = f(k_ref[...] + kr_ref[...], kg_ref[...]).astype(BF16)
        vv = v_ref[...]
        lane = lax.broadcasted_iota(jnp.int32, vv.shape, 1)
        vo_ref[...] = jnp.where(lane == VD, 1.0, vv).astype(BF16)

    head = pl.BlockSpec((tm, HP), lambda i, h: (i, h))
    tab = pl.BlockSpec((tm, HP), lambda i, h: (i, 0))
    gain = pl.BlockSpec((1, HP), lambda i, h: (0, 0))
    return pl.pallas_call(
        body, name=name, grid=(s // tm, HEADS),
        in_specs=[head, head, pl.BlockSpec((tm, HP), lambda i, h: (i, HEADS + h)),
                  pl.BlockSpec((tm, HP), lambda i, h: (i, kr_col)), gain, gain, tab, tab, tab],
        out_specs=[head, head, head],
        out_shape=[jax.ShapeDtypeStruct((s, HEADS * HP), BF16)] * 3,
        compiler_params=_cparams("parallel", "parallel"))(qraw, kv, kv, proj, qg, kg, *tabs)


def qkprep_bwd(qraw, kv, proj, qg, kg, tabs, dq, dk, *, name):
    s = qraw.shape[0]
    tm = _long_rows(s)
    kr_col = (proj.shape[1] - HP) // HP

    def body(q_ref, k_ref, kr_ref, qg_ref, kg_ref, c_ref, s1_ref, s2_ref, dq_ref, dk_ref,
             dqr_ref, dkr_ref, dkrope_ref, dqg_ref, dkg_ref):
        i, h = pl.program_id(0), pl.program_id(1)
        c, s1, s2 = c_ref[...], s1_ref[...], s2_ref[...]

        @pl.when((i == 0) & (h == 0))
        def _():
            dqg_ref[...] = jnp.zeros_like(dqg_ref)
            dkg_ref[...] = jnp.zeros_like(dkg_ref)

        @pl.when(h == 0)
        def _():
            dkrope_ref[...] = jnp.zeros_like(dkrope_ref)

        def f(xv, gain, dout):
            r = lax.rsqrt(jnp.sum(xv * xv, axis=-1, keepdims=True) * (1.0 / QK) + EPS)
            xh = xv * r
            dxn = dout * c + pltpu.roll(dout * s1, ROPE // 2, 1) + pltpu.roll(dout * s2, HP - ROPE // 2, 1)
            dgain = jnp.sum(dxn * xh, axis=0, keepdims=True)
            dxh = dxn * gain
            dx = r * (dxh - xh * (jnp.sum(dxh * xh, axis=-1, keepdims=True) * (1.0 / QK)))
            return dx, dgain

        dxq, dgq = f(q_ref[...], qg_ref[...], dq_ref[...])
        dxk, dgk = f(k_ref[...] + kr_ref[...], kg_ref[...], dk_ref[...])
        dqr_ref[...] = dxq.astype(BF16)
        dkr_ref[...] = dxk.astype(BF16)
        dqg_ref[...] += dgq
        dkg_ref[...] += dgk
        lane = lax.broadcasted_iota(jnp.int32, dxk.shape, 1)
        dkrope_ref[...] += jnp.where((lane >= NOPE) & (lane < QK), dxk, 0.0)

    head = pl.BlockSpec((tm, HP), lambda i, h: (i, h))
    tab = pl.BlockSpec((tm, HP), lambda i, h: (i, 0))
    gain = pl.BlockSpec((1, HP), lambda i, h: (0, 0))
    return pl.pallas_call(
        body, name=name, grid=(s // tm, HEADS),
        in_specs=[head, head, pl.BlockSpec((tm, HP), lambda i, h: (i, kr_col)), gain, gain, tab, tab, tab, head, head],
        out_specs=[head, head, tab, gain, gain],
        out_shape=[jax.ShapeDtypeStruct((s, HEADS * HP), BF16)] * 2
        + [jax.ShapeDtypeStruct((s, HP), F32), jax.ShapeDtypeStruct((1, HP), F32), jax.ShapeDtypeStruct((1, HP), F32)],
        compiler_params=_cparams("arbitrary", "arbitrary"))(qraw, kv, proj, qg, kg, *tabs, dq, dk)


ATT_SCALE = QK ** -0.5
LOG2E = math.log2(math.e)
Q_SCALE = ATT_SCALE * LOG2E
NEG = -1e30


def _att_tile(s):
    return _pick(s, (512, 256, 128))


def _causal(sv, diag):
    r = lax.broadcasted_iota(jnp.int32, sv.shape, 0)
    c = lax.broadcasted_iota(jnp.int32, sv.shape, 1)
    return jnp.where(diag & (c > r), NEG, sv)


NT = (((1,), (1,)), ((), ()))
TN = (((0,), (0,)), ((), ()))


def _row_of(col):
    return jnp.broadcast_to(col, (col.shape[0], LANE)).T[0:SLAB, :]


def _att_specs(s):
    t = _att_tile(s)
    nb = s // t
    tile = pl.BlockSpec((t, HP), lambda h, i: (i, h))
    whole = pl.BlockSpec((s, HP), lambda h, i: (0, h))
    row = pl.BlockSpec((1, 1, SLAB, t), lambda h, i: (h, i, 0, 0))
    rows = pl.BlockSpec((1, nb, SLAB, t), lambda h, i: (h, 0, 0, 0))
    return t, nb, tile, whole, row, rows


def attn_fwd(q, k, v, *, name):
    s = q.shape[0]
    t, nb, tile, whole, row, _ = _att_specs(s)

    def body(q_ref, k_ref, v_ref, o_ref, oh_ref, lset_ref, s_scr, mb_scr, acc):
        qb = pl.program_id(1)
        qv = q_ref[...]

        def scores(j):
            r0 = pl.multiple_of(j * t, t)
            return lax.dot_general(qv, k_ref[pl.ds(r0, t), :], NT, preferred_element_type=F32)

        def fold(sv):
            m = sv[:, 0:LANE]
            for kk in range(1, t // LANE):
                m = jnp.maximum(m, sv[:, kk * LANE:(kk + 1) * LANE])
            return m

        def first(j, m):
            sv = scores(j)
            s_scr[j] = sv
            return jnp.maximum(m, fold(sv))

        m = lax.fori_loop(0, qb, first, jnp.full((t, LANE), NEG, F32))
        sd = _causal(scores(qb), True)
        s_scr[qb] = sd
        mcol = jnp.max(jnp.maximum(m, fold(sd)), axis=-1, keepdims=True)
        mb_scr[...] = jnp.broadcast_to(mcol, (t, t))
        acc[...] = jnp.zeros_like(acc)

        def second(j, carry):
            r0 = pl.multiple_of(j * t, t)
            p = jnp.exp2(s_scr[j] - mb_scr[...]).astype(BF16)
            acc[...] += jnp.dot(p, v_ref[pl.ds(r0, t), :], preferred_element_type=F32)
            return carry

        lax.fori_loop(0, qb + 1, second, 0)
        av = acc[...]
        lsum = av[:, VD:VD + 1]
        lane = lax.broadcasted_iota(jnp.int32, av.shape, 1)
        ov = jnp.where(lane == VD, 0.0, av / lsum)
        o_ref[...] = ov
        oh_ref[...] = ov.astype(BF16)
        lset_ref[0, 0] = _row_of(mcol + jnp.log2(lsum))

    return pl.pallas_call(
        body, name=name, grid=(HEADS, nb), in_specs=[tile, whole, whole], out_specs=[tile, tile, row],
        out_shape=[jax.ShapeDtypeStruct((s, HEADS * HP), F32), jax.ShapeDtypeStruct((s, HEADS * HP), BF16),
                   jax.ShapeDtypeStruct((HEADS, nb, SLAB, t), F32)],
        scratch_shapes=[pltpu.VMEM((nb, t, t), F32), pltpu.VMEM((t, t), F32), pltpu.VMEM((t, HP), F32)],
        compiler_params=_cparams("parallel", "parallel"))(q, k, v)


def attn_delta(o, do, *, name):
    s = o.shape[0]
    t, nb, tile, _, row, _ = _att_specs(s)

    def body(o_ref, do_ref, doh_ref, dt_ref):
        dov = do_ref[...]
        doh_ref[...] = dov.astype(BF16)
        dt_ref[0, 0] = _row_of(jnp.sum(dov * o_ref[...], axis=-1, keepdims=True))

    return pl.pallas_call(
        body, name=name, grid=(HEADS, nb), in_specs=[tile, tile], out_specs=[tile, row],
        out_shape=[jax.ShapeDtypeStruct((s, HEADS * HP), BF16), jax.ShapeDtypeStruct((HEADS, nb, SLAB, t), F32)],
        compiler_params=_cparams("parallel", "parallel"))(o, do)


def attn_bwd(q, k, v, doh, lset, deltat, *, name):
    s = q.shape[0]
    t, nb, tile, whole, _, rows = _att_specs(s)

    def body(q_ref, k_ref, v_ref, do_ref, lt_ref, dt_ref, dq_ref, dk_ref, dv_ref, dk_acc, dv_acc):
        kb = pl.program_id(1)
        kt, vt = k_ref[...], v_ref[...]
        dk_acc[...] = jnp.zeros_like(dk_acc)
        dv_acc[...] = jnp.zeros_like(dv_acc)

        @pl.when(kb == 0)
        def _():
            dq_ref[...] = jnp.zeros_like(dq_ref)

        def step(i, diag):
            r0 = pl.multiple_of(i * t, t)
            qi, doi = q_ref[pl.ds(r0, t), :], do_ref[pl.ds(r0, t), :]
            st = lax.dot_general(kt, qi, NT, preferred_element_type=F32)
            if diag:
                kr = lax.broadcasted_iota(jnp.int32, st.shape, 0)
                qc = lax.broadcasted_iota(jnp.int32, st.shape, 1)
                st = jnp.where(kr > qc, NEG, st)
            pt = jnp.exp2(st - lt_ref[0, i][0:1, :])
            dpt = lax.dot_general(vt, doi, NT, preferred_element_type=F32)
            dst = (pt * (dpt - dt_ref[0, i][0:1, :])).astype(BF16)
            dv_acc[...] += jnp.dot(pt.astype(BF16), doi, preferred_element_type=F32)
            dk_acc[...] += jnp.dot(dst, qi, preferred_element_type=F32)
            dq_ref[pl.ds(r0, t), :] += lax.dot_general(dst, kt, TN, preferred_element_type=F32) * ATT_SCALE

        def off_diag(i, carry):
            step(i, False)
            return carry

        step(kb, True)
        lax.fori_loop(kb + 1, nb, off_diag, 0)
        dk_ref[...] = dk_acc[...] * (1.0 / LOG2E)
        dvv = dv_acc[...]
        lane = lax.broadcasted_iota(jnp.int32, dvv.shape, 1)
        dv_ref[...] = jnp.where(lane == VD, 0.0, dvv).astype(BF16)

    return pl.pallas_call(
        body, name=name, grid=(HEADS, nb), in_specs=[whole, tile, tile, whole, rows, rows],
        out_specs=[whole, tile, tile],
        out_shape=[jax.ShapeDtypeStruct((s, HEADS * HP), F32), jax.ShapeDtypeStruct((s, HEADS * HP), F32),
                   jax.ShapeDtypeStruct((s, HEADS * HP), BF16)],
        scratch_shapes=[pltpu.VMEM((t, HP), F32), pltpu.VMEM((t, HP), F32)],
        compiler_params=_cparams("parallel", "arbitrary"))(q, k, v, doh, lset, deltat)


HALO = 8
CW = 256


def _shifts(zw):
    return zw, pltpu.roll(zw, 1, 0), pltpu.roll(zw, 2, 0)


def _conv3(sh, w):
    return w[2:3] * sh[0] + w[1:2] * sh[1] + w[0:1] * sh[2]


def _conv3_t(dc, w):
    n = dc.shape[0]
    return w[2:3] * dc + w[1:2] * pltpu.roll(dc, n - 1, 0) + w[0:1] * pltpu.roll(dc, n - 2, 0)


def _conv3_dw(dc, sh, r):
    return [jnp.sum(dc * z[HALO:HALO + r], axis=0, keepdims=True) for z in (sh[2], sh[1], sh[0])]


def _halo_specs(r, colfn):
    rb = r // HALO
    cur = pl.BlockSpec((r, CW), lambda j, i: (i, colfn(j)))
    prev = pl.BlockSpec((HALO, CW), lambda j, i: (jnp.maximum(i * rb - 1, 0), colfn(j)))

    def nxt(nrow_blocks):
        return pl.BlockSpec((HALO, CW), lambda j, i: (jnp.minimum((i + 1) * rb, nrow_blocks * rb - 1), colfn(j)))

    return cur, prev, nxt


H_HALO = 16


def ffn_up_act(h, w_up, w, *, name):
    s, kdim = h.shape
    hh = w_up.shape[1] // 2
    nj = hh // CW
    r = _long_rows(s)
    nt = s // r
    rb = r // H_HALO

    def body(h_ref, hp_ref, bg_ref, bv_ref, wg_ref, wv_ref, ug_ref, uv_ref, o_ref):
        pm = (pl.program_id(1) > 0).astype(F32)
        hv, hp = h_ref[...], hp_ref[...]

        def half(b_ref, w_ref, u_ref):
            up = jnp.dot(hv, b_ref[...], preferred_element_type=F32)
            u_ref[...] = up
            prev = jnp.dot(hp, b_ref[...], preferred_element_type=F32) * pm
            return _conv3(_shifts(jnp.concatenate([prev, up], axis=0)), w_ref[...])[H_HALO:]

        cg = half(bg_ref, wg_ref, ug_ref)
        cv = half(bv_ref, wv_ref, uv_ref)
        o_ref[...] = (cg * jax.nn.sigmoid(cg) * cv).astype(BF16)

    tile = pl.BlockSpec((r, CW), lambda j, i: (i, j))
    return pl.pallas_call(
        body, name=name, grid=(nj, nt),
        in_specs=[pl.BlockSpec((r, kdim), lambda j, i: (i, 0)),
                  pl.BlockSpec((H_HALO, kdim), lambda j, i: (jnp.maximum(i * rb - 1, 0), 0)),
                  pl.BlockSpec((kdim, CW), lambda j, i: (0, j)), pl.BlockSpec((kdim, CW), lambda j, i: (0, nj + j)),
                  pl.BlockSpec((3, CW), lambda j, i: (0, j)), pl.BlockSpec((3, CW), lambda j, i: (0, nj + j))],
        out_specs=[tile, tile, tile],
        out_shape=[jax.ShapeDtypeStruct((s, hh), F32), jax.ShapeDtypeStruct((s, hh), F32),
                   jax.ShapeDtypeStruct((s, hh), BF16)],
        compiler_params=_cparams("parallel", "parallel"))(h, h, w_up, w_up, w, w)


def ffnact_bwd(up, w, dact, *, name):
    upg, upv = up
    s, hh = upg.shape
    nj = hh // CW
    r = _rows(s)
    nt = s // r

    def body(g_ref, gp_ref, gn_ref, v_ref, vp_ref, vn_ref, wg_ref, wv_ref, da_ref, dan_ref,
             dg_ref, dv_ref, dwg_ref, dwv_ref):
        i = pl.program_id(1)
        pm = (i > 0).astype(F32)
        nm = (i < nt - 1).astype(F32)

        @pl.when(i == 0)
        def _():
            dwg_ref[...] = jnp.zeros_like(dwg_ref)
            dwv_ref[...] = jnp.zeros_like(dwv_ref)

        wg, wv = wg_ref[...], wv_ref[...]
        zg = jnp.concatenate([gp_ref[...] * pm, g_ref[...], gn_ref[...]], axis=0)
        zv = jnp.concatenate([vp_ref[...] * pm, v_ref[...], vn_ref[...]], axis=0)
        zg, zv = _shifts(zg), _shifts(zv)
        cg = _conv3(zg, wg)[HALO:]
        cv = _conv3(zv, wv)[HALO:]
        da = jnp.concatenate([da_ref[...], dan_ref[...] * nm], axis=0)
        sg = jax.nn.sigmoid(cg)
        dcg = da * cv * (sg * (1.0 + cg * (1.0 - sg)))
        dcv = da * (cg * sg)
        dg_ref[...] = _conv3_t(dcg, wg)[:r].astype(BF16)
        dv_ref[...] = _conv3_t(dcv, wv)[:r].astype(BF16)
        for kk, (a, b) in enumerate(zip(_conv3_dw(dcg[:r], zg, r), _conv3_dw(dcv[:r], zv, r))):
            dwg_ref[kk:kk + 1, :] += a
            dwv_ref[kk:kk + 1, :] += b

    gcur, gprev, gnext = _halo_specs(r, lambda j: j)
    vcur, vprev, vnext = gcur, gprev, gnext
    acur, _, anext = _halo_specs(r, lambda j: j)
    wg = pl.BlockSpec((3, CW), lambda j, i: (0, j))
    wv = pl.BlockSpec((3, CW), lambda j, i: (0, nj + j))
    dupg, dupv, dwg, dwv = pl.pallas_call(
        body, name=name, grid=(nj, nt),
        in_specs=[gcur, gprev, gnext(nt), vcur, vprev, vnext(nt), wg, wv, acur, anext(nt)],
        out_specs=[acur, acur, wg, wg],
        out_shape=[jax.ShapeDtypeStruct((s, hh), BF16), jax.ShapeDtypeStruct((s, hh), BF16),
                   jax.ShapeDtypeStruct((3, hh), F32), jax.ShapeDtypeStruct((3, hh), F32)],
        compiler_params=_cparams("parallel", "arbitrary"))(upg, upg, upg, upv, upv, upv, w, w, dact, dact)
    return (dupg, dupv), jnp.concatenate([dwg, dwv], axis=1)


def sconv_fwd(proj, w, *, name):
    s = proj.shape[0]
    nj = CONVC // CW
    r = _long_rows(s)
    nt = s // r

    def body(b_ref, c_ref, cp_ref, x_ref, xp_ref, w_ref, o_ref):
        pm = (pl.program_id(1) > 0).astype(F32)
        zw = jnp.concatenate([cp_ref[...] * xp_ref[...] * pm, c_ref[...] * x_ref[...]], axis=0)
        o_ref[...] = (b_ref[...] * _conv3(_shifts(zw), w_ref[...])[HALO:]).astype(BF16)

    bcur, _, _ = _halo_specs(r, lambda j: (QR + KVR) // CW + j)
    ccur, cprev, _ = _halo_specs(r, lambda j: (QR + KVR + CONVC) // CW + j)
    xcur, xprev, _ = _halo_specs(r, lambda j: (QR + KVR + 2 * CONVC) // CW + j)
    ws = pl.BlockSpec((3, CW), lambda j, i: (0, j))
    return pl.pallas_call(
        body, name=name, grid=(nj, nt), in_specs=[bcur, ccur, cprev, xcur, xprev, ws],
        out_specs=pl.BlockSpec((r, CW), lambda j, i: (i, j)),
        out_shape=jax.ShapeDtypeStruct((s, CONVC), BF16),
        compiler_params=_cparams("parallel", "parallel"))(proj, proj, proj, proj, proj, w)


def sconv_bwd(proj, w, dy, *, name):
    s = proj.shape[0]
    nj = CONVC // CW
    r = _rows(s)
    nt = s // r

    def body(b_ref, bn_ref, c_ref, cp_ref, x_ref, xp_ref, w_ref, dy_ref, dyn_ref, db_ref, dc_ref, dx_ref, dw_ref):
        i = pl.program_id(1)
        pm = (i > 0).astype(F32)
        nm = (i < nt - 1).astype(F32)

        @pl.when(i == 0)
        def _():
            dw_ref[...] = jnp.zeros_like(dw_ref)

        wv = w_ref[...]
        zw = jnp.concatenate([cp_ref[...] * xp_ref[...] * pm, c_ref[...] * x_ref[...]], axis=0)
        zw = _shifts(zw)
        conv = _conv3(zw, wv)[HALO:]
        dyv = dy_ref[...]
        db_ref[...] = (dyv * conv).astype(BF16)
        dconv = jnp.concatenate([dyv * b_ref[...], dyn_ref[...] * bn_ref[...] * nm], axis=0)
        dz = _conv3_t(dconv, wv)[:r]
        dc_ref[...] = (dz * x_ref[...]).astype(BF16)
        dx_ref[...] = (dz * c_ref[...]).astype(BF16)
        for kk, a in enumerate(_conv3_dw(dconv[:r], zw, r)):
            dw_ref[kk:kk + 1, :] += a

    bcur, _, bnext = _halo_specs(r, lambda j: (QR + KVR) // CW + j)
    ccur, cprev, _ = _halo_specs(r, lambda j: (QR + KVR + CONVC) // CW + j)
    xcur, xprev, _ = _halo_specs(r, lambda j: (QR + KVR + 2 * CONVC) // CW + j)
    ycur, _, ynext = _halo_specs(r, lambda j: j)
    ws = pl.BlockSpec((3, CW), lambda j, i: (0, j))
    out = pl.BlockSpec((r, CW), lambda j, i: (i, j))
    db, dc, dx, dw = pl.pallas_call(
        body, name=name, grid=(nj, nt),
        in_specs=[bcur, bnext(nt), ccur, cprev, xcur, xprev, ws, ycur, ynext(nt)],
        out_specs=[out, out, out, ws],
        out_shape=[jax.ShapeDtypeStruct((s, CONVC), BF16)] * 3 + [jax.ShapeDtypeStruct((3, CONVC), F32)],
        compiler_params=_cparams("parallel", "arbitrary"))(proj, proj, proj, proj, proj, proj, w, dy, dy)
    return (db, dc, dx), dw


SW = 512
NJ = NST // SW


def _scan_tables(ar, ai):
    def cmul(x, y):
        return x[0] * y[0] - x[1] * y[1], x[0] * y[1] + x[1] * y[0]

    def build(a, reverse):
        pw = [a]
        for _ in range(SLAB - 1):
            pw.append(cmul(pw[-1], a))
        row = jnp.arange(SLAB)[:, None]
        tabs = []
        for kk in (1, 2, 4):
            mask = ((row < SLAB - kk) if reverse else (row >= kk)).astype(F32)
            tabs += [mask * pw[kk - 1][0][None, :], mask * pw[kk - 1][1][None, :]]
        order = list(range(SLAB - 1, -1, -1)) if reverse else list(range(SLAB))
        tabs += [jnp.stack([pw[o][0] for o in order]), jnp.stack([pw[o][1] for o in order])]
        return jnp.stack(tabs)

    return build((ar, ai), False), build((ar, -ai), True)


def _slab_scan(xr, xi, tabs, cr, ci, reverse):
    for n, kk in enumerate((1, 2, 4)):
        sh = SLAB - kk if reverse else kk
        tr, ti = tabs[2 * n], tabs[2 * n + 1]
        sr, si = pltpu.roll(xr, sh, 0), pltpu.roll(xi, sh, 0)
        xr, xi = xr + tr * sr - ti * si, xi + tr * si + ti * sr
    tr, ti = tabs[6], tabs[7]
    return xr + tr * cr - ti * ci, xi + tr * ci + ti * cr


def s5_fwd(u, bbd_r, bbd_i, cbd_r, cbd_i, tab, *, name):
    s = u.shape[0]
    tbk = _long_rows(s)
    nt = s // tbk
    nsl = tbk // SLAB

    def body(u_ref, br_ref, bi_ref, cr_ref, ci_ref, tab_ref, y_ref, sr_ref, si_ref, bur, bui, carry):
        @pl.when(pl.program_id(1) == 0)
        def _():
            carry[...] = jnp.zeros_like(carry)

        ub = u_ref[...].astype(BF16)
        bur[...] = jnp.dot(ub, br_ref[0], preferred_element_type=F32)
        bui[...] = jnp.dot(ub, bi_ref[0], preferred_element_type=F32)
        tabs = [tab_ref[n] for n in range(8)]

        def slab(n, c):
            r0 = pl.multiple_of(n * SLAB, SLAB)
            sr, si = _slab_scan(bur[pl.ds(r0, SLAB), :], bui[pl.ds(r0, SLAB), :], tabs, c[0], c[1], False)
            sr_ref[pl.ds(r0, SLAB), :] = sr
            si_ref[pl.ds(r0, SLAB), :] = si
            return (jnp.broadcast_to(sr[SLAB - 1:SLAB], sr.shape), jnp.broadcast_to(si[SLAB - 1:SLAB], si.shape))

        cr, ci = lax.fori_loop(0, nsl, slab, (carry[0], carry[1]))
        carry[0] = cr
        carry[1] = ci
        y_ref[...] = (jnp.dot(sr_ref[...].astype(BF16), cr_ref[0], preferred_element_type=F32)
                      - jnp.dot(si_ref[...].astype(BF16), ci_ref[0], preferred_element_type=F32))

    us = pl.BlockSpec((tbk, LANE), lambda j, t: (t, j))
    bs = pl.BlockSpec((1, LANE, SW), lambda j, t: (j, 0, 0))
    cs = pl.BlockSpec((1, SW, LANE), lambda j, t: (j, 0, 0))
    ts = pl.BlockSpec((8, SLAB, SW), lambda j, t: (0, 0, j))
    ss = pl.BlockSpec((tbk, SW), lambda j, t: (t, j))
    return pl.pallas_call(
        body, name=name, grid=(NJ, nt), in_specs=[us, bs, bs, cs, cs, ts], out_specs=[us, ss, ss],
        out_shape=[jax.ShapeDtypeStruct((s, D_MODEL), F32), jax.ShapeDtypeStruct((s, NST), F32),
                   jax.ShapeDtypeStruct((s, NST), F32)],
        scratch_shapes=[pltpu.VMEM((tbk, SW), F32), pltpu.VMEM((tbk, SW), F32), pltpu.VMEM((2, SLAB, SW), F32)],
        compiler_params=_cparams("parallel", "arbitrary"))(u, bbd_r, bbd_i, cbd_r, cbd_i, tab)


def s5_bwd(u, dy, dskip, st_r, st_i, bbd_r, bbd_i, cbd_r, cbd_i, tabrev, *, name):
    s = u.shape[0]
    tbk = _long_rows(s)
    nt = s // tbk
    nsl = tbk // SLAB
    rbk = tbk // SLAB

    def body(u_ref, dy_ref, d_ref, sr_ref, si_ref, pr_ref, pi_ref, br_ref, bi_ref, cr_ref, ci_ref, tab_ref,
             du_ref, dbr_ref, dbi_ref, dcr_ref, dci_ref, da_ref, lam_r, lam_i, carry):
        t = pl.program_id(1)

        @pl.when(t == 0)
        def _():
            carry[...] = jnp.zeros_like(carry)
            dbr_ref[...] = jnp.zeros_like(dbr_ref)
            dbi_ref[...] = jnp.zeros_like(dbi_ref)
            dcr_ref[...] = jnp.zeros_like(dcr_ref)
            dci_ref[...] = jnp.zeros_like(dci_ref)
            da_ref[...] = jnp.zeros_like(da_ref)

        dyv = dy_ref[...]
        dyh = dyv.astype(BF16)
        lam_r[...] = lax.dot_general(dyh, cr_ref[0], NT, preferred_element_type=F32)
        lam_i[...] = -lax.dot_general(dyh, ci_ref[0], NT, preferred_element_type=F32)
        tabs = [tab_ref[n] for n in range(8)]

        def slab(n, c):
            r0 = pl.multiple_of((nsl - 1 - n) * SLAB, SLAB)
            lr, li = _slab_scan(lam_r[pl.ds(r0, SLAB), :], lam_i[pl.ds(r0, SLAB), :], tabs, c[0], c[1], True)
            lam_r[pl.ds(r0, SLAB), :] = lr
            lam_i[pl.ds(r0, SLAB), :] = li
            return (jnp.broadcast_to(lr[0:1], lr.shape), jnp.broadcast_to(li[0:1], li.shape))

        cr, ci = lax.fori_loop(0, nsl, slab, (carry[0], carry[1]))
        carry[0] = cr
        carry[1] = ci
        lr, li = lam_r[...], lam_i[...]
        lrh, lih = lr.astype(BF16), li.astype(BF16)
        du = (dyv * d_ref[...] + lax.dot_general(lrh, br_ref[0], NT, preferred_element_type=F32)
              + lax.dot_general(lih, bi_ref[0], NT, preferred_element_type=F32))
        du_ref[...] = du.astype(BF16)
        ub = u_ref[...].astype(BF16)
        dbr_ref[0] += lax.dot_general(ub, lrh, TN, preferred_element_type=F32)
        dbi_ref[0] += lax.dot_general(ub, lih, TN, preferred_element_type=F32)
        srv, siv = sr_ref[...], si_ref[...]
        dcr_ref[0] += lax.dot_general(srv.astype(BF16), dyh, TN, preferred_element_type=F32)
        dci_ref[0] -= lax.dot_general(siv.astype(BF16), dyh, TN, preferred_element_type=F32)
        first = lax.broadcasted_iota(jnp.int32, srv.shape, 0) == 0
        pm = (t < nt - 1).astype(F32)
        spr = jnp.where(first, pr_ref[SLAB - 1:SLAB, :] * pm, pltpu.roll(srv, 1, 0))
        spi = jnp.where(first, pi_ref[SLAB - 1:SLAB, :] * pm, pltpu.roll(siv, 1, 0))
        da_ref[0:1, :] += jnp.sum(lr * spr + li * spi, axis=0, keepdims=True)
        da_ref[1:2, :] += jnp.sum(li * spr - lr * spi, axis=0, keepdims=True)

    rv = lambda t: nt - 1 - t
    us = pl.BlockSpec((tbk, LANE), lambda j, t: (rv(t), j))
    ds = pl.BlockSpec((1, LANE), lambda j, t: (0, j))
    ss = pl.BlockSpec((tbk, SW), lambda j, t: (rv(t), j))
    ps = pl.BlockSpec((SLAB, SW), lambda j, t: (jnp.maximum(rv(t) * rbk - 1, 0), j))
    bs = pl.BlockSpec((1, LANE, SW), lambda j, t: (j, 0, 0))
    cs = pl.BlockSpec((1, SW, LANE), lambda j, t: (j, 0, 0))
    ts = pl.BlockSpec((8, SLAB, SW), lambda j, t: (0, 0, j))
    das = pl.BlockSpec((2, SW), lambda j, t: (0, j))
    return pl.pallas_call(
        body, name=name, grid=(NJ, nt),
        in_specs=[us, us, ds, ss, ss, ps, ps, bs, bs, cs, cs, ts],
        out_specs=[us, bs, bs, cs, cs, das],
        out_shape=[jax.ShapeDtypeStruct((s, D_MODEL), BF16),
                   jax.ShapeDtypeStruct((NJ, LANE, SW), F32), jax.ShapeDtypeStruct((NJ, LANE, SW), F32),
                   jax.ShapeDtypeStruct((NJ, SW, LANE), F32), jax.ShapeDtypeStruct((NJ, SW, LANE), F32),
                   jax.ShapeDtypeStruct((2, NST), F32)],
        scratch_shapes=[pltpu.VMEM((tbk, SW), F32), pltpu.VMEM((tbk, SW), F32), pltpu.VMEM((2, SLAB, SW), F32)],
        compiler_params=_cparams("parallel", "arbitrary"))(
            u, dy, dskip, st_r, st_i, st_r, st_i, bbd_r, bbd_i, cbd_r, cbd_i, tabrev)


GELU_C = math.sqrt(2.0 / math.pi)
GELU_A = 0.044715


def s5post_fwd(y, u, dskip, *, name):
    s = y.shape[0]
    tm = _rows(s)

    def body(y_ref, u_ref, d_ref, o_ref):
        z = y_ref[...] + d_ref[...] * u_ref[...]
        o_ref[...] = (0.5 * z * (1.0 + jnp.tanh(GELU_C * (z + GELU_A * z * z * z)))).astype(BF16)

    row = pl.BlockSpec((tm, D_MODEL), lambda i: (i, 0))
    vec = pl.BlockSpec((1, D_MODEL), lambda i: (0, 0))
    return pl.pallas_call(body, name=name, grid=(s // tm,), in_specs=[row, row, vec], out_specs=row,
                          out_shape=jax.ShapeDtypeStruct((s, D_MODEL), BF16),
                          compiler_params=_cparams("parallel"))(y, u, dskip)


def s5post_bwd(y, u, dskip, dg, *, name):
    s = y.shape[0]
    tm = _rows(s)

    def body(y_ref, u_ref, d_ref, dg_ref, dz_ref, dd_ref):
        @pl.when(pl.program_id(0) == 0)
        def _():
            dd_ref[...] = jnp.zeros_like(dd_ref)

        uv = u_ref[...]
        z = y_ref[...] + d_ref[...] * uv
        th = jnp.tanh(GELU_C * (z + GELU_A * z * z * z))
        dgelu = 0.5 * (1.0 + th) + 0.5 * z * (1.0 - th * th) * (GELU_C * (1.0 + 3.0 * GELU_A * z * z))
        dz = dg_ref[...] * dgelu
        dz_ref[...] = dz
        dd_ref[...] += jnp.sum(dz * uv, axis=0, keepdims=True)

    row = pl.BlockSpec((tm, D_MODEL), lambda i: (i, 0))
    vec = pl.BlockSpec((1, D_MODEL), lambda i: (0, 0))
    return pl.pallas_call(body, name=name, grid=(s // tm,), in_specs=[row, row, vec, row], out_specs=[row, vec],
                          out_shape=[jax.ShapeDtypeStruct((s, D_MODEL), F32), jax.ShapeDtypeStruct((1, D_MODEL), F32)],
                          compiler_params=_cparams("arbitrary"))(y, u, dskip, dg)


def glu_fwd(glu, x, *, name):
    s = x.shape[0]
    tm = _rows(s)

    def body(a_ref, b_ref, x_ref, o_ref):
        o_ref[...] = x_ref[...] + a_ref[...] * jax.nn.sigmoid(b_ref[...])

    row = pl.BlockSpec((tm, D_MODEL), lambda i: (i, 0))
    return pl.pallas_call(body, name=name, grid=(s // tm,),
                          in_specs=[row, pl.BlockSpec((tm, D_MODEL), lambda i: (i, 1)), row], out_specs=row,
                          out_shape=jax.ShapeDtypeStruct((s, D_MODEL), F32),
                          compiler_params=_cparams("parallel"))(glu, glu, x)


def glu_bwd(glu, dx, *, name):
    s = dx.shape[0]
    tm = _rows(s)

    def body(a_ref, b_ref, dx_ref, o_ref):
        sg = jax.nn.sigmoid(b_ref[...])
        dxv = dx_ref[...]
        o_ref[:, :D_MODEL] = (dxv * sg).astype(BF16)
        o_ref[:, D_MODEL:] = (dxv * a_ref[...] * sg * (1.0 - sg)).astype(BF16)

    row = pl.BlockSpec((tm, D_MODEL), lambda i: (i, 0))
    return pl.pallas_call(body, name=name, grid=(s // tm,),
                          in_specs=[row, pl.BlockSpec((tm, D_MODEL), lambda i: (i, 1)), row],
                          out_specs=pl.BlockSpec((tm, 2 * D_MODEL), lambda i: (i, 0)),
                          out_shape=jax.ShapeDtypeStruct((s, 2 * D_MODEL), BF16),
                          compiler_params=_cparams("parallel"))(glu, glu, dx)


def loss_head(y, target, *, name):
    s = y.shape[0]
    tm = _rows(s)

    def body(y_ref, t_ref, dy_ref, dyh_ref, l_ref):
        @pl.when(pl.program_id(0) == 0)
        def _():
            l_ref[...] = jnp.zeros_like(l_ref)

        e = y_ref[...] - t_ref[...]
        dy_ref[...] = e * (1.0 / D_MODEL)
        dyh_ref[...] = (e * (1.0 / D_MODEL)).astype(BF16)
        e2 = jnp.sum((e * e).reshape(tm // 8, 8, D_MODEL), axis=0)
        acc = e2[:, 0:LANE]
        for kk in range(1, D_MODEL // LANE):
            acc = acc + e2[:, kk * LANE:(kk + 1) * LANE]
        l_ref[...] += acc

    row = pl.BlockSpec((tm, D_MODEL), lambda i: (i, 0))
    return pl.pallas_call(body, name=name, grid=(s // tm,), in_specs=[row, row],
                          out_specs=[row, row, pl.BlockSpec((8, LANE), lambda i: (0, 0))],
                          out_shape=[jax.ShapeDtypeStruct((s, D_MODEL), F32), jax.ShapeDtypeStruct((s, D_MODEL), BF16),
                                     jax.ShapeDtypeStruct((8, LANE), F32)],
                          compiler_params=_cparams("arbitrary"))(y, target)


NCHIP = 4


def _mesh_pos():
    return lax.axis_index("x"), lax.axis_index("y"), lax.axis_index("c")


def _chip_exchange(bufs, scatter, name):
    n = len(bufs)
    shapes = [b.shape[1:] if scatter else b.shape for b in bufs]

    def body(*refs):
        ins, outs = refs[:n], refs[n:2 * n]
        send_sems, recv_sems, local_sems = refs[2 * n:]
        x, y, c = _mesh_pos()
        me = 2 * x + y
        peers = [(1 - x, y), (x, 1 - y), (1 - x, 1 - y)]

        def copy(a, j, px, py, dst_slot):
            src = ins[a].at[2 * px + py] if scatter else ins[a]
            return pltpu.make_async_remote_copy(src_ref=src, dst_ref=outs[a].at[dst_slot],
                                                send_sem=send_sems.at[3 * a + j], recv_sem=recv_sems.at[3 * a + j],
                                                device_id=(px, py, c), device_id_type=MESH)

        mine = [pltpu.make_async_copy(ins[a].at[me] if scatter else ins[a], outs[a].at[me], local_sems.at[a])
                for a in range(n)]
        sends = [copy(a, j, px, py, me) for a in range(n) for j, (px, py) in enumerate(peers)]
        for cp in mine + sends:
            cp.start()
        for a in range(n):
            for j, (px, py) in enumerate(peers):
                copy(a, j, px, py, 2 * px + py).wait_recv()
        for cp in sends:
            cp.wait_send()
        for cp in mine:
            cp.wait()

    return pl.pallas_call(
        body, name=name, in_specs=[ANY] * n, out_specs=[ANY] * n,
        out_shape=[jax.ShapeDtypeStruct((NCHIP,) + tuple(shp), b.dtype) for shp, b in zip(shapes, bufs)],
        scratch_shapes=[pltpu.SemaphoreType.DMA((3 * n,)), pltpu.SemaphoreType.DMA((3 * n,)),
                        pltpu.SemaphoreType.DMA((n,))],
    )(*bufs)


HBM_SPEC = pl.BlockSpec(memory_space=pltpu.HBM)
SEM_SPEC = pl.BlockSpec(memory_space=pltpu.SEMAPHORE)
DATAFLOW = pltpu.SideEffectType.DATAFLOW_SIDE_EFFECTING


def _exchange_copy(ins, lands, send_sems, recv_sems, scatter, a, j, px, py, c, dst_slot):
    src = ins[a].at[2 * px + py] if scatter else ins[a]
    return pltpu.make_async_remote_copy(src_ref=src, dst_ref=lands[a].at[dst_slot],
                                        send_sem=send_sems.at[3 * a + j], recv_sem=recv_sems.at[3 * a + j],
                                        device_id=(px, py, c), device_id_type=MESH)


def _own_copy(ins, lands, local_sems, scatter, a, me):
    return pltpu.make_async_copy(ins[a].at[me] if scatter else ins[a], lands[a].at[me], local_sems.at[a])


def exchange_start(bufs, scatter, name):
    n = len(bufs)
    lands = [lax.empty((NCHIP,) + tuple(b.shape[1:] if scatter else b.shape), b.dtype) for b in bufs]

    def body(*refs):
        ins, lnd, send_sems, recv_sems, local_sems, token = (refs[:n], refs[n:2 * n], refs[2 * n], refs[2 * n + 1],
                                                             refs[2 * n + 2], refs[-1])
        x, y, c = _mesh_pos()
        me = 2 * x + y
        for a in range(n):
            _own_copy(ins, lnd, local_sems, scatter, a, me).start()
            for j, (px, py) in enumerate([(1 - x, y), (x, 1 - y), (1 - x, 1 - y)]):
                _exchange_copy(ins, lnd, send_sems, recv_sems, scatter, a, j, px, py, c, me).start()
        token[...] = jnp.zeros_like(token)

    thru = [pltpu.HBM(b.shape, b.dtype) for b in list(bufs) + lands]
    out = pl.pallas_call(
        body, name=name, in_specs=[HBM_SPEC] * (2 * n),
        out_specs=[SEM_SPEC] * 3 + [HBM_SPEC] * (2 * n) + [pl.BlockSpec(memory_space=pltpu.VMEM)],
        out_shape=[pltpu.SemaphoreType.DMA((3 * n,)), pltpu.SemaphoreType.DMA((3 * n,)), pltpu.SemaphoreType.DMA((n,))]
        + thru + [jax.ShapeDtypeStruct((SLAB, LANE), F32)],
        input_output_aliases={k: 3 + k for k in range(2 * n)},
        compiler_params=pltpu.CompilerParams(has_side_effects=DATAFLOW),
    )(*[pltpu.with_memory_space_constraint(b, pltpu.HBM) for b in list(bufs) + lands])
    return tuple(out[:3]), out[3:3 + n], out[3 + n:3 + 2 * n], out[-1][0, 0]


def exchange_wait(started, after, scatter, name):
    sems, bufs, lands, _ = started
    n = len(bufs)

    def body(*refs):
        ins, lnd, ssem, rsem, lsem = refs[:n], refs[n:2 * n], refs[2 * n], refs[2 * n + 1], refs[2 * n + 2]
        x, y, c = _mesh_pos()
        for a in range(n):
            _own_copy(ins, lnd, lsem, scatter, a, 2 * x + y).wait()
            for j, (px, py) in enumerate([(1 - x, y), (x, 1 - y), (1 - x, 1 - y)]):
                cp = _exchange_copy(ins, lnd, ssem, rsem, scatter, a, j, px, py, c, 2 * px + py)
                cp.wait_send()
                cp.wait_recv()

    thru = [pltpu.HBM(b.shape, b.dtype) for b in list(bufs) + list(lands)]
    out = pl.pallas_call(
        body, name=name, in_specs=[HBM_SPEC] * (2 * n) + [SEM_SPEC] * 3 + [ANY],
        out_specs=[HBM_SPEC] * (2 * n), out_shape=thru,
        input_output_aliases={k: k for k in range(2 * n)},
        compiler_params=pltpu.CompilerParams(has_side_effects=DATAFLOW),
    )(*bufs, *lands, *sems, after)
    return out[n:]


def sibling_swap(bufs, name):
    n = len(bufs)

    def body(*refs):
        ins, outs, send_sems, recv_sems = refs[:n], refs[n:2 * n], refs[2 * n], refs[2 * n + 1]
        x, y, c = _mesh_pos()
        cps = [pltpu.make_async_remote_copy(src_ref=ins[k], dst_ref=outs[k], send_sem=send_sems.at[k],
                                            recv_sem=recv_sems.at[k], device_id=(x, y, 1 - c), device_id_type=MESH)
               for k in range(n)]
        for cp in cps:
            cp.start()
        for cp in cps:
            cp.wait()

    return pl.pallas_call(
        body, name=name, in_specs=[ANY] * n, out_specs=[ANY] * n,
        out_shape=[jax.ShapeDtypeStruct(b.shape, b.dtype) for b in bufs],
        scratch_shapes=[pltpu.SemaphoreType.DMA((n,)), pltpu.SemaphoreType.DMA((n,))],
    )(*bufs)


def _swap_copy(ins, lands, send_sems, recv_sems, a):
    x, y, c = _mesh_pos()
    return pltpu.make_async_remote_copy(src_ref=ins[a], dst_ref=lands[a], send_sem=send_sems.at[a],
                                        recv_sem=recv_sems.at[a], device_id=(x, y, 1 - c), device_id_type=MESH)


def swap_start(bufs, name):
    n = len(bufs)
    lands = [lax.empty(b.shape, b.dtype) for b in bufs]

    def body(*refs):
        for a in range(n):
            _swap_copy(refs[:n], refs[n:2 * n], refs[2 * n], refs[2 * n + 1], a).start()
        refs[-1][...] = jnp.zeros_like(refs[-1])

    thru = [pltpu.HBM(b.shape, b.dtype) for b in list(bufs) + lands]
    out = pl.pallas_call(
        body, name=name, in_specs=[HBM_SPEC] * (2 * n),
        out_specs=[SEM_SPEC] * 2 + [HBM_SPEC] * (2 * n) + [pl.BlockSpec(memory_space=pltpu.VMEM)],
        out_shape=[pltpu.SemaphoreType.DMA((n,)), pltpu.SemaphoreType.DMA((n,))] + thru
        + [jax.ShapeDtypeStruct((SLAB, LANE), F32)],
        input_output_aliases={k: 2 + k for k in range(2 * n)},
        compiler_params=pltpu.CompilerParams(has_side_effects=DATAFLOW),
    )(*[pltpu.with_memory_space_constraint(b, pltpu.HBM) for b in list(bufs) + lands])
    return tuple(out[:2]), out[2:2 + n], out[2 + n:2 + 2 * n], out[-1][0, 0]


def swap_wait(started, after, name):
    sems, bufs, lands, _ = started
    n = len(bufs)

    def body(*refs):
        for a in range(n):
            _swap_copy(refs[:n], refs[n:2 * n], refs[2 * n], refs[2 * n + 1], a).wait()

    thru = [pltpu.HBM(b.shape, b.dtype) for b in list(bufs) + list(lands)]
    out = pl.pallas_call(
        body, name=name, in_specs=[HBM_SPEC] * (2 * n) + [SEM_SPEC] * 2 + [ANY],
        out_specs=[HBM_SPEC] * (2 * n), out_shape=thru,
        input_output_aliases={k: k for k in range(2 * n)},
        compiler_params=pltpu.CompilerParams(has_side_effects=DATAFLOW),
    )(*bufs, *lands, *sems, after)
    return out[n:]


EW_VMEM_BUDGET = 20 * 1024 * 1024


def _ew_rows(rows, w, bytes_per_elem):
    wpad = -(-w // LANE) * LANE
    for t in (1024, 512, 256, 128, 64, 32, 16, 8):
        if rows % t == 0 and 2 * t * wpad * bytes_per_elem <= EW_VMEM_BUDGET:
            return t
    return rows


def sum_slots(buf, *, name):
    _, rows, w = buf.shape
    tm = _ew_rows(rows, w, NCHIP * buf.dtype.itemsize + 4)

    def body(b_ref, o_ref):
        acc = b_ref[0].astype(F32)
        for kk in range(1, NCHIP):
            acc = acc + b_ref[kk].astype(F32)
        o_ref[...] = acc

    return pl.pallas_call(body, name=name, grid=(rows // tm,),
                          in_specs=[pl.BlockSpec((NCHIP, tm, w), lambda i: (0, i, 0))],
                          out_specs=pl.BlockSpec((tm, w), lambda i: (i, 0)),
                          out_shape=jax.ShapeDtypeStruct((rows, w), F32),
                          compiler_params=_cparams("parallel"))(buf)


def adamw(p_mine, p_other, w, m, v, *, name):
    rows, wd = w.shape
    tm = _ew_rows(rows, wd, 9 * 4)
    c1 = 1.0 - ADAM_B1 ** ADAM_STEP
    c2 = 1.0 - ADAM_B2 ** ADAM_STEP

    def body(a_ref, b_ref, w_ref, m_ref, v_ref, g_ref, d_ref, nm_ref, nv_ref):
        g = a_ref[...] + b_ref[...]
        nm = ADAM_B1 * m_ref[...] + (1.0 - ADAM_B1) * g
        nv = ADAM_B2 * v_ref[...] + (1.0 - ADAM_B2) * (g * g)
        g_ref[...] = g
        nm_ref[...] = nm
        nv_ref[...] = nv
        d_ref[...] = -ADAM_LR * ((nm / c1) / (jnp.sqrt(nv / c2) + ADAM_EPS) + ADAM_WD * w_ref[...])

    row = pl.BlockSpec((tm, wd), lambda i: (i, 0))
    return pl.pallas_call(body, name=name, grid=(rows // tm,), in_specs=[row] * 5, out_specs=[row] * 4,
                          out_shape=[jax.ShapeDtypeStruct((rows, wd), F32)] * 4,
                          compiler_params=_cparams("parallel"))(p_mine, p_other, w, m, v)


def _rows2d(a, lead=0):
    tail = a.shape[lead:]
    n = int(np.prod(tail))
    if tail[-1] < LANE // 2 and n % (8 * LANE) == 0:
        return a.reshape(a.shape[:lead] + (n // (8 * LANE), 8 * LANE))
    return a.reshape(a.shape[:lead] + (-1, tail[-1]))


BIG = [("mix_w_in", 2), ("w_uq", 2), ("w_ukv", 2), ("mix_w_out", 1), ("ssm_w_in", 1), ("w_glu", 2),
       ("ffn_w_up", 2), ("ffn_w_down", 1)]
SMALL = [("sconv_w", 2), ("ssm_norm", 1), ("d_skip", 1), ("ffn_conv_w", 2)]
REPL = ["attn_norm", "cq_norm", "ckv_norm", "q_gain", "k_gain", "lambda_re", "lambda_im", "log_step",
        "b_re", "b_im", "c_re", "c_im", "ffn_norm"]
ORDER = ["attn_norm", "mix_w_in", "cq_norm", "ckv_norm", "w_uq", "w_ukv", "q_gain", "k_gain", "sconv_w", "mix_w_out",
         "ssm_norm", "ssm_w_in", "lambda_re", "lambda_im", "log_step", "b_re", "b_im", "c_re", "c_im", "d_skip",
         "w_glu", "ffn_norm", "ffn_w_up", "ffn_conv_w", "ffn_w_down"]


def _join(g, axis):
    if axis == 0:
        return g.reshape((-1,) + g.shape[2:])
    return jnp.concatenate([g[k] for k in range(NCHIP)], axis=axis)


def _split(full, axis, parts=NCHIP):
    if axis == 0:
        return full.reshape((parts, -1) + full.shape[1:])
    return jnp.stack(jnp.split(full, parts, axis=axis))


def _discretize(lr, li, ls, b_re, b_im):
    dt = jnp.exp(ls)[:, None]
    mag = jnp.exp(lr * dt)
    ar, ai = mag * jnp.cos(li * dt), mag * jnp.sin(li * dt)
    nr, ni = ar - 1.0, ai
    den = lr * lr + li * li
    zr, zi = (nr * lr + ni * li) / den, (ni * lr - nr * li) / den
    bbar_r = zr[..., None] * b_re - zi[..., None] * b_im
    bbar_i = zr[..., None] * b_im + zi[..., None] * b_re
    return ar, ai, bbar_r, bbar_i


def _b_blockdiag(bbar):
    gl = G // NJ
    bb = bbar.reshape(NJ, gl, P, GC).transpose(0, 1, 3, 2)
    return jnp.einsum("jgcp,gh->jgchp", bb, jnp.eye(gl, dtype=bbar.dtype)).reshape(NJ, gl * GC, gl * P)


def _b_blockdiag_t(dbd):
    gl = G // NJ
    d = jnp.einsum("jgchp,gh->jgcp", dbd.reshape(NJ, gl, GC, gl, P), jnp.eye(gl, dtype=dbd.dtype))
    return d.transpose(0, 1, 3, 2).reshape(G, P, GC)


def _c_blockdiag(cmat):
    gl = G // NJ
    cc = cmat.reshape(NJ, gl, GC, P).transpose(0, 1, 3, 2)
    return jnp.einsum("jgpc,gh->jgphc", cc, jnp.eye(gl, dtype=cmat.dtype)).reshape(NJ, gl * P, gl * GC)


def _c_blockdiag_t(dbd):
    gl = G // NJ
    d = jnp.einsum("jgphc,gh->jgpc", dbd.reshape(NJ, gl, P, gl, GC), jnp.eye(gl, dtype=dbd.dtype))
    return d.transpose(0, 1, 3, 2).reshape(G, GC, P)


def _pad_heads_cols(w, width):
    r = w.shape[0]
    return jnp.pad(w.reshape(r, HEADS, width), ((0, 0), (0, 0), (0, HP - width))).reshape(r, HEADS * HP)


def _unpad_heads_cols(w, width):
    r = w.shape[0]
    return w.reshape(r, HEADS, HP)[:, :, :width].reshape(r, HEADS * width)


W_IN_SPLIT = (QR + KVR, QR + KVR + ROPE)


def _w_in_layout(w):
    a, b = W_IN_SPLIT
    kr = jnp.pad(w[:, a:b], ((0, 0), (NOPE, HP - QK)))
    return jnp.concatenate([w[:, :a], w[:, b:], kr], axis=1)


def _ffn_fwd(x, l, wt, name):
    h = rms_fwd(x, wt["ffn_norm"][l][None], name=f"{name}_norm")
    upg, upv, act = ffn_up_act(h, wt["ffn_w_up"][l], wt["ffn_conv_w"][l], name=f"{name}_upact")
    up = (upg, upv)
    out = mm(act, wt["ffn_w_down"][l], add=x, name=f"{name}_down")
    return out, (x, h, up, act)


def _ffn_bwd(dout, douth, saved, l, wt, name):
    x, h, up, act = saved
    g = {}
    dact = mm(douth, wt["ffn_w_down"][l], tb=True, name=f"{name}_ddown")
    g["ffn_w_down"] = mm(act, douth, ta=True, out_dtype=BF16, name=f"{name}_dwdown")
    dup, g["ffn_conv_w"] = ffnact_bwd(up, wt["ffn_conv_w"][l], dact, name=f"{name}_dact")
    g["ffn_w_up"] = tuple(mm(h, d, ta=True, out_dtype=BF16, name=f"{name}_dwup{kk}") for kk, d in enumerate(dup))
    dh = mm_nt_segments(dup, wt["ffn_w_up"][l], name=f"{name}_dup")
    dx, dxh, dg = rms_bwd(x, wt["ffn_norm"][l][None], dh, add=dout, twin=True, name=f"{name}_dnorm")
    g["ffn_norm"] = dg[0]
    return dx, dxh, g


def _even_fwd(x, i, wt, tabs, name):
    h = rms_fwd(x, wt["attn_norm"][i][None], name=f"{name}_norm")
    proj = mm(h, wt["w_in2"][i], name=f"{name}_in")
    cqn = rms_fwd(proj, wt["cq_norm"][i][None], col=0, name=f"{name}_cqnorm")
    ckvn = rms_fwd(proj, wt["ckv_norm"][i][None], col=1, name=f"{name}_ckvnorm")
    qraw = mm(cqn, wt["w_uq_p"][i], name=f"{name}_uq")
    kv = mm(ckvn, wt["w_ukv_p"][i], name=f"{name}_ukv")
    q, k, v = qkprep_fwd(qraw, kv, proj, wt["q_gain_p"][i], wt["k_gain_p"][i], tabs, name=f"{name}_qkprep")
    o, oh, lset = attn_fwd(q, k, v, name=f"{name}_attn")
    conv = sconv_fwd(proj, wt["sconv_w"][i], name=f"{name}_sconv")
    t = mm(oh, wt["w_out_a"][i], add=x, name=f"{name}_outa")
    out = mm(conv, wt["w_out_c"][i], add=t, name=f"{name}_outc")
    return out, (x, h, proj, cqn, ckvn, qraw, kv, q, k, v, o, oh, lset, conv)


def _even_bwd(dout, douth, saved, i, wt, tabs, name):
    x, h, proj, cqn, ckvn, qraw, kv, q, k, v, o, oh, lset, conv = saved
    g = {}
    do = mm(douth, wt["w_out_a"][i], tb=True, name=f"{name}_douta")
    dconv = mm(douth, wt["w_out_c"][i], tb=True, name=f"{name}_doutc")
    g["w_out_a"] = mm(oh, douth, ta=True, out_dtype=BF16, name=f"{name}_dwouta")
    g["w_out_c"] = mm(conv, douth, ta=True, out_dtype=BF16, name=f"{name}_dwoutc")
    dgates, g["sconv_w"] = sconv_bwd(proj, wt["sconv_w"][i], dconv, name=f"{name}_dsconv")
    doh, deltat = attn_delta(o, do, name=f"{name}_dattn_delta")
    dq, dk, dv = attn_bwd(q, k, v, doh, lset, deltat, name=f"{name}_dattn")
    dqraw, dkraw, dkrope, dqg, dkg = qkprep_bwd(qraw, kv, proj, wt["q_gain_p"][i], wt["k_gain_p"][i], tabs, dq, dk,
                                                name=f"{name}_dqkprep")
    g["q_gain"], g["k_gain"] = dqg[0, :QK], dkg[0, :QK]
    dcqn = mm(dqraw, wt["w_uq_p"][i], tb=True, name=f"{name}_duq")
    g["w_uq_p"] = mm(cqn, dqraw, ta=True, out_dtype=BF16, name=f"{name}_dwuq")
    dkv = (dkraw, dv)
    dckvn = mm_nt_segments(dkv, wt["w_ukv_p"][i], name=f"{name}_dukv")
    g["w_ukv_p"] = tuple(mm(ckvn, d, ta=True, out_dtype=BF16, name=f"{name}_dwukv{kk}") for kk, d in enumerate(dkv))
    dcq, dgq = rms_bwd(proj, wt["cq_norm"][i][None], dcqn, col=0, out_dtype=BF16, name=f"{name}_dcqnorm")
    dckv, dgkv = rms_bwd(proj, wt["ckv_norm"][i][None], dckvn, col=1, out_dtype=BF16, name=f"{name}_dckvnorm")
    g["cq_norm"], g["ckv_norm"] = dgq[0], dgkv[0]
    dproj = (dcq, dckv, *dgates, dkrope)
    g["w_in2"] = tuple(mm(h, d, ta=True, out_dtype=BF16, name=f"{name}_dwin{kk}") for kk, d in enumerate(dproj))
    dh = mm_nt_segments(dproj, wt["w_in2"][i], name=f"{name}_din")
    dx, dxh, dg = rms_bwd(x, wt["attn_norm"][i][None], dh, add=dout, twin=True, name=f"{name}_dnorm")
    g["attn_norm"] = dg[0]
    return dx, dxh, g


def _odd_fwd(x, i, wt, name):
    h = rms_fwd(x, wt["ssm_norm"][i][None], name=f"{name}_norm")
    u = mm(h, wt["ssm_w_in"][i], name=f"{name}_in")
    y, st_r, st_i = s5_fwd(u, wt["bbd_r"][i], wt["bbd_i"][i], wt["cbd_r"][i], wt["cbd_i"][i], wt["tab_f"][i],
                           name=f"{name}_scan")
    gl = s5post_fwd(y, u, wt["d_skip"][i][None], name=f"{name}_gelu")
    glu = mm(gl, wt["w_glu"][i], name=f"{name}_glu")
    out = glu_fwd(glu, x, name=f"{name}_gate")
    return out, (x, h, u, y, st_r, st_i, gl, glu)


def _odd_bwd(dout, douth, saved, i, wt, name):
    x, h, u, y, st_r, st_i, gl, glu = saved
    g = {}
    dglu = glu_bwd(glu, dout, name=f"{name}_dgate")
    g["w_glu"] = mm(gl, dglu, ta=True, out_dtype=BF16, name=f"{name}_dwglu")
    dgl = mm(dglu, wt["w_glu"][i], tb=True, name=f"{name}_dglu")
    dz, dd = s5post_bwd(y, u, wt["d_skip"][i][None], dgl, name=f"{name}_dgelu")
    g["d_skip"] = dd[0]
    du, g["bbd_r"], g["bbd_i"], g["cbd_r"], g["cbd_i"], g["a"] = s5_bwd(
        u, dz, wt["d_skip"][i][None], st_r, st_i, wt["bbd_r"][i], wt["bbd_i"][i], wt["cbd_r"][i], wt["cbd_i"][i],
        wt["tab_r"][i], name=f"{name}_dscan")
    g["ssm_w_in"] = mm(h, du, ta=True, out_dtype=BF16, name=f"{name}_dwin")
    dh = mm(du, wt["ssm_w_in"][i], tb=True, name=f"{name}_din")
    dx, dxh, dg = rms_bwd(x, wt["ssm_norm"][i][None], dh, add=dout, twin=True, name=f"{name}_dnorm")
    g["ssm_norm"] = dg[0]
    return dx, dxh, g


MATMUL_WEIGHTS = {"even": ("mix_w_in", "w_uq", "w_ukv", "mix_w_out"), "odd": ("ssm_w_in", "w_glu"),
                  "ffn": ("ffn_w_up", "ffn_w_down")}
ODD_SMALL = ("lambda_re", "lambda_im", "log_step", "b_re", "b_im", "c_re", "c_im", "ssm_norm", "d_skip")


def _even_layouts(fw, wt, i):
    wt["w_in2"][i] = _w_in_layout(fw["mix_w_in"])
    wt["w_uq_p"][i] = _pad_heads_cols(fw["w_uq"], QK)
    ukv = fw["w_ukv"].reshape(KVR, HEADS, NOPE + VD)
    wt["w_ukv_p"][i] = jnp.concatenate(
        [_pad_heads_cols(ukv[:, :, :NOPE].reshape(KVR, HEADS * NOPE), NOPE),
         _pad_heads_cols(ukv[:, :, NOPE:].reshape(KVR, HEADS * VD), VD)], axis=1)
    wt["w_out_a"][i] = _pad_heads_cols(fw["mix_w_out"][:HEADS * VD].T, VD).T
    wt["w_out_c"][i] = fw["mix_w_out"][HEADS * VD:]


def _even_layouts_t(g):
    dk_, dv_ = g["w_ukv_p"]
    dcq, dckv, dgb, dgc, dci, dkr = g["w_in2"]
    return {"mix_w_in": jnp.concatenate([dcq, dckv, dkr[:, NOPE:QK], dgb, dgc, dci], axis=1),
            "w_uq": _unpad_heads_cols(g["w_uq_p"], QK),
            "w_ukv": jnp.concatenate([dk_.reshape(KVR, HEADS, HP)[:, :, :NOPE], dv_.reshape(KVR, HEADS, HP)[:, :, :VD]],
                                     axis=2).reshape(KVR, HEADS * (NOPE + VD)),
            "mix_w_out": jnp.concatenate([_unpad_heads_cols(g["w_out_a"].T, VD).T, g["w_out_c"]], axis=0)}


def _local_step(x, target, full, getw, putg):
    s = x.shape[0]
    n_even = (DEPTH + 1) // 2
    n_odd = DEPTH // 2
    tabs = _rope_tables(s)
    wt = dict(full)
    for key in ("w_in2", "w_uq_p", "w_ukv_p", "w_out_a", "w_out_c") + sum(MATMUL_WEIGHTS.values(), ()):
        wt[key] = {}
    wt["q_gain_p"] = jnp.pad(full["q_gain"], ((0, 0), (0, HP - QK)))[:, None, :]
    wt["k_gain_p"] = jnp.pad(full["k_gain"], ((0, 0), (0, HP - QK)))[:, None, :]

    disc_vjp = []
    for key in ("bbd_r", "bbd_i", "cbd_r", "cbd_i", "tab_f", "tab_r"):
        wt[key] = []
    for i in range(n_odd):
        (ar, ai, bbr, bbi), vjp = jax.vjp(_discretize, full["lambda_re"][i], full["lambda_im"][i], full["log_step"][i],
                                          full["b_re"][i], full["b_im"][i])
        disc_vjp.append(vjp)
        tf, tr = _scan_tables(ar.reshape(-1), ai.reshape(-1))
        wt["tab_f"].append(tf)
        wt["tab_r"].append(tr)
        wt["bbd_r"].append(_b_blockdiag(bbr).astype(BF16))
        wt["bbd_i"].append(_b_blockdiag(bbi).astype(BF16))
        wt["cbd_r"].append(_c_blockdiag(full["c_re"][i]).astype(BF16))
        wt["cbd_i"].append(_c_blockdiag(full["c_im"][i]).astype(BF16))

    saved = []
    for layer in range(DEPTH):
        i = layer // 2
        if layer % 2 == 0:
            fw, tok = getw("even", i, x)
            wt["attn_norm"] = full["attn_norm"] + tok
            _even_layouts(fw, wt, i)
            x, sm = _even_fwd(x, i, wt, tabs, f"l{layer}_mla")
        else:
            fw, tok = getw("odd", i, x)
            wt["ssm_norm"] = full["ssm_norm"] + tok
            for n, a in fw.items():
                wt[n][i] = a
            x, sm = _odd_fwd(x, i, wt, f"l{layer}_s5")
        fw, tok = getw("ffn", layer, x)
        wt["ffn_norm"] = full["ffn_norm"] + tok
        for n, a in fw.items():
            wt[n][layer] = a
        x, sf = _ffn_fwd(x, layer, wt, f"l{layer}_ffn")
        saved.append((sm, sf))
    dx, dxh, lslab = loss_head(x, target, name="loss_head")

    own = [n for n in ORDER if n not in sum(MATMUL_WEIGHTS.values(), ())]
    grads = {n: [None] * (DEPTH if n.startswith("ffn") else n_even) for n in own}
    tok = 0.0
    for layer in reversed(range(DEPTH)):
        i = layer // 2
        sm, sf = saved[layer]
        wt["ffn_conv_w"] = full["ffn_conv_w"] + tok
        dx, dxh, g = _ffn_bwd(dx, dxh, sf, layer, wt, f"l{layer}_ffn")
        tok = putg("ffn", layer, {n: g[n] for n in MATMUL_WEIGHTS["ffn"]})
        for n in ("ffn_norm", "ffn_conv_w"):
            grads[n][layer] = g[n]
        if layer % 2 == 0:
            wt["sconv_w"] = full["sconv_w"] + tok
            dx, dxh, g = _even_bwd(dx, dxh, sm, i, wt, tabs, f"l{layer}_mla")
            tok = putg("even", i, _even_layouts_t(g))
            for n in ("attn_norm", "cq_norm", "ckv_norm", "q_gain", "k_gain", "sconv_w"):
                grads[n][i] = g[n]
        else:
            wt["d_skip"] = full["d_skip"] + tok
            dx, dxh, g = _odd_bwd(dx, dxh, sm, i, wt, f"l{layer}_s5")
            tok = putg("odd", i, {n: g[n] for n in MATMUL_WEIGHTS["odd"]})
            dlr, dli, dls, dbr, dbi = disc_vjp[i]((g["a"][0].reshape(G, P), g["a"][1].reshape(G, P),
                                                    _b_blockdiag_t(g["bbd_r"]), _b_blockdiag_t(g["bbd_i"])))
            grads["lambda_re"][i], grads["lambda_im"][i], grads["log_step"][i] = dlr, dli, dls
            grads["b_re"][i], grads["b_im"][i] = dbr, dbi
            grads["c_re"][i], grads["c_im"][i] = _c_blockdiag_t(g["cbd_r"]), _c_blockdiag_t(g["cbd_i"])
            for n in ("ssm_norm", "d_skip"):
                grads[n][i] = g[n]
            if i == 0:
                tok = tok + putg("odd_small", 0, {n: jnp.stack(grads[n]) for n in ODD_SMALL})
    grads = {n: jnp.stack(v) for n, v in grads.items()}
    return jnp.sum(lslab), dx, grads


def kernel(x, attn_norm, mix_w_in, cq_norm, ckv_norm, w_uq, w_ukv, q_gain, k_gain, sconv_w, mix_w_out, ssm_norm, ssm_w_in, lambda_re, lambda_im, log_step, b_re, b_im, c_re, c_im, d_skip, w_glu, ffn_norm, ffn_w_up, ffn_conv_w, ffn_w_down, loss_target, m_attn_norm, m_mix_w_in, m_cq_norm, m_ckv_norm, m_w_uq, m_w_ukv, m_q_gain, m_k_gain, m_sconv_w, m_mix_w_out, m_ssm_norm, m_ssm_w_in, m_lambda_re, m_lambda_im, m_log_step, m_b_re, m_b_im, m_c_re, m_c_im, m_d_skip, m_w_glu, m_ffn_norm, m_ffn_w_up, m_ffn_conv_w, m_ffn_w_down, v_attn_norm, v_mix_w_in, v_cq_norm, v_ckv_norm, v_w_uq, v_w_ukv, v_q_gain, v_k_gain, v_sconv_w, v_mix_w_out, v_ssm_norm, v_ssm_w_in, v_lambda_re, v_lambda_im, v_log_step, v_b_re, v_b_im, v_c_re, v_c_im, v_d_skip, v_w_glu, v_ffn_norm, v_ffn_w_up, v_ffn_conv_w, v_ffn_w_down):
    args = dict(locals())
    w = {n: args[n] for n in ORDER}
    m = {n: args["m_" + n] for n in ORDER}
    v = {n: args["v_" + n] for n in ORDER}
    me = 2 * lax.axis_index("x") + lax.axis_index("y")

    axis = dict(BIG)

    gs = _chip_exchange([w[n] for n, _ in SMALL], False, "gather_w_f32")
    full = {n: w[n] for n in REPL}
    for (n, ax), g in zip(SMALL, gs):
        full[n] = _join(g, ax)
    parts = [(("even", "odd")[layer % 2], layer // 2) for layer in range(DEPTH)]
    parts = [p for layer, mixer in enumerate(parts) for p in (mixer, ("ffn", layer))]
    gathers = {}

    def start_gather(kind, idx, zero):
        shards = [(w[n][idx] + zero).astype(BF16) for n in MATMUL_WEIGHTS[kind]]
        gathers[kind, idx] = exchange_start(shards, False, f"gather_start_{kind}{idx}")

    start_gather(*parts[0], 0.0 * gs[0][(0,) * gs[0].ndim])

    def getw(kind, idx, after):
        got = exchange_wait(gathers[kind, idx], after, False, f"gather_wait_{kind}{idx}")
        nxt = parts.index((kind, idx)) + 1
        tok = 0.0
        if nxt < len(parts):
            start_gather(*parts[nxt], 0.0 * got[0][(0,) * got[0].ndim].astype(F32))
            tok = gathers[parts[nxt]][3]
        return {n: _join(g, axis[n] - 1) for n, g in zip(MATMUL_WEIGHTS[kind], got)}, tok

    scatters, early = [], []

    def putg(kind, idx, g):
        if kind == "odd_small":
            arrs = [_rows2d(g[n]) for n in ODD_SMALL]
            early.append(exchange_start(arrs, False, "gather_start_g_odd"))
            return early[0][3]
        blocks = [jnp.concatenate([_split(part, axis[n] - 1, NCHIP // len(g[n])) for part in g[n]])
                  if isinstance(g[n], tuple) else _split(g[n], axis[n] - 1) for n in MATMUL_WEIGHTS[kind]]
        tok = 0.0
        if scatters:
            tok = reduce_part(*scatters.pop(), blocks[0])
        scatters.append((kind, idx, exchange_start(blocks, True, f"scatter_start_{kind}{idx}")))
        return tok + scatters[-1][2][3]

    swaps = []

    def reduce_part(kind, idx, started, after):
        got = exchange_wait(started, after, True, f"scatter_wait_{kind}{idx}")
        sums = [sum_slots(_rows2d(sl, 1), name=f"sum_{n}{idx}") for n, sl in zip(MATMUL_WEIGHTS[kind], got)]
        swaps.append((kind, idx, sums, swap_start(sums, f"swap_start_{kind}{idx}")))
        return swaps[-1][3][3]

    sq, dx, grads = _local_step(x[0], loss_target[0], full, getw, putg)
    loss = lax.psum(0.5 * sq / D_MODEL, ("x", "y", "c"))

    late_names = [n for n in REPL + [n for n, _ in SMALL] if n not in ODD_SMALL]
    rep_names = list(ODD_SMALL) + late_names
    names = [n for n, _ in BIG] + rep_names
    late = [_rows2d(grads[n]) for n in late_names]
    late_started = exchange_start(late, False, "gather_start_g_f32")
    reduce_part(*scatters.pop(), dx)
    slots = (exchange_wait(early[0], dx, False, "gather_wait_g_odd")
             + exchange_wait(late_started, swaps[-1][2][0], False, "gather_wait_g_f32"))
    small_mine = [sum_slots(_rows2d(sl, 1), name=f"sum_{n}") for n, sl in zip(rep_names, slots)]
    small_other = list(sibling_swap(small_mine, "swap_g_small"))
    summed, swapped = {}, {}
    for kind, idx, sums, started in swaps:
        for n, p, q in zip(MATMUL_WEIGHTS[kind], sums, swap_wait(started, small_other[0], f"swap_wait_{kind}{idx}")):
            summed[n, idx], swapped[n, idx] = p, q
    mine = [jnp.concatenate([summed[n, idx] for idx in range(w[n].shape[0])], axis=0) for n, _ in BIG] + small_mine
    other = [jnp.concatenate([swapped[n, idx] for idx in range(w[n].shape[0])], axis=0) for n, _ in BIG] + small_other

    def local(n, p):
        ax = dict(SMALL).get(n)
        if ax is None:
            return p
        part = lax.dynamic_index_in_dim(_split(p.reshape(grads[n].shape), ax), me, 0, keepdims=False)
        return _rows2d(part)

    outs = {}
    for n, p, q in zip(names, mine, other):
        res = adamw(local(n, p), local(n, q), _rows2d(w[n]), _rows2d(m[n]), _rows2d(v[n]), name=f"adamw_{n}")
        outs[n] = [r.reshape(w[n].shape) for r in res]
    return (loss, dx[None], *[outs[n][0] for n in ORDER], *[outs[n][1] for n in ORDER],
            *[outs[n][2] for n in ORDER], *[outs[n][3] for n in ORDER])
```

```python
import math

import numpy as np
import jax
import jax.numpy as jnp
from jax import lax
from jax.experimental import pallas as pl
from jax.experimental.pallas import tpu as pltpu

F32, BF16 = jnp.float32, jnp.bfloat16

D_MODEL = 1024
DEPTH = 4
HEADS = 8
NOPE, ROPE, QK, VD = 64, 32, 96, 64
HP = 128
QR, KVR = 256, 256
CONVC = 512
G, P, GC = 64, 64, 16
NST = G * P
SLAB = 8
LANE = 128
EPS = 1e-6
ROPE_THETA = 10000.0
ADAM_LR, ADAM_B1, ADAM_B2, ADAM_EPS, ADAM_WD, ADAM_STEP = 0.001, 0.9, 0.999, 1e-08, 0.01, 10
VMEM_LIMIT = 48 * 1024 * 1024
MESH = pl.DeviceIdType.MESH
ANY = pl.BlockSpec(memory_space=pl.ANY)


def _cparams(*sem):
    return pltpu.CompilerParams(dimension_semantics=sem, vmem_limit_bytes=VMEM_LIMIT)


def _pick(dim, prefs):
    for p in prefs:
        if dim % p == 0:
            return p
    return dim


def _rows(s):
    return _pick(s, (512, 256, 128, 64, 32, 16, 8))


def _long_rows(s):
    return _pick(s, (1024, 512, 256, 128, 64, 32, 16, 8))


MM_VMEM_BUDGET = 36 * 1024 * 1024
MM_MAX_TILE_ELEMS = 640 * 1024
HBM_BYTES_PER_US = 3.0e6
STEP_OVERHEAD_US = 0.35


def _lane_tiles(n):
    c = {t for t in range(LANE, min(n, 1536) + 1, LANE) if n % t == 0}
    if n <= 2304 or not c:
        c.add(n)
    return sorted(c, reverse=True)


def _mm_tiles(m, n, k, sa, sb, so):
    best = None
    for tm in [t for t in (1024, 512, 256) if m % t == 0] or [m]:
        for tn in _lane_tiles(n):
            if tm * tn > MM_MAX_TILE_ELEMS:
                continue
            if 2 * (tm * k * sa + k * tn * sb + tm * tn * so) + 4 * tm * tn > MM_VMEM_BUDGET:
                continue
            steps = (m // tm) * (n // tn)
            for inner_n in (True, False):
                moved = (m * k * sa + (m // tm) * k * n * sb) if inner_n else (k * n * sb + (n // tn) * m * k * sa)
                cost = (moved + m * n * so) / HBM_BYTES_PER_US + steps * STEP_OVERHEAD_US
                if best is None or cost < best[0]:
                    best = (cost, tm, tn, inner_n)
    return best[1:]


def mm(a, b, *, ta=False, tb=False, add=None, out_dtype=F32, name):
    if ta:
        kdim, m = a.shape
    else:
        m, kdim = a.shape
    n = b.shape[0] if tb else b.shape[1]
    so = jnp.dtype(out_dtype).itemsize + (0 if add is None else add.dtype.itemsize)
    tm, tn, inner_n = _mm_tiles(m, n, kdim, a.dtype.itemsize, b.dtype.itemsize, so)
    dn = (((0 if ta else 1,), (1 if tb else 0,)), ((), ()))

    def body(*refs):
        if add is None:
            a_ref, b_ref, o_ref = refs
        else:
            a_ref, b_ref, add_ref, o_ref = refs
        r = lax.dot_general(a_ref[...].astype(BF16), b_ref[...].astype(BF16), dn, preferred_element_type=F32)
        if add is not None:
            r = r + add_ref[...].astype(F32)
        o_ref[...] = r.astype(out_dtype)

    ij = (lambda g0, g1: (g0, g1)) if inner_n else (lambda g0, g1: (g1, g0))
    a_spec = (pl.BlockSpec((kdim, tm), lambda g0, g1: (0, ij(g0, g1)[0])) if ta
              else pl.BlockSpec((tm, kdim), lambda g0, g1: (ij(g0, g1)[0], 0)))
    b_spec = (pl.BlockSpec((tn, kdim), lambda g0, g1: (ij(g0, g1)[1], 0)) if tb
              else pl.BlockSpec((kdim, tn), lambda g0, g1: (0, ij(g0, g1)[1])))
    o_spec = pl.BlockSpec((tm, tn), lambda g0, g1: ij(g0, g1))
    ins, specs = [a, b], [a_spec, b_spec]
    if add is not None:
        ins.append(add)
        specs.append(o_spec)
    grid = (m // tm, n // tn) if inner_n else (n // tn, m // tm)
    return pl.pallas_call(
        body, name=name, grid=grid, in_specs=specs, out_specs=o_spec,
        out_shape=jax.ShapeDtypeStruct((m, n), out_dtype),
        compiler_params=_cparams("parallel", "parallel"))(*ins)


def mm_nt_segments(segs, b, *, name):
    m = segs[0].shape[0]
    n, kdim = b.shape
    widths = [sg.shape[1] for sg in segs]
    offs = [sum(widths[:i]) for i in range(len(segs))]
    assert sum(widths) == kdim and all(o % wd == 0 for o, wd in zip(offs, widths))
    sa = max(sg.dtype.itemsize for sg in segs)
    tm, tn, inner_n = _mm_tiles(m, n, kdim, sa, b.dtype.itemsize, 4)
    ns = len(segs)

    def body(*refs):
        o_ref = refs[-1]
        r = None
        for a_ref, b_ref in zip(refs[:ns], refs[ns:2 * ns]):
            d = lax.dot_general(a_ref[...].astype(BF16), b_ref[...].astype(BF16), NT, preferred_element_type=F32)
            r = d if r is None else r + d
        o_ref[...] = r

    ij = (lambda g0, g1: (g0, g1)) if inner_n else (lambda g0, g1: (g1, g0))
    a_specs = [pl.BlockSpec((tm, wd), lambda g0, g1: (ij(g0, g1)[0], 0)) for wd in widths]
    b_specs = [pl.BlockSpec((tn, wd), lambda g0, g1, blk=o // wd: (ij(g0, g1)[1], blk)) for o, wd in zip(offs, widths)]
    grid = (m // tm, n // tn) if inner_n else (n // tn, m // tm)
    return pl.pallas_call(
        body, name=name, grid=grid, in_specs=a_specs + b_specs,
        out_specs=pl.BlockSpec((tm, tn), lambda g0, g1: ij(g0, g1)),
        out_shape=jax.ShapeDtypeStruct((m, n), F32),
        compiler_params=_cparams("parallel", "parallel"))(*segs, *([b] * ns))


def rms_fwd(x, g, *, col=0, out_dtype=BF16, name):
    s = x.shape[0]
    d = g.shape[1]
    tm = _rows(s)

    def body(x_ref, g_ref, o_ref):
        xv = x_ref[...]
        r = lax.rsqrt(jnp.mean(xv * xv, axis=-1, keepdims=True) + EPS)
        o_ref[...] = (xv * r * g_ref[...]).astype(out_dtype)

    return pl.pallas_call(
        body, name=name, grid=(s // tm,),
        in_specs=[pl.BlockSpec((tm, d), lambda i: (i, col)), pl.BlockSpec((1, d), lambda i: (0, 0))],
        out_specs=pl.BlockSpec((tm, d), lambda i: (i, 0)),
        out_shape=jax.ShapeDtypeStruct((s, d), out_dtype),
        compiler_params=_cparams("parallel"))(x, g)


def rms_bwd(x, g, dy, *, col=0, add=None, out_dtype=F32, twin=False, name):
    s = x.shape[0]
    d = g.shape[1]
    tm = _rows(s)

    def body(*refs):
        refs = list(refs)
        dg_ref = refs.pop()
        dxh_ref = refs.pop() if twin else None
        dx_ref = refs.pop()
        add_ref = refs.pop() if add is not None else None
        x_ref, g_ref, dy_ref = refs

        @pl.when(pl.program_id(0) == 0)
        def _():
            dg_ref[...] = jnp.zeros_like(dg_ref)

        xv = x_ref[...]
        dyv = dy_ref[...].astype(F32)
        r = lax.rsqrt(jnp.mean(xv * xv, axis=-1, keepdims=True) + EPS)
        xh = xv * r
        dg_ref[...] += jnp.sum(dyv * xh, axis=0, keepdims=True)
        dxh = dyv * g_ref[...]
        dx = r * (dxh - xh * jnp.mean(dxh * xh, axis=-1, keepdims=True))
        if add is not None:
            dx = dx + add_ref[...]
        dx_ref[...] = dx.astype(out_dtype)
        if twin:
            dxh_ref[...] = dx.astype(BF16)

    row = pl.BlockSpec((tm, d), lambda i: (i, 0))
    vec = pl.BlockSpec((1, d), lambda i: (0, 0))
    ins = [x, g, dy]
    specs = [pl.BlockSpec((tm, d), lambda i: (i, col)), vec, row]
    if add is not None:
        ins.append(add)
        specs.append(row)
    dxs = [jax.ShapeDtypeStruct((s, d), out_dtype)] + ([jax.ShapeDtypeStruct((s, d), BF16)] if twin else [])
    return pl.pallas_call(
        body, name=name, grid=(s // tm,), in_specs=specs,
        out_specs=[row] * len(dxs) + [vec],
        out_shape=dxs + [jax.ShapeDtypeStruct((1, d), F32)],
        compiler_params=_cparams("arbitrary"))(*ins)


def _rope_tables(s):
    inv = 1.0 / (ROPE_THETA ** (jnp.arange(0, ROPE, 2, dtype=F32) / ROPE))
    ang = jnp.arange(s, dtype=F32)[:, None] * inv[None, :]
    cos, sin = jnp.cos(ang), jnp.sin(ang)
    z = lambda w: jnp.zeros((s, w), F32)
    c = jnp.concatenate([jnp.ones((s, NOPE), F32), cos, cos, z(HP - QK)], axis=1)
    s1 = jnp.concatenate([z(NOPE), -sin, z(HP - NOPE - ROPE // 2)], axis=1)
    s2 = jnp.concatenate([z(NOPE + ROPE // 2), sin, z(HP - QK)], axis=1)
    return c, s1, s2


def qkprep_fwd(qraw, kv, proj, qg, kg, tabs, *, name):
    s = qraw.shape[0]
    tm = _long_rows(s)
    kr_col = (proj.shape[1] - HP) // HP

    def body(q_ref, k_ref, v_ref, kr_ref, qg_ref, kg_ref, c_ref, s1_ref, s2_ref, qo_ref, ko_ref, vo_ref):
        c, s1, s2 = c_ref[...], s1_ref[...], s2_ref[...]

        def f(xv, gain):
            r = lax.rsqrt(jnp.sum(xv * xv, axis=-1, keepdims=True) * (1.0 / QK) + EPS)
            xn = xv * r * gain
            return xn * c + pltpu.roll(xn, HP - ROPE // 2, 1) * s1 + pltpu.roll(xn, ROPE // 2, 1) * s2

        qo_ref[...] = (f(q_ref[...], qg_ref[...]) * Q_SCALE).astype(BF16)
        ko_ref[...] = f(k_ref[...] + kr_ref[...], kg_ref[...]).astype(BF16)
        vv = v_ref[...]
        lane = lax.broadcasted_iota(jnp.int32, vv.shape, 1)
        vo_ref[...] = jnp.where(lane == VD, 1.0, vv).astype(BF16)

    head = pl.BlockSpec((tm, HP), lambda i, h: (i, h))
    tab = pl.BlockSpec((tm, HP), lambda i, h: (i, 0))
    gain = pl.BlockSpec((1, HP), lambda i, h: (0, 0))
    return pl.pallas_call(
        body, name=name, grid=(s // tm, HEADS),
        in_specs=[head, head, pl.BlockSpec((tm, HP), lambda i, h: (i, HEADS + h)),
                  pl.BlockSpec((tm, HP), lambda i, h: (i, kr_col)), gain, gain, tab, tab, tab],
        out_specs=[head, head, head],
        out_shape=[jax.ShapeDtypeStruct((s, HEADS * HP), BF16)] * 3,
        compiler_params=_cparams("parallel", "parallel"))(qraw, kv, kv, proj, qg, kg, *tabs)


def qkprep_bwd(qraw, kv, proj, qg, kg, tabs, dq, dk, *, name):
    s = qraw.shape[0]
    tm = _long_rows(s)
    kr_col = (proj.shape[1] - HP) // HP

    def body(q_ref, k_ref, kr_ref, qg_ref, kg_ref, c_ref, s1_ref, s2_ref, dq_ref, dk_ref,
             dqr_ref, dkr_ref, dkrope_ref, dqg_ref, dkg_ref):
        i, h = pl.program_id(0), pl.program_id(1)
        c, s1, s2 = c_ref[...], s1_ref[...], s2_ref[...]

        @pl.when((i == 0) & (h == 0))
        def _():
            dqg_ref[...] = jnp.zeros_like(dqg_ref)
            dkg_ref[...] = jnp.zeros_like(dkg_ref)

        @pl.when(h == 0)
        def _():
            dkrope_ref[...] = jnp.zeros_like(dkrope_ref)

        def f(xv, gain, dout):
            r = lax.rsqrt(jnp.sum(xv * xv, axis=-1, keepdims=True) * (1.0 / QK) + EPS)
            xh = xv * r
            dxn = dout * c + pltpu.roll(dout * s1, ROPE // 2, 1) + pltpu.roll(dout * s2, HP - ROPE // 2, 1)
            dgain = jnp.sum(dxn * xh, axis=0, keepdims=True)
            dxh = dxn * gain
            dx = r * (dxh - xh * (jnp.sum(dxh * xh, axis=-1, keepdims=True) * (1.0 / QK)))
            return dx, dgain

        dxq, dgq = f(q_ref[...], qg_ref[...], dq_ref[...])
        dxk, dgk = f(k_ref[...] + kr_ref[...], kg_ref[...], dk_ref[...])
        dqr_ref[...] = dxq.astype(BF16)
        dkr_ref[...] = dxk.astype(BF16)
        dqg_ref[...] += dgq
        dkg_ref[...] += dgk
        lane = lax.broadcasted_iota(jnp.int32, dxk.shape, 1)
        dkrope_ref[...] += jnp.where((lane >= NOPE) & (lane < QK), dxk, 0.0)

    head = pl.BlockSpec((tm, HP), lambda i, h: (i, h))
    tab = pl.BlockSpec((tm, HP), lambda i, h: (i, 0))
    gain = pl.BlockSpec((1, HP), lambda i, h: (0, 0))
    return pl.pallas_call(
        body, name=name, grid=(s // tm, HEADS),
        in_specs=[head, head, pl.BlockSpec((tm, HP), lambda i, h: (i, kr_col)), gain, gain, tab, tab, tab, head, head],
        out_specs=[head, head, tab, gain, gain],
        out_shape=[jax.ShapeDtypeStruct((s, HEADS * HP), BF16)] * 2
        + [jax.ShapeDtypeStruct((s, HP), F32), jax.ShapeDtypeStruct((1, HP), F32), jax.ShapeDtypeStruct((1, HP), F32)],
        compiler_params=_cparams("arbitrary", "arbitrary"))(qraw, kv, proj, qg, kg, *tabs, dq, dk)


ATT_SCALE = QK ** -0.5
LOG2E = math.log2(math.e)
Q_SCALE = ATT_SCALE * LOG2E
NEG = -1e30


def _att_tile(s):
    return _pick(s, (512, 256, 128))


def _causal(sv, diag):
    r = lax.broadcasted_iota(jnp.int32, sv.shape, 0)
    c = lax.broadcasted_iota(jnp.int32, sv.shape, 1)
    return jnp.where(diag & (c > r), NEG, sv)


NT = (((1,), (1,)), ((), ()))
TN = (((0,), (0,)), ((), ()))


def _row_of(col):
    return jnp.broadcast_to(col, (col.shape[0], LANE)).T[0:SLAB, :]


def _att_specs(s):
    t = _att_tile(s)
    nb = s // t
    tile = pl.BlockSpec((t, HP), lambda h, i: (i, h))
    whole = pl.BlockSpec((s, HP), lambda h, i: (0, h))
    row = pl.BlockSpec((1, 1, SLAB, t), lambda h, i: (h, i, 0, 0))
    rows = pl.BlockSpec((1, nb, SLAB, t), lambda h, i: (h, 0, 0, 0))
    return t, nb, tile, whole, row, rows


def attn_fwd(q, k, v, *, name):
    s = q.shape[0]
    t, nb, tile, whole, row, _ = _att_specs(s)

    def body(q_ref, k_ref, v_ref, o_ref, oh_ref, lset_ref, s_scr, mb_scr, acc):
        qb = pl.program_id(1)
        qv = q_ref[...]

        def scores(j):
            r0 = pl.multiple_of(j * t, t)
            return lax.dot_general(qv, k_ref[pl.ds(r0, t), :], NT, preferred_element_type=F32)

        def fold(sv):
            m = sv[:, 0:LANE]
            for kk in range(1, t // LANE):
                m = jnp.maximum(m, sv[:, kk * LANE:(kk + 1) * LANE])
            return m

        def first(j, m):
            sv = scores(j)
            s_scr[j] = sv
            return jnp.maximum(m, fold(sv))

        m = lax.fori_loop(0, qb, first, jnp.full((t, LANE), NEG, F32))
        sd = _causal(scores(qb), True)
        s_scr[qb] = sd
        mcol = jnp.max(jnp.maximum(m, fold(sd)), axis=-1, keepdims=True)
        mb_scr[...] = jnp.broadcast_to(mcol, (t, t))
        acc[...] = jnp.zeros_like(acc)

        def second(j, carry):
            r0 = pl.multiple_of(j * t, t)
            p = jnp.exp2(s_scr[j] - mb_scr[...]).astype(BF16)
            acc[...] += jnp.dot(p, v_ref[pl.ds(r0, t), :], preferred_element_type=F32)
            return carry

        lax.fori_loop(0, qb + 1, second, 0)
        av = acc[...]
        lsum = av[:, VD:VD + 1]
        lane = lax.broadcasted_iota(jnp.int32, av.shape, 1)
        ov = jnp.where(lane == VD, 0.0, av / lsum)
        o_ref[...] = ov
        oh_ref[...] = ov.astype(BF16)
        lset_ref[0, 0] = _row_of(mcol + jnp.log2(lsum))

    return pl.pallas_call(
        body, name=name, grid=(HEADS, nb), in_specs=[tile, whole, whole], out_specs=[tile, tile, row],
        out_shape=[jax.ShapeDtypeStruct((s, HEADS * HP), F32), jax.ShapeDtypeStruct((s, HEADS * HP), BF16),
                   jax.ShapeDtypeStruct((HEADS, nb, SLAB, t), F32)],
        scratch_shapes=[pltpu.VMEM((nb, t, t), F32), pltpu.VMEM((t, t), F32), pltpu.VMEM((t, HP), F32)],
        compiler_params=_cparams("parallel", "parallel"))(q, k, v)


def attn_delta(o, do, *, name):
    s = o.shape[0]
    t, nb, tile, _, row, _ = _att_specs(s)

    def body(o_ref, do_ref, doh_ref, dt_ref):
        dov = do_ref[...]
        doh_ref[...] = dov.astype(BF16)
        dt_ref[0, 0] = _row_of(jnp.sum(dov * o_ref[...], axis=-1, keepdims=True))

    return pl.pallas_call(
        body, name=name, grid=(HEADS, nb), in_specs=[tile, tile], out_specs=[tile, row],
        out_shape=[jax.ShapeDtypeStruct((s, HEADS * HP), BF16), jax.ShapeDtypeStruct((HEADS, nb, SLAB, t), F32)],
        compiler_params=_cparams("parallel", "parallel"))(o, do)


def attn_bwd(q, k, v, doh, lset, deltat, *, name):
    s = q.shape[0]
    t, nb, tile, whole, _, rows = _att_specs(s)

    def body(q_ref, k_ref, v_ref, do_ref, lt_ref, dt_ref, dq_ref, dk_ref, dv_ref, dk_acc, dv_acc):
        kb = pl.program_id(1)
        kt, vt = k_ref[...], v_ref[...]
        dk_acc[...] = jnp.zeros_like(dk_acc)
        dv_acc[...] = jnp.zeros_like(dv_acc)

        @pl.when(kb == 0)
        def _():
            dq_ref[...] = jnp.zeros_like(dq_ref)

        def step(i, diag):
            r0 = pl.multiple_of(i * t, t)
            qi, doi = q_ref[pl.ds(r0, t), :], do_ref[pl.ds(r0, t), :]
            st = lax.dot_general(kt, qi, NT, preferred_element_type=F32)
            if diag:
                kr = lax.broadcasted_iota(jnp.int32, st.shape, 0)
                qc = lax.broadcasted_iota(jnp.int32, st.shape, 1)
                st = jnp.where(kr > qc, NEG, st)
            pt = jnp.exp2(st - lt_ref[0, i][0:1, :])
            dpt = lax.dot_general(vt, doi, NT, preferred_element_type=F32)
            dst = (pt * (dpt - dt_ref[0, i][0:1, :])).astype(BF16)
            dv_acc[...] += jnp.dot(pt.astype(BF16), doi, preferred_element_type=F32)
            dk_acc[...] += jnp.dot(dst, qi, preferred_element_type=F32)
            dq_ref[pl.ds(r0, t), :] += lax.dot_general(dst, kt, TN, preferred_element_type=F32) * ATT_SCALE

        def off_diag(i, carry):
            step(i, False)
            return carry

        step(kb, True)
        lax.fori_loop(kb + 1, nb, off_diag, 0)
        dk_ref[...] = dk_acc[...] * (1.0 / LOG2E)
        dvv = dv_acc[...]
        lane = lax.broadcasted_iota(jnp.int32, dvv.shape, 1)
        dv_ref[...] = jnp.where(lane == VD, 0.0, dvv).astype(BF16)

    return pl.pallas_call(
        body, name=name, grid=(HEADS, nb), in_specs=[whole, tile, tile, whole, rows, rows],
        out_specs=[whole, tile, tile],
        out_shape=[jax.ShapeDtypeStruct((s, HEADS * HP), F32), jax.ShapeDtypeStruct((s, HEADS * HP), F32),
                   jax.ShapeDtypeStruct((s, HEADS * HP), BF16)],
        scratch_shapes=[pltpu.VMEM((t, HP), F32), pltpu.VMEM((t, HP), F32)],
        compiler_params=_cparams("parallel", "arbitrary"))(q, k, v, doh, lset, deltat)


HALO = 8
CW = 256


def _shifts(zw):
    return zw, pltpu.roll(zw, 1, 0), pltpu.roll(zw, 2, 0)


def _conv3(sh, w):
    return w[2:3] * sh[0] + w[1:2] * sh[1] + w[0:1] * sh[2]


def _conv3_t(dc, w):
    n = dc.shape[0]
    return w[2:3] * dc + w[1:2] * pltpu.roll(dc, n - 1, 0) + w[0:1] * pltpu.roll(dc, n - 2, 0)


def _conv3_dw(dc, sh, r):
    return [jnp.sum(dc * z[HALO:HALO + r], axis=0, keepdims=True) for z in (sh[2], sh[1], sh[0])]


def _halo_specs(r, colfn):
    rb = r // HALO
    cur = pl.BlockSpec((r, CW), lambda j, i: (i, colfn(j)))
    prev = pl.BlockSpec((HALO, CW), lambda j, i: (jnp.maximum(i * rb - 1, 0), colfn(j)))

    def nxt(nrow_blocks):
        return pl.BlockSpec((HALO, CW), lambda j, i: (jnp.minimum((i + 1) * rb, nrow_blocks * rb - 1), colfn(j)))

    return cur, prev, nxt


H_HALO = 16


def ffn_up_act(h, w_up, w, *, name):
    s, kdim = h.shape
    hh = w_up.shape[1] // 2
    nj = hh // CW
    r = _long_rows(s)
    nt = s // r
    rb = r // H_HALO

    def body(h_ref, hp_ref, bg_ref, bv_ref, wg_ref, wv_ref, ug_ref, uv_ref, o_ref):
        pm = (pl.program_id(1) > 0).astype(F32)
        hv, hp = h_ref[...], hp_ref[...]

        def half(b_ref, w_ref, u_ref):
            up = jnp.dot(hv, b_ref[...], preferred_element_type=F32)
            u_ref[...] = up
            prev = jnp.dot(hp, b_ref[...], preferred_element_type=F32) * pm
            return _conv3(_shifts(jnp.concatenate([prev, up], axis=0)), w_ref[...])[H_HALO:]

        cg = half(bg_ref, wg_ref, ug_ref)
        cv = half(bv_ref, wv_ref, uv_ref)
        o_ref[...] = (cg * jax.nn.sigmoid(cg) * cv).astype(BF16)

    tile = pl.BlockSpec((r, CW), lambda j, i: (i, j))
    return pl.pallas_call(
        body, name=name, grid=(nj, nt),
        in_specs=[pl.BlockSpec((r, kdim), lambda j, i: (i, 0)),
                  pl.BlockSpec((H_HALO, kdim), lambda j, i: (jnp.maximum(i * rb - 1, 0), 0)),
                  pl.BlockSpec((kdim, CW), lambda j, i: (0, j)), pl.BlockSpec((kdim, CW), lambda j, i: (0, nj + j)),
                  pl.BlockSpec((3, CW), lambda j, i: (0, j)), pl.BlockSpec((3, CW), lambda j, i: (0, nj + j))],
        out_specs=[tile, tile, tile],
        out_shape=[jax.ShapeDtypeStruct((s, hh), F32), jax.ShapeDtypeStruct((s, hh), F32),
                   jax.ShapeDtypeStruct((s, hh), BF16)],
        compiler_params=_cparams("parallel", "parallel"))(h, h, w_up, w_up, w, w)


def ffnact_bwd(up, w, douth, w_down, *, name):
    upg, upv = up
    s, hh = upg.shape
    nj = hh // CW
    r = _rows(s)
    nt = s // r

    def body(g_ref, gp_ref, gn_ref, v_ref, vp_ref, vn_ref, wg_ref, wv_ref, do_ref, don_ref, wd_ref,
             dg_ref, dv_ref, dwg_ref, dwv_ref):
        i = pl.program_id(1)
        pm = (i > 0).astype(F32)
        nm = (i < nt - 1).astype(F32)

        @pl.when(i == 0)
        def _():
            dwg_ref[...] = jnp.zeros_like(dwg_ref)
            dwv_ref[...] = jnp.zeros_like(dwv_ref)

        wg, wv = wg_ref[...], wv_ref[...]
        zg = jnp.concatenate([gp_ref[...] * pm, g_ref[...], gn_ref[...]], axis=0)
        zv = jnp.concatenate([vp_ref[...] * pm, v_ref[...], vn_ref[...]], axis=0)
        zg, zv = _shifts(zg), _shifts(zv)
        cg = _conv3(zg, wg)[HALO:]
        cv = _conv3(zv, wv)[HALO:]
        wd = wd_ref[...]
        da = jnp.concatenate(
            [lax.dot_general(do_ref[...], wd, NT, preferred_element_type=F32),
             lax.dot_general(don_ref[...], wd, NT, preferred_element_type=F32)[:HALO] * nm], axis=0)
        sg = jax.nn.sigmoid(cg)
        dcg = da * cv * (sg * (1.0 + cg * (1.0 - sg)))
        dcv = da * (cg * sg)
        dg_ref[...] = _conv3_t(dcg, wg)[:r].astype(BF16)
        dv_ref[...] = _conv3_t(dcv, wv)[:r].astype(BF16)
        for kk, (a, b) in enumerate(zip(_conv3_dw(dcg[:r], zg, r), _conv3_dw(dcv[:r], zv, r))):
            dwg_ref[kk:kk + 1, :] += a
            dwv_ref[kk:kk + 1, :] += b

    gcur, gprev, gnext = _halo_specs(r, lambda j: j)
    vcur, vprev, vnext = gcur, gprev, gnext
    acur, _, _ = _halo_specs(r, lambda j: j)
    wg = pl.BlockSpec((3, CW), lambda j, i: (0, j))
    wv = pl.BlockSpec((3, CW), lambda j, i: (0, nj + j))
    dm = douth.shape[1]
    nb16 = r // H_HALO
    docur = pl.BlockSpec((r, dm), lambda j, i: (i, 0))
    donext = pl.BlockSpec((H_HALO, dm), lambda j, i: (jnp.minimum((i + 1) * nb16, nt * nb16 - 1), 0))
    wds = pl.BlockSpec((CW, dm), lambda j, i: (j, 0))
    dupg, dupv, dwg, dwv = pl.pallas_call(
        body, name=name, grid=(nj, nt),
        in_specs=[gcur, gprev, gnext(nt), vcur, vprev, vnext(nt), wg, wv, docur, donext, wds],
        out_specs=[acur, acur, wg, wg],
        out_shape=[jax.ShapeDtypeStruct((s, hh), BF16), jax.ShapeDtypeStruct((s, hh), BF16),
                   jax.ShapeDtypeStruct((3, hh), F32), jax.ShapeDtypeStruct((3, hh), F32)],
        compiler_params=_cparams("parallel", "arbitrary"))(upg, upg, upg, upv, upv, upv, w, w, douth, douth, w_down)
    return (dupg, dupv), jnp.concatenate([dwg, dwv], axis=1)


def sconv_fwd(proj, w, *, name):
    s = proj.shape[0]
    nj = CONVC // CW
    r = _long_rows(s)
    nt = s // r

    def body(b_ref, c_ref, cp_ref, x_ref, xp_ref, w_ref, o_ref):
        pm = (pl.program_id(1) > 0).astype(F32)
        zw = jnp.concatenate([cp_ref[...] * xp_ref[...] * pm, c_ref[...] * x_ref[...]], axis=0)
        o_ref[...] = (b_ref[...] * _conv3(_shifts(zw), w_ref[...])[HALO:]).astype(BF16)

    bcur, _, _ = _halo_specs(r, lambda j: (QR + KVR) // CW + j)
    ccur, cprev, _ = _halo_specs(r, lambda j: (QR + KVR + CONVC) // CW + j)
    xcur, xprev, _ = _halo_specs(r, lambda j: (QR + KVR + 2 * CONVC) // CW + j)
    ws = pl.BlockSpec((3, CW), lambda j, i: (0, j))
    return pl.pallas_call(
        body, name=name, grid=(nj, nt), in_specs=[bcur, ccur, cprev, xcur, xprev, ws],
        out_specs=pl.BlockSpec((r, CW), lambda j, i: (i, j)),
        out_shape=jax.ShapeDtypeStruct((s, CONVC), BF16),
        compiler_params=_cparams("parallel", "parallel"))(proj, proj, proj, proj, proj, w)


def sconv_bwd(proj, w, dy, *, name):
    s = proj.shape[0]
    nj = CONVC // CW
    r = _rows(s)
    nt = s // r

    def body(b_ref, bn_ref, c_ref, cp_ref, x_ref, xp_ref, w_ref, dy_ref, dyn_ref, db_ref, dc_ref, dx_ref, dw_ref):
        i = pl.program_id(1)
        pm = (i > 0).astype(F32)
        nm = (i < nt - 1).astype(F32)

        @pl.when(i == 0)
        def _():
            dw_ref[...] = jnp.zeros_like(dw_ref)

        wv = w_ref[...]
        zw = jnp.concatenate([cp_ref[...] * xp_ref[...] * pm, c_ref[...] * x_ref[...]], axis=0)
        zw = _shifts(zw)
        conv = _conv3(zw, wv)[HALO:]
        dyv = dy_ref[...]
        db_ref[...] = (dyv * conv).astype(BF16)
        dconv = jnp.concatenate([dyv * b_ref[...], dyn_ref[...] * bn_ref[...] * nm], axis=0)
        dz = _conv3_t(dconv, wv)[:r]
        dc_ref[...] = (dz * x_ref[...]).astype(BF16)
        dx_ref[...] = (dz * c_ref[...]).astype(BF16)
        for kk, a in enumerate(_conv3_dw(dconv[:r], zw, r)):
            dw_ref[kk:kk + 1, :] += a

    bcur, _, bnext = _halo_specs(r, lambda j: (QR + KVR) // CW + j)
    ccur, cprev, _ = _halo_specs(r, lambda j: (QR + KVR + CONVC) // CW + j)
    xcur, xprev, _ = _halo_specs(r, lambda j: (QR + KVR + 2 * CONVC) // CW + j)
    ycur, _, ynext = _halo_specs(r, lambda j: j)
    ws = pl.BlockSpec((3, CW), lambda j, i: (0, j))
    out = pl.BlockSpec((r, CW), lambda j, i: (i, j))
    db, dc, dx, dw = pl.pallas_call(
        body, name=name, grid=(nj, nt),
        in_specs=[bcur, bnext(nt), ccur, cprev, xcur, xprev, ws, ycur, ynext(nt)],
        out_specs=[out, out, out, ws],
        out_shape=[jax.ShapeDtypeStruct((s, CONVC), BF16)] * 3 + [jax.ShapeDtypeStruct((3, CONVC), F32)],
        compiler_params=_cparams("parallel", "arbitrary"))(proj, proj, proj, proj, proj, proj, w, dy, dy)
    return (db, dc, dx), dw


SW = 512
NJ = NST // SW


def _scan_tables(ar, ai):
    def cmul(x, y):
        return x[0] * y[0] - x[1] * y[1], x[0] * y[1] + x[1] * y[0]

    def build(a, reverse):
        pw = [a]
        for _ in range(SLAB - 1):
            pw.append(cmul(pw[-1], a))
        row = jnp.arange(SLAB)[:, None]
        tabs = []
        for kk in (1, 2, 4):
            mask = ((row < SLAB - kk) if reverse else (row >= kk)).astype(F32)
            tabs += [mask * pw[kk - 1][0][None, :], mask * pw[kk - 1][1][None, :]]
        order = list(range(SLAB - 1, -1, -1)) if reverse else list(range(SLAB))
        tabs += [jnp.stack([pw[o][0] for o in order]), jnp.stack([pw[o][1] for o in order])]
        return jnp.stack(tabs)

    return build((ar, ai), False), build((ar, -ai), True)


def _slab_scan(xr, xi, tabs, cr, ci, reverse):
    for n, kk in enumerate((1, 2, 4)):
        sh = SLAB - kk if reverse else kk
        tr, ti = tabs[2 * n], tabs[2 * n + 1]
        sr, si = pltpu.roll(xr, sh, 0), pltpu.roll(xi, sh, 0)
        xr, xi = xr + tr * sr - ti * si, xi + tr * si + ti * sr
    tr, ti = tabs[6], tabs[7]
    return xr + tr * cr - ti * ci, xi + tr * ci + ti * cr


def s5_fwd(u, bbd_r, bbd_i, cbd_r, cbd_i, tab, *, name):
    s = u.shape[0]
    tbk = _long_rows(s)
    nt = s // tbk
    nsl = tbk // SLAB

    def body(u_ref, br_ref, bi_ref, cr_ref, ci_ref, tab_ref, y_ref, sr_ref, si_ref, bur, bui, carry):
        @pl.when(pl.program_id(1) == 0)
        def _():
            carry[...] = jnp.zeros_like(carry)

        ub = u_ref[...].astype(BF16)
        bur[...] = jnp.dot(ub, br_ref[0], preferred_element_type=F32)
        bui[...] = jnp.dot(ub, bi_ref[0], preferred_element_type=F32)
        tabs = [tab_ref[n] for n in range(8)]

        def slab(n, c):
            r0 = pl.multiple_of(n * SLAB, SLAB)
            sr, si = _slab_scan(bur[pl.ds(r0, SLAB), :], bui[pl.ds(r0, SLAB), :], tabs, c[0], c[1], False)
            sr_ref[pl.ds(r0, SLAB), :] = sr
            si_ref[pl.ds(r0, SLAB), :] = si
            return (jnp.broadcast_to(sr[SLAB - 1:SLAB], sr.shape), jnp.broadcast_to(si[SLAB - 1:SLAB], si.shape))

        cr, ci = lax.fori_loop(0, nsl, slab, (carry[0], carry[1]))
        carry[0] = cr
        carry[1] = ci
        y_ref[...] = (jnp.dot(sr_ref[...].astype(BF16), cr_ref[0], preferred_element_type=F32)
                      - jnp.dot(si_ref[...].astype(BF16), ci_ref[0], preferred_element_type=F32))

    us = pl.BlockSpec((tbk, LANE), lambda j, t: (t, j))
    bs = pl.BlockSpec((1, LANE, SW), lambda j, t: (j, 0, 0))
    cs = pl.BlockSpec((1, SW, LANE), lambda j, t: (j, 0, 0))
    ts = pl.BlockSpec((8, SLAB, SW), lambda j, t: (0, 0, j))
    ss = pl.BlockSpec((tbk, SW), lambda j, t: (t, j))
    return pl.pallas_call(
        body, name=name, grid=(NJ, nt), in_specs=[us, bs, bs, cs, cs, ts], out_specs=[us, ss, ss],
        out_shape=[jax.ShapeDtypeStruct((s, D_MODEL), F32), jax.ShapeDtypeStruct((s, NST), F32),
                   jax.ShapeDtypeStruct((s, NST), F32)],
        scratch_shapes=[pltpu.VMEM((tbk, SW), F32), pltpu.VMEM((tbk, SW), F32), pltpu.VMEM((2, SLAB, SW), F32)],
        compiler_params=_cparams("parallel", "arbitrary"))(u, bbd_r, bbd_i, cbd_r, cbd_i, tab)


def s5_bwd(u, dy, dskip, st_r, st_i, bbd_r, bbd_i, cbd_r, cbd_i, tabrev, *, name):
    s = u.shape[0]
    tbk = _long_rows(s)
    nt = s // tbk
    nsl = tbk // SLAB
    rbk = tbk // SLAB

    def body(u_ref, dy_ref, d_ref, sr_ref, si_ref, pr_ref, pi_ref, br_ref, bi_ref, cr_ref, ci_ref, tab_ref,
             du_ref, dbr_ref, dbi_ref, dcr_ref, dci_ref, da_ref, lam_r, lam_i, carry):
        t = pl.program_id(1)

        @pl.when(t == 0)
        def _():
            carry[...] = jnp.zeros_like(carry)
            dbr_ref[...] = jnp.zeros_like(dbr_ref)
            dbi_ref[...] = jnp.zeros_like(dbi_ref)
            dcr_ref[...] = jnp.zeros_like(dcr_ref)
            dci_ref[...] = jnp.zeros_like(dci_ref)
            da_ref[...] = jnp.zeros_like(da_ref)

        dyv = dy_ref[...]
        dyh = dyv.astype(BF16)
        lam_r[...] = lax.dot_general(dyh, cr_ref[0], NT, preferred_element_type=F32)
        lam_i[...] = -lax.dot_general(dyh, ci_ref[0], NT, preferred_element_type=F32)
        tabs = [tab_ref[n] for n in range(8)]

        def slab(n, c):
            r0 = pl.multiple_of((nsl - 1 - n) * SLAB, SLAB)
            lr, li = _slab_scan(lam_r[pl.ds(r0, SLAB), :], lam_i[pl.ds(r0, SLAB), :], tabs, c[0], c[1], True)
            lam_r[pl.ds(r0, SLAB), :] = lr
            lam_i[pl.ds(r0, SLAB), :] = li
            return (jnp.broadcast_to(lr[0:1], lr.shape), jnp.broadcast_to(li[0:1], li.shape))

        cr, ci = lax.fori_loop(0, nsl, slab, (carry[0], carry[1]))
        carry[0] = cr
        carry[1] = ci
        lr, li = lam_r[...], lam_i[...]
        lrh, lih = lr.astype(BF16), li.astype(BF16)
        du = (dyv * d_ref[...] + lax.dot_general(lrh, br_ref[0], NT, preferred_element_type=F32)
              + lax.dot_general(lih, bi_ref[0], NT, preferred_element_type=F32))
        du_ref[...] = du.astype(BF16)
        ub = u_ref[...].astype(BF16)
        dbr_ref[0] += lax.dot_general(ub, lrh, TN, preferred_element_type=F32)
        dbi_ref[0] += lax.dot_general(ub, lih, TN, preferred_element_type=F32)
        srv, siv = sr_ref[...], si_ref[...]
        dcr_ref[0] += lax.dot_general(srv.astype(BF16), dyh, TN, preferred_element_type=F32)
        dci_ref[0] -= lax.dot_general(siv.astype(BF16), dyh, TN, preferred_element_type=F32)
        first = lax.broadcasted_iota(jnp.int32, srv.shape, 0) == 0
        pm = (t < nt - 1).astype(F32)
        spr = jnp.where(first, pr_ref[SLAB - 1:SLAB, :] * pm, pltpu.roll(srv, 1, 0))
        spi = jnp.where(first, pi_ref[SLAB - 1:SLAB, :] * pm, pltpu.roll(siv, 1, 0))
        da_ref[0:1, :] += jnp.sum(lr * spr + li * spi, axis=0, keepdims=True)
        da_ref[1:2, :] += jnp.sum(li * spr - lr * spi, axis=0, keepdims=True)

    rv = lambda t: nt - 1 - t
    us = pl.BlockSpec((tbk, LANE), lambda j, t: (rv(t), j))
    ds = pl.BlockSpec((1, LANE), lambda j, t: (0, j))
    ss = pl.BlockSpec((tbk, SW), lambda j, t: (rv(t), j))
    ps = pl.BlockSpec((SLAB, SW), lambda j, t: (jnp.maximum(rv(t) * rbk - 1, 0), j))
    bs = pl.BlockSpec((1, LANE, SW), lambda j, t: (j, 0, 0))
    cs = pl.BlockSpec((1, SW, LANE), lambda j, t: (j, 0, 0))
    ts = pl.BlockSpec((8, SLAB, SW), lambda j, t: (0, 0, j))
    das = pl.BlockSpec((2, SW), lambda j, t: (0, j))
    return pl.pallas_call(
        body, name=name, grid=(NJ, nt),
        in_specs=[us, us, ds, ss, ss, ps, ps, bs, bs, cs, cs, ts],
        out_specs=[us, bs, bs, cs, cs, das],
        out_shape=[jax.ShapeDtypeStruct((s, D_MODEL), BF16),
                   jax.ShapeDtypeStruct((NJ, LANE, SW), F32), jax.ShapeDtypeStruct((NJ, LANE, SW), F32),
                   jax.ShapeDtypeStruct((NJ, SW, LANE), F32), jax.ShapeDtypeStruct((NJ, SW, LANE), F32),
                   jax.ShapeDtypeStruct((2, NST), F32)],
        scratch_shapes=[pltpu.VMEM((tbk, SW), F32), pltpu.VMEM((tbk, SW), F32), pltpu.VMEM((2, SLAB, SW), F32)],
        compiler_params=_cparams("parallel", "arbitrary"))(
            u, dy, dskip, st_r, st_i, st_r, st_i, bbd_r, bbd_i, cbd_r, cbd_i, tabrev)


GELU_C = math.sqrt(2.0 / math.pi)
GELU_A = 0.044715


def s5post_fwd(y, u, dskip, *, name):
    s = y.shape[0]
    tm = _rows(s)

    def body(y_ref, u_ref, d_ref, o_ref):
        z = y_ref[...] + d_ref[...] * u_ref[...]
        o_ref[...] = (0.5 * z * (1.0 + jnp.tanh(GELU_C * (z + GELU_A * z * z * z)))).astype(BF16)

    row = pl.BlockSpec((tm, D_MODEL), lambda i: (i, 0))
    vec = pl.BlockSpec((1, D_MODEL), lambda i: (0, 0))
    return pl.pallas_call(body, name=name, grid=(s // tm,), in_specs=[row, row, vec], out_specs=row,
                          out_shape=jax.ShapeDtypeStruct((s, D_MODEL), BF16),
                          compiler_params=_cparams("parallel"))(y, u, dskip)


def s5post_bwd(y, u, dskip, dg, *, name):
    s = y.shape[0]
    tm = _rows(s)

    def body(y_ref, u_ref, d_ref, dg_ref, dz_ref, dd_ref):
        @pl.when(pl.program_id(0) == 0)
        def _():
            dd_ref[...] = jnp.zeros_like(dd_ref)

        uv = u_ref[...]
        z = y_ref[...] + d_ref[...] * uv
        th = jnp.tanh(GELU_C * (z + GELU_A * z * z * z))
        dgelu = 0.5 * (1.0 + th) + 0.5 * z * (1.0 - th * th) * (GELU_C * (1.0 + 3.0 * GELU_A * z * z))
        dz = dg_ref[...] * dgelu
        dz_ref[...] = dz
        dd_ref[...] += jnp.sum(dz * uv, axis=0, keepdims=True)

    row = pl.BlockSpec((tm, D_MODEL), lambda i: (i, 0))
    vec = pl.BlockSpec((1, D_MODEL), lambda i: (0, 0))
    return pl.pallas_call(body, name=name, grid=(s // tm,), in_specs=[row, row, vec, row], out_specs=[row, vec],
                          out_shape=[jax.ShapeDtypeStruct((s, D_MODEL), F32), jax.ShapeDtypeStruct((1, D_MODEL), F32)],
                          compiler_params=_cparams("arbitrary"))(y, u, dskip, dg)


def glu_fwd(glu, x, *, name):
    s = x.shape[0]
    tm = _rows(s)

    def body(a_ref, b_ref, x_ref, o_ref):
        o_ref[...] = x_ref[...] + a_ref[...] * jax.nn.sigmoid(b_ref[...])

    row = pl.BlockSpec((tm, D_MODEL), lambda i: (i, 0))
    return pl.pallas_call(body, name=name, grid=(s // tm,),
                          in_specs=[row, pl.BlockSpec((tm, D_MODEL), lambda i: (i, 1)), row], out_specs=row,
                          out_shape=jax.ShapeDtypeStruct((s, D_MODEL), F32),
                          compiler_params=_cparams("parallel"))(glu, glu, x)


def glu_bwd(glu, dx, *, name):
    s = dx.shape[0]
    tm = _rows(s)

    def body(a_ref, b_ref, dx_ref, o_ref):
        sg = jax.nn.sigmoid(b_ref[...])
        dxv = dx_ref[...]
        o_ref[:, :D_MODEL] = (dxv * sg).astype(BF16)
        o_ref[:, D_MODEL:] = (dxv * a_ref[...] * sg * (1.0 - sg)).astype(BF16)

    row = pl.BlockSpec((tm, D_MODEL), lambda i: (i, 0))
    return pl.pallas_call(body, name=name, grid=(s // tm,),
                          in_specs=[row, pl.BlockSpec((tm, D_MODEL), lambda i: (i, 1)), row],
                          out_specs=pl.BlockSpec((tm, 2 * D_MODEL), lambda i: (i, 0)),
                          out_shape=jax.ShapeDtypeStruct((s, 2 * D_MODEL), BF16),
                          compiler_params=_cparams("parallel"))(glu, glu, dx)


def loss_head(y, target, *, name):
    s = y.shape[0]
    tm = _rows(s)

    def body(y_ref, t_ref, dy_ref, dyh_ref, l_ref):
        @pl.when(pl.program_id(0) == 0)
        def _():
            l_ref[...] = jnp.zeros_like(l_ref)

        e = y_ref[...] - t_ref[...]
        dy_ref[...] = e * (1.0 / D_MODEL)
        dyh_ref[...] = (e * (1.0 / D_MODEL)).astype(BF16)
        e2 = jnp.sum((e * e).reshape(tm // 8, 8, D_MODEL), axis=0)
        acc = e2[:, 0:LANE]
        for kk in range(1, D_MODEL // LANE):
            acc = acc + e2[:, kk * LANE:(kk + 1) * LANE]
        l_ref[...] += acc

    row = pl.BlockSpec((tm, D_MODEL), lambda i: (i, 0))
    return pl.pallas_call(body, name=name, grid=(s // tm,), in_specs=[row, row],
                          out_specs=[row, row, pl.BlockSpec((8, LANE), lambda i: (0, 0))],
                          out_shape=[jax.ShapeDtypeStruct((s, D_MODEL), F32), jax.ShapeDtypeStruct((s, D_MODEL), BF16),
                                     jax.ShapeDtypeStruct((8, LANE), F32)],
                          compiler_params=_cparams("arbitrary"))(y, target)


NCHIP = 4


def _mesh_pos():
    return lax.axis_index("x"), lax.axis_index("y"), lax.axis_index("c")


def _chip_exchange(bufs, scatter, name):
    n = len(bufs)
    shapes = [b.shape[1:] if scatter else b.shape for b in bufs]

    def body(*refs):
        ins, outs = refs[:n], refs[n:2 * n]
        send_sems, recv_sems, local_sems = refs[2 * n:]
        x, y, c = _mesh_pos()
        me = 2 * x + y
        peers = [(1 - x, y), (x, 1 - y), (1 - x, 1 - y)]

        def copy(a, j, px, py, dst_slot):
            src = ins[a].at[2 * px + py] if scatter else ins[a]
            return pltpu.make_async_remote_copy(src_ref=src, dst_ref=outs[a].at[dst_slot],
                                                send_sem=send_sems.at[3 * a + j], recv_sem=recv_sems.at[3 * a + j],
                                                device_id=(px, py, c), device_id_type=MESH)

        mine = [pltpu.make_async_copy(ins[a].at[me] if scatter else ins[a], outs[a].at[me], local_sems.at[a])
                for a in range(n)]
        sends = [copy(a, j, px, py, me) for a in range(n) for j, (px, py) in enumerate(peers)]
        for cp in mine + sends:
            cp.start()
        for a in range(n):
            for j, (px, py) in enumerate(peers):
                copy(a, j, px, py, 2 * px + py).wait_recv()
        for cp in sends:
            cp.wait_send()
        for cp in mine:
            cp.wait()

    return pl.pallas_call(
        body, name=name, in_specs=[ANY] * n, out_specs=[ANY] * n,
        out_shape=[jax.ShapeDtypeStruct((NCHIP,) + tuple(shp), b.dtype) for shp, b in zip(shapes, bufs)],
        scratch_shapes=[pltpu.SemaphoreType.DMA((3 * n,)), pltpu.SemaphoreType.DMA((3 * n,)),
                        pltpu.SemaphoreType.DMA((n,))],
    )(*bufs)


HBM_SPEC = pl.BlockSpec(memory_space=pltpu.HBM)
SEM_SPEC = pl.BlockSpec(memory_space=pltpu.SEMAPHORE)
DATAFLOW = pltpu.SideEffectType.DATAFLOW_SIDE_EFFECTING


def _exchange_copy(ins, lands, send_sems, recv_sems, scatter, a, j, px, py, c, dst_slot):
    src = ins[a].at[2 * px + py] if scatter else ins[a]
    return pltpu.make_async_remote_copy(src_ref=src, dst_ref=lands[a].at[dst_slot],
                                        send_sem=send_sems.at[3 * a + j], recv_sem=recv_sems.at[3 * a + j],
                                        device_id=(px, py, c), device_id_type=MESH)


def _own_copy(ins, lands, local_sems, scatter, a, me):
    return pltpu.make_async_copy(ins[a].at[me] if scatter else ins[a], lands[a].at[me], local_sems.at[a])


def exchange_start(bufs, scatter, name):
    n = len(bufs)
    lands = [lax.empty((NCHIP,) + tuple(b.shape[1:] if scatter else b.shape), b.dtype) for b in bufs]

    def body(*refs):
        ins, lnd, send_sems, recv_sems, local_sems, token = (refs[:n], refs[n:2 * n], refs[2 * n], refs[2 * n + 1],
                                                             refs[2 * n + 2], refs[-1])
        x, y, c = _mesh_pos()
        me = 2 * x + y
        for a in range(n):
            _own_copy(ins, lnd, local_sems, scatter, a, me).start()
            for j, (px, py) in enumerate([(1 - x, y), (x, 1 - y), (1 - x, 1 - y)]):
                _exchange_copy(ins, lnd, send_sems, recv_sems, scatter, a, j, px, py, c, me).start()
        token[...] = jnp.zeros_like(token)

    thru = [pltpu.HBM(b.shape, b.dtype) for b in list(bufs) + lands]
    out = pl.pallas_call(
        body, name=name, in_specs=[HBM_SPEC] * (2 * n),
        out_specs=[SEM_SPEC] * 3 + [HBM_SPEC] * (2 * n) + [pl.BlockSpec(memory_space=pltpu.VMEM)],
        out_shape=[pltpu.SemaphoreType.DMA((3 * n,)), pltpu.SemaphoreType.DMA((3 * n,)), pltpu.SemaphoreType.DMA((n,))]
        + thru + [jax.ShapeDtypeStruct((SLAB, LANE), F32)],
        input_output_aliases={k: 3 + k for k in range(2 * n)},
        compiler_params=pltpu.CompilerParams(has_side_effects=DATAFLOW),
    )(*[pltpu.with_memory_space_constraint(b, pltpu.HBM) for b in list(bufs) + lands])
    return tuple(out[:3]), out[3:3 + n], out[3 + n:3 + 2 * n], out[-1][0, 0]


def exchange_wait(started, after, scatter, name):
    sems, bufs, lands, _ = started
    n = len(bufs)

    def body(*refs):
        ins, lnd, ssem, rsem, lsem = refs[:n], refs[n:2 * n], refs[2 * n], refs[2 * n + 1], refs[2 * n + 2]
        x, y, c = _mesh_pos()
        for a in range(n):
            _own_copy(ins, lnd, lsem, scatter, a, 2 * x + y).wait()
            for j, (px, py) in enumerate([(1 - x, y), (x, 1 - y), (1 - x, 1 - y)]):
                cp = _exchange_copy(ins, lnd, ssem, rsem, scatter, a, j, px, py, c, 2 * px + py)
                cp.wait_send()
                cp.wait_recv()

    thru = [pltpu.HBM(b.shape, b.dtype) for b in list(bufs) + list(lands)]
    out = pl.pallas_call(
        body, name=name, in_specs=[HBM_SPEC] * (2 * n) + [SEM_SPEC] * 3 + [ANY],
        out_specs=[HBM_SPEC] * (2 * n), out_shape=thru,
        input_output_aliases={k: k for k in range(2 * n)},
        compiler_params=pltpu.CompilerParams(has_side_effects=DATAFLOW),
    )(*bufs, *lands, *sems, after)
    return out[n:]


def sibling_swap(bufs, name):
    n = len(bufs)

    def body(*refs):
        ins, outs, send_sems, recv_sems = refs[:n], refs[n:2 * n], refs[2 * n], refs[2 * n + 1]
        x, y, c = _mesh_pos()
        cps = [pltpu.make_async_remote_copy(src_ref=ins[k], dst_ref=outs[k], send_sem=send_sems.at[k],
                                            recv_sem=recv_sems.at[k], device_id=(x, y, 1 - c), device_id_type=MESH)
               for k in range(n)]
        for cp in cps:
            cp.start()
        for cp in cps:
            cp.wait()

    return pl.pallas_call(
        body, name=name, in_specs=[ANY] * n, out_specs=[ANY] * n,
        out_shape=[jax.ShapeDtypeStruct(b.shape, b.dtype) for b in bufs],
        scratch_shapes=[pltpu.SemaphoreType.DMA((n,)), pltpu.SemaphoreType.DMA((n,))],
    )(*bufs)


def _swap_copy(ins, lands, send_sems, recv_sems, a):
    x, y, c = _mesh_pos()
    return pltpu.make_async_remote_copy(src_ref=ins[a], dst_ref=lands[a], send_sem=send_sems.at[a],
                                        recv_sem=recv_sems.at[a], device_id=(x, y, 1 - c), device_id_type=MESH)


def swap_start(bufs, name):
    n = len(bufs)
    lands = [lax.empty(b.shape, b.dtype) for b in bufs]

    def body(*refs):
        for a in range(n):
            _swap_copy(refs[:n], refs[n:2 * n], refs[2 * n], refs[2 * n + 1], a).start()
        refs[-1][...] = jnp.zeros_like(refs[-1])

    thru = [pltpu.HBM(b.shape, b.dtype) for b in list(bufs) + lands]
    out = pl.pallas_call(
        body, name=name, in_specs=[HBM_SPEC] * (2 * n),
        out_specs=[SEM_SPEC] * 2 + [HBM_SPEC] * (2 * n) + [pl.BlockSpec(memory_space=pltpu.VMEM)],
        out_shape=[pltpu.SemaphoreType.DMA((n,)), pltpu.SemaphoreType.DMA((n,))] + thru
        + [jax.ShapeDtypeStruct((SLAB, LANE), F32)],
        input_output_aliases={k: 2 + k for k in range(2 * n)},
        compiler_params=pltpu.CompilerParams(has_side_effects=DATAFLOW),
    )(*[pltpu.with_memory_space_constraint(b, pltpu.HBM) for b in list(bufs) + lands])
    return tuple(out[:2]), out[2:2 + n], out[2 + n:2 + 2 * n], out[-1][0, 0]


def swap_wait(started, after, name):
    sems, bufs, lands, _ = started
    n = len(bufs)

    def body(*refs):
        for a in range(n):
            _swap_copy(refs[:n], refs[n:2 * n], refs[2 * n], refs[2 * n + 1], a).wait()

    thru = [pltpu.HBM(b.shape, b.dtype) for b in list(bufs) + list(lands)]
    out = pl.pallas_call(
        body, name=name, in_specs=[HBM_SPEC] * (2 * n) + [SEM_SPEC] * 2 + [ANY],
        out_specs=[HBM_SPEC] * (2 * n), out_shape=thru,
        input_output_aliases={k: k for k in range(2 * n)},
        compiler_params=pltpu.CompilerParams(has_side_effects=DATAFLOW),
    )(*bufs, *lands, *sems, after)
    return out[n:]


EW_VMEM_BUDGET = 20 * 1024 * 1024


def _ew_rows(rows, w, bytes_per_elem):
    wpad = -(-w // LANE) * LANE
    for t in (1024, 512, 256, 128, 64, 32, 16, 8):
        if rows % t == 0 and 2 * t * wpad * bytes_per_elem <= EW_VMEM_BUDGET:
            return t
    return rows


def sum_slots(buf, *, name):
    _, rows, w = buf.shape
    tm = _ew_rows(rows, w, NCHIP * buf.dtype.itemsize + 4)

    def body(b_ref, o_ref):
        acc = b_ref[0].astype(F32)
        for kk in range(1, NCHIP):
            acc = acc + b_ref[kk].astype(F32)
        o_ref[...] = acc

    return pl.pallas_call(body, name=name, grid=(rows // tm,),
                          in_specs=[pl.BlockSpec((NCHIP, tm, w), lambda i: (0, i, 0))],
                          out_specs=pl.BlockSpec((tm, w), lambda i: (i, 0)),
                          out_shape=jax.ShapeDtypeStruct((rows, w), F32),
                          compiler_params=_cparams("parallel"))(buf)


def adamw(p_mine, p_other, w, m, v, *, name):
    rows, wd = w.shape
    tm = _ew_rows(rows, wd, 9 * 4)
    c1 = 1.0 - ADAM_B1 ** ADAM_STEP
    c2 = 1.0 - ADAM_B2 ** ADAM_STEP

    def body(a_ref, b_ref, w_ref, m_ref, v_ref, g_ref, d_ref, nm_ref, nv_ref):
        g = a_ref[...] + b_ref[...]
        nm = ADAM_B1 * m_ref[...] + (1.0 - ADAM_B1) * g
        nv = ADAM_B2 * v_ref[...] + (1.0 - ADAM_B2) * (g * g)
        g_ref[...] = g
        nm_ref[...] = nm
        nv_ref[...] = nv
        d_ref[...] = -ADAM_LR * ((nm / c1) / (jnp.sqrt(nv / c2) + ADAM_EPS) + ADAM_WD * w_ref[...])

    row = pl.BlockSpec((tm, wd), lambda i: (i, 0))
    return pl.pallas_call(body, name=name, grid=(rows // tm,), in_specs=[row] * 5, out_specs=[row] * 4,
                          out_shape=[jax.ShapeDtypeStruct((rows, wd), F32)] * 4,
                          compiler_params=_cparams("parallel"))(p_mine, p_other, w, m, v)


def _rows2d(a, lead=0):
    tail = a.shape[lead:]
    n = int(np.prod(tail))
    if tail[-1] < LANE // 2 and n % (8 * LANE) == 0:
        return a.reshape(a.shape[:lead] + (n // (8 * LANE), 8 * LANE))
    return a.reshape(a.shape[:lead] + (-1, tail[-1]))


BIG = [("mix_w_in", 2), ("w_uq", 2), ("w_ukv", 2), ("mix_w_out", 1), ("ssm_w_in", 1), ("w_glu", 2),
       ("ffn_w_up", 2), ("ffn_w_down", 1)]
SMALL = [("sconv_w", 2), ("ssm_norm", 1), ("d_skip", 1), ("ffn_conv_w", 2)]
REPL = ["attn_norm", "cq_norm", "ckv_norm", "q_gain", "k_gain", "lambda_re", "lambda_im", "log_step",
        "b_re", "b_im", "c_re", "c_im", "ffn_norm"]
ORDER = ["attn_norm", "mix_w_in", "cq_norm", "ckv_norm", "w_uq", "w_ukv", "q_gain", "k_gain", "sconv_w", "mix_w_out",
         "ssm_norm", "ssm_w_in", "lambda_re", "lambda_im", "log_step", "b_re", "b_im", "c_re", "c_im", "d_skip",
         "w_glu", "ffn_norm", "ffn_w_up", "ffn_conv_w", "ffn_w_down"]


def _join(g, axis):
    if axis == 0:
        return g.reshape((-1,) + g.shape[2:])
    return jnp.concatenate([g[k] for k in range(NCHIP)], axis=axis)


def _split(full, axis, parts=NCHIP):
    if axis == 0:
        return full.reshape((parts, -1) + full.shape[1:])
    return jnp.stack(jnp.split(full, parts, axis=axis))


def _discretize(lr, li, ls, b_re, b_im):
    dt = jnp.exp(ls)[:, None]
    mag = jnp.exp(lr * dt)
    ar, ai = mag * jnp.cos(li * dt), mag * jnp.sin(li * dt)
    nr, ni = ar - 1.0, ai
    den = lr * lr + li * li
    zr, zi = (nr * lr + ni * li) / den, (ni * lr - nr * li) / den
    bbar_r = zr[..., None] * b_re - zi[..., None] * b_im
    bbar_i = zr[..., None] * b_im + zi[..., None] * b_re
    return ar, ai, bbar_r, bbar_i


def _b_blockdiag(bbar):
    gl = G // NJ
    bb = bbar.reshape(NJ, gl, P, GC).transpose(0, 1, 3, 2)
    return jnp.einsum("jgcp,gh->jgchp", bb, jnp.eye(gl, dtype=bbar.dtype)).reshape(NJ, gl * GC, gl * P)


def _b_blockdiag_t(dbd):
    gl = G // NJ
    d = jnp.einsum("jgchp,gh->jgcp", dbd.reshape(NJ, gl, GC, gl, P), jnp.eye(gl, dtype=dbd.dtype))
    return d.transpose(0, 1, 3, 2).reshape(G, P, GC)


def _c_blockdiag(cmat):
    gl = G // NJ
    cc = cmat.reshape(NJ, gl, GC, P).transpose(0, 1, 3, 2)
    return jnp.einsum("jgpc,gh->jgphc", cc, jnp.eye(gl, dtype=cmat.dtype)).reshape(NJ, gl * P, gl * GC)


def _c_blockdiag_t(dbd):
    gl = G // NJ
    d = jnp.einsum("jgphc,gh->jgpc", dbd.reshape(NJ, gl, P, gl, GC), jnp.eye(gl, dtype=dbd.dtype))
    return d.transpose(0, 1, 3, 2).reshape(G, GC, P)


def _pad_heads_cols(w, width):
    r = w.shape[0]
    return jnp.pad(w.reshape(r, HEADS, width), ((0, 0), (0, 0), (0, HP - width))).reshape(r, HEADS * HP)


def _unpad_heads_cols(w, width):
    r = w.shape[0]
    return w.reshape(r, HEADS, HP)[:, :, :width].reshape(r, HEADS * width)


W_IN_SPLIT = (QR + KVR, QR + KVR + ROPE)


def _w_in_layout(w):
    a, b = W_IN_SPLIT
    kr = jnp.pad(w[:, a:b], ((0, 0), (NOPE, HP - QK)))
    return jnp.concatenate([w[:, :a], w[:, b:], kr], axis=1)


def _ffn_fwd(x, l, wt, name):
    h = rms_fwd(x, wt["ffn_norm"][l][None], name=f"{name}_norm")
    upg, upv, act = ffn_up_act(h, wt["ffn_w_up"][l], wt["ffn_conv_w"][l], name=f"{name}_upact")
    up = (upg, upv)
    out = mm(act, wt["ffn_w_down"][l], add=x, name=f"{name}_down")
    return out, (x, h, up, act)


def _ffn_bwd(dout, douth, saved, l, wt, name):
    x, h, up, act = saved
    g = {}
    g["ffn_w_down"] = mm(act, douth, ta=True, out_dtype=BF16, name=f"{name}_dwdown")
    dup, g["ffn_conv_w"] = ffnact_bwd(up, wt["ffn_conv_w"][l], douth, wt["ffn_w_down"][l], name=f"{name}_dact")
    g["ffn_w_up"] = tuple(mm(h, d, ta=True, out_dtype=BF16, name=f"{name}_dwup{kk}") for kk, d in enumerate(dup))
    dh = mm_nt_segments(dup, wt["ffn_w_up"][l], name=f"{name}_dup")
    dx, dxh, dg = rms_bwd(x, wt["ffn_norm"][l][None], dh, add=dout, twin=True, name=f"{name}_dnorm")
    g["ffn_norm"] = dg[0]
    return dx, dxh, g


def _even_fwd(x, i, wt, tabs, name):
    h = rms_fwd(x, wt["attn_norm"][i][None], name=f"{name}_norm")
    proj = mm(h, wt["w_in2"][i], name=f"{name}_in")
    cqn = rms_fwd(proj, wt["cq_norm"][i][None], col=0, name=f"{name}_cqnorm")
    ckvn = rms_fwd(proj, wt["ckv_norm"][i][None], col=1, name=f"{name}_ckvnorm")
    qraw = mm(cqn, wt["w_uq_p"][i], name=f"{name}_uq")
    kv = mm(ckvn, wt["w_ukv_p"][i], name=f"{name}_ukv")
    q, k, v = qkprep_fwd(qraw, kv, proj, wt["q_gain_p"][i], wt["k_gain_p"][i], tabs, name=f"{name}_qkprep")
    o, oh, lset = attn_fwd(q, k, v, name=f"{name}_attn")
    conv = sconv_fwd(proj, wt["sconv_w"][i], name=f"{name}_sconv")
    t = mm(oh, wt["w_out_a"][i], add=x, name=f"{name}_outa")
    out = mm(conv, wt["w_out_c"][i], add=t, name=f"{name}_outc")
    return out, (x, h, proj, cqn, ckvn, qraw, kv, q, k, v, o, oh, lset, conv)


def _even_bwd(dout, douth, saved, i, wt, tabs, name):
    x, h, proj, cqn, ckvn, qraw, kv, q, k, v, o, oh, lset, conv = saved
    g = {}
    do = mm(douth, wt["w_out_a"][i], tb=True, name=f"{name}_douta")
    dconv = mm(douth, wt["w_out_c"][i], tb=True, name=f"{name}_doutc")
    g["w_out_a"] = mm(oh, douth, ta=True, out_dtype=BF16, name=f"{name}_dwouta")
    g["w_out_c"] = mm(conv, douth, ta=True, out_dtype=BF16, name=f"{name}_dwoutc")
    dgates, g["sconv_w"] = sconv_bwd(proj, wt["sconv_w"][i], dconv, name=f"{name}_dsconv")
    doh, deltat = attn_delta(o, do, name=f"{name}_dattn_delta")
    dq, dk, dv = attn_bwd(q, k, v, doh, lset, deltat, name=f"{name}_dattn")
    dqraw, dkraw, dkrope, dqg, dkg = qkprep_bwd(qraw, kv, proj, wt["q_gain_p"][i], wt["k_gain_p"][i], tabs, dq, dk,
                                                name=f"{name}_dqkprep")
    g["q_gain"], g["k_gain"] = dqg[0, :QK], dkg[0, :QK]
    dcqn = mm(dqraw, wt["w_uq_p"][i], tb=True, name=f"{name}_duq")
    g["w_uq_p"] = mm(cqn, dqraw, ta=True, out_dtype=BF16, name=f"{name}_dwuq")
    dkv = (dkraw, dv)
    dckvn = mm_nt_segments(dkv, wt["w_ukv_p"][i], name=f"{name}_dukv")
    g["w_ukv_p"] = tuple(mm(ckvn, d, ta=True, out_dtype=BF16, name=f"{name}_dwukv{kk}") for kk, d in enumerate(dkv))
    dcq, dgq = rms_bwd(proj, wt["cq_norm"][i][None], dcqn, col=0, out_dtype=BF16, name=f"{name}_dcqnorm")
    dckv, dgkv = rms_bwd(proj, wt["ckv_norm"][i][None], dckvn, col=1, out_dtype=BF16, name=f"{name}_dckvnorm")
    g["cq_norm"], g["ckv_norm"] = dgq[0], dgkv[0]
    dproj = (dcq, dckv, *dgates, dkrope)
    g["w_in2"] = tuple(mm(h, d, ta=True, out_dtype=BF16, name=f"{name}_dwin{kk}") for kk, d in enumerate(dproj))
    dh = mm_nt_segments(dproj, wt["w_in2"][i], name=f"{name}_din")
    dx, dxh, dg = rms_bwd(x, wt["attn_norm"][i][None], dh, add=dout, twin=True, name=f"{name}_dnorm")
    g["attn_norm"] = dg[0]
    return dx, dxh, g


def _odd_fwd(x, i, wt, name):
    h = rms_fwd(x, wt["ssm_norm"][i][None], name=f"{name}_norm")
    u = mm(h, wt["ssm_w_in"][i], name=f"{name}_in")
    y, st_r, st_i = s5_fwd(u, wt["bbd_r"][i], wt["bbd_i"][i], wt["cbd_r"][i], wt["cbd_i"][i], wt["tab_f"][i],
                           name=f"{name}_scan")
    gl = s5post_fwd(y, u, wt["d_skip"][i][None], name=f"{name}_gelu")
    glu = mm(gl, wt["w_glu"][i], name=f"{name}_glu")
    out = glu_fwd(glu, x, name=f"{name}_gate")
    return out, (x, h, u, y, st_r, st_i, gl, glu)


def _odd_bwd(dout, douth, saved, i, wt, name):
    x, h, u, y, st_r, st_i, gl, glu = saved
    g = {}
    dglu = glu_bwd(glu, dout, name=f"{name}_dgate")
    g["w_glu"] = mm(gl, dglu, ta=True, out_dtype=BF16, name=f"{name}_dwglu")
    dgl = mm(dglu, wt["w_glu"][i], tb=True, name=f"{name}_dglu")
    dz, dd = s5post_bwd(y, u, wt["d_skip"][i][None], dgl, name=f"{name}_dgelu")
    g["d_skip"] = dd[0]
    du, g["bbd_r"], g["bbd_i"], g["cbd_r"], g["cbd_i"], g["a"] = s5_bwd(
        u, dz, wt["d_skip"][i][None], st_r, st_i, wt["bbd_r"][i], wt["bbd_i"][i], wt["cbd_r"][i], wt["cbd_i"][i],
        wt["tab_r"][i], name=f"{name}_dscan")
    g["ssm_w_in"] = mm(h, du, ta=True, out_dtype=BF16, name=f"{name}_dwin")
    dh = mm(du, wt["ssm_w_in"][i], tb=True, name=f"{name}_din")
    dx, dxh, dg = rms_bwd(x, wt["ssm_norm"][i][None], dh, add=dout, twin=True, name=f"{name}_dnorm")
    g["ssm_norm"] = dg[0]
    return dx, dxh, g


MATMUL_WEIGHTS = {"even": ("mix_w_in", "w_uq", "w_ukv", "mix_w_out"), "odd": ("ssm_w_in", "w_glu"),
                  "ffn": ("ffn_w_up", "ffn_w_down")}
ODD_SMALL = ("lambda_re", "lambda_im", "log_step", "b_re", "b_im", "c_re", "c_im", "ssm_norm", "d_skip")


def _even_layouts(fw, wt, i):
    wt["w_in2"][i] = _w_in_layout(fw["mix_w_in"])
    wt["w_uq_p"][i] = _pad_heads_cols(fw["w_uq"], QK)
    ukv = fw["w_ukv"].reshape(KVR, HEADS, NOPE + VD)
    wt["w_ukv_p"][i] = jnp.concatenate(
        [_pad_heads_cols(ukv[:, :, :NOPE].reshape(KVR, HEADS * NOPE), NOPE),
         _pad_heads_cols(ukv[:, :, NOPE:].reshape(KVR, HEADS * VD), VD)], axis=1)
    wt["w_out_a"][i] = _pad_heads_cols(fw["mix_w_out"][:HEADS * VD].T, VD).T
    wt["w_out_c"][i] = fw["mix_w_out"][HEADS * VD:]


def _even_layouts_t(g):
    dk_, dv_ = g["w_ukv_p"]
    dcq, dckv, dgb, dgc, dci, dkr = g["w_in2"]
    return {"mix_w_in": jnp.concatenate([dcq, dckv, dkr[:, NOPE:QK], dgb, dgc, dci], axis=1),
            "w_uq": _unpad_heads_cols(g["w_uq_p"], QK),
            "w_ukv": jnp.concatenate([dk_.reshape(KVR, HEADS, HP)[:, :, :NOPE], dv_.reshape(KVR, HEADS, HP)[:, :, :VD]],
                                     axis=2).reshape(KVR, HEADS * (NOPE + VD)),
            "mix_w_out": jnp.concatenate([_unpad_heads_cols(g["w_out_a"].T, VD).T, g["w_out_c"]], axis=0)}


def _local_step(x, target, full, getw, putg):
    s = x.shape[0]
    n_even = (DEPTH + 1) // 2
    n_odd = DEPTH // 2
    tabs = _rope_tables(s)
    wt = dict(full)
    for key in ("w_in2", "w_uq_p", "w_ukv_p", "w_out_a", "w_out_c") + sum(MATMUL_WEIGHTS.values(), ()):
        wt[key] = {}
    wt["q_gain_p"] = jnp.pad(full["q_gain"], ((0, 0), (0, HP - QK)))[:, None, :]
    wt["k_gain_p"] = jnp.pad(full["k_gain"], ((0, 0), (0, HP - QK)))[:, None, :]

    disc_vjp = []
    for key in ("bbd_r", "bbd_i", "cbd_r", "cbd_i", "tab_f", "tab_r"):
        wt[key] = []
    for i in range(n_odd):
        (ar, ai, bbr, bbi), vjp = jax.vjp(_discretize, full["lambda_re"][i], full["lambda_im"][i], full["log_step"][i],
                                          full["b_re"][i], full["b_im"][i])
        disc_vjp.append(vjp)
        tf, tr = _scan_tables(ar.reshape(-1), ai.reshape(-1))
        wt["tab_f"].append(tf)
        wt["tab_r"].append(tr)
        wt["bbd_r"].append(_b_blockdiag(bbr).astype(BF16))
        wt["bbd_i"].append(_b_blockdiag(bbi).astype(BF16))
        wt["cbd_r"].append(_c_blockdiag(full["c_re"][i]).astype(BF16))
        wt["cbd_i"].append(_c_blockdiag(full["c_im"][i]).astype(BF16))

    saved = []
    for layer in range(DEPTH):
        i = layer // 2
        if layer % 2 == 0:
            fw, tok = getw("even", i, x)
            wt["attn_norm"] = full["attn_norm"] + tok
            _even_layouts(fw, wt, i)
            x, sm = _even_fwd(x, i, wt, tabs, f"l{layer}_mla")
        else:
            fw, tok = getw("odd", i, x)
            wt["ssm_norm"] = full["ssm_norm"] + tok
            for n, a in fw.items():
                wt[n][i] = a
            x, sm = _odd_fwd(x, i, wt, f"l{layer}_s5")
        fw, tok = getw("ffn", layer, x)
        wt["ffn_norm"] = full["ffn_norm"] + tok
        for n, a in fw.items():
            wt[n][layer] = a
        x, sf = _ffn_fwd(x, layer, wt, f"l{layer}_ffn")
        saved.append((sm, sf))
    dx, dxh, lslab = loss_head(x, target, name="loss_head")

    own = [n for n in ORDER if n not in sum(MATMUL_WEIGHTS.values(), ())]
    grads = {n: [None] * (DEPTH if n.startswith("ffn") else n_even) for n in own}
    tok = 0.0
    for layer in reversed(range(DEPTH)):
        i = layer // 2
        sm, sf = saved[layer]
        wt["ffn_conv_w"] = full["ffn_conv_w"] + tok
        dx, dxh, g = _ffn_bwd(dx, dxh, sf, layer, wt, f"l{layer}_ffn")
        tok = putg("ffn", layer, {n: g[n] for n in MATMUL_WEIGHTS["ffn"]})
        for n in ("ffn_norm", "ffn_conv_w"):
            grads[n][layer] = g[n]
        if layer % 2 == 0:
            wt["sconv_w"] = full["sconv_w"] + tok
            dx, dxh, g = _even_bwd(dx, dxh, sm, i, wt, tabs, f"l{layer}_mla")
            tok = putg("even", i, _even_layouts_t(g))
            for n in ("attn_norm", "cq_norm", "ckv_norm", "q_gain", "k_gain", "sconv_w"):
                grads[n][i] = g[n]
        else:
            wt["d_skip"] = full["d_skip"] + tok
            dx, dxh, g = _odd_bwd(dx, dxh, sm, i, wt, f"l{layer}_s5")
            tok = putg("odd", i, {n: g[n] for n in MATMUL_WEIGHTS["odd"]})
            dlr, dli, dls, dbr, dbi = disc_vjp[i]((g["a"][0].reshape(G, P), g["a"][1].reshape(G, P),
                                                    _b_blockdiag_t(g["bbd_r"]), _b_blockdiag_t(g["bbd_i"])))
            grads["lambda_re"][i], grads["lambda_im"][i], grads["log_step"][i] = dlr, dli, dls
            grads["b_re"][i], grads["b_im"][i] = dbr, dbi
            grads["c_re"][i], grads["c_im"][i] = _c_blockdiag_t(g["cbd_r"]), _c_blockdiag_t(g["cbd_i"])
            for n in ("ssm_norm", "d_skip"):
                grads[n][i] = g[n]
            if i == 0:
                tok = tok + putg("odd_small", 0, {n: jnp.stack(grads[n]) for n in ODD_SMALL})
    grads = {n: jnp.stack(v) for n, v in grads.items()}
    return jnp.sum(lslab), dx, grads


def kernel(x, attn_norm, mix_w_in, cq_norm, ckv_norm, w_uq, w_ukv, q_gain, k_gain, sconv_w, mix_w_out, ssm_norm, ssm_w_in, lambda_re, lambda_im, log_step, b_re, b_im, c_re, c_im, d_skip, w_glu, ffn_norm, ffn_w_up, ffn_conv_w, ffn_w_down, loss_target, m_attn_norm, m_mix_w_in, m_cq_norm, m_ckv_norm, m_w_uq, m_w_ukv, m_q_gain, m_k_gain, m_sconv_w, m_mix_w_out, m_ssm_norm, m_ssm_w_in, m_lambda_re, m_lambda_im, m_log_step, m_b_re, m_b_im, m_c_re, m_c_im, m_d_skip, m_w_glu, m_ffn_norm, m_ffn_w_up, m_ffn_conv_w, m_ffn_w_down, v_attn_norm, v_mix_w_in, v_cq_norm, v_ckv_norm, v_w_uq, v_w_ukv, v_q_gain, v_k_gain, v_sconv_w, v_mix_w_out, v_ssm_norm, v_ssm_w_in, v_lambda_re, v_lambda_im, v_log_step, v_b_re, v_b_im, v_c_re, v_c_im, v_d_skip, v_w_glu, v_ffn_norm, v_ffn_w_up, v_ffn_conv_w, v_ffn_w_down):
    args = dict(locals())
    w = {n: args[n] for n in ORDER}
    m = {n: args["m_" + n] for n in ORDER}
    v = {n: args["v_" + n] for n in ORDER}
    me = 2 * lax.axis_index("x") + lax.axis_index("y")

    axis = dict(BIG)

    gs = _chip_exchange([w[n] for n, _ in SMALL], False, "gather_w_f32")
    full = {n: w[n] for n in REPL}
    for (n, ax), g in zip(SMALL, gs):
        full[n] = _join(g, ax)
    parts = [(("even", "odd")[layer % 2], layer // 2) for layer in range(DEPTH)]
    parts = [p for layer, mixer in enumerate(parts) for p in (mixer, ("ffn", layer))]
    gathers = {}

    def start_gather(kind, idx, zero):
        shards = [(w[n][idx] + zero).astype(BF16) for n in MATMUL_WEIGHTS[kind]]
        gathers[kind, idx] = exchange_start(shards, False, f"gather_start_{kind}{idx}")

    start_gather(*parts[0], 0.0 * gs[0][(0,) * gs[0].ndim])

    def getw(kind, idx, after):
        got = exchange_wait(gathers[kind, idx], after, False, f"gather_wait_{kind}{idx}")
        nxt = parts.index((kind, idx)) + 1
        tok = 0.0
        if nxt < len(parts):
            start_gather(*parts[nxt], 0.0 * got[0][(0,) * got[0].ndim].astype(F32))
            tok = gathers[parts[nxt]][3]
        return {n: _join(g, axis[n] - 1) for n, g in zip(MATMUL_WEIGHTS[kind], got)}, tok

    scatters, early = [], []

    def putg(kind, idx, g):
        if kind == "odd_small":
            arrs = [_rows2d(g[n]) for n in ODD_SMALL]
            early.append(exchange_start(arrs, False, "gather_start_g_odd"))
            return early[0][3]
        blocks = [jnp.concatenate([_split(part, axis[n] - 1, NCHIP // len(g[n])) for part in g[n]])
                  if isinstance(g[n], tuple) else _split(g[n], axis[n] - 1) for n in MATMUL_WEIGHTS[kind]]
        tok = 0.0
        if scatters:
            tok = reduce_part(*scatters.pop(), blocks[0])
        scatters.append((kind, idx, exchange_start(blocks, True, f"scatter_start_{kind}{idx}")))
        return tok + scatters[-1][2][3]

    swaps = []

    def reduce_part(kind, idx, started, after):
        got = exchange_wait(started, after, True, f"scatter_wait_{kind}{idx}")
        sums = [sum_slots(_rows2d(sl, 1), name=f"sum_{n}{idx}") for n, sl in zip(MATMUL_WEIGHTS[kind], got)]
        swaps.append((kind, idx, sums, swap_start(sums, f"swap_start_{kind}{idx}")))
        return swaps[-1][3][3]

    sq, dx, grads = _local_step(x[0], loss_target[0], full, getw, putg)
    loss = lax.psum(0.5 * sq / D_MODEL, ("x", "y", "c"))

    late_names = [n for n in REPL + [n for n, _ in SMALL] if n not in ODD_SMALL]
    rep_names = list(ODD_SMALL) + late_names
    names = [n for n, _ in BIG] + rep_names
    late = [_rows2d(grads[n]) for n in late_names]
    late_started = exchange_start(late, False, "gather_start_g_f32")
    reduce_part(*scatters.pop(), dx)
    slots = (exchange_wait(early[0], dx, False, "gather_wait_g_odd")
             + exchange_wait(late_started, swaps[-1][2][0], False, "gather_wait_g_f32"))
    small_mine = [sum_slots(_rows2d(sl, 1), name=f"sum_{n}") for n, sl in zip(rep_names, slots)]
    small_other = list(sibling_swap(small_mine, "swap_g_small"))
    summed, swapped = {}, {}
    for kind, idx, sums, started in swaps:
        for n, p, q in zip(MATMUL_WEIGHTS[kind], sums, swap_wait(started, small_other[0], f"swap_wait_{kind}{idx}")):
            summed[n, idx], swapped[n, idx] = p, q
    mine = [jnp.concatenate([summed[n, idx] for idx in range(w[n].shape[0])], axis=0) for n, _ in BIG] + small_mine
    other = [jnp.concatenate([swapped[n, idx] for idx in range(w[n].shape[0])], axis=0) for n, _ in BIG] + small_other

    def local(n, p):
        ax = dict(SMALL).get(n)
        if ax is None:
            return p
        part = lax.dynamic_index_in_dim(_split(p.reshape(grads[n].shape), ax), me, 0, keepdims=False)
        return _rows2d(part)

    outs = {}
    for n, p, q in zip(names, mine, other):
        res = adamw(local(n, p), local(n, q), _rows2d(w[n]), _rows2d(m[n]), _rows2d(v[n]), name=f"adamw_{n}")
        outs[n] = [r.reshape(w[n].shape) for r in res]
    return (loss, dx[None], *[outs[n][0] for n in ORDER], *[outs[n][1] for n in ORDER],
            *[outs[n][2] for n in ORDER], *[outs[n][3] for n in ORDER])
```

```python
import math

import numpy as np
import jax
import jax.numpy as jnp
from jax import lax
from jax.experimental import pallas as pl
from jax.experimental.pallas import tpu as pltpu

F32, BF16 = jnp.float32, jnp.bfloat16

D_MODEL = 1024
DEPTH = 4
HEADS = 8
NOPE, ROPE, QK, VD = 64, 32, 96, 64
HP = 128
QR, KVR = 256, 256
CONVC = 512
G, P, GC = 64, 64, 16
NST = G * P
SLAB = 8
LANE = 128
EPS = 1e-6
ROPE_THETA = 10000.0
ADAM_LR, ADAM_B1, ADAM_B2, ADAM_EPS, ADAM_WD, ADAM_STEP = 0.001, 0.9, 0.999, 1e-08, 0.01, 10
VMEM_LIMIT = 48 * 1024 * 1024
MESH = pl.DeviceIdType.MESH
ANY = pl.BlockSpec(memory_space=pl.ANY)


def _cparams(*sem):
    return pltpu.CompilerParams(dimension_semantics=sem, vmem_limit_bytes=VMEM_LIMIT)


def _pick(dim, prefs):
    for p in prefs:
        if dim % p == 0:
            return p
    return dim


def _rows(s):
    return _pick(s, (512, 256, 128, 64, 32, 16, 8))


def _long_rows(s):
    return _pick(s, (1024, 512, 256, 128, 64, 32, 16, 8))


MM_VMEM_BUDGET = 36 * 1024 * 1024
MM_MAX_TILE_ELEMS = 640 * 1024
HBM_BYTES_PER_US = 3.0e6
STEP_OVERHEAD_US = 0.35


def _lane_tiles(n):
    c = {t for t in range(LANE, min(n, 1536) + 1, LANE) if n % t == 0}
    if n <= 2304 or not c:
        c.add(n)
    return sorted(c, reverse=True)


def _mm_tiles(m, n, k, sa, sb, so):
    best = None
    for tm in [t for t in (1024, 512, 256) if m % t == 0] or [m]:
        for tn in _lane_tiles(n):
            if tm * tn > MM_MAX_TILE_ELEMS:
                continue
            if 2 * (tm * k * sa + k * tn * sb + tm * tn * so) + 4 * tm * tn > MM_VMEM_BUDGET:
                continue
            steps = (m // tm) * (n // tn)
            for inner_n in (True, False):
                moved = (m * k * sa + (m // tm) * k * n * sb) if inner_n else (k * n * sb + (n // tn) * m * k * sa)
                cost = (moved + m * n * so) / HBM_BYTES_PER_US + steps * STEP_OVERHEAD_US
                if best is None or cost < best[0]:
                    best = (cost, tm, tn, inner_n)
    return best[1:]


def mm(a, b, *, ta=False, tb=False, add=None, out_dtype=F32, name):
    if ta:
        kdim, m = a.shape
    else:
        m, kdim = a.shape
    n = b.shape[0] if tb else b.shape[1]
    so = jnp.dtype(out_dtype).itemsize + (0 if add is None else add.dtype.itemsize)
    tm, tn, inner_n = _mm_tiles(m, n, kdim, a.dtype.itemsize, b.dtype.itemsize, so)
    dn = (((0 if ta else 1,), (1 if tb else 0,)), ((), ()))

    def body(*refs):
        if add is None:
            a_ref, b_ref, o_ref = refs
        else:
            a_ref, b_ref, add_ref, o_ref = refs
        r = lax.dot_general(a_ref[...].astype(BF16), b_ref[...].astype(BF16), dn, preferred_element_type=F32)
        if add is not None:
            r = r + add_ref[...].astype(F32)
        o_ref[...] = r.astype(out_dtype)

    ij = (lambda g0, g1: (g0, g1)) if inner_n else (lambda g0, g1: (g1, g0))
    a_spec = (pl.BlockSpec((kdim, tm), lambda g0, g1: (0, ij(g0, g1)[0])) if ta
              else pl.BlockSpec((tm, kdim), lambda g0, g1: (ij(g0, g1)[0], 0)))
    b_spec = (pl.BlockSpec((tn, kdim), lambda g0, g1: (ij(g0, g1)[1], 0)) if tb
              else pl.BlockSpec((kdim, tn), lambda g0, g1: (0, ij(g0, g1)[1])))
    o_spec = pl.BlockSpec((tm, tn), lambda g0, g1: ij(g0, g1))
    ins, specs = [a, b], [a_spec, b_spec]
    if add is not None:
        ins.append(add)
        specs.append(o_spec)
    grid = (m // tm, n // tn) if inner_n else (n // tn, m // tm)
    return pl.pallas_call(
        body, name=name, grid=grid, in_specs=specs, out_specs=o_spec,
        out_shape=jax.ShapeDtypeStruct((m, n), out_dtype),
        compiler_params=_cparams("parallel", "parallel"))(*ins)


def mm_nt_segments(segs, b, *, name):
    m = segs[0].shape[0]
    n, kdim = b.shape
    widths = [sg.shape[1] for sg in segs]
    offs = [sum(widths[:i]) for i in range(len(segs))]
    assert sum(widths) == kdim and all(o % wd == 0 for o, wd in zip(offs, widths))
    sa = max(sg.dtype.itemsize for sg in segs)
    tm, tn, inner_n = _mm_tiles(m, n, kdim, sa, b.dtype.itemsize, 4)
    ns = len(segs)

    def body(*refs):
        o_ref = refs[-1]
        r = None
        for a_ref, b_ref in zip(refs[:ns], refs[ns:2 * ns]):
            d = lax.dot_general(a_ref[...].astype(BF16), b_ref[...].astype(BF16), NT, preferred_element_type=F32)
            r = d if r is None else r + d
        o_ref[...] = r

    ij = (lambda g0, g1: (g0, g1)) if inner_n else (lambda g0, g1: (g1, g0))
    a_specs = [pl.BlockSpec((tm, wd), lambda g0, g1: (ij(g0, g1)[0], 0)) for wd in widths]
    b_specs = [pl.BlockSpec((tn, wd), lambda g0, g1, blk=o // wd: (ij(g0, g1)[1], blk)) for o, wd in zip(offs, widths)]
    grid = (m // tm, n // tn) if inner_n else (n // tn, m // tm)
    return pl.pallas_call(
        body, name=name, grid=grid, in_specs=a_specs + b_specs,
        out_specs=pl.BlockSpec((tm, tn), lambda g0, g1: ij(g0, g1)),
        out_shape=jax.ShapeDtypeStruct((m, n), F32),
        compiler_params=_cparams("parallel", "parallel"))(*segs, *([b] * ns))


def rms_fwd(x, g, *, col=0, out_dtype=BF16, name):
    s = x.shape[0]
    d = g.shape[1]
    tm = _rows(s)

    def body(x_ref, g_ref, o_ref):
        xv = x_ref[...]
        r = lax.rsqrt(jnp.mean(xv * xv, axis=-1, keepdims=True) + EPS)
        o_ref[...] = (xv * r * g_ref[...]).astype(out_dtype)

    return pl.pallas_call(
        body, name=name, grid=(s // tm,),
        in_specs=[pl.BlockSpec((tm, d), lambda i: (i, col)), pl.BlockSpec((1, d), lambda i: (0, 0))],
        out_specs=pl.BlockSpec((tm, d), lambda i: (i, 0)),
        out_shape=jax.ShapeDtypeStruct((s, d), out_dtype),
        compiler_params=_cparams("parallel"))(x, g)


def rms_bwd(x, g, dy, *, col=0, add=None, out_dtype=F32, twin=False, name):
    s = x.shape[0]
    d = g.shape[1]
    tm = _rows(s)

    def body(*refs):
        refs = list(refs)
        dg_ref = refs.pop()
        dxh_ref = refs.pop() if twin else None
        dx_ref = refs.pop()
        add_ref = refs.pop() if add is not None else None
        x_ref, g_ref, dy_ref = refs

        @pl.when(pl.program_id(0) == 0)
        def _():
            dg_ref[...] = jnp.zeros_like(dg_ref)

        xv = x_ref[...]
        dyv = dy_ref[...].astype(F32)
        r = lax.rsqrt(jnp.mean(xv * xv, axis=-1, keepdims=True) + EPS)
        xh = xv * r
        dg_ref[...] += jnp.sum(dyv * xh, axis=0, keepdims=True)
        dxh = dyv * g_ref[...]
        dx = r * (dxh - xh * jnp.mean(dxh * xh, axis=-1, keepdims=True))
        if add is not None:
            dx = dx + add_ref[...]
        dx_ref[...] = dx.astype(out_dtype)
        if twin:
            dxh_ref[...] = dx.astype(BF16)

    row = pl.BlockSpec((tm, d), lambda i: (i, 0))
    vec = pl.BlockSpec((1, d), lambda i: (0, 0))
    ins = [x, g, dy]
    specs = [pl.BlockSpec((tm, d), lambda i: (i, col)), vec, row]
    if add is not None:
        ins.append(add)
        specs.append(row)
    dxs = [jax.ShapeDtypeStruct((s, d), out_dtype)] + ([jax.ShapeDtypeStruct((s, d), BF16)] if twin else [])
    return pl.pallas_call(
        body, name=name, grid=(s // tm,), in_specs=specs,
        out_specs=[row] * len(dxs) + [vec],
        out_shape=dxs + [jax.ShapeDtypeStruct((1, d), F32)],
        compiler_params=_cparams("arbitrary"))(*ins)


def _rope_tables(s):
    inv = 1.0 / (ROPE_THETA ** (jnp.arange(0, ROPE, 2, dtype=F32) / ROPE))
    ang = jnp.arange(s, dtype=F32)[:, None] * inv[None, :]
    cos, sin = jnp.cos(ang), jnp.sin(ang)
    z = lambda w: jnp.zeros((s, w), F32)
    c = jnp.concatenate([jnp.ones((s, NOPE), F32), cos, cos, z(HP - QK)], axis=1)
    s1 = jnp.concatenate([z(NOPE), -sin, z(HP - NOPE - ROPE // 2)], axis=1)
    s2 = jnp.concatenate([z(NOPE + ROPE // 2), sin, z(HP - QK)], axis=1)
    return c, s1, s2


def qkprep_fwd(qraw, kv, proj, qg, kg, tabs, *, name):
    s = qraw.shape[0]
    tm = _long_rows(s)
    kr_col = (proj.shape[1] - HP) // HP

    def body(q_ref, k_ref, v_ref, kr_ref, qg_ref, kg_ref, c_ref, s1_ref, s2_ref, qo_ref, ko_ref, vo_ref):
        c, s1, s2 = c_ref[...], s1_ref[...], s2_ref[...]

        def f(xv, gain):
            r = lax.rsqrt(jnp.sum(xv * xv, axis=-1, keepdims=True) * (1.0 / QK) + EPS)
            xn = xv * r * gain
            return xn * c + pltpu.roll(xn, HP - ROPE // 2, 1) * s1 + pltpu.roll(xn, ROPE // 2, 1) * s2

        qo_ref[...] = (f(q_ref[...], qg_ref[...]) * Q_SCALE).astype(BF16)
        ko_ref[...] = f(k_ref[...] + kr_ref[...], kg_ref[...]).astype(BF16)
        vv = v_ref[...]
        lane = lax.broadcasted_iota(jnp.int32, vv.shape, 1)
        vo_ref[...] = jnp.where(lane == VD, 1.0, vv).astype(BF16)

    head = pl.BlockSpec((tm, HP), lambda i, h: (i, h))
    tab = pl.BlockSpec((tm, HP), lambda i, h: (i, 0))
    gain = pl.BlockSpec((1, HP), lambda i, h: (0, 0))
    return pl.pallas_call(
        body, name=name, grid=(s // tm, HEADS),
        in_specs=[head, head, pl.BlockSpec((tm, HP), lambda i, h: (i, HEADS + h)),
                  pl.BlockSpec((tm, HP), lambda i, h: (i, kr_col)), gain, gain, tab, tab, tab],
        out_specs=[head, head, head],
        out_shape=[jax.ShapeDtypeStruct((s, HEADS * HP), BF16)] * 3,
        compiler_params=_cparams("parallel", "parallel"))(qraw, kv, kv, proj, qg, kg, *tabs)


def qkprep_bwd(qraw, kv, proj, qg, kg, tabs, dq, dk, *, name):
    s = qraw.shape[0]
    tm = _long_rows(s)
    kr_col = (proj.shape[1] - HP) // HP

    def body(q_ref, k_ref, kr_ref, qg_ref, kg_ref, c_ref, s1_ref, s2_ref, dq_ref, dk_ref,
             dqr_ref, dkr_ref, dkrope_ref, dqg_ref, dkg_ref):
        i, h = pl.program_id(0), pl.program_id(1)
        c, s1, s2 = c_ref[...], s1_ref[...], s2_ref[...]

        @pl.when((i == 0) & (h == 0))
        def _():
            dqg_ref[...] = jnp.zeros_like(dqg_ref)
            dkg_ref[...] = jnp.zeros_like(dkg_ref)

        @pl.when(h == 0)
        def _():
            dkrope_ref[...] = jnp.zeros_like(dkrope_ref)

        def f(xv, gain, dout):
            r = lax.rsqrt(jnp.sum(xv * xv, axis=-1, keepdims=True) * (1.0 / QK) + EPS)
            xh = xv * r
            dxn = dout * c + pltpu.roll(dout * s1, ROPE // 2, 1) + pltpu.roll(dout * s2, HP - ROPE // 2, 1)
            dgain = jnp.sum(dxn * xh, axis=0, keepdims=True)
            dxh = dxn * gain
            dx = r * (dxh - xh * (jnp.sum(dxh * xh, axis=-1, keepdims=True) * (1.0 / QK)))
            return dx, dgain

        dxq, dgq = f(q_ref[...], qg_ref[...], dq_ref[...])
        dxk, dgk = f(k_ref[...] + kr_ref[...], kg_ref[...], dk_ref[...])
        dqr_ref[...] = dxq.astype(BF16)
        dkr_ref[...] = dxk.astype(BF16)
        dqg_ref[...] += dgq
        dkg_ref[...] += dgk
        lane = lax.broadcasted_iota(jnp.int32, dxk.shape, 1)
        dkrope_ref[...] += jnp.where((lane >= NOPE) & (lane < QK), dxk, 0.0)

    head = pl.BlockSpec((tm, HP), lambda i, h: (i, h))
    tab = pl.BlockSpec((tm, HP), lambda i, h: (i, 0))
    gain = pl.BlockSpec((1, HP), lambda i, h: (0, 0))
    return pl.pallas_call(
        body, name=name, grid=(s // tm, HEADS),
        in_specs=[head, head, pl.BlockSpec((tm, HP), lambda i, h: (i, kr_col)), gain, gain, tab, tab, tab, head, head],
        out_specs=[head, head, tab, gain, gain],
        out_shape=[jax.ShapeDtypeStruct((s, HEADS * HP), BF16)] * 2
        + [jax.ShapeDtypeStruct((s, HP), F32), jax.ShapeDtypeStruct((1, HP), F32), jax.ShapeDtypeStruct((1, HP), F32)],
        compiler_params=_cparams("arbitrary", "arbitrary"))(qraw, kv, proj, qg, kg, *tabs, dq, dk)


ATT_SCALE = QK ** -0.5
LOG2E = math.log2(math.e)
Q_SCALE = ATT_SCALE * LOG2E
NEG = -1e30


def _att_tile(s):
    return _pick(s, (512, 256, 128))


def _causal(sv, diag):
    r = lax.broadcasted_iota(jnp.int32, sv.shape, 0)
    c = lax.broadcasted_iota(jnp.int32, sv.shape, 1)
    return jnp.where(diag & (c > r), NEG, sv)


NT = (((1,), (1,)), ((), ()))
TN = (((0,), (0,)), ((), ()))


def _row_of(col):
    return jnp.broadcast_to(col, (col.shape[0], LANE)).T[0:SLAB, :]


def _att_specs(s):
    t = _att_tile(s)
    nb = s // t
    tile = pl.BlockSpec((t, HP), lambda h, i: (i, h))
    whole = pl.BlockSpec((s, HP), lambda h, i: (0, h))
    row = pl.BlockSpec((1, 1, SLAB, t), lambda h, i: (h, i, 0, 0))
    rows = pl.BlockSpec((1, nb, SLAB, t), lambda h, i: (h, 0, 0, 0))
    return t, nb, tile, whole, row, rows


def attn_fwd(q, k, v, *, name):
    s = q.shape[0]
    t, nb, tile, whole, row, _ = _att_specs(s)

    def body(q_ref, k_ref, v_ref, o_ref, oh_ref, lset_ref, s_scr, mb_scr, acc):
        qb = pl.program_id(1)
        qv = q_ref[...]

        def scores(j):
            r0 = pl.multiple_of(j * t, t)
            return lax.dot_general(qv, k_ref[pl.ds(r0, t), :], NT, preferred_element_type=F32)

        def fold(sv):
            m = sv[:, 0:LANE]
            for kk in range(1, t // LANE):
                m = jnp.maximum(m, sv[:, kk * LANE:(kk + 1) * LANE])
            return m

        def first(j, m):
            sv = scores(j)
            s_scr[j] = sv
            return jnp.maximum(m, fold(sv))

        m = lax.fori_loop(0, qb, first, jnp.full((t, LANE), NEG, F32))
        sd = _causal(scores(qb), True)
        s_scr[qb] = sd
        mcol = jnp.max(jnp.maximum(m, fold(sd)), axis=-1, keepdims=True)
        mb_scr[...] = jnp.broadcast_to(mcol, (t, t))
        acc[...] = jnp.zeros_like(acc)

        def second(j, carry):
            r0 = pl.multiple_of(j * t, t)
            p = jnp.exp2(s_scr[j] - mb_scr[...]).astype(BF16)
            acc[...] += jnp.dot(p, v_ref[pl.ds(r0, t), :], preferred_element_type=F32)
            return carry

        lax.fori_loop(0, qb + 1, second, 0)
        av = acc[...]
        lsum = av[:, VD:VD + 1]
        lane = lax.broadcasted_iota(jnp.int32, av.shape, 1)
        ov = jnp.where(lane == VD, 0.0, av / lsum)
        o_ref[...] = ov
        oh_ref[...] = ov.astype(BF16)
        lset_ref[0, 0] = _row_of(mcol + jnp.log2(lsum))

    return pl.pallas_call(
        body, name=name, grid=(HEADS, nb), in_specs=[tile, whole, whole], out_specs=[tile, tile, row],
        out_shape=[jax.ShapeDtypeStruct((s, HEADS * HP), F32), jax.ShapeDtypeStruct((s, HEADS * HP), BF16),
                   jax.ShapeDtypeStruct((HEADS, nb, SLAB, t), F32)],
        scratch_shapes=[pltpu.VMEM((nb, t, t), F32), pltpu.VMEM((t, t), F32), pltpu.VMEM((t, HP), F32)],
        compiler_params=_cparams("parallel", "parallel"))(q, k, v)


def attn_delta(o, do, *, name):
    s = o.shape[0]
    t, nb, tile, _, row, _ = _att_specs(s)

    def body(o_ref, do_ref, doh_ref, dt_ref):
        dov = do_ref[...]
        doh_ref[...] = dov.astype(BF16)
        dt_ref[0, 0] = _row_of(jnp.sum(dov * o_ref[...], axis=-1, keepdims=True))

    return pl.pallas_call(
        body, name=name, grid=(HEADS, nb), in_specs=[tile, tile], out_specs=[tile, row],
        out_shape=[jax.ShapeDtypeStruct((s, HEADS * HP), BF16), jax.ShapeDtypeStruct((HEADS, nb, SLAB, t), F32)],
        compiler_params=_cparams("parallel", "parallel"))(o, do)


def attn_bwd(q, k, v, doh, lset, deltat, *, name):
    s = q.shape[0]
    t, nb, tile, whole, _, rows = _att_specs(s)

    def body(q_ref, k_ref, v_ref, do_ref, lt_ref, dt_ref, dq_ref, dk_ref, dv_ref, dk_acc, dv_acc):
        kb = pl.program_id(1)
        kt, vt = k_ref[...], v_ref[...]
        dk_acc[...] = jnp.zeros_like(dk_acc)
        dv_acc[...] = jnp.zeros_like(dv_acc)

        @pl.when(kb == 0)
        def _():
            dq_ref[...] = jnp.zeros_like(dq_ref)

        def step(i, diag):
            r0 = pl.multiple_of(i * t, t)
            qi, doi = q_ref[pl.ds(r0, t), :], do_ref[pl.ds(r0, t), :]
            st = lax.dot_general(kt, qi, NT, preferred_element_type=F32)
            if diag:
                kr = lax.broadcasted_iota(jnp.int32, st.shape, 0)
                qc = lax.broadcasted_iota(jnp.int32, st.shape, 1)
                st = jnp.where(kr > qc, NEG, st)
            pt = jnp.exp2(st - lt_ref[0, i][0:1, :])
            dpt = lax.dot_general(vt, doi, NT, preferred_element_type=F32)
            dst = (pt * (dpt - dt_ref[0, i][0:1, :])).astype(BF16)
            dv_acc[...] += jnp.dot(pt.astype(BF16), doi, preferred_element_type=F32)
            dk_acc[...] += jnp.dot(dst, qi, preferred_element_type=F32)
            dq_ref[pl.ds(r0, t), :] += lax.dot_general(dst, kt, TN, preferred_element_type=F32) * ATT_SCALE

        def off_diag(i, carry):
            step(i, False)
            return carry

        step(kb, True)
        lax.fori_loop(kb + 1, nb, off_diag, 0)
        dk_ref[...] = dk_acc[...] * (1.0 / LOG2E)
        dvv = dv_acc[...]
        lane = lax.broadcasted_iota(jnp.int32, dvv.shape, 1)
        dv_ref[...] = jnp.where(lane == VD, 0.0, dvv).astype(BF16)

    return pl.pallas_call(
        body, name=name, grid=(HEADS, nb), in_specs=[whole, tile, tile, whole, rows, rows],
        out_specs=[whole, tile, tile],
        out_shape=[jax.ShapeDtypeStruct((s, HEADS * HP), F32), jax.ShapeDtypeStruct((s, HEADS * HP), F32),
                   jax.ShapeDtypeStruct((s, HEADS * HP), BF16)],
        scratch_shapes=[pltpu.VMEM((t, HP), F32), pltpu.VMEM((t, HP), F32)],
        compiler_params=_cparams("parallel", "arbitrary"))(q, k, v, doh, lset, deltat)


HALO = 8
CW = 256


def _shifts(zw):
    return zw, pltpu.roll(zw, 1, 0), pltpu.roll(zw, 2, 0)


def _conv3(sh, w):
    return w[2:3] * sh[0] + w[1:2] * sh[1] + w[0:1] * sh[2]


def _conv3_t(dc, w):
    n = dc.shape[0]
    return w[2:3] * dc + w[1:2] * pltpu.roll(dc, n - 1, 0) + w[0:1] * pltpu.roll(dc, n - 2, 0)


def _conv3_dw(dc, sh, r):
    return [jnp.sum(dc * z[HALO:HALO + r], axis=0, keepdims=True) for z in (sh[2], sh[1], sh[0])]


def _halo_specs(r, colfn):
    rb = r // HALO
    cur = pl.BlockSpec((r, CW), lambda j, i: (i, colfn(j)))
    prev = pl.BlockSpec((HALO, CW), lambda j, i: (jnp.maximum(i * rb - 1, 0), colfn(j)))

    def nxt(nrow_blocks):
        return pl.BlockSpec((HALO, CW), lambda j, i: (jnp.minimum((i + 1) * rb, nrow_blocks * rb - 1), colfn(j)))

    return cur, prev, nxt


H_HALO = 16


def ffn_up_act(h, w_up, w, *, name):
    s, kdim = h.shape
    hh = w_up.shape[1] // 2
    nj = hh // CW
    r = _long_rows(s)
    nt = s // r
    rb = r // H_HALO

    def body(h_ref, hp_ref, bg_ref, bv_ref, wg_ref, wv_ref, ug_ref, uv_ref, o_ref):
        pm = (pl.program_id(1) > 0).astype(F32)
        hv, hp = h_ref[...], hp_ref[...]

        def half(b_ref, w_ref, u_ref):
            up = jnp.dot(hv, b_ref[...], preferred_element_type=F32)
            u_ref[...] = up
            prev = jnp.dot(hp, b_ref[...], preferred_element_type=F32) * pm
            return _conv3(_shifts(jnp.concatenate([prev, up], axis=0)), w_ref[...])[H_HALO:]

        cg = half(bg_ref, wg_ref, ug_ref)
        cv = half(bv_ref, wv_ref, uv_ref)
        o_ref[...] = (cg * jax.nn.sigmoid(cg) * cv).astype(BF16)

    tile = pl.BlockSpec((r, CW), lambda j, i: (i, j))
    return pl.pallas_call(
        body, name=name, grid=(nj, nt),
        in_specs=[pl.BlockSpec((r, kdim), lambda j, i: (i, 0)),
                  pl.BlockSpec((H_HALO, kdim), lambda j, i: (jnp.maximum(i * rb - 1, 0), 0)),
                  pl.BlockSpec((kdim, CW), lambda j, i: (0, j)), pl.BlockSpec((kdim, CW), lambda j, i: (0, nj + j)),
                  pl.BlockSpec((3, CW), lambda j, i: (0, j)), pl.BlockSpec((3, CW), lambda j, i: (0, nj + j))],
        out_specs=[tile, tile, tile],
        out_shape=[jax.ShapeDtypeStruct((s, hh), F32), jax.ShapeDtypeStruct((s, hh), F32),
                   jax.ShapeDtypeStruct((s, hh), BF16)],
        compiler_params=_cparams("parallel", "parallel"))(h, h, w_up, w_up, w, w)


def ffnact_bwd(up, w, douth, w_down, *, name):
    upg, upv = up
    s, hh = upg.shape
    nj = hh // CW
    r = _long_rows(s)
    nt = s // r

    def body(g_ref, gp_ref, gn_ref, v_ref, vp_ref, vn_ref, wg_ref, wv_ref, do_ref, don_ref, wd_ref,
             dg_ref, dv_ref, dwg_ref, dwv_ref):
        i = pl.program_id(1)
        pm = (i > 0).astype(F32)
        nm = (i < nt - 1).astype(F32)

        @pl.when(i == 0)
        def _():
            dwg_ref[...] = jnp.zeros_like(dwg_ref)
            dwv_ref[...] = jnp.zeros_like(dwv_ref)

        wg, wv = wg_ref[...], wv_ref[...]
        zg = jnp.concatenate([gp_ref[...] * pm, g_ref[...], gn_ref[...]], axis=0)
        zv = jnp.concatenate([vp_ref[...] * pm, v_ref[...], vn_ref[...]], axis=0)
        zg, zv = _shifts(zg), _shifts(zv)
        cg = _conv3(zg, wg)[HALO:]
        cv = _conv3(zv, wv)[HALO:]
        wd = wd_ref[...]
        da = jnp.concatenate(
            [lax.dot_general(do_ref[...], wd, NT, preferred_element_type=F32),
             lax.dot_general(don_ref[...], wd, NT, preferred_element_type=F32)[:HALO] * nm], axis=0)
        sg = jax.nn.sigmoid(cg)
        dcg = da * cv * (sg * (1.0 + cg * (1.0 - sg)))
        dcv = da * (cg * sg)
        dg_ref[...] = _conv3_t(dcg, wg)[:r].astype(BF16)
        dv_ref[...] = _conv3_t(dcv, wv)[:r].astype(BF16)
        for kk, (a, b) in enumerate(zip(_conv3_dw(dcg[:r], zg, r), _conv3_dw(dcv[:r], zv, r))):
            dwg_ref[kk:kk + 1, :] += a
            dwv_ref[kk:kk + 1, :] += b

    gcur, gprev, gnext = _halo_specs(r, lambda j: j)
    vcur, vprev, vnext = gcur, gprev, gnext
    acur, _, _ = _halo_specs(r, lambda j: j)
    wg = pl.BlockSpec((3, CW), lambda j, i: (0, j))
    wv = pl.BlockSpec((3, CW), lambda j, i: (0, nj + j))
    dm = douth.shape[1]
    nb16 = r // H_HALO
    docur = pl.BlockSpec((r, dm), lambda j, i: (i, 0))
    donext = pl.BlockSpec((H_HALO, dm), lambda j, i: (jnp.minimum((i + 1) * nb16, nt * nb16 - 1), 0))
    wds = pl.BlockSpec((CW, dm), lambda j, i: (j, 0))
    dupg, dupv, dwg, dwv = pl.pallas_call(
        body, name=name, grid=(nj, nt),
        in_specs=[gcur, gprev, gnext(nt), vcur, vprev, vnext(nt), wg, wv, docur, donext, wds],
        out_specs=[acur, acur, wg, wg],
        out_shape=[jax.ShapeDtypeStruct((s, hh), BF16), jax.ShapeDtypeStruct((s, hh), BF16),
                   jax.ShapeDtypeStruct((3, hh), F32), jax.ShapeDtypeStruct((3, hh), F32)],
        compiler_params=_cparams("parallel", "arbitrary"))(upg, upg, upg, upv, upv, upv, w, w, douth, douth, w_down)
    return (dupg, dupv), jnp.concatenate([dwg, dwv], axis=1)


def sconv_fwd(proj, w, *, name):
    s = proj.shape[0]
    nj = CONVC // CW
    r = _long_rows(s)
    nt = s // r

    def body(b_ref, c_ref, cp_ref, x_ref, xp_ref, w_ref, o_ref):
        pm = (pl.program_id(1) > 0).astype(F32)
        zw = jnp.concatenate([cp_ref[...] * xp_ref[...] * pm, c_ref[...] * x_ref[...]], axis=0)
        o_ref[...] = (b_ref[...] * _conv3(_shifts(zw), w_ref[...])[HALO:]).astype(BF16)

    bcur, _, _ = _halo_specs(r, lambda j: (QR + KVR) // CW + j)
    ccur, cprev, _ = _halo_specs(r, lambda j: (QR + KVR + CONVC) // CW + j)
    xcur, xprev, _ = _halo_specs(r, lambda j: (QR + KVR + 2 * CONVC) // CW + j)
    ws = pl.BlockSpec((3, CW), lambda j, i: (0, j))
    return pl.pallas_call(
        body, name=name, grid=(nj, nt), in_specs=[bcur, ccur, cprev, xcur, xprev, ws],
        out_specs=pl.BlockSpec((r, CW), lambda j, i: (i, j)),
        out_shape=jax.ShapeDtypeStruct((s, CONVC), BF16),
        compiler_params=_cparams("parallel", "parallel"))(proj, proj, proj, proj, proj, w)


def sconv_bwd(proj, w, dy, *, name):
    s = proj.shape[0]
    nj = CONVC // CW
    r = _rows(s)
    nt = s // r

    def body(b_ref, bn_ref, c_ref, cp_ref, x_ref, xp_ref, w_ref, dy_ref, dyn_ref, db_ref, dc_ref, dx_ref, dw_ref):
        i = pl.program_id(1)
        pm = (i > 0).astype(F32)
        nm = (i < nt - 1).astype(F32)

        @pl.when(i == 0)
        def _():
            dw_ref[...] = jnp.zeros_like(dw_ref)

        wv = w_ref[...]
        zw = jnp.concatenate([cp_ref[...] * xp_ref[...] * pm, c_ref[...] * x_ref[...]], axis=0)
        zw = _shifts(zw)
        conv = _conv3(zw, wv)[HALO:]
        dyv = dy_ref[...]
        db_ref[...] = (dyv * conv).astype(BF16)
        dconv = jnp.concatenate([dyv * b_ref[...], dyn_ref[...] * bn_ref[...] * nm], axis=0)
        dz = _conv3_t(dconv, wv)[:r]
        dc_ref[...] = (dz * x_ref[...]).astype(BF16)
        dx_ref[...] = (dz * c_ref[...]).astype(BF16)
        for kk, a in enumerate(_conv3_dw(dconv[:r], zw, r)):
            dw_ref[kk:kk + 1, :] += a

    bcur, _, bnext = _halo_specs(r, lambda j: (QR + KVR) // CW + j)
    ccur, cprev, _ = _halo_specs(r, lambda j: (QR + KVR + CONVC) // CW + j)
    xcur, xprev, _ = _halo_specs(r, lambda j: (QR + KVR + 2 * CONVC) // CW + j)
    ycur, _, ynext = _halo_specs(r, lambda j: j)
    ws = pl.BlockSpec((3, CW), lambda j, i: (0, j))
    out = pl.BlockSpec((r, CW), lambda j, i: (i, j))
    db, dc, dx, dw = pl.pallas_call(
        body, name=name, grid=(nj, nt),
        in_specs=[bcur, bnext(nt), ccur, cprev, xcur, xprev, ws, ycur, ynext(nt)],
        out_specs=[out, out, out, ws],
        out_shape=[jax.ShapeDtypeStruct((s, CONVC), BF16)] * 3 + [jax.ShapeDtypeStruct((3, CONVC), F32)],
        compiler_params=_cparams("parallel", "arbitrary"))(proj, proj, proj, proj, proj, proj, w, dy, dy)
    return (db, dc, dx), dw


SW = 512
NJ = NST // SW


def _scan_tables(ar, ai):
    def cmul(x, y):
        return x[0] * y[0] - x[1] * y[1], x[0] * y[1] + x[1] * y[0]

    def build(a, reverse):
        pw = [a]
        for _ in range(SLAB - 1):
            pw.append(cmul(pw[-1], a))
        row = jnp.arange(SLAB)[:, None]
        tabs = []
        for kk in (1, 2, 4):
            mask = ((row < SLAB - kk) if reverse else (row >= kk)).astype(F32)
            tabs += [mask * pw[kk - 1][0][None, :], mask * pw[kk - 1][1][None, :]]
        order = list(range(SLAB - 1, -1, -1)) if reverse else list(range(SLAB))
        tabs += [jnp.stack([pw[o][0] for o in order]), jnp.stack([pw[o][1] for o in order])]
        return jnp.stack(tabs)

    return build((ar, ai), False), build((ar, -ai), True)


def _slab_scan(xr, xi, tabs, cr, ci, reverse):
    for n, kk in enumerate((1, 2, 4)):
        sh = SLAB - kk if reverse else kk
        tr, ti = tabs[2 * n], tabs[2 * n + 1]
        sr, si = pltpu.roll(xr, sh, 0), pltpu.roll(xi, sh, 0)
        xr, xi = xr + tr * sr - ti * si, xi + tr * si + ti * sr
    tr, ti = tabs[6], tabs[7]
    return xr + tr * cr - ti * ci, xi + tr * ci + ti * cr


def s5_fwd(u, bbd_r, bbd_i, cbd_r, cbd_i, tab, *, name):
    s = u.shape[0]
    tbk = _long_rows(s)
    nt = s // tbk
    nsl = tbk // SLAB

    def body(u_ref, br_ref, bi_ref, cr_ref, ci_ref, tab_ref, y_ref, sr_ref, si_ref, bur, bui, carry):
        @pl.when(pl.program_id(1) == 0)
        def _():
            carry[...] = jnp.zeros_like(carry)

        ub = u_ref[...].astype(BF16)
        bur[...] = jnp.dot(ub, br_ref[0], preferred_element_type=F32)
        bui[...] = jnp.dot(ub, bi_ref[0], preferred_element_type=F32)
        tabs = [tab_ref[n] for n in range(8)]

        def slab(n, c):
            r0 = pl.multiple_of(n * SLAB, SLAB)
            sr, si = _slab_scan(bur[pl.ds(r0, SLAB), :], bui[pl.ds(r0, SLAB), :], tabs, c[0], c[1], False)
            sr_ref[pl.ds(r0, SLAB), :] = sr
            si_ref[pl.ds(r0, SLAB), :] = si
            return (jnp.broadcast_to(sr[SLAB - 1:SLAB], sr.shape), jnp.broadcast_to(si[SLAB - 1:SLAB], si.shape))

        cr, ci = lax.fori_loop(0, nsl, slab, (carry[0], carry[1]))
        carry[0] = cr
        carry[1] = ci
        y_ref[...] = (jnp.dot(sr_ref[...].astype(BF16), cr_ref[0], preferred_element_type=F32)
                      - jnp.dot(si_ref[...].astype(BF16), ci_ref[0], preferred_element_type=F32))

    us = pl.BlockSpec((tbk, LANE), lambda j, t: (t, j))
    bs = pl.BlockSpec((1, LANE, SW), lambda j, t: (j, 0, 0))
    cs = pl.BlockSpec((1, SW, LANE), lambda j, t: (j, 0, 0))
    ts = pl.BlockSpec((8, SLAB, SW), lambda j, t: (0, 0, j))
    ss = pl.BlockSpec((tbk, SW), lambda j, t: (t, j))
    return pl.pallas_call(
        body, name=name, grid=(NJ, nt), in_specs=[us, bs, bs, cs, cs, ts], out_specs=[us, ss, ss],
        out_shape=[jax.ShapeDtypeStruct((s, D_MODEL), F32), jax.ShapeDtypeStruct((s, NST), F32),
                   jax.ShapeDtypeStruct((s, NST), F32)],
        scratch_shapes=[pltpu.VMEM((tbk, SW), F32), pltpu.VMEM((tbk, SW), F32), pltpu.VMEM((2, SLAB, SW), F32)],
        compiler_params=_cparams("parallel", "arbitrary"))(u, bbd_r, bbd_i, cbd_r, cbd_i, tab)


def s5_bwd(u, dy, dskip, st_r, st_i, bbd_r, bbd_i, cbd_r, cbd_i, tabrev, *, name):
    s = u.shape[0]
    tbk = _long_rows(s)
    nt = s // tbk
    nsl = tbk // SLAB
    rbk = tbk // SLAB

    def body(u_ref, dy_ref, d_ref, sr_ref, si_ref, pr_ref, pi_ref, br_ref, bi_ref, cr_ref, ci_ref, tab_ref,
             du_ref, dbr_ref, dbi_ref, dcr_ref, dci_ref, da_ref, lam_r, lam_i, carry):
        t = pl.program_id(1)

        @pl.when(t == 0)
        def _():
            carry[...] = jnp.zeros_like(carry)
            dbr_ref[...] = jnp.zeros_like(dbr_ref)
            dbi_ref[...] = jnp.zeros_like(dbi_ref)
            dcr_ref[...] = jnp.zeros_like(dcr_ref)
            dci_ref[...] = jnp.zeros_like(dci_ref)
            da_ref[...] = jnp.zeros_like(da_ref)

        dyv = dy_ref[...]
        dyh = dyv.astype(BF16)
        lam_r[...] = lax.dot_general(dyh, cr_ref[0], NT, preferred_element_type=F32)
        lam_i[...] = -lax.dot_general(dyh, ci_ref[0], NT, preferred_element_type=F32)
        tabs = [tab_ref[n] for n in range(8)]

        def slab(n, c):
            r0 = pl.multiple_of((nsl - 1 - n) * SLAB, SLAB)
            lr, li = _slab_scan(lam_r[pl.ds(r0, SLAB), :], lam_i[pl.ds(r0, SLAB), :], tabs, c[0], c[1], True)
            lam_r[pl.ds(r0, SLAB), :] = lr
            lam_i[pl.ds(r0, SLAB), :] = li
            return (jnp.broadcast_to(lr[0:1], lr.shape), jnp.broadcast_to(li[0:1], li.shape))

        cr, ci = lax.fori_loop(0, nsl, slab, (carry[0], carry[1]))
        carry[0] = cr
        carry[1] = ci
        lr, li = lam_r[...], lam_i[...]
        lrh, lih = lr.astype(BF16), li.astype(BF16)
        du = (dyv * d_ref[...] + lax.dot_general(lrh, br_ref[0], NT, preferred_element_type=F32)
              + lax.dot_general(lih, bi_ref[0], NT, preferred_element_type=F32))
        du_ref[...] = du.astype(BF16)
        ub = u_ref[...].astype(BF16)
        dbr_ref[0] += lax.dot_general(ub, lrh, TN, preferred_element_type=F32)
        dbi_ref[0] += lax.dot_general(ub, lih, TN, preferred_element_type=F32)
        srv, siv = sr_ref[...], si_ref[...]
        dcr_ref[0] += lax.dot_general(srv.astype(BF16), dyh, TN, preferred_element_type=F32)
        dci_ref[0] -= lax.dot_general(siv.astype(BF16), dyh, TN, preferred_element_type=F32)
        first = lax.broadcasted_iota(jnp.int32, srv.shape, 0) == 0
        pm = (t < nt - 1).astype(F32)
        spr = jnp.where(first, pr_ref[SLAB - 1:SLAB, :] * pm, pltpu.roll(srv, 1, 0))
        spi = jnp.where(first, pi_ref[SLAB - 1:SLAB, :] * pm, pltpu.roll(siv, 1, 0))
        da_ref[0:1, :] += jnp.sum(lr * spr + li * spi, axis=0, keepdims=True)
        da_ref[1:2, :] += jnp.sum(li * spr - lr * spi, axis=0, keepdims=True)

    rv = lambda t: nt - 1 - t
    us = pl.BlockSpec((tbk, LANE), lambda j, t: (rv(t), j))
    ds = pl.BlockSpec((1, LANE), lambda j, t: (0, j))
    ss = pl.BlockSpec((tbk, SW), lambda j, t: (rv(t), j))
    ps = pl.BlockSpec((SLAB, SW), lambda j, t: (jnp.maximum(rv(t) * rbk - 1, 0), j))
    bs = pl.BlockSpec((1, LANE, SW), lambda j, t: (j, 0, 0))
    cs = pl.BlockSpec((1, SW, LANE), lambda j, t: (j, 0, 0))
    ts = pl.BlockSpec((8, SLAB, SW), lambda j, t: (0, 0, j))
    das = pl.BlockSpec((2, SW), lambda j, t: (0, j))
    return pl.pallas_call(
        body, name=name, grid=(NJ, nt),
        in_specs=[us, us, ds, ss, ss, ps, ps, bs, bs, cs, cs, ts],
        out_specs=[us, bs, bs, cs, cs, das],
        out_shape=[jax.ShapeDtypeStruct((s, D_MODEL), BF16),
                   jax.ShapeDtypeStruct((NJ, LANE, SW), F32), jax.ShapeDtypeStruct((NJ, LANE, SW), F32),
                   jax.ShapeDtypeStruct((NJ, SW, LANE), F32), jax.ShapeDtypeStruct((NJ, SW, LANE), F32),
                   jax.ShapeDtypeStruct((2, NST), F32)],
        scratch_shapes=[pltpu.VMEM((tbk, SW), F32), pltpu.VMEM((tbk, SW), F32), pltpu.VMEM((2, SLAB, SW), F32)],
        compiler_params=_cparams("parallel", "arbitrary"))(
            u, dy, dskip, st_r, st_i, st_r, st_i, bbd_r, bbd_i, cbd_r, cbd_i, tabrev)


GELU_C = math.sqrt(2.0 / math.pi)
GELU_A = 0.044715


def s5post_fwd(y, u, dskip, *, name):
    s = y.shape[0]
    tm = _rows(s)

    def body(y_ref, u_ref, d_ref, o_ref):
        z = y_ref[...] + d_ref[...] * u_ref[...]
        o_ref[...] = (0.5 * z * (1.0 + jnp.tanh(GELU_C * (z + GELU_A * z * z * z)))).astype(BF16)

    row = pl.BlockSpec((tm, D_MODEL), lambda i: (i, 0))
    vec = pl.BlockSpec((1, D_MODEL), lambda i: (0, 0))
    return pl.pallas_call(body, name=name, grid=(s // tm,), in_specs=[row, row, vec], out_specs=row,
                          out_shape=jax.ShapeDtypeStruct((s, D_MODEL), BF16),
                          compiler_params=_cparams("parallel"))(y, u, dskip)


def s5post_bwd(y, u, dskip, dg, *, name):
    s = y.shape[0]
    tm = _rows(s)

    def body(y_ref, u_ref, d_ref, dg_ref, dz_ref, dd_ref):
        @pl.when(pl.program_id(0) == 0)
        def _():
            dd_ref[...] = jnp.zeros_like(dd_ref)

        uv = u_ref[...]
        z = y_ref[...] + d_ref[...] * uv
        th = jnp.tanh(GELU_C * (z + GELU_A * z * z * z))
        dgelu = 0.5 * (1.0 + th) + 0.5 * z * (1.0 - th * th) * (GELU_C * (1.0 + 3.0 * GELU_A * z * z))
        dz = dg_ref[...] * dgelu
        dz_ref[...] = dz
        dd_ref[...] += jnp.sum(dz * uv, axis=0, keepdims=True)

    row = pl.BlockSpec((tm, D_MODEL), lambda i: (i, 0))
    vec = pl.BlockSpec((1, D_MODEL), lambda i: (0, 0))
    return pl.pallas_call(body, name=name, grid=(s // tm,), in_specs=[row, row, vec, row], out_specs=[row, vec],
                          out_shape=[jax.ShapeDtypeStruct((s, D_MODEL), F32), jax.ShapeDtypeStruct((1, D_MODEL), F32)],
                          compiler_params=_cparams("arbitrary"))(y, u, dskip, dg)


def glu_fwd(glu, x, *, name):
    s = x.shape[0]
    tm = _rows(s)

    def body(a_ref, b_ref, x_ref, o_ref):
        o_ref[...] = x_ref[...] + a_ref[...] * jax.nn.sigmoid(b_ref[...])

    row = pl.BlockSpec((tm, D_MODEL), lambda i: (i, 0))
    return pl.pallas_call(body, name=name, grid=(s // tm,),
                          in_specs=[row, pl.BlockSpec((tm, D_MODEL), lambda i: (i, 1)), row], out_specs=row,
                          out_shape=jax.ShapeDtypeStruct((s, D_MODEL), F32),
                          compiler_params=_cparams("parallel"))(glu, glu, x)


def glu_bwd(glu, dx, *, name):
    s = dx.shape[0]
    tm = _rows(s)

    def body(a_ref, b_ref, dx_ref, o_ref):
        sg = jax.nn.sigmoid(b_ref[...])
        dxv = dx_ref[...]
        o_ref[:, :D_MODEL] = (dxv * sg).astype(BF16)
        o_ref[:, D_MODEL:] = (dxv * a_ref[...] * sg * (1.0 - sg)).astype(BF16)

    row = pl.BlockSpec((tm, D_MODEL), lambda i: (i, 0))
    return pl.pallas_call(body, name=name, grid=(s // tm,),
                          in_specs=[row, pl.BlockSpec((tm, D_MODEL), lambda i: (i, 1)), row],
                          out_specs=pl.BlockSpec((tm, 2 * D_MODEL), lambda i: (i, 0)),
                          out_shape=jax.ShapeDtypeStruct((s, 2 * D_MODEL), BF16),
                          compiler_params=_cparams("parallel"))(glu, glu, dx)


def loss_head(y, target, *, name):
    s = y.shape[0]
    tm = _rows(s)

    def body(y_ref, t_ref, dy_ref, dyh_ref, l_ref):
        @pl.when(pl.program_id(0) == 0)
        def _():
            l_ref[...] = jnp.zeros_like(l_ref)

        e = y_ref[...] - t_ref[...]
        dy_ref[...] = e * (1.0 / D_MODEL)
        dyh_ref[...] = (e * (1.0 / D_MODEL)).astype(BF16)
        e2 = jnp.sum((e * e).reshape(tm // 8, 8, D_MODEL), axis=0)
        acc = e2[:, 0:LANE]
        for kk in range(1, D_MODEL // LANE):
            acc = acc + e2[:, kk * LANE:(kk + 1) * LANE]
        l_ref[...] += acc

    row = pl.BlockSpec((tm, D_MODEL), lambda i: (i, 0))
    return pl.pallas_call(body, name=name, grid=(s // tm,), in_specs=[row, row],
                          out_specs=[row, row, pl.BlockSpec((8, LANE), lambda i: (0, 0))],
                          out_shape=[jax.ShapeDtypeStruct((s, D_MODEL), F32), jax.ShapeDtypeStruct((s, D_MODEL), BF16),
                                     jax.ShapeDtypeStruct((8, LANE), F32)],
                          compiler_params=_cparams("arbitrary"))(y, target)


NCHIP = 4


def _mesh_pos():
    return lax.axis_index("x"), lax.axis_index("y"), lax.axis_index("c")


def _chip_exchange(bufs, scatter, name):
    n = len(bufs)
    shapes = [b.shape[1:] if scatter else b.shape for b in bufs]

    def body(*refs):
        ins, outs = refs[:n], refs[n:2 * n]
        send_sems, recv_sems, local_sems = refs[2 * n:]
        x, y, c = _mesh_pos()
        me = 2 * x + y
        peers = [(1 - x, y), (x, 1 - y), (1 - x, 1 - y)]

        def copy(a, j, px, py, dst_slot):
            src = ins[a].at[2 * px + py] if scatter else ins[a]
            return pltpu.make_async_remote_copy(src_ref=src, dst_ref=outs[a].at[dst_slot],
                                                send_sem=send_sems.at[3 * a + j], recv_sem=recv_sems.at[3 * a + j],
                                                device_id=(px, py, c), device_id_type=MESH)

        mine = [pltpu.make_async_copy(ins[a].at[me] if scatter else ins[a], outs[a].at[me], local_sems.at[a])
                for a in range(n)]
        sends = [copy(a, j, px, py, me) for a in range(n) for j, (px, py) in enumerate(peers)]
        for cp in mine + sends:
            cp.start()
        for a in range(n):
            for j, (px, py) in enumerate(peers):
                copy(a, j, px, py, 2 * px + py).wait_recv()
        for cp in sends:
            cp.wait_send()
        for cp in mine:
            cp.wait()

    return pl.pallas_call(
        body, name=name, in_specs=[ANY] * n, out_specs=[ANY] * n,
        out_shape=[jax.ShapeDtypeStruct((NCHIP,) + tuple(shp), b.dtype) for shp, b in zip(shapes, bufs)],
        scratch_shapes=[pltpu.SemaphoreType.DMA((3 * n,)), pltpu.SemaphoreType.DMA((3 * n,)),
                        pltpu.SemaphoreType.DMA((n,))],
    )(*bufs)


HBM_SPEC = pl.BlockSpec(memory_space=pltpu.HBM)
SEM_SPEC = pl.BlockSpec(memory_space=pltpu.SEMAPHORE)
DATAFLOW = pltpu.SideEffectType.DATAFLOW_SIDE_EFFECTING


def _exchange_copy(ins, lands, send_sems, recv_sems, scatter, a, j, px, py, c, dst_slot):
    src = ins[a].at[2 * px + py] if scatter else ins[a]
    return pltpu.make_async_remote_copy(src_ref=src, dst_ref=lands[a].at[dst_slot],
                                        send_sem=send_sems.at[3 * a + j], recv_sem=recv_sems.at[3 * a + j],
                                        device_id=(px, py, c), device_id_type=MESH)


def _own_copy(ins, lands, local_sems, scatter, a, me):
    return pltpu.make_async_copy(ins[a].at[me] if scatter else ins[a], lands[a].at[me], local_sems.at[a])


def exchange_start(bufs, scatter, name):
    n = len(bufs)
    lands = [lax.empty((NCHIP,) + tuple(b.shape[1:] if scatter else b.shape), b.dtype) for b in bufs]

    def body(*refs):
        ins, lnd, send_sems, recv_sems, local_sems, token = (refs[:n], refs[n:2 * n], refs[2 * n], refs[2 * n + 1],
                                                             refs[2 * n + 2], refs[-1])
        x, y, c = _mesh_pos()
        me = 2 * x + y
        for a in range(n):
            _own_copy(ins, lnd, local_sems, scatter, a, me).start()
            for j, (px, py) in enumerate([(1 - x, y), (x, 1 - y), (1 - x, 1 - y)]):
                _exchange_copy(ins, lnd, send_sems, recv_sems, scatter, a, j, px, py, c, me).start()
        token[...] = jnp.zeros_like(token)

    thru = [pltpu.HBM(b.shape, b.dtype) for b in list(bufs) + lands]
    out = pl.pallas_call(
        body, name=name, in_specs=[HBM_SPEC] * (2 * n),
        out_specs=[SEM_SPEC] * 3 + [HBM_SPEC] * (2 * n) + [pl.BlockSpec(memory_space=pltpu.VMEM)],
        out_shape=[pltpu.SemaphoreType.DMA((3 * n,)), pltpu.SemaphoreType.DMA((3 * n,)), pltpu.SemaphoreType.DMA((n,))]
        + thru + [jax.ShapeDtypeStruct((SLAB, LANE), F32)],
        input_output_aliases={k: 3 + k for k in range(2 * n)},
        compiler_params=pltpu.CompilerParams(has_side_effects=DATAFLOW),
    )(*[pltpu.with_memory_space_constraint(b, pltpu.HBM) for b in list(bufs) + lands])
    return tuple(out[:3]), out[3:3 + n], out[3 + n:3 + 2 * n], out[-1][0, 0]


def exchange_wait(started, after, scatter, name):
    sems, bufs, lands, _ = started
    n = len(bufs)

    def body(*refs):
        ins, lnd, ssem, rsem, lsem = refs[:n], refs[n:2 * n], refs[2 * n], refs[2 * n + 1], refs[2 * n + 2]
        x, y, c = _mesh_pos()
        for a in range(n):
            _own_copy(ins, lnd, lsem, scatter, a, 2 * x + y).wait()
            for j, (px, py) in enumerate([(1 - x, y), (x, 1 - y), (1 - x, 1 - y)]):
                cp = _exchange_copy(ins, lnd, ssem, rsem, scatter, a, j, px, py, c, 2 * px + py)
                cp.wait_send()
                cp.wait_recv()

    thru = [pltpu.HBM(b.shape, b.dtype) for b in list(bufs) + list(lands)]
    out = pl.pallas_call(
        body, name=name, in_specs=[HBM_SPEC] * (2 * n) + [SEM_SPEC] * 3 + [ANY],
        out_specs=[HBM_SPEC] * (2 * n), out_shape=thru,
        input_output_aliases={k: k for k in range(2 * n)},
        compiler_params=pltpu.CompilerParams(has_side_effects=DATAFLOW),
    )(*bufs, *lands, *sems, after)
    return out[n:]


def sibling_swap(bufs, name):
    n = len(bufs)

    def body(*refs):
        ins, outs, send_sems, recv_sems = refs[:n], refs[n:2 * n], refs[2 * n], refs[2 * n + 1]
        x, y, c = _mesh_pos()
        cps = [pltpu.make_async_remote_copy(src_ref=ins[k], dst_ref=outs[k], send_sem=send_sems.at[k],
                                            recv_sem=recv_sems.at[k], device_id=(x, y, 1 - c), device_id_type=MESH)
               for k in range(n)]
        for cp in cps:
            cp.start()
        for cp in cps:
            cp.wait()

    return pl.pallas_call(
        body, name=name, in_specs=[ANY] * n, out_specs=[ANY] * n,
        out_shape=[jax.ShapeDtypeStruct(b.shape, b.dtype) for b in bufs],
        scratch_shapes=[pltpu.SemaphoreType.DMA((n,)), pltpu.SemaphoreType.DMA((n,))],
    )(*bufs)


def _swap_copy(ins, lands, send_sems, recv_sems, a):
    x, y, c = _mesh_pos()
    return pltpu.make_async_remote_copy(src_ref=ins[a], dst_ref=lands[a], send_sem=send_sems.at[a],
                                        recv_sem=recv_sems.at[a], device_id=(x, y, 1 - c), device_id_type=MESH)


def swap_start(bufs, name):
    n = len(bufs)
    lands = [lax.empty(b.shape, b.dtype) for b in bufs]

    def body(*refs):
        for a in range(n):
            _swap_copy(refs[:n], refs[n:2 * n], refs[2 * n], refs[2 * n + 1], a).start()
        refs[-1][...] = jnp.zeros_like(refs[-1])

    thru = [pltpu.HBM(b.shape, b.dtype) for b in list(bufs) + lands]
    out = pl.pallas_call(
        body, name=name, in_specs=[HBM_SPEC] * (2 * n),
        out_specs=[SEM_SPEC] * 2 + [HBM_SPEC] * (2 * n) + [pl.BlockSpec(memory_space=pltpu.VMEM)],
        out_shape=[pltpu.SemaphoreType.DMA((n,)), pltpu.SemaphoreType.DMA((n,))] + thru
        + [jax.ShapeDtypeStruct((SLAB, LANE), F32)],
        input_output_aliases={k: 2 + k for k in range(2 * n)},
        compiler_params=pltpu.CompilerParams(has_side_effects=DATAFLOW),
    )(*[pltpu.with_memory_space_constraint(b, pltpu.HBM) for b in list(bufs) + lands])
    return tuple(out[:2]), out[2:2 + n], out[2 + n:2 + 2 * n], out[-1][0, 0]


def swap_wait(started, after, name):
    sems, bufs, lands, _ = started
    n = len(bufs)

    def body(*refs):
        for a in range(n):
            _swap_copy(refs[:n], refs[n:2 * n], refs[2 * n], refs[2 * n + 1], a).wait()

    thru = [pltpu.HBM(b.shape, b.dtype) for b in list(bufs) + list(lands)]
    out = pl.pallas_call(
        body, name=name, in_specs=[HBM_SPEC] * (2 * n) + [SEM_SPEC] * 2 + [ANY],
        out_specs=[HBM_SPEC] * (2 * n), out_shape=thru,
        input_output_aliases={k: k for k in range(2 * n)},
        compiler_params=pltpu.CompilerParams(has_side_effects=DATAFLOW),
    )(*bufs, *lands, *sems, after)
    return out[n:]


EW_VMEM_BUDGET = 20 * 1024 * 1024


def _ew_rows(rows, w, bytes_per_elem):
    wpad = -(-w // LANE) * LANE
    for t in (1024, 512, 256, 128, 64, 32, 16, 8):
        if rows % t == 0 and 2 * t * wpad * bytes_per_elem <= EW_VMEM_BUDGET:
            return t
    return rows


def sum_slots(buf, *, name):
    _, rows, w = buf.shape
    tm = _ew_rows(rows, w, NCHIP * buf.dtype.itemsize + 4)

    def body(b_ref, o_ref):
        acc = b_ref[0].astype(F32)
        for kk in range(1, NCHIP):
            acc = acc + b_ref[kk].astype(F32)
        o_ref[...] = acc

    return pl.pallas_call(body, name=name, grid=(rows // tm,),
                          in_specs=[pl.BlockSpec((NCHIP, tm, w), lambda i: (0, i, 0))],
                          out_specs=pl.BlockSpec((tm, w), lambda i: (i, 0)),
                          out_shape=jax.ShapeDtypeStruct((rows, w), F32),
                          compiler_params=_cparams("parallel"))(buf)


def adamw(p_mine, p_other, w, m, v, *, name):
    rows, wd = w.shape
    tm = _ew_rows(rows, wd, 9 * 4)
    c1 = 1.0 - ADAM_B1 ** ADAM_STEP
    c2 = 1.0 - ADAM_B2 ** ADAM_STEP

    def body(a_ref, b_ref, w_ref, m_ref, v_ref, g_ref, d_ref, nm_ref, nv_ref):
        g = a_ref[...] + b_ref[...]
        nm = ADAM_B1 * m_ref[...] + (1.0 - ADAM_B1) * g
        nv = ADAM_B2 * v_ref[...] + (1.0 - ADAM_B2) * (g * g)
        g_ref[...] = g
        nm_ref[...] = nm
        nv_ref[...] = nv
        d_ref[...] = -ADAM_LR * ((nm / c1) / (jnp.sqrt(nv / c2) + ADAM_EPS) + ADAM_WD * w_ref[...])

    row = pl.BlockSpec((tm, wd), lambda i: (i, 0))
    return pl.pallas_call(body, name=name, grid=(rows // tm,), in_specs=[row] * 5, out_specs=[row] * 4,
                          out_shape=[jax.ShapeDtypeStruct((rows, wd), F32)] * 4,
                          compiler_params=_cparams("parallel"))(p_mine, p_other, w, m, v)


def _rows2d(a, lead=0):
    tail = a.shape[lead:]
    n = int(np.prod(tail))
    if tail[-1] < LANE // 2 and n % (8 * LANE) == 0:
        return a.reshape(a.shape[:lead] + (n // (8 * LANE), 8 * LANE))
    return a.reshape(a.shape[:lead] + (-1, tail[-1]))


BIG = [("mix_w_in", 2), ("w_uq", 2), ("w_ukv", 2), ("mix_w_out", 1), ("ssm_w_in", 1), ("w_glu", 2),
       ("ffn_w_up", 2), ("ffn_w_down", 1)]
SMALL = [("sconv_w", 2), ("ssm_norm", 1), ("d_skip", 1), ("ffn_conv_w", 2)]
REPL = ["attn_norm", "cq_norm", "ckv_norm", "q_gain", "k_gain", "lambda_re", "lambda_im", "log_step",
        "b_re", "b_im", "c_re", "c_im", "ffn_norm"]
ORDER = ["attn_norm", "mix_w_in", "cq_norm", "ckv_norm", "w_uq", "w_ukv", "q_gain", "k_gain", "sconv_w", "mix_w_out",
         "ssm_norm", "ssm_w_in", "lambda_re", "lambda_im", "log_step", "b_re", "b_im", "c_re", "c_im", "d_skip",
         "w_glu", "ffn_norm", "ffn_w_up", "ffn_conv_w", "ffn_w_down"]


def _join(g, axis):
    if axis == 0:
        return g.reshape((-1,) + g.shape[2:])
    return jnp.concatenate([g[k] for k in range(NCHIP)], axis=axis)


def _split(full, axis, parts=NCHIP):
    if axis == 0:
        return full.reshape((parts, -1) + full.shape[1:])
    return jnp.stack(jnp.split(full, parts, axis=axis))


def _discretize(lr, li, ls, b_re, b_im):
    dt = jnp.exp(ls)[:, None]
    mag = jnp.exp(lr * dt)
    ar, ai = mag * jnp.cos(li * dt), mag * jnp.sin(li * dt)
    nr, ni = ar - 1.0, ai
    den = lr * lr + li * li
    zr, zi = (nr * lr + ni * li) / den, (ni * lr - nr * li) / den
    bbar_r = zr[..., None] * b_re - zi[..., None] * b_im
    bbar_i = zr[..., None] * b_im + zi[..., None] * b_re
    return ar, ai, bbar_r, bbar_i


def _b_blockdiag(bbar):
    gl = G // NJ
    bb = bbar.reshape(NJ, gl, P, GC).transpose(0, 1, 3, 2)
    return jnp.einsum("jgcp,gh->jgchp", bb, jnp.eye(gl, dtype=bbar.dtype)).reshape(NJ, gl * GC, gl * P)


def _b_blockdiag_t(dbd):
    gl = G // NJ
    d = jnp.einsum("jgchp,gh->jgcp", dbd.reshape(NJ, gl, GC, gl, P), jnp.eye(gl, dtype=dbd.dtype))
    return d.transpose(0, 1, 3, 2).reshape(G, P, GC)


def _c_blockdiag(cmat):
    gl = G // NJ
    cc = cmat.reshape(NJ, gl, GC, P).transpose(0, 1, 3, 2)
    return jnp.einsum("jgpc,gh->jgphc", cc, jnp.eye(gl, dtype=cmat.dtype)).reshape(NJ, gl * P, gl * GC)


def _c_blockdiag_t(dbd):
    gl = G // NJ
    d = jnp.einsum("jgphc,gh->jgpc", dbd.reshape(NJ, gl, P, gl, GC), jnp.eye(gl, dtype=dbd.dtype))
    return d.transpose(0, 1, 3, 2).reshape(G, GC, P)


def _pad_heads_cols(w, width):
    r = w.shape[0]
    return jnp.pad(w.reshape(r, HEADS, width), ((0, 0), (0, 0), (0, HP - width))).reshape(r, HEADS * HP)


def _unpad_heads_cols(w, width):
    r = w.shape[0]
    return w.reshape(r, HEADS, HP)[:, :, :width].reshape(r, HEADS * width)


W_IN_SPLIT = (QR + KVR, QR + KVR + ROPE)


def _w_in_layout(w):
    a, b = W_IN_SPLIT
    kr = jnp.pad(w[:, a:b], ((0, 0), (NOPE, HP - QK)))
    return jnp.concatenate([w[:, :a], w[:, b:], kr], axis=1)


def _ffn_fwd(x, l, wt, name):
    h = rms_fwd(x, wt["ffn_norm"][l][None], name=f"{name}_norm")
    upg, upv, act = ffn_up_act(h, wt["ffn_w_up"][l], wt["ffn_conv_w"][l], name=f"{name}_upact")
    up = (upg, upv)
    out = mm(act, wt["ffn_w_down"][l], add=x, name=f"{name}_down")
    return out, (x, h, up, act)


def _ffn_bwd(dout, douth, saved, l, wt, name):
    x, h, up, act = saved
    g = {}
    g["ffn_w_down"] = mm(act, douth, ta=True, out_dtype=BF16, name=f"{name}_dwdown")
    dup, g["ffn_conv_w"] = ffnact_bwd(up, wt["ffn_conv_w"][l], douth, wt["ffn_w_down"][l], name=f"{name}_dact")
    g["ffn_w_up"] = tuple(mm(h, d, ta=True, out_dtype=BF16, name=f"{name}_dwup{kk}") for kk, d in enumerate(dup))
    dh = mm_nt_segments(dup, wt["ffn_w_up"][l], name=f"{name}_dup")
    dx, dxh, dg = rms_bwd(x, wt["ffn_norm"][l][None], dh, add=dout, twin=True, name=f"{name}_dnorm")
    g["ffn_norm"] = dg[0]
    return dx, dxh, g


def _even_fwd(x, i, wt, tabs, name):
    h = rms_fwd(x, wt["attn_norm"][i][None], name=f"{name}_norm")
    proj = mm(h, wt["w_in2"][i], name=f"{name}_in")
    cqn = rms_fwd(proj, wt["cq_norm"][i][None], col=0, name=f"{name}_cqnorm")
    ckvn = rms_fwd(proj, wt["ckv_norm"][i][None], col=1, name=f"{name}_ckvnorm")
    qraw = mm(cqn, wt["w_uq_p"][i], name=f"{name}_uq")
    kv = mm(ckvn, wt["w_ukv_p"][i], name=f"{name}_ukv")
    q, k, v = qkprep_fwd(qraw, kv, proj, wt["q_gain_p"][i], wt["k_gain_p"][i], tabs, name=f"{name}_qkprep")
    o, oh, lset = attn_fwd(q, k, v, name=f"{name}_attn")
    conv = sconv_fwd(proj, wt["sconv_w"][i], name=f"{name}_sconv")
    t = mm(oh, wt["w_out_a"][i], add=x, name=f"{name}_outa")
    out = mm(conv, wt["w_out_c"][i], add=t, name=f"{name}_outc")
    return out, (x, h, proj, cqn, ckvn, qraw, kv, q, k, v, o, oh, lset, conv)


def _even_bwd(dout, douth, saved, i, wt, tabs, name):
    x, h, proj, cqn, ckvn, qraw, kv, q, k, v, o, oh, lset, conv = saved
    g = {}
    do = mm(douth, wt["w_out_a"][i], tb=True, name=f"{name}_douta")
    dconv = mm(douth, wt["w_out_c"][i], tb=True, name=f"{name}_doutc")
    g["w_out_a"] = mm(oh, douth, ta=True, out_dtype=BF16, name=f"{name}_dwouta")
    g["w_out_c"] = mm(conv, douth, ta=True, out_dtype=BF16, name=f"{name}_dwoutc")
    dgates, g["sconv_w"] = sconv_bwd(proj, wt["sconv_w"][i], dconv, name=f"{name}_dsconv")
    doh, deltat = attn_delta(o, do, name=f"{name}_dattn_delta")
    dq, dk, dv = attn_bwd(q, k, v, doh, lset, deltat, name=f"{name}_dattn")
    dqraw, dkraw, dkrope, dqg, dkg = qkprep_bwd(qraw, kv, proj, wt["q_gain_p"][i], wt["k_gain_p"][i], tabs, dq, dk,
                                                name=f"{name}_dqkprep")
    g["q_gain"], g["k_gain"] = dqg[0, :QK], dkg[0, :QK]
    dcqn = mm(dqraw, wt["w_uq_p"][i], tb=True, name=f"{name}_duq")
    g["w_uq_p"] = mm(cqn, dqraw, ta=True, out_dtype=BF16, name=f"{name}_dwuq")
    dkv = (dkraw, dv)
    dckvn = mm_nt_segments(dkv, wt["w_ukv_p"][i], name=f"{name}_dukv")
    g["w_ukv_p"] = tuple(mm(ckvn, d, ta=True, out_dtype=BF16, name=f"{name}_dwukv{kk}") for kk, d in enumerate(dkv))
    dcq, dgq = rms_bwd(proj, wt["cq_norm"][i][None], dcqn, col=0, out_dtype=BF16, name=f"{name}_dcqnorm")
    dckv, dgkv = rms_bwd(proj, wt["ckv_norm"][i][None], dckvn, col=1, out_dtype=BF16, name=f"{name}_dckvnorm")
    g["cq_norm"], g["ckv_norm"] = dgq[0], dgkv[0]
    dproj = (dcq, dckv, *dgates, dkrope)
    g["w_in2"] = tuple(mm(h, d, ta=True, out_dtype=BF16, name=f"{name}_dwin{kk}") for kk, d in enumerate(dproj))
    dh = mm_nt_segments(dproj, wt["w_in2"][i], name=f"{name}_din")
    dx, dxh, dg = rms_bwd(x, wt["attn_norm"][i][None], dh, add=dout, twin=True, name=f"{name}_dnorm")
    g["attn_norm"] = dg[0]
    return dx, dxh, g


def _odd_fwd(x, i, wt, name):
    h = rms_fwd(x, wt["ssm_norm"][i][None], name=f"{name}_norm")
    u = mm(h, wt["ssm_w_in"][i], name=f"{name}_in")
    y, st_r, st_i = s5_fwd(u, wt["bbd_r"][i], wt["bbd_i"][i], wt["cbd_r"][i], wt["cbd_i"][i], wt["tab_f"][i],
                           name=f"{name}_scan")
    gl = s5post_fwd(y, u, wt["d_skip"][i][None], name=f"{name}_gelu")
    glu = mm(gl, wt["w_glu"][i], name=f"{name}_glu")
    out = glu_fwd(glu, x, name=f"{name}_gate")
    return out, (x, h, u, y, st_r, st_i, gl, glu)


def _odd_bwd(dout, douth, saved, i, wt, name):
    x, h, u, y, st_r, st_i, gl, glu = saved
    g = {}
    dglu = glu_bwd(glu, dout, name=f"{name}_dgate")
    g["w_glu"] = mm(gl, dglu, ta=True, out_dtype=BF16, name=f"{name}_dwglu")
    dgl = mm(dglu, wt["w_glu"][i], tb=True, name=f"{name}_dglu")
    dz, dd = s5post_bwd(y, u, wt["d_skip"][i][None], dgl, name=f"{name}_dgelu")
    g["d_skip"] = dd[0]
    du, g["bbd_r"], g["bbd_i"], g["cbd_r"], g["cbd_i"], g["a"] = s5_bwd(
        u, dz, wt["d_skip"][i][None], st_r, st_i, wt["bbd_r"][i], wt["bbd_i"][i], wt["cbd_r"][i], wt["cbd_i"][i],
        wt["tab_r"][i], name=f"{name}_dscan")
    g["ssm_w_in"] = mm(h, du, ta=True, out_dtype=BF16, name=f"{name}_dwin")
    dh = mm(du, wt["ssm_w_in"][i], tb=True, name=f"{name}_din")
    dx, dxh, dg = rms_bwd(x, wt["ssm_norm"][i][None], dh, add=dout, twin=True, name=f"{name}_dnorm")
    g["ssm_norm"] = dg[0]
    return dx, dxh, g


MATMUL_WEIGHTS = {"even": ("mix_w_in", "w_uq", "w_ukv", "mix_w_out"), "odd": ("ssm_w_in", "w_glu"),
                  "ffn": ("ffn_w_up", "ffn_w_down")}
ODD_SMALL = ("lambda_re", "lambda_im", "log_step", "b_re", "b_im", "c_re", "c_im", "ssm_norm", "d_skip")


def _even_layouts(fw, wt, i):
    wt["w_in2"][i] = _w_in_layout(fw["mix_w_in"])
    wt["w_uq_p"][i] = _pad_heads_cols(fw["w_uq"], QK)
    ukv = fw["w_ukv"].reshape(KVR, HEADS, NOPE + VD)
    wt["w_ukv_p"][i] = jnp.concatenate(
        [_pad_heads_cols(ukv[:, :, :NOPE].reshape(KVR, HEADS * NOPE), NOPE),
         _pad_heads_cols(ukv[:, :, NOPE:].reshape(KVR, HEADS * VD), VD)], axis=1)
    wt["w_out_a"][i] = _pad_heads_cols(fw["mix_w_out"][:HEADS * VD].T, VD).T
    wt["w_out_c"][i] = fw["mix_w_out"][HEADS * VD:]


def _even_layouts_t(g):
    dk_, dv_ = g["w_ukv_p"]
    dcq, dckv, dgb, dgc, dci, dkr = g["w_in2"]
    return {"mix_w_in": jnp.concatenate([dcq, dckv, dkr[:, NOPE:QK], dgb, dgc, dci], axis=1),
            "w_uq": _unpad_heads_cols(g["w_uq_p"], QK),
            "w_ukv": jnp.concatenate([dk_.reshape(KVR, HEADS, HP)[:, :, :NOPE], dv_.reshape(KVR, HEADS, HP)[:, :, :VD]],
                                     axis=2).reshape(KVR, HEADS * (NOPE + VD)),
            "mix_w_out": jnp.concatenate([_unpad_heads_cols(g["w_out_a"].T, VD).T, g["w_out_c"]], axis=0)}


def _local_step(x, target, full, getw, putg):
    s = x.shape[0]
    n_even = (DEPTH + 1) // 2
    n_odd = DEPTH // 2
    tabs = _rope_tables(s)
    wt = dict(full)
    for key in ("w_in2", "w_uq_p", "w_ukv_p", "w_out_a", "w_out_c") + sum(MATMUL_WEIGHTS.values(), ()):
        wt[key] = {}
    wt["q_gain_p"] = jnp.pad(full["q_gain"], ((0, 0), (0, HP - QK)))[:, None, :]
    wt["k_gain_p"] = jnp.pad(full["k_gain"], ((0, 0), (0, HP - QK)))[:, None, :]

    disc_vjp = []
    for key in ("bbd_r", "bbd_i", "cbd_r", "cbd_i", "tab_f", "tab_r"):
        wt[key] = []
    for i in range(n_odd):
        (ar, ai, bbr, bbi), vjp = jax.vjp(_discretize, full["lambda_re"][i], full["lambda_im"][i], full["log_step"][i],
                                          full["b_re"][i], full["b_im"][i])
        disc_vjp.append(vjp)
        tf, tr = _scan_tables(ar.reshape(-1), ai.reshape(-1))
        wt["tab_f"].append(tf)
        wt["tab_r"].append(tr)
        wt["bbd_r"].append(_b_blockdiag(bbr).astype(BF16))
        wt["bbd_i"].append(_b_blockdiag(bbi).astype(BF16))
        wt["cbd_r"].append(_c_blockdiag(full["c_re"][i]).astype(BF16))
        wt["cbd_i"].append(_c_blockdiag(full["c_im"][i]).astype(BF16))

    saved = []
    for layer in range(DEPTH):
        i = layer // 2
        if layer % 2 == 0:
            fw, tok = getw("even", i, x)
            wt["attn_norm"] = full["attn_norm"] + tok
            _even_layouts(fw, wt, i)
            x, sm = _even_fwd(x, i, wt, tabs, f"l{layer}_mla")
        else:
            fw, tok = getw("odd", i, x)
            wt["ssm_norm"] = full["ssm_norm"] + tok
            for n, a in fw.items():
                wt[n][i] = a
            x, sm = _odd_fwd(x, i, wt, f"l{layer}_s5")
        fw, tok = getw("ffn", layer, x)
        wt["ffn_norm"] = full["ffn_norm"] + tok
        for n, a in fw.items():
            wt[n][layer] = a
        x, sf = _ffn_fwd(x, layer, wt, f"l{layer}_ffn")
        saved.append((sm, sf))
    dx, dxh, lslab = loss_head(x, target, name="loss_head")

    own = [n for n in ORDER if n not in sum(MATMUL_WEIGHTS.values(), ())]
    grads = {n: [None] * (DEPTH if n.startswith("ffn") else n_even) for n in own}
    tok = 0.0
    for layer in reversed(range(DEPTH)):
        i = layer // 2
        sm, sf = saved[layer]
        wt["ffn_conv_w"] = full["ffn_conv_w"] + tok
        dx, dxh, g = _ffn_bwd(dx, dxh, sf, layer, wt, f"l{layer}_ffn")
        tok = putg("ffn", layer, {n: g[n] for n in MATMUL_WEIGHTS["ffn"]})
        for n in ("ffn_norm", "ffn_conv_w"):
            grads[n][layer] = g[n]
        if layer % 2 == 0:
            wt["sconv_w"] = full["sconv_w"] + tok
            dx, dxh, g = _even_bwd(dx, dxh, sm, i, wt, tabs, f"l{layer}_mla")
            tok = putg("even", i, _even_layouts_t(g))
            for n in ("attn_norm", "cq_norm", "ckv_norm", "q_gain", "k_gain", "sconv_w"):
                grads[n][i] = g[n]
        else:
            wt["d_skip"] = full["d_skip"] + tok
            dx, dxh, g = _odd_bwd(dx, dxh, sm, i, wt, f"l{layer}_s5")
            tok = putg("odd", i, {n: g[n] for n in MATMUL_WEIGHTS["odd"]})
            dlr, dli, dls, dbr, dbi = disc_vjp[i]((g["a"][0].reshape(G, P), g["a"][1].reshape(G, P),
                                                    _b_blockdiag_t(g["bbd_r"]), _b_blockdiag_t(g["bbd_i"])))
            grads["lambda_re"][i], grads["lambda_im"][i], grads["log_step"][i] = dlr, dli, dls
            grads["b_re"][i], grads["b_im"][i] = dbr, dbi
            grads["c_re"][i], grads["c_im"][i] = _c_blockdiag_t(g["cbd_r"]), _c_blockdiag_t(g["cbd_i"])
            for n in ("ssm_norm", "d_skip"):
                grads[n][i] = g[n]
            if i == 0:
                tok = tok + putg("odd_small", 0, {n: jnp.stack(grads[n]) for n in ODD_SMALL})
    grads = {n: jnp.stack(v) for n, v in grads.items()}
    return jnp.sum(lslab), dx, grads


def kernel(x, attn_norm, mix_w_in, cq_norm, ckv_norm, w_uq, w_ukv, q_gain, k_gain, sconv_w, mix_w_out, ssm_norm, ssm_w_in, lambda_re, lambda_im, log_step, b_re, b_im, c_re, c_im, d_skip, w_glu, ffn_norm, ffn_w_up, ffn_conv_w, ffn_w_down, loss_target, m_attn_norm, m_mix_w_in, m_cq_norm, m_ckv_norm, m_w_uq, m_w_ukv, m_q_gain, m_k_gain, m_sconv_w, m_mix_w_out, m_ssm_norm, m_ssm_w_in, m_lambda_re, m_lambda_im, m_log_step, m_b_re, m_b_im, m_c_re, m_c_im, m_d_skip, m_w_glu, m_ffn_norm, m_ffn_w_up, m_ffn_conv_w, m_ffn_w_down, v_attn_norm, v_mix_w_in, v_cq_norm, v_ckv_norm, v_w_uq, v_w_ukv, v_q_gain, v_k_gain, v_sconv_w, v_mix_w_out, v_ssm_norm, v_ssm_w_in, v_lambda_re, v_lambda_im, v_log_step, v_b_re, v_b_im, v_c_re, v_c_im, v_d_skip, v_w_glu, v_ffn_norm, v_ffn_w_up, v_ffn_conv_w, v_ffn_w_down):
    args = dict(locals())
    w = {n: args[n] for n in ORDER}
    m = {n: args["m_" + n] for n in ORDER}
    v = {n: args["v_" + n] for n in ORDER}
    me = 2 * lax.axis_index("x") + lax.axis_index("y")

    axis = dict(BIG)

    gs = _chip_exchange([w[n] for n, _ in SMALL], False, "gather_w_f32")
    full = {n: w[n] for n in REPL}
    for (n, ax), g in zip(SMALL, gs):
        full[n] = _join(g, ax)
    parts = [(("even", "odd")[layer % 2], layer // 2) for layer in range(DEPTH)]
    parts = [p for layer, mixer in enumerate(parts) for p in (mixer, ("ffn", layer))]
    gathers = {}

    def start_gather(kind, idx, zero):
        shards = [(w[n][idx] + zero).astype(BF16) for n in MATMUL_WEIGHTS[kind]]
        gathers[kind, idx] = exchange_start(shards, False, f"gather_start_{kind}{idx}")

    start_gather(*parts[0], 0.0 * gs[0][(0,) * gs[0].ndim])

    def getw(kind, idx, after):
        got = exchange_wait(gathers[kind, idx], after, False, f"gather_wait_{kind}{idx}")
        nxt = parts.index((kind, idx)) + 1
        tok = 0.0
        if nxt < len(parts):
            start_gather(*parts[nxt], 0.0 * got[0][(0,) * got[0].ndim].astype(F32))
            tok = gathers[parts[nxt]][3]
        return {n: _join(g, axis[n] - 1) for n, g in zip(MATMUL_WEIGHTS[kind], got)}, tok

    scatters, early = [], []

    def putg(kind, idx, g):
        if kind == "odd_small":
            arrs = [_rows2d(g[n]) for n in ODD_SMALL]
            early.append(exchange_start(arrs, False, "gather_start_g_odd"))
            return early[0][3]
        blocks = [jnp.concatenate([_split(part, axis[n] - 1, NCHIP // len(g[n])) for part in g[n]])
                  if isinstance(g[n], tuple) else _split(g[n], axis[n] - 1) for n in MATMUL_WEIGHTS[kind]]
        tok = 0.0
        if scatters:
            tok = reduce_part(*scatters.pop(), blocks[0])
        scatters.append((kind, idx, exchange_start(blocks, True, f"scatter_start_{kind}{idx}")))
        return tok + scatters[-1][2][3]

    swaps = []

    def reduce_part(kind, idx, started, after):
        got = exchange_wait(started, after, True, f"scatter_wait_{kind}{idx}")
        sums = [sum_slots(_rows2d(sl, 1), name=f"sum_{n}{idx}") for n, sl in zip(MATMUL_WEIGHTS[kind], got)]
        swaps.append((kind, idx, sums, swap_start(sums, f"swap_start_{kind}{idx}")))
        return swaps[-1][3][3]

    sq, dx, grads = _local_step(x[0], loss_target[0], full, getw, putg)
    loss = lax.psum(0.5 * sq / D_MODEL, ("x", "y", "c"))

    late_names = [n for n in REPL + [n for n, _ in SMALL] if n not in ODD_SMALL]
    rep_names = list(ODD_SMALL) + late_names
    names = [n for n, _ in BIG] + rep_names
    late = [_rows2d(grads[n]) for n in late_names]
    late_started = exchange_start(late, False, "gather_start_g_f32")
    reduce_part(*scatters.pop(), dx)
    slots = (exchange_wait(early[0], dx, False, "gather_wait_g_odd")
             + exchange_wait(late_started, swaps[-1][2][0], False, "gather_wait_g_f32"))
    small_mine = [sum_slots(_rows2d(sl, 1), name=f"sum_{n}") for n, sl in zip(rep_names, slots)]
    small_other = list(sibling_swap(small_mine, "swap_g_small"))
    summed, swapped = {}, {}
    for kind, idx, sums, started in swaps:
        for n, p, q in zip(MATMUL_WEIGHTS[kind], sums, swap_wait(started, small_other[0], f"swap_wait_{kind}{idx}")):
            summed[n, idx], swapped[n, idx] = p, q
    mine = [jnp.concatenate([summed[n, idx] for idx in range(w[n].shape[0])], axis=0) for n, _ in BIG] + small_mine
    other = [jnp.concatenate([swapped[n, idx] for idx in range(w[n].shape[0])], axis=0) for n, _ in BIG] + small_other

    def local(n, p):
        ax = dict(SMALL).get(n)
        if ax is None:
            return p
        part = lax.dynamic_index_in_dim(_split(p.reshape(grads[n].shape), ax), me, 0, keepdims=False)
        return _rows2d(part)

    outs = {}
    for n, p, q in zip(names, mine, other):
        res = adamw(local(n, p), local(n, q), _rows2d(w[n]), _rows2d(m[n]), _rows2d(v[n]), name=f"adamw_{n}")
        outs[n] = [r.reshape(w[n].shape) for r in res]
    return (loss, dx[None], *[outs[n][0] for n in ORDER], *[outs[n][1] for n in ORDER],
            *[outs[n][2] for n in ORDER], *[outs[n][3] for n in ORDER])
```
